```python
import jax, jax.numpy as jnp
from jax import lax
import numpy as np

D_MODEL = 1024
BATCH = 8
SEQ = 4096
DEPTH = 4

CHUNK = 64
N_LEFT_CHUNKS = 8
BAND_CHUNKS = N_LEFT_CHUNKS + 1
BAND = BAND_CHUNKS * CHUNK
LEFT_PAD = N_LEFT_CHUNKS * CHUNK
MIX_WIDTH = D_MODEL
CONV_WIDTH = MIX_WIDTH // 2
ATTN_WIDTH = MIX_WIDTH - CONV_WIDTH
HEAD_DIM = 64
N_HEADS = ATTN_WIDTH // HEAD_DIM
CONV_GROUPS = 8
CONV_K = 3
REL_CLIP = 128
D_FF = -(-8 * D_MODEL // (3 * 256)) * 256
PROJ_WIDTH = 3 * CONV_WIDTH + 3 * ATTN_WIDTH
EPS = 1e-6
NEG_INF = -1e30

kernel_name = "hybrid_shortconv_chunkattn_sandwich_trunk"


def rms_norm(x, g):
    xf = x.astype(jnp.float32)
    y = xf * lax.rsqrt(jnp.mean(xf * xf, axis=-1, keepdims=True) + EPS)
    return (y * g.astype(jnp.float32)).astype(x.dtype)


def group_rms_norm(y, g, n_groups):
    b, s, w = y.shape
    yf = y.astype(jnp.float32).reshape(b, s, n_groups, w // n_groups)
    yf = yf * lax.rsqrt(jnp.mean(yf * yf, axis=-1, keepdims=True) + EPS)
    return (yf.reshape(b, s, w) * g.astype(jnp.float32)).astype(y.dtype)


def short_gated_conv(h, b_gate, c_gate, w_conv):
    s = h.shape[1]
    u = c_gate * h
    up = jnp.pad(u, ((0, 0), (CONV_K - 1, 0), (0, 0)))
    out = up[:, 0:s] * w_conv[:, 0]
    for k in range(1, CONV_K):
        out = out + up[:, k:k + s] * w_conv[:, k]
    return b_gate * out


def chunked_band_attention(q, k, v, rel_bias):
    b, s, _ = q.shape
    nc = s // CHUNK
    qc = q.reshape(b, nc, CHUNK, N_HEADS, HEAD_DIM) * (HEAD_DIM ** -0.5)
    kp = jnp.pad(k, ((0, 0), (LEFT_PAD, 0), (0, 0))).reshape(b, nc + N_LEFT_CHUNKS, CHUNK, N_HEADS, HEAD_DIM)
    vp = jnp.pad(v, ((0, 0), (LEFT_PAD, 0), (0, 0))).reshape(b, nc + N_LEFT_CHUNKS, CHUNK, N_HEADS, HEAD_DIM)
    kb = jnp.concatenate([kp[:, o:o + nc] for o in range(BAND_CHUNKS)], axis=2)
    vb = jnp.concatenate([vp[:, o:o + nc] for o in range(BAND_CHUNKS)], axis=2)
    scores = jnp.einsum('bcqhd,bckhd->bhcqk', qc, kb).astype(jnp.float32)
    qi = jnp.arange(CHUNK)[:, None]
    kj = jnp.arange(BAND)[None, :]
    rel_idx = jnp.clip(qi - kj + LEFT_PAD, -REL_CLIP, REL_CLIP) + REL_CLIP
    bias = rel_bias.astype(jnp.float32)[:, rel_idx]
    key_pos = jnp.arange(nc)[:, None] * CHUNK - LEFT_PAD + jnp.arange(BAND)[None, :]
    valid = key_pos >= 0
    scores = jnp.where(valid[None, None, :, None, :], scores + bias[None, :, None], NEG_INF)
    p = jax.nn.softmax(scores, axis=-1).astype(v.dtype)
    o = jnp.einsum('bhcqk,bckhd->bcqhd', p, vb)
    return o.reshape(b, s, ATTN_WIDTH)


def _fwd_setup_inputs(seed: int = 0) -> dict:
    key = jax.random.key(seed)
    ks = jax.random.split(key, 14)
    f32 = jnp.float32

    def gain(k, n):
        return 1.0 + 0.1 * jax.random.normal(k, (DEPTH, n), f32)

    return {
        "x": jax.random.normal(ks[0], (BATCH, SEQ, D_MODEL), f32),
        "w_in": jax.random.normal(ks[1], (DEPTH, D_MODEL, PROJ_WIDTH), f32) * D_MODEL ** -0.5,
        "w_conv": jax.random.normal(ks[2], (DEPTH, CONV_WIDTH, CONV_K), f32) * CONV_K ** -0.5,
        "rel_bias": 0.5 * jax.random.normal(ks[3], (DEPTH, N_HEADS, 2 * REL_CLIP + 1), f32),
        "g_conv_out": gain(ks[4], CONV_WIDTH),
        "g_attn_out": gain(ks[5], ATTN_WIDTH),
        "w_out": jax.random.normal(ks[6], (DEPTH, MIX_WIDTH, D_MODEL), f32) * MIX_WIDTH ** -0.5,
        "g_pre_mix": gain(ks[7], D_MODEL),
        "g_post_mix": gain(ks[8], D_MODEL),
        "g_pre_ffn": gain(ks[9], D_MODEL),
        "g_post_ffn": gain(ks[10], D_MODEL),
        "w_ffn_in": jax.random.normal(ks[11], (DEPTH, D_MODEL, 2 * D_FF), f32) * D_MODEL ** -0.5,
        "w_ffn_out": jax.random.normal(ks[12], (DEPTH, D_FF, D_MODEL), f32) * D_FF ** -0.5,
    }


def _fwd_reference(x, w_in, w_conv, rel_bias, g_conv_out, g_attn_out, w_out,
              g_pre_mix, g_post_mix, g_pre_ffn, g_post_ffn, w_ffn_in, w_ffn_out):
    for l in range(DEPTH):
        h = rms_norm(x, g_pre_mix[l])
        proj = jnp.einsum('bsd,dp->bsp', h, w_in[l])
        hc, bg, cg, q, k, v = jnp.split(proj, 6, axis=-1)
        y_conv = short_gated_conv(hc, bg, cg, w_conv[l])
        y_attn = chunked_band_attention(q, k, v, rel_bias[l])
        y = jnp.concatenate([group_rms_norm(y_conv, g_conv_out[l], CONV_GROUPS),
                             group_rms_norm(y_attn, g_attn_out[l], N_HEADS)], axis=-1)
        y = jnp.einsum('bsm,md->bsd', y, w_out[l])
        x = x + rms_norm(y, g_post_mix[l])
        h = rms_norm(x, g_pre_ffn[l])
        gate, up = jnp.split(jnp.einsum('bsd,df->bsf', h, w_ffn_in[l]), 2, axis=-1)
        f = jnp.einsum('bsf,fd->bsd', jax.nn.silu(gate) * up, w_ffn_out[l])
        x = x + rms_norm(f, g_post_ffn[l])
    return x


import jax as _jax
import jax.numpy as _jnp

TWIN_FORMAT = 'train_step'
FWD_PARAMS = ['x', 'w_in', 'w_conv', 'rel_bias', 'g_conv_out', 'g_attn_out', 'w_out', 'g_pre_mix', 'g_post_mix', 'g_pre_ffn', 'g_post_ffn', 'w_ffn_in', 'w_ffn_out']
TWIN_WEIGHTS = ['w_in', 'w_conv', 'rel_bias', 'g_conv_out', 'g_attn_out', 'w_out', 'g_pre_mix', 'g_post_mix', 'g_pre_ffn', 'g_post_ffn', 'w_ffn_in', 'w_ffn_out']
TWIN_DIFF_INPUT = 'x'
TWIN_INPUTS = ['x', 'w_in', 'w_conv', 'rel_bias', 'g_conv_out', 'g_attn_out', 'w_out', 'g_pre_mix', 'g_post_mix', 'g_pre_ffn', 'g_post_ffn', 'w_ffn_in', 'w_ffn_out', 'loss_target', 'm_w_in', 'm_w_conv', 'm_rel_bias', 'm_g_conv_out', 'm_g_attn_out', 'm_w_out', 'm_g_pre_mix', 'm_g_post_mix', 'm_g_pre_ffn', 'm_g_post_ffn', 'm_w_ffn_in', 'm_w_ffn_out', 'v_w_in', 'v_w_conv', 'v_rel_bias', 'v_g_conv_out', 'v_g_attn_out', 'v_w_out', 'v_g_pre_mix', 'v_g_post_mix', 'v_g_pre_ffn', 'v_g_post_ffn', 'v_w_ffn_in', 'v_w_ffn_out']
TWIN_OUTPUTS = ['loss', 'grad_x', 'grad_w_in', 'grad_w_conv', 'grad_rel_bias', 'grad_g_conv_out', 'grad_g_attn_out', 'grad_w_out', 'grad_g_pre_mix', 'grad_g_post_mix', 'grad_g_pre_ffn', 'grad_g_post_ffn', 'grad_w_ffn_in', 'grad_w_ffn_out', 'delta_w_in', 'delta_w_conv', 'delta_rel_bias', 'delta_g_conv_out', 'delta_g_attn_out', 'delta_w_out', 'delta_g_pre_mix', 'delta_g_post_mix', 'delta_g_pre_ffn', 'delta_g_post_ffn', 'delta_w_ffn_in', 'delta_w_ffn_out', 'new_m_w_in', 'new_m_w_conv', 'new_m_rel_bias', 'new_m_g_conv_out', 'new_m_g_attn_out', 'new_m_w_out', 'new_m_g_pre_mix', 'new_m_g_post_mix', 'new_m_g_pre_ffn', 'new_m_g_post_ffn', 'new_m_w_ffn_in', 'new_m_w_ffn_out', 'new_v_w_in', 'new_v_w_conv', 'new_v_rel_bias', 'new_v_g_conv_out', 'new_v_g_attn_out', 'new_v_w_out', 'new_v_g_pre_mix', 'new_v_g_post_mix', 'new_v_g_pre_ffn', 'new_v_g_post_ffn', 'new_v_w_ffn_in', 'new_v_w_ffn_out']
TWIN_LEAF_KINDS = {'loss': 'loss', 'grad_x': 'grad_x', 'grad_w_in': 'grad_w', 'grad_w_conv': 'grad_w', 'grad_rel_bias': 'grad_w', 'grad_g_conv_out': 'grad_w', 'grad_g_attn_out': 'grad_w', 'grad_w_out': 'grad_w', 'grad_g_pre_mix': 'grad_w', 'grad_g_post_mix': 'grad_w', 'grad_g_pre_ffn': 'grad_w', 'grad_g_post_ffn': 'grad_w', 'grad_w_ffn_in': 'grad_w', 'grad_w_ffn_out': 'grad_w', 'delta_w_in': 'delta_w', 'delta_w_conv': 'delta_w', 'delta_rel_bias': 'delta_w', 'delta_g_conv_out': 'delta_w', 'delta_g_attn_out': 'delta_w', 'delta_w_out': 'delta_w', 'delta_g_pre_mix': 'delta_w', 'delta_g_post_mix': 'delta_w', 'delta_g_pre_ffn': 'delta_w', 'delta_g_post_ffn': 'delta_w', 'delta_w_ffn_in': 'delta_w', 'delta_w_ffn_out': 'delta_w', 'new_m_w_in': 'new_m', 'new_m_w_conv': 'new_m', 'new_m_rel_bias': 'new_m', 'new_m_g_conv_out': 'new_m', 'new_m_g_attn_out': 'new_m', 'new_m_w_out': 'new_m', 'new_m_g_pre_mix': 'new_m', 'new_m_g_post_mix': 'new_m', 'new_m_g_pre_ffn': 'new_m', 'new_m_g_post_ffn': 'new_m', 'new_m_w_ffn_in': 'new_m', 'new_m_w_ffn_out': 'new_m', 'new_v_w_in': 'new_v', 'new_v_w_conv': 'new_v', 'new_v_rel_bias': 'new_v', 'new_v_g_conv_out': 'new_v', 'new_v_g_attn_out': 'new_v', 'new_v_w_out': 'new_v', 'new_v_g_pre_mix': 'new_v', 'new_v_g_post_mix': 'new_v', 'new_v_g_pre_ffn': 'new_v', 'new_v_g_post_ffn': 'new_v', 'new_v_w_ffn_in': 'new_v', 'new_v_w_ffn_out': 'new_v'}


def _forward(args):
    return _fwd_reference(*[args[k] for k in FWD_PARAMS])


def _output_shape():
    def fwd():
        inp = _fwd_setup_inputs(0)
        return _fwd_reference(*[inp[k] for k in FWD_PARAMS])
    out = _jax.eval_shape(fwd)
    return out.shape, out.dtype

N_MICROBATCH = 1
ADAM_LR = 0.001
ADAM_B1 = 0.9
ADAM_B2 = 0.999
ADAM_EPS = 1e-08
ADAM_WD = 0.01
ADAM_STEP = 10
PER_EXAMPLE_BATCH_AXIS = {'x': 0, 'loss_target': 0}
SHARED_INPUTS = []
_WEIGHT_DTYPES = {'w_in': _jnp.float32, 'w_conv': _jnp.float32, 'rel_bias': _jnp.float32, 'g_conv_out': _jnp.float32, 'g_attn_out': _jnp.float32, 'w_out': _jnp.float32, 'g_pre_mix': _jnp.float32, 'g_post_mix': _jnp.float32, 'g_pre_ffn': _jnp.float32, 'g_post_ffn': _jnp.float32, 'w_ffn_in': _jnp.float32, 'w_ffn_out': _jnp.float32}
MOMENT_SCALE = {'w_in': 5.292747e+00, 'w_conv': 1.661695e+00, 'rel_bias': 6.596081e-01, 'g_conv_out': 1.641150e+00, 'g_attn_out': 1.218943e+01, 'w_out': 9.242339e+00, 'g_pre_mix': 9.175584e+00, 'g_post_mix': 3.374900e+01, 'g_pre_ffn': 3.343074e+00, 'g_post_ffn': 3.191276e+01, 'w_ffn_in': 1.417408e+00, 'w_ffn_out': 2.755018e+00}


def _to_microbatches(a, axis):
    t = _jnp.moveaxis(a, axis, 0)
    t = t.reshape((N_MICROBATCH, t.shape[0] // N_MICROBATCH) + t.shape[1:])
    return _jnp.moveaxis(t, 1, axis + 1)


def setup_inputs(seed: int = 0) -> dict:
    inp = _fwd_setup_inputs(seed)
    key = _jax.random.fold_in(_jax.random.key(seed), 7919)
    shape, _ = _output_shape()
    out = dict(inp)
    out["loss_target"] = _jax.random.normal(_jax.random.fold_in(key, 0), shape, _jnp.float32)
    for i, name in enumerate(TWIN_WEIGHTS):
        w = inp[name].astype(_jnp.float32)
        if MOMENT_SCALE is None:
            s = _jnp.sqrt(_jnp.mean(_jnp.square(w)) + 1e-30)
        else:
            s = MOMENT_SCALE[name]
        km, kv = _jax.random.split(_jax.random.fold_in(key, i + 1))
        out[name] = w
        out["m_" + name] = s * _jax.random.normal(km, w.shape, _jnp.float32)
        out["v_" + name] = (s * s) * _jax.random.uniform(kv, w.shape, _jnp.float32, 0.5, 1.5)
    if N_MICROBATCH > 1:
        for name, axis in PER_EXAMPLE_BATCH_AXIS.items():
            out[name] = _to_microbatches(out[name], axis)
    return {'x': out['x'], 'w_in': out['w_in'], 'w_conv': out['w_conv'], 'rel_bias': out['rel_bias'], 'g_conv_out': out['g_conv_out'], 'g_attn_out': out['g_attn_out'], 'w_out': out['w_out'], 'g_pre_mix': out['g_pre_mix'], 'g_post_mix': out['g_post_mix'], 'g_pre_ffn': out['g_pre_ffn'], 'g_post_ffn': out['g_post_ffn'], 'w_ffn_in': out['w_ffn_in'], 'w_ffn_out': out['w_ffn_out'], 'loss_target': out['loss_target'], 'm_w_in': out['m_w_in'], 'm_w_conv': out['m_w_conv'], 'm_rel_bias': out['m_rel_bias'], 'm_g_conv_out': out['m_g_conv_out'], 'm_g_attn_out': out['m_g_attn_out'], 'm_w_out': out['m_w_out'], 'm_g_pre_mix': out['m_g_pre_mix'], 'm_g_post_mix': out['m_g_post_mix'], 'm_g_pre_ffn': out['m_g_pre_ffn'], 'm_g_post_ffn': out['m_g_post_ffn'], 'm_w_ffn_in': out['m_w_ffn_in'], 'm_w_ffn_out': out['m_w_ffn_out'], 'v_w_in': out['v_w_in'], 'v_w_conv': out['v_w_conv'], 'v_rel_bias': out['v_rel_bias'], 'v_g_conv_out': out['v_g_conv_out'], 'v_g_attn_out': out['v_g_attn_out'], 'v_w_out': out['v_w_out'], 'v_g_pre_mix': out['v_g_pre_mix'], 'v_g_post_mix': out['v_g_post_mix'], 'v_g_pre_ffn': out['v_g_pre_ffn'], 'v_g_post_ffn': out['v_g_post_ffn'], 'v_w_ffn_in': out['v_w_ffn_in'], 'v_w_ffn_out': out['v_w_ffn_out']}


def _loss(weights, diff, rest, loss_target):
    with _jax.named_scope("forward"):
        args = {**rest, TWIN_DIFF_INPUT: diff, **{k: w.astype(_WEIGHT_DTYPES[k]) for k, w in weights.items()}}
        y = _forward(args)
    with _jax.named_scope("loss_head"):
        err = _jnp.square(y.astype(_jnp.float32) - loss_target)
        return 0.5 * _jnp.sum(_jnp.mean(err, axis=-1)) if err.ndim else 0.5 * err


def _adamw(w, g, m, v):
    m = ADAM_B1 * m + (1.0 - ADAM_B1) * g
    v = ADAM_B2 * v + (1.0 - ADAM_B2) * _jnp.square(g)
    m_hat = m / (1.0 - ADAM_B1 ** ADAM_STEP)
    v_hat = v / (1.0 - ADAM_B2 ** ADAM_STEP)
    delta = -ADAM_LR * (m_hat / (_jnp.sqrt(v_hat) + ADAM_EPS) + ADAM_WD * w)
    return delta, m, v


def reference(x, w_in, w_conv, rel_bias, g_conv_out, g_attn_out, w_out, g_pre_mix, g_post_mix, g_pre_ffn, g_post_ffn, w_ffn_in, w_ffn_out, loss_target, m_w_in, m_w_conv, m_rel_bias, m_g_conv_out, m_g_attn_out, m_w_out, m_g_pre_mix, m_g_post_mix, m_g_pre_ffn, m_g_post_ffn, m_w_ffn_in, m_w_ffn_out, v_w_in, v_w_conv, v_rel_bias, v_g_conv_out, v_g_attn_out, v_w_out, v_g_pre_mix, v_g_post_mix, v_g_pre_ffn, v_g_post_ffn, v_w_ffn_in, v_w_ffn_out):
    given = dict(x=x, w_in=w_in, w_conv=w_conv, rel_bias=rel_bias, g_conv_out=g_conv_out, g_attn_out=g_attn_out, w_out=w_out, g_pre_mix=g_pre_mix, g_post_mix=g_post_mix, g_pre_ffn=g_pre_ffn, g_post_ffn=g_post_ffn, w_ffn_in=w_ffn_in, w_ffn_out=w_ffn_out, loss_target=loss_target, m_w_in=m_w_in, m_w_conv=m_w_conv, m_rel_bias=m_rel_bias, m_g_conv_out=m_g_conv_out, m_g_attn_out=m_g_attn_out, m_w_out=m_w_out, m_g_pre_mix=m_g_pre_mix, m_g_post_mix=m_g_post_mix, m_g_pre_ffn=m_g_pre_ffn, m_g_post_ffn=m_g_post_ffn, m_w_ffn_in=m_w_ffn_in, m_w_ffn_out=m_w_ffn_out, v_w_in=v_w_in, v_w_conv=v_w_conv, v_rel_bias=v_rel_bias, v_g_conv_out=v_g_conv_out, v_g_attn_out=v_g_attn_out, v_w_out=v_w_out, v_g_pre_mix=v_g_pre_mix, v_g_post_mix=v_g_post_mix, v_g_pre_ffn=v_g_pre_ffn, v_g_post_ffn=v_g_post_ffn, v_w_ffn_in=v_w_ffn_in, v_w_ffn_out=v_w_ffn_out)
    weights = {n: given[n] for n in TWIN_WEIGHTS}
    shared = {n: given[n] for n in SHARED_INPUTS}
    per_example = {n: given[n] for n in ['x']}
    grad_fn = _jax.value_and_grad(_loss, argnums=(0, 1))

    def one_microbatch(ex, loss_target):
        ex = dict(ex)
        diff = ex.pop(TWIN_DIFF_INPUT)
        return grad_fn(weights, diff, {**shared, **ex}, loss_target)

    if N_MICROBATCH == 1:
        loss, (grad_w, grad_x) = one_microbatch(per_example, given["loss_target"])
    else:
        def body(carry, xs):
            loss_sum, grad_sum = carry
            l_k, (gw_k, gx_k) = one_microbatch(xs[0], xs[1])
            with _jax.named_scope("update"):
                return (loss_sum + l_k, _jax.tree.map(_jnp.add, grad_sum, gw_k)), gx_k

        init = (_jnp.zeros((), _jnp.float32), _jax.tree.map(_jnp.zeros_like, weights))
        (loss, grad_w), grad_x = _jax.lax.scan(body, init, (per_example, given["loss_target"]))
    with _jax.named_scope("update"):
        delta_w, new_m, new_v = {}, {}, {}
        for n in TWIN_WEIGHTS:
            delta_w[n], new_m[n], new_v[n] = _adamw(weights[n], grad_w[n], given["m_" + n], given["v_" + n])
    return (loss, grad_x, *[grad_w[n] for n in TWIN_WEIGHTS], *[delta_w[n] for n in TWIN_WEIGHTS],
            *[new_m[n] for n in TWIN_WEIGHTS], *[new_v[n] for n in TWIN_WEIGHTS])
```

```python
import functools

import jax
import jax.numpy as jnp
from jax import lax
from jax.experimental import pallas as pl
from jax.experimental.pallas import tpu as pltpu

F32 = jnp.float32
BF16 = jnp.bfloat16

D_MODEL = 1024
N_DEV = 8
CHUNK = 64
N_LEFT_CHUNKS = 8
CONV_WIDTH = 512
ATTN_WIDTH = 512
HEAD_DIM = 64
N_HEADS = 8
REL_CLIP = 128
REL_PAD = 384
PROJ_WIDTH = 3072
PROJ_SHARD = PROJ_WIDTH // N_DEV
D_FF = 2816
FF_SHARD = 2 * D_FF // N_DEV
FFO_SHARD = D_FF // N_DEV
EPS = 1e-6
NEG_INF = -1e30
Q_BLOCK = 4 * CHUNK
K_BAND = Q_BLOCK + N_LEFT_CHUNKS * CHUNK
LEFT = N_LEFT_CHUNKS * CHUNK
TOEP = 1024

ADAM_LR = 0.001
ADAM_B1 = 0.9
ADAM_B2 = 0.999
ADAM_EPS = 1e-08
ADAM_WD = 0.01
ADAM_STEP = 10

VMEM_LIMIT = 52 * 1024 * 1024
MESH = pl.DeviceIdType.MESH
ANY = pl.BlockSpec(memory_space=pl.ANY)

NT = (((1,), (1,)), ((), ()))
TN = (((0,), (0,)), ((), ()))


def _cparams(n_axes=0):
    sem = ("arbitrary",) * n_axes if n_axes else None
    return pltpu.CompilerParams(dimension_semantics=sem, vmem_limit_bytes=VMEM_LIMIT)


def _dot(a, b):
    return jnp.dot(a, b, preferred_element_type=F32)


def _dot_nt(a, b):
    return lax.dot_general(a, b, NT, preferred_element_type=F32)


def _dot_tn(a, b):
    return lax.dot_general(a, b, TN, preferred_element_type=F32)


def _rstd(v):
    return lax.rsqrt(jnp.mean(v * v, axis=-1, keepdims=True) + EPS)


def _group_matrix():
    r = lax.broadcasted_iota(jnp.int32, (128, 128), 0) >> 6
    c = lax.broadcasted_iota(jnp.int32, (128, 128), 1) >> 6
    return jnp.where(r == c, 1.0, 0.0).astype(BF16)


def _group_mean(v, gmat):
    hi = v.astype(BF16)
    lo = (v - hi.astype(F32)).astype(BF16)
    return (_dot(hi, gmat) + _dot(lo, gmat)) * (1.0 / HEAD_DIM)


def _split3(v):
    hi = v.astype(BF16)
    r1 = v - hi.astype(F32)
    mid = r1.astype(BF16)
    lo = (r1 - mid.astype(F32)).astype(BF16)
    return hi, mid, lo


def _row_tile(rows, cands=(1024, 512, 704, 256, 128, 64, 32, 16)):
    for c in cands:
        if rows % c == 0:
            return c
    return rows


def _cast_bf16(x, name):
    shape = x.shape
    x2 = x.reshape(-1, shape[-1])
    rows, cols = x2.shape
    tr = _row_tile(rows)

    def body(x_ref, o_ref):
        o_ref[...] = x_ref[...].astype(BF16)

    out = pl.pallas_call(
        body, name=name, grid=(rows // tr,),
        in_specs=[pl.BlockSpec((tr, cols), lambda i: (i, 0))],
        out_specs=pl.BlockSpec((tr, cols), lambda i: (i, 0)),
        out_shape=jax.ShapeDtypeStruct((rows, cols), BF16),
        compiler_params=_cparams(1),
    )(x2)
    return out.reshape(shape)


def _norm_cast(x, g3, l, tm):
    s = x.shape[0]

    def body(x_ref, g_ref, o_ref):
        v = x_ref[...]
        o_ref[...] = (v * _rstd(v) * g_ref[...]).astype(BF16)

    return pl.pallas_call(
        body, name="norm_cast", grid=(s // tm,),
        in_specs=[pl.BlockSpec((tm, D_MODEL), lambda i: (i, 0)),
                  pl.BlockSpec((None, 1, D_MODEL), lambda i: (l, 0, 0))],
        out_specs=pl.BlockSpec((tm, D_MODEL), lambda i: (i, 0)),
        out_shape=jax.ShapeDtypeStruct((s, D_MODEL), BF16),
        compiler_params=_cparams(1),
    )(x, g3)


def _dev_index(px, py, pc):
    return 4 * px + 2 * py + pc


def _all_gather(arrays, layered):
    n = len(arrays)
    out_shapes = []
    for a, lay in zip(arrays, layered):
        shp = (a.shape[0], N_DEV) + a.shape[1:] if lay else (N_DEV,) + a.shape
        out_shapes.append(jax.ShapeDtypeStruct(shp, a.dtype))

    def body(*refs):
        srcs, dsts = refs[:n], refs[n:2 * n]
        send_sems, recv_sems, local_sems = refs[2 * n:]
        x, y, c = lax.axis_index("x"), lax.axis_index("y"), lax.axis_index("c")
        me = _dev_index(x, y, c)
        sibling = (x, y, 1 - c)
        chips = [(1 - x, y), (x, 1 - y), (1 - x, 1 - y)]

        def slot(k, idx):
            return dsts[k].at[:, idx] if layered[k] else dsts[k].at[idx]

        def copy(k, s, idx, to, src=None):
            return pltpu.make_async_remote_copy(
                src_ref=slot(k, idx) if src is None else src, dst_ref=slot(k, idx),
                send_sem=send_sems.at[k, s], recv_sem=recv_sems.at[k, s],
                device_id=to, device_id_type=MESH)

        mine = [pltpu.make_async_copy(srcs[k], slot(k, me), local_sems.at[k]) for k in range(n)]
        for cp in mine:
            cp.start()
        started = []
        for k in range(n):
            cp = copy(k, 0, me, sibling, src=srcs[k])
            cp.start()
            started.append(cp)
            for j, chip in enumerate(chips):
                cp = copy(k, 1 + j, me, (chip[0], chip[1], c), src=srcs[k])
                cp.start()
                started.append(cp)
        for j, chip in enumerate(chips):
            idx = _dev_index(chip[0], chip[1], c)
            for k in range(n):
                copy(k, 1 + j, idx, (x, y, c)).wait_recv()
                cp = copy(k, 4 + j, idx, sibling)
                cp.start()
                started.append(cp)
        for k in range(n):
            copy(k, 0, _dev_index(x, y, 1 - c), (x, y, c)).wait_recv()
        for j, chip in enumerate(chips):
            idx = _dev_index(chip[0], chip[1], 1 - c)
            for k in range(n):
                copy(k, 4 + j, idx, (x, y, c)).wait_recv()
        for cp in started:
            cp.wait_send()
        for cp in mine:
            cp.wait()

    return pl.pallas_call(
        body, name="all_gather_weights",
        in_specs=[ANY] * n, out_specs=[ANY] * n, out_shape=out_shapes,
        scratch_shapes=[pltpu.SemaphoreType.DMA((n, 7)), pltpu.SemaphoreType.DMA((n, 7)),
                        pltpu.SemaphoreType.DMA((n,))],
    )(*arrays)


_PEER_FLIPS = [(0, 0, 1), (1, 0, 0), (0, 1, 0), (1, 1, 0), (1, 0, 1), (0, 1, 1), (1, 1, 1)]


def _exchange_grads(per_layer, n_layers, small):
    flat = [g for per_w in per_layer for g in per_w]
    n_big = len(flat)
    n = n_big + 1
    out_shapes = [jax.ShapeDtypeStruct((N_DEV, n_layers) + per_w[0].shape[1:], per_w[0].dtype)
                  for per_w in per_layer]
    out_shapes.append(jax.ShapeDtypeStruct((N_DEV,) + small.shape, small.dtype))
    n_w = len(per_layer)

    def body(*refs):
        srcs = refs[:n]
        lands = refs[n:n + n_w + 1]
        send_sems, recv_sems, local_sems = refs[n + n_w + 1:]
        x, y, c = lax.axis_index("x"), lax.axis_index("y"), lax.axis_index("c")
        me = _dev_index(x, y, c)

        def src_of(k, dest_idx):
            return srcs[k] if k == n_big else srcs[k].at[dest_idx]

        def land_of(k, src_idx):
            if k == n_big:
                return lands[n_w].at[src_idx]
            return lands[k // n_layers].at[src_idx, k % n_layers]

        mine = [pltpu.make_async_copy(src_of(k, me), land_of(k, me), local_sems.at[k]) for k in range(n)]
        for cp in mine:
            cp.start()
        sends = []
        for s, (fx, fy, fc) in enumerate(_PEER_FLIPS):
            px, py, pc = x ^ fx, y ^ fy, c ^ fc
            pidx = _dev_index(px, py, pc)
            for k in range(n):
                cp = pltpu.make_async_remote_copy(
                    src_ref=src_of(k, pidx), dst_ref=land_of(k, me),
                    send_sem=send_sems.at[k, s], recv_sem=recv_sems.at[k, s],
                    device_id=(px, py, pc), device_id_type=MESH)
                cp.start()
                sends.append(cp)
        for s, (fx, fy, fc) in enumerate(_PEER_FLIPS):
            pidx = _dev_index(x ^ fx, y ^ fy, c ^ fc)
            for k in range(n):
                pltpu.make_async_remote_copy(
                    src_ref=land_of(k, pidx), dst_ref=land_of(k, pidx),
                    send_sem=send_sems.at[k, s], recv_sem=recv_sems.at[k, s],
                    device_id=(x, y, c), device_id_type=MESH).wait_recv()
        for cp in sends:
            cp.wait_send()
        for cp in mine:
            cp.wait()

    return pl.pallas_call(
        body, name="exchange_grads",
        in_specs=[ANY] * n, out_specs=[ANY] * (n_w + 1), out_shape=out_shapes,
        scratch_shapes=[pltpu.SemaphoreType.DMA((n, 7)), pltpu.SemaphoreType.DMA((n, 7)),
                        pltpu.SemaphoreType.DMA((n,))],
    )(*flat, small)


def _in_proj(h, win_g, l, s, tq, part):
    pad = part
    dtype = BF16 if part else F32

    def body(a_ref, b_ref, o_ref):
        def compute():
            a = a_ref[...]
            o_ref[:, 0:PROJ_SHARD] = _dot(a, b_ref[0]).astype(dtype)
            o_ref[:, PROJ_SHARD:2 * PROJ_SHARD] = _dot(a, b_ref[1]).astype(dtype)

        if pad:
            i = pl.program_id(0)

            @pl.when(i == 0)
            def _():
                o_ref[...] = jnp.zeros(o_ref.shape, dtype)

            pl.when(i > 0)(compute)
        else:
            compute()

    return pl.pallas_call(
        body, name="in_proj_qkv" if part else "in_proj_conv", grid=(s // tq + pad, 2),
        in_specs=[pl.BlockSpec((tq, D_MODEL), lambda i, j: (jnp.maximum(i - pad, 0), 0)),
                  pl.BlockSpec((None, 2, D_MODEL, PROJ_SHARD), lambda i, j: (l, 2 * part + j, 0, 0))],
        out_specs=pl.BlockSpec((tq, 2 * PROJ_SHARD), lambda i, j: (i, j)),
        out_shape=jax.ShapeDtypeStruct((s + pad * tq, PROJ_WIDTH // 2), dtype),
        compiler_params=_cparams(2),
    )(h, win_g)


def _conv_fwd(pc, wc, g3, l, s, tr):
    hb = tr // 8

    def body(pc_ref, prev_ref, wc_ref, g_ref, o_ref):
        i = pl.program_id(0)
        gmat = _group_matrix()
        for j in range(CONV_WIDTH // 128):
            c0, c1, c2 = 128 * j, CONV_WIDTH + 128 * j, 2 * CONV_WIDTH + 128 * j
            hc = pc_ref[:, c0:c0 + 128]
            bg = pc_ref[:, c1:c1 + 128]
            cg = pc_ref[:, c2:c2 + 128]
            u_prev = jnp.where(i > 0, prev_ref[:, c2:c2 + 128] * prev_ref[:, c0:c0 + 128], 0.0)
            u = cg * hc
            full = jnp.concatenate([u_prev, u], axis=0)
            u1 = pltpu.roll(full, 1, 0)[8:]
            u2 = pltpu.roll(full, 2, 0)[8:]
            out = (u2 * wc_ref[0:1, c0:c0 + 128] + u1 * wc_ref[1:2, c0:c0 + 128]
                   + u * wc_ref[2:3, c0:c0 + 128])
            yc = bg * out
            r = lax.rsqrt(_group_mean(yc * yc, gmat) + EPS)
            o_ref[:, c0:c0 + 128] = (yc * r * g_ref[:, c0:c0 + 128]).astype(BF16)

    return pl.pallas_call(
        body, name="conv_fwd", grid=(s // tr,),
        in_specs=[pl.BlockSpec((tr, 3 * CONV_WIDTH), lambda i: (i, 0)),
                  pl.BlockSpec((8, 3 * CONV_WIDTH), lambda i: (jnp.maximum(i * hb - 1, 0), 0)),
                  pl.BlockSpec((None, 8, CONV_WIDTH), lambda i: (l, 0, 0)),
                  pl.BlockSpec((None, 1, CONV_WIDTH), lambda i: (l, 0, 0))],
        out_specs=pl.BlockSpec((tr, CONV_WIDTH), lambda i: (i, 0)),
        out_shape=jax.ShapeDtypeStruct((s, CONV_WIDTH), BF16),
        compiler_params=_cparams(1),
    )(pc, pc, wc, g3)


def _toeplitz_source(need_p=True):
    r_i = lax.broadcasted_iota(jnp.int32, (REL_PAD, TOEP), 0)
    m_i = lax.broadcasted_iota(jnp.int32, (REL_PAD, TOEP), 1)
    idx = jnp.clip((K_BAND - 1) - m_i, -REL_CLIP, REL_CLIP) + REL_CLIP
    return jnp.where(r_i == idx, 1.0, 0.0).astype(BF16)


def _bias_build(rbp):
    n_layers = rbp.shape[0]

    def body(rb_ref, o_ref, t_ref):
        pmat = _toeplitz_source()
        hi, mid, lo = _split3(rb_ref[...])
        t_ref[...] = _dot(hi, pmat) + _dot(mid, pmat) + _dot(lo, pmat)
        shift = (Q_BLOCK - 1) - lax.broadcasted_iota(jnp.int32, (Q_BLOCK, TOEP), 0)
        kk = lax.broadcasted_iota(jnp.int32, (Q_BLOCK, K_BAND), 1) >> 6
        qq = lax.broadcasted_iota(jnp.int32, (Q_BLOCK, K_BAND), 0) >> 6
        dchunk = kk - qq
        in_band = jnp.where(dchunk >= 0, jnp.where(dchunk <= N_LEFT_CHUNKS, 1, 0), 0) == 1
        for h in range(N_HEADS):
            b = jnp.broadcast_to(t_ref[pl.ds(h, 1), :], (Q_BLOCK, TOEP))
            for bit in range(8):
                rolled = pltpu.roll(b, TOEP - (1 << bit), 1)
                b = jnp.where(((shift >> bit) & 1) == 1, rolled, b)
            o_ref[h] = jnp.where(in_band, b[:, :K_BAND], NEG_INF)

    return pl.pallas_call(
        body, name="bias_build", grid=(n_layers,),
        in_specs=[pl.BlockSpec((None, N_HEADS, REL_PAD), lambda l: (l, 0, 0))],
        out_specs=pl.BlockSpec((None, N_HEADS, Q_BLOCK, K_BAND), lambda l: (l, 0, 0, 0)),
        out_shape=jax.ShapeDtypeStruct((n_layers, N_HEADS, Q_BLOCK, K_BAND), F32),
        scratch_shapes=[pltpu.VMEM((N_HEADS, TOEP), F32)],
        compiler_params=_cparams(1),
    )(rbp)


def _bias_bwd(ds_sum):
    n_layers = ds_sum.shape[0]

    def body(ds_ref, o_ref, t_ref):
        pmat = _toeplitz_source()
        shift = (Q_BLOCK - 1) - lax.broadcasted_iota(jnp.int32, (Q_BLOCK, TOEP), 0)
        for h in range(N_HEADS):
            d = jnp.concatenate([ds_ref[h], jnp.zeros((Q_BLOCK, TOEP - K_BAND), F32)], axis=1)
            for bit in range(8):
                rolled = pltpu.roll(d, 1 << bit, 1)
                d = jnp.where(((shift >> bit) & 1) == 1, rolled, d)
            t_ref[pl.ds(h, 1), :] = jnp.sum(d, axis=0, keepdims=True)
        hi, mid, lo = _split3(t_ref[...])
        o_ref[...] = _dot_nt(hi, pmat) + _dot_nt(mid, pmat) + _dot_nt(lo, pmat)

    return pl.pallas_call(
        body, name="bias_bwd", grid=(n_layers,),
        in_specs=[pl.BlockSpec((None, N_HEADS, Q_BLOCK, K_BAND), lambda l: (l, 0, 0, 0))],
        out_specs=pl.BlockSpec((None, N_HEADS, REL_PAD), lambda l: (l, 0, 0)),
        out_shape=jax.ShapeDtypeStruct((n_layers, N_HEADS, REL_PAD), F32),
        scratch_shapes=[pltpu.VMEM((N_HEADS, TOEP), F32)],
        compiler_params=_cparams(1),
    )(ds_sum)


def _attn_fwd(qkvp, biasm, g3, l, s, pad):
    nb = s // Q_BLOCK
    qb0 = pad // Q_BLOCK
    scale = HEAD_DIM ** -0.5

    def body(q_ref, k_ref, v_ref, b_ref, g_ref, o_ref, lse_ref, yn_ref):
        blk = pl.program_id(1)
        koff = pl.multiple_of(blk * Q_BLOCK + (pad - LEFT), Q_BLOCK)
        q = q_ref[...]
        kb = k_ref[pl.ds(koff, K_BAND), :]
        vb = v_ref[pl.ds(koff, K_BAND), :]
        lane = lax.broadcasted_iota(jnp.int32, (1, 128), 1)
        kpos = lax.broadcasted_iota(jnp.int32, (Q_BLOCK, K_BAND), 1) + (blk * Q_BLOCK - LEFT)
        kvalid = kpos >= 0
        outs, lses = [], []
        for hh in range(2):
            in_head = (lane >> 6) == hh
            qm = jnp.where(in_head, q, jnp.zeros_like(q)) * jnp.asarray(scale, BF16)
            sc = _dot_nt(qm, kb) + b_ref[hh]
            sc = jnp.where(kvalid, sc, NEG_INF)
            m = jnp.max(sc, axis=1, keepdims=True)
            e = jnp.exp(sc - m)
            den = jnp.sum(e, axis=1, keepdims=True)
            outs.append(_dot(e.astype(BF16), vb) * (1.0 / den))
            lses.append(m + jnp.log(den))
        first = lane < HEAD_DIM
        o = jnp.where(first, outs[0], outs[1])
        o_ref[...] = o
        lse_ref[...] = jnp.where(first, lses[0], lses[1])
        r = lax.rsqrt(_group_mean(o * o, _group_matrix()) + EPS)
        yn_ref[...] = (o * r * g_ref[...]).astype(BF16)

    blk_out = pl.BlockSpec((Q_BLOCK, 128), lambda p, b: (b, p))
    return pl.pallas_call(
        body, name="attn_fwd", grid=(ATTN_WIDTH // 128, nb),
        in_specs=[pl.BlockSpec((Q_BLOCK, 128), lambda p, b: (qb0 + b, p)),
                  pl.BlockSpec((s + pad, 128), lambda p, b: (0, 4 + p)),
                  pl.BlockSpec((s + pad, 128), lambda p, b: (0, 8 + p)),
                  pl.BlockSpec((None, 2, Q_BLOCK, K_BAND), lambda p, b: (l, p, 0, 0)),
                  pl.BlockSpec((None, 1, 128), lambda p, b: (l, 0, p))],
        out_specs=[blk_out, blk_out, blk_out],
        out_shape=[jax.ShapeDtypeStruct((s, ATTN_WIDTH), F32),
                   jax.ShapeDtypeStruct((s, ATTN_WIDTH), F32),
                   jax.ShapeDtypeStruct((s, ATTN_WIDTH), BF16)],
        compiler_params=_cparams(2),
    )(qkvp, qkvp, qkvp, biasm, g3)


def _out_proj_fwd(ync, yna, wout_g, x, g_post3, g_next3, l, s, tm):
    half = D_MODEL // 2

    def body(a1_ref, a2_ref, w_ref, x_ref, gp_ref, gn_ref, z_ref, xm_ref, h_ref):
        z = _dot(a1_ref[...], w_ref[0:half, :]) + _dot(a2_ref[...], w_ref[half:D_MODEL, :])
        z_ref[...] = z
        xm = x_ref[...] + z * _rstd(z) * gp_ref[...]
        xm_ref[...] = xm
        h_ref[...] = (xm * _rstd(xm) * gn_ref[...]).astype(BF16)

    row = pl.BlockSpec((tm, D_MODEL), lambda i: (i, 0))
    gain = pl.BlockSpec((None, 1, D_MODEL), lambda i: (l, 0, 0))
    return pl.pallas_call(
        body, name="out_proj_fwd", grid=(s // tm,),
        in_specs=[pl.BlockSpec((tm, half), lambda i: (i, 0)), pl.BlockSpec((tm, half), lambda i: (i, 0)),
                  pl.BlockSpec((None, D_MODEL, D_MODEL), lambda i: (l, 0, 0)), row, gain, gain],
        out_specs=[row, row, row],
        out_shape=[jax.ShapeDtypeStruct((s, D_MODEL), F32), jax.ShapeDtypeStruct((s, D_MODEL), F32),
                   jax.ShapeDtypeStruct((s, D_MODEL), BF16)],
        compiler_params=_cparams(1),
    )(ync, yna, wout_g, x, g_post3, g_next3)


def _ffn_in_fwd(h2, wfin_g, l, s, tm):
    def body(h_ref, wg_ref, wu_ref, gate_ref, up_ref, act_ref):
        h = h_ref[...]
        gate = _dot(h, wg_ref[...])
        up = _dot(h, wu_ref[...])
        gate_ref[...] = gate.astype(BF16)
        up_ref[...] = up.astype(BF16)
        act_ref[...] = (gate * (1.0 / (1.0 + jnp.exp(-gate))) * up).astype(BF16)

    blk = pl.BlockSpec((None, tm, FF_SHARD), lambda b, i: (b, i, 0))
    shp = jax.ShapeDtypeStruct((4, s, FF_SHARD), BF16)
    return pl.pallas_call(
        body, name="ffn_in_fwd", grid=(4, s // tm),
        in_specs=[pl.BlockSpec((tm, D_MODEL), lambda b, i: (i, 0)),
                  pl.BlockSpec((None, None, D_MODEL, FF_SHARD), lambda b, i: (l, b, 0, 0)),
                  pl.BlockSpec((None, None, D_MODEL, FF_SHARD), lambda b, i: (l, 4 + b, 0, 0))],
        out_specs=[blk, blk, blk], out_shape=[shp, shp, shp],
        compiler_params=_cparams(2),
    )(h2, wfin_g, wfin_g)


def _ffn_out_fwd(act, wfo4, xm, g_post3, g_next3, l, l_next, s, tm):
    def body(a_ref, w_ref, x_ref, gp_ref, gn_ref, f_ref, xo_ref, h_ref):
        f = _dot(a_ref[0], w_ref[0])
        for b in range(1, 4):
            f = f + _dot(a_ref[b], w_ref[b])
        f_ref[...] = f
        xo = x_ref[...] + f * _rstd(f) * gp_ref[...]
        xo_ref[...] = xo
        h_ref[...] = (xo * _rstd(xo) * gn_ref[...]).astype(BF16)

    row = pl.BlockSpec((tm, D_MODEL), lambda i: (i, 0))
    return pl.pallas_call(
        body, name="ffn_out_fwd", grid=(s // tm,),
        in_specs=[pl.BlockSpec((4, tm, FF_SHARD), lambda i: (0, i, 0)),
                  pl.BlockSpec((None, 4, FF_SHARD, D_MODEL), lambda i: (l, 0, 0, 0)), row,
                  pl.BlockSpec((None, 1, D_MODEL), lambda i: (l, 0, 0)),
                  pl.BlockSpec((None, 1, D_MODEL), lambda i: (l_next, 0, 0))],
        out_specs=[row, row, row],
        out_shape=[jax.ShapeDtypeStruct((s, D_MODEL), F32), jax.ShapeDtypeStruct((s, D_MODEL), F32),
                   jax.ShapeDtypeStruct((s, D_MODEL), BF16)],
        compiler_params=_cparams(1),
    )(act, wfo4, xm, g_post3, g_next3)


def _loss_grad(xf, target, s, tm):
    def body(x_ref, t_ref, dx_ref, sq_ref):
        i = pl.program_id(0)
        err = x_ref[...] - t_ref[...]
        dx_ref[...] = err * (1.0 / D_MODEL)
        cs = jnp.sum(err * err, axis=0, keepdims=True)
        part = cs[:, 0:128]
        for k in range(1, D_MODEL // 128):
            part = part + cs[:, 128 * k:128 * (k + 1)]

        @pl.when(i == 0)
        def _():
            sq_ref[...] = jnp.zeros(sq_ref.shape, F32)

        sq_ref[0:1, :] += part

    row = pl.BlockSpec((tm, D_MODEL), lambda i: (i, 0))
    return pl.pallas_call(
        body, name="loss_grad", grid=(s // tm,),
        in_specs=[row, row],
        out_specs=[row, pl.BlockSpec((8, 128), lambda i: (0, 0))],
        out_shape=[jax.ShapeDtypeStruct((s, D_MODEL), F32), jax.ShapeDtypeStruct((8, 128), F32)],
        compiler_params=_cparams(1),
    )(xf, target)


def _norm_bwd_rows(v, g, dy):
    r = _rstd(v)
    vn = v * r
    gd = dy * g
    dv = r * (gd - vn * jnp.mean(vn * gd, axis=-1, keepdims=True))
    return dv, dy * vn


def _accum_cols(ref, val, first):
    @pl.when(first)
    def _():
        ref[...] = jnp.zeros(ref.shape, F32)

    ref[0:1, :] += jnp.sum(val, axis=0, keepdims=True)


def _norm_bwd(z, g3, dy, l, s, tm):
    def body(z_ref, g_ref, dy_ref, dz_ref, dg_ref):
        dz, dyn = _norm_bwd_rows(z_ref[...], g_ref[...], dy_ref[...])
        dz_ref[...] = dz.astype(BF16)
        _accum_cols(dg_ref, dyn, pl.program_id(0) == 0)

    row = pl.BlockSpec((tm, D_MODEL), lambda i: (i, 0))
    return pl.pallas_call(
        body, name="norm_bwd", grid=(s // tm,),
        in_specs=[row, pl.BlockSpec((None, 1, D_MODEL), lambda i: (l, 0, 0)), row],
        out_specs=[row, pl.BlockSpec((8, D_MODEL), lambda i: (0, 0))],
        out_shape=[jax.ShapeDtypeStruct((s, D_MODEL), BF16), jax.ShapeDtypeStruct((8, D_MODEL), F32)],
        compiler_params=_cparams(1),
    )(z, g3, dy)


def _ffn_out_bwd(df, wfo4, gate, up, l, s, tm):
    def body(df_ref, w_ref, gate_ref, up_ref, dg_ref, du_ref):
        da = _dot_nt(df_ref[...], w_ref[...])
        g = gate_ref[...].astype(F32)
        u = up_ref[...].astype(F32)
        sg = 1.0 / (1.0 + jnp.exp(-g))
        dg_ref[...] = (da * u * (sg * (1.0 + g * (1.0 - sg)))).astype(BF16)
        du_ref[...] = (da * (g * sg)).astype(BF16)

    blk = pl.BlockSpec((None, tm, FF_SHARD), lambda b, i: (b, i, 0))
    shp = jax.ShapeDtypeStruct((4, s, FF_SHARD), BF16)
    return pl.pallas_call(
        body, name="ffn_out_bwd", grid=(4, s // tm),
        in_specs=[pl.BlockSpec((tm, D_MODEL), lambda b, i: (i, 0)),
                  pl.BlockSpec((None, None, FF_SHARD, D_MODEL), lambda b, i: (l, b, 0, 0)), blk, blk],
        out_specs=[blk, blk], out_shape=[shp, shp],
        compiler_params=_cparams(2),
    )(df, wfo4, gate, up)


def _dw_ffn_out(act, df, s, tm):
    nm = s // tm

    def body(a_ref, b_ref, o_ref, acc_ref):
        m = pl.program_id(1)

        @pl.when(m == 0)
        def _():
            acc_ref[...] = jnp.zeros(acc_ref.shape, F32)

        acc_ref[...] += _dot_tn(a_ref[...], b_ref[...])

        @pl.when(m == nm - 1)
        def _():
            o_ref[...] = acc_ref[...].astype(BF16)

    return pl.pallas_call(
        body, name="dw_ffn_out", grid=(4, nm),
        in_specs=[pl.BlockSpec((None, tm, FF_SHARD), lambda n, m: (n, m, 0)),
                  pl.BlockSpec((tm, D_MODEL), lambda n, m: (m, 0))],
        out_specs=pl.BlockSpec((FF_SHARD, D_MODEL), lambda n, m: (n, 0)),
        out_shape=jax.ShapeDtypeStruct((D_FF, D_MODEL), BF16),
        scratch_shapes=[pltpu.VMEM((FF_SHARD, D_MODEL), F32)],
        compiler_params=_cparams(2),
    )(act, df)


def _dw_ffn_in(h2, dgate, dup, s, tm):
    nm = s // tm

    def body(a_ref, b1_ref, b2_ref, o_ref, acc_ref):
        n, m = pl.program_id(0), pl.program_id(1)

        @pl.when(m == 0)
        def _():
            acc_ref[...] = jnp.zeros(acc_ref.shape, F32)

        @pl.when(n < 4)
        def _():
            acc_ref[...] += _dot_tn(a_ref[...], b1_ref[...])

        @pl.when(n >= 4)
        def _():
            acc_ref[...] += _dot_tn(a_ref[...], b2_ref[...])

        @pl.when(m == nm - 1)
        def _():
            o_ref[...] = acc_ref[...].astype(BF16)

    return pl.pallas_call(
        body, name="dw_ffn_in", grid=(N_DEV, nm),
        in_specs=[pl.BlockSpec((tm, D_MODEL), lambda n, m: (m, 0)),
                  pl.BlockSpec((None, tm, FF_SHARD), lambda n, m: (jnp.minimum(n, 3), m, 0)),
                  pl.BlockSpec((None, tm, FF_SHARD), lambda n, m: (jnp.maximum(n - 4, 0), m, 0))],
        out_specs=pl.BlockSpec((None, D_MODEL, FF_SHARD), lambda n, m: (n, 0, 0)),
        out_shape=jax.ShapeDtypeStruct((N_DEV, D_MODEL, FF_SHARD), BF16),
        scratch_shapes=[pltpu.VMEM((D_MODEL, FF_SHARD), F32)],
        compiler_params=_cparams(2),
    )(h2, dgate, dup)


def _ffn_in_bwd(dgate, dup, wfin_g, xm, g_pre3, dres, z, g_post3, l, s, tm):
    def body(d1_ref, d2_ref, w_ref, xm_ref, gp_ref, dres_ref, z_ref, gq_ref,
             dxm_ref, dz_ref, dgp_ref, dgq_ref, acc_ref):
        i, j = pl.program_id(0), pl.program_id(1)

        @pl.when(j == 0)
        def _():
            acc_ref[...] = jnp.zeros(acc_ref.shape, F32)

        @pl.when(j < 4)
        def _():
            acc_ref[...] += _dot_nt(d1_ref[...], w_ref[...])

        @pl.when(j >= 4)
        def _():
            acc_ref[...] += _dot_nt(d2_ref[...], w_ref[...])

        @pl.when(j == N_DEV - 1)
        def _():
            dh = acc_ref[...]
            dx, dyn = _norm_bwd_rows(xm_ref[...], gp_ref[...], dh)
            dxm = dres_ref[...] + dx
            dxm_ref[...] = dxm
            _accum_cols(dgp_ref, dyn, i == 0)
            dz, dyn2 = _norm_bwd_rows(z_ref[...], gq_ref[...], dxm)
            dz_ref[...] = dz.astype(BF16)
            _accum_cols(dgq_ref, dyn2, i == 0)

    row = pl.BlockSpec((tm, D_MODEL), lambda i, j: (i, 0))
    gain = pl.BlockSpec((None, 1, D_MODEL), lambda i, j: (l, 0, 0))
    dgs = pl.BlockSpec((8, D_MODEL), lambda i, j: (0, 0))
    return pl.pallas_call(
        body, name="ffn_in_bwd", grid=(s // tm, N_DEV),
        in_specs=[pl.BlockSpec((None, tm, FF_SHARD), lambda i, j: (jnp.minimum(j, 3), i, 0)),
                  pl.BlockSpec((None, tm, FF_SHARD), lambda i, j: (jnp.maximum(j - 4, 0), i, 0)),
                  pl.BlockSpec((None, None, D_MODEL, FF_SHARD), lambda i, j: (l, j, 0, 0)),
                  row, gain, row, row, gain],
        out_specs=[row, row, dgs, dgs],
        out_shape=[jax.ShapeDtypeStruct((s, D_MODEL), F32), jax.ShapeDtypeStruct((s, D_MODEL), BF16),
                   jax.ShapeDtypeStruct((8, D_MODEL), F32), jax.ShapeDtypeStruct((8, D_MODEL), F32)],
        scratch_shapes=[pltpu.VMEM((tm, D_MODEL), F32)],
        compiler_params=_cparams(2),
    )(dgate, dup, wfin_g, xm, g_pre3, dres, z, g_post3)


def _dw_out(ync, yna, dz, s, tm):
    nm = s // tm
    half = D_MODEL // 2

    def body(a1_ref, a2_ref, b_ref, o_ref, acc_ref):
        n, m = pl.program_id(0), pl.program_id(1)

        @pl.when(m == 0)
        def _():
            acc_ref[...] = jnp.zeros(acc_ref.shape, F32)

        @pl.when(n == 0)
        def _():
            acc_ref[...] += _dot_tn(a1_ref[...], b_ref[...])

        @pl.when(n == 1)
        def _():
            acc_ref[...] += _dot_tn(a2_ref[...], b_ref[...])

        @pl.when(m == nm - 1)
        def _():
            o_ref[...] = acc_ref[...].astype(BF16)

    return pl.pallas_call(
        body, name="dw_out", grid=(2, nm),
        in_specs=[pl.BlockSpec((tm, half), lambda n, m: (m, 0)), pl.BlockSpec((tm, half), lambda n, m: (m, 0)),
                  pl.BlockSpec((tm, D_MODEL), lambda n, m: (m, 0))],
        out_specs=pl.BlockSpec((half, D_MODEL), lambda n, m: (n, 0)),
        out_shape=jax.ShapeDtypeStruct((D_MODEL, D_MODEL), BF16),
        scratch_shapes=[pltpu.VMEM((half, D_MODEL), F32)],
        compiler_params=_cparams(2),
    )(ync, yna, dz)


def _out_proj_bwd(dz, wout_g, o, g3, l, s, tm):
    def body(dz_ref, w_ref, o_ref, g_ref, dyc_ref, do_ref, dg_ref):
        dy = _dot_nt(dz_ref[...], w_ref[...])
        dyc_ref[...] = dy[:, 0:CONV_WIDTH]
        gmat = _group_matrix()
        first = pl.program_id(0) == 0

        @pl.when(first)
        def _():
            dg_ref[...] = jnp.zeros(dg_ref.shape, F32)

        for j in range(ATTN_WIDTH // 128):
            c0 = 128 * j
            ov = o_ref[:, c0:c0 + 128]
            dyn = dy[:, CONV_WIDTH + c0:CONV_WIDTH + c0 + 128]
            r = lax.rsqrt(_group_mean(ov * ov, gmat) + EPS)
            on = ov * r
            gd = dyn * g_ref[:, c0:c0 + 128]
            do_ref[:, c0:c0 + 128] = r * (gd - on * _group_mean(on * gd, gmat))
            dg_ref[0:1, c0:c0 + 128] += jnp.sum(dyn * on, axis=0, keepdims=True)

    halfrow = pl.BlockSpec((tm, ATTN_WIDTH), lambda i: (i, 0))
    return pl.pallas_call(
        body, name="out_proj_bwd", grid=(s // tm,),
        in_specs=[pl.BlockSpec((tm, D_MODEL), lambda i: (i, 0)),
                  pl.BlockSpec((None, D_MODEL, D_MODEL), lambda i: (l, 0, 0)), halfrow,
                  pl.BlockSpec((None, 1, ATTN_WIDTH), lambda i: (l, 0, 0))],
        out_specs=[halfrow, halfrow, pl.BlockSpec((8, ATTN_WIDTH), lambda i: (0, 0))],
        out_shape=[jax.ShapeDtypeStruct((s, CONV_WIDTH), F32), jax.ShapeDtypeStruct((s, ATTN_WIDTH), F32),
                   jax.ShapeDtypeStruct((8, ATTN_WIDTH), F32)],
        compiler_params=_cparams(1),
    )(dz, wout_g, o, g3)


def _conv_bwd(pc, dyc, wc, g3, l, s, tr):
    hb = tr // 8
    nt = s // tr
    ext = tr + 16
    last_hb = s // 8 - 1

    def body(pc_ref, prev_ref, next_ref, dy_ref, dyn_ref, wc_ref, g_ref, dpc_ref, dw_ref, dg_ref):
        i = pl.program_id(0)
        gmat = _group_matrix()
        row = lax.broadcasted_iota(jnp.int32, (ext, 128), 0) + (i * tr - 8)
        inside = jnp.where(row >= 0, jnp.where(row < s, 1, 0), 0) == 1

        @pl.when(i == 0)
        def _():
            dw_ref[...] = jnp.zeros(dw_ref.shape, F32)
            dg_ref[...] = jnp.zeros(dg_ref.shape, F32)

        def extend(ref_prev, ref_mid, ref_next, c):
            parts = [ref_prev[:, c:c + 128] if ref_prev is not None else jnp.zeros((8, 128), F32),
                     ref_mid[:, c:c + 128], ref_next[:, c:c + 128]]
            return jnp.concatenate(parts, axis=0)

        for j in range(CONV_WIDTH // 128):
            c0, c1, c2 = 128 * j, CONV_WIDTH + 128 * j, 2 * CONV_WIDTH + 128 * j
            hc = extend(prev_ref, pc_ref, next_ref, c0)
            bg = extend(prev_ref, pc_ref, next_ref, c1)
            cg = extend(prev_ref, pc_ref, next_ref, c2)
            dyn = extend(None, dy_ref, dyn_ref, c0)
            w0, w1, w2 = (wc_ref[0:1, c0:c0 + 128], wc_ref[1:2, c0:c0 + 128], wc_ref[2:3, c0:c0 + 128])
            gain = g_ref[:, c0:c0 + 128]
            u = jnp.where(inside, cg * hc, 0.0)
            u1 = pltpu.roll(u, 1, 0)
            u2 = pltpu.roll(u, 2, 0)
            out = u2 * w0 + u1 * w1 + u * w2
            yc = bg * out
            r = lax.rsqrt(_group_mean(yc * yc, gmat) + EPS)
            ycn = yc * r
            gd = dyn * gain
            dyc = r * (gd - ycn * _group_mean(ycn * gd, gmat))
            dout = jnp.where(inside, dyc * bg, 0.0)
            du = dout * w2 + pltpu.roll(dout, ext - 1, 0) * w1 + pltpu.roll(dout, ext - 2, 0) * w0
            sl = slice(8, 8 + tr)
            dpc_ref[:, c0:c0 + 128] = (du[sl] * cg[sl]).astype(BF16)
            dpc_ref[:, c1:c1 + 128] = (dyc[sl] * out[sl]).astype(BF16)
            dpc_ref[:, c2:c2 + 128] = (du[sl] * hc[sl]).astype(BF16)
            dw_ref[0:1, c0:c0 + 128] += jnp.sum(dout[sl] * u2[sl], axis=0, keepdims=True)
            dw_ref[1:2, c0:c0 + 128] += jnp.sum(dout[sl] * u1[sl], axis=0, keepdims=True)
            dw_ref[2:3, c0:c0 + 128] += jnp.sum(dout[sl] * u[sl], axis=0, keepdims=True)
            dg_ref[0:1, c0:c0 + 128] += jnp.sum(dyn[sl] * ycn[sl], axis=0, keepdims=True)

    wide = 3 * CONV_WIDTH
    return pl.pallas_call(
        body, name="conv_bwd", grid=(nt,),
        in_specs=[pl.BlockSpec((tr, wide), lambda i: (i, 0)),
                  pl.BlockSpec((8, wide), lambda i: (jnp.maximum(i * hb - 1, 0), 0)),
                  pl.BlockSpec((8, wide), lambda i: (jnp.minimum((i + 1) * hb, last_hb), 0)),
                  pl.BlockSpec((tr, CONV_WIDTH), lambda i: (i, 0)),
                  pl.BlockSpec((8, CONV_WIDTH), lambda i: (jnp.minimum((i + 1) * hb, last_hb), 0)),
                  pl.BlockSpec((None, 8, CONV_WIDTH), lambda i: (l, 0, 0)),
                  pl.BlockSpec((None, 1, CONV_WIDTH), lambda i: (l, 0, 0))],
        out_specs=[pl.BlockSpec((tr, wide), lambda i: (i, 0)),
                   pl.BlockSpec((8, CONV_WIDTH), lambda i: (0, 0)),
                   pl.BlockSpec((8, CONV_WIDTH), lambda i: (0, 0))],
        out_shape=[jax.ShapeDtypeStruct((s, wide), BF16), jax.ShapeDtypeStruct((8, CONV_WIDTH), F32),
                   jax.ShapeDtypeStruct((8, CONV_WIDTH), F32)],
        compiler_params=_cparams(1),
    )(pc, pc, pc, dyc, dyc, wc, g3)


def _attn_bwd(qkvp, biasm, o, lse, do, l, s, pad):
    nb = s // Q_BLOCK
    qb0 = pad // Q_BLOCK
    scale = HEAD_DIM ** -0.5

    def body(q_ref, k_ref, v_ref, b_ref, o_ref, lse_ref, do_ref,
             dq_ref, dk_ref, dv_ref, ds_ref, dk_acc, dv_acc):
        blk = pl.program_id(1)

        @pl.when(blk == 0)
        def _():
            dk_acc[...] = jnp.zeros(dk_acc.shape, F32)
            dv_acc[...] = jnp.zeros(dv_acc.shape, F32)
            ds_ref[...] = jnp.zeros(ds_ref.shape, F32)

        koff = pl.multiple_of(blk * Q_BLOCK + (pad - LEFT), Q_BLOCK)
        q = q_ref[...]
        kb = k_ref[pl.ds(koff, K_BAND), :]
        vb = v_ref[pl.ds(koff, K_BAND), :]
        ov = o_ref[...]
        dov = do_ref[...]
        lse_v = lse_ref[...]
        lane = lax.broadcasted_iota(jnp.int32, (1, 128), 1)
        kpos = lax.broadcasted_iota(jnp.int32, (Q_BLOCK, K_BAND), 1) + (blk * Q_BLOCK - LEFT)
        kvalid = kpos >= 0
        prod = dov * ov
        dq_parts = []
        dk_new = jnp.zeros((K_BAND, 128), F32)
        dv_new = jnp.zeros((K_BAND, 128), F32)
        for hh in range(2):
            in_head = (lane >> 6) == hh
            qm = jnp.where(in_head, q, jnp.zeros_like(q)) * jnp.asarray(scale, BF16)
            dom = jnp.where(in_head, dov, 0.0).astype(BF16)
            delta = jnp.sum(jnp.where(in_head, prod, 0.0), axis=1, keepdims=True)
            lse_h = lse_v[:, HEAD_DIM * hh:HEAD_DIM * hh + 1]
            sc = _dot_nt(qm, kb) + b_ref[hh]
            sc = jnp.where(kvalid, sc, NEG_INF)
            p = jnp.exp(sc - lse_h)
            dp = _dot_nt(dom, vb)
            ds = p * (dp - delta)
            ds_ref[hh] += ds
            dsb = ds.astype(BF16)
            dq_parts.append(_dot(dsb, kb) * scale)
            dk_new = dk_new + _dot_tn(dsb, qm)
            dv_new = dv_new + _dot_tn(p.astype(BF16), dom)
        dq_ref[...] = jnp.where(lane < HEAD_DIM, dq_parts[0], dq_parts[1]).astype(BF16)
        dk_acc[pl.ds(koff, K_BAND), :] += dk_new
        dv_acc[pl.ds(koff, K_BAND), :] += dv_new

        @pl.when(blk == nb - 1)
        def _():
            dk_ref[...] = dk_acc[pad:pad + s, :].astype(BF16)
            dv_ref[...] = dv_acc[pad:pad + s, :].astype(BF16)

    qblk = pl.BlockSpec((Q_BLOCK, 128), lambda p, b: (b, p))
    col = pl.BlockSpec((s, 128), lambda p, b: (0, p))
    shp = jax.ShapeDtypeStruct((s, ATTN_WIDTH), BF16)
    return pl.pallas_call(
        body, name="attn_bwd", grid=(ATTN_WIDTH // 128, nb),
        in_specs=[pl.BlockSpec((Q_BLOCK, 128), lambda p, b: (qb0 + b, p)),
                  pl.BlockSpec((s + pad, 128), lambda p, b: (0, 4 + p)),
                  pl.BlockSpec((s + pad, 128), lambda p, b: (0, 8 + p)),
                  pl.BlockSpec((None, 2, Q_BLOCK, K_BAND), lambda p, b: (l, p, 0, 0)),
                  qblk, qblk, qblk],
        out_specs=[qblk, col, col, pl.BlockSpec((2, Q_BLOCK, K_BAND), lambda p, b: (p, 0, 0))],
        out_shape=[shp, shp, shp, jax.ShapeDtypeStruct((N_HEADS, Q_BLOCK, K_BAND), F32)],
        scratch_shapes=[pltpu.VMEM((s + pad, 128), F32), pltpu.VMEM((s + pad, 128), F32)],
        compiler_params=_cparams(2),
    )(qkvp, qkvp, qkvp, biasm, o, lse, do)


def _dw_in(h, dproj, s, tm):
    nm = s // tm

    def body(a_ref, b_ref, o_ref, acc_ref):
        m = pl.program_id(1)

        @pl.when(m == 0)
        def _():
            acc_ref[...] = jnp.zeros(acc_ref.shape, F32)

        acc_ref[...] += _dot_tn(a_ref[...], b_ref[...])

        @pl.when(m == nm - 1)
        def _():
            o_ref[0] = acc_ref[:, 0:PROJ_SHARD].astype(BF16)
            o_ref[1] = acc_ref[:, PROJ_SHARD:2 * PROJ_SHARD].astype(BF16)

    return pl.pallas_call(
        body, name="dw_in", grid=(4, nm),
        in_specs=[pl.BlockSpec((tm, D_MODEL), lambda n, m: (m, 0)),
                  pl.BlockSpec((tm, 2 * PROJ_SHARD), lambda n, m: (m, n))],
        out_specs=pl.BlockSpec((2, D_MODEL, PROJ_SHARD), lambda n, m: (n, 0, 0)),
        out_shape=jax.ShapeDtypeStruct((N_DEV, D_MODEL, PROJ_SHARD), BF16),
        scratch_shapes=[pltpu.VMEM((D_MODEL, 2 * PROJ_SHARD), F32)],
        compiler_params=_cparams(2),
    )(h, dproj)


def _in_proj_bwd(dproj, win_g, x, g3, dres, l, s, tm):
    def body(d_ref, w_ref, x_ref, g_ref, dres_ref, dx_ref, dg_ref, acc_ref):
        i, j = pl.program_id(0), pl.program_id(1)

        @pl.when(j == 0)
        def _():
            acc_ref[...] = jnp.zeros(acc_ref.shape, F32)

        acc_ref[...] += (_dot_nt(d_ref[:, 0:PROJ_SHARD], w_ref[0])
                         + _dot_nt(d_ref[:, PROJ_SHARD:2 * PROJ_SHARD], w_ref[1]))

        @pl.when(j == 3)
        def _():
            dx, dyn = _norm_bwd_rows(x_ref[...], g_ref[...], acc_ref[...])
            dx_ref[...] = dres_ref[...] + dx
            _accum_cols(dg_ref, dyn, i == 0)

    row = pl.BlockSpec((tm, D_MODEL), lambda i, j: (i, 0))
    return pl.pallas_call(
        body, name="in_proj_bwd", grid=(s // tm, 4),
        in_specs=[pl.BlockSpec((tm, 2 * PROJ_SHARD), lambda i, j: (i, j)),
                  pl.BlockSpec((None, 2, D_MODEL, PROJ_SHARD), lambda i, j: (l, j, 0, 0)),
                  row, pl.BlockSpec((None, 1, D_MODEL), lambda i, j: (l, 0, 0)), row],
        out_specs=[row, pl.BlockSpec((8, D_MODEL), lambda i, j: (0, 0))],
        out_shape=[jax.ShapeDtypeStruct((s, D_MODEL), F32), jax.ShapeDtypeStruct((8, D_MODEL), F32)],
        scratch_shapes=[pltpu.VMEM((tm, D_MODEL), F32)],
        compiler_params=_cparams(2),
    )(dproj, win_g, x, g3, dres)


def _adamw(name, w, m, v, land):
    rows, cols = w.shape
    n_part = land.shape[0]
    tr = _row_tile(rows, tuple(c for c in (512, 352, 256, 176, 128, 64, 32, 16, 8) if c * cols <= 256 * 1024))
    c1 = 1.0 - ADAM_B1 ** ADAM_STEP
    c2 = 1.0 - ADAM_B2 ** ADAM_STEP

    def body(w_ref, m_ref, v_ref, l_ref, g_ref, d_ref, nm_ref, nv_ref):
        g = l_ref[0].astype(F32)
        for p in range(1, n_part):
            g = g + l_ref[p].astype(F32)
        g_ref[...] = g
        m1 = ADAM_B1 * m_ref[...] + (1.0 - ADAM_B1) * g
        v1 = ADAM_B2 * v_ref[...] + (1.0 - ADAM_B2) * (g * g)
        nm_ref[...] = m1
        nv_ref[...] = v1
        d_ref[...] = -ADAM_LR * ((m1 / c1) / (jnp.sqrt(v1 / c2) + ADAM_EPS) + ADAM_WD * w_ref[...])

    blk = pl.BlockSpec((tr, cols), lambda i: (i, 0))
    shp = jax.ShapeDtypeStruct((rows, cols), F32)
    return pl.pallas_call(
        body, name=name, grid=(rows // tr,),
        in_specs=[blk, blk, blk, pl.BlockSpec((n_part, tr, cols), lambda i: (0, i, 0))],
        out_specs=[blk, blk, blk, blk], out_shape=[shp, shp, shp, shp],
        compiler_params=_cparams(1),
    )(w, m, v, land)


def _pack_small(rel, gco, gao, gpm, gqm, gpf, gqf):
    n_layers = rel.shape[0]
    relp = jnp.pad(rel, ((0, 0), (0, 0), (0, REL_PAD - rel.shape[2])))
    parts = [relp.reshape(n_layers * N_HEADS * REL_PAD // 128, 128)]
    parts += [a.reshape(-1, 128) for a in (gco, gao, gpm, gqm, gpf, gqf)]
    return jnp.concatenate(parts, axis=0)


def _unpack_small(p, n_layers):
    n_rel = n_layers * N_HEADS * REL_PAD // 128
    rel = p[:n_rel].reshape(n_layers, N_HEADS, REL_PAD)[:, :, :2 * REL_CLIP + 1]
    outs = [rel]
    r0 = n_rel
    for width in (CONV_WIDTH, ATTN_WIDTH, D_MODEL, D_MODEL, D_MODEL, D_MODEL):
        nr = n_layers * width // 128
        outs.append(p[r0:r0 + nr].reshape(n_layers, width))
        r0 += nr
    return outs


def kernel(x, w_in, w_conv, rel_bias, g_conv_out, g_attn_out, w_out, g_pre_mix, g_post_mix, g_pre_ffn, g_post_ffn, w_ffn_in, w_ffn_out, loss_target, m_w_in, m_w_conv, m_rel_bias, m_g_conv_out, m_g_attn_out, m_w_out, m_g_pre_mix, m_g_post_mix, m_g_pre_ffn, m_g_post_ffn, m_w_ffn_in, m_w_ffn_out, v_w_in, v_w_conv, v_rel_bias, v_g_conv_out, v_g_attn_out, v_w_out, v_g_pre_mix, v_g_post_mix, v_g_pre_ffn, v_g_post_ffn, v_w_ffn_in, v_w_ffn_out):
    n_layers = w_in.shape[0]
    s = x.shape[1]
    assert x.shape == (1, s, D_MODEL) and s % 1024 == 0
    assert w_in.shape == (n_layers, D_MODEL, PROJ_SHARD) and w_ffn_in.shape == (n_layers, D_MODEL, FF_SHARD)
    tm = 512
    tq = 1024 if s >= 2048 else 512
    tw = min(1024, s)
    x0 = x.reshape(s, D_MODEL)
    target = loss_target.reshape(s, D_MODEL)
    dev = _dev_index(lax.axis_index("x"), lax.axis_index("y"), lax.axis_index("c"))

    wc_local = jnp.pad(jnp.transpose(w_conv, (0, 2, 1)).reshape(-1), (0, 1024 - n_layers * 3 * 64)).reshape(8, 128)
    win_g, wout_g, wfin_g, wfout_g, wc_g = _all_gather(
        [_cast_bf16(w_in, "cast_w_in"), _cast_bf16(w_out, "cast_w_out"),
         _cast_bf16(w_ffn_in, "cast_w_ffn_in"), _cast_bf16(w_ffn_out, "cast_w_ffn_out"), wc_local],
        [True, True, True, True, False])
    wout_full = wout_g.reshape(n_layers, D_MODEL, D_MODEL)
    wfo4 = wfout_g.reshape(n_layers, 4, FF_SHARD, D_MODEL)
    wc_full = wc_g.reshape(N_DEV, 1024)[:, :n_layers * 3 * 64].reshape(N_DEV, n_layers, 3, 64)
    wc_full = jnp.transpose(wc_full, (1, 2, 0, 3)).reshape(n_layers, 3, CONV_WIDTH)
    wc_full = jnp.pad(wc_full, ((0, 0), (0, 5), (0, 0)))

    g3 = {k: v.reshape(n_layers, 1, -1) for k, v in dict(
        conv=g_conv_out, attn=g_attn_out, pre_mix=g_pre_mix, post_mix=g_post_mix,
        pre_ffn=g_pre_ffn, post_ffn=g_post_ffn).items()}
    biasm = _bias_build(jnp.pad(rel_bias, ((0, 0), (0, 0), (0, REL_PAD - rel_bias.shape[2]))))

    saved = []
    xl = x0
    h = _norm_cast(x0, g3["pre_mix"], 0, tm)
    for l in range(n_layers):
        pc = _in_proj(h, win_g, l, s, tq, 0)
        qkvp = _in_proj(h, win_g, l, s, tq, 1)
        ync = _conv_fwd(pc, wc_full, g3["conv"], l, s, tm)
        o, lse, yna = _attn_fwd(qkvp, biasm, g3["attn"], l, s, tq)
        z, xm, h2 = _out_proj_fwd(ync, yna, wout_full, xl, g3["post_mix"], g3["pre_ffn"], l, s, tm)
        gate, up, act = _ffn_in_fwd(h2, wfin_g, l, s, tm)
        l_next = min(l + 1, n_layers - 1)
        f, xo, h_next = _ffn_out_fwd(act, wfo4, xm, g3["post_ffn"], g3["pre_mix"], l, l_next, s, tm)
        saved.append(dict(x=xl, h=h, pc=pc, qkvp=qkvp, ync=ync, yna=yna, o=o, lse=lse, z=z, xm=xm,
                          h2=h2, gate=gate, up=up, act=act, f=f))
        xl, h = xo, h_next

    dx, sq = _loss_grad(xl, target, s, tm)
    loss = lax.psum(jnp.sum(sq) * (0.5 / D_MODEL), ("x", "y", "c"))

    grads = dict(win=[None] * n_layers, wout=[None] * n_layers, wfin=[None] * n_layers, wfout=[None] * n_layers)
    small = {k: [None] * n_layers for k in ("rel", "gco", "gao", "gpm", "gqm", "gpf", "gqf", "wc")}
    ds_all = [None] * n_layers
    for l in reversed(range(n_layers)):
        sv = saved[l]
        df, dg = _norm_bwd(sv["f"], g3["post_ffn"], dx, l, s, tm)
        small["gqf"][l] = dg[0]
        dgate, dup = _ffn_out_bwd(df, wfo4, sv["gate"], sv["up"], l, s, tm)
        grads["wfout"][l] = _dw_ffn_out(sv["act"], df, s, tw).reshape(N_DEV, FFO_SHARD, D_MODEL)
        grads["wfin"][l] = _dw_ffn_in(sv["h2"], dgate, dup, s, tw)
        dxm, dz, dg_pre_ffn, dg_post_mix = _ffn_in_bwd(
            dgate, dup, wfin_g, sv["xm"], g3["pre_ffn"], dx, sv["z"], g3["post_mix"], l, s, tm)
        small["gpf"][l] = dg_pre_ffn[0]
        small["gqm"][l] = dg_post_mix[0]
        grads["wout"][l] = _dw_out(sv["ync"], sv["yna"], dz, s, tw).reshape(N_DEV, D_MODEL // N_DEV, D_MODEL)
        dyc, do, dg_attn = _out_proj_bwd(dz, wout_full, sv["o"], g3["attn"], l, s, tm)
        small["gao"][l] = dg_attn[0]
        dpc, dwc, dg_conv = _conv_bwd(sv["pc"], dyc, wc_full, g3["conv"], l, s, tm)
        small["wc"][l] = dwc[0:3]
        small["gco"][l] = dg_conv[0]
        dq, dk, dv, ds_all[l] = _attn_bwd(sv["qkvp"], biasm, sv["o"], sv["lse"], do, l, s, tq)
        dproj = jnp.concatenate([dpc, dq, dk, dv], axis=1)
        grads["win"][l] = _dw_in(sv["h"], dproj, s, tw)
        dx, dg_pre_mix = _in_proj_bwd(dproj, win_g, sv["x"], g3["pre_mix"], dxm, l, s, tm)
        small["gpm"][l] = dg_pre_mix[0]
    grad_x = dx.reshape(1, s, D_MODEL)

    d_rel = _bias_bwd(jnp.stack(ds_all))[:, :, :2 * REL_CLIP + 1]
    small_vec = jnp.concatenate(
        [_pack_small(d_rel, *[jnp.stack(small[k]) for k in ("gco", "gao", "gpm", "gqm", "gpf", "gqf")]),
         jnp.stack(small["wc"]).reshape(-1, 128)], axis=0)
    pad_rows = (-small_vec.shape[0]) % 8
    small_vec = jnp.pad(small_vec, ((0, pad_rows), (0, 0)))

    land_in, land_out, land_fin, land_fout, land_small = _exchange_grads(
        [grads["win"], grads["wout"], grads["wfin"], grads["wfout"]], n_layers, small_vec)

    def big(name, w, m, v, land):
        shape = w.shape
        rows = shape[0] * shape[1]
        outs = _adamw(name, w.reshape(rows, shape[2]), m.reshape(rows, shape[2]), v.reshape(rows, shape[2]),
                      land.reshape(N_DEV, rows, shape[2]))
        return [t.reshape(shape) for t in outs]

    r_in = big("adamw_w_in", w_in, m_w_in, v_w_in, land_in)
    r_out = big("adamw_w_out", w_out, m_w_out, v_w_out, land_out)
    r_fin = big("adamw_w_ffn_in", w_ffn_in, m_w_ffn_in, v_w_ffn_in, land_fin)
    r_fout = big("adamw_w_ffn_out", w_ffn_out, m_w_ffn_out, v_w_ffn_out, land_fout)

    n_rep = 256 * n_layers // 4
    rep = _adamw(
        "adamw_replicated",
        _pack_small(rel_bias, g_conv_out, g_attn_out, g_pre_mix, g_post_mix, g_pre_ffn, g_post_ffn),
        _pack_small(m_rel_bias, m_g_conv_out, m_g_attn_out, m_g_pre_mix, m_g_post_mix, m_g_pre_ffn, m_g_post_ffn),
        _pack_small(v_rel_bias, v_g_conv_out, v_g_attn_out, v_g_pre_mix, v_g_post_mix, v_g_pre_ffn, v_g_post_ffn),
        land_small[:, :n_rep])
    rep = [_unpack_small(t, n_layers) for t in rep]

    wc_rows = n_layers * 3 * CONV_WIDTH // 128
    zeros_wc = jnp.zeros((wc_rows, 128), F32)
    g_wc_full = _adamw("sum_w_conv", zeros_wc, zeros_wc, zeros_wc, land_small[:, n_rep:n_rep + wc_rows])[0]
    g_wc_full = g_wc_full.reshape(n_layers, 3, CONV_WIDTH)
    g_wc = lax.dynamic_slice_in_dim(g_wc_full, dev * (CONV_WIDTH // N_DEV), CONV_WIDTH // N_DEV, axis=2)
    g_wc = jnp.transpose(g_wc, (0, 2, 1))

    def tiny(a):
        flat = a.reshape(-1)
        return jnp.pad(flat, (0, (-flat.shape[0]) % 1024)).reshape(-1, 128)

    r_wc = _adamw("adamw_w_conv", tiny(w_conv), tiny(m_w_conv), tiny(v_w_conv), tiny(g_wc)[None])
    r_wc = [t.reshape(-1)[:w_conv.size].reshape(w_conv.shape) for t in r_wc]

    def leaf(kind):
        return [r_in[kind], r_wc[kind], rep[kind][0], rep[kind][1], rep[kind][2], r_out[kind],
                rep[kind][3], rep[kind][4], rep[kind][5], rep[kind][6], r_fin[kind], r_fout[kind]]

    return (loss, grad_x, *leaf(0), *leaf(1), *leaf(2), *leaf(3))
```

```python
import math

import jax
import jax.numpy as jnp
from jax import lax
from jax.experimental import pallas as pl
from jax.experimental.pallas import tpu as pltpu

F32 = jnp.float32
BF16 = jnp.bfloat16

D_MODEL = 1024
N_DEV = 8
CHUNK = 64
N_LEFT_CHUNKS = 8
CONV_WIDTH = 512
ATTN_WIDTH = 512
HEAD_DIM = 64
N_HEADS = 8
REL_CLIP = 128
REL_PAD = 384
PROJ_WIDTH = 3072
PROJ_SHARD = PROJ_WIDTH // N_DEV
D_FF = 2816
FF_SHARD = 2 * D_FF // N_DEV
FFO_SHARD = D_FF // N_DEV
EPS = 1e-6
NEG_INF = -1e30
Q_BLOCK = 4 * CHUNK
K_BAND = Q_BLOCK + N_LEFT_CHUNKS * CHUNK
LEFT = N_LEFT_CHUNKS * CHUNK
TOEP = 1024

ADAM_LR = 0.001
ADAM_B1 = 0.9
ADAM_B2 = 0.999
ADAM_EPS = 1e-08
ADAM_WD = 0.01
ADAM_STEP = 10

VMEM_LIMIT = 52 * 1024 * 1024
MESH = pl.DeviceIdType.MESH
ANY = pl.BlockSpec(memory_space=pl.ANY)

NT = (((1,), (1,)), ((), ()))
TN = (((0,), (0,)), ((), ()))


def _dot(a, b):
    return jnp.dot(a, b, preferred_element_type=F32)


def _dot_nt(a, b):
    return lax.dot_general(a, b, NT, preferred_element_type=F32)


def _dot_tn(a, b):
    return lax.dot_general(a, b, TN, preferred_element_type=F32)


def _rstd(v):
    return lax.rsqrt(jnp.mean(v * v, axis=-1, keepdims=True) + EPS)


def _group_matrix():
    r = lax.broadcasted_iota(jnp.int32, (128, 128), 0) >> 6
    c = lax.broadcasted_iota(jnp.int32, (128, 128), 1) >> 6
    return jnp.where(r == c, 1.0, 0.0).astype(BF16)


def _group_mean(v, gmat):
    hi = v.astype(BF16)
    lo = (v - hi.astype(F32)).astype(BF16)
    return (_dot(hi, gmat) + _dot(lo, gmat)) * (1.0 / HEAD_DIM)


def _split3(v):
    hi = v.astype(BF16)
    r1 = v - hi.astype(F32)
    mid = r1.astype(BF16)
    lo = (r1 - mid.astype(F32)).astype(BF16)
    return hi, mid, lo


def _row_tile(rows, cands=(1024, 512, 704, 256, 128, 64, 32, 16)):
    for c in cands:
        if rows % c == 0:
            return c
    return rows


def _dev_index(px, py, pc):
    return 4 * px + 2 * py + pc


def _when(cond):
    if cond is True:
        return lambda fn: fn()
    return pl.when(cond)


def _phases(grid):
    def phases():
        if not grid:
            return True, True, True
        lin = pl.program_id(0)
        for a in range(1, len(grid)):
            lin = lin * grid[a] + pl.program_id(a)
        total = math.prod(grid)
        return lin == 0, lin == total // 2, lin == total - 1
    return phases


class _Gather:
    def __init__(self, items):
        self.items = items
        self.args = [a for a, _ in items]
        n = len(items)
        self.out_shape = [jax.ShapeDtypeStruct((N_DEV,) + (a.shape if lay is None else a.shape[1:]), a.dtype)
                          for a, lay in items]
        self.scratch = [pltpu.SemaphoreType.DMA((n, 7)), pltpu.SemaphoreType.DMA((n, 7)),
                        pltpu.SemaphoreType.DMA((n,))]

    def _ctx(self, ins, outs, sems):
        send_sems, recv_sems, local_sems = sems
        x, y, c = lax.axis_index("x"), lax.axis_index("y"), lax.axis_index("c")
        chips = [(1 - x, y), (x, 1 - y), (1 - x, 1 - y)]

        def src(k):
            lay = self.items[k][1]
            return ins[k] if lay is None else ins[k].at[lay]

        def copy(k, s, idx, to, from_src=False):
            return pltpu.make_async_remote_copy(
                src_ref=src(k) if from_src else outs[k].at[idx], dst_ref=outs[k].at[idx],
                send_sem=send_sems.at[k, s], recv_sem=recv_sems.at[k, s],
                device_id=to, device_id_type=MESH)

        def local(k):
            return pltpu.make_async_copy(src(k), outs[k].at[_dev_index(x, y, c)], local_sems.at[k])

        return x, y, c, chips, copy, local

    def start(self, ins, outs, sems, cond):
        n = len(self.items)

        @_when(cond)
        def _():
            x, y, c, chips, copy, local = self._ctx(ins, outs, sems)
            me = _dev_index(x, y, c)
            for k in range(n):
                local(k).start()
                copy(k, 0, me, (x, y, 1 - c), from_src=True).start()
                for j, chip in enumerate(chips):
                    copy(k, 1 + j, me, (chip[0], chip[1], c), from_src=True).start()

    def forward(self, ins, outs, sems, cond):
        n = len(self.items)

        @_when(cond)
        def _():
            x, y, c, chips, copy, local = self._ctx(ins, outs, sems)
            for j, chip in enumerate(chips):
                idx = _dev_index(chip[0], chip[1], c)
                for k in range(n):
                    copy(k, 1 + j, idx, (x, y, c)).wait_recv()
                    copy(k, 4 + j, idx, (x, y, 1 - c)).start()

    def finish(self, ins, outs, sems, cond):
        n = len(self.items)

        @_when(cond)
        def _():
            x, y, c, chips, copy, local = self._ctx(ins, outs, sems)
            me = _dev_index(x, y, c)
            for k in range(n):
                copy(k, 0, _dev_index(x, y, 1 - c), (x, y, c)).wait_recv()
            for j, chip in enumerate(chips):
                idx = _dev_index(chip[0], chip[1], 1 - c)
                for k in range(n):
                    copy(k, 4 + j, idx, (x, y, c)).wait_recv()
            for k in range(n):
                for s in range(4):
                    copy(k, s, me, (x, y, c), from_src=True).wait_send()
                for j, chip in enumerate(chips):
                    copy(k, 4 + j, _dev_index(chip[0], chip[1], c), (x, y, c)).wait_send()
                local(k).wait()


_PEER_FLIPS = [(0, 0, 1), (1, 0, 0), (0, 1, 0), (1, 1, 0), (1, 0, 1), (0, 1, 1), (1, 1, 1)]


class _Scatter:
    def __init__(self, items):
        self.items = items
        self.args = [a for a, _ in items]
        n = len(items)
        self.out_shape = [jax.ShapeDtypeStruct((N_DEV,) + (a.shape if whole else a.shape[1:]), a.dtype)
                          for a, whole in items]
        self.scratch = [pltpu.SemaphoreType.DMA((n, 7)), pltpu.SemaphoreType.DMA((n, 7)),
                        pltpu.SemaphoreType.DMA((n,))]

    def _ctx(self, ins, outs, sems):
        send_sems, recv_sems, local_sems = sems
        x, y, c = lax.axis_index("x"), lax.axis_index("y"), lax.axis_index("c")
        me = _dev_index(x, y, c)

        def src(k, dest_idx):
            return ins[k] if self.items[k][1] else ins[k].at[dest_idx]

        def send(k, s):
            fx, fy, fc = _PEER_FLIPS[s]
            px, py, pc = x ^ fx, y ^ fy, c ^ fc
            return pltpu.make_async_remote_copy(
                src_ref=src(k, _dev_index(px, py, pc)), dst_ref=outs[k].at[me],
                send_sem=send_sems.at[k, s], recv_sem=recv_sems.at[k, s],
                device_id=(px, py, pc), device_id_type=MESH)

        def recv(k, s):
            fx, fy, fc = _PEER_FLIPS[s]
            pidx = _dev_index(x ^ fx, y ^ fy, c ^ fc)
            return pltpu.make_async_remote_copy(
                src_ref=outs[k].at[pidx], dst_ref=outs[k].at[pidx],
                send_sem=send_sems.at[k, s], recv_sem=recv_sems.at[k, s],
                device_id=(x, y, c), device_id_type=MESH)

        def local(k):
            return pltpu.make_async_copy(src(k, me), outs[k].at[me], local_sems.at[k])

        return send, recv, local

    def start(self, ins, outs, sems, cond):
        n = len(self.items)

        @_when(cond)
        def _():
            send, recv, local = self._ctx(ins, outs, sems)
            for k in range(n):
                local(k).start()
            for s in range(7):
                for k in range(n):
                    send(k, s).start()

    def forward(self, ins, outs, sems, cond):
        pass

    def finish(self, ins, outs, sems, cond):
        n = len(self.items)

        @_when(cond)
        def _():
            send, recv, local = self._ctx(ins, outs, sems)
            for s in range(7):
                for k in range(n):
                    recv(k, s).wait_recv()
            for s in range(7):
                for k in range(n):
                    send(k, s).wait_send()
            for k in range(n):
                local(k).wait()


def _call(body, *, name, grid, in_specs, out_specs, out_shape, args, scratch=(), comm=None):
    n_hi, n_ho, n_hs = len(args), len(out_shape), len(scratch)
    c_args = list(comm.args) if comm else []
    c_out = list(comm.out_shape) if comm else []
    c_scr = list(comm.scratch) if comm else []
    phases = _phases(grid)

    def kern(*refs):
        cuts = [n_hi, len(c_args), n_ho, len(c_out), n_hs, len(c_scr)]
        parts, pos = [], 0
        for n in cuts:
            parts.append(refs[pos:pos + n])
            pos += n
        hi, ci, ho, co, hs, cs = parts
        if comm:
            first, mid, last = phases()
            comm.start(ci, co, cs, first)
            comm.forward(ci, co, cs, mid)
        body(*hi, *ho, *hs)
        if comm:
            comm.finish(ci, co, cs, last)

    sem = ("arbitrary",) * len(grid) if grid else None
    return pl.pallas_call(
        kern, name=name, grid=grid,
        in_specs=list(in_specs) + [ANY] * len(c_args),
        out_specs=list(out_specs) + [ANY] * len(c_out),
        out_shape=list(out_shape) + c_out,
        scratch_shapes=list(scratch) + c_scr,
        compiler_params=pltpu.CompilerParams(dimension_semantics=sem, vmem_limit_bytes=VMEM_LIMIT),
    )(*args, *c_args)


def _comm_only(name, comm):
    return _call(lambda: None, name=name, grid=(), in_specs=[], out_specs=[], out_shape=[], args=[], comm=comm)


def _cast_bf16(x, name):
    shape = x.shape
    x2 = x.reshape(-1, shape[-1])
    rows, cols = x2.shape
    tr = _row_tile(rows)

    def body(x_ref, o_ref):
        o_ref[...] = x_ref[...].astype(BF16)

    blk = pl.BlockSpec((tr, cols), lambda i: (i, 0))
    out, = _call(body, name=name, grid=(rows // tr,), in_specs=[blk], out_specs=[blk],
                 out_shape=[jax.ShapeDtypeStruct((rows, cols), BF16)], args=[x2])
    return out.reshape(shape)


def _norm_cast(x, g3, l, tm):
    s = x.shape[0]

    def body(x_ref, g_ref, o_ref):
        v = x_ref[...]
        o_ref[...] = (v * _rstd(v) * g_ref[...]).astype(BF16)

    row = pl.BlockSpec((tm, D_MODEL), lambda i: (i, 0))
    out, = _call(body, name="norm_cast", grid=(s // tm,),
                 in_specs=[row, pl.BlockSpec((None, 1, D_MODEL), lambda i: (l, 0, 0))], out_specs=[row],
                 out_shape=[jax.ShapeDtypeStruct((s, D_MODEL), BF16)], args=[x, g3])
    return out


def _in_proj(h, win, s, tq, part):
    pad = part
    dtype = BF16 if part else F32

    def body(a_ref, b_ref, o_ref):
        def compute():
            a = a_ref[...]
            o_ref[:, 0:PROJ_SHARD] = _dot(a, b_ref[0]).astype(dtype)
            o_ref[:, PROJ_SHARD:2 * PROJ_SHARD] = _dot(a, b_ref[1]).astype(dtype)

        if pad:
            i = pl.program_id(0)

            @pl.when(i == 0)
            def _():
                o_ref[...] = jnp.zeros(o_ref.shape, dtype)

            pl.when(i > 0)(compute)
        else:
            compute()

    out, = _call(
        body, name="in_proj_qkv" if part else "in_proj_conv", grid=(s // tq + pad, 2),
        in_specs=[pl.BlockSpec((tq, D_MODEL), lambda i, j: (jnp.maximum(i - pad, 0), 0)),
                  pl.BlockSpec((2, D_MODEL, PROJ_SHARD), lambda i, j: (2 * part + j, 0, 0))],
        out_specs=[pl.BlockSpec((tq, 2 * PROJ_SHARD), lambda i, j: (i, j))],
        out_shape=[jax.ShapeDtypeStruct((s + pad * tq, PROJ_WIDTH // 2), dtype)], args=[h, win])
    return out


def _conv_fwd(pc, wc, g3, l, s, tr):
    hb = tr // 8

    def body(pc_ref, prev_ref, wc_ref, g_ref, o_ref):
        i = pl.program_id(0)
        gmat = _group_matrix()
        for j in range(CONV_WIDTH // 128):
            c0, c1, c2 = 128 * j, CONV_WIDTH + 128 * j, 2 * CONV_WIDTH + 128 * j
            hc = pc_ref[:, c0:c0 + 128]
            bg = pc_ref[:, c1:c1 + 128]
            cg = pc_ref[:, c2:c2 + 128]
            u_prev = jnp.where(i > 0, prev_ref[:, c2:c2 + 128] * prev_ref[:, c0:c0 + 128], 0.0)
            u = cg * hc
            full = jnp.concatenate([u_prev, u], axis=0)
            u1 = pltpu.roll(full, 1, 0)[8:]
            u2 = pltpu.roll(full, 2, 0)[8:]
            out = (u2 * wc_ref[0:1, c0:c0 + 128] + u1 * wc_ref[1:2, c0:c0 + 128]
                   + u * wc_ref[2:3, c0:c0 + 128])
            yc = bg * out
            r = lax.rsqrt(_group_mean(yc * yc, gmat) + EPS)
            o_ref[:, c0:c0 + 128] = (yc * r * g_ref[:, c0:c0 + 128]).astype(BF16)

    out, = _call(
        body, name="conv_fwd", grid=(s // tr,),
        in_specs=[pl.BlockSpec((tr, 3 * CONV_WIDTH), lambda i: (i, 0)),
                  pl.BlockSpec((8, 3 * CONV_WIDTH), lambda i: (jnp.maximum(i * hb - 1, 0), 0)),
                  pl.BlockSpec((None, 8, CONV_WIDTH), lambda i: (l, 0, 0)),
                  pl.BlockSpec((None, 1, CONV_WIDTH), lambda i: (l, 0, 0))],
        out_specs=[pl.BlockSpec((tr, CONV_WIDTH), lambda i: (i, 0))],
        out_shape=[jax.ShapeDtypeStruct((s, CONV_WIDTH), BF16)], args=[pc, pc, wc, g3])
    return out


def _toeplitz_source():
    r_i = lax.broadcasted_iota(jnp.int32, (REL_PAD, TOEP), 0)
    m_i = lax.broadcasted_iota(jnp.int32, (REL_PAD, TOEP), 1)
    idx = jnp.clip((K_BAND - 1) - m_i, -REL_CLIP, REL_CLIP) + REL_CLIP
    return jnp.where(r_i == idx, 1.0, 0.0).astype(BF16)


def _bias_build(rbp):
    n_layers = rbp.shape[0]

    def body(rb_ref, o_ref, t_ref):
        pmat = _toeplitz_source()
        hi, mid, lo = _split3(rb_ref[...])
        t_ref[...] = _dot(hi, pmat) + _dot(mid, pmat) + _dot(lo, pmat)
        shift = (Q_BLOCK - 1) - lax.broadcasted_iota(jnp.int32, (Q_BLOCK, TOEP), 0)
        kk = lax.broadcasted_iota(jnp.int32, (Q_BLOCK, K_BAND), 1) >> 6
        qq = lax.broadcasted_iota(jnp.int32, (Q_BLOCK, K_BAND), 0) >> 6
        dchunk = kk - qq
        in_band = jnp.where(dchunk >= 0, jnp.where(dchunk <= N_LEFT_CHUNKS, 1, 0), 0) == 1
        for h in range(N_HEADS):
            b = jnp.broadcast_to(t_ref[pl.ds(h, 1), :], (Q_BLOCK, TOEP))
            for bit in range(8):
                rolled = pltpu.roll(b, TOEP - (1 << bit), 1)
                b = jnp.where(((shift >> bit) & 1) == 1, rolled, b)
            o_ref[h] = jnp.where(in_band, b[:, :K_BAND], NEG_INF)

    out, = _call(
        body, name="bias_build", grid=(n_layers,),
        in_specs=[pl.BlockSpec((None, N_HEADS, REL_PAD), lambda l: (l, 0, 0))],
        out_specs=[pl.BlockSpec((None, N_HEADS, Q_BLOCK, K_BAND), lambda l: (l, 0, 0, 0))],
        out_shape=[jax.ShapeDtypeStruct((n_layers, N_HEADS, Q_BLOCK, K_BAND), F32)],
        scratch=[pltpu.VMEM((N_HEADS, TOEP), F32)], args=[rbp])
    return out


def _bias_bwd(ds_sum):
    n_layers = ds_sum.shape[0]

    def body(ds_ref, o_ref, t_ref):
        pmat = _toeplitz_source()
        shift = (Q_BLOCK - 1) - lax.broadcasted_iota(jnp.int32, (Q_BLOCK, TOEP), 0)
        for h in range(N_HEADS):
            d = jnp.concatenate([ds_ref[h], jnp.zeros((Q_BLOCK, TOEP - K_BAND), F32)], axis=1)
            for bit in range(8):
                rolled = pltpu.roll(d, 1 << bit, 1)
                d = jnp.where(((shift >> bit) & 1) == 1, rolled, d)
            t_ref[pl.ds(h, 1), :] = jnp.sum(d, axis=0, keepdims=True)
        hi, mid, lo = _split3(t_ref[...])
        o_ref[...] = _dot_nt(hi, pmat) + _dot_nt(mid, pmat) + _dot_nt(lo, pmat)

    out, = _call(
        body, name="bias_bwd", grid=(n_layers,),
        in_specs=[pl.BlockSpec((None, N_HEADS, Q_BLOCK, K_BAND), lambda l: (l, 0, 0, 0))],
        out_specs=[pl.BlockSpec((None, N_HEADS, REL_PAD), lambda l: (l, 0, 0))],
        out_shape=[jax.ShapeDtypeStruct((n_layers, N_HEADS, REL_PAD), F32)],
        scratch=[pltpu.VMEM((N_HEADS, TOEP), F32)], args=[ds_sum])
    return out


def _attn_fwd(qkvp, biasm, g3, l, s, pad, comm=None):
    nb = s // Q_BLOCK
    qb0 = pad // Q_BLOCK
    scale = HEAD_DIM ** -0.5

    def body(q_ref, k_ref, v_ref, b_ref, g_ref, o_ref, lse_ref, yn_ref):
        blk = pl.program_id(1)
        koff = pl.multiple_of(blk * Q_BLOCK + (pad - LEFT), Q_BLOCK)
        q = q_ref[...]
        kb = k_ref[pl.ds(koff, K_BAND), :]
        vb = v_ref[pl.ds(koff, K_BAND), :]
        lane = lax.broadcasted_iota(jnp.int32, (1, 128), 1)
        kpos = lax.broadcasted_iota(jnp.int32, (Q_BLOCK, K_BAND), 1) + (blk * Q_BLOCK - LEFT)
        kvalid = kpos >= 0
        outs, lses = [], []
        for hh in range(2):
            in_head = (lane >> 6) == hh
            qm = jnp.where(in_head, q, jnp.zeros_like(q)) * jnp.asarray(scale, BF16)
            sc = _dot_nt(qm, kb) + b_ref[hh]
            sc = jnp.where(kvalid, sc, NEG_INF)
            m = jnp.max(sc, axis=1, keepdims=True)
            e = jnp.exp(sc - m)
            den = jnp.sum(e, axis=1, keepdims=True)
            outs.append(_dot(e.astype(BF16), vb) * (1.0 / den))
            lses.append(m + jnp.log(den))
        first = lane < HEAD_DIM
        o = jnp.where(first, outs[0], outs[1])
        o_ref[...] = o
        lse_ref[...] = jnp.where(first, lses[0], lses[1])
        r = lax.rsqrt(_group_mean(o * o, _group_matrix()) + EPS)
        yn_ref[...] = (o * r * g_ref[...]).astype(BF16)

    blk_out = pl.BlockSpec((Q_BLOCK, 128), lambda p, b: (b, p))
    return _call(
        body, name="attn_fwd", grid=(ATTN_WIDTH // 128, nb),
        in_specs=[pl.BlockSpec((Q_BLOCK, 128), lambda p, b: (qb0 + b, p)),
                  pl.BlockSpec((s + pad, 128), lambda p, b: (0, 4 + p)),
                  pl.BlockSpec((s + pad, 128), lambda p, b: (0, 8 + p)),
                  pl.BlockSpec((None, 2, Q_BLOCK, K_BAND), lambda p, b: (l, p, 0, 0)),
                  pl.BlockSpec((None, 1, 128), lambda p, b: (l, 0, p))],
        out_specs=[blk_out, blk_out, blk_out],
        out_shape=[jax.ShapeDtypeStruct((s, ATTN_WIDTH), F32),
                   jax.ShapeDtypeStruct((s, ATTN_WIDTH), F32),
                   jax.ShapeDtypeStruct((s, ATTN_WIDTH), BF16)],
        args=[qkvp, qkvp, qkvp, biasm, g3], comm=comm)


def _out_proj_fwd(ync, yna, wout, x, g_post3, g_next3, l, s, tm):
    half = D_MODEL // 2

    def body(a1_ref, a2_ref, w_ref, x_ref, gp_ref, gn_ref, z_ref, xm_ref, h_ref):
        z = _dot(a1_ref[...], w_ref[0:half, :]) + _dot(a2_ref[...], w_ref[half:D_MODEL, :])
        z_ref[...] = z
        xm = x_ref[...] + z * _rstd(z) * gp_ref[...]
        xm_ref[...] = xm
        h_ref[...] = (xm * _rstd(xm) * gn_ref[...]).astype(BF16)

    row = pl.BlockSpec((tm, D_MODEL), lambda i: (i, 0))
    gain = pl.BlockSpec((None, 1, D_MODEL), lambda i: (l, 0, 0))
    return _call(
        body, name="out_proj_fwd", grid=(s // tm,),
        in_specs=[pl.BlockSpec((tm, half), lambda i: (i, 0)), pl.BlockSpec((tm, half), lambda i: (i, 0)),
                  pl.BlockSpec((D_MODEL, D_MODEL), lambda i: (0, 0)), row, gain, gain],
        out_specs=[row, row, row],
        out_shape=[jax.ShapeDtypeStruct((s, D_MODEL), F32), jax.ShapeDtypeStruct((s, D_MODEL), F32),
                   jax.ShapeDtypeStruct((s, D_MODEL), BF16)],
        args=[ync, yna, wout, x, g_post3, g_next3])


def _ffn_in_fwd(h2, wfin, s, tm, comm=None):
    def body(h_ref, wg_ref, wu_ref, gate_ref, up_ref, act_ref):
        h = h_ref[...]
        gate = _dot(h, wg_ref[...])
        up = _dot(h, wu_ref[...])
        gate_ref[...] = gate.astype(BF16)
        up_ref[...] = up.astype(BF16)
        act_ref[...] = (gate * (1.0 / (1.0 + jnp.exp(-gate))) * up).astype(BF16)

    blk = pl.BlockSpec((None, tm, FF_SHARD), lambda b, i: (b, i, 0))
    shp = jax.ShapeDtypeStruct((4, s, FF_SHARD), BF16)
    return _call(
        body, name="ffn_in_fwd", grid=(4, s // tm),
        in_specs=[pl.BlockSpec((tm, D_MODEL), lambda b, i: (i, 0)),
                  pl.BlockSpec((None, D_MODEL, FF_SHARD), lambda b, i: (b, 0, 0)),
                  pl.BlockSpec((None, D_MODEL, FF_SHARD), lambda b, i: (4 + b, 0, 0))],
        out_specs=[blk, blk, blk], out_shape=[shp, shp, shp],
        args=[h2, wfin, wfin], comm=comm)


def _ffn_out_fwd(act, wfo4, xm, g_post3, g_next3, l, l_next, s, tm, comm=None):
    def body(a_ref, w_ref, x_ref, gp_ref, gn_ref, f_ref, xo_ref, h_ref):
        f = _dot(a_ref[0], w_ref[0])
        for b in range(1, 4):
            f = f + _dot(a_ref[b], w_ref[b])
        f_ref[...] = f
        xo = x_ref[...] + f * _rstd(f) * gp_ref[...]
        xo_ref[...] = xo
        h_ref[...] = (xo * _rstd(xo) * gn_ref[...]).astype(BF16)

    row = pl.BlockSpec((tm, D_MODEL), lambda i: (i, 0))
    return _call(
        body, name="ffn_out_fwd", grid=(s // tm,),
        in_specs=[pl.BlockSpec((4, tm, FF_SHARD), lambda i: (0, i, 0)),
                  pl.BlockSpec((4, FF_SHARD, D_MODEL), lambda i: (0, 0, 0)), row,
                  pl.BlockSpec((None, 1, D_MODEL), lambda i: (l, 0, 0)),
                  pl.BlockSpec((None, 1, D_MODEL), lambda i: (l_next, 0, 0))],
        out_specs=[row, row, row],
        out_shape=[jax.ShapeDtypeStruct((s, D_MODEL), F32), jax.ShapeDtypeStruct((s, D_MODEL), F32),
                   jax.ShapeDtypeStruct((s, D_MODEL), BF16)],
        args=[act, wfo4, xm, g_post3, g_next3], comm=comm)


def _loss_grad(xf, target, s, tm):
    def body(x_ref, t_ref, dx_ref, sq_ref):
        i = pl.program_id(0)
        err = x_ref[...] - t_ref[...]
        dx_ref[...] = err * (1.0 / D_MODEL)
        cs = jnp.sum(err * err, axis=0, keepdims=True)
        part = cs[:, 0:128]
        for k in range(1, D_MODEL // 128):
            part = part + cs[:, 128 * k:128 * (k + 1)]

        @pl.when(i == 0)
        def _():
            sq_ref[...] = jnp.zeros(sq_ref.shape, F32)

        sq_ref[0:1, :] += part

    row = pl.BlockSpec((tm, D_MODEL), lambda i: (i, 0))
    return _call(
        body, name="loss_grad", grid=(s // tm,), in_specs=[row, row],
        out_specs=[row, pl.BlockSpec((8, 128), lambda i: (0, 0))],
        out_shape=[jax.ShapeDtypeStruct((s, D_MODEL), F32), jax.ShapeDtypeStruct((8, 128), F32)],
        args=[xf, target])


def _norm_bwd_rows(v, g, dy):
    r = _rstd(v)
    vn = v * r
    gd = dy * g
    dv = r * (gd - vn * jnp.mean(vn * gd, axis=-1, keepdims=True))
    return dv, dy * vn


def _accum_cols(ref, val, first):
    @pl.when(first)
    def _():
        ref[...] = jnp.zeros(ref.shape, F32)

    ref[0:1, :] += jnp.sum(val, axis=0, keepdims=True)


def _norm_bwd(z, g3, dy, l, s, tm):
    def body(z_ref, g_ref, dy_ref, dz_ref, dg_ref):
        dz, dyn = _norm_bwd_rows(z_ref[...], g_ref[...], dy_ref[...])
        dz_ref[...] = dz.astype(BF16)
        _accum_cols(dg_ref, dyn, pl.program_id(0) == 0)

    row = pl.BlockSpec((tm, D_MODEL), lambda i: (i, 0))
    return _call(
        body, name="norm_bwd", grid=(s // tm,),
        in_specs=[row, pl.BlockSpec((None, 1, D_MODEL), lambda i: (l, 0, 0)), row],
        out_specs=[row, pl.BlockSpec((8, D_MODEL), lambda i: (0, 0))],
        out_shape=[jax.ShapeDtypeStruct((s, D_MODEL), BF16), jax.ShapeDtypeStruct((8, D_MODEL), F32)],
        args=[z, g3, dy])


def _ffn_out_bwd(df, wfo4, gate, up, s, tm, comm=None):
    def body(df_ref, w_ref, gate_ref, up_ref, dg_ref, du_ref):
        da = _dot_nt(df_ref[...], w_ref[...])
        g = gate_ref[...].astype(F32)
        u = up_ref[...].astype(F32)
        sg = 1.0 / (1.0 + jnp.exp(-g))
        dg_ref[...] = (da * u * (sg * (1.0 + g * (1.0 - sg)))).astype(BF16)
        du_ref[...] = (da * (g * sg)).astype(BF16)

    blk = pl.BlockSpec((None, tm, FF_SHARD), lambda b, i: (b, i, 0))
    shp = jax.ShapeDtypeStruct((4, s, FF_SHARD), BF16)
    return _call(
        body, name="ffn_out_bwd", grid=(4, s // tm),
        in_specs=[pl.BlockSpec((tm, D_MODEL), lambda b, i: (i, 0)),
                  pl.BlockSpec((None, FF_SHARD, D_MODEL), lambda b, i: (b, 0, 0)), blk, blk],
        out_specs=[blk, blk], out_shape=[shp, shp],
        args=[df, wfo4, gate, up], comm=comm)


def _dw_ffn_out(act, df, s, tm):
    nm = s // tm

    def body(a_ref, b_ref, o_ref, acc_ref):
        m = pl.program_id(1)

        @pl.when(m == 0)
        def _():
            acc_ref[...] = jnp.zeros(acc_ref.shape, F32)

        acc_ref[...] += _dot_tn(a_ref[...], b_ref[...])

        @pl.when(m == nm - 1)
        def _():
            o_ref[...] = acc_ref[...].astype(BF16)

    out, = _call(
        body, name="dw_ffn_out", grid=(4, nm),
        in_specs=[pl.BlockSpec((None, tm, FF_SHARD), lambda n, m: (n, m, 0)),
                  pl.BlockSpec((tm, D_MODEL), lambda n, m: (m, 0))],
        out_specs=[pl.BlockSpec((FF_SHARD, D_MODEL), lambda n, m: (n, 0))],
        out_shape=[jax.ShapeDtypeStruct((D_FF, D_MODEL), BF16)],
        scratch=[pltpu.VMEM((FF_SHARD, D_MODEL), F32)], args=[act, df])
    return out


def _dw_ffn_in(h2, dgate, dup, s, tm):
    nm = s // tm

    def body(a_ref, b1_ref, b2_ref, o_ref, acc_ref):
        n, m = pl.program_id(0), pl.program_id(1)

        @pl.when(m == 0)
        def _():
            acc_ref[...] = jnp.zeros(acc_ref.shape, F32)

        @pl.when(n < 4)
        def _():
            acc_ref[...] += _dot_tn(a_ref[...], b1_ref[...])

        @pl.when(n >= 4)
        def _():
            acc_ref[...] += _dot_tn(a_ref[...], b2_ref[...])

        @pl.when(m == nm - 1)
        def _():
            o_ref[...] = acc_ref[...].astype(BF16)

    out, = _call(
        body, name="dw_ffn_in", grid=(N_DEV, nm),
        in_specs=[pl.BlockSpec((tm, D_MODEL), lambda n, m: (m, 0)),
                  pl.BlockSpec((None, tm, FF_SHARD), lambda n, m: (jnp.minimum(n, 3), m, 0)),
                  pl.BlockSpec((None, tm, FF_SHARD), lambda n, m: (jnp.maximum(n - 4, 0), m, 0))],
        out_specs=[pl.BlockSpec((None, D_MODEL, FF_SHARD), lambda n, m: (n, 0, 0))],
        out_shape=[jax.ShapeDtypeStruct((N_DEV, D_MODEL, FF_SHARD), BF16)],
        scratch=[pltpu.VMEM((D_MODEL, FF_SHARD), F32)], args=[h2, dgate, dup])
    return out


def _ffn_in_bwd(dgate, dup, wfin, xm, g_pre3, dres, z, g_post3, l, s, tm, comm=None):
    def body(d1_ref, d2_ref, w_ref, xm_ref, gp_ref, dres_ref, z_ref, gq_ref,
             dxm_ref, dz_ref, dgp_ref, dgq_ref, acc_ref):
        i, j = pl.program_id(0), pl.program_id(1)

        @pl.when(j == 0)
        def _():
            acc_ref[...] = jnp.zeros(acc_ref.shape, F32)

        @pl.when(j < 4)
        def _():
            acc_ref[...] += _dot_nt(d1_ref[...], w_ref[...])

        @pl.when(j >= 4)
        def _():
            acc_ref[...] += _dot_nt(d2_ref[...], w_ref[...])

        @pl.when(j == N_DEV - 1)
        def _():
            dh = acc_ref[...]
            dx, dyn = _norm_bwd_rows(xm_ref[...], gp_ref[...], dh)
            dxm = dres_ref[...] + dx
            dxm_ref[...] = dxm
            _accum_cols(dgp_ref, dyn, i == 0)
            dz, dyn2 = _norm_bwd_rows(z_ref[...], gq_ref[...], dxm)
            dz_ref[...] = dz.astype(BF16)
            _accum_cols(dgq_ref, dyn2, i == 0)

    row = pl.BlockSpec((tm, D_MODEL), lambda i, j: (i, 0))
    gain = pl.BlockSpec((None, 1, D_MODEL), lambda i, j: (l, 0, 0))
    dgs = pl.BlockSpec((8, D_MODEL), lambda i, j: (0, 0))
    return _call(
        body, name="ffn_in_bwd", grid=(s // tm, N_DEV),
        in_specs=[pl.BlockSpec((None, tm, FF_SHARD), lambda i, j: (jnp.minimum(j, 3), i, 0)),
                  pl.BlockSpec((None, tm, FF_SHARD), lambda i, j: (jnp.maximum(j - 4, 0), i, 0)),
                  pl.BlockSpec((None, D_MODEL, FF_SHARD), lambda i, j: (j, 0, 0)),
                  row, gain, row, row, gain],
        out_specs=[row, row, dgs, dgs],
        out_shape=[jax.ShapeDtypeStruct((s, D_MODEL), F32), jax.ShapeDtypeStruct((s, D_MODEL), BF16),
                   jax.ShapeDtypeStruct((8, D_MODEL), F32), jax.ShapeDtypeStruct((8, D_MODEL), F32)],
        scratch=[pltpu.VMEM((tm, D_MODEL), F32)],
        args=[dgate, dup, wfin, xm, g_pre3, dres, z, g_post3], comm=comm)


def _dw_out(ync, yna, dz, s, tm):
    nm = s // tm
    half = D_MODEL // 2

    def body(a1_ref, a2_ref, b_ref, o_ref, acc_ref):
        n, m = pl.program_id(0), pl.program_id(1)

        @pl.when(m == 0)
        def _():
            acc_ref[...] = jnp.zeros(acc_ref.shape, F32)

        @pl.when(n == 0)
        def _():
            acc_ref[...] += _dot_tn(a1_ref[...], b_ref[...])

        @pl.when(n == 1)
        def _():
            acc_ref[...] += _dot_tn(a2_ref[...], b_ref[...])

        @pl.when(m == nm - 1)
        def _():
            o_ref[...] = acc_ref[...].astype(BF16)

    out, = _call(
        body, name="dw_out", grid=(2, nm),
        in_specs=[pl.BlockSpec((tm, half), lambda n, m: (m, 0)), pl.BlockSpec((tm, half), lambda n, m: (m, 0)),
                  pl.BlockSpec((tm, D_MODEL), lambda n, m: (m, 0))],
        out_specs=[pl.BlockSpec((half, D_MODEL), lambda n, m: (n, 0))],
        out_shape=[jax.ShapeDtypeStruct((D_MODEL, D_MODEL), BF16)],
        scratch=[pltpu.VMEM((half, D_MODEL), F32)], args=[ync, yna, dz])
    return out


def _out_proj_bwd(dz, wout, o, g3, l, s, tm):
    def body(dz_ref, w_ref, o_ref, g_ref, dyc_ref, do_ref, dg_ref):
        dy = _dot_nt(dz_ref[...], w_ref[...])
        dyc_ref[...] = dy[:, 0:CONV_WIDTH]
        gmat = _group_matrix()

        @pl.when(pl.program_id(0) == 0)
        def _():
            dg_ref[...] = jnp.zeros(dg_ref.shape, F32)

        for j in range(ATTN_WIDTH // 128):
            c0 = 128 * j
            ov = o_ref[:, c0:c0 + 128]
            dyn = dy[:, CONV_WIDTH + c0:CONV_WIDTH + c0 + 128]
            r = lax.rsqrt(_group_mean(ov * ov, gmat) + EPS)
            on = ov * r
            gd = dyn * g_ref[:, c0:c0 + 128]
            do_ref[:, c0:c0 + 128] = r * (gd - on * _group_mean(on * gd, gmat))
            dg_ref[0:1, c0:c0 + 128] += jnp.sum(dyn * on, axis=0, keepdims=True)

    halfrow = pl.BlockSpec((tm, ATTN_WIDTH), lambda i: (i, 0))
    return _call(
        body, name="out_proj_bwd", grid=(s // tm,),
        in_specs=[pl.BlockSpec((tm, D_MODEL), lambda i: (i, 0)),
                  pl.BlockSpec((D_MODEL, D_MODEL), lambda i: (0, 0)), halfrow,
                  pl.BlockSpec((None, 1, ATTN_WIDTH), lambda i: (l, 0, 0))],
        out_specs=[halfrow, halfrow, pl.BlockSpec((8, ATTN_WIDTH), lambda i: (0, 0))],
        out_shape=[jax.ShapeDtypeStruct((s, CONV_WIDTH), F32), jax.ShapeDtypeStruct((s, ATTN_WIDTH), F32),
                   jax.ShapeDtypeStruct((8, ATTN_WIDTH), F32)],
        args=[dz, wout, o, g3])


def _conv_bwd(pc, dyc, wc, g3, l, s, tr):
    hb = tr // 8
    nt = s // tr
    ext = tr + 16
    last_hb = s // 8 - 1

    def body(pc_ref, prev_ref, next_ref, dy_ref, dyn_ref, wc_ref, g_ref, dpc_ref, dw_ref, dg_ref):
        i = pl.program_id(0)
        gmat = _group_matrix()
        row = lax.broadcasted_iota(jnp.int32, (ext, 128), 0) + (i * tr - 8)
        inside = jnp.where(row >= 0, jnp.where(row < s, 1, 0), 0) == 1

        @pl.when(i == 0)
        def _():
            dw_ref[...] = jnp.zeros(dw_ref.shape, F32)
            dg_ref[...] = jnp.zeros(dg_ref.shape, F32)

        def extend(ref_prev, ref_mid, ref_next, c):
            parts = [ref_prev[:, c:c + 128] if ref_prev is not None else jnp.zeros((8, 128), F32),
                     ref_mid[:, c:c + 128], ref_next[:, c:c + 128]]
            return jnp.concatenate(parts, axis=0)

        for j in range(CONV_WIDTH // 128):
            c0, c1, c2 = 128 * j, CONV_WIDTH + 128 * j, 2 * CONV_WIDTH + 128 * j
            hc = extend(prev_ref, pc_ref, next_ref, c0)
            bg = extend(prev_ref, pc_ref, next_ref, c1)
            cg = extend(prev_ref, pc_ref, next_ref, c2)
            dyn = extend(None, dy_ref, dyn_ref, c0)
            w0, w1, w2 = (wc_ref[0:1, c0:c0 + 128], wc_ref[1:2, c0:c0 + 128], wc_ref[2:3, c0:c0 + 128])
            gain = g_ref[:, c0:c0 + 128]
            u = jnp.where(inside, cg * hc, 0.0)
            u1 = pltpu.roll(u, 1, 0)
            u2 = pltpu.roll(u, 2, 0)
            out = u2 * w0 + u1 * w1 + u * w2
            yc = bg * out
            r = lax.rsqrt(_group_mean(yc * yc, gmat) + EPS)
            ycn = yc * r
            gd = dyn * gain
            dyc = r * (gd - ycn * _group_mean(ycn * gd, gmat))
            dout = jnp.where(inside, dyc * bg, 0.0)
            du = dout * w2 + pltpu.roll(dout, ext - 1, 0) * w1 + pltpu.roll(dout, ext - 2, 0) * w0
            sl = slice(8, 8 + tr)
            dpc_ref[:, c0:c0 + 128] = (du[sl] * cg[sl]).astype(BF16)
            dpc_ref[:, c1:c1 + 128] = (dyc[sl] * out[sl]).astype(BF16)
            dpc_ref[:, c2:c2 + 128] = (du[sl] * hc[sl]).astype(BF16)
            dw_ref[0:1, c0:c0 + 128] += jnp.sum(dout[sl] * u2[sl], axis=0, keepdims=True)
            dw_ref[1:2, c0:c0 + 128] += jnp.sum(dout[sl] * u1[sl], axis=0, keepdims=True)
            dw_ref[2:3, c0:c0 + 128] += jnp.sum(dout[sl] * u[sl], axis=0, keepdims=True)
            dg_ref[0:1, c0:c0 + 128] += jnp.sum(dyn[sl] * ycn[sl], axis=0, keepdims=True)

    wide = 3 * CONV_WIDTH
    return _call(
        body, name="conv_bwd", grid=(nt,),
        in_specs=[pl.BlockSpec((tr, wide), lambda i: (i, 0)),
                  pl.BlockSpec((8, wide), lambda i: (jnp.maximum(i * hb - 1, 0), 0)),
                  pl.BlockSpec((8, wide), lambda i: (jnp.minimum((i + 1) * hb, last_hb), 0)),
                  pl.BlockSpec((tr, CONV_WIDTH), lambda i: (i, 0)),
                  pl.BlockSpec((8, CONV_WIDTH), lambda i: (jnp.minimum((i + 1) * hb, last_hb), 0)),
                  pl.BlockSpec((None, 8, CONV_WIDTH), lambda i: (l, 0, 0)),
                  pl.BlockSpec((None, 1, CONV_WIDTH), lambda i: (l, 0, 0))],
        out_specs=[pl.BlockSpec((tr, wide), lambda i: (i, 0)),
                   pl.BlockSpec((8, CONV_WIDTH), lambda i: (0, 0)),
                   pl.BlockSpec((8, CONV_WIDTH), lambda i: (0, 0))],
        out_shape=[jax.ShapeDtypeStruct((s, wide), BF16), jax.ShapeDtypeStruct((8, CONV_WIDTH), F32),
                   jax.ShapeDtypeStruct((8, CONV_WIDTH), F32)],
        args=[pc, pc, pc, dyc, dyc, wc, g3])


def _attn_bwd(qkvp, biasm, o, lse, do, l, s, pad, comm=None):
    nb = s // Q_BLOCK
    qb0 = pad // Q_BLOCK
    scale = HEAD_DIM ** -0.5

    def body(q_ref, k_ref, v_ref, b_ref, o_ref, lse_ref, do_ref,
             dq_ref, dk_ref, dv_ref, ds_ref, dk_acc, dv_acc):
        blk = pl.program_id(1)

        @pl.when(blk == 0)
        def _():
            dk_acc[...] = jnp.zeros(dk_acc.shape, F32)
            dv_acc[...] = jnp.zeros(dv_acc.shape, F32)
            ds_ref[...] = jnp.zeros(ds_ref.shape, F32)

        koff = pl.multiple_of(blk * Q_BLOCK + (pad - LEFT), Q_BLOCK)
        q = q_ref[...]
        kb = k_ref[pl.ds(koff, K_BAND), :]
        vb = v_ref[pl.ds(koff, K_BAND), :]
        ov = o_ref[...]
        dov = do_ref[...]
        lse_v = lse_ref[...]
        lane = lax.broadcasted_iota(jnp.int32, (1, 128), 1)
        kpos = lax.broadcasted_iota(jnp.int32, (Q_BLOCK, K_BAND), 1) + (blk * Q_BLOCK - LEFT)
        kvalid = kpos >= 0
        prod = dov * ov
        dq_parts = []
        dk_new = jnp.zeros((K_BAND, 128), F32)
        dv_new = jnp.zeros((K_BAND, 128), F32)
        for hh in range(2):
            in_head = (lane >> 6) == hh
            qm = jnp.where(in_head, q, jnp.zeros_like(q)) * jnp.asarray(scale, BF16)
            dom = jnp.where(in_head, dov, 0.0).astype(BF16)
            delta = jnp.sum(jnp.where(in_head, prod, 0.0), axis=1, keepdims=True)
            lse_h = lse_v[:, HEAD_DIM * hh:HEAD_DIM * hh + 1]
            sc = _dot_nt(qm, kb) + b_ref[hh]
            sc = jnp.where(kvalid, sc, NEG_INF)
            p = jnp.exp(sc - lse_h)
            dp = _dot_nt(dom, vb)
            ds = p * (dp - delta)
            ds_ref[hh] += ds
            dsb = ds.astype(BF16)
            dq_parts.append(_dot(dsb, kb) * scale)
            dk_new = dk_new + _dot_tn(dsb, qm)
            dv_new = dv_new + _dot_tn(p.astype(BF16), dom)
        dq_ref[...] = jnp.where(lane < HEAD_DIM, dq_parts[0], dq_parts[1]).astype(BF16)
        dk_acc[pl.ds(koff, K_BAND), :] += dk_new
        dv_acc[pl.ds(koff, K_BAND), :] += dv_new

        @pl.when(blk == nb - 1)
        def _():
            dk_ref[...] = dk_acc[pad:pad + s, :].astype(BF16)
            dv_ref[...] = dv_acc[pad:pad + s, :].astype(BF16)

    qblk = pl.BlockSpec((Q_BLOCK, 128), lambda p, b: (b, p))
    col = pl.BlockSpec((s, 128), lambda p, b: (0, p))
    shp = jax.ShapeDtypeStruct((s, ATTN_WIDTH), BF16)
    return _call(
        body, name="attn_bwd", grid=(ATTN_WIDTH // 128, nb),
        in_specs=[pl.BlockSpec((Q_BLOCK, 128), lambda p, b: (qb0 + b, p)),
                  pl.BlockSpec((s + pad, 128), lambda p, b: (0, 4 + p)),
                  pl.BlockSpec((s + pad, 128), lambda p, b: (0, 8 + p)),
                  pl.BlockSpec((None, 2, Q_BLOCK, K_BAND), lambda p, b: (l, p, 0, 0)),
                  qblk, qblk, qblk],
        out_specs=[qblk, col, col, pl.BlockSpec((2, Q_BLOCK, K_BAND), lambda p, b: (p, 0, 0))],
        out_shape=[shp, shp, shp, jax.ShapeDtypeStruct((N_HEADS, Q_BLOCK, K_BAND), F32)],
        scratch=[pltpu.VMEM((s + pad, 128), F32), pltpu.VMEM((s + pad, 128), F32)],
        args=[qkvp, qkvp, qkvp, biasm, o, lse, do], comm=comm)


def _dw_in(h, dproj, s, tm):
    nm = s // tm

    def body(a_ref, b_ref, o_ref, acc_ref):
        m = pl.program_id(1)

        @pl.when(m == 0)
        def _():
            acc_ref[...] = jnp.zeros(acc_ref.shape, F32)

        acc_ref[...] += _dot_tn(a_ref[...], b_ref[...])

        @pl.when(m == nm - 1)
        def _():
            o_ref[0] = acc_ref[:, 0:PROJ_SHARD].astype(BF16)
            o_ref[1] = acc_ref[:, PROJ_SHARD:2 * PROJ_SHARD].astype(BF16)

    out, = _call(
        body, name="dw_in", grid=(4, nm),
        in_specs=[pl.BlockSpec((tm, D_MODEL), lambda n, m: (m, 0)),
                  pl.BlockSpec((tm, 2 * PROJ_SHARD), lambda n, m: (m, n))],
        out_specs=[pl.BlockSpec((2, D_MODEL, PROJ_SHARD), lambda n, m: (n, 0, 0))],
        out_shape=[jax.ShapeDtypeStruct((N_DEV, D_MODEL, PROJ_SHARD), BF16)],
        scratch=[pltpu.VMEM((D_MODEL, 2 * PROJ_SHARD), F32)], args=[h, dproj])
    return out


def _in_proj_bwd(dproj, win, x, g3, dres, l, s, tm, comm=None):
    def body(d_ref, w_ref, x_ref, g_ref, dres_ref, dx_ref, dg_ref, acc_ref):
        i, j = pl.program_id(0), pl.program_id(1)

        @pl.when(j == 0)
        def _():
            acc_ref[...] = jnp.zeros(acc_ref.shape, F32)

        acc_ref[...] += (_dot_nt(d_ref[:, 0:PROJ_SHARD], w_ref[0])
                         + _dot_nt(d_ref[:, PROJ_SHARD:2 * PROJ_SHARD], w_ref[1]))

        @pl.when(j == 3)
        def _():
            dx, dyn = _norm_bwd_rows(x_ref[...], g_ref[...], acc_ref[...])
            dx_ref[...] = dres_ref[...] + dx
            _accum_cols(dg_ref, dyn, i == 0)

    row = pl.BlockSpec((tm, D_MODEL), lambda i, j: (i, 0))
    return _call(
        body, name="in_proj_bwd", grid=(s // tm, 4),
        in_specs=[pl.BlockSpec((tm, 2 * PROJ_SHARD), lambda i, j: (i, j)),
                  pl.BlockSpec((2, D_MODEL, PROJ_SHARD), lambda i, j: (j, 0, 0)),
                  row, pl.BlockSpec((None, 1, D_MODEL), lambda i, j: (l, 0, 0)), row],
        out_specs=[row, pl.BlockSpec((8, D_MODEL), lambda i, j: (0, 0))],
        out_shape=[jax.ShapeDtypeStruct((s, D_MODEL), F32), jax.ShapeDtypeStruct((8, D_MODEL), F32)],
        scratch=[pltpu.VMEM((tm, D_MODEL), F32)],
        args=[dproj, win, x, g3, dres], comm=comm)


def _adamw(name, w, m, v, lands):
    groups, rows, cols = w.shape
    assert len(lands) == groups
    n_part = lands[0].shape[0]
    tr = _row_tile(rows, tuple(c for c in (512, 352, 256, 176, 128, 64, 32, 16, 8) if c * cols <= 256 * 1024))
    c1 = 1.0 - ADAM_B1 ** ADAM_STEP
    c2 = 1.0 - ADAM_B2 ** ADAM_STEP

    def body(w_ref, m_ref, v_ref, *rest):
        land_refs = rest[:groups]
        g_ref, d_ref, nm_ref, nv_ref = rest[groups:]
        grp = pl.program_id(0)
        for gi in range(groups):
            @pl.when(grp == gi)
            def _():
                l_ref = land_refs[gi]
                g = l_ref[0].astype(F32)
                for p in range(1, n_part):
                    g = g + l_ref[p].astype(F32)
                g_ref[...] = g
                m1 = ADAM_B1 * m_ref[...] + (1.0 - ADAM_B1) * g
                v1 = ADAM_B2 * v_ref[...] + (1.0 - ADAM_B2) * (g * g)
                nm_ref[...] = m1
                nv_ref[...] = v1
                d_ref[...] = -ADAM_LR * ((m1 / c1) / (jnp.sqrt(v1 / c2) + ADAM_EPS) + ADAM_WD * w_ref[...])

    blk = pl.BlockSpec((None, tr, cols), lambda g, i: (g, i, 0))
    shp = jax.ShapeDtypeStruct((groups, rows, cols), F32)

    def land_spec(gi):
        return pl.BlockSpec((n_part, tr, cols), lambda g, i: (0, jnp.where(g == gi, i, 0), 0))

    return _call(
        body, name=name, grid=(groups, rows // tr),
        in_specs=[blk, blk, blk] + [land_spec(gi) for gi in range(groups)],
        out_specs=[blk, blk, blk, blk], out_shape=[shp, shp, shp, shp],
        args=[w, m, v] + list(lands))


def _pack_small(rel, gco, gao, gpm, gqm, gpf, gqf):
    n_layers = rel.shape[0]
    relp = jnp.pad(rel, ((0, 0), (0, 0), (0, REL_PAD - rel.shape[2])))
    parts = [relp.reshape(n_layers * N_HEADS * REL_PAD // 128, 128)]
    parts += [a.reshape(-1, 128) for a in (gco, gao, gpm, gqm, gpf, gqf)]
    return jnp.concatenate(parts, axis=0)


def _unpack_small(p, n_layers):
    n_rel = n_layers * N_HEADS * REL_PAD // 128
    rel = p[:n_rel].reshape(n_layers, N_HEADS, REL_PAD)[:, :, :2 * REL_CLIP + 1]
    outs = [rel]
    r0 = n_rel
    for width in (CONV_WIDTH, ATTN_WIDTH, D_MODEL, D_MODEL, D_MODEL, D_MODEL):
        nr = n_layers * width // 128
        outs.append(p[r0:r0 + nr].reshape(n_layers, width))
        r0 += nr
    return outs


def kernel(x, w_in, w_conv, rel_bias, g_conv_out, g_attn_out, w_out, g_pre_mix, g_post_mix, g_pre_ffn, g_post_ffn, w_ffn_in, w_ffn_out, loss_target, m_w_in, m_w_conv, m_rel_bias, m_g_conv_out, m_g_attn_out, m_w_out, m_g_pre_mix, m_g_post_mix, m_g_pre_ffn, m_g_post_ffn, m_w_ffn_in, m_w_ffn_out, v_w_in, v_w_conv, v_rel_bias, v_g_conv_out, v_g_attn_out, v_w_out, v_g_pre_mix, v_g_post_mix, v_g_pre_ffn, v_g_post_ffn, v_w_ffn_in, v_w_ffn_out):
    n_layers = w_in.shape[0]
    s = x.shape[1]
    assert x.shape == (1, s, D_MODEL) and s % 1024 == 0
    assert w_in.shape == (n_layers, D_MODEL, PROJ_SHARD) and w_ffn_in.shape == (n_layers, D_MODEL, FF_SHARD)
    tm = 512
    tq = 1024 if s >= 2048 else 512
    tw = min(1024, s)
    x0 = x.reshape(s, D_MODEL)
    target = loss_target.reshape(s, D_MODEL)
    dev = _dev_index(lax.axis_index("x"), lax.axis_index("y"), lax.axis_index("c"))

    local_w = [_cast_bf16(w_in, "cast_w_in"), _cast_bf16(w_out, "cast_w_out"),
               _cast_bf16(w_ffn_in, "cast_w_ffn_in"), _cast_bf16(w_ffn_out, "cast_w_ffn_out")]
    wc_local = jnp.pad(jnp.transpose(w_conv, (0, 2, 1)).reshape(-1), (0, 1024 - n_layers * 3 * 64)).reshape(8, 128)
    *w0, wc_g = _comm_only("gather_layer0", _Gather([(a, 0) for a in local_w] + [(wc_local, None)]))
    weights = [None] * n_layers
    weights[0] = list(w0)
    wc_full = wc_g.reshape(N_DEV, 1024)[:, :n_layers * 3 * 64].reshape(N_DEV, n_layers, 3, 64)
    wc_full = jnp.transpose(wc_full, (1, 2, 0, 3)).reshape(n_layers, 3, CONV_WIDTH)
    wc_full = jnp.pad(wc_full, ((0, 0), (0, 5), (0, 0)))

    g3 = {k: v.reshape(n_layers, 1, -1) for k, v in dict(
        conv=g_conv_out, attn=g_attn_out, pre_mix=g_pre_mix, post_mix=g_post_mix,
        pre_ffn=g_pre_ffn, post_ffn=g_post_ffn).items()}
    biasm = _bias_build(jnp.pad(rel_bias, ((0, 0), (0, 0), (0, REL_PAD - rel_bias.shape[2]))))

    def views(l):
        win, wout, wfin, wfout = weights[l]
        return win, wout.reshape(D_MODEL, D_MODEL), wfin, wfout.reshape(4, FF_SHARD, D_MODEL)

    saved = []
    xl = x0
    h = _norm_cast(x0, g3["pre_mix"], 0, tm)
    for l in range(n_layers):
        win, wout, wfin, wfo4 = views(l)
        nxt = l + 1 < n_layers
        pc = _in_proj(h, win, s, tq, 0)
        qkvp = _in_proj(h, win, s, tq, 1)
        ync = _conv_fwd(pc, wc_full, g3["conv"], l, s, tm)
        o, lse, yna, *got_fin = _attn_fwd(qkvp, biasm, g3["attn"], l, s, tq,
                                          comm=_Gather([(local_w[2], l + 1)]) if nxt else None)
        z, xm, h2 = _out_proj_fwd(ync, yna, wout, xl, g3["post_mix"], g3["pre_ffn"], l, s, tm)
        gate, up, act, *got_io = _ffn_in_fwd(h2, wfin, s, tm,
                                             comm=_Gather([(local_w[0], l + 1), (local_w[1], l + 1)]) if nxt else None)
        l_next = min(l + 1, n_layers - 1)
        f, xo, h_next, *got_fout = _ffn_out_fwd(act, wfo4, xm, g3["post_ffn"], g3["pre_mix"], l, l_next, s, tm,
                                                comm=_Gather([(local_w[3], l + 1)]) if nxt else None)
        if nxt:
            weights[l + 1] = [got_io[0], got_io[1], got_fin[0], got_fout[0]]
        saved.append(dict(x=xl, h=h, pc=pc, qkvp=qkvp, ync=ync, yna=yna, o=o, lse=lse, z=z, xm=xm,
                          h2=h2, gate=gate, up=up, act=act, f=f))
        xl, h = xo, h_next

    dx, sq = _loss_grad(xl, target, s, tm)
    loss = lax.psum(jnp.sum(sq) * (0.5 / D_MODEL), ("x", "y", "c"))

    lands = dict(win=[None] * n_layers, wout=[None] * n_layers, wfin=[None] * n_layers, wfout=[None] * n_layers)
    small = {k: [None] * n_layers for k in ("gco", "gao", "gpm", "gqm", "gpf", "gqf", "wc")}
    ds_all = [None] * n_layers
    pending_win = None
    for l in reversed(range(n_layers)):
        sv = saved[l]
        win, wout, wfin, wfo4 = views(l)
        df, dg = _norm_bwd(sv["f"], g3["post_ffn"], dx, l, s, tm)
        small["gqf"][l] = dg[0]
        comm = _Scatter([(pending_win[1], False)]) if pending_win else None
        dgate, dup, *got = _ffn_out_bwd(df, wfo4, sv["gate"], sv["up"], s, tm, comm=comm)
        if pending_win:
            lands["win"][pending_win[0]] = got[0]
        d_wfout = _dw_ffn_out(sv["act"], df, s, tw).reshape(N_DEV, FFO_SHARD, D_MODEL)
        d_wfin = _dw_ffn_in(sv["h2"], dgate, dup, s, tw)
        dxm, dz, dg_pre_ffn, dg_post_mix, lands["wfout"][l] = _ffn_in_bwd(
            dgate, dup, wfin, sv["xm"], g3["pre_ffn"], dx, sv["z"], g3["post_mix"], l, s, tm,
            comm=_Scatter([(d_wfout, False)]))
        small["gpf"][l] = dg_pre_ffn[0]
        small["gqm"][l] = dg_post_mix[0]
        d_wout = _dw_out(sv["ync"], sv["yna"], dz, s, tw).reshape(N_DEV, D_MODEL // N_DEV, D_MODEL)
        dyc, do, dg_attn = _out_proj_bwd(dz, wout, sv["o"], g3["attn"], l, s, tm)
        small["gao"][l] = dg_attn[0]
        dpc, dwc, dg_conv = _conv_bwd(sv["pc"], dyc, wc_full, g3["conv"], l, s, tm)
        small["wc"][l] = dwc[0:3]
        small["gco"][l] = dg_conv[0]
        dq, dk, dv, ds_all[l], lands["wfin"][l] = _attn_bwd(
            sv["qkvp"], biasm, sv["o"], sv["lse"], do, l, s, tq, comm=_Scatter([(d_wfin, False)]))
        dproj = jnp.concatenate([dpc, dq, dk, dv], axis=1)
        pending_win = (l, _dw_in(sv["h"], dproj, s, tw))
        dx, dg_pre_mix, lands["wout"][l] = _in_proj_bwd(
            dproj, win, sv["x"], g3["pre_mix"], dxm, l, s, tm, comm=_Scatter([(d_wout, False)]))
        small["gpm"][l] = dg_pre_mix[0]
    grad_x = dx.reshape(1, s, D_MODEL)

    d_rel = _bias_bwd(jnp.stack(ds_all))[:, :, :2 * REL_CLIP + 1]
    small_vec = jnp.concatenate(
        [_pack_small(d_rel, *[jnp.stack(small[k]) for k in ("gco", "gao", "gpm", "gqm", "gpf", "gqf")]),
         jnp.stack(small["wc"]).reshape(-1, 128)], axis=0)
    small_vec = jnp.pad(small_vec, ((0, (-small_vec.shape[0]) % 8), (0, 0)))
    lands["win"][pending_win[0]], land_small = _comm_only(
        "exchange_last", _Scatter([(pending_win[1], False), (small_vec, True)]))

    r_in = _adamw("adamw_w_in", w_in, m_w_in, v_w_in, lands["win"])
    r_out = _adamw("adamw_w_out", w_out, m_w_out, v_w_out, lands["wout"])
    r_fin = _adamw("adamw_w_ffn_in", w_ffn_in, m_w_ffn_in, v_w_ffn_in, lands["wfin"])
    r_fout = _adamw("adamw_w_ffn_out", w_ffn_out, m_w_ffn_out, v_w_ffn_out, lands["wfout"])

    n_rep = 64 * n_layers
    rep = _adamw(
        "adamw_replicated",
        _pack_small(rel_bias, g_conv_out, g_attn_out, g_pre_mix, g_post_mix, g_pre_ffn, g_post_ffn)[None],
        _pack_small(m_rel_bias, m_g_conv_out, m_g_attn_out, m_g_pre_mix, m_g_post_mix, m_g_pre_ffn, m_g_post_ffn)[None],
        _pack_small(v_rel_bias, v_g_conv_out, v_g_attn_out, v_g_pre_mix, v_g_post_mix, v_g_pre_ffn, v_g_post_ffn)[None],
        [land_small[:, :n_rep]])
    rep = [_unpack_small(t[0], n_layers) for t in rep]

    wc_rows = n_layers * 3 * CONV_WIDTH // 128
    zeros_wc = jnp.zeros((1, wc_rows, 128), F32)
    g_wc_full = _adamw("sum_w_conv", zeros_wc, zeros_wc, zeros_wc, [land_small[:, n_rep:n_rep + wc_rows]])[0]
    g_wc_full = g_wc_full.reshape(n_layers, 3, CONV_WIDTH)
    g_wc = lax.dynamic_slice_in_dim(g_wc_full, dev * (CONV_WIDTH // N_DEV), CONV_WIDTH // N_DEV, axis=2)
    g_wc = jnp.transpose(g_wc, (0, 2, 1))

    def tiny(a):
        flat = a.reshape(-1)
        return jnp.pad(flat, (0, (-flat.shape[0]) % 1024)).reshape(1, -1, 128)

    r_wc = _adamw("adamw_w_conv", tiny(w_conv), tiny(m_w_conv), tiny(v_w_conv), [tiny(g_wc)])
    r_wc = [t.reshape(-1)[:w_conv.size].reshape(w_conv.shape) for t in r_wc]

    def leaf(kind):
        return [r_in[kind], r_wc[kind], rep[kind][0], rep[kind][1], rep[kind][2], r_out[kind],
                rep[kind][3], rep[kind][4], rep[kind][5], rep[kind][6], r_fin[kind], r_fout[kind]]

    return (loss, grad_x, *leaf(0), *leaf(1), *leaf(2), *leaf(3))
```

```python
import math

import jax
import jax.numpy as jnp
from jax import lax
from jax.experimental import pallas as pl
from jax.experimental.pallas import tpu as pltpu

F32 = jnp.float32
BF16 = jnp.bfloat16

D_MODEL = 1024
N_DEV = 8
CHUNK = 64
N_LEFT_CHUNKS = 8
CONV_WIDTH = 512
ATTN_WIDTH = 512
HEAD_DIM = 64
N_HEADS = 8
REL_CLIP = 128
REL_PAD = 384
PROJ_WIDTH = 3072
PROJ_SHARD = PROJ_WIDTH // N_DEV
D_FF = 2816
FF_SHARD = 2 * D_FF // N_DEV
FFO_SHARD = D_FF // N_DEV
EPS = 1e-6
NEG_INF = -1e30
Q_BLOCK = 4 * CHUNK
K_BAND = Q_BLOCK + N_LEFT_CHUNKS * CHUNK
LEFT = N_LEFT_CHUNKS * CHUNK
TOEP = 1024

ADAM_LR = 0.001
ADAM_B1 = 0.9
ADAM_B2 = 0.999
ADAM_EPS = 1e-08
ADAM_WD = 0.01
ADAM_STEP = 10

VMEM_LIMIT = 52 * 1024 * 1024
MESH = pl.DeviceIdType.MESH
ANY = pl.BlockSpec(memory_space=pl.ANY)

NT = (((1,), (1,)), ((), ()))
TN = (((0,), (0,)), ((), ()))


def _dot(a, b):
    return jnp.dot(a, b, preferred_element_type=F32)


def _dot_nt(a, b):
    return lax.dot_general(a, b, NT, preferred_element_type=F32)


def _dot_tn(a, b):
    return lax.dot_general(a, b, TN, preferred_element_type=F32)


def _rstd(v):
    return lax.rsqrt(jnp.mean(v * v, axis=-1, keepdims=True) + EPS)


def _group_matrix():
    r = lax.broadcasted_iota(jnp.int32, (128, 128), 0) >> 6
    c = lax.broadcasted_iota(jnp.int32, (128, 128), 1) >> 6
    return jnp.where(r == c, 1.0, 0.0).astype(BF16)


def _group_mean(v, gmat):
    hi = v.astype(BF16)
    lo = (v - hi.astype(F32)).astype(BF16)
    return (_dot(hi, gmat) + _dot(lo, gmat)) * (1.0 / HEAD_DIM)


def _split3(v):
    hi = v.astype(BF16)
    r1 = v - hi.astype(F32)
    mid = r1.astype(BF16)
    lo = (r1 - mid.astype(F32)).astype(BF16)
    return hi, mid, lo


def _row_tile(rows, cands=(1024, 512, 704, 256, 128, 64, 32, 16)):
    for c in cands:
        if rows % c == 0:
            return c
    return rows


def _dev_index(px, py, pc):
    return 4 * px + 2 * py + pc


def _when(cond):
    if cond is True:
        return lambda fn: fn()
    return pl.when(cond)


def _phases(grid):
    def phases():
        if not grid:
            return True, True, True
        lin = pl.program_id(0)
        for a in range(1, len(grid)):
            lin = lin * grid[a] + pl.program_id(a)
        total = math.prod(grid)
        return lin == 0, lin == (3 * total) // 4, lin == total - 1
    return phases


class _Gather:
    def __init__(self, items):
        self.items = items
        self.args = [a for a, _ in items]
        n = len(items)
        self.out_shape = [jax.ShapeDtypeStruct((N_DEV,) + (a.shape if lay is None else a.shape[1:]), a.dtype)
                          for a, lay in items]
        self.scratch = [pltpu.SemaphoreType.DMA((n, 7)), pltpu.SemaphoreType.DMA((n, 7)),
                        pltpu.SemaphoreType.DMA((n,))]

    def _ctx(self, ins, outs, sems):
        send_sems, recv_sems, local_sems = sems
        x, y, c = lax.axis_index("x"), lax.axis_index("y"), lax.axis_index("c")
        chips = [(1 - x, y), (x, 1 - y), (1 - x, 1 - y)]

        def src(k):
            lay = self.items[k][1]
            return ins[k] if lay is None else ins[k].at[lay]

        def copy(k, s, idx, to, from_src=False):
            return pltpu.make_async_remote_copy(
                src_ref=src(k) if from_src else outs[k].at[idx], dst_ref=outs[k].at[idx],
                send_sem=send_sems.at[k, s], recv_sem=recv_sems.at[k, s],
                device_id=to, device_id_type=MESH)

        def local(k):
            return pltpu.make_async_copy(src(k), outs[k].at[_dev_index(x, y, c)], local_sems.at[k])

        return x, y, c, chips, copy, local

    def start(self, ins, outs, sems, cond):
        n = len(self.items)

        @_when(cond)
        def _():
            x, y, c, chips, copy, local = self._ctx(ins, outs, sems)
            me = _dev_index(x, y, c)
            for k in range(n):
                local(k).start()
                copy(k, 0, me, (x, y, 1 - c), from_src=True).start()
                for j, chip in enumerate(chips):
                    copy(k, 1 + j, me, (chip[0], chip[1], c), from_src=True).start()

    def forward(self, ins, outs, sems, cond):
        n = len(self.items)

        @_when(cond)
        def _():
            x, y, c, chips, copy, local = self._ctx(ins, outs, sems)
            for j, chip in enumerate(chips):
                idx = _dev_index(chip[0], chip[1], c)
                for k in range(n):
                    copy(k, 1 + j, idx, (x, y, c)).wait_recv()
                    copy(k, 4 + j, idx, (x, y, 1 - c)).start()

    def finish(self, ins, outs, sems, cond):
        n = len(self.items)

        @_when(cond)
        def _():
            x, y, c, chips, copy, local = self._ctx(ins, outs, sems)
            me = _dev_index(x, y, c)
            for k in range(n):
                copy(k, 0, _dev_index(x, y, 1 - c), (x, y, c)).wait_recv()
            for j, chip in enumerate(chips):
                idx = _dev_index(chip[0], chip[1], 1 - c)
                for k in range(n):
                    copy(k, 4 + j, idx, (x, y, c)).wait_recv()
            for k in range(n):
                for s in range(4):
                    copy(k, s, me, (x, y, c), from_src=True).wait_send()
                for j, chip in enumerate(chips):
                    copy(k, 4 + j, _dev_index(chip[0], chip[1], c), (x, y, c)).wait_send()
                local(k).wait()


_PEER_FLIPS = [(0, 0, 1), (1, 0, 0), (0, 1, 0), (1, 1, 0), (1, 0, 1), (0, 1, 1), (1, 1, 1)]


class _Scatter:
    def __init__(self, items):
        self.items = items
        self.args = [a for a, _ in items]
        n = len(items)
        self.out_shape = [jax.ShapeDtypeStruct((N_DEV,) + (a.shape if whole else a.shape[1:]), a.dtype)
                          for a, whole in items]
        self.scratch = [pltpu.SemaphoreType.DMA((n, 7)), pltpu.SemaphoreType.DMA((n, 7)),
                        pltpu.SemaphoreType.DMA((n,))]

    def _ctx(self, ins, outs, sems):
        send_sems, recv_sems, local_sems = sems
        x, y, c = lax.axis_index("x"), lax.axis_index("y"), lax.axis_index("c")
        me = _dev_index(x, y, c)

        def src(k, dest_idx):
            return ins[k] if self.items[k][1] else ins[k].at[dest_idx]

        def send(k, s):
            fx, fy, fc = _PEER_FLIPS[s]
            px, py, pc = x ^ fx, y ^ fy, c ^ fc
            return pltpu.make_async_remote_copy(
                src_ref=src(k, _dev_index(px, py, pc)), dst_ref=outs[k].at[me],
                send_sem=send_sems.at[k, s], recv_sem=recv_sems.at[k, s],
                device_id=(px, py, pc), device_id_type=MESH)

        def recv(k, s):
            fx, fy, fc = _PEER_FLIPS[s]
            pidx = _dev_index(x ^ fx, y ^ fy, c ^ fc)
            return pltpu.make_async_remote_copy(
                src_ref=outs[k].at[pidx], dst_ref=outs[k].at[pidx],
                send_sem=send_sems.at[k, s], recv_sem=recv_sems.at[k, s],
                device_id=(x, y, c), device_id_type=MESH)

        def local(k):
            return pltpu.make_async_copy(src(k, me), outs[k].at[me], local_sems.at[k])

        return send, recv, local

    def start(self, ins, outs, sems, cond):
        n = len(self.items)

        @_when(cond)
        def _():
            send, recv, local = self._ctx(ins, outs, sems)
            for k in range(n):
                local(k).start()
            for s in range(7):
                for k in range(n):
                    send(k, s).start()

    def forward(self, ins, outs, sems, cond):
        pass

    def finish(self, ins, outs, sems, cond):
        n = len(self.items)

        @_when(cond)
        def _():
            send, recv, local = self._ctx(ins, outs, sems)
            for s in range(7):
                for k in range(n):
                    recv(k, s).wait_recv()
            for s in range(7):
                for k in range(n):
                    send(k, s).wait_send()
            for k in range(n):
                local(k).wait()


def _call(body, *, name, grid, in_specs, out_specs, out_shape, args, scratch=(), comm=None):
    n_hi, n_ho, n_hs = len(args), len(out_shape), len(scratch)
    c_args = list(comm.args) if comm else []
    c_out = list(comm.out_shape) if comm else []
    c_scr = list(comm.scratch) if comm else []
    phases = _phases(grid)

    def kern(*refs):
        cuts = [n_hi, len(c_args), n_ho, len(c_out), n_hs, len(c_scr)]
        parts, pos = [], 0
        for n in cuts:
            parts.append(refs[pos:pos + n])
            pos += n
        hi, ci, ho, co, hs, cs = parts
        if comm:
            first, mid, last = phases()
            comm.start(ci, co, cs, first)
            comm.forward(ci, co, cs, mid)
        body(*hi, *ho, *hs)
        if comm:
            comm.finish(ci, co, cs, last)

    sem = ("arbitrary",) * len(grid) if grid else None
    return pl.pallas_call(
        kern, name=name, grid=grid,
        in_specs=list(in_specs) + [ANY] * len(c_args),
        out_specs=list(out_specs) + [ANY] * len(c_out),
        out_shape=list(out_shape) + c_out,
        scratch_shapes=list(scratch) + c_scr,
        compiler_params=pltpu.CompilerParams(dimension_semantics=sem, vmem_limit_bytes=VMEM_LIMIT),
    )(*args, *c_args)


def _comm_only(name, comm):
    return _call(lambda: None, name=name, grid=(), in_specs=[], out_specs=[], out_shape=[], args=[], comm=comm)


def _cast_bf16(x, name):
    shape = x.shape
    x2 = x.reshape(-1, shape[-1])
    rows, cols = x2.shape
    tr = _row_tile(rows)

    def body(x_ref, o_ref):
        o_ref[...] = x_ref[...].astype(BF16)

    blk = pl.BlockSpec((tr, cols), lambda i: (i, 0))
    out, = _call(body, name=name, grid=(rows // tr,), in_specs=[blk], out_specs=[blk],
                 out_shape=[jax.ShapeDtypeStruct((rows, cols), BF16)], args=[x2])
    return out.reshape(shape)


def _norm_cast(x, g3, l, tm):
    s = x.shape[0]

    def body(x_ref, g_ref, o_ref):
        v = x_ref[...]
        o_ref[...] = (v * _rstd(v) * g_ref[...]).astype(BF16)

    row = pl.BlockSpec((tm, D_MODEL), lambda i: (i, 0))
    out, = _call(body, name="norm_cast", grid=(s // tm,),
                 in_specs=[row, pl.BlockSpec((None, 1, D_MODEL), lambda i: (l, 0, 0))], out_specs=[row],
                 out_shape=[jax.ShapeDtypeStruct((s, D_MODEL), BF16)], args=[x, g3])
    return out


def _in_proj(h, win, s, tq, part):
    pad = part
    dtype = BF16 if part else F32

    def body(a_ref, b_ref, o_ref):
        def compute():
            a = a_ref[...]
            for j in range(4):
                o_ref[:, PROJ_SHARD * j:PROJ_SHARD * (j + 1)] = _dot(a, b_ref[j]).astype(dtype)

        if pad:
            i = pl.program_id(0)

            @pl.when(i == 0)
            def _():
                o_ref[...] = jnp.zeros(o_ref.shape, dtype)

            pl.when(i > 0)(compute)
        else:
            compute()

    out, = _call(
        body, name="in_proj_qkv" if part else "in_proj_conv", grid=(s // tq + pad,),
        in_specs=[pl.BlockSpec((tq, D_MODEL), lambda i: (jnp.maximum(i - pad, 0), 0)),
                  pl.BlockSpec((4, D_MODEL, PROJ_SHARD), lambda i: (part, 0, 0))],
        out_specs=[pl.BlockSpec((tq, 4 * PROJ_SHARD), lambda i: (i, 0))],
        out_shape=[jax.ShapeDtypeStruct((s + pad * tq, PROJ_WIDTH // 2), dtype)], args=[h, win])
    return out


def _conv_fwd(pc, wc, g3, l, s, tr):
    hb = tr // 8

    def body(pc_ref, prev_ref, wc_ref, g_ref, o_ref):
        i = pl.program_id(0)
        gmat = _group_matrix()
        for j in range(CONV_WIDTH // 128):
            c0, c1, c2 = 128 * j, CONV_WIDTH + 128 * j, 2 * CONV_WIDTH + 128 * j
            hc = pc_ref[:, c0:c0 + 128]
            bg = pc_ref[:, c1:c1 + 128]
            cg = pc_ref[:, c2:c2 + 128]
            u_prev = jnp.where(i > 0, prev_ref[:, c2:c2 + 128] * prev_ref[:, c0:c0 + 128], 0.0)
            u = cg * hc
            full = jnp.concatenate([u_prev, u], axis=0)
            u1 = pltpu.roll(full, 1, 0)[8:]
            u2 = pltpu.roll(full, 2, 0)[8:]
            out = (u2 * wc_ref[0:1, c0:c0 + 128] + u1 * wc_ref[1:2, c0:c0 + 128]
                   + u * wc_ref[2:3, c0:c0 + 128])
            yc = bg * out
            r = lax.rsqrt(_group_mean(yc * yc, gmat) + EPS)
            o_ref[:, c0:c0 + 128] = (yc * r * g_ref[:, c0:c0 + 128]).astype(BF16)

    out, = _call(
        body, name="conv_fwd", grid=(s // tr,),
        in_specs=[pl.BlockSpec((tr, 3 * CONV_WIDTH), lambda i: (i, 0)),
                  pl.BlockSpec((8, 3 * CONV_WIDTH), lambda i: (jnp.maximum(i * hb - 1, 0), 0)),
                  pl.BlockSpec((None, 8, CONV_WIDTH), lambda i: (l, 0, 0)),
                  pl.BlockSpec((None, 1, CONV_WIDTH), lambda i: (l, 0, 0))],
        out_specs=[pl.BlockSpec((tr, CONV_WIDTH), lambda i: (i, 0))],
        out_shape=[jax.ShapeDtypeStruct((s, CONV_WIDTH), BF16)], args=[pc, pc, wc, g3])
    return out


def _toeplitz_source():
    r_i = lax.broadcasted_iota(jnp.int32, (REL_PAD, TOEP), 0)
    m_i = lax.broadcasted_iota(jnp.int32, (REL_PAD, TOEP), 1)
    idx = jnp.clip((K_BAND - 1) - m_i, -REL_CLIP, REL_CLIP) + REL_CLIP
    return jnp.where(r_i == idx, 1.0, 0.0).astype(BF16)


def _bias_build(rbp):
    n_layers = rbp.shape[0]

    def body(rb_ref, o_ref, t_ref):
        pmat = _toeplitz_source()
        hi, mid, lo = _split3(rb_ref[...])
        t_ref[...] = _dot(hi, pmat) + _dot(mid, pmat) + _dot(lo, pmat)
        shift = (Q_BLOCK - 1) - lax.broadcasted_iota(jnp.int32, (Q_BLOCK, TOEP), 0)
        kk = lax.broadcasted_iota(jnp.int32, (Q_BLOCK, K_BAND), 1) >> 6
        qq = lax.broadcasted_iota(jnp.int32, (Q_BLOCK, K_BAND), 0) >> 6
        dchunk = kk - qq
        in_band = jnp.where(dchunk >= 0, jnp.where(dchunk <= N_LEFT_CHUNKS, 1, 0), 0) == 1
        for h in range(N_HEADS):
            b = jnp.broadcast_to(t_ref[pl.ds(h, 1), :], (Q_BLOCK, TOEP))
            for bit in range(8):
                rolled = pltpu.roll(b, TOEP - (1 << bit), 1)
                b = jnp.where(((shift >> bit) & 1) == 1, rolled, b)
            o_ref[h] = jnp.where(in_band, b[:, :K_BAND], NEG_INF)

    out, = _call(
        body, name="bias_build", grid=(n_layers,),
        in_specs=[pl.BlockSpec((None, N_HEADS, REL_PAD), lambda l: (l, 0, 0))],
        out_specs=[pl.BlockSpec((None, N_HEADS, Q_BLOCK, K_BAND), lambda l: (l, 0, 0, 0))],
        out_shape=[jax.ShapeDtypeStruct((n_layers, N_HEADS, Q_BLOCK, K_BAND), F32)],
        scratch=[pltpu.VMEM((N_HEADS, TOEP), F32)], args=[rbp])
    return out


def _bias_bwd(ds_sum):
    n_layers = ds_sum.shape[0]

    def body(ds_ref, o_ref, t_ref):
        pmat = _toeplitz_source()
        shift = (Q_BLOCK - 1) - lax.broadcasted_iota(jnp.int32, (Q_BLOCK, TOEP), 0)
        for h in range(N_HEADS):
            d = jnp.concatenate([ds_ref[h], jnp.zeros((Q_BLOCK, TOEP - K_BAND), F32)], axis=1)
            for bit in range(8):
                rolled = pltpu.roll(d, 1 << bit, 1)
                d = jnp.where(((shift >> bit) & 1) == 1, rolled, d)
            t_ref[pl.ds(h, 1), :] = jnp.sum(d, axis=0, keepdims=True)
        hi, mid, lo = _split3(t_ref[...])
        o_ref[...] = _dot_nt(hi, pmat) + _dot_nt(mid, pmat) + _dot_nt(lo, pmat)

    out, = _call(
        body, name="bias_bwd", grid=(n_layers,),
        in_specs=[pl.BlockSpec((None, N_HEADS, Q_BLOCK, K_BAND), lambda l: (l, 0, 0, 0))],
        out_specs=[pl.BlockSpec((None, N_HEADS, REL_PAD), lambda l: (l, 0, 0))],
        out_shape=[jax.ShapeDtypeStruct((n_layers, N_HEADS, REL_PAD), F32)],
        scratch=[pltpu.VMEM((N_HEADS, TOEP), F32)], args=[ds_sum])
    return out


def _attn_fwd(qkvp, biasm, g3, l, s, pad, comm=None):
    nb = s // Q_BLOCK
    qb0 = pad // Q_BLOCK
    scale = HEAD_DIM ** -0.5

    def body(q_ref, k_ref, v_ref, b_ref, g_ref, o_ref, lse_ref, yn_ref):
        blk = pl.program_id(1)
        koff = pl.multiple_of(blk * Q_BLOCK + (pad - LEFT), Q_BLOCK)
        q = q_ref[...]
        kb = k_ref[pl.ds(koff, K_BAND), :]
        vb = v_ref[pl.ds(koff, K_BAND), :]
        lane = lax.broadcasted_iota(jnp.int32, (1, 128), 1)
        kpos = lax.broadcasted_iota(jnp.int32, (Q_BLOCK, K_BAND), 1) + (blk * Q_BLOCK - LEFT)
        kvalid = kpos >= 0
        outs, lses = [], []
        for hh in range(2):
            in_head = (lane >> 6) == hh
            qm = jnp.where(in_head, q, jnp.zeros_like(q)) * jnp.asarray(scale, BF16)
            sc = _dot_nt(qm, kb) + b_ref[hh]
            sc = jnp.where(kvalid, sc, NEG_INF)
            m = jnp.max(sc, axis=1, keepdims=True)
            e = jnp.exp(sc - m)
            den = jnp.sum(e, axis=1, keepdims=True)
            outs.append(_dot(e.astype(BF16), vb) * (1.0 / den))
            lses.append(m + jnp.log(den))
        first = lane < HEAD_DIM
        o = jnp.where(first, outs[0], outs[1])
        o_ref[...] = o
        lse_ref[...] = jnp.where(first, lses[0], lses[1])
        r = lax.rsqrt(_group_mean(o * o, _group_matrix()) + EPS)
        yn_ref[...] = (o * r * g_ref[...]).astype(BF16)

    blk_out = pl.BlockSpec((Q_BLOCK, 128), lambda p, b: (b, p))
    return _call(
        body, name="attn_fwd", grid=(ATTN_WIDTH // 128, nb),
        in_specs=[pl.BlockSpec((Q_BLOCK, 128), lambda p, b: (qb0 + b, p)),
                  pl.BlockSpec((s + pad, 128), lambda p, b: (0, 4 + p)),
                  pl.BlockSpec((s + pad, 128), lambda p, b: (0, 8 + p)),
                  pl.BlockSpec((None, 2, Q_BLOCK, K_BAND), lambda p, b: (l, p, 0, 0)),
                  pl.BlockSpec((None, 1, 128), lambda p, b: (l, 0, p))],
        out_specs=[blk_out, blk_out, blk_out],
        out_shape=[jax.ShapeDtypeStruct((s, ATTN_WIDTH), F32),
                   jax.ShapeDtypeStruct((s, ATTN_WIDTH), F32),
                   jax.ShapeDtypeStruct((s, ATTN_WIDTH), BF16)],
        args=[qkvp, qkvp, qkvp, biasm, g3], comm=comm)


def _out_proj_fwd(ync, yna, wout, x, g_post3, g_next3, l, s, tm):
    half = D_MODEL // 2

    def body(a1_ref, a2_ref, w_ref, x_ref, gp_ref, gn_ref, z_ref, xm_ref, h_ref):
        z = _dot(a1_ref[...], w_ref[0:half, :]) + _dot(a2_ref[...], w_ref[half:D_MODEL, :])
        z_ref[...] = z
        xm = x_ref[...] + z * _rstd(z) * gp_ref[...]
        xm_ref[...] = xm
        h_ref[...] = (xm * _rstd(xm) * gn_ref[...]).astype(BF16)

    row = pl.BlockSpec((tm, D_MODEL), lambda i: (i, 0))
    gain = pl.BlockSpec((None, 1, D_MODEL), lambda i: (l, 0, 0))
    return _call(
        body, name="out_proj_fwd", grid=(s // tm,),
        in_specs=[pl.BlockSpec((tm, half), lambda i: (i, 0)), pl.BlockSpec((tm, half), lambda i: (i, 0)),
                  pl.BlockSpec((D_MODEL, D_MODEL), lambda i: (0, 0)), row, gain, gain],
        out_specs=[row, row, row],
        out_shape=[jax.ShapeDtypeStruct((s, D_MODEL), F32), jax.ShapeDtypeStruct((s, D_MODEL), F32),
                   jax.ShapeDtypeStruct((s, D_MODEL), BF16)],
        args=[ync, yna, wout, x, g_post3, g_next3])


def _ffn_in_fwd(h2, wfin, s, tm, comm=None):
    def body(h_ref, wg_ref, wu_ref, gu_ref, act_ref):
        h = h_ref[...]
        gate = _dot(h, wg_ref[...])
        up = _dot(h, wu_ref[...])
        gu_ref[0] = gate.astype(BF16)
        gu_ref[1] = up.astype(BF16)
        act_ref[...] = (gate * (1.0 / (1.0 + jnp.exp(-gate))) * up).astype(BF16)

    return _call(
        body, name="ffn_in_fwd", grid=(4, s // tm),
        in_specs=[pl.BlockSpec((tm, D_MODEL), lambda b, i: (i, 0)),
                  pl.BlockSpec((None, D_MODEL, FF_SHARD), lambda b, i: (b, 0, 0)),
                  pl.BlockSpec((None, D_MODEL, FF_SHARD), lambda b, i: (4 + b, 0, 0))],
        out_specs=[pl.BlockSpec((None, 2, tm, FF_SHARD), lambda b, i: (b, 0, i, 0)),
                   pl.BlockSpec((None, tm, FF_SHARD), lambda b, i: (b, i, 0))],
        out_shape=[jax.ShapeDtypeStruct((4, 2, s, FF_SHARD), BF16), jax.ShapeDtypeStruct((4, s, FF_SHARD), BF16)],
        args=[h2, wfin, wfin], comm=comm)


def _ffn_out_fwd(act, wfo4, xm, g_post3, g_next3, l, l_next, s, tm, comm=None):
    def body(a_ref, w_ref, x_ref, gp_ref, gn_ref, f_ref, xo_ref, h_ref):
        f = _dot(a_ref[0], w_ref[0])
        for b in range(1, 4):
            f = f + _dot(a_ref[b], w_ref[b])
        f_ref[...] = f
        xo = x_ref[...] + f * _rstd(f) * gp_ref[...]
        xo_ref[...] = xo
        h_ref[...] = (xo * _rstd(xo) * gn_ref[...]).astype(BF16)

    row = pl.BlockSpec((tm, D_MODEL), lambda i: (i, 0))
    return _call(
        body, name="ffn_out_fwd", grid=(s // tm,),
        in_specs=[pl.BlockSpec((4, tm, FF_SHARD), lambda i: (0, i, 0)),
                  pl.BlockSpec((4, FF_SHARD, D_MODEL), lambda i: (0, 0, 0)), row,
                  pl.BlockSpec((None, 1, D_MODEL), lambda i: (l, 0, 0)),
                  pl.BlockSpec((None, 1, D_MODEL), lambda i: (l_next, 0, 0))],
        out_specs=[row, row, row],
        out_shape=[jax.ShapeDtypeStruct((s, D_MODEL), F32), jax.ShapeDtypeStruct((s, D_MODEL), F32),
                   jax.ShapeDtypeStruct((s, D_MODEL), BF16)],
        args=[act, wfo4, xm, g_post3, g_next3], comm=comm)


def _loss_grad(xf, target, s, tm):
    def body(x_ref, t_ref, dx_ref, sq_ref):
        i = pl.program_id(0)
        err = x_ref[...] - t_ref[...]
        dx_ref[...] = err * (1.0 / D_MODEL)
        cs = jnp.sum(err * err, axis=0, keepdims=True)
        part = cs[:, 0:128]
        for k in range(1, D_MODEL // 128):
            part = part + cs[:, 128 * k:128 * (k + 1)]

        @pl.when(i == 0)
        def _():
            sq_ref[...] = jnp.zeros(sq_ref.shape, F32)

        sq_ref[0:1, :] += part

    row = pl.BlockSpec((tm, D_MODEL), lambda i: (i, 0))
    return _call(
        body, name="loss_grad", grid=(s // tm,), in_specs=[row, row],
        out_specs=[row, pl.BlockSpec((8, 128), lambda i: (0, 0))],
        out_shape=[jax.ShapeDtypeStruct((s, D_MODEL), F32), jax.ShapeDtypeStruct((8, 128), F32)],
        args=[xf, target])


def _norm_bwd_rows(v, g, dy):
    r = _rstd(v)
    vn = v * r
    gd = dy * g
    dv = r * (gd - vn * jnp.mean(vn * gd, axis=-1, keepdims=True))
    return dv, dy * vn


def _accum_cols(ref, val, first):
    @pl.when(first)
    def _():
        ref[...] = jnp.zeros(ref.shape, F32)

    ref[0:1, :] += jnp.sum(val, axis=0, keepdims=True)


def _norm_bwd(z, g3, dy, l, s, tm):
    def body(z_ref, g_ref, dy_ref, dz_ref, dg_ref):
        dz, dyn = _norm_bwd_rows(z_ref[...], g_ref[...], dy_ref[...])
        dz_ref[...] = dz.astype(BF16)
        _accum_cols(dg_ref, dyn, pl.program_id(0) == 0)

    row = pl.BlockSpec((tm, D_MODEL), lambda i: (i, 0))
    return _call(
        body, name="norm_bwd", grid=(s // tm,),
        in_specs=[row, pl.BlockSpec((None, 1, D_MODEL), lambda i: (l, 0, 0)), row],
        out_specs=[row, pl.BlockSpec((8, D_MODEL), lambda i: (0, 0))],
        out_shape=[jax.ShapeDtypeStruct((s, D_MODEL), BF16), jax.ShapeDtypeStruct((8, D_MODEL), F32)],
        args=[z, g3, dy])


def _ffn_out_bwd(df, wfo4, gu, s, tm, comm=None):
    def body(df_ref, w_ref, gu_ref, dgu_ref):
        da = _dot_nt(df_ref[...], w_ref[...])
        g = gu_ref[0].astype(F32)
        u = gu_ref[1].astype(F32)
        sg = 1.0 / (1.0 + jnp.exp(-g))
        dgu_ref[0] = (da * u * (sg * (1.0 + g * (1.0 - sg)))).astype(BF16)
        dgu_ref[1] = (da * (g * sg)).astype(BF16)

    blk = pl.BlockSpec((None, 2, tm, FF_SHARD), lambda b, i: (b, 0, i, 0))
    return _call(
        body, name="ffn_out_bwd", grid=(4, s // tm),
        in_specs=[pl.BlockSpec((tm, D_MODEL), lambda b, i: (i, 0)),
                  pl.BlockSpec((None, FF_SHARD, D_MODEL), lambda b, i: (b, 0, 0)), blk],
        out_specs=[blk], out_shape=[jax.ShapeDtypeStruct((4, 2, s, FF_SHARD), BF16)],
        args=[df, wfo4, gu], comm=comm)


def _dw_ffn_out(act, df, s):
    def body(a_ref, b_ref, o_ref):
        o_ref[...] = _dot_tn(a_ref[...], b_ref[...]).astype(BF16)

    out, = _call(
        body, name="dw_ffn_out", grid=(4,),
        in_specs=[pl.BlockSpec((None, s, FF_SHARD), lambda n: (n, 0, 0)),
                  pl.BlockSpec((s, D_MODEL), lambda n: (0, 0))],
        out_specs=[pl.BlockSpec((FF_SHARD, D_MODEL), lambda n: (n, 0))],
        out_shape=[jax.ShapeDtypeStruct((D_FF, D_MODEL), BF16)], args=[act, df])
    return out


def _dw_ffn_in(h2, dgu, s):
    def body(a_ref, b_ref, o_ref):
        o_ref[...] = _dot_tn(a_ref[...], b_ref[...]).astype(BF16)

    out, = _call(
        body, name="dw_ffn_in", grid=(N_DEV,),
        in_specs=[pl.BlockSpec((s, D_MODEL), lambda n: (0, 0)),
                  pl.BlockSpec((None, None, s, FF_SHARD), lambda n: (n % 4, n // 4, 0, 0))],
        out_specs=[pl.BlockSpec((None, D_MODEL, FF_SHARD), lambda n: (n, 0, 0))],
        out_shape=[jax.ShapeDtypeStruct((N_DEV, D_MODEL, FF_SHARD), BF16)], args=[h2, dgu])
    return out


def _ffn_in_bwd(dgu, wfin, xm, g_pre3, dres, z, g_post3, l, s, tm, comm=None):
    def body(d_ref, w_ref, xm_ref, gp_ref, dres_ref, z_ref, gq_ref, dxm_ref, dz_ref, dgp_ref, dgq_ref):
        first = pl.program_id(0) == 0
        dh = _dot_nt(d_ref[0, 0], w_ref[0])
        for j in range(1, N_DEV):
            dh = dh + _dot_nt(d_ref[j % 4, j // 4], w_ref[j])
        dx, dyn = _norm_bwd_rows(xm_ref[...], gp_ref[...], dh)
        dxm = dres_ref[...] + dx
        dxm_ref[...] = dxm
        _accum_cols(dgp_ref, dyn, first)
        dz, dyn2 = _norm_bwd_rows(z_ref[...], gq_ref[...], dxm)
        dz_ref[...] = dz.astype(BF16)
        _accum_cols(dgq_ref, dyn2, first)

    row = pl.BlockSpec((tm, D_MODEL), lambda i: (i, 0))
    gain = pl.BlockSpec((None, 1, D_MODEL), lambda i: (l, 0, 0))
    dgs = pl.BlockSpec((8, D_MODEL), lambda i: (0, 0))
    return _call(
        body, name="ffn_in_bwd", grid=(s // tm,),
        in_specs=[pl.BlockSpec((4, 2, tm, FF_SHARD), lambda i: (0, 0, i, 0)),
                  pl.BlockSpec((N_DEV, D_MODEL, FF_SHARD), lambda i: (0, 0, 0)),
                  row, gain, row, row, gain],
        out_specs=[row, row, dgs, dgs],
        out_shape=[jax.ShapeDtypeStruct((s, D_MODEL), F32), jax.ShapeDtypeStruct((s, D_MODEL), BF16),
                   jax.ShapeDtypeStruct((8, D_MODEL), F32), jax.ShapeDtypeStruct((8, D_MODEL), F32)],
        args=[dgu, wfin, xm, g_pre3, dres, z, g_post3], comm=comm)


def _dw_out(ync, yna, dz, s):
    half = D_MODEL // 2

    def body(a1_ref, a2_ref, b_ref, o_ref):
        b = b_ref[...]
        o_ref[0:half, :] = _dot_tn(a1_ref[...], b).astype(BF16)
        o_ref[half:D_MODEL, :] = _dot_tn(a2_ref[...], b).astype(BF16)

    out, = _call(
        body, name="dw_out", grid=(2,),
        in_specs=[pl.BlockSpec((s, half), lambda n: (0, 0)), pl.BlockSpec((s, half), lambda n: (0, 0)),
                  pl.BlockSpec((s, half), lambda n: (0, n))],
        out_specs=[pl.BlockSpec((D_MODEL, half), lambda n: (0, n))],
        out_shape=[jax.ShapeDtypeStruct((D_MODEL, D_MODEL), BF16)], args=[ync, yna, dz])
    return out


def _out_proj_bwd(dz, wout, o, g3, l, s, tm):
    def body(dz_ref, w_ref, o_ref, g_ref, dyc_ref, do_ref, dg_ref):
        dy = _dot_nt(dz_ref[...], w_ref[...])
        dyc_ref[...] = dy[:, 0:CONV_WIDTH]
        gmat = _group_matrix()

        @pl.when(pl.program_id(0) == 0)
        def _():
            dg_ref[...] = jnp.zeros(dg_ref.shape, F32)

        for j in range(ATTN_WIDTH // 128):
            c0 = 128 * j
            ov = o_ref[:, c0:c0 + 128]
            dyn = dy[:, CONV_WIDTH + c0:CONV_WIDTH + c0 + 128]
            r = lax.rsqrt(_group_mean(ov * ov, gmat) + EPS)
            on = ov * r
            gd = dyn * g_ref[:, c0:c0 + 128]
            do_ref[:, c0:c0 + 128] = r * (gd - on * _group_mean(on * gd, gmat))
            dg_ref[0:1, c0:c0 + 128] += jnp.sum(dyn * on, axis=0, keepdims=True)

    halfrow = pl.BlockSpec((tm, ATTN_WIDTH), lambda i: (i, 0))
    return _call(
        body, name="out_proj_bwd", grid=(s // tm,),
        in_specs=[pl.BlockSpec((tm, D_MODEL), lambda i: (i, 0)),
                  pl.BlockSpec((D_MODEL, D_MODEL), lambda i: (0, 0)), halfrow,
                  pl.BlockSpec((None, 1, ATTN_WIDTH), lambda i: (l, 0, 0))],
        out_specs=[halfrow, halfrow, pl.BlockSpec((8, ATTN_WIDTH), lambda i: (0, 0))],
        out_shape=[jax.ShapeDtypeStruct((s, CONV_WIDTH), F32), jax.ShapeDtypeStruct((s, ATTN_WIDTH), F32),
                   jax.ShapeDtypeStruct((8, ATTN_WIDTH), F32)],
        args=[dz, wout, o, g3])


def _conv_bwd(pc, dyc, wc, g3, l, s, tr):
    hb = tr // 8
    nt = s // tr
    ext = tr + 16
    last_hb = s // 8 - 1

    def body(pc_ref, prev_ref, next_ref, dy_ref, dyn_ref, wc_ref, g_ref, dpc_ref, dw_ref, dg_ref):
        i = pl.program_id(0)
        gmat = _group_matrix()
        row = lax.broadcasted_iota(jnp.int32, (ext, 128), 0) + (i * tr - 8)
        inside = jnp.where(row >= 0, jnp.where(row < s, 1, 0), 0) == 1

        @pl.when(i == 0)
        def _():
            dw_ref[...] = jnp.zeros(dw_ref.shape, F32)
            dg_ref[...] = jnp.zeros(dg_ref.shape, F32)

        def extend(ref_prev, ref_mid, ref_next, c):
            parts = [ref_prev[:, c:c + 128] if ref_prev is not None else jnp.zeros((8, 128), F32),
                     ref_mid[:, c:c + 128], ref_next[:, c:c + 128]]
            return jnp.concatenate(parts, axis=0)

        for j in range(CONV_WIDTH // 128):
            c0, c1, c2 = 128 * j, CONV_WIDTH + 128 * j, 2 * CONV_WIDTH + 128 * j
            hc = extend(prev_ref, pc_ref, next_ref, c0)
            bg = extend(prev_ref, pc_ref, next_ref, c1)
            cg = extend(prev_ref, pc_ref, next_ref, c2)
            dyn = extend(None, dy_ref, dyn_ref, c0)
            w0, w1, w2 = (wc_ref[0:1, c0:c0 + 128], wc_ref[1:2, c0:c0 + 128], wc_ref[2:3, c0:c0 + 128])
            gain = g_ref[:, c0:c0 + 128]
            u = jnp.where(inside, cg * hc, 0.0)
            u1 = pltpu.roll(u, 1, 0)
            u2 = pltpu.roll(u, 2, 0)
            out = u2 * w0 + u1 * w1 + u * w2
            yc = bg * out
            r = lax.rsqrt(_group_mean(yc * yc, gmat) + EPS)
            ycn = yc * r
            gd = dyn * gain
            dyc = r * (gd - ycn * _group_mean(ycn * gd, gmat))
            dout = jnp.where(inside, dyc * bg, 0.0)
            du = dout * w2 + pltpu.roll(dout, ext - 1, 0) * w1 + pltpu.roll(dout, ext - 2, 0) * w0
            sl = slice(8, 8 + tr)
            dpc_ref[:, c0:c0 + 128] = (du[sl] * cg[sl]).astype(BF16)
            dpc_ref[:, c1:c1 + 128] = (dyc[sl] * out[sl]).astype(BF16)
            dpc_ref[:, c2:c2 + 128] = (du[sl] * hc[sl]).astype(BF16)
            dw_ref[0:1, c0:c0 + 128] += jnp.sum(dout[sl] * u2[sl], axis=0, keepdims=True)
            dw_ref[1:2, c0:c0 + 128] += jnp.sum(dout[sl] * u1[sl], axis=0, keepdims=True)
            dw_ref[2:3, c0:c0 + 128] += jnp.sum(dout[sl] * u[sl], axis=0, keepdims=True)
            dg_ref[0:1, c0:c0 + 128] += jnp.sum(dyn[sl] * ycn[sl], axis=0, keepdims=True)

    wide = 3 * CONV_WIDTH
    return _call(
        body, name="conv_bwd", grid=(nt,),
        in_specs=[pl.BlockSpec((tr, wide), lambda i: (i, 0)),
                  pl.BlockSpec((8, wide), lambda i: (jnp.maximum(i * hb - 1, 0), 0)),
                  pl.BlockSpec((8, wide), lambda i: (jnp.minimum((i + 1) * hb, last_hb), 0)),
                  pl.BlockSpec((tr, CONV_WIDTH), lambda i: (i, 0)),
                  pl.BlockSpec((8, CONV_WIDTH), lambda i: (jnp.minimum((i + 1) * hb, last_hb), 0)),
                  pl.BlockSpec((None, 8, CONV_WIDTH), lambda i: (l, 0, 0)),
                  pl.BlockSpec((None, 1, CONV_WIDTH), lambda i: (l, 0, 0))],
        out_specs=[pl.BlockSpec((tr, wide), lambda i: (i, 0)),
                   pl.BlockSpec((8, CONV_WIDTH), lambda i: (0, 0)),
                   pl.BlockSpec((8, CONV_WIDTH), lambda i: (0, 0))],
        out_shape=[jax.ShapeDtypeStruct((s, wide), BF16), jax.ShapeDtypeStruct((8, CONV_WIDTH), F32),
                   jax.ShapeDtypeStruct((8, CONV_WIDTH), F32)],
        args=[pc, pc, pc, dyc, dyc, wc, g3])


def _attn_bwd(qkvp, biasm, o, lse, do, l, s, pad, comm=None):
    nb = s // Q_BLOCK
    qb0 = pad // Q_BLOCK
    scale = HEAD_DIM ** -0.5

    def body(q_ref, k_ref, v_ref, b_ref, o_ref, lse_ref, do_ref,
             dq_ref, dk_ref, dv_ref, ds_ref, dk_acc, dv_acc):
        blk = pl.program_id(1)

        @pl.when(blk == 0)
        def _():
            dk_acc[...] = jnp.zeros(dk_acc.shape, F32)
            dv_acc[...] = jnp.zeros(dv_acc.shape, F32)
            ds_ref[...] = jnp.zeros(ds_ref.shape, F32)

        koff = pl.multiple_of(blk * Q_BLOCK + (pad - LEFT), Q_BLOCK)
        q = q_ref[...]
        kb = k_ref[pl.ds(koff, K_BAND), :]
        vb = v_ref[pl.ds(koff, K_BAND), :]
        ov = o_ref[...]
        dov = do_ref[...]
        lse_v = lse_ref[...]
        lane = lax.broadcasted_iota(jnp.int32, (1, 128), 1)
        kpos = lax.broadcasted_iota(jnp.int32, (Q_BLOCK, K_BAND), 1) + (blk * Q_BLOCK - LEFT)
        kvalid = kpos >= 0
        prod = dov * ov
        dq_parts = []
        dk_new = jnp.zeros((K_BAND, 128), F32)
        dv_new = jnp.zeros((K_BAND, 128), F32)
        for hh in range(2):
            in_head = (lane >> 6) == hh
            qm = jnp.where(in_head, q, jnp.zeros_like(q)) * jnp.asarray(scale, BF16)
            dom = jnp.where(in_head, dov, 0.0).astype(BF16)
            delta = jnp.sum(jnp.where(in_head, prod, 0.0), axis=1, keepdims=True)
            lse_h = lse_v[:, HEAD_DIM * hh:HEAD_DIM * hh + 1]
            sc = _dot_nt(qm, kb) + b_ref[hh]
            sc = jnp.where(kvalid, sc, NEG_INF)
            p = jnp.exp(sc - lse_h)
            dp = _dot_nt(dom, vb)
            ds = p * (dp - delta)
            ds_ref[hh] += ds
            dsb = ds.astype(BF16)
            dq_parts.append(_dot(dsb, kb) * scale)
            dk_new = dk_new + _dot_tn(dsb, qm)
            dv_new = dv_new + _dot_tn(p.astype(BF16), dom)
        dq_ref[...] = jnp.where(lane < HEAD_DIM, dq_parts[0], dq_parts[1]).astype(BF16)
        dk_acc[pl.ds(koff, K_BAND), :] += dk_new
        dv_acc[pl.ds(koff, K_BAND), :] += dv_new

        @pl.when(blk == nb - 1)
        def _():
            dk_ref[...] = dk_acc[pad:pad + s, :].astype(BF16)
            dv_ref[...] = dv_acc[pad:pad + s, :].astype(BF16)

    qblk = pl.BlockSpec((Q_BLOCK, 128), lambda p, b: (b, p))
    col = pl.BlockSpec((s, 128), lambda p, b: (0, p))
    shp = jax.ShapeDtypeStruct((s, ATTN_WIDTH), BF16)
    return _call(
        body, name="attn_bwd", grid=(ATTN_WIDTH // 128, nb),
        in_specs=[pl.BlockSpec((Q_BLOCK, 128), lambda p, b: (qb0 + b, p)),
                  pl.BlockSpec((s + pad, 128), lambda p, b: (0, 4 + p)),
                  pl.BlockSpec((s + pad, 128), lambda p, b: (0, 8 + p)),
                  pl.BlockSpec((None, 2, Q_BLOCK, K_BAND), lambda p, b: (l, p, 0, 0)),
                  qblk, qblk, qblk],
        out_specs=[qblk, col, col, pl.BlockSpec((2, Q_BLOCK, K_BAND), lambda p, b: (p, 0, 0))],
        out_shape=[shp, shp, shp, jax.ShapeDtypeStruct((N_HEADS, Q_BLOCK, K_BAND), F32)],
        scratch=[pltpu.VMEM((s + pad, 128), F32), pltpu.VMEM((s + pad, 128), F32)],
        args=[qkvp, qkvp, qkvp, biasm, o, lse, do], comm=comm)


def _dw_in(h, dproj, s):
    def body(a_ref, b_ref, o_ref):
        acc = _dot_tn(a_ref[...], b_ref[...])
        o_ref[0] = acc[:, 0:PROJ_SHARD].astype(BF16)
        o_ref[1] = acc[:, PROJ_SHARD:2 * PROJ_SHARD].astype(BF16)

    out, = _call(
        body, name="dw_in", grid=(4,),
        in_specs=[pl.BlockSpec((s, D_MODEL), lambda n: (0, 0)),
                  pl.BlockSpec((s, 2 * PROJ_SHARD), lambda n: (0, n))],
        out_specs=[pl.BlockSpec((2, D_MODEL, PROJ_SHARD), lambda n: (n, 0, 0))],
        out_shape=[jax.ShapeDtypeStruct((N_DEV, D_MODEL, PROJ_SHARD), BF16)], args=[h, dproj])
    return out


def _in_proj_bwd(dproj, win, x, g3, dres, l, s, tm, comm=None):
    def body(d_ref, w_ref, x_ref, g_ref, dres_ref, dx_ref, dg_ref):
        dh = _dot_nt(d_ref[:, 0:PROJ_SHARD], w_ref[0])
        for j in range(1, N_DEV):
            dh = dh + _dot_nt(d_ref[:, PROJ_SHARD * j:PROJ_SHARD * (j + 1)], w_ref[j])
        dx, dyn = _norm_bwd_rows(x_ref[...], g_ref[...], dh)
        dx_ref[...] = dres_ref[...] + dx
        _accum_cols(dg_ref, dyn, pl.program_id(0) == 0)

    row = pl.BlockSpec((tm, D_MODEL), lambda i: (i, 0))
    return _call(
        body, name="in_proj_bwd", grid=(s // tm,),
        in_specs=[pl.BlockSpec((tm, PROJ_WIDTH), lambda i: (i, 0)),
                  pl.BlockSpec((N_DEV, D_MODEL, PROJ_SHARD), lambda i: (0, 0, 0)),
                  row, pl.BlockSpec((None, 1, D_MODEL), lambda i: (l, 0, 0)), row],
        out_specs=[row, pl.BlockSpec((8, D_MODEL), lambda i: (0, 0))],
        out_shape=[jax.ShapeDtypeStruct((s, D_MODEL), F32), jax.ShapeDtypeStruct((8, D_MODEL), F32)],
        args=[dproj, win, x, g3, dres], comm=comm)


def _adamw(name, w, m, v, lands):
    groups, rows, cols = w.shape
    assert len(lands) == groups
    n_part = lands[0].shape[0]
    tr = _row_tile(rows, tuple(c for c in (512, 352, 256, 176, 128, 64, 32, 16, 8) if c * cols <= 256 * 1024))
    c1 = 1.0 - ADAM_B1 ** ADAM_STEP
    c2 = 1.0 - ADAM_B2 ** ADAM_STEP

    def body(w_ref, m_ref, v_ref, *rest):
        land_refs = rest[:groups]
        g_ref, d_ref, nm_ref, nv_ref = rest[groups:]
        grp = pl.program_id(0)
        for gi in range(groups):
            @pl.when(grp == gi)
            def _():
                l_ref = land_refs[gi]
                g = l_ref[0].astype(F32)
                for p in range(1, n_part):
                    g = g + l_ref[p].astype(F32)
                g_ref[...] = g
                m1 = ADAM_B1 * m_ref[...] + (1.0 - ADAM_B1) * g
                v1 = ADAM_B2 * v_ref[...] + (1.0 - ADAM_B2) * (g * g)
                nm_ref[...] = m1
                nv_ref[...] = v1
                d_ref[...] = -ADAM_LR * ((m1 / c1) / (jnp.sqrt(v1 / c2) + ADAM_EPS) + ADAM_WD * w_ref[...])

    blk = pl.BlockSpec((None, tr, cols), lambda g, i: (g, i, 0))
    shp = jax.ShapeDtypeStruct((groups, rows, cols), F32)

    def land_spec(gi):
        return pl.BlockSpec((n_part, tr, cols), lambda g, i: (0, jnp.where(g == gi, i, 0), 0))

    return _call(
        body, name=name, grid=(groups, rows // tr),
        in_specs=[blk, blk, blk] + [land_spec(gi) for gi in range(groups)],
        out_specs=[blk, blk, blk, blk], out_shape=[shp, shp, shp, shp],
        args=[w, m, v] + list(lands))


def _pack_small(rel, gco, gao, gpm, gqm, gpf, gqf):
    n_layers = rel.shape[0]
    relp = jnp.pad(rel, ((0, 0), (0, 0), (0, REL_PAD - rel.shape[2])))
    parts = [relp.reshape(n_layers * N_HEADS * REL_PAD // 128, 128)]
    parts += [a.reshape(-1, 128) for a in (gco, gao, gpm, gqm, gpf, gqf)]
    return jnp.concatenate(parts, axis=0)


def _unpack_small(p, n_layers):
    n_rel = n_layers * N_HEADS * REL_PAD // 128
    rel = p[:n_rel].reshape(n_layers, N_HEADS, REL_PAD)[:, :, :2 * REL_CLIP + 1]
    outs = [rel]
    r0 = n_rel
    for width in (CONV_WIDTH, ATTN_WIDTH, D_MODEL, D_MODEL, D_MODEL, D_MODEL):
        nr = n_layers * width // 128
        outs.append(p[r0:r0 + nr].reshape(n_layers, width))
        r0 += nr
    return outs


def kernel(x, w_in, w_conv, rel_bias, g_conv_out, g_attn_out, w_out, g_pre_mix, g_post_mix, g_pre_ffn, g_post_ffn, w_ffn_in, w_ffn_out, loss_target, m_w_in, m_w_conv, m_rel_bias, m_g_conv_out, m_g_attn_out, m_w_out, m_g_pre_mix, m_g_post_mix, m_g_pre_ffn, m_g_post_ffn, m_w_ffn_in, m_w_ffn_out, v_w_in, v_w_conv, v_rel_bias, v_g_conv_out, v_g_attn_out, v_w_out, v_g_pre_mix, v_g_post_mix, v_g_pre_ffn, v_g_post_ffn, v_w_ffn_in, v_w_ffn_out):
    n_layers = w_in.shape[0]
    s = x.shape[1]
    assert x.shape == (1, s, D_MODEL) and s % 1024 == 0
    assert w_in.shape == (n_layers, D_MODEL, PROJ_SHARD) and w_ffn_in.shape == (n_layers, D_MODEL, FF_SHARD)
    tm = 512
    tq = 1024 if s >= 2048 else 512
    tf = min(2048, s)
    x0 = x.reshape(s, D_MODEL)
    target = loss_target.reshape(s, D_MODEL)
    dev = _dev_index(lax.axis_index("x"), lax.axis_index("y"), lax.axis_index("c"))

    local_w = [_cast_bf16(w_in, "cast_w_in"), _cast_bf16(w_out, "cast_w_out"),
               _cast_bf16(w_ffn_in, "cast_w_ffn_in"), _cast_bf16(w_ffn_out, "cast_w_ffn_out")]
    wc_local = jnp.pad(jnp.transpose(w_conv, (0, 2, 1)).reshape(-1), (0, 1024 - n_layers * 3 * 64)).reshape(8, 128)
    *w0, wc_g = _comm_only("gather_layer0", _Gather([(a, 0) for a in local_w] + [(wc_local, None)]))
    weights = [None] * n_layers
    weights[0] = list(w0)
    wc_full = wc_g.reshape(N_DEV, 1024)[:, :n_layers * 3 * 64].reshape(N_DEV, n_layers, 3, 64)
    wc_full = jnp.transpose(wc_full, (1, 2, 0, 3)).reshape(n_layers, 3, CONV_WIDTH)
    wc_full = jnp.pad(wc_full, ((0, 0), (0, 5), (0, 0)))

    g3 = {k: v.reshape(n_layers, 1, -1) for k, v in dict(
        conv=g_conv_out, attn=g_attn_out, pre_mix=g_pre_mix, post_mix=g_post_mix,
        pre_ffn=g_pre_ffn, post_ffn=g_post_ffn).items()}
    biasm = _bias_build(jnp.pad(rel_bias, ((0, 0), (0, 0), (0, REL_PAD - rel_bias.shape[2]))))

    def views(l):
        win, wout, wfin, wfout = weights[l]
        return win, wout.reshape(D_MODEL, D_MODEL), wfin, wfout.reshape(4, FF_SHARD, D_MODEL)

    saved = []
    xl = x0
    h = _norm_cast(x0, g3["pre_mix"], 0, tm)
    for l in range(n_layers):
        win, wout, wfin, wfo4 = views(l)
        nxt = l + 1 < n_layers
        pc = _in_proj(h, win, s, tq, 0)
        qkvp = _in_proj(h, win, s, tq, 1)
        ync = _conv_fwd(pc, wc_full, g3["conv"], l, s, tm)
        o, lse, yna, *got_fin = _attn_fwd(qkvp, biasm, g3["attn"], l, s, tq,
                                          comm=_Gather([(local_w[2], l + 1)]) if nxt else None)
        z, xm, h2 = _out_proj_fwd(ync, yna, wout, xl, g3["post_mix"], g3["pre_ffn"], l, s, tq)
        gu, act, *got_io = _ffn_in_fwd(h2, wfin, s, tf,
                                             comm=_Gather([(local_w[0], l + 1), (local_w[1], l + 1)]) if nxt else None)
        l_next = min(l + 1, n_layers - 1)
        f, xo, h_next, *got_fout = _ffn_out_fwd(act, wfo4, xm, g3["post_ffn"], g3["pre_mix"], l, l_next, s, tm,
                                                comm=_Gather([(local_w[3], l + 1)]) if nxt else None)
        if nxt:
            weights[l + 1] = [got_io[0], got_io[1], got_fin[0], got_fout[0]]
        saved.append(dict(x=xl, h=h, pc=pc, qkvp=qkvp, ync=ync, yna=yna, o=o, lse=lse, z=z, xm=xm,
                          h2=h2, gu=gu, act=act, f=f))
        xl, h = xo, h_next

    dx, sq = _loss_grad(xl, target, s, tm)
    loss = lax.psum(jnp.sum(sq) * (0.5 / D_MODEL), ("x", "y", "c"))

    lands = dict(win=[None] * n_layers, wout=[None] * n_layers, wfin=[None] * n_layers, wfout=[None] * n_layers)
    small = {k: [None] * n_layers for k in ("gco", "gao", "gpm", "gqm", "gpf", "gqf", "wc")}
    ds_all = [None] * n_layers
    pending_win = None
    for l in reversed(range(n_layers)):
        sv = saved[l]
        win, wout, wfin, wfo4 = views(l)
        df, dg = _norm_bwd(sv["f"], g3["post_ffn"], dx, l, s, tm)
        small["gqf"][l] = dg[0]
        comm = _Scatter([(pending_win[1], False)]) if pending_win else None
        dgu, *got = _ffn_out_bwd(df, wfo4, sv["gu"], s, tf, comm=comm)
        if pending_win:
            lands["win"][pending_win[0]] = got[0]
        d_wfout = _dw_ffn_out(sv["act"], df, s).reshape(N_DEV, FFO_SHARD, D_MODEL)
        d_wfin = _dw_ffn_in(sv["h2"], dgu, s)
        dxm, dz, dg_pre_ffn, dg_post_mix, lands["wfout"][l] = _ffn_in_bwd(
            dgu, wfin, sv["xm"], g3["pre_ffn"], dx, sv["z"], g3["post_mix"], l, s, tm // 2,
            comm=_Scatter([(d_wfout, False)]))
        small["gpf"][l] = dg_pre_ffn[0]
        small["gqm"][l] = dg_post_mix[0]
        d_wout = _dw_out(sv["ync"], sv["yna"], dz, s).reshape(N_DEV, D_MODEL // N_DEV, D_MODEL)
        dyc, do, dg_attn = _out_proj_bwd(dz, wout, sv["o"], g3["attn"], l, s, tq)
        small["gao"][l] = dg_attn[0]
        dpc, dwc, dg_conv = _conv_bwd(sv["pc"], dyc, wc_full, g3["conv"], l, s, tm)
        small["wc"][l] = dwc[0:3]
        small["gco"][l] = dg_conv[0]
        dq, dk, dv, ds_all[l], lands["wfin"][l] = _attn_bwd(
            sv["qkvp"], biasm, sv["o"], sv["lse"], do, l, s, tq, comm=_Scatter([(d_wfin, False)]))
        dproj = jnp.concatenate([dpc, dq, dk, dv], axis=1)
        pending_win = (l, _dw_in(sv["h"], dproj, s))
        dx, dg_pre_mix, lands["wout"][l] = _in_proj_bwd(
            dproj, win, sv["x"], g3["pre_mix"], dxm, l, s, tm, comm=_Scatter([(d_wout, False)]))
        small["gpm"][l] = dg_pre_mix[0]
    grad_x = dx.reshape(1, s, D_MODEL)

    d_rel = _bias_bwd(jnp.stack(ds_all))[:, :, :2 * REL_CLIP + 1]
    small_vec = jnp.concatenate(
        [_pack_small(d_rel, *[jnp.stack(small[k]) for k in ("gco", "gao", "gpm", "gqm", "gpf", "gqf")]),
         jnp.stack(small["wc"]).reshape(-1, 128)], axis=0)
    small_vec = jnp.pad(small_vec, ((0, (-small_vec.shape[0]) % 8), (0, 0)))
    lands["win"][pending_win[0]], land_small = _comm_only(
        "exchange_last", _Scatter([(pending_win[1], False), (small_vec, True)]))

    r_in = _adamw("adamw_w_in", w_in, m_w_in, v_w_in, lands["win"])
    r_out = _adamw("adamw_w_out", w_out, m_w_out, v_w_out, lands["wout"])
    r_fin = _adamw("adamw_w_ffn_in", w_ffn_in, m_w_ffn_in, v_w_ffn_in, lands["wfin"])
    r_fout = _adamw("adamw_w_ffn_out", w_ffn_out, m_w_ffn_out, v_w_ffn_out, lands["wfout"])

    n_rep = 64 * n_layers
    rep = _adamw(
        "adamw_replicated",
        _pack_small(rel_bias, g_conv_out, g_attn_out, g_pre_mix, g_post_mix, g_pre_ffn, g_post_ffn)[None],
        _pack_small(m_rel_bias, m_g_conv_out, m_g_attn_out, m_g_pre_mix, m_g_post_mix, m_g_pre_ffn, m_g_post_ffn)[None],
        _pack_small(v_rel_bias, v_g_conv_out, v_g_attn_out, v_g_pre_mix, v_g_post_mix, v_g_pre_ffn, v_g_post_ffn)[None],
        [land_small[:, :n_rep]])
    rep = [_unpack_small(t[0], n_layers) for t in rep]

    wc_rows = n_layers * 3 * CONV_WIDTH // 128
    zeros_wc = jnp.zeros((1, wc_rows, 128), F32)
    g_wc_full = _adamw("sum_w_conv", zeros_wc, zeros_wc, zeros_wc, [land_small[:, n_rep:n_rep + wc_rows]])[0]
    g_wc_full = g_wc_full.reshape(n_layers, 3, CONV_WIDTH)
    g_wc = lax.dynamic_slice_in_dim(g_wc_full, dev * (CONV_WIDTH // N_DEV), CONV_WIDTH // N_DEV, axis=2)
    g_wc = jnp.transpose(g_wc, (0, 2, 1))

    def tiny(a):
        flat = a.reshape(-1)
        return jnp.pad(flat, (0, (-flat.shape[0]) % 1024)).reshape(1, -1, 128)

    r_wc = _adamw("adamw_w_conv", tiny(w_conv), tiny(m_w_conv), tiny(v_w_conv), [tiny(g_wc)])
    r_wc = [t.reshape(-1)[:w_conv.size].reshape(w_conv.shape) for t in r_wc]

    def leaf(kind):
        return [r_in[kind], r_wc[kind], rep[kind][0], rep[kind][1], rep[kind][2], r_out[kind],
                rep[kind][3], rep[kind][4], rep[kind][5], rep[kind][6], r_fin[kind], r_fout[kind]]

    return (loss, grad_x, *leaf(0), *leaf(1), *leaf(2), *leaf(3))
```

```python
import math

import jax
import jax.numpy as jnp
from jax import lax
from jax.experimental import pallas as pl
from jax.experimental.pallas import tpu as pltpu

F32 = jnp.float32
BF16 = jnp.bfloat16

D_MODEL = 1024
N_DEV = 8
CHUNK = 64
N_LEFT_CHUNKS = 8
CONV_WIDTH = 512
ATTN_WIDTH = 512
HEAD_DIM = 64
N_HEADS = 8
REL_CLIP = 128
REL_PAD = 384
PROJ_WIDTH = 3072
PROJ_SHARD = PROJ_WIDTH // N_DEV
D_FF = 2816
FF_SHARD = 2 * D_FF // N_DEV
FFO_SHARD = D_FF // N_DEV
EPS = 1e-6
NEG_INF = -1e30
Q_BLOCK = 4 * CHUNK
K_BAND = Q_BLOCK + N_LEFT_CHUNKS * CHUNK
LEFT = N_LEFT_CHUNKS * CHUNK
TOEP = 1024

ADAM_LR = 0.001
ADAM_B1 = 0.9
ADAM_B2 = 0.999
ADAM_EPS = 1e-08
ADAM_WD = 0.01
ADAM_STEP = 10

VMEM_LIMIT = 52 * 1024 * 1024
MESH = pl.DeviceIdType.MESH
ANY = pl.BlockSpec(memory_space=pl.ANY)

NT = (((1,), (1,)), ((), ()))
TN = (((0,), (0,)), ((), ()))


def _dot(a, b):
    return jnp.dot(a, b, preferred_element_type=F32)


def _dot_nt(a, b):
    return lax.dot_general(a, b, NT, preferred_element_type=F32)


def _dot_tn(a, b):
    return lax.dot_general(a, b, TN, preferred_element_type=F32)


def _rstd(v):
    return lax.rsqrt(jnp.mean(v * v, axis=-1, keepdims=True) + EPS)


def _group_matrix():
    r = lax.broadcasted_iota(jnp.int32, (128, 128), 0) >> 6
    c = lax.broadcasted_iota(jnp.int32, (128, 128), 1) >> 6
    return jnp.where(r == c, 1.0, 0.0).astype(BF16)


def _group_mean(v, gmat):
    hi = v.astype(BF16)
    lo = (v - hi.astype(F32)).astype(BF16)
    return (_dot(hi, gmat) + _dot(lo, gmat)) * (1.0 / HEAD_DIM)


def _split3(v):
    hi = v.astype(BF16)
    r1 = v - hi.astype(F32)
    mid = r1.astype(BF16)
    lo = (r1 - mid.astype(F32)).astype(BF16)
    return hi, mid, lo


def _row_tile(rows, cands=(1024, 512, 704, 256, 128, 64, 32, 16)):
    for c in cands:
        if rows % c == 0:
            return c
    return rows


def _dev_index(px, py, pc):
    return 4 * px + 2 * py + pc


def _when(cond):
    if cond is True:
        return lambda fn: fn()
    return pl.when(cond)


def _phases(grid):
    def phases():
        if not grid:
            return True, True, True
        lin = pl.program_id(0)
        for a in range(1, len(grid)):
            lin = lin * grid[a] + pl.program_id(a)
        total = math.prod(grid)
        return lin == 0, lin == (3 * total) // 4, lin == total - 1
    return phases


class _Gather:
    def __init__(self, items):
        self.items = items
        self.args = [a for a, _ in items]
        n = len(items)
        self.out_shape = [jax.ShapeDtypeStruct((N_DEV,) + (a.shape if lay is None else a.shape[1:]), a.dtype)
                          for a, lay in items]
        self.scratch = [pltpu.SemaphoreType.DMA((n, 7)), pltpu.SemaphoreType.DMA((n, 7)),
                        pltpu.SemaphoreType.DMA((n,))]

    def _ctx(self, ins, outs, sems):
        send_sems, recv_sems, local_sems = sems
        x, y, c = lax.axis_index("x"), lax.axis_index("y"), lax.axis_index("c")
        chips = [(1 - x, y), (x, 1 - y), (1 - x, 1 - y)]

        def src(k):
            lay = self.items[k][1]
            return ins[k] if lay is None else ins[k].at[lay]

        def copy(k, s, idx, to, from_src=False):
            return pltpu.make_async_remote_copy(
                src_ref=src(k) if from_src else outs[k].at[idx], dst_ref=outs[k].at[idx],
                send_sem=send_sems.at[k, s], recv_sem=recv_sems.at[k, s],
                device_id=to, device_id_type=MESH)

        def local(k):
            return pltpu.make_async_copy(src(k), outs[k].at[_dev_index(x, y, c)], local_sems.at[k])

        return x, y, c, chips, copy, local

    def start(self, ins, outs, sems, cond):
        n = len(self.items)

        @_when(cond)
        def _():
            x, y, c, chips, copy, local = self._ctx(ins, outs, sems)
            me = _dev_index(x, y, c)
            for k in range(n):
                local(k).start()
                copy(k, 0, me, (x, y, 1 - c), from_src=True).start()
                for j, chip in enumerate(chips):
                    copy(k, 1 + j, me, (chip[0], chip[1], c), from_src=True).start()

    def forward(self, ins, outs, sems, cond):
        n = len(self.items)

        @_when(cond)
        def _():
            x, y, c, chips, copy, local = self._ctx(ins, outs, sems)
            for j, chip in enumerate(chips):
                idx = _dev_index(chip[0], chip[1], c)
                for k in range(n):
                    copy(k, 1 + j, idx, (x, y, c)).wait_recv()
                    copy(k, 4 + j, idx, (x, y, 1 - c)).start()

    def finish(self, ins, outs, sems, cond):
        n = len(self.items)

        @_when(cond)
        def _():
            x, y, c, chips, copy, local = self._ctx(ins, outs, sems)
            me = _dev_index(x, y, c)
            for k in range(n):
                copy(k, 0, _dev_index(x, y, 1 - c), (x, y, c)).wait_recv()
            for j, chip in enumerate(chips):
                idx = _dev_index(chip[0], chip[1], 1 - c)
                for k in range(n):
                    copy(k, 4 + j, idx, (x, y, c)).wait_recv()
            for k in range(n):
                for s in range(4):
                    copy(k, s, me, (x, y, c), from_src=True).wait_send()
                for j, chip in enumerate(chips):
                    copy(k, 4 + j, _dev_index(chip[0], chip[1], c), (x, y, c)).wait_send()
                local(k).wait()


_PEER_FLIPS = [(0, 0, 1), (1, 0, 0), (0, 1, 0), (1, 1, 0), (1, 0, 1), (0, 1, 1), (1, 1, 1)]


class _Scatter:
    def __init__(self, items):
        self.items = items
        self.args = [a for a, _ in items]
        n = len(items)
        self.out_shape = [jax.ShapeDtypeStruct((N_DEV,) + (a.shape if whole else a.shape[1:]), a.dtype)
                          for a, whole in items]
        self.scratch = [pltpu.SemaphoreType.DMA((n, 7)), pltpu.SemaphoreType.DMA((n, 7)),
                        pltpu.SemaphoreType.DMA((n,))]

    def _ctx(self, ins, outs, sems):
        send_sems, recv_sems, local_sems = sems
        x, y, c = lax.axis_index("x"), lax.axis_index("y"), lax.axis_index("c")
        me = _dev_index(x, y, c)

        def src(k, dest_idx):
            return ins[k] if self.items[k][1] else ins[k].at[dest_idx]

        def send(k, s):
            fx, fy, fc = _PEER_FLIPS[s]
            px, py, pc = x ^ fx, y ^ fy, c ^ fc
            return pltpu.make_async_remote_copy(
                src_ref=src(k, _dev_index(px, py, pc)), dst_ref=outs[k].at[me],
                send_sem=send_sems.at[k, s], recv_sem=recv_sems.at[k, s],
                device_id=(px, py, pc), device_id_type=MESH)

        def recv(k, s):
            fx, fy, fc = _PEER_FLIPS[s]
            pidx = _dev_index(x ^ fx, y ^ fy, c ^ fc)
            return pltpu.make_async_remote_copy(
                src_ref=outs[k].at[pidx], dst_ref=outs[k].at[pidx],
                send_sem=send_sems.at[k, s], recv_sem=recv_sems.at[k, s],
                device_id=(x, y, c), device_id_type=MESH)

        def local(k):
            return pltpu.make_async_copy(src(k, me), outs[k].at[me], local_sems.at[k])

        return send, recv, local

    def start(self, ins, outs, sems, cond):
        n = len(self.items)

        @_when(cond)
        def _():
            send, recv, local = self._ctx(ins, outs, sems)
            for k in range(n):
                local(k).start()
            for s in range(7):
                for k in range(n):
                    send(k, s).start()

    def forward(self, ins, outs, sems, cond):
        pass

    def finish(self, ins, outs, sems, cond):
        n = len(self.items)

        @_when(cond)
        def _():
            send, recv, local = self._ctx(ins, outs, sems)
            for s in range(7):
                for k in range(n):
                    recv(k, s).wait_recv()
            for s in range(7):
                for k in range(n):
                    send(k, s).wait_send()
            for k in range(n):
                local(k).wait()


def _call(body, *, name, grid, in_specs, out_specs, out_shape, args, scratch=(), comm=None):
    n_hi, n_ho, n_hs = len(args), len(out_shape), len(scratch)
    c_args = list(comm.args) if comm else []
    c_out = list(comm.out_shape) if comm else []
    c_scr = list(comm.scratch) if comm else []
    phases = _phases(grid)

    def kern(*refs):
        cuts = [n_hi, len(c_args), n_ho, len(c_out), n_hs, len(c_scr)]
        parts, pos = [], 0
        for n in cuts:
            parts.append(refs[pos:pos + n])
            pos += n
        hi, ci, ho, co, hs, cs = parts
        if comm:
            first, mid, last = phases()
            comm.start(ci, co, cs, first)
            comm.forward(ci, co, cs, mid)
        body(*hi, *ho, *hs)
        if comm:
            comm.finish(ci, co, cs, last)

    sem = ("arbitrary",) * len(grid) if grid else None
    return pl.pallas_call(
        kern, name=name, grid=grid,
        in_specs=list(in_specs) + [ANY] * len(c_args),
        out_specs=list(out_specs) + [ANY] * len(c_out),
        out_shape=list(out_shape) + c_out,
        scratch_shapes=list(scratch) + c_scr,
        compiler_params=pltpu.CompilerParams(dimension_semantics=sem, vmem_limit_bytes=VMEM_LIMIT),
    )(*args, *c_args)


def _comm_only(name, comm):
    return _call(lambda: None, name=name, grid=(), in_specs=[], out_specs=[], out_shape=[], args=[], comm=comm)


def _cast_bf16(x, name):
    shape = x.shape
    x2 = x.reshape(-1, shape[-1])
    rows, cols = x2.shape
    tr = _row_tile(rows)

    def body(x_ref, o_ref):
        o_ref[...] = x_ref[...].astype(BF16)

    blk = pl.BlockSpec((tr, cols), lambda i: (i, 0))
    out, = _call(body, name=name, grid=(rows // tr,), in_specs=[blk], out_specs=[blk],
                 out_shape=[jax.ShapeDtypeStruct((rows, cols), BF16)], args=[x2])
    return out.reshape(shape)


def _norm_cast(x, g3, l, tm):
    s = x.shape[0]

    def body(x_ref, g_ref, o_ref):
        v = x_ref[...]
        o_ref[...] = (v * _rstd(v) * g_ref[...]).astype(BF16)

    row = pl.BlockSpec((tm, D_MODEL), lambda i: (i, 0))
    out, = _call(body, name="norm_cast", grid=(s // tm,),
                 in_specs=[row, pl.BlockSpec((None, 1, D_MODEL), lambda i: (l, 0, 0))], out_specs=[row],
                 out_shape=[jax.ShapeDtypeStruct((s, D_MODEL), BF16)], args=[x, g3])
    return out


def _in_proj(h, win, s, tq, part, comm=None):
    pad = part
    dtype = BF16 if part else F32

    def body(a_ref, b_ref, o_ref):
        def compute():
            a = a_ref[...]
            for j in range(4):
                o_ref[:, PROJ_SHARD * j:PROJ_SHARD * (j + 1)] = _dot(a, b_ref[j]).astype(dtype)

        if pad:
            i = pl.program_id(0)

            @pl.when(i == 0)
            def _():
                o_ref[...] = jnp.zeros(o_ref.shape, dtype)

            pl.when(i > 0)(compute)
        else:
            compute()

    return _call(
        body, name="in_proj_qkv" if part else "in_proj_conv", grid=(s // tq + pad,),
        in_specs=[pl.BlockSpec((tq, D_MODEL), lambda i: (jnp.maximum(i - pad, 0), 0)),
                  pl.BlockSpec((4, D_MODEL, PROJ_SHARD), lambda i: (part, 0, 0))],
        out_specs=[pl.BlockSpec((tq, 4 * PROJ_SHARD), lambda i: (i, 0))],
        out_shape=[jax.ShapeDtypeStruct((s + pad * tq, PROJ_WIDTH // 2), dtype)], args=[h, win], comm=comm)


def _conv_fwd(pc, wc, g3, l, s, tr):
    hb = tr // 8

    def body(pc_ref, prev_ref, wc_ref, g_ref, o_ref):
        i = pl.program_id(0)
        gmat = _group_matrix()
        for j in range(CONV_WIDTH // 128):
            c0, c1, c2 = 128 * j, CONV_WIDTH + 128 * j, 2 * CONV_WIDTH + 128 * j
            hc = pc_ref[:, c0:c0 + 128]
            bg = pc_ref[:, c1:c1 + 128]
            cg = pc_ref[:, c2:c2 + 128]
            u_prev = jnp.where(i > 0, prev_ref[:, c2:c2 + 128] * prev_ref[:, c0:c0 + 128], 0.0)
            u = cg * hc
            full = jnp.concatenate([u_prev, u], axis=0)
            u1 = pltpu.roll(full, 1, 0)[8:]
            u2 = pltpu.roll(full, 2, 0)[8:]
            out = (u2 * wc_ref[0:1, c0:c0 + 128] + u1 * wc_ref[1:2, c0:c0 + 128]
                   + u * wc_ref[2:3, c0:c0 + 128])
            yc = bg * out
            r = lax.rsqrt(_group_mean(yc * yc, gmat) + EPS)
            o_ref[:, c0:c0 + 128] = (yc * r * g_ref[:, c0:c0 + 128]).astype(BF16)

    out, = _call(
        body, name="conv_fwd", grid=(s // tr,),
        in_specs=[pl.BlockSpec((tr, 3 * CONV_WIDTH), lambda i: (i, 0)),
                  pl.BlockSpec((8, 3 * CONV_WIDTH), lambda i: (jnp.maximum(i * hb - 1, 0), 0)),
                  pl.BlockSpec((None, 8, CONV_WIDTH), lambda i: (l, 0, 0)),
                  pl.BlockSpec((None, 1, CONV_WIDTH), lambda i: (l, 0, 0))],
        out_specs=[pl.BlockSpec((tr, CONV_WIDTH), lambda i: (i, 0))],
        out_shape=[jax.ShapeDtypeStruct((s, CONV_WIDTH), BF16)], args=[pc, pc, wc, g3])
    return out


def _toeplitz_source():
    r_i = lax.broadcasted_iota(jnp.int32, (REL_PAD, TOEP), 0)
    m_i = lax.broadcasted_iota(jnp.int32, (REL_PAD, TOEP), 1)
    idx = jnp.clip((K_BAND - 1) - m_i, -REL_CLIP, REL_CLIP) + REL_CLIP
    return jnp.where(r_i == idx, 1.0, 0.0).astype(BF16)


def _bias_build(rbp):
    n_layers = rbp.shape[0]

    def body(rb_ref, o_ref, t_ref):
        pmat = _toeplitz_source()
        hi, mid, lo = _split3(rb_ref[...])
        t_ref[...] = _dot(hi, pmat) + _dot(mid, pmat) + _dot(lo, pmat)
        shift = (Q_BLOCK - 1) - lax.broadcasted_iota(jnp.int32, (Q_BLOCK, TOEP), 0)
        kk = lax.broadcasted_iota(jnp.int32, (Q_BLOCK, K_BAND), 1) >> 6
        qq = lax.broadcasted_iota(jnp.int32, (Q_BLOCK, K_BAND), 0) >> 6
        dchunk = kk - qq
        in_band = jnp.where(dchunk >= 0, jnp.where(dchunk <= N_LEFT_CHUNKS, 1, 0), 0) == 1
        for h in range(N_HEADS):
            b = jnp.broadcast_to(t_ref[pl.ds(h, 1), :], (Q_BLOCK, TOEP))
            for bit in range(8):
                rolled = pltpu.roll(b, TOEP - (1 << bit), 1)
                b = jnp.where(((shift >> bit) & 1) == 1, rolled, b)
            o_ref[h] = jnp.where(in_band, b[:, :K_BAND], NEG_INF)

    out, = _call(
        body, name="bias_build", grid=(n_layers,),
        in_specs=[pl.BlockSpec((None, N_HEADS, REL_PAD), lambda l: (l, 0, 0))],
        out_specs=[pl.BlockSpec((None, N_HEADS, Q_BLOCK, K_BAND), lambda l: (l, 0, 0, 0))],
        out_shape=[jax.ShapeDtypeStruct((n_layers, N_HEADS, Q_BLOCK, K_BAND), F32)],
        scratch=[pltpu.VMEM((N_HEADS, TOEP), F32)], args=[rbp])
    return out


def _bias_bwd(ds_sum):
    n_layers = ds_sum.shape[0]

    def body(ds_ref, o_ref, t_ref):
        pmat = _toeplitz_source()
        shift = (Q_BLOCK - 1) - lax.broadcasted_iota(jnp.int32, (Q_BLOCK, TOEP), 0)
        for h in range(N_HEADS):
            d = jnp.concatenate([ds_ref[h], jnp.zeros((Q_BLOCK, TOEP - K_BAND), F32)], axis=1)
            for bit in range(8):
                rolled = pltpu.roll(d, 1 << bit, 1)
                d = jnp.where(((shift >> bit) & 1) == 1, rolled, d)
            t_ref[pl.ds(h, 1), :] = jnp.sum(d, axis=0, keepdims=True)
        hi, mid, lo = _split3(t_ref[...])
        o_ref[...] = _dot_nt(hi, pmat) + _dot_nt(mid, pmat) + _dot_nt(lo, pmat)

    out, = _call(
        body, name="bias_bwd", grid=(n_layers,),
        in_specs=[pl.BlockSpec((None, N_HEADS, Q_BLOCK, K_BAND), lambda l: (l, 0, 0, 0))],
        out_specs=[pl.BlockSpec((None, N_HEADS, REL_PAD), lambda l: (l, 0, 0))],
        out_shape=[jax.ShapeDtypeStruct((n_layers, N_HEADS, REL_PAD), F32)],
        scratch=[pltpu.VMEM((N_HEADS, TOEP), F32)], args=[ds_sum])
    return out


def _attn_fwd(qkvp, biasm, g3, l, s, pad, comm=None):
    nb = s // Q_BLOCK
    qb0 = pad // Q_BLOCK
    scale = HEAD_DIM ** -0.5

    def body(q_ref, k_ref, v_ref, b_ref, g_ref, o_ref, lse_ref, yn_ref):
        blk = pl.program_id(1)
        koff = pl.multiple_of(blk * Q_BLOCK + (pad - LEFT), Q_BLOCK)
        q = q_ref[...]
        kb = k_ref[pl.ds(koff, K_BAND), :]
        vb = v_ref[pl.ds(koff, K_BAND), :]
        lane = lax.broadcasted_iota(jnp.int32, (1, 128), 1)
        kpos = lax.broadcasted_iota(jnp.int32, (Q_BLOCK, K_BAND), 1) + (blk * Q_BLOCK - LEFT)
        kvalid = kpos >= 0
        outs, lses = [], []
        for hh in range(2):
            in_head = (lane >> 6) == hh
            qm = jnp.where(in_head, q, jnp.zeros_like(q)) * jnp.asarray(scale, BF16)
            sc = _dot_nt(qm, kb) + b_ref[hh]
            sc = jnp.where(kvalid, sc, NEG_INF)
            m = jnp.max(sc, axis=1, keepdims=True)
            e = jnp.exp(sc - m)
            den = jnp.sum(e, axis=1, keepdims=True)
            outs.append(_dot(e.astype(BF16), vb) * (1.0 / den))
            lses.append(m + jnp.log(den))
        first = lane < HEAD_DIM
        o = jnp.where(first, outs[0], outs[1])
        o_ref[...] = o
        lse_ref[...] = jnp.where(first, lses[0], lses[1])
        r = lax.rsqrt(_group_mean(o * o, _group_matrix()) + EPS)
        yn_ref[...] = (o * r * g_ref[...]).astype(BF16)

    blk_out = pl.BlockSpec((Q_BLOCK, 128), lambda p, b: (b, p))
    return _call(
        body, name="attn_fwd", grid=(ATTN_WIDTH // 128, nb),
        in_specs=[pl.BlockSpec((Q_BLOCK, 128), lambda p, b: (qb0 + b, p)),
                  pl.BlockSpec((s + pad, 128), lambda p, b: (0, 4 + p)),
                  pl.BlockSpec((s + pad, 128), lambda p, b: (0, 8 + p)),
                  pl.BlockSpec((None, 2, Q_BLOCK, K_BAND), lambda p, b: (l, p, 0, 0)),
                  pl.BlockSpec((None, 1, 128), lambda p, b: (l, 0, p))],
        out_specs=[blk_out, blk_out, blk_out],
        out_shape=[jax.ShapeDtypeStruct((s, ATTN_WIDTH), F32),
                   jax.ShapeDtypeStruct((s, ATTN_WIDTH), F32),
                   jax.ShapeDtypeStruct((s, ATTN_WIDTH), BF16)],
        args=[qkvp, qkvp, qkvp, biasm, g3], comm=comm)


def _out_proj_fwd(ync, yna, wout, x, g_post3, g_next3, l, s, tm):
    half = D_MODEL // 2

    def body(a1_ref, a2_ref, w_ref, x_ref, gp_ref, gn_ref, z_ref, xm_ref, h_ref):
        z = _dot(a1_ref[...], w_ref[0:half, :]) + _dot(a2_ref[...], w_ref[half:D_MODEL, :])
        z_ref[...] = z
        xm = x_ref[...] + z * _rstd(z) * gp_ref[...]
        xm_ref[...] = xm
        h_ref[...] = (xm * _rstd(xm) * gn_ref[...]).astype(BF16)

    row = pl.BlockSpec((tm, D_MODEL), lambda i: (i, 0))
    gain = pl.BlockSpec((None, 1, D_MODEL), lambda i: (l, 0, 0))
    return _call(
        body, name="out_proj_fwd", grid=(s // tm,),
        in_specs=[pl.BlockSpec((tm, half), lambda i: (i, 0)), pl.BlockSpec((tm, half), lambda i: (i, 0)),
                  pl.BlockSpec((D_MODEL, D_MODEL), lambda i: (0, 0)), row, gain, gain],
        out_specs=[row, row, row],
        out_shape=[jax.ShapeDtypeStruct((s, D_MODEL), F32), jax.ShapeDtypeStruct((s, D_MODEL), F32),
                   jax.ShapeDtypeStruct((s, D_MODEL), BF16)],
        args=[ync, yna, wout, x, g_post3, g_next3])


def _ffn_in_fwd(h2, wfin, s, tm, comm=None):
    def body(h_ref, wg_ref, wu_ref, gu_ref, act_ref):
        h = h_ref[...]
        gate = _dot_nt(h, wg_ref[...])
        up = _dot_nt(h, wu_ref[...])
        gu_ref[0] = gate.astype(BF16)
        gu_ref[1] = up.astype(BF16)
        act_ref[...] = (gate * (1.0 / (1.0 + jnp.exp(-gate))) * up).astype(BF16)

    return _call(
        body, name="ffn_in_fwd", grid=(4, s // tm),
        in_specs=[pl.BlockSpec((tm, D_MODEL), lambda b, i: (i, 0)),
                  pl.BlockSpec((None, FF_SHARD, D_MODEL), lambda b, i: (b, 0, 0)),
                  pl.BlockSpec((None, FF_SHARD, D_MODEL), lambda b, i: (4 + b, 0, 0))],
        out_specs=[pl.BlockSpec((None, 2, tm, FF_SHARD), lambda b, i: (b, 0, i, 0)),
                   pl.BlockSpec((None, tm, FF_SHARD), lambda b, i: (b, i, 0))],
        out_shape=[jax.ShapeDtypeStruct((4, 2, s, FF_SHARD), BF16), jax.ShapeDtypeStruct((4, s, FF_SHARD), BF16)],
        args=[h2, wfin, wfin], comm=comm)


def _ffn_out_fwd(act, wfo4, xm, g_post3, g_next3, l, l_next, s, tm, comm=None):
    def body(a_ref, w_ref, x_ref, gp_ref, gn_ref, f_ref, xo_ref, h_ref):
        f = _dot(a_ref[0], w_ref[0])
        for b in range(1, 4):
            f = f + _dot(a_ref[b], w_ref[b])
        f_ref[...] = f
        xo = x_ref[...] + f * _rstd(f) * gp_ref[...]
        xo_ref[...] = xo
        h_ref[...] = (xo * _rstd(xo) * gn_ref[...]).astype(BF16)

    row = pl.BlockSpec((tm, D_MODEL), lambda i: (i, 0))
    return _call(
        body, name="ffn_out_fwd", grid=(s // tm,),
        in_specs=[pl.BlockSpec((4, tm, FF_SHARD), lambda i: (0, i, 0)),
                  pl.BlockSpec((4, FF_SHARD, D_MODEL), lambda i: (0, 0, 0)), row,
                  pl.BlockSpec((None, 1, D_MODEL), lambda i: (l, 0, 0)),
                  pl.BlockSpec((None, 1, D_MODEL), lambda i: (l_next, 0, 0))],
        out_specs=[row, row, row],
        out_shape=[jax.ShapeDtypeStruct((s, D_MODEL), F32), jax.ShapeDtypeStruct((s, D_MODEL), F32),
                   jax.ShapeDtypeStruct((s, D_MODEL), BF16)],
        args=[act, wfo4, xm, g_post3, g_next3], comm=comm)


def _loss_grad(xf, target, f, g3, l, s, tm):
    def body(x_ref, t_ref, f_ref, g_ref, dx_ref, sq_ref, df_ref, dg_ref):
        i = pl.program_id(0)
        err = x_ref[...] - t_ref[...]
        dx = err * (1.0 / D_MODEL)
        dx_ref[...] = dx
        df, dyn = _norm_bwd_rows(f_ref[...], g_ref[...], dx)
        df_ref[...] = df.astype(BF16)
        _accum_cols(dg_ref, dyn, i == 0)
        cs = jnp.sum(err * err, axis=0, keepdims=True)
        part = cs[:, 0:128]
        for k in range(1, D_MODEL // 128):
            part = part + cs[:, 128 * k:128 * (k + 1)]

        @pl.when(i == 0)
        def _():
            sq_ref[...] = jnp.zeros(sq_ref.shape, F32)

        sq_ref[0:1, :] += part

    row = pl.BlockSpec((tm, D_MODEL), lambda i: (i, 0))
    return _call(
        body, name="loss_grad", grid=(s // tm,),
        in_specs=[row, row, row, pl.BlockSpec((None, 1, D_MODEL), lambda i: (l, 0, 0))],
        out_specs=[row, pl.BlockSpec((8, 128), lambda i: (0, 0)), row, pl.BlockSpec((8, D_MODEL), lambda i: (0, 0))],
        out_shape=[jax.ShapeDtypeStruct((s, D_MODEL), F32), jax.ShapeDtypeStruct((8, 128), F32),
                   jax.ShapeDtypeStruct((s, D_MODEL), BF16), jax.ShapeDtypeStruct((8, D_MODEL), F32)],
        args=[xf, target, f, g3])


def _norm_bwd_rows(v, g, dy):
    r = _rstd(v)
    vn = v * r
    gd = dy * g
    dv = r * (gd - vn * jnp.mean(vn * gd, axis=-1, keepdims=True))
    return dv, dy * vn


def _accum_cols(ref, val, first):
    @pl.when(first)
    def _():
        ref[...] = jnp.zeros(ref.shape, F32)

    ref[0:1, :] += jnp.sum(val, axis=0, keepdims=True)


def _ffn_out_bwd(df, wfo4, gu, s, tm, comm=None):
    def body(df_ref, w_ref, gu_ref, dgu_ref):
        da = _dot_nt(df_ref[...], w_ref[...])
        g = gu_ref[0].astype(F32)
        u = gu_ref[1].astype(F32)
        sg = 1.0 / (1.0 + jnp.exp(-g))
        dgu_ref[0] = (da * u * (sg * (1.0 + g * (1.0 - sg)))).astype(BF16)
        dgu_ref[1] = (da * (g * sg)).astype(BF16)

    blk = pl.BlockSpec((None, 2, tm, FF_SHARD), lambda b, i: (b, 0, i, 0))
    return _call(
        body, name="ffn_out_bwd", grid=(4, s // tm),
        in_specs=[pl.BlockSpec((tm, D_MODEL), lambda b, i: (i, 0)),
                  pl.BlockSpec((None, FF_SHARD, D_MODEL), lambda b, i: (b, 0, 0)), blk],
        out_specs=[blk], out_shape=[jax.ShapeDtypeStruct((4, 2, s, FF_SHARD), BF16)],
        args=[df, wfo4, gu], comm=comm)


def _dw_ffn_out(act, df, s):
    def body(a_ref, b_ref, o_ref):
        o_ref[...] = _dot_tn(a_ref[...], b_ref[...]).astype(BF16)

    out, = _call(
        body, name="dw_ffn_out", grid=(4,),
        in_specs=[pl.BlockSpec((None, s, FF_SHARD), lambda n: (n, 0, 0)),
                  pl.BlockSpec((s, D_MODEL), lambda n: (0, 0))],
        out_specs=[pl.BlockSpec((FF_SHARD, D_MODEL), lambda n: (n, 0))],
        out_shape=[jax.ShapeDtypeStruct((D_FF, D_MODEL), BF16)], args=[act, df])
    return out


def _dw_ffn_in(h2, dgu, s):
    def body(a_ref, b_ref, o_ref):
        o_ref[...] = _dot_tn(b_ref[...], a_ref[...]).astype(BF16)

    out, = _call(
        body, name="dw_ffn_in", grid=(N_DEV,),
        in_specs=[pl.BlockSpec((s, D_MODEL), lambda n: (0, 0)),
                  pl.BlockSpec((None, None, s, FF_SHARD), lambda n: (n % 4, n // 4, 0, 0))],
        out_specs=[pl.BlockSpec((None, FF_SHARD, D_MODEL), lambda n: (n, 0, 0))],
        out_shape=[jax.ShapeDtypeStruct((N_DEV, FF_SHARD, D_MODEL), BF16)], args=[h2, dgu])
    return out


def _ffn_in_bwd(dgu, wfin, xm, g_pre3, dres, z, g_post3, l, s, tm, comm=None):
    def body(d_ref, w_ref, xm_ref, gp_ref, dres_ref, z_ref, gq_ref, dxm_ref, dz_ref, dgp_ref, dgq_ref):
        first = pl.program_id(0) == 0
        dh = _dot(d_ref[0, 0], w_ref[0])
        for j in range(1, N_DEV):
            dh = dh + _dot(d_ref[j % 4, j // 4], w_ref[j])
        dx, dyn = _norm_bwd_rows(xm_ref[...], gp_ref[...], dh)
        dxm = dres_ref[...] + dx
        dxm_ref[...] = dxm
        _accum_cols(dgp_ref, dyn, first)
        dz, dyn2 = _norm_bwd_rows(z_ref[...], gq_ref[...], dxm)
        dz_ref[...] = dz.astype(BF16)
        _accum_cols(dgq_ref, dyn2, first)

    row = pl.BlockSpec((tm, D_MODEL), lambda i: (i, 0))
    gain = pl.BlockSpec((None, 1, D_MODEL), lambda i: (l, 0, 0))
    dgs = pl.BlockSpec((8, D_MODEL), lambda i: (0, 0))
    return _call(
        body, name="ffn_in_bwd", grid=(s // tm,),
        in_specs=[pl.BlockSpec((4, 2, tm, FF_SHARD), lambda i: (0, 0, i, 0)),
                  pl.BlockSpec((N_DEV, FF_SHARD, D_MODEL), lambda i: (0, 0, 0)),
                  row, gain, row, row, gain],
        out_specs=[row, row, dgs, dgs],
        out_shape=[jax.ShapeDtypeStruct((s, D_MODEL), F32), jax.ShapeDtypeStruct((s, D_MODEL), BF16),
                   jax.ShapeDtypeStruct((8, D_MODEL), F32), jax.ShapeDtypeStruct((8, D_MODEL), F32)],
        args=[dgu, wfin, xm, g_pre3, dres, z, g_post3], comm=comm)


def _dw_out(ync, yna, dz, s):
    half = D_MODEL // 2

    def body(a1_ref, a2_ref, b_ref, o_ref):
        b = b_ref[...]
        o_ref[0:half, :] = _dot_tn(a1_ref[...], b).astype(BF16)
        o_ref[half:D_MODEL, :] = _dot_tn(a2_ref[...], b).astype(BF16)

    out, = _call(
        body, name="dw_out", grid=(2,),
        in_specs=[pl.BlockSpec((s, half), lambda n: (0, 0)), pl.BlockSpec((s, half), lambda n: (0, 0)),
                  pl.BlockSpec((s, half), lambda n: (0, n))],
        out_specs=[pl.BlockSpec((D_MODEL, half), lambda n: (0, n))],
        out_shape=[jax.ShapeDtypeStruct((D_MODEL, D_MODEL), BF16)], args=[ync, yna, dz])
    return out


def _out_proj_bwd(dz, wout, o, g3, l, s, tm):
    def body(dz_ref, w_ref, o_ref, g_ref, dyc_ref, do_ref, dg_ref):
        dy = _dot_nt(dz_ref[...], w_ref[...])
        dyc_ref[...] = dy[:, 0:CONV_WIDTH]
        gmat = _group_matrix()

        @pl.when(pl.program_id(0) == 0)
        def _():
            dg_ref[...] = jnp.zeros(dg_ref.shape, F32)

        for j in range(ATTN_WIDTH // 128):
            c0 = 128 * j
            ov = o_ref[:, c0:c0 + 128]
            dyn = dy[:, CONV_WIDTH + c0:CONV_WIDTH + c0 + 128]
            r = lax.rsqrt(_group_mean(ov * ov, gmat) + EPS)
            on = ov * r
            gd = dyn * g_ref[:, c0:c0 + 128]
            do_ref[:, c0:c0 + 128] = r * (gd - on * _group_mean(on * gd, gmat))
            dg_ref[0:1, c0:c0 + 128] += jnp.sum(dyn * on, axis=0, keepdims=True)

    halfrow = pl.BlockSpec((tm, ATTN_WIDTH), lambda i: (i, 0))
    return _call(
        body, name="out_proj_bwd", grid=(s // tm,),
        in_specs=[pl.BlockSpec((tm, D_MODEL), lambda i: (i, 0)),
                  pl.BlockSpec((D_MODEL, D_MODEL), lambda i: (0, 0)), halfrow,
                  pl.BlockSpec((None, 1, ATTN_WIDTH), lambda i: (l, 0, 0))],
        out_specs=[halfrow, halfrow, pl.BlockSpec((8, ATTN_WIDTH), lambda i: (0, 0))],
        out_shape=[jax.ShapeDtypeStruct((s, CONV_WIDTH), F32), jax.ShapeDtypeStruct((s, ATTN_WIDTH), F32),
                   jax.ShapeDtypeStruct((8, ATTN_WIDTH), F32)],
        args=[dz, wout, o, g3])


def _conv_bwd(pc, dyc, wc, g3, dq, dk, dv, l, s, tr):
    hb = tr // 8
    nt = s // tr
    ext = tr + 16
    last_hb = s // 8 - 1

    def body(pc_ref, prev_ref, next_ref, dy_ref, dyn_ref, wc_ref, g_ref, dq_ref, dk_ref, dv_ref,
             dpc_ref, dw_ref, dg_ref):
        i = pl.program_id(0)
        for part, ref in enumerate((dq_ref, dk_ref, dv_ref)):
            c = 3 * CONV_WIDTH + ATTN_WIDTH * part
            dpc_ref[:, c:c + ATTN_WIDTH] = ref[...]
        gmat = _group_matrix()
        row = lax.broadcasted_iota(jnp.int32, (ext, 128), 0) + (i * tr - 8)
        inside = jnp.where(row >= 0, jnp.where(row < s, 1, 0), 0) == 1

        @pl.when(i == 0)
        def _():
            dw_ref[...] = jnp.zeros(dw_ref.shape, F32)
            dg_ref[...] = jnp.zeros(dg_ref.shape, F32)

        def extend(ref_prev, ref_mid, ref_next, c):
            parts = [ref_prev[:, c:c + 128] if ref_prev is not None else jnp.zeros((8, 128), F32),
                     ref_mid[:, c:c + 128], ref_next[:, c:c + 128]]
            return jnp.concatenate(parts, axis=0)

        for j in range(CONV_WIDTH // 128):
            c0, c1, c2 = 128 * j, CONV_WIDTH + 128 * j, 2 * CONV_WIDTH + 128 * j
            hc = extend(prev_ref, pc_ref, next_ref, c0)
            bg = extend(prev_ref, pc_ref, next_ref, c1)
            cg = extend(prev_ref, pc_ref, next_ref, c2)
            dyn = extend(None, dy_ref, dyn_ref, c0)
            w0, w1, w2 = (wc_ref[0:1, c0:c0 + 128], wc_ref[1:2, c0:c0 + 128], wc_ref[2:3, c0:c0 + 128])
            gain = g_ref[:, c0:c0 + 128]
            u = jnp.where(inside, cg * hc, 0.0)
            u1 = pltpu.roll(u, 1, 0)
            u2 = pltpu.roll(u, 2, 0)
            out = u2 * w0 + u1 * w1 + u * w2
            yc = bg * out
            r = lax.rsqrt(_group_mean(yc * yc, gmat) + EPS)
            ycn = yc * r
            gd = dyn * gain
            dyc = r * (gd - ycn * _group_mean(ycn * gd, gmat))
            dout = jnp.where(inside, dyc * bg, 0.0)
            du = dout * w2 + pltpu.roll(dout, ext - 1, 0) * w1 + pltpu.roll(dout, ext - 2, 0) * w0
            sl = slice(8, 8 + tr)
            dpc_ref[:, c0:c0 + 128] = (du[sl] * cg[sl]).astype(BF16)
            dpc_ref[:, c1:c1 + 128] = (dyc[sl] * out[sl]).astype(BF16)
            dpc_ref[:, c2:c2 + 128] = (du[sl] * hc[sl]).astype(BF16)
            dw_ref[0:1, c0:c0 + 128] += jnp.sum(dout[sl] * u2[sl], axis=0, keepdims=True)
            dw_ref[1:2, c0:c0 + 128] += jnp.sum(dout[sl] * u1[sl], axis=0, keepdims=True)
            dw_ref[2:3, c0:c0 + 128] += jnp.sum(dout[sl] * u[sl], axis=0, keepdims=True)
            dg_ref[0:1, c0:c0 + 128] += jnp.sum(dyn[sl] * ycn[sl], axis=0, keepdims=True)

    wide = 3 * CONV_WIDTH
    return _call(
        body, name="conv_bwd", grid=(nt,),
        in_specs=[pl.BlockSpec((tr, wide), lambda i: (i, 0)),
                  pl.BlockSpec((8, wide), lambda i: (jnp.maximum(i * hb - 1, 0), 0)),
                  pl.BlockSpec((8, wide), lambda i: (jnp.minimum((i + 1) * hb, last_hb), 0)),
                  pl.BlockSpec((tr, CONV_WIDTH), lambda i: (i, 0)),
                  pl.BlockSpec((8, CONV_WIDTH), lambda i: (jnp.minimum((i + 1) * hb, last_hb), 0)),
                  pl.BlockSpec((None, 8, CONV_WIDTH), lambda i: (l, 0, 0)),
                  pl.BlockSpec((None, 1, CONV_WIDTH), lambda i: (l, 0, 0)),
                  pl.BlockSpec((tr, ATTN_WIDTH), lambda i: (i, 0)),
                  pl.BlockSpec((tr, ATTN_WIDTH), lambda i: (i, 0)),
                  pl.BlockSpec((tr, ATTN_WIDTH), lambda i: (i, 0))],
        out_specs=[pl.BlockSpec((tr, PROJ_WIDTH), lambda i: (i, 0)),
                   pl.BlockSpec((8, CONV_WIDTH), lambda i: (0, 0)),
                   pl.BlockSpec((8, CONV_WIDTH), lambda i: (0, 0))],
        out_shape=[jax.ShapeDtypeStruct((s, PROJ_WIDTH), BF16), jax.ShapeDtypeStruct((8, CONV_WIDTH), F32),
                   jax.ShapeDtypeStruct((8, CONV_WIDTH), F32)],
        args=[pc, pc, pc, dyc, dyc, wc, g3, dq, dk, dv])


def _attn_bwd(qkvp, biasm, o, lse, do, l, s, pad, comm=None):
    nb = s // Q_BLOCK
    qb0 = pad // Q_BLOCK
    scale = HEAD_DIM ** -0.5

    def body(q_ref, k_ref, v_ref, b_ref, o_ref, lse_ref, do_ref,
             dq_ref, dk_ref, dv_ref, ds_ref, dk_acc, dv_acc):
        blk = pl.program_id(1)

        @pl.when(blk == 0)
        def _():
            dk_acc[...] = jnp.zeros(dk_acc.shape, F32)
            dv_acc[...] = jnp.zeros(dv_acc.shape, F32)
            ds_ref[...] = jnp.zeros(ds_ref.shape, F32)

        koff = pl.multiple_of(blk * Q_BLOCK + (pad - LEFT), Q_BLOCK)
        q = q_ref[...]
        kb = k_ref[pl.ds(koff, K_BAND), :]
        vb = v_ref[pl.ds(koff, K_BAND), :]
        ov = o_ref[...]
        dov = do_ref[...]
        lse_v = lse_ref[...]
        lane = lax.broadcasted_iota(jnp.int32, (1, 128), 1)
        kpos = lax.broadcasted_iota(jnp.int32, (Q_BLOCK, K_BAND), 1) + (blk * Q_BLOCK - LEFT)
        kvalid = kpos >= 0
        prod = dov * ov
        dq_parts = []
        dk_new = jnp.zeros((K_BAND, 128), F32)
        dv_new = jnp.zeros((K_BAND, 128), F32)
        for hh in range(2):
            in_head = (lane >> 6) == hh
            qm = jnp.where(in_head, q, jnp.zeros_like(q)) * jnp.asarray(scale, BF16)
            dom = jnp.where(in_head, dov, 0.0).astype(BF16)
            delta = jnp.sum(jnp.where(in_head, prod, 0.0), axis=1, keepdims=True)
            lse_h = lse_v[:, HEAD_DIM * hh:HEAD_DIM * hh + 1]
            sc = _dot_nt(qm, kb) + b_ref[hh]
            sc = jnp.where(kvalid, sc, NEG_INF)
            p = jnp.exp(sc - lse_h)
            dp = _dot_nt(dom, vb)
            ds = p * (dp - delta)
            ds_ref[hh] += ds
            dsb = ds.astype(BF16)
            dq_parts.append(_dot(dsb, kb) * scale)
            dk_new = dk_new + _dot_tn(dsb, qm)
            dv_new = dv_new + _dot_tn(p.astype(BF16), dom)
        dq_ref[...] = jnp.where(lane < HEAD_DIM, dq_parts[0], dq_parts[1]).astype(BF16)
        dk_acc[pl.ds(koff, K_BAND), :] += dk_new
        dv_acc[pl.ds(koff, K_BAND), :] += dv_new

        @pl.when(blk == nb - 1)
        def _():
            dk_ref[...] = dk_acc[pad:pad + s, :].astype(BF16)
            dv_ref[...] = dv_acc[pad:pad + s, :].astype(BF16)

    qblk = pl.BlockSpec((Q_BLOCK, 128), lambda p, b: (b, p))
    col = pl.BlockSpec((s, 128), lambda p, b: (0, p))
    shp = jax.ShapeDtypeStruct((s, ATTN_WIDTH), BF16)
    return _call(
        body, name="attn_bwd", grid=(ATTN_WIDTH // 128, nb),
        in_specs=[pl.BlockSpec((Q_BLOCK, 128), lambda p, b: (qb0 + b, p)),
                  pl.BlockSpec((s + pad, 128), lambda p, b: (0, 4 + p)),
                  pl.BlockSpec((s + pad, 128), lambda p, b: (0, 8 + p)),
                  pl.BlockSpec((None, 2, Q_BLOCK, K_BAND), lambda p, b: (l, p, 0, 0)),
                  qblk, qblk, qblk],
        out_specs=[qblk, col, col, pl.BlockSpec((2, Q_BLOCK, K_BAND), lambda p, b: (p, 0, 0))],
        out_shape=[shp, shp, shp, jax.ShapeDtypeStruct((N_HEADS, Q_BLOCK, K_BAND), F32)],
        scratch=[pltpu.VMEM((s + pad, 128), F32), pltpu.VMEM((s + pad, 128), F32)],
        args=[qkvp, qkvp, qkvp, biasm, o, lse, do], comm=comm)


def _dw_in(h, dproj, s):
    def body(a_ref, b_ref, o_ref):
        acc = _dot_tn(a_ref[...], b_ref[...])
        o_ref[0] = acc[:, 0:PROJ_SHARD].astype(BF16)
        o_ref[1] = acc[:, PROJ_SHARD:2 * PROJ_SHARD].astype(BF16)

    out, = _call(
        body, name="dw_in", grid=(4,),
        in_specs=[pl.BlockSpec((s, D_MODEL), lambda n: (0, 0)),
                  pl.BlockSpec((s, 2 * PROJ_SHARD), lambda n: (0, n))],
        out_specs=[pl.BlockSpec((2, D_MODEL, PROJ_SHARD), lambda n: (n, 0, 0))],
        out_shape=[jax.ShapeDtypeStruct((N_DEV, D_MODEL, PROJ_SHARD), BF16)], args=[h, dproj])
    return out


def _in_proj_bwd(dproj, win, x, g3, dres, l, s, tm, f_prev=None, g_post3=None, comm=None):
    chain = f_prev is not None

    def body(d_ref, w_ref, x_ref, g_ref, dres_ref, *rest):
        first = pl.program_id(0) == 0
        dh = _dot_nt(d_ref[:, 0:PROJ_SHARD], w_ref[0])
        for j in range(1, N_DEV):
            dh = dh + _dot_nt(d_ref[:, PROJ_SHARD * j:PROJ_SHARD * (j + 1)], w_ref[j])
        dx, dyn = _norm_bwd_rows(x_ref[...], g_ref[...], dh)
        dx = dres_ref[...] + dx
        if chain:
            f_ref, gq_ref, dx_ref, dg_ref, df_ref, dgq_ref = rest
            df, dyn2 = _norm_bwd_rows(f_ref[...], gq_ref[...], dx)
            df_ref[...] = df.astype(BF16)
            _accum_cols(dgq_ref, dyn2, first)
        else:
            dx_ref, dg_ref = rest
        dx_ref[...] = dx
        _accum_cols(dg_ref, dyn, first)

    row = pl.BlockSpec((tm, D_MODEL), lambda i: (i, 0))
    dgs = pl.BlockSpec((8, D_MODEL), lambda i: (0, 0))
    in_specs = [pl.BlockSpec((tm, PROJ_WIDTH), lambda i: (i, 0)),
                pl.BlockSpec((N_DEV, D_MODEL, PROJ_SHARD), lambda i: (0, 0, 0)),
                row, pl.BlockSpec((None, 1, D_MODEL), lambda i: (l, 0, 0)), row]
    out_specs = [row, dgs]
    out_shape = [jax.ShapeDtypeStruct((s, D_MODEL), F32), jax.ShapeDtypeStruct((8, D_MODEL), F32)]
    args = [dproj, win, x, g3, dres]
    if chain:
        in_specs += [row, pl.BlockSpec((None, 1, D_MODEL), lambda i: (l - 1, 0, 0))]
        out_specs += [row, dgs]
        out_shape += [jax.ShapeDtypeStruct((s, D_MODEL), BF16), jax.ShapeDtypeStruct((8, D_MODEL), F32)]
        args += [f_prev, g_post3]
    return _call(body, name="in_proj_bwd", grid=(s // tm,), in_specs=in_specs, out_specs=out_specs,
                 out_shape=out_shape, args=args, comm=comm)


def _adamw(name, w, m, v, lands):
    groups, rows, cols = w.shape
    assert len(lands) == groups
    n_part = lands[0].shape[0]
    tr = _row_tile(rows, tuple(c for c in (512, 352, 256, 176, 128, 64, 32, 16, 8) if c * cols <= 256 * 1024))
    c1 = 1.0 - ADAM_B1 ** ADAM_STEP
    c2 = 1.0 - ADAM_B2 ** ADAM_STEP

    def body(w_ref, m_ref, v_ref, *rest):
        land_refs = rest[:groups]
        g_ref, d_ref, nm_ref, nv_ref = rest[groups:]
        grp = pl.program_id(0)
        for gi in range(groups):
            @pl.when(grp == gi)
            def _():
                l_ref = land_refs[gi]
                g = l_ref[0].astype(F32)
                for p in range(1, n_part):
                    g = g + l_ref[p].astype(F32)
                g_ref[...] = g
                m1 = ADAM_B1 * m_ref[...] + (1.0 - ADAM_B1) * g
                v1 = ADAM_B2 * v_ref[...] + (1.0 - ADAM_B2) * (g * g)
                nm_ref[...] = m1
                nv_ref[...] = v1
                d_ref[...] = -ADAM_LR * ((m1 / c1) / (jnp.sqrt(v1 / c2) + ADAM_EPS) + ADAM_WD * w_ref[...])

    blk = pl.BlockSpec((None, tr, cols), lambda g, i: (g, i, 0))
    shp = jax.ShapeDtypeStruct((groups, rows, cols), F32)

    def land_spec(gi):
        return pl.BlockSpec((n_part, tr, cols), lambda g, i: (0, jnp.where(g == gi, i, 0), 0))

    return _call(
        body, name=name, grid=(groups, rows // tr),
        in_specs=[blk, blk, blk] + [land_spec(gi) for gi in range(groups)],
        out_specs=[blk, blk, blk, blk], out_shape=[shp, shp, shp, shp],
        args=[w, m, v] + list(lands))


def _pack_small(rel, gco, gao, gpm, gqm, gpf, gqf):
    n_layers = rel.shape[0]
    relp = jnp.pad(rel, ((0, 0), (0, 0), (0, REL_PAD - rel.shape[2])))
    parts = [relp.reshape(n_layers * N_HEADS * REL_PAD // 128, 128)]
    parts += [a.reshape(-1, 128) for a in (gco, gao, gpm, gqm, gpf, gqf)]
    return jnp.concatenate(parts, axis=0)


def _unpack_small(p, n_layers):
    n_rel = n_layers * N_HEADS * REL_PAD // 128
    rel = p[:n_rel].reshape(n_layers, N_HEADS, REL_PAD)[:, :, :2 * REL_CLIP + 1]
    outs = [rel]
    r0 = n_rel
    for width in (CONV_WIDTH, ATTN_WIDTH, D_MODEL, D_MODEL, D_MODEL, D_MODEL):
        nr = n_layers * width // 128
        outs.append(p[r0:r0 + nr].reshape(n_layers, width))
        r0 += nr
    return outs


def kernel(x, w_in, w_conv, rel_bias, g_conv_out, g_attn_out, w_out, g_pre_mix, g_post_mix, g_pre_ffn, g_post_ffn, w_ffn_in, w_ffn_out, loss_target, m_w_in, m_w_conv, m_rel_bias, m_g_conv_out, m_g_attn_out, m_w_out, m_g_pre_mix, m_g_post_mix, m_g_pre_ffn, m_g_post_ffn, m_w_ffn_in, m_w_ffn_out, v_w_in, v_w_conv, v_rel_bias, v_g_conv_out, v_g_attn_out, v_w_out, v_g_pre_mix, v_g_post_mix, v_g_pre_ffn, v_g_post_ffn, v_w_ffn_in, v_w_ffn_out):
    n_layers = w_in.shape[0]
    s = x.shape[1]
    assert x.shape == (1, s, D_MODEL) and s % 1024 == 0
    assert w_in.shape == (n_layers, D_MODEL, PROJ_SHARD) and w_ffn_in.shape == (n_layers, D_MODEL, FF_SHARD)
    tm = 512
    tq = 1024 if s >= 2048 else 512
    tf = min(2048, s)
    x0 = x.reshape(s, D_MODEL)
    target = loss_target.reshape(s, D_MODEL)
    dev = _dev_index(lax.axis_index("x"), lax.axis_index("y"), lax.axis_index("c"))

    wt_ffn_in, mt_ffn_in, vt_ffn_in = (jnp.transpose(a, (0, 2, 1)) for a in (w_ffn_in, m_w_ffn_in, v_w_ffn_in))
    local_w = [_cast_bf16(w_in, "cast_w_in"), _cast_bf16(w_out, "cast_w_out"),
               _cast_bf16(wt_ffn_in, "cast_w_ffn_in"), _cast_bf16(w_ffn_out, "cast_w_ffn_out")]
    wc_local = jnp.pad(jnp.transpose(w_conv, (0, 2, 1)).reshape(-1), (0, 1024 - n_layers * 3 * 64)).reshape(8, 128)
    win_next, wc_g = _comm_only("gather_first", _Gather([(local_w[0], 0), (wc_local, None)]))
    weights = [None] * n_layers
    wc_full = wc_g.reshape(N_DEV, 1024)[:, :n_layers * 3 * 64].reshape(N_DEV, n_layers, 3, 64)
    wc_full = jnp.transpose(wc_full, (1, 2, 0, 3)).reshape(n_layers, 3, CONV_WIDTH)
    wc_full = jnp.pad(wc_full, ((0, 0), (0, 5), (0, 0)))

    g3 = {k: v.reshape(n_layers, 1, -1) for k, v in dict(
        conv=g_conv_out, attn=g_attn_out, pre_mix=g_pre_mix, post_mix=g_post_mix,
        pre_ffn=g_pre_ffn, post_ffn=g_post_ffn).items()}
    biasm = _bias_build(jnp.pad(rel_bias, ((0, 0), (0, 0), (0, REL_PAD - rel_bias.shape[2]))))

    saved = []
    xl = x0
    h = _norm_cast(x0, g3["pre_mix"], 0, tm)
    for l in range(n_layers):
        win = win_next
        pc, wout = _in_proj(h, win, s, tq, 0, comm=_Gather([(local_w[1], l)]))
        qkvp, = _in_proj(h, win, s, tq, 1)
        ync = _conv_fwd(pc, wc_full, g3["conv"], l, s, tm)
        o, lse, yna, wfin = _attn_fwd(qkvp, biasm, g3["attn"], l, s, tq, comm=_Gather([(local_w[2], l)]))
        wout = wout.reshape(D_MODEL, D_MODEL)
        z, xm, h2 = _out_proj_fwd(ync, yna, wout, xl, g3["post_mix"], g3["pre_ffn"], l, s, tq)
        gu, act, wfout = _ffn_in_fwd(h2, wfin, s, tf, comm=_Gather([(local_w[3], l)]))
        wfo4 = wfout.reshape(4, FF_SHARD, D_MODEL)
        l_next = min(l + 1, n_layers - 1)
        f, xo, h_next, *got = _ffn_out_fwd(act, wfo4, xm, g3["post_ffn"], g3["pre_mix"], l, l_next, s, tm,
                                           comm=_Gather([(local_w[0], l + 1)]) if l + 1 < n_layers else None)
        weights[l] = [win, wout, wfin, wfo4]
        win_next = got[0] if got else None
        saved.append(dict(x=xl, h=h, pc=pc, qkvp=qkvp, ync=ync, yna=yna, o=o, lse=lse, z=z, xm=xm,
                          h2=h2, gu=gu, act=act, f=f))
        xl, h = xo, h_next

    dx, sq, df, dg_post_ffn = _loss_grad(xl, target, saved[-1]["f"], g3["post_ffn"], n_layers - 1, s, tm)
    loss = lax.psum(jnp.sum(sq) * (0.5 / D_MODEL), ("x", "y", "c"))

    lands = dict(win=[None] * n_layers, wout=[None] * n_layers, wfin=[None] * n_layers, wfout=[None] * n_layers)
    small = {k: [None] * n_layers for k in ("gco", "gao", "gpm", "gqm", "gpf", "gqf", "wc")}
    d_rel = [None] * n_layers
    pending = []
    for l in reversed(range(n_layers)):
        sv = saved[l]
        win, wout, wfin, wfo4 = weights[l]
        small["gqf"][l] = dg_post_ffn[0]
        dgu, *got = _ffn_out_bwd(df, wfo4, sv["gu"], s, tf,
                                 comm=_Scatter([(a, False) for _, _, a in pending]) if pending else None)
        for (key, lay, _), land in zip(pending, got):
            lands[key][lay] = land
        d_wfout = _dw_ffn_out(sv["act"], df, s).reshape(N_DEV, FFO_SHARD, D_MODEL)
        d_wfin = _dw_ffn_in(sv["h2"], dgu, s)
        dxm, dz, dg_pre_ffn, dg_post_mix, lands["wfout"][l] = _ffn_in_bwd(
            dgu, wfin, sv["xm"], g3["pre_ffn"], dx, sv["z"], g3["post_mix"], l, s, tm // 2,
            comm=_Scatter([(d_wfout, False)]))
        small["gpf"][l] = dg_pre_ffn[0]
        small["gqm"][l] = dg_post_mix[0]
        d_wout = _dw_out(sv["ync"], sv["yna"], dz, s).reshape(N_DEV, D_MODEL // N_DEV, D_MODEL)
        dyc, do, dg_attn = _out_proj_bwd(dz, wout, sv["o"], g3["attn"], l, s, tq)
        small["gao"][l] = dg_attn[0]
        dq, dk, dv, ds_sum, lands["wfin"][l] = _attn_bwd(
            sv["qkvp"], biasm, sv["o"], sv["lse"], do, l, s, tq, comm=_Scatter([(d_wfin, False)]))
        d_rel[l] = _bias_bwd(ds_sum[None])
        dproj, dwc, dg_conv = _conv_bwd(sv["pc"], dyc, wc_full, g3["conv"], dq, dk, dv, l, s, tm)
        small["wc"][l] = dwc[0:3]
        small["gco"][l] = dg_conv[0]
        d_win = _dw_in(sv["h"], dproj, s)
        if l > 0:
            dx, dg_pre_mix, df, dg_post_ffn, lands["wout"][l] = _in_proj_bwd(
                dproj, win, sv["x"], g3["pre_mix"], dxm, l, s, tm, f_prev=saved[l - 1]["f"],
                g_post3=g3["post_ffn"], comm=_Scatter([(d_wout, False)]))
            pending = [("win", l, d_win)]
        else:
            dx, dg_pre_mix = _in_proj_bwd(dproj, win, sv["x"], g3["pre_mix"], dxm, l, s, tm)
            pending = [("win", l, d_win), ("wout", l, d_wout)]
        small["gpm"][l] = dg_pre_mix[0]
    grad_x = dx.reshape(1, s, D_MODEL)

    small_vec = jnp.concatenate(
        [_pack_small(jnp.concatenate(d_rel)[:, :, :2 * REL_CLIP + 1],
                     *[jnp.stack(small[k]) for k in ("gco", "gao", "gpm", "gqm", "gpf", "gqf")]),
         jnp.stack(small["wc"]).reshape(-1, 128)], axis=0)
    small_vec = jnp.pad(small_vec, ((0, (-small_vec.shape[0]) % 8), (0, 0)))
    *got, land_small = _comm_only(
        "exchange_last", _Scatter([(a, False) for _, _, a in pending] + [(small_vec, True)]))
    for (key, lay, _), land in zip(pending, got):
        lands[key][lay] = land

    r_in = _adamw("adamw_w_in", w_in, m_w_in, v_w_in, lands["win"])
    r_out = _adamw("adamw_w_out", w_out, m_w_out, v_w_out, lands["wout"])
    r_fin = [jnp.transpose(t, (0, 2, 1)) for t in _adamw("adamw_w_ffn_in", wt_ffn_in, mt_ffn_in, vt_ffn_in, lands["wfin"])]
    r_fout = _adamw("adamw_w_ffn_out", w_ffn_out, m_w_ffn_out, v_w_ffn_out, lands["wfout"])

    n_rep = 64 * n_layers
    rep = _adamw(
        "adamw_replicated",
        _pack_small(rel_bias, g_conv_out, g_attn_out, g_pre_mix, g_post_mix, g_pre_ffn, g_post_ffn)[None],
        _pack_small(m_rel_bias, m_g_conv_out, m_g_attn_out, m_g_pre_mix, m_g_post_mix, m_g_pre_ffn, m_g_post_ffn)[None],
        _pack_small(v_rel_bias, v_g_conv_out, v_g_attn_out, v_g_pre_mix, v_g_post_mix, v_g_pre_ffn, v_g_post_ffn)[None],
        [land_small[:, :n_rep]])
    rep = [_unpack_small(t[0], n_layers) for t in rep]

    wc_rows = n_layers * 3 * CONV_WIDTH // 128
    zeros_wc = jnp.zeros((1, wc_rows, 128), F32)
    g_wc_full = _adamw("sum_w_conv", zeros_wc, zeros_wc, zeros_wc, [land_small[:, n_rep:n_rep + wc_rows]])[0]
    g_wc_full = g_wc_full.reshape(n_layers, 3, CONV_WIDTH)
    g_wc = lax.dynamic_slice_in_dim(g_wc_full, dev * (CONV_WIDTH // N_DEV), CONV_WIDTH // N_DEV, axis=2)
    g_wc = jnp.transpose(g_wc, (0, 2, 1))

    def tiny(a):
        flat = a.reshape(-1)
        return jnp.pad(flat, (0, (-flat.shape[0]) % 1024)).reshape(1, -1, 128)

    r_wc = _adamw("adamw_w_conv", tiny(w_conv), tiny(m_w_conv), tiny(v_w_conv), [tiny(g_wc)])
    r_wc = [t.reshape(-1)[:w_conv.size].reshape(w_conv.shape) for t in r_wc]

    def leaf(kind):
        return [r_in[kind], r_wc[kind], rep[kind][0], rep[kind][1], rep[kind][2], r_out[kind],
                rep[kind][3], rep[kind][4], rep[kind][5], rep[kind][6], r_fin[kind], r_fout[kind]]

    return (loss, grad_x, *leaf(0), *leaf(1), *leaf(2), *leaf(3))
```

```python
import math

import jax
import jax.numpy as jnp
from jax import lax
from jax.experimental import pallas as pl
from jax.experimental.pallas import tpu as pltpu

F32 = jnp.float32
BF16 = jnp.bfloat16

D_MODEL = 1024
N_DEV = 8
CHUNK = 64
N_LEFT_CHUNKS = 8
CONV_WIDTH = 512
ATTN_WIDTH = 512
HEAD_DIM = 64
N_HEADS = 8
REL_CLIP = 128
REL_PAD = 384
PROJ_WIDTH = 3072
PROJ_SHARD = PROJ_WIDTH // N_DEV
D_FF = 2816
FF_SHARD = 2 * D_FF // N_DEV
FFO_SHARD = D_FF // N_DEV
EPS = 1e-6
NEG_INF = -1e30
ATTN_PAIRS = 2
Q_BLOCK = 4 * CHUNK
K_BAND = Q_BLOCK + N_LEFT_CHUNKS * CHUNK
LEFT = N_LEFT_CHUNKS * CHUNK
TOEP = 1024

ADAM_LR = 0.001
ADAM_B1 = 0.9
ADAM_B2 = 0.999
ADAM_EPS = 1e-08
ADAM_WD = 0.01
ADAM_STEP = 10

VMEM_LIMIT = 52 * 1024 * 1024
SUB_ROWS = 256
MESH = pl.DeviceIdType.MESH
ANY = pl.BlockSpec(memory_space=pl.ANY)

NT = (((1,), (1,)), ((), ()))
TN = (((0,), (0,)), ((), ()))


def _dot(a, b):
    return jnp.dot(a, b, preferred_element_type=F32)


def _dot_nt(a, b):
    return lax.dot_general(a, b, NT, preferred_element_type=F32)


def _dot_tn(a, b):
    return lax.dot_general(a, b, TN, preferred_element_type=F32)


def _rstd(v):
    return lax.rsqrt(jnp.mean(v * v, axis=-1, keepdims=True) + EPS)


def _group_matrix():
    r = lax.broadcasted_iota(jnp.int32, (128, 128), 0) >> 6
    c = lax.broadcasted_iota(jnp.int32, (128, 128), 1) >> 6
    return jnp.where(r == c, 1.0, 0.0).astype(BF16)


def _group_mean(v, gmat):
    hi = v.astype(BF16)
    lo = (v - hi.astype(F32)).astype(BF16)
    return (_dot(hi, gmat) + _dot(lo, gmat)) * (1.0 / HEAD_DIM)


def _split3(v):
    hi = v.astype(BF16)
    r1 = v - hi.astype(F32)
    mid = r1.astype(BF16)
    lo = (r1 - mid.astype(F32)).astype(BF16)
    return hi, mid, lo


def _row_tile(rows, cands=(1024, 512, 704, 256, 128, 64, 32, 16)):
    for c in cands:
        if rows % c == 0:
            return c
    return rows


def _dev_index(px, py, pc):
    return 4 * px + 2 * py + pc


def _when(cond):
    if cond is True:
        return lambda fn: fn()
    return pl.when(cond)


def _phases(grid):
    def phases():
        if not grid:
            return True, True, True
        lin = pl.program_id(0)
        for a in range(1, len(grid)):
            lin = lin * grid[a] + pl.program_id(a)
        total = math.prod(grid)
        return lin == 0, lin == (3 * total) // 4, lin == total - 1
    return phases


class _Gather:
    def __init__(self, items):
        self.items = items
        self.args = [a for a, _ in items]
        n = len(items)
        self.out_shape = [jax.ShapeDtypeStruct((N_DEV,) + (a.shape if lay is None else a.shape[1:]), a.dtype)
                          for a, lay in items]
        self.scratch = [pltpu.SemaphoreType.DMA((n, 7)), pltpu.SemaphoreType.DMA((n, 7)),
                        pltpu.SemaphoreType.DMA((n,))]

    def _ctx(self, ins, outs, sems):
        send_sems, recv_sems, local_sems = sems
        x, y, c = lax.axis_index("x"), lax.axis_index("y"), lax.axis_index("c")
        chips = [(1 - x, y), (x, 1 - y), (1 - x, 1 - y)]

        def src(k):
            lay = self.items[k][1]
            return ins[k] if lay is None else ins[k].at[lay]

        def copy(k, s, idx, to, from_src=False):
            return pltpu.make_async_remote_copy(
                src_ref=src(k) if from_src else outs[k].at[idx], dst_ref=outs[k].at[idx],
                send_sem=send_sems.at[k, s], recv_sem=recv_sems.at[k, s],
                device_id=to, device_id_type=MESH)

        def local(k):
            return pltpu.make_async_copy(src(k), outs[k].at[_dev_index(x, y, c)], local_sems.at[k])

        return x, y, c, chips, copy, local

    def start(self, ins, outs, sems, cond):
        n = len(self.items)

        @_when(cond)
        def _():
            x, y, c, chips, copy, local = self._ctx(ins, outs, sems)
            me = _dev_index(x, y, c)
            for k in range(n):
                local(k).start()
                copy(k, 0, me, (x, y, 1 - c), from_src=True).start()
                for j, chip in enumerate(chips):
                    copy(k, 1 + j, me, (chip[0], chip[1], c), from_src=True).start()

    def forward(self, ins, outs, sems, cond):
        n = len(self.items)

        @_when(cond)
        def _():
            x, y, c, chips, copy, local = self._ctx(ins, outs, sems)
            for j, chip in enumerate(chips):
                idx = _dev_index(chip[0], chip[1], c)
                for k in range(n):
                    copy(k, 1 + j, idx, (x, y, c)).wait_recv()
                    copy(k, 4 + j, idx, (x, y, 1 - c)).start()

    def finish(self, ins, outs, sems, cond):
        n = len(self.items)

        @_when(cond)
        def _():
            x, y, c, chips, copy, local = self._ctx(ins, outs, sems)
            me = _dev_index(x, y, c)
            for k in range(n):
                copy(k, 0, _dev_index(x, y, 1 - c), (x, y, c)).wait_recv()
            for j, chip in enumerate(chips):
                idx = _dev_index(chip[0], chip[1], 1 - c)
                for k in range(n):
                    copy(k, 4 + j, idx, (x, y, c)).wait_recv()
            for k in range(n):
                for s in range(4):
                    copy(k, s, me, (x, y, c), from_src=True).wait_send()
                for j, chip in enumerate(chips):
                    copy(k, 4 + j, _dev_index(chip[0], chip[1], c), (x, y, c)).wait_send()
                local(k).wait()


_PEER_FLIPS = [(0, 0, 1), (1, 0, 0), (0, 1, 0), (1, 1, 0), (1, 0, 1), (0, 1, 1), (1, 1, 1)]


class _Scatter:
    def __init__(self, items):
        self.items = items
        self.args = [a for a, _ in items]
        n = len(items)
        self.out_shape = [jax.ShapeDtypeStruct((N_DEV,) + (a.shape if whole else a.shape[1:]), a.dtype)
                          for a, whole in items]
        self.scratch = [pltpu.SemaphoreType.DMA((n, 7)), pltpu.SemaphoreType.DMA((n, 7)),
                        pltpu.SemaphoreType.DMA((n,))]

    def _ctx(self, ins, outs, sems):
        send_sems, recv_sems, local_sems = sems
        x, y, c = lax.axis_index("x"), lax.axis_index("y"), lax.axis_index("c")
        me = _dev_index(x, y, c)

        def src(k, dest_idx):
            return ins[k] if self.items[k][1] else ins[k].at[dest_idx]

        def send(k, s):
            fx, fy, fc = _PEER_FLIPS[s]
            px, py, pc = x ^ fx, y ^ fy, c ^ fc
            return pltpu.make_async_remote_copy(
                src_ref=src(k, _dev_index(px, py, pc)), dst_ref=outs[k].at[me],
                send_sem=send_sems.at[k, s], recv_sem=recv_sems.at[k, s],
                device_id=(px, py, pc), device_id_type=MESH)

        def recv(k, s):
            fx, fy, fc = _PEER_FLIPS[s]
            pidx = _dev_index(x ^ fx, y ^ fy, c ^ fc)
            return pltpu.make_async_remote_copy(
                src_ref=outs[k].at[pidx], dst_ref=outs[k].at[pidx],
                send_sem=send_sems.at[k, s], recv_sem=recv_sems.at[k, s],
                device_id=(x, y, c), device_id_type=MESH)

        def local(k):
            return pltpu.make_async_copy(src(k, me), outs[k].at[me], local_sems.at[k])

        return send, recv, local

    def start(self, ins, outs, sems, cond):
        n = len(self.items)

        @_when(cond)
        def _():
            send, recv, local = self._ctx(ins, outs, sems)
            for k in range(n):
                local(k).start()
            for s in range(7):
                for k in range(n):
                    send(k, s).start()

    def forward(self, ins, outs, sems, cond):
        pass

    def finish(self, ins, outs, sems, cond):
        n = len(self.items)

        @_when(cond)
        def _():
            send, recv, local = self._ctx(ins, outs, sems)
            for s in range(7):
                for k in range(n):
                    recv(k, s).wait_recv()
            for s in range(7):
                for k in range(n):
                    send(k, s).wait_send()
            for k in range(n):
                local(k).wait()


def _call(body, *, name, grid, in_specs, out_specs, out_shape, args, scratch=(), comm=None):
    n_hi, n_ho, n_hs = len(args), len(out_shape), len(scratch)
    c_args = list(comm.args) if comm else []
    c_out = list(comm.out_shape) if comm else []
    c_scr = list(comm.scratch) if comm else []
    phases = _phases(grid)

    def kern(*refs):
        cuts = [n_hi, len(c_args), n_ho, len(c_out), n_hs, len(c_scr)]
        parts, pos = [], 0
        for n in cuts:
            parts.append(refs[pos:pos + n])
            pos += n
        hi, ci, ho, co, hs, cs = parts
        if comm:
            first, mid, last = phases()
            comm.start(ci, co, cs, first)
            comm.forward(ci, co, cs, mid)
        body(*hi, *ho, *hs)
        if comm:
            comm.finish(ci, co, cs, last)

    sem = ("arbitrary",) * len(grid) if grid else None
    return pl.pallas_call(
        kern, name=name, grid=grid,
        in_specs=list(in_specs) + [ANY] * len(c_args),
        out_specs=list(out_specs) + [ANY] * len(c_out),
        out_shape=list(out_shape) + c_out,
        scratch_shapes=list(scratch) + c_scr,
        compiler_params=pltpu.CompilerParams(dimension_semantics=sem, vmem_limit_bytes=VMEM_LIMIT),
    )(*args, *c_args)


def _comm_only(name, comm):
    return _call(lambda: None, name=name, grid=(), in_specs=[], out_specs=[], out_shape=[], args=[], comm=comm)


def _cast_bf16(x, name):
    shape = x.shape
    x2 = x.reshape(-1, shape[-1])
    rows, cols = x2.shape
    tr = _row_tile(rows)

    def body(x_ref, o_ref):
        o_ref[...] = x_ref[...].astype(BF16)

    blk = pl.BlockSpec((tr, cols), lambda i: (i, 0))
    out, = _call(body, name=name, grid=(rows // tr,), in_specs=[blk], out_specs=[blk],
                 out_shape=[jax.ShapeDtypeStruct((rows, cols), BF16)], args=[x2])
    return out.reshape(shape)


def _norm_cast(x, g3, l, tm):
    s = x.shape[0]

    def body(x_ref, g_ref, o_ref):
        v = x_ref[...]
        o_ref[...] = (v * _rstd(v) * g_ref[...]).astype(BF16)

    row = pl.BlockSpec((tm, D_MODEL), lambda i: (i, 0))
    out, = _call(body, name="norm_cast", grid=(s // tm,),
                 in_specs=[row, pl.BlockSpec((None, 1, D_MODEL), lambda i: (l, 0, 0))], out_specs=[row],
                 out_shape=[jax.ShapeDtypeStruct((s, D_MODEL), BF16)], args=[x, g3])
    return out


def _in_proj(h, win, s, tq, part, comm=None):
    pad = part
    dtype = BF16 if part else F32

    def body(a_ref, b_ref, o_ref):
        def compute():
            a = a_ref[...]
            for j in range(4):
                o_ref[:, PROJ_SHARD * j:PROJ_SHARD * (j + 1)] = _dot(a, b_ref[j]).astype(dtype)

        if pad:
            i = pl.program_id(0)

            @pl.when(i == 0)
            def _():
                o_ref[...] = jnp.zeros(o_ref.shape, dtype)

            pl.when(i > 0)(compute)
        else:
            compute()

    return _call(
        body, name="in_proj_qkv" if part else "in_proj_conv", grid=(s // tq + pad,),
        in_specs=[pl.BlockSpec((tq, D_MODEL), lambda i: (jnp.maximum(i - pad, 0), 0)),
                  pl.BlockSpec((4, D_MODEL, PROJ_SHARD), lambda i: (part, 0, 0))],
        out_specs=[pl.BlockSpec((tq, 4 * PROJ_SHARD), lambda i: (i, 0))],
        out_shape=[jax.ShapeDtypeStruct((s + pad * tq, PROJ_WIDTH // 2), dtype)], args=[h, win], comm=comm)


def _conv_fwd(pc, wc, g3, l, s, tr):
    hb = tr // 8

    def body(pc_ref, prev_ref, wc_ref, g_ref, o_ref):
        i = pl.program_id(0)
        gmat = _group_matrix()
        for j in range(CONV_WIDTH // 128):
            c0, c1, c2 = 128 * j, CONV_WIDTH + 128 * j, 2 * CONV_WIDTH + 128 * j
            hc = pc_ref[:, c0:c0 + 128]
            bg = pc_ref[:, c1:c1 + 128]
            cg = pc_ref[:, c2:c2 + 128]
            u_prev = jnp.where(i > 0, prev_ref[:, c2:c2 + 128] * prev_ref[:, c0:c0 + 128], 0.0)
            u = cg * hc
            full = jnp.concatenate([u_prev, u], axis=0)
            u1 = pltpu.roll(full, 1, 0)[8:]
            u2 = pltpu.roll(full, 2, 0)[8:]
            out = (u2 * wc_ref[0:1, c0:c0 + 128] + u1 * wc_ref[1:2, c0:c0 + 128]
                   + u * wc_ref[2:3, c0:c0 + 128])
            yc = bg * out
            r = lax.rsqrt(_group_mean(yc * yc, gmat) + EPS)
            o_ref[:, c0:c0 + 128] = (yc * r * g_ref[:, c0:c0 + 128]).astype(BF16)

    out, = _call(
        body, name="conv_fwd", grid=(s // tr,),
        in_specs=[pl.BlockSpec((tr, 3 * CONV_WIDTH), lambda i: (i, 0)),
                  pl.BlockSpec((8, 3 * CONV_WIDTH), lambda i: (jnp.maximum(i * hb - 1, 0), 0)),
                  pl.BlockSpec((None, 8, CONV_WIDTH), lambda i: (l, 0, 0)),
                  pl.BlockSpec((None, 1, CONV_WIDTH), lambda i: (l, 0, 0))],
        out_specs=[pl.BlockSpec((tr, CONV_WIDTH), lambda i: (i, 0))],
        out_shape=[jax.ShapeDtypeStruct((s, CONV_WIDTH), BF16)], args=[pc, pc, wc, g3])
    return out


def _toeplitz_source():
    r_i = lax.broadcasted_iota(jnp.int32, (REL_PAD, TOEP), 0)
    m_i = lax.broadcasted_iota(jnp.int32, (REL_PAD, TOEP), 1)
    idx = jnp.clip((K_BAND - 1) - m_i, -REL_CLIP, REL_CLIP) + REL_CLIP
    return jnp.where(r_i == idx, 1.0, 0.0).astype(BF16)


def _bias_build(rbp):
    n_layers = rbp.shape[0]

    def body(rb_ref, o_ref, t_ref):
        pmat = _toeplitz_source()
        hi, mid, lo = _split3(rb_ref[...])
        t_ref[...] = _dot(hi, pmat) + _dot(mid, pmat) + _dot(lo, pmat)
        shift = (CHUNK - 1) - lax.broadcasted_iota(jnp.int32, (CHUNK, TOEP), 0)
        kchunk = lax.broadcasted_iota(jnp.int32, (CHUNK, K_BAND), 1) >> 6
        for h in range(N_HEADS):
            b = jnp.broadcast_to(t_ref[pl.ds(h, 1), :], (CHUNK, TOEP))
            for bit in range(6):
                rolled = pltpu.roll(b, TOEP - (1 << bit), 1)
                b = jnp.where(((shift >> bit) & 1) == 1, rolled, b)
            for cq in range(Q_BLOCK // CHUNK):
                off = CHUNK * (Q_BLOCK // CHUNK - 1 - cq)
                band = pltpu.roll(b, TOEP - off, 1) if off else b
                dchunk = kchunk - cq
                in_band = jnp.where(dchunk >= 0, jnp.where(dchunk <= N_LEFT_CHUNKS, 1, 0), 0) == 1
                o_ref[h, CHUNK * cq:CHUNK * (cq + 1), :] = jnp.where(in_band, band[:, :K_BAND], NEG_INF)

    out, = _call(
        body, name="bias_build", grid=(n_layers,),
        in_specs=[pl.BlockSpec((None, N_HEADS, REL_PAD), lambda l: (l, 0, 0))],
        out_specs=[pl.BlockSpec((None, N_HEADS, Q_BLOCK, K_BAND), lambda l: (l, 0, 0, 0))],
        out_shape=[jax.ShapeDtypeStruct((n_layers, N_HEADS, Q_BLOCK, K_BAND), F32)],
        scratch=[pltpu.VMEM((N_HEADS, TOEP), F32)], args=[rbp])
    return out


def _bias_bwd(ds_sum):
    n_layers = ds_sum.shape[0]

    def body(ds_ref, o_ref, t_ref):
        pmat = _toeplitz_source()
        shift = (CHUNK - 1) - lax.broadcasted_iota(jnp.int32, (CHUNK, TOEP), 0)
        for h in range(N_HEADS):
            d = None
            for cq in range(Q_BLOCK // CHUNK):
                off = CHUNK * (Q_BLOCK // CHUNK - 1 - cq)
                part = jnp.concatenate([ds_ref[h, CHUNK * cq:CHUNK * (cq + 1), :],
                                        jnp.zeros((CHUNK, TOEP - K_BAND), F32)], axis=1)
                part = pltpu.roll(part, off, 1) if off else part
                d = part if d is None else d + part
            for bit in range(6):
                rolled = pltpu.roll(d, 1 << bit, 1)
                d = jnp.where(((shift >> bit) & 1) == 1, rolled, d)
            t_ref[pl.ds(h, 1), :] = jnp.sum(d, axis=0, keepdims=True)
        hi, mid, lo = _split3(t_ref[...])
        o_ref[...] = _dot_nt(hi, pmat) + _dot_nt(mid, pmat) + _dot_nt(lo, pmat)

    out, = _call(
        body, name="bias_bwd", grid=(n_layers,),
        in_specs=[pl.BlockSpec((None, N_HEADS, Q_BLOCK, K_BAND), lambda l: (l, 0, 0, 0))],
        out_specs=[pl.BlockSpec((None, N_HEADS, REL_PAD), lambda l: (l, 0, 0))],
        out_shape=[jax.ShapeDtypeStruct((n_layers, N_HEADS, REL_PAD), F32)],
        scratch=[pltpu.VMEM((N_HEADS, TOEP), F32)], args=[ds_sum])
    return out


def _attn_fwd(qkvp, biasm, g3, l, s, pad, comm=None):
    nb = s // Q_BLOCK
    qb0 = pad // Q_BLOCK
    scale = HEAD_DIM ** -0.5
    wide = 128 * ATTN_PAIRS

    def body(q_ref, k_ref, v_ref, b_ref, g_ref, o_ref, lse_ref, yn_ref):
        blk = pl.program_id(1)
        koff = pl.multiple_of(blk * Q_BLOCK + (pad - LEFT), Q_BLOCK)
        lane = lax.broadcasted_iota(jnp.int32, (1, 128), 1)
        kpos = lax.broadcasted_iota(jnp.int32, (1, K_BAND), 1) + (blk * Q_BLOCK - LEFT)
        kmask = jnp.where(kpos >= 0, 0.0, NEG_INF)
        gmat = _group_matrix()
        for pr in range(ATTN_PAIRS):
            ls = slice(128 * pr, 128 * (pr + 1))
            q = q_ref[:, ls]
            kb = k_ref[pl.ds(koff, K_BAND), ls]
            vb = v_ref[pl.ds(koff, K_BAND), ls]
            outs, lses = [], []
            for hh in range(2):
                in_head = (lane >> 6) == hh
                qm = jnp.where(in_head, q, jnp.zeros_like(q)) * jnp.asarray(scale, BF16)
                sc = _dot_nt(qm, kb) + b_ref[2 * pr + hh] + kmask
                m = jnp.max(sc, axis=1, keepdims=True)
                e = jnp.exp(sc - m)
                den = jnp.sum(e, axis=1, keepdims=True)
                outs.append(_dot(e.astype(BF16), vb) * (1.0 / den))
                lses.append(m + jnp.log(den))
            first = lane < HEAD_DIM
            o = jnp.where(first, outs[0], outs[1])
            o_ref[:, ls] = o
            lse_ref[:, ls] = jnp.where(first, lses[0], lses[1])
            r = lax.rsqrt(_group_mean(o * o, gmat) + EPS)
            yn_ref[:, ls] = (o * r * g_ref[:, ls]).astype(BF16)

    blk_out = pl.BlockSpec((Q_BLOCK, wide), lambda p, b: (b, p))
    n_grp = ATTN_WIDTH // wide
    return _call(
        body, name="attn_fwd", grid=(n_grp, nb),
        in_specs=[pl.BlockSpec((Q_BLOCK, wide), lambda p, b: (qb0 + b, p)),
                  pl.BlockSpec((s + pad, wide), lambda p, b: (0, n_grp + p)),
                  pl.BlockSpec((s + pad, wide), lambda p, b: (0, 2 * n_grp + p)),
                  pl.BlockSpec((None, 2 * ATTN_PAIRS, Q_BLOCK, K_BAND), lambda p, b: (l, p, 0, 0)),
                  pl.BlockSpec((None, 1, wide), lambda p, b: (l, 0, p))],
        out_specs=[blk_out, blk_out, blk_out],
        out_shape=[jax.ShapeDtypeStruct((s, ATTN_WIDTH), F32),
                   jax.ShapeDtypeStruct((s, ATTN_WIDTH), F32),
                   jax.ShapeDtypeStruct((s, ATTN_WIDTH), BF16)],
        args=[qkvp, qkvp, qkvp, biasm, g3], comm=comm)


def _out_proj_fwd(ync, yna, wout, x, g_post3, g_next3, l, s, tm):
    half = D_MODEL // 2

    def body(a1_ref, a2_ref, w_ref, x_ref, gp_ref, gn_ref, z_ref, xm_ref, h_ref):
        for rs in _row_subtiles(tm, SUB_ROWS):
            z = _dot(a1_ref[rs, :], w_ref[0:half, :]) + _dot(a2_ref[rs, :], w_ref[half:D_MODEL, :])
            z_ref[rs, :] = z
            xm = x_ref[rs, :] + z * _rstd(z) * gp_ref[...]
            xm_ref[rs, :] = xm
            h_ref[rs, :] = (xm * _rstd(xm) * gn_ref[...]).astype(BF16)

    row = pl.BlockSpec((tm, D_MODEL), lambda i: (i, 0))
    gain = pl.BlockSpec((None, 1, D_MODEL), lambda i: (l, 0, 0))
    return _call(
        body, name="out_proj_fwd", grid=(s // tm,),
        in_specs=[pl.BlockSpec((tm, half), lambda i: (i, 0)), pl.BlockSpec((tm, half), lambda i: (i, 0)),
                  pl.BlockSpec((D_MODEL, D_MODEL), lambda i: (0, 0)), row, gain, gain],
        out_specs=[row, row, row],
        out_shape=[jax.ShapeDtypeStruct((s, D_MODEL), F32), jax.ShapeDtypeStruct((s, D_MODEL), F32),
                   jax.ShapeDtypeStruct((s, D_MODEL), BF16)],
        args=[ync, yna, wout, x, g_post3, g_next3])


def _ffn_in_fwd(h2, wfin, s, tm, comm=None):
    def body(h_ref, wg_ref, wu_ref, gu_ref, act_ref):
        h = h_ref[...]
        gate = _dot_nt(h, wg_ref[...])
        up = _dot_nt(h, wu_ref[...])
        gu_ref[0] = gate.astype(BF16)
        gu_ref[1] = up.astype(BF16)
        act_ref[...] = (gate * (1.0 / (1.0 + jnp.exp(-gate))) * up).astype(BF16)

    return _call(
        body, name="ffn_in_fwd", grid=(4, s // tm),
        in_specs=[pl.BlockSpec((tm, D_MODEL), lambda b, i: (i, 0)),
                  pl.BlockSpec((None, FF_SHARD, D_MODEL), lambda b, i: (b, 0, 0)),
                  pl.BlockSpec((None, FF_SHARD, D_MODEL), lambda b, i: (4 + b, 0, 0))],
        out_specs=[pl.BlockSpec((None, 2, tm, FF_SHARD), lambda b, i: (b, 0, i, 0)),
                   pl.BlockSpec((None, tm, FF_SHARD), lambda b, i: (b, i, 0))],
        out_shape=[jax.ShapeDtypeStruct((4, 2, s, FF_SHARD), BF16), jax.ShapeDtypeStruct((4, s, FF_SHARD), BF16)],
        args=[h2, wfin, wfin], comm=comm)


def _ffn_out_fwd(act, wfo4, xm, g_post3, g_next3, l, l_next, s, tm, comm=None):
    def body(a_ref, w_ref, x_ref, gp_ref, gn_ref, f_ref, xo_ref, h_ref):
        for rs in _row_subtiles(tm, SUB_ROWS):
            f = _dot(a_ref[0, rs, :], w_ref[0])
            for b in range(1, 4):
                f = f + _dot(a_ref[b, rs, :], w_ref[b])
            f_ref[rs, :] = f
            xo = x_ref[rs, :] + f * _rstd(f) * gp_ref[...]
            xo_ref[rs, :] = xo
            h_ref[rs, :] = (xo * _rstd(xo) * gn_ref[...]).astype(BF16)

    row = pl.BlockSpec((tm, D_MODEL), lambda i: (i, 0))
    return _call(
        body, name="ffn_out_fwd", grid=(s // tm,),
        in_specs=[pl.BlockSpec((4, tm, FF_SHARD), lambda i: (0, i, 0)),
                  pl.BlockSpec((4, FF_SHARD, D_MODEL), lambda i: (0, 0, 0)), row,
                  pl.BlockSpec((None, 1, D_MODEL), lambda i: (l, 0, 0)),
                  pl.BlockSpec((None, 1, D_MODEL), lambda i: (l_next, 0, 0))],
        out_specs=[row, row, row],
        out_shape=[jax.ShapeDtypeStruct((s, D_MODEL), F32), jax.ShapeDtypeStruct((s, D_MODEL), F32),
                   jax.ShapeDtypeStruct((s, D_MODEL), BF16)],
        args=[act, wfo4, xm, g_post3, g_next3], comm=comm)


def _loss_grad(xf, target, f, g3, l, s, tm):
    def body(x_ref, t_ref, f_ref, g_ref, dx_ref, sq_ref, df_ref, dg_ref):
        i = pl.program_id(0)
        err = x_ref[...] - t_ref[...]
        dx = err * (1.0 / D_MODEL)
        dx_ref[...] = dx
        df, dyn = _norm_bwd_rows(f_ref[...], g_ref[...], dx)
        df_ref[...] = df.astype(BF16)
        _accum_cols(dg_ref, dyn, i == 0)
        cs = jnp.sum(err * err, axis=0, keepdims=True)
        part = cs[:, 0:128]
        for k in range(1, D_MODEL // 128):
            part = part + cs[:, 128 * k:128 * (k + 1)]

        @pl.when(i == 0)
        def _():
            sq_ref[...] = jnp.zeros(sq_ref.shape, F32)

        sq_ref[0:1, :] += part

    row = pl.BlockSpec((tm, D_MODEL), lambda i: (i, 0))
    return _call(
        body, name="loss_grad", grid=(s // tm,),
        in_specs=[row, row, row, pl.BlockSpec((None, 1, D_MODEL), lambda i: (l, 0, 0))],
        out_specs=[row, pl.BlockSpec((8, 128), lambda i: (0, 0)), row, pl.BlockSpec((8, D_MODEL), lambda i: (0, 0))],
        out_shape=[jax.ShapeDtypeStruct((s, D_MODEL), F32), jax.ShapeDtypeStruct((8, 128), F32),
                   jax.ShapeDtypeStruct((s, D_MODEL), BF16), jax.ShapeDtypeStruct((8, D_MODEL), F32)],
        args=[xf, target, f, g3])


def _norm_bwd_rows(v, g, dy):
    r = _rstd(v)
    vn = v * r
    gd = dy * g
    dv = r * (gd - vn * jnp.mean(vn * gd, axis=-1, keepdims=True))
    return dv, dy * vn


def _zero_first(refs, first):
    @pl.when(first)
    def _():
        for ref in refs:
            ref[...] = jnp.zeros(ref.shape, F32)


def _add_cols(ref, val):
    ref[0:1, :] += jnp.sum(val, axis=0, keepdims=True)


def _accum_cols(ref, val, first):
    _zero_first((ref,), first)
    _add_cols(ref, val)


def _row_subtiles(rows, sub):
    sub = min(sub, rows)
    return [slice(r, r + sub) for r in range(0, rows, sub)]


def _ffn_out_bwd(df, wfo4, gu, s, tm, comm=None):
    def body(df_ref, w_ref, gu_ref, dgu_ref):
        da = _dot_nt(df_ref[...], w_ref[...])
        g = gu_ref[0].astype(F32)
        u = gu_ref[1].astype(F32)
        sg = 1.0 / (1.0 + jnp.exp(-g))
        dgu_ref[0] = (da * u * (sg * (1.0 + g * (1.0 - sg)))).astype(BF16)
        dgu_ref[1] = (da * (g * sg)).astype(BF16)

    blk = pl.BlockSpec((None, 2, tm, FF_SHARD), lambda b, i: (b, 0, i, 0))
    return _call(
        body, name="ffn_out_bwd", grid=(4, s // tm),
        in_specs=[pl.BlockSpec((tm, D_MODEL), lambda b, i: (i, 0)),
                  pl.BlockSpec((None, FF_SHARD, D_MODEL), lambda b, i: (b, 0, 0)), blk],
        out_specs=[blk], out_shape=[jax.ShapeDtypeStruct((4, 2, s, FF_SHARD), BF16)],
        args=[df, wfo4, gu], comm=comm)


def _dw_ffn_out(act, df, s):
    def body(a_ref, b_ref, o_ref):
        o_ref[...] = _dot_tn(a_ref[...], b_ref[...]).astype(BF16)

    out, = _call(
        body, name="dw_ffn_out", grid=(4,),
        in_specs=[pl.BlockSpec((None, s, FF_SHARD), lambda n: (n, 0, 0)),
                  pl.BlockSpec((s, D_MODEL), lambda n: (0, 0))],
        out_specs=[pl.BlockSpec((FF_SHARD, D_MODEL), lambda n: (n, 0))],
        out_shape=[jax.ShapeDtypeStruct((D_FF, D_MODEL), BF16)], args=[act, df])
    return out


def _dw_ffn_in(h2, dgu, s):
    def body(a_ref, b_ref, o_ref):
        o_ref[...] = _dot_tn(b_ref[...], a_ref[...]).astype(BF16)

    out, = _call(
        body, name="dw_ffn_in", grid=(N_DEV,),
        in_specs=[pl.BlockSpec((s, D_MODEL), lambda n: (0, 0)),
                  pl.BlockSpec((None, None, s, FF_SHARD), lambda n: (n % 4, n // 4, 0, 0))],
        out_specs=[pl.BlockSpec((None, FF_SHARD, D_MODEL), lambda n: (n, 0, 0))],
        out_shape=[jax.ShapeDtypeStruct((N_DEV, FF_SHARD, D_MODEL), BF16)], args=[h2, dgu])
    return out


def _ffn_in_bwd(dgu, wfin, xm, g_pre3, dres, z, g_post3, l, s, tm, comm=None):
    def body(d_ref, w_ref, xm_ref, gp_ref, dres_ref, z_ref, gq_ref, dxm_ref, dz_ref, dgp_ref, dgq_ref):
        _zero_first((dgp_ref, dgq_ref), pl.program_id(0) == 0)
        for rs in _row_subtiles(tm, SUB_ROWS):
            dh = _dot(d_ref[0, 0, rs, :], w_ref[0])
            for j in range(1, N_DEV):
                dh = dh + _dot(d_ref[j % 4, j // 4, rs, :], w_ref[j])
            dx, dyn = _norm_bwd_rows(xm_ref[rs, :], gp_ref[...], dh)
            dxm = dres_ref[rs, :] + dx
            dxm_ref[rs, :] = dxm
            _add_cols(dgp_ref, dyn)
            dz, dyn2 = _norm_bwd_rows(z_ref[rs, :], gq_ref[...], dxm)
            dz_ref[rs, :] = dz.astype(BF16)
            _add_cols(dgq_ref, dyn2)

    row = pl.BlockSpec((tm, D_MODEL), lambda i: (i, 0))
    gain = pl.BlockSpec((None, 1, D_MODEL), lambda i: (l, 0, 0))
    dgs = pl.BlockSpec((8, D_MODEL), lambda i: (0, 0))
    return _call(
        body, name="ffn_in_bwd", grid=(s // tm,),
        in_specs=[pl.BlockSpec((4, 2, tm, FF_SHARD), lambda i: (0, 0, i, 0)),
                  pl.BlockSpec((N_DEV, FF_SHARD, D_MODEL), lambda i: (0, 0, 0), pipeline_mode=pl.Buffered(1)),
                  row, gain, row, row, gain],
        out_specs=[row, row, dgs, dgs],
        out_shape=[jax.ShapeDtypeStruct((s, D_MODEL), F32), jax.ShapeDtypeStruct((s, D_MODEL), BF16),
                   jax.ShapeDtypeStruct((8, D_MODEL), F32), jax.ShapeDtypeStruct((8, D_MODEL), F32)],
        args=[dgu, wfin, xm, g_pre3, dres, z, g_post3], comm=comm)


def _dw_out(ync, yna, dz, s):
    half = D_MODEL // 2

    def body(a1_ref, a2_ref, b_ref, o_ref):
        b = b_ref[...]
        o_ref[0:half, :] = _dot_tn(a1_ref[...], b).astype(BF16)
        o_ref[half:D_MODEL, :] = _dot_tn(a2_ref[...], b).astype(BF16)

    out, = _call(
        body, name="dw_out", grid=(2,),
        in_specs=[pl.BlockSpec((s, half), lambda n: (0, 0)), pl.BlockSpec((s, half), lambda n: (0, 0)),
                  pl.BlockSpec((s, half), lambda n: (0, n))],
        out_specs=[pl.BlockSpec((D_MODEL, half), lambda n: (0, n))],
        out_shape=[jax.ShapeDtypeStruct((D_MODEL, D_MODEL), BF16)], args=[ync, yna, dz])
    return out


def _out_proj_bwd(dz, wout, o, g3, l, s, tm):
    def body(dz_ref, w_ref, o_ref, g_ref, dyc_ref, do_ref, dg_ref):
        gmat = _group_matrix()
        _zero_first((dg_ref,), pl.program_id(0) == 0)
        for rs in _row_subtiles(tm, SUB_ROWS):
            dy = _dot_nt(dz_ref[rs, :], w_ref[...])
            dyc_ref[rs, :] = dy[:, 0:CONV_WIDTH]
            for j in range(ATTN_WIDTH // 128):
                c0 = 128 * j
                ov = o_ref[rs, c0:c0 + 128]
                dyn = dy[:, CONV_WIDTH + c0:CONV_WIDTH + c0 + 128]
                r = lax.rsqrt(_group_mean(ov * ov, gmat) + EPS)
                on = ov * r
                gd = dyn * g_ref[:, c0:c0 + 128]
                do_ref[rs, c0:c0 + 128] = r * (gd - on * _group_mean(on * gd, gmat))
                dg_ref[0:1, c0:c0 + 128] += jnp.sum(dyn * on, axis=0, keepdims=True)

    halfrow = pl.BlockSpec((tm, ATTN_WIDTH), lambda i: (i, 0))
    return _call(
        body, name="out_proj_bwd", grid=(s // tm,),
        in_specs=[pl.BlockSpec((tm, D_MODEL), lambda i: (i, 0)),
                  pl.BlockSpec((D_MODEL, D_MODEL), lambda i: (0, 0)), halfrow,
                  pl.BlockSpec((None, 1, ATTN_WIDTH), lambda i: (l, 0, 0))],
        out_specs=[halfrow, halfrow, pl.BlockSpec((8, ATTN_WIDTH), lambda i: (0, 0))],
        out_shape=[jax.ShapeDtypeStruct((s, CONV_WIDTH), F32), jax.ShapeDtypeStruct((s, ATTN_WIDTH), F32),
                   jax.ShapeDtypeStruct((8, ATTN_WIDTH), F32)],
        args=[dz, wout, o, g3])


def _conv_bwd(pc, dyc, wc, g3, dq, dk, dv, l, s, tr):
    hb = tr // 8
    nt = s // tr
    ext = tr + 16
    last_hb = s // 8 - 1

    def body(pc_ref, prev_ref, next_ref, dy_ref, dyn_ref, wc_ref, g_ref, dq_ref, dk_ref, dv_ref,
             dpc_ref, dw_ref, dg_ref):
        i = pl.program_id(0)
        for part, ref in enumerate((dq_ref, dk_ref, dv_ref)):
            c = 3 * CONV_WIDTH + ATTN_WIDTH * part
            dpc_ref[:, c:c + ATTN_WIDTH] = ref[...]
        gmat = _group_matrix()
        row = lax.broadcasted_iota(jnp.int32, (ext, 128), 0) + (i * tr - 8)
        inside = jnp.where(row >= 0, jnp.where(row < s, 1, 0), 0) == 1

        @pl.when(i == 0)
        def _():
            dw_ref[...] = jnp.zeros(dw_ref.shape, F32)
            dg_ref[...] = jnp.zeros(dg_ref.shape, F32)

        def extend(ref_prev, ref_mid, ref_next, c):
            parts = [ref_prev[:, c:c + 128] if ref_prev is not None else jnp.zeros((8, 128), F32),
                     ref_mid[:, c:c + 128], ref_next[:, c:c + 128]]
            return jnp.concatenate(parts, axis=0)

        for j in range(CONV_WIDTH // 128):
            c0, c1, c2 = 128 * j, CONV_WIDTH + 128 * j, 2 * CONV_WIDTH + 128 * j
            hc = extend(prev_ref, pc_ref, next_ref, c0)
            bg = extend(prev_ref, pc_ref, next_ref, c1)
            cg = extend(prev_ref, pc_ref, next_ref, c2)
            dyn = extend(None, dy_ref, dyn_ref, c0)
            w0, w1, w2 = (wc_ref[0:1, c0:c0 + 128], wc_ref[1:2, c0:c0 + 128], wc_ref[2:3, c0:c0 + 128])
            gain = g_ref[:, c0:c0 + 128]
            u = jnp.where(inside, cg * hc, 0.0)
            u1 = pltpu.roll(u, 1, 0)
            u2 = pltpu.roll(u, 2, 0)
            out = u2 * w0 + u1 * w1 + u * w2
            yc = bg * out
            r = lax.rsqrt(_group_mean(yc * yc, gmat) + EPS)
            ycn = yc * r
            gd = dyn * gain
            dyc = r * (gd - ycn * _group_mean(ycn * gd, gmat))
            dout = jnp.where(inside, dyc * bg, 0.0)
            du = dout * w2 + pltpu.roll(dout, ext - 1, 0) * w1 + pltpu.roll(dout, ext - 2, 0) * w0
            sl = slice(8, 8 + tr)
            dpc_ref[:, c0:c0 + 128] = (du[sl] * cg[sl]).astype(BF16)
            dpc_ref[:, c1:c1 + 128] = (dyc[sl] * out[sl]).astype(BF16)
            dpc_ref[:, c2:c2 + 128] = (du[sl] * hc[sl]).astype(BF16)
            dw_ref[0:1, c0:c0 + 128] += jnp.sum(dout[sl] * u2[sl], axis=0, keepdims=True)
            dw_ref[1:2, c0:c0 + 128] += jnp.sum(dout[sl] * u1[sl], axis=0, keepdims=True)
            dw_ref[2:3, c0:c0 + 128] += jnp.sum(dout[sl] * u[sl], axis=0, keepdims=True)
            dg_ref[0:1, c0:c0 + 128] += jnp.sum(dyn[sl] * ycn[sl], axis=0, keepdims=True)

    wide = 3 * CONV_WIDTH
    return _call(
        body, name="conv_bwd", grid=(nt,),
        in_specs=[pl.BlockSpec((tr, wide), lambda i: (i, 0)),
                  pl.BlockSpec((8, wide), lambda i: (jnp.maximum(i * hb - 1, 0), 0)),
                  pl.BlockSpec((8, wide), lambda i: (jnp.minimum((i + 1) * hb, last_hb), 0)),
                  pl.BlockSpec((tr, CONV_WIDTH), lambda i: (i, 0)),
                  pl.BlockSpec((8, CONV_WIDTH), lambda i: (jnp.minimum((i + 1) * hb, last_hb), 0)),
                  pl.BlockSpec((None, 8, CONV_WIDTH), lambda i: (l, 0, 0)),
                  pl.BlockSpec((None, 1, CONV_WIDTH), lambda i: (l, 0, 0)),
                  pl.BlockSpec((tr, ATTN_WIDTH), lambda i: (i, 0)),
                  pl.BlockSpec((tr, ATTN_WIDTH), lambda i: (i, 0)),
                  pl.BlockSpec((tr, ATTN_WIDTH), lambda i: (i, 0))],
        out_specs=[pl.BlockSpec((tr, PROJ_WIDTH), lambda i: (i, 0)),
                   pl.BlockSpec((8, CONV_WIDTH), lambda i: (0, 0)),
                   pl.BlockSpec((8, CONV_WIDTH), lambda i: (0, 0))],
        out_shape=[jax.ShapeDtypeStruct((s, PROJ_WIDTH), BF16), jax.ShapeDtypeStruct((8, CONV_WIDTH), F32),
                   jax.ShapeDtypeStruct((8, CONV_WIDTH), F32)],
        args=[pc, pc, pc, dyc, dyc, wc, g3, dq, dk, dv])


def _attn_bwd(qkvp, biasm, o, lse, do, l, s, pad, comm=None):
    nb = s // Q_BLOCK
    qb0 = pad // Q_BLOCK
    scale = HEAD_DIM ** -0.5
    wide = 128 * ATTN_PAIRS

    def body(q_ref, k_ref, v_ref, b_ref, o_ref, lse_ref, do_ref,
             dq_ref, dk_ref, dv_ref, ds_ref, dk_acc, dv_acc):
        blk = pl.program_id(1)

        @pl.when(blk == 0)
        def _():
            dk_acc[...] = jnp.zeros(dk_acc.shape, F32)
            dv_acc[...] = jnp.zeros(dv_acc.shape, F32)
            ds_ref[...] = jnp.zeros(ds_ref.shape, F32)

        koff = pl.multiple_of(blk * Q_BLOCK + (pad - LEFT), Q_BLOCK)
        lane = lax.broadcasted_iota(jnp.int32, (1, 128), 1)
        kpos = lax.broadcasted_iota(jnp.int32, (1, K_BAND), 1) + (blk * Q_BLOCK - LEFT)
        kmask = jnp.where(kpos >= 0, 0.0, NEG_INF)
        for pr in range(ATTN_PAIRS):
            ls = slice(128 * pr, 128 * (pr + 1))
            q = q_ref[:, ls]
            kb = k_ref[pl.ds(koff, K_BAND), ls]
            vb = v_ref[pl.ds(koff, K_BAND), ls]
            dov = do_ref[:, ls]
            lse_v = lse_ref[:, ls]
            prod = dov * o_ref[:, ls]
            dq_parts = []
            dk_new = jnp.zeros((K_BAND, 128), F32)
            dv_new = jnp.zeros((K_BAND, 128), F32)
            for hh in range(2):
                in_head = (lane >> 6) == hh
                qm = jnp.where(in_head, q, jnp.zeros_like(q)) * jnp.asarray(scale, BF16)
                dom = jnp.where(in_head, dov, 0.0).astype(BF16)
                delta = jnp.sum(jnp.where(in_head, prod, 0.0), axis=1, keepdims=True)
                lse_h = lse_v[:, HEAD_DIM * hh:HEAD_DIM * hh + 1]
                sc = _dot_nt(qm, kb) + b_ref[2 * pr + hh] + kmask
                p = jnp.exp(sc - lse_h)
                dp = _dot_nt(dom, vb)
                ds = p * (dp - delta)
                ds_ref[2 * pr + hh] += ds
                dsb = ds.astype(BF16)
                dq_parts.append(_dot(dsb, kb) * scale)
                dk_new = dk_new + _dot_tn(dsb, qm)
                dv_new = dv_new + _dot_tn(p.astype(BF16), dom)
            dq_ref[:, ls] = jnp.where(lane < HEAD_DIM, dq_parts[0], dq_parts[1]).astype(BF16)
            dk_acc[pl.ds(koff, K_BAND), ls] += dk_new
            dv_acc[pl.ds(koff, K_BAND), ls] += dv_new

        @pl.when(blk == nb - 1)
        def _():
            dk_ref[...] = dk_acc[pad:pad + s, :].astype(BF16)
            dv_ref[...] = dv_acc[pad:pad + s, :].astype(BF16)

    n_grp = ATTN_WIDTH // wide
    qblk = pl.BlockSpec((Q_BLOCK, wide), lambda p, b: (b, p))
    col = pl.BlockSpec((s, wide), lambda p, b: (0, p))
    shp = jax.ShapeDtypeStruct((s, ATTN_WIDTH), BF16)
    return _call(
        body, name="attn_bwd", grid=(n_grp, nb),
        in_specs=[pl.BlockSpec((Q_BLOCK, wide), lambda p, b: (qb0 + b, p)),
                  pl.BlockSpec((s + pad, wide), lambda p, b: (0, n_grp + p)),
                  pl.BlockSpec((s + pad, wide), lambda p, b: (0, 2 * n_grp + p)),
                  pl.BlockSpec((None, 2 * ATTN_PAIRS, Q_BLOCK, K_BAND), lambda p, b: (l, p, 0, 0)),
                  qblk, qblk, qblk],
        out_specs=[qblk, col, col, pl.BlockSpec((2 * ATTN_PAIRS, Q_BLOCK, K_BAND), lambda p, b: (p, 0, 0))],
        out_shape=[shp, shp, shp, jax.ShapeDtypeStruct((N_HEADS, Q_BLOCK, K_BAND), F32)],
        scratch=[pltpu.VMEM((s + pad, wide), F32), pltpu.VMEM((s + pad, wide), F32)],
        args=[qkvp, qkvp, qkvp, biasm, o, lse, do], comm=comm)


def _dw_in(h, dproj, s):
    def body(a_ref, b_ref, o_ref):
        acc = _dot_tn(a_ref[...], b_ref[...])
        o_ref[0] = acc[:, 0:PROJ_SHARD].astype(BF16)
        o_ref[1] = acc[:, PROJ_SHARD:2 * PROJ_SHARD].astype(BF16)

    out, = _call(
        body, name="dw_in", grid=(4,),
        in_specs=[pl.BlockSpec((s, D_MODEL), lambda n: (0, 0)),
                  pl.BlockSpec((s, 2 * PROJ_SHARD), lambda n: (0, n))],
        out_specs=[pl.BlockSpec((2, D_MODEL, PROJ_SHARD), lambda n: (n, 0, 0))],
        out_shape=[jax.ShapeDtypeStruct((N_DEV, D_MODEL, PROJ_SHARD), BF16)], args=[h, dproj])
    return out


def _in_proj_bwd(dproj, win, x, g3, dres, l, s, tm, f_prev=None, g_post3=None, comm=None):
    chain = f_prev is not None

    def body(d_ref, w_ref, x_ref, g_ref, dres_ref, *rest):
        if chain:
            f_ref, gq_ref, dx_ref, dg_ref, df_ref, dgq_ref = rest
            _zero_first((dg_ref, dgq_ref), pl.program_id(0) == 0)
        else:
            dx_ref, dg_ref = rest
            _zero_first((dg_ref,), pl.program_id(0) == 0)
        for rs in _row_subtiles(tm, SUB_ROWS):
            dh = _dot_nt(d_ref[rs, 0:PROJ_SHARD], w_ref[0])
            for j in range(1, N_DEV):
                dh = dh + _dot_nt(d_ref[rs, PROJ_SHARD * j:PROJ_SHARD * (j + 1)], w_ref[j])
            dx, dyn = _norm_bwd_rows(x_ref[rs, :], g_ref[...], dh)
            dx = dres_ref[rs, :] + dx
            dx_ref[rs, :] = dx
            _add_cols(dg_ref, dyn)
            if chain:
                df, dyn2 = _norm_bwd_rows(f_ref[rs, :], gq_ref[...], dx)
                df_ref[rs, :] = df.astype(BF16)
                _add_cols(dgq_ref, dyn2)

    row = pl.BlockSpec((tm, D_MODEL), lambda i: (i, 0))
    dgs = pl.BlockSpec((8, D_MODEL), lambda i: (0, 0))
    in_specs = [pl.BlockSpec((tm, PROJ_WIDTH), lambda i: (i, 0)),
                pl.BlockSpec((N_DEV, D_MODEL, PROJ_SHARD), lambda i: (0, 0, 0)),
                row, pl.BlockSpec((None, 1, D_MODEL), lambda i: (l, 0, 0)), row]
    out_specs = [row, dgs]
    out_shape = [jax.ShapeDtypeStruct((s, D_MODEL), F32), jax.ShapeDtypeStruct((8, D_MODEL), F32)]
    args = [dproj, win, x, g3, dres]
    if chain:
        in_specs += [row, pl.BlockSpec((None, 1, D_MODEL), lambda i: (l - 1, 0, 0))]
        out_specs += [row, dgs]
        out_shape += [jax.ShapeDtypeStruct((s, D_MODEL), BF16), jax.ShapeDtypeStruct((8, D_MODEL), F32)]
        args += [f_prev, g_post3]
    return _call(body, name="in_proj_bwd", grid=(s // tm,), in_specs=in_specs, out_specs=out_specs,
                 out_shape=out_shape, args=args, comm=comm)


def _adamw(name, w, m, v, lands):
    groups, rows, cols = w.shape
    assert len(lands) == groups
    n_part = lands[0].shape[0]
    tr = _row_tile(rows, tuple(c for c in (512, 352, 256, 176, 128, 64, 32, 16, 8) if c * cols <= 256 * 1024))
    c1 = 1.0 - ADAM_B1 ** ADAM_STEP
    c2 = 1.0 - ADAM_B2 ** ADAM_STEP

    def body(w_ref, m_ref, v_ref, *rest):
        land_refs = rest[:groups]
        g_ref, d_ref, nm_ref, nv_ref = rest[groups:]
        grp = pl.program_id(0)
        for gi in range(groups):
            @pl.when(grp == gi)
            def _():
                l_ref = land_refs[gi]
                g = l_ref[0].astype(F32)
                for p in range(1, n_part):
                    g = g + l_ref[p].astype(F32)
                g_ref[...] = g
                m1 = ADAM_B1 * m_ref[...] + (1.0 - ADAM_B1) * g
                v1 = ADAM_B2 * v_ref[...] + (1.0 - ADAM_B2) * (g * g)
                nm_ref[...] = m1
                nv_ref[...] = v1
                d_ref[...] = -ADAM_LR * ((m1 / c1) / (jnp.sqrt(v1 / c2) + ADAM_EPS) + ADAM_WD * w_ref[...])

    blk = pl.BlockSpec((None, tr, cols), lambda g, i: (g, i, 0))
    shp = jax.ShapeDtypeStruct((groups, rows, cols), F32)

    def land_spec(gi):
        return pl.BlockSpec((n_part, tr, cols), lambda g, i: (0, jnp.where(g == gi, i, 0), 0))

    return _call(
        body, name=name, grid=(groups, rows // tr),
        in_specs=[blk, blk, blk] + [land_spec(gi) for gi in range(groups)],
        out_specs=[blk, blk, blk, blk], out_shape=[shp, shp, shp, shp],
        args=[w, m, v] + list(lands))


def _pack_small(rel, gco, gao, gpm, gqm, gpf, gqf):
    n_layers = rel.shape[0]
    relp = jnp.pad(rel, ((0, 0), (0, 0), (0, REL_PAD - rel.shape[2])))
    parts = [relp.reshape(n_layers * N_HEADS * REL_PAD // 128, 128)]
    parts += [a.reshape(-1, 128) for a in (gco, gao, gpm, gqm, gpf, gqf)]
    return jnp.concatenate(parts, axis=0)


def _unpack_small(p, n_layers):
    n_rel = n_layers * N_HEADS * REL_PAD // 128
    rel = p[:n_rel].reshape(n_layers, N_HEADS, REL_PAD)[:, :, :2 * REL_CLIP + 1]
    outs = [rel]
    r0 = n_rel
    for width in (CONV_WIDTH, ATTN_WIDTH, D_MODEL, D_MODEL, D_MODEL, D_MODEL):
        nr = n_layers * width // 128
        outs.append(p[r0:r0 + nr].reshape(n_layers, width))
        r0 += nr
    return outs


def kernel(x, w_in, w_conv, rel_bias, g_conv_out, g_attn_out, w_out, g_pre_mix, g_post_mix, g_pre_ffn, g_post_ffn, w_ffn_in, w_ffn_out, loss_target, m_w_in, m_w_conv, m_rel_bias, m_g_conv_out, m_g_attn_out, m_w_out, m_g_pre_mix, m_g_post_mix, m_g_pre_ffn, m_g_post_ffn, m_w_ffn_in, m_w_ffn_out, v_w_in, v_w_conv, v_rel_bias, v_g_conv_out, v_g_attn_out, v_w_out, v_g_pre_mix, v_g_post_mix, v_g_pre_ffn, v_g_post_ffn, v_w_ffn_in, v_w_ffn_out):
    n_layers = w_in.shape[0]
    s = x.shape[1]
    assert x.shape == (1, s, D_MODEL) and s % 1024 == 0
    assert w_in.shape == (n_layers, D_MODEL, PROJ_SHARD) and w_ffn_in.shape == (n_layers, D_MODEL, FF_SHARD)
    tm = 512
    tq = 1024 if s >= 2048 else 512
    tf = min(2048, s)
    x0 = x.reshape(s, D_MODEL)
    target = loss_target.reshape(s, D_MODEL)
    dev = _dev_index(lax.axis_index("x"), lax.axis_index("y"), lax.axis_index("c"))

    wt_ffn_in, mt_ffn_in, vt_ffn_in = (jnp.transpose(a, (0, 2, 1)) for a in (w_ffn_in, m_w_ffn_in, v_w_ffn_in))
    local_w = [_cast_bf16(w_in, "cast_w_in"), _cast_bf16(w_out, "cast_w_out"),
               _cast_bf16(wt_ffn_in, "cast_w_ffn_in"), _cast_bf16(w_ffn_out, "cast_w_ffn_out")]
    wc_local = jnp.pad(jnp.transpose(w_conv, (0, 2, 1)).reshape(-1), (0, 1024 - n_layers * 3 * 64)).reshape(8, 128)
    win_next, wc_g = _comm_only("gather_first", _Gather([(local_w[0], 0), (wc_local, None)]))
    weights = [None] * n_layers
    wc_full = wc_g.reshape(N_DEV, 1024)[:, :n_layers * 3 * 64].reshape(N_DEV, n_layers, 3, 64)
    wc_full = jnp.transpose(wc_full, (1, 2, 0, 3)).reshape(n_layers, 3, CONV_WIDTH)
    wc_full = jnp.pad(wc_full, ((0, 0), (0, 5), (0, 0)))

    g3 = {k: v.reshape(n_layers, 1, -1) for k, v in dict(
        conv=g_conv_out, attn=g_attn_out, pre_mix=g_pre_mix, post_mix=g_post_mix,
        pre_ffn=g_pre_ffn, post_ffn=g_post_ffn).items()}
    biasm = _bias_build(jnp.pad(rel_bias, ((0, 0), (0, 0), (0, REL_PAD - rel_bias.shape[2]))))

    saved = []
    xl = x0
    h = _norm_cast(x0, g3["pre_mix"], 0, tm)
    for l in range(n_layers):
        win = win_next
        pc, wout = _in_proj(h, win, s, tq, 0, comm=_Gather([(local_w[1], l)]))
        qkvp, = _in_proj(h, win, s, tq, 1)
        ync = _conv_fwd(pc, wc_full, g3["conv"], l, s, tm)
        o, lse, yna, wfin = _attn_fwd(qkvp, biasm, g3["attn"], l, s, tq, comm=_Gather([(local_w[2], l)]))
        wout = wout.reshape(D_MODEL, D_MODEL)
        z, xm, h2 = _out_proj_fwd(ync, yna, wout, xl, g3["post_mix"], g3["pre_ffn"], l, s, tq)
        gu, act, wfout = _ffn_in_fwd(h2, wfin, s, tf, comm=_Gather([(local_w[3], l)]))
        wfo4 = wfout.reshape(4, FF_SHARD, D_MODEL)
        l_next = min(l + 1, n_layers - 1)
        f, xo, h_next, *got = _ffn_out_fwd(act, wfo4, xm, g3["post_ffn"], g3["pre_mix"], l, l_next, s, tm,
                                           comm=_Gather([(local_w[0], l + 1)]) if l + 1 < n_layers else None)
        weights[l] = [win, wout, wfin, wfo4]
        win_next = got[0] if got else None
        saved.append(dict(x=xl, h=h, pc=pc, qkvp=qkvp, ync=ync, yna=yna, o=o, lse=lse, z=z, xm=xm,
                          h2=h2, gu=gu, act=act, f=f))
        xl, h = xo, h_next

    dx, sq, df, dg_post_ffn = _loss_grad(xl, target, saved[-1]["f"], g3["post_ffn"], n_layers - 1, s, tm)
    loss = lax.psum(jnp.sum(sq) * (0.5 / D_MODEL), ("x", "y", "c"))

    lands = dict(win=[None] * n_layers, wout=[None] * n_layers, wfin=[None] * n_layers, wfout=[None] * n_layers)
    small = {k: [None] * n_layers for k in ("gco", "gao", "gpm", "gqm", "gpf", "gqf", "wc")}
    d_rel = [None] * n_layers
    pending = []
    for l in reversed(range(n_layers)):
        sv = saved[l]
        win, wout, wfin, wfo4 = weights[l]
        small["gqf"][l] = dg_post_ffn[0]
        dgu, *got = _ffn_out_bwd(df, wfo4, sv["gu"], s, tf,
                                 comm=_Scatter([(a, False) for _, _, a in pending]) if pending else None)
        for (key, lay, _), land in zip(pending, got):
            lands[key][lay] = land
        d_wfout = _dw_ffn_out(sv["act"], df, s).reshape(N_DEV, FFO_SHARD, D_MODEL)
        d_wfin = _dw_ffn_in(sv["h2"], dgu, s)
        dxm, dz, dg_pre_ffn, dg_post_mix, lands["wfout"][l] = _ffn_in_bwd(
            dgu, wfin, sv["xm"], g3["pre_ffn"], dx, sv["z"], g3["post_mix"], l, s, tm,
            comm=_Scatter([(d_wfout, False)]))
        small["gpf"][l] = dg_pre_ffn[0]
        small["gqm"][l] = dg_post_mix[0]
        d_wout = _dw_out(sv["ync"], sv["yna"], dz, s).reshape(N_DEV, D_MODEL // N_DEV, D_MODEL)
        dyc, do, dg_attn = _out_proj_bwd(dz, wout, sv["o"], g3["attn"], l, s, tq)
        small["gao"][l] = dg_attn[0]
        dq, dk, dv, ds_sum, lands["wfin"][l] = _attn_bwd(
            sv["qkvp"], biasm, sv["o"], sv["lse"], do, l, s, tq, comm=_Scatter([(d_wfin, False)]))
        d_rel[l] = _bias_bwd(ds_sum[None])
        dproj, dwc, dg_conv = _conv_bwd(sv["pc"], dyc, wc_full, g3["conv"], dq, dk, dv, l, s, tm)
        small["wc"][l] = dwc[0:3]
        small["gco"][l] = dg_conv[0]
        d_win = _dw_in(sv["h"], dproj, s)
        if l > 0:
            dx, dg_pre_mix, df, dg_post_ffn, lands["wout"][l] = _in_proj_bwd(
                dproj, win, sv["x"], g3["pre_mix"], dxm, l, s, tm, f_prev=saved[l - 1]["f"],
                g_post3=g3["post_ffn"], comm=_Scatter([(d_wout, False)]))
            pending = [("win", l, d_win)]
        else:
            dx, dg_pre_mix = _in_proj_bwd(dproj, win, sv["x"], g3["pre_mix"], dxm, l, s, tm)
            pending = [("win", l, d_win), ("wout", l, d_wout)]
        small["gpm"][l] = dg_pre_mix[0]
    grad_x = dx.reshape(1, s, D_MODEL)

    small_vec = jnp.concatenate(
        [_pack_small(jnp.concatenate(d_rel)[:, :, :2 * REL_CLIP + 1],
                     *[jnp.stack(small[k]) for k in ("gco", "gao", "gpm", "gqm", "gpf", "gqf")]),
         jnp.stack(small["wc"]).reshape(-1, 128)], axis=0)
    small_vec = jnp.pad(small_vec, ((0, (-small_vec.shape[0]) % 8), (0, 0)))
    *got, land_small = _comm_only(
        "exchange_last", _Scatter([(a, False) for _, _, a in pending] + [(small_vec, True)]))
    for (key, lay, _), land in zip(pending, got):
        lands[key][lay] = land

    r_in = _adamw("adamw_w_in", w_in, m_w_in, v_w_in, lands["win"])
    r_out = _adamw("adamw_w_out", w_out, m_w_out, v_w_out, lands["wout"])
    r_fin = [jnp.transpose(t, (0, 2, 1)) for t in _adamw("adamw_w_ffn_in", wt_ffn_in, mt_ffn_in, vt_ffn_in, lands["wfin"])]
    r_fout = _adamw("adamw_w_ffn_out", w_ffn_out, m_w_ffn_out, v_w_ffn_out, lands["wfout"])

    n_rep = 64 * n_layers
    rep = _adamw(
        "adamw_replicated",
        _pack_small(rel_bias, g_conv_out, g_attn_out, g_pre_mix, g_post_mix, g_pre_ffn, g_post_ffn)[None],
        _pack_small(m_rel_bias, m_g_conv_out, m_g_attn_out, m_g_pre_mix, m_g_post_mix, m_g_pre_ffn, m_g_post_ffn)[None],
        _pack_small(v_rel_bias, v_g_conv_out, v_g_attn_out, v_g_pre_mix, v_g_post_mix, v_g_pre_ffn, v_g_post_ffn)[None],
        [land_small[:, :n_rep]])
    rep = [_unpack_small(t[0], n_layers) for t in rep]

    wc_rows = n_layers * 3 * CONV_WIDTH // 128
    zeros_wc = jnp.zeros((1, wc_rows, 128), F32)
    g_wc_full = _adamw("sum_w_conv", zeros_wc, zeros_wc, zeros_wc, [land_small[:, n_rep:n_rep + wc_rows]])[0]
    g_wc_full = g_wc_full.reshape(n_layers, 3, CONV_WIDTH)
    g_wc = lax.dynamic_slice_in_dim(g_wc_full, dev * (CONV_WIDTH // N_DEV), CONV_WIDTH // N_DEV, axis=2)
    g_wc = jnp.transpose(g_wc, (0, 2, 1))

    def tiny(a):
        flat = a.reshape(-1)
        return jnp.pad(flat, (0, (-flat.shape[0]) % 1024)).reshape(1, -1, 128)

    r_wc = _adamw("adamw_w_conv", tiny(w_conv), tiny(m_w_conv), tiny(v_w_conv), [tiny(g_wc)])
    r_wc = [t.reshape(-1)[:w_conv.size].reshape(w_conv.shape) for t in r_wc]

    def leaf(kind):
        return [r_in[kind], r_wc[kind], rep[kind][0], rep[kind][1], rep[kind][2], r_out[kind],
                rep[kind][3], rep[kind][4], rep[kind][5], rep[kind][6], r_fin[kind], r_fout[kind]]

    return (loss, grad_x, *leaf(0), *leaf(1), *leaf(2), *leaf(3))
```

```python
import math

import jax
import jax.numpy as jnp
from jax import lax
from jax.experimental import pallas as pl
from jax.experimental.pallas import tpu as pltpu

F32 = jnp.float32
BF16 = jnp.bfloat16

D_MODEL = 1024
N_DEV = 8
CHUNK = 64
N_LEFT_CHUNKS = 8
CONV_WIDTH = 512
ATTN_WIDTH = 512
HEAD_DIM = 64
N_HEADS = 8
REL_CLIP = 128
REL_PAD = 384
PROJ_WIDTH = 3072
PROJ_SHARD = PROJ_WIDTH // N_DEV
D_FF = 2816
FF_SHARD = 2 * D_FF // N_DEV
FFO_SHARD = D_FF // N_DEV
EPS = 1e-6
NEG_INF = -1e30
ATTN_PAIRS = 2
Q_BLOCK = 4 * CHUNK
K_BAND = Q_BLOCK + N_LEFT_CHUNKS * CHUNK
LEFT = N_LEFT_CHUNKS * CHUNK
TOEP = 1024

ADAM_LR = 0.001
ADAM_B1 = 0.9
ADAM_B2 = 0.999
ADAM_EPS = 1e-08
ADAM_WD = 0.01
ADAM_STEP = 10

VMEM_LIMIT = 52 * 1024 * 1024
SUB_ROWS = 256
MESH = pl.DeviceIdType.MESH
ANY = pl.BlockSpec(memory_space=pl.ANY)

NT = (((1,), (1,)), ((), ()))
TN = (((0,), (0,)), ((), ()))


def _dot(a, b):
    return jnp.dot(a, b, preferred_element_type=F32)


def _dot_nt(a, b):
    return lax.dot_general(a, b, NT, preferred_element_type=F32)


def _dot_tn(a, b):
    return lax.dot_general(a, b, TN, preferred_element_type=F32)


def _rstd(v):
    return lax.rsqrt(jnp.mean(v * v, axis=-1, keepdims=True) + EPS)


def _group_matrix():
    r = lax.broadcasted_iota(jnp.int32, (128, 128), 0) >> 6
    c = lax.broadcasted_iota(jnp.int32, (128, 128), 1) >> 6
    return jnp.where(r == c, 1.0, 0.0).astype(BF16)


def _group_mean(v, gmat):
    hi = v.astype(BF16)
    lo = (v - hi.astype(F32)).astype(BF16)
    return (_dot(hi, gmat) + _dot(lo, gmat)) * (1.0 / HEAD_DIM)


def _split3(v):
    hi = v.astype(BF16)
    r1 = v - hi.astype(F32)
    mid = r1.astype(BF16)
    lo = (r1 - mid.astype(F32)).astype(BF16)
    return hi, mid, lo


def _row_tile(rows, cands=(1024, 512, 704, 256, 128, 64, 32, 16)):
    for c in cands:
        if rows % c == 0:
            return c
    return rows


def _dev_index(px, py, pc):
    return 4 * px + 2 * py + pc


def _when(cond):
    if cond is True:
        return lambda fn: fn()
    return pl.when(cond)


def _phases(grid):
    def phases():
        if not grid:
            return True, True, True
        lin = pl.program_id(0)
        for a in range(1, len(grid)):
            lin = lin * grid[a] + pl.program_id(a)
        total = math.prod(grid)
        return lin == 0, lin == (3 * total) // 4, lin == total - 1
    return phases


class _Gather:
    def __init__(self, items):
        self.items = items
        self.args = [a for a, _ in items]
        n = len(items)
        self.out_shape = [jax.ShapeDtypeStruct((N_DEV,) + (a.shape if lay is None else a.shape[1:]), a.dtype)
                          for a, lay in items]
        self.scratch = [pltpu.SemaphoreType.DMA((n, 7)), pltpu.SemaphoreType.DMA((n, 7)),
                        pltpu.SemaphoreType.DMA((n,))]

    def _ctx(self, ins, outs, sems):
        send_sems, recv_sems, local_sems = sems
        x, y, c = lax.axis_index("x"), lax.axis_index("y"), lax.axis_index("c")
        chips = [(1 - x, y), (x, 1 - y), (1 - x, 1 - y)]

        def src(k):
            lay = self.items[k][1]
            return ins[k] if lay is None else ins[k].at[lay]

        def copy(k, s, idx, to, from_src=False):
            return pltpu.make_async_remote_copy(
                src_ref=src(k) if from_src else outs[k].at[idx], dst_ref=outs[k].at[idx],
                send_sem=send_sems.at[k, s], recv_sem=recv_sems.at[k, s],
                device_id=to, device_id_type=MESH)

        def local(k):
            return pltpu.make_async_copy(src(k), outs[k].at[_dev_index(x, y, c)], local_sems.at[k])

        return x, y, c, chips, copy, local

    def start(self, ins, outs, sems, cond):
        n = len(self.items)

        @_when(cond)
        def _():
            x, y, c, chips, copy, local = self._ctx(ins, outs, sems)
            me = _dev_index(x, y, c)
            for k in range(n):
                local(k).start()
                copy(k, 0, me, (x, y, 1 - c), from_src=True).start()
                for j, chip in enumerate(chips):
                    copy(k, 1 + j, me, (chip[0], chip[1], c), from_src=True).start()

    def forward(self, ins, outs, sems, cond):
        n = len(self.items)

        @_when(cond)
        def _():
            x, y, c, chips, copy, local = self._ctx(ins, outs, sems)
            for j, chip in enumerate(chips):
                idx = _dev_index(chip[0], chip[1], c)
                for k in range(n):
                    copy(k, 1 + j, idx, (x, y, c)).wait_recv()
                    copy(k, 4 + j, idx, (x, y, 1 - c)).start()

    def finish(self, ins, outs, sems, cond):
        n = len(self.items)

        @_when(cond)
        def _():
            x, y, c, chips, copy, local = self._ctx(ins, outs, sems)
            me = _dev_index(x, y, c)
            for k in range(n):
                copy(k, 0, _dev_index(x, y, 1 - c), (x, y, c)).wait_recv()
            for j, chip in enumerate(chips):
                idx = _dev_index(chip[0], chip[1], 1 - c)
                for k in range(n):
                    copy(k, 4 + j, idx, (x, y, c)).wait_recv()
            for k in range(n):
                for s in range(4):
                    copy(k, s, me, (x, y, c), from_src=True).wait_send()
                for j, chip in enumerate(chips):
                    copy(k, 4 + j, _dev_index(chip[0], chip[1], c), (x, y, c)).wait_send()
                local(k).wait()


_PEER_FLIPS = [(0, 0, 1), (1, 0, 0), (0, 1, 0), (1, 1, 0), (1, 0, 1), (0, 1, 1), (1, 1, 1)]


def _call(body, *, name, grid, in_specs, out_specs, out_shape, args, scratch=(), comm=None):
    n_hi, n_ho, n_hs = len(args), len(out_shape), len(scratch)
    c_args = list(comm.args) if comm else []
    c_out = list(comm.out_shape) if comm else []
    c_scr = list(comm.scratch) if comm else []
    phases = _phases(grid)

    def kern(*refs):
        cuts = [n_hi, len(c_args), n_ho, len(c_out), n_hs, len(c_scr)]
        parts, pos = [], 0
        for n in cuts:
            parts.append(refs[pos:pos + n])
            pos += n
        hi, ci, ho, co, hs, cs = parts
        if comm:
            first, mid, last = phases()
            comm.start(ci, co, cs, first)
            comm.forward(ci, co, cs, mid)
        body(*hi, *ho, *hs)
        if comm:
            comm.finish(ci, co, cs, last)

    sem = ("arbitrary",) * len(grid) if grid else None
    return pl.pallas_call(
        kern, name=name, grid=grid,
        in_specs=list(in_specs) + [ANY] * len(c_args),
        out_specs=list(out_specs) + [ANY] * len(c_out),
        out_shape=list(out_shape) + c_out,
        scratch_shapes=list(scratch) + c_scr,
        compiler_params=pltpu.CompilerParams(dimension_semantics=sem, vmem_limit_bytes=VMEM_LIMIT),
    )(*args, *c_args)


def _comm_only(name, comm):
    return _call(lambda: None, name=name, grid=(), in_specs=[], out_specs=[], out_shape=[], args=[], comm=comm)


HBM_SPEC = pl.BlockSpec(memory_space=pltpu.HBM)
SEM_SPEC = pl.BlockSpec(memory_space=pltpu.SEMAPHORE)
SIDE_EFFECT = pltpu.SideEffectType.DATAFLOW_SIDE_EFFECTING


def _exchange_peer(x, y, c, s):
    fx, fy, fc = _PEER_FLIPS[s]
    return x ^ fx, y ^ fy, c ^ fc


def _exchange_start(name, items):
    n = len(items)
    srcs = [pltpu.with_memory_space_constraint(a, pltpu.HBM) for a, _ in items]
    land_shapes = [(N_DEV,) + (a.shape if whole else a.shape[1:]) for a, whole in items]
    lands = [pltpu.with_memory_space_constraint(lax.empty(shp, a.dtype), pltpu.HBM)
             for shp, (a, _) in zip(land_shapes, items)]

    n_sem = 7 * n

    def body(*refs):
        src_refs, land_refs = refs[:n], refs[n:2 * n]
        send_sems = refs[2 * n:2 * n + n_sem]
        recv_sems = refs[2 * n + n_sem:2 * n + 2 * n_sem]
        token = refs[-1]
        x, y, c = lax.axis_index("x"), lax.axis_index("y"), lax.axis_index("c")
        me = _dev_index(x, y, c)
        for s in range(7):
            px, py, pc = _exchange_peer(x, y, c, s)
            for k in range(n):
                src = src_refs[k] if items[k][1] else src_refs[k].at[_dev_index(px, py, pc)]
                pltpu.make_async_remote_copy(
                    src_ref=src, dst_ref=land_refs[k].at[me],
                    send_sem=send_sems[7 * k + s], recv_sem=recv_sems[7 * k + s],
                    device_id=(px, py, pc), device_id_type=MESH).start()
        token[...] = jnp.zeros(token.shape, token.dtype)

    outs = pl.pallas_call(
        body, name=name,
        out_shape=(*[pltpu.SemaphoreType.DMA(())] * (2 * n_sem),
                   *[pltpu.HBM(a.shape, a.dtype) for a in srcs],
                   *[pltpu.HBM(shp, a.dtype) for shp, a in zip(land_shapes, srcs)],
                   jax.ShapeDtypeStruct((8, 128), F32)),
        in_specs=[HBM_SPEC] * (2 * n),
        out_specs=(*[SEM_SPEC] * (2 * n_sem), *[HBM_SPEC] * (2 * n), pl.BlockSpec(memory_space=pltpu.VMEM)),
        input_output_aliases={i: 2 * n_sem + i for i in range(2 * n)},
        compiler_params=pltpu.CompilerParams(has_side_effects=SIDE_EFFECT),
    )(*srcs, *lands)
    base = 2 * n_sem
    return (list(outs[:n_sem]), list(outs[n_sem:base]), list(outs[base:base + n]),
            list(outs[base + n:base + 2 * n]), outs[-1])


def _exchange_wait(name, items, send_sems, recv_sems, srcs, lands, after):
    n = len(items)

    n_sem = 7 * n

    def body(*refs):
        src_refs, land_refs = refs[:n], refs[n:2 * n]
        send_refs = refs[2 * n:2 * n + n_sem]
        recv_refs = refs[2 * n + n_sem:2 * n + 2 * n_sem]
        x, y, c = lax.axis_index("x"), lax.axis_index("y"), lax.axis_index("c")
        for s in range(7):
            for k in range(n):
                copy = pltpu.make_async_remote_copy(
                    src_ref=src_refs[k] if items[k][1] else src_refs[k].at[0], dst_ref=land_refs[k].at[0],
                    send_sem=send_refs[7 * k + s], recv_sem=recv_refs[7 * k + s],
                    device_id=(x, y, c), device_id_type=MESH)
                copy.wait_send()
                copy.wait_recv()

    outs = pl.pallas_call(
        body, name=name,
        out_shape=(*[pltpu.HBM(a.shape, a.dtype) for a in srcs], *[pltpu.HBM(a.shape, a.dtype) for a in lands]),
        in_specs=[HBM_SPEC] * (2 * n) + [SEM_SPEC] * (2 * n_sem) + [ANY],
        out_specs=tuple([HBM_SPEC] * (2 * n)),
        input_output_aliases={i: i for i in range(2 * n)},
        compiler_params=pltpu.CompilerParams(has_side_effects=SIDE_EFFECT),
    )(*srcs, *lands, *send_sems, *recv_sems, after)
    return list(outs[:n]), list(outs[n:])


def _cast_bf16(x, name):
    shape = x.shape
    x2 = x.reshape(-1, shape[-1])
    rows, cols = x2.shape
    tr = _row_tile(rows)

    def body(x_ref, o_ref):
        o_ref[...] = x_ref[...].astype(BF16)

    blk = pl.BlockSpec((tr, cols), lambda i: (i, 0))
    out, = _call(body, name=name, grid=(rows // tr,), in_specs=[blk], out_specs=[blk],
                 out_shape=[jax.ShapeDtypeStruct((rows, cols), BF16)], args=[x2])
    return out.reshape(shape)


def _norm_cast(x, g3, l, tm):
    s = x.shape[0]

    def body(x_ref, g_ref, o_ref):
        v = x_ref[...]
        o_ref[...] = (v * _rstd(v) * g_ref[...]).astype(BF16)

    row = pl.BlockSpec((tm, D_MODEL), lambda i: (i, 0))
    out, = _call(body, name="norm_cast", grid=(s // tm,),
                 in_specs=[row, pl.BlockSpec((None, 1, D_MODEL), lambda i: (l, 0, 0))], out_specs=[row],
                 out_shape=[jax.ShapeDtypeStruct((s, D_MODEL), BF16)], args=[x, g3])
    return out


def _in_proj(h, win, s, tq, part, comm=None):
    pad = part
    dtype = BF16 if part else F32

    def body(a_ref, b_ref, o_ref):
        def compute():
            a = a_ref[...]
            for j in range(4):
                o_ref[:, PROJ_SHARD * j:PROJ_SHARD * (j + 1)] = _dot(a, b_ref[j]).astype(dtype)

        if pad:
            i = pl.program_id(0)

            @pl.when(i == 0)
            def _():
                o_ref[...] = jnp.zeros(o_ref.shape, dtype)

            pl.when(i > 0)(compute)
        else:
            compute()

    return _call(
        body, name="in_proj_qkv" if part else "in_proj_conv", grid=(s // tq + pad,),
        in_specs=[pl.BlockSpec((tq, D_MODEL), lambda i: (jnp.maximum(i - pad, 0), 0)),
                  pl.BlockSpec((4, D_MODEL, PROJ_SHARD), lambda i: (part, 0, 0))],
        out_specs=[pl.BlockSpec((tq, 4 * PROJ_SHARD), lambda i: (i, 0))],
        out_shape=[jax.ShapeDtypeStruct((s + pad * tq, PROJ_WIDTH // 2), dtype)], args=[h, win], comm=comm)


def _conv_fwd(pc, wc, g3, l, s, tr):
    hb = tr // 8

    def body(pc_ref, prev_ref, wc_ref, g_ref, o_ref):
        i = pl.program_id(0)
        gmat = _group_matrix()
        for j in range(CONV_WIDTH // 128):
            c0, c1, c2 = 128 * j, CONV_WIDTH + 128 * j, 2 * CONV_WIDTH + 128 * j
            hc = pc_ref[:, c0:c0 + 128]
            bg = pc_ref[:, c1:c1 + 128]
            cg = pc_ref[:, c2:c2 + 128]
            u_prev = jnp.where(i > 0, prev_ref[:, c2:c2 + 128] * prev_ref[:, c0:c0 + 128], 0.0)
            u = cg * hc
            full = jnp.concatenate([u_prev, u], axis=0)
            u1 = pltpu.roll(full, 1, 0)[8:]
            u2 = pltpu.roll(full, 2, 0)[8:]
            out = (u2 * wc_ref[0:1, c0:c0 + 128] + u1 * wc_ref[1:2, c0:c0 + 128]
                   + u * wc_ref[2:3, c0:c0 + 128])
            yc = bg * out
            r = lax.rsqrt(_group_mean(yc * yc, gmat) + EPS)
            o_ref[:, c0:c0 + 128] = (yc * r * g_ref[:, c0:c0 + 128]).astype(BF16)

    out, = _call(
        body, name="conv_fwd", grid=(s // tr,),
        in_specs=[pl.BlockSpec((tr, 3 * CONV_WIDTH), lambda i: (i, 0)),
                  pl.BlockSpec((8, 3 * CONV_WIDTH), lambda i: (jnp.maximum(i * hb - 1, 0), 0)),
                  pl.BlockSpec((None, 8, CONV_WIDTH), lambda i: (l, 0, 0)),
                  pl.BlockSpec((None, 1, CONV_WIDTH), lambda i: (l, 0, 0))],
        out_specs=[pl.BlockSpec((tr, CONV_WIDTH), lambda i: (i, 0))],
        out_shape=[jax.ShapeDtypeStruct((s, CONV_WIDTH), BF16)], args=[pc, pc, wc, g3])
    return out


def _toeplitz_source():
    r_i = lax.broadcasted_iota(jnp.int32, (REL_PAD, TOEP), 0)
    m_i = lax.broadcasted_iota(jnp.int32, (REL_PAD, TOEP), 1)
    idx = jnp.clip((K_BAND - 1) - m_i, -REL_CLIP, REL_CLIP) + REL_CLIP
    return jnp.where(r_i == idx, 1.0, 0.0).astype(BF16)


def _bias_build(rbp):
    n_layers = rbp.shape[0]

    def body(rb_ref, o_ref, t_ref):
        pmat = _toeplitz_source()
        hi, mid, lo = _split3(rb_ref[...])
        t_ref[...] = _dot(hi, pmat) + _dot(mid, pmat) + _dot(lo, pmat)
        shift = (CHUNK - 1) - lax.broadcasted_iota(jnp.int32, (CHUNK, TOEP), 0)
        kchunk = lax.broadcasted_iota(jnp.int32, (CHUNK, K_BAND), 1) >> 6
        for h in range(N_HEADS):
            b = jnp.broadcast_to(t_ref[pl.ds(h, 1), :], (CHUNK, TOEP))
            for bit in range(6):
                rolled = pltpu.roll(b, TOEP - (1 << bit), 1)
                b = jnp.where(((shift >> bit) & 1) == 1, rolled, b)
            for cq in range(Q_BLOCK // CHUNK):
                off = CHUNK * (Q_BLOCK // CHUNK - 1 - cq)
                band = pltpu.roll(b, TOEP - off, 1) if off else b
                dchunk = kchunk - cq
                in_band = jnp.where(dchunk >= 0, jnp.where(dchunk <= N_LEFT_CHUNKS, 1, 0), 0) == 1
                o_ref[h, CHUNK * cq:CHUNK * (cq + 1), :] = jnp.where(in_band, band[:, :K_BAND], NEG_INF)

    out, = _call(
        body, name="bias_build", grid=(n_layers,),
        in_specs=[pl.BlockSpec((None, N_HEADS, REL_PAD), lambda l: (l, 0, 0))],
        out_specs=[pl.BlockSpec((None, N_HEADS, Q_BLOCK, K_BAND), lambda l: (l, 0, 0, 0))],
        out_shape=[jax.ShapeDtypeStruct((n_layers, N_HEADS, Q_BLOCK, K_BAND), F32)],
        scratch=[pltpu.VMEM((N_HEADS, TOEP), F32)], args=[rbp])
    return out


def _bias_bwd(ds_sum):
    n_layers = ds_sum.shape[0]

    def body(ds_ref, o_ref, t_ref):
        pmat = _toeplitz_source()
        shift = (CHUNK - 1) - lax.broadcasted_iota(jnp.int32, (CHUNK, TOEP), 0)
        for h in range(N_HEADS):
            d = None
            for cq in range(Q_BLOCK // CHUNK):
                off = CHUNK * (Q_BLOCK // CHUNK - 1 - cq)
                part = jnp.concatenate([ds_ref[h, CHUNK * cq:CHUNK * (cq + 1), :],
                                        jnp.zeros((CHUNK, TOEP - K_BAND), F32)], axis=1)
                part = pltpu.roll(part, off, 1) if off else part
                d = part if d is None else d + part
            for bit in range(6):
                rolled = pltpu.roll(d, 1 << bit, 1)
                d = jnp.where(((shift >> bit) & 1) == 1, rolled, d)
            t_ref[pl.ds(h, 1), :] = jnp.sum(d, axis=0, keepdims=True)
        hi, mid, lo = _split3(t_ref[...])
        o_ref[...] = _dot_nt(hi, pmat) + _dot_nt(mid, pmat) + _dot_nt(lo, pmat)

    out, = _call(
        body, name="bias_bwd", grid=(n_layers,),
        in_specs=[pl.BlockSpec((None, N_HEADS, Q_BLOCK, K_BAND), lambda l: (l, 0, 0, 0))],
        out_specs=[pl.BlockSpec((None, N_HEADS, REL_PAD), lambda l: (l, 0, 0))],
        out_shape=[jax.ShapeDtypeStruct((n_layers, N_HEADS, REL_PAD), F32)],
        scratch=[pltpu.VMEM((N_HEADS, TOEP), F32)], args=[ds_sum])
    return out


def _attn_fwd(qkvp, biasm, g3, l, s, pad, comm=None):
    nb = s // Q_BLOCK
    qb0 = pad // Q_BLOCK
    scale = HEAD_DIM ** -0.5
    wide = 128 * ATTN_PAIRS

    def body(q_ref, k_ref, v_ref, b_ref, g_ref, o_ref, lse_ref, yn_ref):
        blk = pl.program_id(1)
        koff = pl.multiple_of(blk * Q_BLOCK + (pad - LEFT), Q_BLOCK)
        lane = lax.broadcasted_iota(jnp.int32, (1, 128), 1)
        kpos = lax.broadcasted_iota(jnp.int32, (1, K_BAND), 1) + (blk * Q_BLOCK - LEFT)
        kmask = jnp.where(kpos >= 0, 0.0, NEG_INF)
        gmat = _group_matrix()
        for pr in range(ATTN_PAIRS):
            ls = slice(128 * pr, 128 * (pr + 1))
            q = q_ref[:, ls]
            kb = k_ref[pl.ds(koff, K_BAND), ls]
            vb = v_ref[pl.ds(koff, K_BAND), ls]
            outs, lses = [], []
            for hh in range(2):
                in_head = (lane >> 6) == hh
                qm = jnp.where(in_head, q, jnp.zeros_like(q)) * jnp.asarray(scale, BF16)
                sc = _dot_nt(qm, kb) + b_ref[2 * pr + hh] + kmask
                m = jnp.max(sc, axis=1, keepdims=True)
                e = jnp.exp(sc - m)
                den = jnp.sum(e, axis=1, keepdims=True)
                outs.append(_dot(e.astype(BF16), vb) * (1.0 / den))
                lses.append(m + jnp.log(den))
            first = lane < HEAD_DIM
            o = jnp.where(first, outs[0], outs[1])
            o_ref[:, ls] = o
            lse_ref[:, ls] = jnp.where(first, lses[0], lses[1])
            r = lax.rsqrt(_group_mean(o * o, gmat) + EPS)
            yn_ref[:, ls] = (o * r * g_ref[:, ls]).astype(BF16)

    blk_out = pl.BlockSpec((Q_BLOCK, wide), lambda p, b: (b, p))
    n_grp = ATTN_WIDTH // wide
    return _call(
        body, name="attn_fwd", grid=(n_grp, nb),
        in_specs=[pl.BlockSpec((Q_BLOCK, wide), lambda p, b: (qb0 + b, p)),
                  pl.BlockSpec((s + pad, wide), lambda p, b: (0, n_grp + p)),
                  pl.BlockSpec((s + pad, wide), lambda p, b: (0, 2 * n_grp + p)),
                  pl.BlockSpec((None, 2 * ATTN_PAIRS, Q_BLOCK, K_BAND), lambda p, b: (l, p, 0, 0)),
                  pl.BlockSpec((None, 1, wide), lambda p, b: (l, 0, p))],
        out_specs=[blk_out, blk_out, blk_out],
        out_shape=[jax.ShapeDtypeStruct((s, ATTN_WIDTH), F32),
                   jax.ShapeDtypeStruct((s, ATTN_WIDTH), F32),
                   jax.ShapeDtypeStruct((s, ATTN_WIDTH), BF16)],
        args=[qkvp, qkvp, qkvp, biasm, g3], comm=comm)


def _out_proj_fwd(ync, yna, wout, x, g_post3, g_next3, l, s, tm):
    half = D_MODEL // 2

    def body(a1_ref, a2_ref, w_ref, x_ref, gp_ref, gn_ref, z_ref, xm_ref, h_ref):
        for rs in _row_subtiles(tm, SUB_ROWS):
            z = _dot(a1_ref[rs, :], w_ref[0:half, :]) + _dot(a2_ref[rs, :], w_ref[half:D_MODEL, :])
            z_ref[rs, :] = z
            xm = x_ref[rs, :] + z * _rstd(z) * gp_ref[...]
            xm_ref[rs, :] = xm
            h_ref[rs, :] = (xm * _rstd(xm) * gn_ref[...]).astype(BF16)

    row = pl.BlockSpec((tm, D_MODEL), lambda i: (i, 0))
    gain = pl.BlockSpec((None, 1, D_MODEL), lambda i: (l, 0, 0))
    return _call(
        body, name="out_proj_fwd", grid=(s // tm,),
        in_specs=[pl.BlockSpec((tm, half), lambda i: (i, 0)), pl.BlockSpec((tm, half), lambda i: (i, 0)),
                  pl.BlockSpec((D_MODEL, D_MODEL), lambda i: (0, 0)), row, gain, gain],
        out_specs=[row, row, row],
        out_shape=[jax.ShapeDtypeStruct((s, D_MODEL), F32), jax.ShapeDtypeStruct((s, D_MODEL), F32),
                   jax.ShapeDtypeStruct((s, D_MODEL), BF16)],
        args=[ync, yna, wout, x, g_post3, g_next3])


def _ffn_in_fwd(h2, wfin, s, tm, comm=None):
    def body(h_ref, wg_ref, wu_ref, gu_ref, act_ref):
        h = h_ref[...]
        gate = _dot_nt(h, wg_ref[...])
        up = _dot_nt(h, wu_ref[...])
        gu_ref[0] = gate.astype(BF16)
        gu_ref[1] = up.astype(BF16)
        act_ref[...] = (gate * (1.0 / (1.0 + jnp.exp(-gate))) * up).astype(BF16)

    return _call(
        body, name="ffn_in_fwd", grid=(4, s // tm),
        in_specs=[pl.BlockSpec((tm, D_MODEL), lambda b, i: (i, 0)),
                  pl.BlockSpec((None, FF_SHARD, D_MODEL), lambda b, i: (b, 0, 0)),
                  pl.BlockSpec((None, FF_SHARD, D_MODEL), lambda b, i: (4 + b, 0, 0))],
        out_specs=[pl.BlockSpec((None, 2, tm, FF_SHARD), lambda b, i: (b, 0, i, 0)),
                   pl.BlockSpec((None, tm, FF_SHARD), lambda b, i: (b, i, 0))],
        out_shape=[jax.ShapeDtypeStruct((4, 2, s, FF_SHARD), BF16), jax.ShapeDtypeStruct((4, s, FF_SHARD), BF16)],
        args=[h2, wfin, wfin], comm=comm)


def _ffn_out_fwd(act, wfo4, xm, g_post3, g_next3, l, l_next, s, tm, comm=None):
    def body(a_ref, w_ref, x_ref, gp_ref, gn_ref, f_ref, xo_ref, h_ref):
        for rs in _row_subtiles(tm, SUB_ROWS):
            f = _dot(a_ref[0, rs, :], w_ref[0])
            for b in range(1, 4):
                f = f + _dot(a_ref[b, rs, :], w_ref[b])
            f_ref[rs, :] = f
            xo = x_ref[rs, :] + f * _rstd(f) * gp_ref[...]
            xo_ref[rs, :] = xo
            h_ref[rs, :] = (xo * _rstd(xo) * gn_ref[...]).astype(BF16)

    row = pl.BlockSpec((tm, D_MODEL), lambda i: (i, 0))
    return _call(
        body, name="ffn_out_fwd", grid=(s // tm,),
        in_specs=[pl.BlockSpec((4, tm, FF_SHARD), lambda i: (0, i, 0)),
                  pl.BlockSpec((4, FF_SHARD, D_MODEL), lambda i: (0, 0, 0)), row,
                  pl.BlockSpec((None, 1, D_MODEL), lambda i: (l, 0, 0)),
                  pl.BlockSpec((None, 1, D_MODEL), lambda i: (l_next, 0, 0))],
        out_specs=[row, row, row],
        out_shape=[jax.ShapeDtypeStruct((s, D_MODEL), F32), jax.ShapeDtypeStruct((s, D_MODEL), F32),
                   jax.ShapeDtypeStruct((s, D_MODEL), BF16)],
        args=[act, wfo4, xm, g_post3, g_next3], comm=comm)


def _loss_grad(xf, target, f, g3, l, s, tm):
    def body(x_ref, t_ref, f_ref, g_ref, dx_ref, sq_ref, df_ref, dg_ref):
        i = pl.program_id(0)
        err = x_ref[...] - t_ref[...]
        dx = err * (1.0 / D_MODEL)
        dx_ref[...] = dx
        df, dyn = _norm_bwd_rows(f_ref[...], g_ref[...], dx)
        df_ref[...] = df.astype(BF16)
        _accum_cols(dg_ref, dyn, i == 0)
        cs = jnp.sum(err * err, axis=0, keepdims=True)
        part = cs[:, 0:128]
        for k in range(1, D_MODEL // 128):
            part = part + cs[:, 128 * k:128 * (k + 1)]

        @pl.when(i == 0)
        def _():
            sq_ref[...] = jnp.zeros(sq_ref.shape, F32)

        sq_ref[0:1, :] += part

    row = pl.BlockSpec((tm, D_MODEL), lambda i: (i, 0))
    return _call(
        body, name="loss_grad", grid=(s // tm,),
        in_specs=[row, row, row, pl.BlockSpec((None, 1, D_MODEL), lambda i: (l, 0, 0))],
        out_specs=[row, pl.BlockSpec((8, 128), lambda i: (0, 0)), row, pl.BlockSpec((8, D_MODEL), lambda i: (0, 0))],
        out_shape=[jax.ShapeDtypeStruct((s, D_MODEL), F32), jax.ShapeDtypeStruct((8, 128), F32),
                   jax.ShapeDtypeStruct((s, D_MODEL), BF16), jax.ShapeDtypeStruct((8, D_MODEL), F32)],
        args=[xf, target, f, g3])


def _norm_bwd_rows(v, g, dy):
    r = _rstd(v)
    vn = v * r
    gd = dy * g
    dv = r * (gd - vn * jnp.mean(vn * gd, axis=-1, keepdims=True))
    return dv, dy * vn


def _zero_first(refs, first):
    @pl.when(first)
    def _():
        for ref in refs:
            ref[...] = jnp.zeros(ref.shape, F32)


def _add_cols(ref, val):
    ref[0:1, :] += jnp.sum(val, axis=0, keepdims=True)


def _accum_cols(ref, val, first):
    _zero_first((ref,), first)
    _add_cols(ref, val)


def _row_subtiles(rows, sub):
    sub = min(sub, rows)
    return [slice(r, r + sub) for r in range(0, rows, sub)]


def _ffn_out_bwd(df, wfo4, gu, s, tm, comm=None):
    def body(df_ref, w_ref, gu_ref, dgu_ref):
        da = _dot_nt(df_ref[...], w_ref[...])
        g = gu_ref[0].astype(F32)
        u = gu_ref[1].astype(F32)
        sg = 1.0 / (1.0 + jnp.exp(-g))
        dgu_ref[0] = (da * u * (sg * (1.0 + g * (1.0 - sg)))).astype(BF16)
        dgu_ref[1] = (da * (g * sg)).astype(BF16)

    blk = pl.BlockSpec((None, 2, tm, FF_SHARD), lambda b, i: (b, 0, i, 0))
    return _call(
        body, name="ffn_out_bwd", grid=(4, s // tm),
        in_specs=[pl.BlockSpec((tm, D_MODEL), lambda b, i: (i, 0)),
                  pl.BlockSpec((None, FF_SHARD, D_MODEL), lambda b, i: (b, 0, 0)), blk],
        out_specs=[blk], out_shape=[jax.ShapeDtypeStruct((4, 2, s, FF_SHARD), BF16)],
        args=[df, wfo4, gu], comm=comm)


def _dw_ffn_out(act, df, s):
    def body(a_ref, b_ref, o_ref):
        o_ref[...] = _dot_tn(a_ref[...], b_ref[...]).astype(BF16)

    out, = _call(
        body, name="dw_ffn_out", grid=(4,),
        in_specs=[pl.BlockSpec((None, s, FF_SHARD), lambda n: (n, 0, 0)),
                  pl.BlockSpec((s, D_MODEL), lambda n: (0, 0))],
        out_specs=[pl.BlockSpec((FF_SHARD, D_MODEL), lambda n: (n, 0))],
        out_shape=[jax.ShapeDtypeStruct((D_FF, D_MODEL), BF16)], args=[act, df])
    return out


def _dw_ffn_in(h2, dgu, s):
    def body(a_ref, b_ref, o_ref):
        o_ref[...] = _dot_tn(b_ref[...], a_ref[...]).astype(BF16)

    out, = _call(
        body, name="dw_ffn_in", grid=(N_DEV,),
        in_specs=[pl.BlockSpec((s, D_MODEL), lambda n: (0, 0)),
                  pl.BlockSpec((None, None, s, FF_SHARD), lambda n: (n % 4, n // 4, 0, 0))],
        out_specs=[pl.BlockSpec((None, FF_SHARD, D_MODEL), lambda n: (n, 0, 0))],
        out_shape=[jax.ShapeDtypeStruct((N_DEV, FF_SHARD, D_MODEL), BF16)], args=[h2, dgu])
    return out


def _ffn_in_bwd(dgu, wfin, xm, g_pre3, dres, z, g_post3, l, s, tm, comm=None):
    def body(d_ref, w_ref, xm_ref, gp_ref, dres_ref, z_ref, gq_ref, dxm_ref, dz_ref, dgp_ref, dgq_ref):
        _zero_first((dgp_ref, dgq_ref), pl.program_id(0) == 0)
        for rs in _row_subtiles(tm, SUB_ROWS):
            dh = _dot(d_ref[0, 0, rs, :], w_ref[0])
            for j in range(1, N_DEV):
                dh = dh + _dot(d_ref[j % 4, j // 4, rs, :], w_ref[j])
            dx, dyn = _norm_bwd_rows(xm_ref[rs, :], gp_ref[...], dh)
            dxm = dres_ref[rs, :] + dx
            dxm_ref[rs, :] = dxm
            _add_cols(dgp_ref, dyn)
            dz, dyn2 = _norm_bwd_rows(z_ref[rs, :], gq_ref[...], dxm)
            dz_ref[rs, :] = dz.astype(BF16)
            _add_cols(dgq_ref, dyn2)

    row = pl.BlockSpec((tm, D_MODEL), lambda i: (i, 0))
    gain = pl.BlockSpec((None, 1, D_MODEL), lambda i: (l, 0, 0))
    dgs = pl.BlockSpec((8, D_MODEL), lambda i: (0, 0))
    return _call(
        body, name="ffn_in_bwd", grid=(s // tm,),
        in_specs=[pl.BlockSpec((4, 2, tm, FF_SHARD), lambda i: (0, 0, i, 0)),
                  pl.BlockSpec((N_DEV, FF_SHARD, D_MODEL), lambda i: (0, 0, 0), pipeline_mode=pl.Buffered(1)),
                  row, gain, row, row, gain],
        out_specs=[row, row, dgs, dgs],
        out_shape=[jax.ShapeDtypeStruct((s, D_MODEL), F32), jax.ShapeDtypeStruct((s, D_MODEL), BF16),
                   jax.ShapeDtypeStruct((8, D_MODEL), F32), jax.ShapeDtypeStruct((8, D_MODEL), F32)],
        args=[dgu, wfin, xm, g_pre3, dres, z, g_post3], comm=comm)


def _dw_out(ync, yna, dz, s):
    half = D_MODEL // 2

    def body(a1_ref, a2_ref, b_ref, o_ref):
        b = b_ref[...]
        o_ref[0:half, :] = _dot_tn(a1_ref[...], b).astype(BF16)
        o_ref[half:D_MODEL, :] = _dot_tn(a2_ref[...], b).astype(BF16)

    out, = _call(
        body, name="dw_out", grid=(2,),
        in_specs=[pl.BlockSpec((s, half), lambda n: (0, 0)), pl.BlockSpec((s, half), lambda n: (0, 0)),
                  pl.BlockSpec((s, half), lambda n: (0, n))],
        out_specs=[pl.BlockSpec((D_MODEL, half), lambda n: (0, n))],
        out_shape=[jax.ShapeDtypeStruct((D_MODEL, D_MODEL), BF16)], args=[ync, yna, dz])
    return out


def _out_proj_bwd(dz, wout, o, g3, l, s, tm):
    def body(dz_ref, w_ref, o_ref, g_ref, dyc_ref, do_ref, dg_ref):
        gmat = _group_matrix()
        _zero_first((dg_ref,), pl.program_id(0) == 0)
        for rs in _row_subtiles(tm, SUB_ROWS):
            dy = _dot_nt(dz_ref[rs, :], w_ref[...])
            dyc_ref[rs, :] = dy[:, 0:CONV_WIDTH]
            for j in range(ATTN_WIDTH // 128):
                c0 = 128 * j
                ov = o_ref[rs, c0:c0 + 128]
                dyn = dy[:, CONV_WIDTH + c0:CONV_WIDTH + c0 + 128]
                r = lax.rsqrt(_group_mean(ov * ov, gmat) + EPS)
                on = ov * r
                gd = dyn * g_ref[:, c0:c0 + 128]
                do_ref[rs, c0:c0 + 128] = r * (gd - on * _group_mean(on * gd, gmat))
                dg_ref[0:1, c0:c0 + 128] += jnp.sum(dyn * on, axis=0, keepdims=True)

    halfrow = pl.BlockSpec((tm, ATTN_WIDTH), lambda i: (i, 0))
    return _call(
        body, name="out_proj_bwd", grid=(s // tm,),
        in_specs=[pl.BlockSpec((tm, D_MODEL), lambda i: (i, 0)),
                  pl.BlockSpec((D_MODEL, D_MODEL), lambda i: (0, 0)), halfrow,
                  pl.BlockSpec((None, 1, ATTN_WIDTH), lambda i: (l, 0, 0))],
        out_specs=[halfrow, halfrow, pl.BlockSpec((8, ATTN_WIDTH), lambda i: (0, 0))],
        out_shape=[jax.ShapeDtypeStruct((s, CONV_WIDTH), F32), jax.ShapeDtypeStruct((s, ATTN_WIDTH), F32),
                   jax.ShapeDtypeStruct((8, ATTN_WIDTH), F32)],
        args=[dz, wout, o, g3])


def _conv_bwd(pc, dyc, wc, g3, dq, dk, dv, l, s, tr):
    hb = tr // 8
    nt = s // tr
    ext = tr + 16
    last_hb = s // 8 - 1

    def body(pc_ref, prev_ref, next_ref, dy_ref, dyn_ref, wc_ref, g_ref, dq_ref, dk_ref, dv_ref,
             dpc_ref, dw_ref, dg_ref):
        i = pl.program_id(0)
        for part, ref in enumerate((dq_ref, dk_ref, dv_ref)):
            c = 3 * CONV_WIDTH + ATTN_WIDTH * part
            dpc_ref[:, c:c + ATTN_WIDTH] = ref[...]
        gmat = _group_matrix()
        row = lax.broadcasted_iota(jnp.int32, (ext, 128), 0) + (i * tr - 8)
        inside = jnp.where(row >= 0, jnp.where(row < s, 1, 0), 0) == 1

        @pl.when(i == 0)
        def _():
            dw_ref[...] = jnp.zeros(dw_ref.shape, F32)
            dg_ref[...] = jnp.zeros(dg_ref.shape, F32)

        def extend(ref_prev, ref_mid, ref_next, c):
            parts = [ref_prev[:, c:c + 128] if ref_prev is not None else jnp.zeros((8, 128), F32),
                     ref_mid[:, c:c + 128], ref_next[:, c:c + 128]]
            return jnp.concatenate(parts, axis=0)

        for j in range(CONV_WIDTH // 128):
            c0, c1, c2 = 128 * j, CONV_WIDTH + 128 * j, 2 * CONV_WIDTH + 128 * j
            hc = extend(prev_ref, pc_ref, next_ref, c0)
            bg = extend(prev_ref, pc_ref, next_ref, c1)
            cg = extend(prev_ref, pc_ref, next_ref, c2)
            dyn = extend(None, dy_ref, dyn_ref, c0)
            w0, w1, w2 = (wc_ref[0:1, c0:c0 + 128], wc_ref[1:2, c0:c0 + 128], wc_ref[2:3, c0:c0 + 128])
            gain = g_ref[:, c0:c0 + 128]
            u = jnp.where(inside, cg * hc, 0.0)
            u1 = pltpu.roll(u, 1, 0)
            u2 = pltpu.roll(u, 2, 0)
            out = u2 * w0 + u1 * w1 + u * w2
            yc = bg * out
            r = lax.rsqrt(_group_mean(yc * yc, gmat) + EPS)
            ycn = yc * r
            gd = dyn * gain
            dyc = r * (gd - ycn * _group_mean(ycn * gd, gmat))
            dout = jnp.where(inside, dyc * bg, 0.0)
            du = dout * w2 + pltpu.roll(dout, ext - 1, 0) * w1 + pltpu.roll(dout, ext - 2, 0) * w0
            sl = slice(8, 8 + tr)
            dpc_ref[:, c0:c0 + 128] = (du[sl] * cg[sl]).astype(BF16)
            dpc_ref[:, c1:c1 + 128] = (dyc[sl] * out[sl]).astype(BF16)
            dpc_ref[:, c2:c2 + 128] = (du[sl] * hc[sl]).astype(BF16)
            dw_ref[0:1, c0:c0 + 128] += jnp.sum(dout[sl] * u2[sl], axis=0, keepdims=True)
            dw_ref[1:2, c0:c0 + 128] += jnp.sum(dout[sl] * u1[sl], axis=0, keepdims=True)
            dw_ref[2:3, c0:c0 + 128] += jnp.sum(dout[sl] * u[sl], axis=0, keepdims=True)
            dg_ref[0:1, c0:c0 + 128] += jnp.sum(dyn[sl] * ycn[sl], axis=0, keepdims=True)

    wide = 3 * CONV_WIDTH
    return _call(
        body, name="conv_bwd", grid=(nt,),
        in_specs=[pl.BlockSpec((tr, wide), lambda i: (i, 0)),
                  pl.BlockSpec((8, wide), lambda i: (jnp.maximum(i * hb - 1, 0), 0)),
                  pl.BlockSpec((8, wide), lambda i: (jnp.minimum((i + 1) * hb, last_hb), 0)),
                  pl.BlockSpec((tr, CONV_WIDTH), lambda i: (i, 0)),
                  pl.BlockSpec((8, CONV_WIDTH), lambda i: (jnp.minimum((i + 1) * hb, last_hb), 0)),
                  pl.BlockSpec((None, 8, CONV_WIDTH), lambda i: (l, 0, 0)),
                  pl.BlockSpec((None, 1, CONV_WIDTH), lambda i: (l, 0, 0)),
                  pl.BlockSpec((tr, ATTN_WIDTH), lambda i: (i, 0)),
                  pl.BlockSpec((tr, ATTN_WIDTH), lambda i: (i, 0)),
                  pl.BlockSpec((tr, ATTN_WIDTH), lambda i: (i, 0))],
        out_specs=[pl.BlockSpec((tr, PROJ_WIDTH), lambda i: (i, 0)),
                   pl.BlockSpec((8, CONV_WIDTH), lambda i: (0, 0)),
                   pl.BlockSpec((8, CONV_WIDTH), lambda i: (0, 0))],
        out_shape=[jax.ShapeDtypeStruct((s, PROJ_WIDTH), BF16), jax.ShapeDtypeStruct((8, CONV_WIDTH), F32),
                   jax.ShapeDtypeStruct((8, CONV_WIDTH), F32)],
        args=[pc, pc, pc, dyc, dyc, wc, g3, dq, dk, dv])


def _attn_bwd(qkvp, biasm, o, lse, do, l, s, pad, comm=None):
    nb = s // Q_BLOCK
    qb0 = pad // Q_BLOCK
    scale = HEAD_DIM ** -0.5
    wide = 128 * ATTN_PAIRS

    def body(q_ref, k_ref, v_ref, b_ref, o_ref, lse_ref, do_ref,
             dq_ref, dk_ref, dv_ref, ds_ref, dk_acc, dv_acc):
        blk = pl.program_id(1)

        @pl.when(blk == 0)
        def _():
            dk_acc[...] = jnp.zeros(dk_acc.shape, F32)
            dv_acc[...] = jnp.zeros(dv_acc.shape, F32)
            ds_ref[...] = jnp.zeros(ds_ref.shape, F32)

        koff = pl.multiple_of(blk * Q_BLOCK + (pad - LEFT), Q_BLOCK)
        lane = lax.broadcasted_iota(jnp.int32, (1, 128), 1)
        kpos = lax.broadcasted_iota(jnp.int32, (1, K_BAND), 1) + (blk * Q_BLOCK - LEFT)
        kmask = jnp.where(kpos >= 0, 0.0, NEG_INF)
        for pr in range(ATTN_PAIRS):
            ls = slice(128 * pr, 128 * (pr + 1))
            q = q_ref[:, ls]
            kb = k_ref[pl.ds(koff, K_BAND), ls]
            vb = v_ref[pl.ds(koff, K_BAND), ls]
            dov = do_ref[:, ls]
            lse_v = lse_ref[:, ls]
            prod = dov * o_ref[:, ls]
            dq_parts = []
            dk_new = jnp.zeros((K_BAND, 128), F32)
            dv_new = jnp.zeros((K_BAND, 128), F32)
            for hh in range(2):
                in_head = (lane >> 6) == hh
                qm = jnp.where(in_head, q, jnp.zeros_like(q)) * jnp.asarray(scale, BF16)
                dom = jnp.where(in_head, dov, 0.0).astype(BF16)
                delta = jnp.sum(jnp.where(in_head, prod, 0.0), axis=1, keepdims=True)
                lse_h = lse_v[:, HEAD_DIM * hh:HEAD_DIM * hh + 1]
                sc = _dot_nt(qm, kb) + b_ref[2 * pr + hh] + kmask
                p = jnp.exp(sc - lse_h)
                dp = _dot_nt(dom, vb)
                ds = p * (dp - delta)
                ds_ref[2 * pr + hh] += ds
                dsb = ds.astype(BF16)
                dq_parts.append(_dot(dsb, kb) * scale)
                dk_new = dk_new + _dot_tn(dsb, qm)
                dv_new = dv_new + _dot_tn(p.astype(BF16), dom)
            dq_ref[:, ls] = jnp.where(lane < HEAD_DIM, dq_parts[0], dq_parts[1]).astype(BF16)
            dk_acc[pl.ds(koff, K_BAND), ls] += dk_new
            dv_acc[pl.ds(koff, K_BAND), ls] += dv_new

        @pl.when(blk == nb - 1)
        def _():
            dk_ref[...] = dk_acc[pad:pad + s, :].astype(BF16)
            dv_ref[...] = dv_acc[pad:pad + s, :].astype(BF16)

    n_grp = ATTN_WIDTH // wide
    qblk = pl.BlockSpec((Q_BLOCK, wide), lambda p, b: (b, p))
    col = pl.BlockSpec((s, wide), lambda p, b: (0, p))
    shp = jax.ShapeDtypeStruct((s, ATTN_WIDTH), BF16)
    return _call(
        body, name="attn_bwd", grid=(n_grp, nb),
        in_specs=[pl.BlockSpec((Q_BLOCK, wide), lambda p, b: (qb0 + b, p)),
                  pl.BlockSpec((s + pad, wide), lambda p, b: (0, n_grp + p)),
                  pl.BlockSpec((s + pad, wide), lambda p, b: (0, 2 * n_grp + p)),
                  pl.BlockSpec((None, 2 * ATTN_PAIRS, Q_BLOCK, K_BAND), lambda p, b: (l, p, 0, 0)),
                  qblk, qblk, qblk],
        out_specs=[qblk, col, col, pl.BlockSpec((2 * ATTN_PAIRS, Q_BLOCK, K_BAND), lambda p, b: (p, 0, 0))],
        out_shape=[shp, shp, shp, jax.ShapeDtypeStruct((N_HEADS, Q_BLOCK, K_BAND), F32)],
        scratch=[pltpu.VMEM((s + pad, wide), F32), pltpu.VMEM((s + pad, wide), F32)],
        args=[qkvp, qkvp, qkvp, biasm, o, lse, do], comm=comm)


def _dw_in(h, dproj, s):
    def body(a_ref, b_ref, o_ref):
        acc = _dot_tn(a_ref[...], b_ref[...])
        o_ref[0] = acc[:, 0:PROJ_SHARD].astype(BF16)
        o_ref[1] = acc[:, PROJ_SHARD:2 * PROJ_SHARD].astype(BF16)

    out, = _call(
        body, name="dw_in", grid=(4,),
        in_specs=[pl.BlockSpec((s, D_MODEL), lambda n: (0, 0)),
                  pl.BlockSpec((s, 2 * PROJ_SHARD), lambda n: (0, n))],
        out_specs=[pl.BlockSpec((2, D_MODEL, PROJ_SHARD), lambda n: (n, 0, 0))],
        out_shape=[jax.ShapeDtypeStruct((N_DEV, D_MODEL, PROJ_SHARD), BF16)], args=[h, dproj])
    return out


def _in_proj_bwd(dproj, win, x, g3, dres, l, s, tm, f_prev=None, g_post3=None, comm=None):
    chain = f_prev is not None

    def body(d_ref, w_ref, x_ref, g_ref, dres_ref, *rest):
        if chain:
            f_ref, gq_ref, dx_ref, dg_ref, df_ref, dgq_ref = rest
            _zero_first((dg_ref, dgq_ref), pl.program_id(0) == 0)
        else:
            dx_ref, dg_ref = rest
            _zero_first((dg_ref,), pl.program_id(0) == 0)
        for rs in _row_subtiles(tm, SUB_ROWS):
            dh = _dot_nt(d_ref[rs, 0:PROJ_SHARD], w_ref[0])
            for j in range(1, N_DEV):
                dh = dh + _dot_nt(d_ref[rs, PROJ_SHARD * j:PROJ_SHARD * (j + 1)], w_ref[j])
            dx, dyn = _norm_bwd_rows(x_ref[rs, :], g_ref[...], dh)
            dx = dres_ref[rs, :] + dx
            dx_ref[rs, :] = dx
            _add_cols(dg_ref, dyn)
            if chain:
                df, dyn2 = _norm_bwd_rows(f_ref[rs, :], gq_ref[...], dx)
                df_ref[rs, :] = df.astype(BF16)
                _add_cols(dgq_ref, dyn2)

    row = pl.BlockSpec((tm, D_MODEL), lambda i: (i, 0))
    dgs = pl.BlockSpec((8, D_MODEL), lambda i: (0, 0))
    in_specs = [pl.BlockSpec((tm, PROJ_WIDTH), lambda i: (i, 0)),
                pl.BlockSpec((N_DEV, D_MODEL, PROJ_SHARD), lambda i: (0, 0, 0)),
                row, pl.BlockSpec((None, 1, D_MODEL), lambda i: (l, 0, 0)), row]
    out_specs = [row, dgs]
    out_shape = [jax.ShapeDtypeStruct((s, D_MODEL), F32), jax.ShapeDtypeStruct((8, D_MODEL), F32)]
    args = [dproj, win, x, g3, dres]
    if chain:
        in_specs += [row, pl.BlockSpec((None, 1, D_MODEL), lambda i: (l - 1, 0, 0))]
        out_specs += [row, dgs]
        out_shape += [jax.ShapeDtypeStruct((s, D_MODEL), BF16), jax.ShapeDtypeStruct((8, D_MODEL), F32)]
        args += [f_prev, g_post3]
    return _call(body, name="in_proj_bwd", grid=(s // tm,), in_specs=in_specs, out_specs=out_specs,
                 out_shape=out_shape, args=args, comm=comm)


def _adamw(name, w, m, v, lands, owns=None, me=None):
    groups, rows, cols = w.shape
    assert len(lands) == groups
    n_part = lands[0].shape[0]
    tr = _row_tile(rows, tuple(c for c in (512, 352, 256, 176, 128, 64, 32, 16, 8) if c * cols <= 256 * 1024))
    c1 = 1.0 - ADAM_B1 ** ADAM_STEP
    c2 = 1.0 - ADAM_B2 ** ADAM_STEP
    n_own = groups if owns is not None else 0

    def body(*refs):
        if n_own:
            me_ref, refs = refs[0], refs[1:]
        w_ref, m_ref, v_ref = refs[:3]
        land_refs = refs[3:3 + groups]
        own_refs = refs[3 + groups:3 + groups + n_own]
        g_ref, d_ref, nm_ref, nv_ref = refs[3 + groups + n_own:]
        grp = pl.program_id(0)
        for gi in range(groups):
            @pl.when(grp == gi)
            def _():
                l_ref = land_refs[gi]
                g = None
                for p in range(n_part):
                    part = l_ref[p].astype(F32)
                    if n_own:
                        part = jnp.where(me_ref[0] == p, own_refs[gi][...].astype(F32), part)
                    g = part if g is None else g + part
                g_ref[...] = g
                m1 = ADAM_B1 * m_ref[...] + (1.0 - ADAM_B1) * g
                v1 = ADAM_B2 * v_ref[...] + (1.0 - ADAM_B2) * (g * g)
                nm_ref[...] = m1
                nv_ref[...] = v1
                d_ref[...] = -ADAM_LR * ((m1 / c1) / (jnp.sqrt(v1 / c2) + ADAM_EPS) + ADAM_WD * w_ref[...])

    blk = pl.BlockSpec((None, tr, cols), lambda g, i, *_: (g, i, 0))
    shp = jax.ShapeDtypeStruct((groups, rows, cols), F32)

    def land_spec(gi):
        return pl.BlockSpec((n_part, tr, cols), lambda g, i, *_: (0, jnp.where(g == gi, i, 0), 0))

    def own_spec(gi):
        if owns[gi].ndim == 3:
            return pl.BlockSpec((None, tr, cols), lambda g, i, me_ref: (me_ref[0], jnp.where(g == gi, i, 0), 0))
        return pl.BlockSpec((tr, cols), lambda g, i, me_ref: (jnp.where(g == gi, i, 0), 0))

    in_specs = [blk, blk, blk] + [land_spec(gi) for gi in range(groups)] + [own_spec(gi) for gi in range(n_own)]
    args = [w, m, v] + list(lands) + (list(owns) if n_own else [])
    if not n_own:
        return _call(body, name=name, grid=(groups, rows // tr), in_specs=in_specs,
                     out_specs=[blk, blk, blk, blk], out_shape=[shp, shp, shp, shp], args=args)
    return pl.pallas_call(
        body, name=name,
        grid_spec=pltpu.PrefetchScalarGridSpec(
            num_scalar_prefetch=1, grid=(groups, rows // tr), in_specs=in_specs, out_specs=[blk, blk, blk, blk]),
        out_shape=[shp, shp, shp, shp],
        compiler_params=pltpu.CompilerParams(dimension_semantics=("arbitrary", "arbitrary"),
                                             vmem_limit_bytes=VMEM_LIMIT),
    )(me, *args)


def _pack_small(rel, gco, gao, gpm, gqm, gpf, gqf):
    n_layers = rel.shape[0]
    relp = jnp.pad(rel, ((0, 0), (0, 0), (0, REL_PAD - rel.shape[2])))
    parts = [relp.reshape(n_layers * N_HEADS * REL_PAD // 128, 128)]
    parts += [a.reshape(-1, 128) for a in (gco, gao, gpm, gqm, gpf, gqf)]
    return jnp.concatenate(parts, axis=0)


def _unpack_small(p, n_layers):
    n_rel = n_layers * N_HEADS * REL_PAD // 128
    rel = p[:n_rel].reshape(n_layers, N_HEADS, REL_PAD)[:, :, :2 * REL_CLIP + 1]
    outs = [rel]
    r0 = n_rel
    for width in (CONV_WIDTH, ATTN_WIDTH, D_MODEL, D_MODEL, D_MODEL, D_MODEL):
        nr = n_layers * width // 128
        outs.append(p[r0:r0 + nr].reshape(n_layers, width))
        r0 += nr
    return outs


def kernel(x, w_in, w_conv, rel_bias, g_conv_out, g_attn_out, w_out, g_pre_mix, g_post_mix, g_pre_ffn, g_post_ffn, w_ffn_in, w_ffn_out, loss_target, m_w_in, m_w_conv, m_rel_bias, m_g_conv_out, m_g_attn_out, m_w_out, m_g_pre_mix, m_g_post_mix, m_g_pre_ffn, m_g_post_ffn, m_w_ffn_in, m_w_ffn_out, v_w_in, v_w_conv, v_rel_bias, v_g_conv_out, v_g_attn_out, v_w_out, v_g_pre_mix, v_g_post_mix, v_g_pre_ffn, v_g_post_ffn, v_w_ffn_in, v_w_ffn_out):
    n_layers = w_in.shape[0]
    s = x.shape[1]
    assert x.shape == (1, s, D_MODEL) and s % 1024 == 0
    assert w_in.shape == (n_layers, D_MODEL, PROJ_SHARD) and w_ffn_in.shape == (n_layers, D_MODEL, FF_SHARD)
    tm = 512
    tq = 1024 if s >= 2048 else 512
    tf = min(2048, s)
    x0 = x.reshape(s, D_MODEL)
    target = loss_target.reshape(s, D_MODEL)
    dev = _dev_index(lax.axis_index("x"), lax.axis_index("y"), lax.axis_index("c"))

    wt_ffn_in, mt_ffn_in, vt_ffn_in = (jnp.transpose(a, (0, 2, 1)) for a in (w_ffn_in, m_w_ffn_in, v_w_ffn_in))
    local_w = [_cast_bf16(w_in, "cast_w_in"), _cast_bf16(w_out, "cast_w_out"),
               _cast_bf16(wt_ffn_in, "cast_w_ffn_in"), _cast_bf16(w_ffn_out, "cast_w_ffn_out")]
    wc_local = jnp.pad(jnp.transpose(w_conv, (0, 2, 1)).reshape(-1), (0, 1024 - n_layers * 3 * 64)).reshape(8, 128)
    win_next, wc_g = _comm_only("gather_first", _Gather([(local_w[0], 0), (wc_local, None)]))
    weights = [None] * n_layers
    wc_full = wc_g.reshape(N_DEV, 1024)[:, :n_layers * 3 * 64].reshape(N_DEV, n_layers, 3, 64)
    wc_full = jnp.transpose(wc_full, (1, 2, 0, 3)).reshape(n_layers, 3, CONV_WIDTH)
    wc_full = jnp.pad(wc_full, ((0, 0), (0, 5), (0, 0)))

    g3 = {k: v.reshape(n_layers, 1, -1) for k, v in dict(
        conv=g_conv_out, attn=g_attn_out, pre_mix=g_pre_mix, post_mix=g_post_mix,
        pre_ffn=g_pre_ffn, post_ffn=g_post_ffn).items()}
    biasm = _bias_build(jnp.pad(rel_bias, ((0, 0), (0, 0), (0, REL_PAD - rel_bias.shape[2]))))

    saved = []
    xl = x0
    h = _norm_cast(x0, g3["pre_mix"], 0, tm)
    for l in range(n_layers):
        win = win_next
        pc, wout = _in_proj(h, win, s, tq, 0, comm=_Gather([(local_w[1], l)]))
        qkvp, = _in_proj(h, win, s, tq, 1)
        ync = _conv_fwd(pc, wc_full, g3["conv"], l, s, tm)
        o, lse, yna, wfin = _attn_fwd(qkvp, biasm, g3["attn"], l, s, tq, comm=_Gather([(local_w[2], l)]))
        wout = wout.reshape(D_MODEL, D_MODEL)
        z, xm, h2 = _out_proj_fwd(ync, yna, wout, xl, g3["post_mix"], g3["pre_ffn"], l, s, tq)
        gu, act, wfout = _ffn_in_fwd(h2, wfin, s, tf, comm=_Gather([(local_w[3], l)]))
        wfo4 = wfout.reshape(4, FF_SHARD, D_MODEL)
        l_next = min(l + 1, n_layers - 1)
        f, xo, h_next, *got = _ffn_out_fwd(act, wfo4, xm, g3["post_ffn"], g3["pre_mix"], l, l_next, s, tm,
                                           comm=_Gather([(local_w[0], l + 1)]) if l + 1 < n_layers else None)
        weights[l] = [win, wout, wfin, wfo4]
        win_next = got[0] if got else None
        saved.append(dict(x=xl, h=h, pc=pc, qkvp=qkvp, ync=ync, yna=yna, o=o, lse=lse, z=z, xm=xm,
                          h2=h2, gu=gu, act=act, f=f))
        xl, h = xo, h_next

    dx, sq, df, dg_post_ffn = _loss_grad(xl, target, saved[-1]["f"], g3["post_ffn"], n_layers - 1, s, tm)
    loss = lax.psum(jnp.sum(sq) * (0.5 / D_MODEL), ("x", "y", "c"))

    lands = dict(win=[None] * n_layers, wout=[None] * n_layers, wfin=[None] * n_layers, wfout=[None] * n_layers)
    small = {k: [None] * n_layers for k in ("gco", "gao", "gpm", "gqm", "gpf", "gqf", "wc")}
    d_rel = [None] * n_layers
    started = []

    def start(name, keys, l, arrays):
        items = [(a, False) for a in arrays]
        send_sems, recv_sems, srcs, zones, token = _exchange_start(name + "_start", items)
        started.append((name, keys, l, items, send_sems, recv_sems, srcs, zones))
        return token[0:1, 0:1].reshape(1, 1, 1)

    for l in reversed(range(n_layers)):
        sv = saved[l]
        win, wout, wfin, wfo4 = weights[l]
        small["gqf"][l] = dg_post_ffn[0]
        dgu, = _ffn_out_bwd(df, wfo4, sv["gu"], s, tf)
        d_wfout = _dw_ffn_out(sv["act"], df, s).reshape(N_DEV, FFO_SHARD, D_MODEL)
        d_wfin = _dw_ffn_in(sv["h2"], dgu, s)
        token = start(f"exchange_ffn{l}", ("wfout", "wfin"), l, [d_wfout, d_wfin])
        dxm, dz, dg_pre_ffn, dg_post_mix = _ffn_in_bwd(
            dgu, wfin, sv["xm"], g3["pre_ffn"] + token, dx, sv["z"], g3["post_mix"], l, s, tm)
        small["gpf"][l] = dg_pre_ffn[0]
        small["gqm"][l] = dg_post_mix[0]
        d_wout = _dw_out(sv["ync"], sv["yna"], dz, s).reshape(N_DEV, D_MODEL // N_DEV, D_MODEL)
        dyc, do, dg_attn = _out_proj_bwd(dz, wout, sv["o"], g3["attn"], l, s, tq)
        small["gao"][l] = dg_attn[0]
        dq, dk, dv, ds_sum = _attn_bwd(sv["qkvp"], biasm, sv["o"], sv["lse"], do, l, s, tq)
        d_rel[l] = _bias_bwd(ds_sum[None])
        dproj, dwc, dg_conv = _conv_bwd(sv["pc"], dyc, wc_full, g3["conv"], dq, dk, dv, l, s, tm)
        small["wc"][l] = dwc[0:3]
        small["gco"][l] = dg_conv[0]
        d_win = _dw_in(sv["h"], dproj, s)
        token = start(f"exchange_mix{l}", ("wout", "win"), l, [d_wout, d_win])
        if l > 0:
            dx, dg_pre_mix, df, dg_post_ffn = _in_proj_bwd(
                dproj, win, sv["x"], g3["pre_mix"] + token, dxm, l, s, tm, f_prev=saved[l - 1]["f"],
                g_post3=g3["post_ffn"])
        else:
            dx, dg_pre_mix = _in_proj_bwd(dproj, win, sv["x"], g3["pre_mix"] + token, dxm, l, s, tm)
        small["gpm"][l] = dg_pre_mix[0]
    grad_x = dx.reshape(1, s, D_MODEL)

    small_vec = jnp.concatenate(
        [_pack_small(jnp.concatenate(d_rel)[:, :, :2 * REL_CLIP + 1],
                     *[jnp.stack(small[k]) for k in ("gco", "gao", "gpm", "gqm", "gpf", "gqf")]),
         jnp.stack(small["wc"]).reshape(-1, 128)], axis=0)
    small_vec = jnp.pad(small_vec, ((0, (-small_vec.shape[0]) % 8), (0, 0)))
    small_items = [(small_vec, True)]
    small_sems = _exchange_start("exchange_small_start", small_items)

    owns = dict(win=[None] * n_layers, wout=[None] * n_layers, wfin=[None] * n_layers, wfout=[None] * n_layers)

    def wait(which, after):
        for name, keys, l, items, send_sems, recv_sems, srcs, zones in started:
            if keys == which:
                srcs, zones = _exchange_wait(name + "_wait", items, send_sems, recv_sems, srcs, zones, after)
                for key, src, zone in zip(keys, srcs, zones):
                    owns[key][l], lands[key][l] = src, zone

    me = dev.astype(jnp.int32).reshape(1)
    wait(("wfout", "wfin"), dx)
    r_fin = [jnp.transpose(t, (0, 2, 1)) for t in _adamw(
        "adamw_w_ffn_in", wt_ffn_in, mt_ffn_in, vt_ffn_in, lands["wfin"], owns["wfin"], me)]
    r_fout = _adamw("adamw_w_ffn_out", w_ffn_out, m_w_ffn_out, v_w_ffn_out, lands["wfout"], owns["wfout"], me)
    wait(("wout", "win"), r_fout[0])
    r_out = _adamw("adamw_w_out", w_out, m_w_out, v_w_out, lands["wout"], owns["wout"], me)
    r_in = _adamw("adamw_w_in", w_in, m_w_in, v_w_in, lands["win"], owns["win"], me)
    (small_own,), (land_small,) = _exchange_wait(
        "exchange_small_wait", small_items, small_sems[0], small_sems[1], small_sems[2], small_sems[3], r_in[0])

    n_rep = 64 * n_layers
    rep = _adamw(
        "adamw_replicated",
        _pack_small(rel_bias, g_conv_out, g_attn_out, g_pre_mix, g_post_mix, g_pre_ffn, g_post_ffn)[None],
        _pack_small(m_rel_bias, m_g_conv_out, m_g_attn_out, m_g_pre_mix, m_g_post_mix, m_g_pre_ffn, m_g_post_ffn)[None],
        _pack_small(v_rel_bias, v_g_conv_out, v_g_attn_out, v_g_pre_mix, v_g_post_mix, v_g_pre_ffn, v_g_post_ffn)[None],
        [land_small[:, :n_rep]], [small_own[:n_rep]], me)
    rep = [_unpack_small(t[0], n_layers) for t in rep]

    wc_rows = n_layers * 3 * CONV_WIDTH // 128
    zeros_wc = jnp.zeros((1, wc_rows, 128), F32)
    g_wc_full = _adamw("sum_w_conv", zeros_wc, zeros_wc, zeros_wc, [land_small[:, n_rep:n_rep + wc_rows]],
                       [small_own[n_rep:n_rep + wc_rows]], me)[0]
    g_wc_full = g_wc_full.reshape(n_layers, 3, CONV_WIDTH)
    g_wc = lax.dynamic_slice_in_dim(g_wc_full, dev * (CONV_WIDTH // N_DEV), CONV_WIDTH // N_DEV, axis=2)
    g_wc = jnp.transpose(g_wc, (0, 2, 1))

    def tiny(a):
        flat = a.reshape(-1)
        return jnp.pad(flat, (0, (-flat.shape[0]) % 1024)).reshape(1, -1, 128)

    r_wc = _adamw("adamw_w_conv", tiny(w_conv), tiny(m_w_conv), tiny(v_w_conv), [tiny(g_wc)])
    r_wc = [t.reshape(-1)[:w_conv.size].reshape(w_conv.shape) for t in r_wc]

    def leaf(kind):
        return [r_in[kind], r_wc[kind], rep[kind][0], rep[kind][1], rep[kind][2], r_out[kind],
                rep[kind][3], rep[kind][4], rep[kind][5], rep[kind][6], r_fin[kind], r_fout[kind]]

    return (loss, grad_x, *leaf(0), *leaf(1), *leaf(2), *leaf(3))
```

```python
import math

import jax
import jax.numpy as jnp
from jax import lax
from jax.experimental import pallas as pl
from jax.experimental.pallas import tpu as pltpu

F32 = jnp.float32
BF16 = jnp.bfloat16

D_MODEL = 1024
N_DEV = 8
CHUNK = 64
N_LEFT_CHUNKS = 8
CONV_WIDTH = 512
ATTN_WIDTH = 512
HEAD_DIM = 64
N_HEADS = 8
REL_CLIP = 128
REL_PAD = 384
PROJ_WIDTH = 3072
PROJ_SHARD = PROJ_WIDTH // N_DEV
D_FF = 2816
FF_SHARD = 2 * D_FF // N_DEV
FFO_SHARD = D_FF // N_DEV
FF_PAIR = 2 * FF_SHARD
EPS = 1e-6
NEG_INF = -1e30
ATTN_PAIRS = 2
Q_BLOCK = 4 * CHUNK
K_BAND = Q_BLOCK + N_LEFT_CHUNKS * CHUNK
LEFT = N_LEFT_CHUNKS * CHUNK
TOEP = 1024

ADAM_LR = 0.001
ADAM_B1 = 0.9
ADAM_B2 = 0.999
ADAM_EPS = 1e-08
ADAM_WD = 0.01
ADAM_STEP = 10

VMEM_LIMIT = 52 * 1024 * 1024
SUB_ROWS = 256
MESH = pl.DeviceIdType.MESH
ANY = pl.BlockSpec(memory_space=pl.ANY)

NT = (((1,), (1,)), ((), ()))
TN = (((0,), (0,)), ((), ()))


def _dot(a, b):
    return jnp.dot(a, b, preferred_element_type=F32)


def _dot_nt(a, b):
    return lax.dot_general(a, b, NT, preferred_element_type=F32)


def _dot_tn(a, b):
    return lax.dot_general(a, b, TN, preferred_element_type=F32)


def _rstd(v):
    return lax.rsqrt(jnp.mean(v * v, axis=-1, keepdims=True) + EPS)


def _group_matrix():
    r = lax.broadcasted_iota(jnp.int32, (128, 128), 0) >> 6
    c = lax.broadcasted_iota(jnp.int32, (128, 128), 1) >> 6
    return jnp.where(r == c, 1.0, 0.0).astype(BF16)


def _group_mean(v, gmat):
    hi = v.astype(BF16)
    lo = (v - hi.astype(F32)).astype(BF16)
    return (_dot(hi, gmat) + _dot(lo, gmat)) * (1.0 / HEAD_DIM)


def _split3(v):
    hi = v.astype(BF16)
    r1 = v - hi.astype(F32)
    mid = r1.astype(BF16)
    lo = (r1 - mid.astype(F32)).astype(BF16)
    return hi, mid, lo


def _row_tile(rows, cands=(1024, 512, 704, 256, 128, 64, 32, 16)):
    for c in cands:
        if rows % c == 0:
            return c
    return rows


def _dev_index(px, py, pc):
    return 4 * px + 2 * py + pc


def _when(cond):
    if cond is True:
        return lambda fn: fn()
    return pl.when(cond)


def _phases(grid):
    def phases():
        if not grid:
            return True, True, True
        lin = pl.program_id(0)
        for a in range(1, len(grid)):
            lin = lin * grid[a] + pl.program_id(a)
        total = math.prod(grid)
        return lin == 0, lin == (3 * total) // 4, lin == total - 1
    return phases


class _Gather:
    def __init__(self, items):
        self.items = items
        self.args = [a for a, _ in items]
        n = len(items)
        self.out_shape = [jax.ShapeDtypeStruct((N_DEV,) + (a.shape if lay is None else a.shape[1:]), a.dtype)
                          for a, lay in items]
        self.scratch = [pltpu.SemaphoreType.DMA((n, 7)), pltpu.SemaphoreType.DMA((n, 7)),
                        pltpu.SemaphoreType.DMA((n,))]

    def _ctx(self, ins, outs, sems):
        send_sems, recv_sems, local_sems = sems
        x, y, c = lax.axis_index("x"), lax.axis_index("y"), lax.axis_index("c")
        chips = [(1 - x, y), (x, 1 - y), (1 - x, 1 - y)]

        def src(k):
            lay = self.items[k][1]
            return ins[k] if lay is None else ins[k].at[lay]

        def copy(k, s, idx, to, from_src=False):
            return pltpu.make_async_remote_copy(
                src_ref=src(k) if from_src else outs[k].at[idx], dst_ref=outs[k].at[idx],
                send_sem=send_sems.at[k, s], recv_sem=recv_sems.at[k, s],
                device_id=to, device_id_type=MESH)

        def local(k):
            return pltpu.make_async_copy(src(k), outs[k].at[_dev_index(x, y, c)], local_sems.at[k])

        return x, y, c, chips, copy, local

    def start(self, ins, outs, sems, cond):
        n = len(self.items)

        @_when(cond)
        def _():
            x, y, c, chips, copy, local = self._ctx(ins, outs, sems)
            me = _dev_index(x, y, c)
            for k in range(n):
                local(k).start()
                copy(k, 0, me, (x, y, 1 - c), from_src=True).start()
                for j, chip in enumerate(chips):
                    copy(k, 1 + j, me, (chip[0], chip[1], c), from_src=True).start()

    def forward(self, ins, outs, sems, cond):
        n = len(self.items)

        @_when(cond)
        def _():
            x, y, c, chips, copy, local = self._ctx(ins, outs, sems)
            for j, chip in enumerate(chips):
                idx = _dev_index(chip[0], chip[1], c)
                for k in range(n):
                    copy(k, 1 + j, idx, (x, y, c)).wait_recv()
                    copy(k, 4 + j, idx, (x, y, 1 - c)).start()

    def finish(self, ins, outs, sems, cond):
        n = len(self.items)

        @_when(cond)
        def _():
            x, y, c, chips, copy, local = self._ctx(ins, outs, sems)
            me = _dev_index(x, y, c)
            for k in range(n):
                copy(k, 0, _dev_index(x, y, 1 - c), (x, y, c)).wait_recv()
            for j, chip in enumerate(chips):
                idx = _dev_index(chip[0], chip[1], 1 - c)
                for k in range(n):
                    copy(k, 4 + j, idx, (x, y, c)).wait_recv()
            for k in range(n):
                for s in range(4):
                    copy(k, s, me, (x, y, c), from_src=True).wait_send()
                for j, chip in enumerate(chips):
                    copy(k, 4 + j, _dev_index(chip[0], chip[1], c), (x, y, c)).wait_send()
                local(k).wait()


_PEER_FLIPS = [(0, 0, 1), (1, 0, 0), (0, 1, 0), (1, 1, 0), (1, 0, 1), (0, 1, 1), (1, 1, 1)]


def _call(body, *, name, grid, in_specs, out_specs, out_shape, args, scratch=(), comm=None):
    n_hi, n_ho, n_hs = len(args), len(out_shape), len(scratch)
    c_args = list(comm.args) if comm else []
    c_out = list(comm.out_shape) if comm else []
    c_scr = list(comm.scratch) if comm else []
    phases = _phases(grid)

    def kern(*refs):
        cuts = [n_hi, len(c_args), n_ho, len(c_out), n_hs, len(c_scr)]
        parts, pos = [], 0
        for n in cuts:
            parts.append(refs[pos:pos + n])
            pos += n
        hi, ci, ho, co, hs, cs = parts
        if comm:
            first, mid, last = phases()
            comm.start(ci, co, cs, first)
            comm.forward(ci, co, cs, mid)
        body(*hi, *ho, *hs)
        if comm:
            comm.finish(ci, co, cs, last)

    sem = ("arbitrary",) * len(grid) if grid else None
    return pl.pallas_call(
        kern, name=name, grid=grid,
        in_specs=list(in_specs) + [ANY] * len(c_args),
        out_specs=list(out_specs) + [ANY] * len(c_out),
        out_shape=list(out_shape) + c_out,
        scratch_shapes=list(scratch) + c_scr,
        compiler_params=pltpu.CompilerParams(dimension_semantics=sem, vmem_limit_bytes=VMEM_LIMIT),
    )(*args, *c_args)


def _comm_only(name, comm):
    return _call(lambda: None, name=name, grid=(), in_specs=[], out_specs=[], out_shape=[], args=[], comm=comm)


HBM_SPEC = pl.BlockSpec(memory_space=pltpu.HBM)
SEM_SPEC = pl.BlockSpec(memory_space=pltpu.SEMAPHORE)
SIDE_EFFECT = pltpu.SideEffectType.DATAFLOW_SIDE_EFFECTING


def _exchange_peer(x, y, c, s):
    fx, fy, fc = _PEER_FLIPS[s]
    return x ^ fx, y ^ fy, c ^ fc


def _exchange_start(name, items):
    n = len(items)
    srcs = [pltpu.with_memory_space_constraint(a, pltpu.HBM) for a, _ in items]
    land_shapes = [(N_DEV,) + (a.shape if whole else a.shape[1:]) for a, whole in items]
    lands = [pltpu.with_memory_space_constraint(lax.empty(shp, a.dtype), pltpu.HBM)
             for shp, (a, _) in zip(land_shapes, items)]

    n_sem = 7 * n

    def body(*refs):
        src_refs, land_refs = refs[:n], refs[n:2 * n]
        send_sems = refs[2 * n:2 * n + n_sem]
        recv_sems = refs[2 * n + n_sem:2 * n + 2 * n_sem]
        token = refs[-1]
        x, y, c = lax.axis_index("x"), lax.axis_index("y"), lax.axis_index("c")
        me = _dev_index(x, y, c)
        for s in range(7):
            px, py, pc = _exchange_peer(x, y, c, s)
            for k in range(n):
                src = src_refs[k] if items[k][1] else src_refs[k].at[_dev_index(px, py, pc)]
                pltpu.make_async_remote_copy(
                    src_ref=src, dst_ref=land_refs[k].at[me],
                    send_sem=send_sems[7 * k + s], recv_sem=recv_sems[7 * k + s],
                    device_id=(px, py, pc), device_id_type=MESH).start()
        token[...] = jnp.zeros(token.shape, token.dtype)

    outs = pl.pallas_call(
        body, name=name,
        out_shape=(*[pltpu.SemaphoreType.DMA(())] * (2 * n_sem),
                   *[pltpu.HBM(a.shape, a.dtype) for a in srcs],
                   *[pltpu.HBM(shp, a.dtype) for shp, a in zip(land_shapes, srcs)],
                   jax.ShapeDtypeStruct((8, 128), F32)),
        in_specs=[HBM_SPEC] * (2 * n),
        out_specs=(*[SEM_SPEC] * (2 * n_sem), *[HBM_SPEC] * (2 * n), pl.BlockSpec(memory_space=pltpu.VMEM)),
        input_output_aliases={i: 2 * n_sem + i for i in range(2 * n)},
        compiler_params=pltpu.CompilerParams(has_side_effects=SIDE_EFFECT),
    )(*srcs, *lands)
    base = 2 * n_sem
    return (list(outs[:n_sem]), list(outs[n_sem:base]), list(outs[base:base + n]),
            list(outs[base + n:base + 2 * n]), outs[-1])


def _exchange_wait(name, items, send_sems, recv_sems, srcs, lands, after):
    n = len(items)

    n_sem = 7 * n

    def body(*refs):
        src_refs, land_refs = refs[:n], refs[n:2 * n]
        send_refs = refs[2 * n:2 * n + n_sem]
        recv_refs = refs[2 * n + n_sem:2 * n + 2 * n_sem]
        x, y, c = lax.axis_index("x"), lax.axis_index("y"), lax.axis_index("c")
        for s in range(7):
            for k in range(n):
                copy = pltpu.make_async_remote_copy(
                    src_ref=src_refs[k] if items[k][1] else src_refs[k].at[0], dst_ref=land_refs[k].at[0],
                    send_sem=send_refs[7 * k + s], recv_sem=recv_refs[7 * k + s],
                    device_id=(x, y, c), device_id_type=MESH)
                copy.wait_send()
                copy.wait_recv()

    outs = pl.pallas_call(
        body, name=name,
        out_shape=(*[pltpu.HBM(a.shape, a.dtype) for a in srcs], *[pltpu.HBM(a.shape, a.dtype) for a in lands]),
        in_specs=[HBM_SPEC] * (2 * n) + [SEM_SPEC] * (2 * n_sem) + [ANY],
        out_specs=tuple([HBM_SPEC] * (2 * n)),
        input_output_aliases={i: i for i in range(2 * n)},
        compiler_params=pltpu.CompilerParams(has_side_effects=SIDE_EFFECT),
    )(*srcs, *lands, *send_sems, *recv_sems, after)
    return list(outs[:n]), list(outs[n:])


def _cast_bf16(x, name):
    shape = x.shape
    x2 = x.reshape(-1, shape[-1])
    rows, cols = x2.shape
    tr = _row_tile(rows)

    def body(x_ref, o_ref):
        o_ref[...] = x_ref[...].astype(BF16)

    blk = pl.BlockSpec((tr, cols), lambda i: (i, 0))
    out, = _call(body, name=name, grid=(rows // tr,), in_specs=[blk], out_specs=[blk],
                 out_shape=[jax.ShapeDtypeStruct((rows, cols), BF16)], args=[x2])
    return out.reshape(shape)


def _norm_cast(x, g3, l, tm):
    s = x.shape[0]

    def body(x_ref, g_ref, o_ref):
        v = x_ref[...]
        o_ref[...] = (v * _rstd(v) * g_ref[...]).astype(BF16)

    row = pl.BlockSpec((tm, D_MODEL), lambda i: (i, 0))
    out, = _call(body, name="norm_cast", grid=(s // tm,),
                 in_specs=[row, pl.BlockSpec((None, 1, D_MODEL), lambda i: (l, 0, 0))], out_specs=[row],
                 out_shape=[jax.ShapeDtypeStruct((s, D_MODEL), BF16)], args=[x, g3])
    return out


def _in_proj(h, win, s, tq, part, comm=None):
    pad = part
    dtype = BF16 if part else F32

    def body(a_ref, b_ref, o_ref):
        def compute():
            a = a_ref[...]
            for j in range(4):
                o_ref[:, PROJ_SHARD * j:PROJ_SHARD * (j + 1)] = _dot(a, b_ref[j]).astype(dtype)

        if pad:
            i = pl.program_id(0)

            @pl.when(i == 0)
            def _():
                o_ref[...] = jnp.zeros(o_ref.shape, dtype)

            pl.when(i > 0)(compute)
        else:
            compute()

    return _call(
        body, name="in_proj_qkv" if part else "in_proj_conv", grid=(s // tq + pad,),
        in_specs=[pl.BlockSpec((tq, D_MODEL), lambda i: (jnp.maximum(i - pad, 0), 0)),
                  pl.BlockSpec((4, D_MODEL, PROJ_SHARD), lambda i: (part, 0, 0))],
        out_specs=[pl.BlockSpec((tq, 4 * PROJ_SHARD), lambda i: (i, 0))],
        out_shape=[jax.ShapeDtypeStruct((s + pad * tq, PROJ_WIDTH // 2), dtype)], args=[h, win], comm=comm)


def _conv_fwd(pc, wc, g3, l, s, tr):
    hb = tr // 8

    def body(pc_ref, prev_ref, wc_ref, g_ref, o_ref):
        i = pl.program_id(0)
        gmat = _group_matrix()
        for j in range(CONV_WIDTH // 128):
            c0, c1, c2 = 128 * j, CONV_WIDTH + 128 * j, 2 * CONV_WIDTH + 128 * j
            hc = pc_ref[:, c0:c0 + 128]
            bg = pc_ref[:, c1:c1 + 128]
            cg = pc_ref[:, c2:c2 + 128]
            u_prev = jnp.where(i > 0, prev_ref[:, c2:c2 + 128] * prev_ref[:, c0:c0 + 128], 0.0)
            u = cg * hc
            full = jnp.concatenate([u_prev, u], axis=0)
            u1 = pltpu.roll(full, 1, 0)[8:]
            u2 = pltpu.roll(full, 2, 0)[8:]
            out = (u2 * wc_ref[0:1, c0:c0 + 128] + u1 * wc_ref[1:2, c0:c0 + 128]
                   + u * wc_ref[2:3, c0:c0 + 128])
            yc = bg * out
            r = lax.rsqrt(_group_mean(yc * yc, gmat) + EPS)
            o_ref[:, c0:c0 + 128] = (yc * r * g_ref[:, c0:c0 + 128]).astype(BF16)

    out, = _call(
        body, name="conv_fwd", grid=(s // tr,),
        in_specs=[pl.BlockSpec((tr, 3 * CONV_WIDTH), lambda i: (i, 0)),
                  pl.BlockSpec((8, 3 * CONV_WIDTH), lambda i: (jnp.maximum(i * hb - 1, 0), 0)),
                  pl.BlockSpec((None, 8, CONV_WIDTH), lambda i: (l, 0, 0)),
                  pl.BlockSpec((None, 1, CONV_WIDTH), lambda i: (l, 0, 0))],
        out_specs=[pl.BlockSpec((tr, CONV_WIDTH), lambda i: (i, 0))],
        out_shape=[jax.ShapeDtypeStruct((s, CONV_WIDTH), BF16)], args=[pc, pc, wc, g3])
    return out


def _toeplitz_source():
    r_i = lax.broadcasted_iota(jnp.int32, (REL_PAD, TOEP), 0)
    m_i = lax.broadcasted_iota(jnp.int32, (REL_PAD, TOEP), 1)
    idx = jnp.clip((K_BAND - 1) - m_i, -REL_CLIP, REL_CLIP) + REL_CLIP
    return jnp.where(r_i == idx, 1.0, 0.0).astype(BF16)


def _bias_build(rbp):
    n_layers = rbp.shape[0]

    def body(rb_ref, o_ref, t_ref):
        pmat = _toeplitz_source()
        hi, mid, lo = _split3(rb_ref[...])
        t_ref[...] = _dot(hi, pmat) + _dot(mid, pmat) + _dot(lo, pmat)
        shift = (CHUNK - 1) - lax.broadcasted_iota(jnp.int32, (CHUNK, TOEP), 0)
        kchunk = lax.broadcasted_iota(jnp.int32, (CHUNK, K_BAND), 1) >> 6
        for h in range(N_HEADS):
            b = jnp.broadcast_to(t_ref[pl.ds(h, 1), :], (CHUNK, TOEP))
            for bit in range(6):
                rolled = pltpu.roll(b, TOEP - (1 << bit), 1)
                b = jnp.where(((shift >> bit) & 1) == 1, rolled, b)
            for cq in range(Q_BLOCK // CHUNK):
                off = CHUNK * (Q_BLOCK // CHUNK - 1 - cq)
                band = pltpu.roll(b, TOEP - off, 1) if off else b
                dchunk = kchunk - cq
                in_band = jnp.where(dchunk >= 0, jnp.where(dchunk <= N_LEFT_CHUNKS, 1, 0), 0) == 1
                o_ref[h, CHUNK * cq:CHUNK * (cq + 1), :] = jnp.where(in_band, band[:, :K_BAND], NEG_INF)

    out, = _call(
        body, name="bias_build", grid=(n_layers,),
        in_specs=[pl.BlockSpec((None, N_HEADS, REL_PAD), lambda l: (l, 0, 0))],
        out_specs=[pl.BlockSpec((None, N_HEADS, Q_BLOCK, K_BAND), lambda l: (l, 0, 0, 0))],
        out_shape=[jax.ShapeDtypeStruct((n_layers, N_HEADS, Q_BLOCK, K_BAND), F32)],
        scratch=[pltpu.VMEM((N_HEADS, TOEP), F32)], args=[rbp])
    return out


def _bias_bwd(ds_sum):
    n_layers = ds_sum.shape[0]

    def body(ds_ref, o_ref, t_ref):
        pmat = _toeplitz_source()
        shift = (CHUNK - 1) - lax.broadcasted_iota(jnp.int32, (CHUNK, TOEP), 0)
        for h in range(N_HEADS):
            d = None
            for cq in range(Q_BLOCK // CHUNK):
                off = CHUNK * (Q_BLOCK // CHUNK - 1 - cq)
                part = jnp.concatenate([ds_ref[h, CHUNK * cq:CHUNK * (cq + 1), :],
                                        jnp.zeros((CHUNK, TOEP - K_BAND), F32)], axis=1)
                part = pltpu.roll(part, off, 1) if off else part
                d = part if d is None else d + part
            for bit in range(6):
                rolled = pltpu.roll(d, 1 << bit, 1)
                d = jnp.where(((shift >> bit) & 1) == 1, rolled, d)
            t_ref[pl.ds(h, 1), :] = jnp.sum(d, axis=0, keepdims=True)
        hi, mid, lo = _split3(t_ref[...])
        o_ref[...] = _dot_nt(hi, pmat) + _dot_nt(mid, pmat) + _dot_nt(lo, pmat)

    out, = _call(
        body, name="bias_bwd", grid=(n_layers,),
        in_specs=[pl.BlockSpec((None, N_HEADS, Q_BLOCK, K_BAND), lambda l: (l, 0, 0, 0))],
        out_specs=[pl.BlockSpec((None, N_HEADS, REL_PAD), lambda l: (l, 0, 0))],
        out_shape=[jax.ShapeDtypeStruct((n_layers, N_HEADS, REL_PAD), F32)],
        scratch=[pltpu.VMEM((N_HEADS, TOEP), F32)], args=[ds_sum])
    return out


def _attn_fwd(qkvp, biasm, g3, l, s, pad, comm=None):
    nb = s // Q_BLOCK
    qb0 = pad // Q_BLOCK
    scale = HEAD_DIM ** -0.5
    wide = 128 * ATTN_PAIRS

    def body(q_ref, k_ref, v_ref, b_ref, g_ref, o_ref, lse_ref, yn_ref):
        blk = pl.program_id(1)
        koff = pl.multiple_of(blk * Q_BLOCK + (pad - LEFT), Q_BLOCK)
        lane = lax.broadcasted_iota(jnp.int32, (1, 128), 1)
        kpos = lax.broadcasted_iota(jnp.int32, (1, K_BAND), 1) + (blk * Q_BLOCK - LEFT)
        kmask = jnp.where(kpos >= 0, 0.0, NEG_INF)
        gmat = _group_matrix()
        for pr in range(ATTN_PAIRS):
            ls = slice(128 * pr, 128 * (pr + 1))
            q = q_ref[:, ls]
            kb = k_ref[pl.ds(koff, K_BAND), ls]
            vb = v_ref[pl.ds(koff, K_BAND), ls]
            outs, lses = [], []
            for hh in range(2):
                in_head = (lane >> 6) == hh
                qm = jnp.where(in_head, q, jnp.zeros_like(q)) * jnp.asarray(scale, BF16)
                sc = _dot_nt(qm, kb) + b_ref[2 * pr + hh] + kmask
                m = jnp.max(sc, axis=1, keepdims=True)
                e = jnp.exp(sc - m)
                den = jnp.sum(e, axis=1, keepdims=True)
                outs.append(_dot(e.astype(BF16), vb) * (1.0 / den))
                lses.append(m + jnp.log(den))
            first = lane < HEAD_DIM
            o = jnp.where(first, outs[0], outs[1])
            o_ref[:, ls] = o
            lse_ref[:, ls] = jnp.where(first, lses[0], lses[1])
            r = lax.rsqrt(_group_mean(o * o, gmat) + EPS)
            yn_ref[:, ls] = (o * r * g_ref[:, ls]).astype(BF16)

    blk_out = pl.BlockSpec((Q_BLOCK, wide), lambda p, b: (b, p))
    n_grp = ATTN_WIDTH // wide
    return _call(
        body, name="attn_fwd", grid=(n_grp, nb),
        in_specs=[pl.BlockSpec((Q_BLOCK, wide), lambda p, b: (qb0 + b, p)),
                  pl.BlockSpec((s + pad, wide), lambda p, b: (0, n_grp + p)),
                  pl.BlockSpec((s + pad, wide), lambda p, b: (0, 2 * n_grp + p)),
                  pl.BlockSpec((None, 2 * ATTN_PAIRS, Q_BLOCK, K_BAND), lambda p, b: (l, p, 0, 0)),
                  pl.BlockSpec((None, 1, wide), lambda p, b: (l, 0, p))],
        out_specs=[blk_out, blk_out, blk_out],
        out_shape=[jax.ShapeDtypeStruct((s, ATTN_WIDTH), F32),
                   jax.ShapeDtypeStruct((s, ATTN_WIDTH), F32),
                   jax.ShapeDtypeStruct((s, ATTN_WIDTH), BF16)],
        args=[qkvp, qkvp, qkvp, biasm, g3], comm=comm)


def _out_proj_fwd(ync, yna, wout, x, g_post3, g_next3, l, s, tm):
    half = D_MODEL // 2

    def body(a1_ref, a2_ref, w_ref, x_ref, gp_ref, gn_ref, z_ref, xm_ref, h_ref):
        for rs in _row_subtiles(tm, SUB_ROWS):
            z = _dot(a1_ref[rs, :], w_ref[0:half, :]) + _dot(a2_ref[rs, :], w_ref[half:D_MODEL, :])
            z_ref[rs, :] = z
            xm = x_ref[rs, :] + z * _rstd(z) * gp_ref[...]
            xm_ref[rs, :] = xm
            h_ref[rs, :] = (xm * _rstd(xm) * gn_ref[...]).astype(BF16)

    row = pl.BlockSpec((tm, D_MODEL), lambda i: (i, 0))
    gain = pl.BlockSpec((None, 1, D_MODEL), lambda i: (l, 0, 0))
    return _call(
        body, name="out_proj_fwd", grid=(s // tm,),
        in_specs=[pl.BlockSpec((tm, half), lambda i: (i, 0)), pl.BlockSpec((tm, half), lambda i: (i, 0)),
                  pl.BlockSpec((D_MODEL, D_MODEL), lambda i: (0, 0)), row, gain, gain],
        out_specs=[row, row, row],
        out_shape=[jax.ShapeDtypeStruct((s, D_MODEL), F32), jax.ShapeDtypeStruct((s, D_MODEL), F32),
                   jax.ShapeDtypeStruct((s, D_MODEL), BF16)],
        args=[ync, yna, wout, x, g_post3, g_next3])


def _ffn_in_fwd(h2, wfin4, s, tm, comm=None):
    def body(h_ref, wg_ref, wu_ref, gu_ref, act_ref):
        h = h_ref[...]
        gate = _dot_nt(h, wg_ref[...])
        up = _dot_nt(h, wu_ref[...])
        gu_ref[0] = gate.astype(BF16)
        gu_ref[1] = up.astype(BF16)
        act_ref[...] = (gate * (1.0 / (1.0 + jnp.exp(-gate))) * up).astype(BF16)

    return _call(
        body, name="ffn_in_fwd", grid=(2, s // tm),
        in_specs=[pl.BlockSpec((tm, D_MODEL), lambda b, i: (i, 0)),
                  pl.BlockSpec((None, FF_PAIR, D_MODEL), lambda b, i: (b, 0, 0)),
                  pl.BlockSpec((None, FF_PAIR, D_MODEL), lambda b, i: (2 + b, 0, 0))],
        out_specs=[pl.BlockSpec((2, tm, FF_PAIR), lambda b, i: (0, i, b)),
                   pl.BlockSpec((tm, FF_PAIR), lambda b, i: (i, b))],
        out_shape=[jax.ShapeDtypeStruct((2, s, D_FF), BF16), jax.ShapeDtypeStruct((s, D_FF), BF16)],
        args=[h2, wfin4, wfin4], comm=comm)


def _ffn_out_fwd(act, wfo, xm, g_post3, g_next3, l, l_next, s, tm, comm=None):
    def body(a_ref, w_ref, x_ref, gp_ref, gn_ref, f_ref, xo_ref, h_ref):
        for rs in _row_subtiles(tm, SUB_ROWS):
            f = _dot(a_ref[rs, :], w_ref[...])
            f_ref[rs, :] = f
            xo = x_ref[rs, :] + f * _rstd(f) * gp_ref[...]
            xo_ref[rs, :] = xo
            h_ref[rs, :] = (xo * _rstd(xo) * gn_ref[...]).astype(BF16)

    row = pl.BlockSpec((tm, D_MODEL), lambda i: (i, 0))
    return _call(
        body, name="ffn_out_fwd", grid=(s // tm,),
        in_specs=[pl.BlockSpec((tm, D_FF), lambda i: (i, 0)),
                  pl.BlockSpec((D_FF, D_MODEL), lambda i: (0, 0), pipeline_mode=pl.Buffered(1)), row,
                  pl.BlockSpec((None, 1, D_MODEL), lambda i: (l, 0, 0)),
                  pl.BlockSpec((None, 1, D_MODEL), lambda i: (l_next, 0, 0))],
        out_specs=[row, row, row],
        out_shape=[jax.ShapeDtypeStruct((s, D_MODEL), F32), jax.ShapeDtypeStruct((s, D_MODEL), F32),
                   jax.ShapeDtypeStruct((s, D_MODEL), BF16)],
        args=[act, wfo, xm, g_post3, g_next3], comm=comm)


def _loss_grad(xf, target, f, g3, l, s, tm):
    def body(x_ref, t_ref, f_ref, g_ref, dx_ref, sq_ref, df_ref, dg_ref):
        i = pl.program_id(0)
        err = x_ref[...] - t_ref[...]
        dx = err * (1.0 / D_MODEL)
        dx_ref[...] = dx
        df, dyn = _norm_bwd_rows(f_ref[...], g_ref[...], dx)
        df_ref[...] = df.astype(BF16)
        _accum_cols(dg_ref, dyn, i == 0)
        cs = jnp.sum(err * err, axis=0, keepdims=True)
        part = cs[:, 0:128]
        for k in range(1, D_MODEL // 128):
            part = part + cs[:, 128 * k:128 * (k + 1)]

        @pl.when(i == 0)
        def _():
            sq_ref[...] = jnp.zeros(sq_ref.shape, F32)

        sq_ref[0:1, :] += part

    row = pl.BlockSpec((tm, D_MODEL), lambda i: (i, 0))
    return _call(
        body, name="loss_grad", grid=(s // tm,),
        in_specs=[row, row, row, pl.BlockSpec((None, 1, D_MODEL), lambda i: (l, 0, 0))],
        out_specs=[row, pl.BlockSpec((8, 128), lambda i: (0, 0)), row, pl.BlockSpec((8, D_MODEL), lambda i: (0, 0))],
        out_shape=[jax.ShapeDtypeStruct((s, D_MODEL), F32), jax.ShapeDtypeStruct((8, 128), F32),
                   jax.ShapeDtypeStruct((s, D_MODEL), BF16), jax.ShapeDtypeStruct((8, D_MODEL), F32)],
        args=[xf, target, f, g3])


def _norm_bwd_rows(v, g, dy):
    r = _rstd(v)
    vn = v * r
    gd = dy * g
    dv = r * (gd - vn * jnp.mean(vn * gd, axis=-1, keepdims=True))
    return dv, dy * vn


def _zero_first(refs, first):
    @pl.when(first)
    def _():
        for ref in refs:
            ref[...] = jnp.zeros(ref.shape, F32)


def _add_cols(ref, val):
    ref[0:1, :] += jnp.sum(val, axis=0, keepdims=True)


def _accum_cols(ref, val, first):
    _zero_first((ref,), first)
    _add_cols(ref, val)


def _row_subtiles(rows, sub):
    sub = min(sub, rows)
    return [slice(r, r + sub) for r in range(0, rows, sub)]


def _ffn_out_bwd(df, wfo, gu, s, tm, comm=None):
    def body(df_ref, w_ref, gu_ref, dgu_ref):
        da = _dot_nt(df_ref[...], w_ref[...])
        g = gu_ref[0].astype(F32)
        u = gu_ref[1].astype(F32)
        sg = 1.0 / (1.0 + jnp.exp(-g))
        dgu_ref[0] = (da * u * (sg * (1.0 + g * (1.0 - sg)))).astype(BF16)
        dgu_ref[1] = (da * (g * sg)).astype(BF16)

    blk = pl.BlockSpec((2, tm, FF_PAIR), lambda b, i: (0, i, b))
    return _call(
        body, name="ffn_out_bwd", grid=(2, s // tm),
        in_specs=[pl.BlockSpec((tm, D_MODEL), lambda b, i: (i, 0)),
                  pl.BlockSpec((FF_PAIR, D_MODEL), lambda b, i: (b, 0)), blk],
        out_specs=[blk], out_shape=[jax.ShapeDtypeStruct((2, s, D_FF), BF16)],
        args=[df, wfo, gu], comm=comm)


def _dw_ffn_out(act, df, s):
    def body(a_ref, b_ref, o_ref):
        o_ref[...] = _dot_tn(a_ref[...], b_ref[...]).astype(BF16)

    out, = _call(
        body, name="dw_ffn_out", grid=(2,),
        in_specs=[pl.BlockSpec((s, FF_PAIR), lambda n: (0, n)),
                  pl.BlockSpec((s, D_MODEL), lambda n: (0, 0), pipeline_mode=pl.Buffered(1))],
        out_specs=[pl.BlockSpec((FF_PAIR, D_MODEL), lambda n: (n, 0))],
        out_shape=[jax.ShapeDtypeStruct((D_FF, D_MODEL), BF16)], args=[act, df])
    return out


def _dw_ffn_in(h2, dgu, s):
    def body(a_ref, b_ref, o_ref):
        o_ref[...] = _dot_tn(b_ref[...], a_ref[...]).astype(BF16)

    out, = _call(
        body, name="dw_ffn_in", grid=(4,),
        in_specs=[pl.BlockSpec((s, D_MODEL), lambda n: (0, 0), pipeline_mode=pl.Buffered(1)),
                  pl.BlockSpec((None, s, FF_PAIR), lambda n: (n // 2, 0, n % 2))],
        out_specs=[pl.BlockSpec((None, FF_PAIR, D_MODEL), lambda n: (n, 0, 0))],
        out_shape=[jax.ShapeDtypeStruct((4, FF_PAIR, D_MODEL), BF16)], args=[h2, dgu])
    return out


def _ffn_in_bwd(dgu, wfin, xm, g_pre3, dres, z, g_post3, l, s, tm, comm=None):
    def body(d_ref, w_ref, xm_ref, gp_ref, dres_ref, z_ref, gq_ref, dxm_ref, dz_ref, dgp_ref, dgq_ref):
        _zero_first((dgp_ref, dgq_ref), pl.program_id(0) == 0)
        for rs in _row_subtiles(tm, SUB_ROWS):
            dh = _dot(d_ref[0, rs, :], w_ref[0:D_FF, :]) + _dot(d_ref[1, rs, :], w_ref[D_FF:2 * D_FF, :])
            dx, dyn = _norm_bwd_rows(xm_ref[rs, :], gp_ref[...], dh)
            dxm = dres_ref[rs, :] + dx
            dxm_ref[rs, :] = dxm
            _add_cols(dgp_ref, dyn)
            dz, dyn2 = _norm_bwd_rows(z_ref[rs, :], gq_ref[...], dxm)
            dz_ref[rs, :] = dz.astype(BF16)
            _add_cols(dgq_ref, dyn2)

    row = pl.BlockSpec((tm, D_MODEL), lambda i: (i, 0))
    gain = pl.BlockSpec((None, 1, D_MODEL), lambda i: (l, 0, 0))
    dgs = pl.BlockSpec((8, D_MODEL), lambda i: (0, 0))
    return _call(
        body, name="ffn_in_bwd", grid=(s // tm,),
        in_specs=[pl.BlockSpec((2, tm, D_FF), lambda i: (0, i, 0)),
                  pl.BlockSpec((2 * D_FF, D_MODEL), lambda i: (0, 0), pipeline_mode=pl.Buffered(1)),
                  row, gain, row, row, gain],
        out_specs=[row, row, dgs, dgs],
        out_shape=[jax.ShapeDtypeStruct((s, D_MODEL), F32), jax.ShapeDtypeStruct((s, D_MODEL), BF16),
                   jax.ShapeDtypeStruct((8, D_MODEL), F32), jax.ShapeDtypeStruct((8, D_MODEL), F32)],
        args=[dgu, wfin, xm, g_pre3, dres, z, g_post3], comm=comm)


def _dw_out(ync, yna, dz, s):
    half = D_MODEL // 2

    def body(a1_ref, a2_ref, b_ref, o_ref):
        b = b_ref[...]
        o_ref[0:half, :] = _dot_tn(a1_ref[...], b).astype(BF16)
        o_ref[half:D_MODEL, :] = _dot_tn(a2_ref[...], b).astype(BF16)

    out, = _call(
        body, name="dw_out", grid=(2,),
        in_specs=[pl.BlockSpec((s, half), lambda n: (0, 0)), pl.BlockSpec((s, half), lambda n: (0, 0)),
                  pl.BlockSpec((s, half), lambda n: (0, n))],
        out_specs=[pl.BlockSpec((D_MODEL, half), lambda n: (0, n))],
        out_shape=[jax.ShapeDtypeStruct((D_MODEL, D_MODEL), BF16)], args=[ync, yna, dz])
    return out


def _out_proj_bwd(dz, wout, o, g3, l, s, tm):
    def body(dz_ref, w_ref, o_ref, g_ref, dyc_ref, do_ref, dg_ref):
        gmat = _group_matrix()
        _zero_first((dg_ref,), pl.program_id(0) == 0)
        for rs in _row_subtiles(tm, SUB_ROWS):
            dy = _dot_nt(dz_ref[rs, :], w_ref[...])
            dyc_ref[rs, :] = dy[:, 0:CONV_WIDTH]
            for j in range(ATTN_WIDTH // 128):
                c0 = 128 * j
                ov = o_ref[rs, c0:c0 + 128]
                dyn = dy[:, CONV_WIDTH + c0:CONV_WIDTH + c0 + 128]
                r = lax.rsqrt(_group_mean(ov * ov, gmat) + EPS)
                on = ov * r
                gd = dyn * g_ref[:, c0:c0 + 128]
                do_ref[rs, c0:c0 + 128] = r * (gd - on * _group_mean(on * gd, gmat))
                dg_ref[0:1, c0:c0 + 128] += jnp.sum(dyn * on, axis=0, keepdims=True)

    halfrow = pl.BlockSpec((tm, ATTN_WIDTH), lambda i: (i, 0))
    return _call(
        body, name="out_proj_bwd", grid=(s // tm,),
        in_specs=[pl.BlockSpec((tm, D_MODEL), lambda i: (i, 0)),
                  pl.BlockSpec((D_MODEL, D_MODEL), lambda i: (0, 0)), halfrow,
                  pl.BlockSpec((None, 1, ATTN_WIDTH), lambda i: (l, 0, 0))],
        out_specs=[halfrow, halfrow, pl.BlockSpec((8, ATTN_WIDTH), lambda i: (0, 0))],
        out_shape=[jax.ShapeDtypeStruct((s, CONV_WIDTH), F32), jax.ShapeDtypeStruct((s, ATTN_WIDTH), F32),
                   jax.ShapeDtypeStruct((8, ATTN_WIDTH), F32)],
        args=[dz, wout, o, g3])


def _conv_bwd(pc, dyc, wc, g3, dq, dk, dv, l, s, tr):
    hb = tr // 8
    nt = s // tr
    ext = tr + 16
    last_hb = s // 8 - 1

    def body(pc_ref, prev_ref, next_ref, dy_ref, dyn_ref, wc_ref, g_ref, dq_ref, dk_ref, dv_ref,
             dpc_ref, dw_ref, dg_ref):
        i = pl.program_id(0)
        for part, ref in enumerate((dq_ref, dk_ref, dv_ref)):
            c = 3 * CONV_WIDTH + ATTN_WIDTH * part
            dpc_ref[:, c:c + ATTN_WIDTH] = ref[...]
        gmat = _group_matrix()
        row = lax.broadcasted_iota(jnp.int32, (ext, 128), 0) + (i * tr - 8)
        inside = jnp.where(row >= 0, jnp.where(row < s, 1, 0), 0) == 1

        @pl.when(i == 0)
        def _():
            dw_ref[...] = jnp.zeros(dw_ref.shape, F32)
            dg_ref[...] = jnp.zeros(dg_ref.shape, F32)

        def extend(ref_prev, ref_mid, ref_next, c):
            parts = [ref_prev[:, c:c + 128] if ref_prev is not None else jnp.zeros((8, 128), F32),
                     ref_mid[:, c:c + 128], ref_next[:, c:c + 128]]
            return jnp.concatenate(parts, axis=0)

        for j in range(CONV_WIDTH // 128):
            c0, c1, c2 = 128 * j, CONV_WIDTH + 128 * j, 2 * CONV_WIDTH + 128 * j
            hc = extend(prev_ref, pc_ref, next_ref, c0)
            bg = extend(prev_ref, pc_ref, next_ref, c1)
            cg = extend(prev_ref, pc_ref, next_ref, c2)
            dyn = extend(None, dy_ref, dyn_ref, c0)
            w0, w1, w2 = (wc_ref[0:1, c0:c0 + 128], wc_ref[1:2, c0:c0 + 128], wc_ref[2:3, c0:c0 + 128])
            gain = g_ref[:, c0:c0 + 128]
            u = jnp.where(inside, cg * hc, 0.0)
            u1 = pltpu.roll(u, 1, 0)
            u2 = pltpu.roll(u, 2, 0)
            out = u2 * w0 + u1 * w1 + u * w2
            yc = bg * out
            r = lax.rsqrt(_group_mean(yc * yc, gmat) + EPS)
            ycn = yc * r
            gd = dyn * gain
            dyc = r * (gd - ycn * _group_mean(ycn * gd, gmat))
            dout = jnp.where(inside, dyc * bg, 0.0)
            du = dout * w2 + pltpu.roll(dout, ext - 1, 0) * w1 + pltpu.roll(dout, ext - 2, 0) * w0
            sl = slice(8, 8 + tr)
            dpc_ref[:, c0:c0 + 128] = (du[sl] * cg[sl]).astype(BF16)
            dpc_ref[:, c1:c1 + 128] = (dyc[sl] * out[sl]).astype(BF16)
            dpc_ref[:, c2:c2 + 128] = (du[sl] * hc[sl]).astype(BF16)
            dw_ref[0:1, c0:c0 + 128] += jnp.sum(dout[sl] * u2[sl], axis=0, keepdims=True)
            dw_ref[1:2, c0:c0 + 128] += jnp.sum(dout[sl] * u1[sl], axis=0, keepdims=True)
            dw_ref[2:3, c0:c0 + 128] += jnp.sum(dout[sl] * u[sl], axis=0, keepdims=True)
            dg_ref[0:1, c0:c0 + 128] += jnp.sum(dyn[sl] * ycn[sl], axis=0, keepdims=True)

    wide = 3 * CONV_WIDTH
    return _call(
        body, name="conv_bwd", grid=(nt,),
        in_specs=[pl.BlockSpec((tr, wide), lambda i: (i, 0)),
                  pl.BlockSpec((8, wide), lambda i: (jnp.maximum(i * hb - 1, 0), 0)),
                  pl.BlockSpec((8, wide), lambda i: (jnp.minimum((i + 1) * hb, last_hb), 0)),
                  pl.BlockSpec((tr, CONV_WIDTH), lambda i: (i, 0)),
                  pl.BlockSpec((8, CONV_WIDTH), lambda i: (jnp.minimum((i + 1) * hb, last_hb), 0)),
                  pl.BlockSpec((None, 8, CONV_WIDTH), lambda i: (l, 0, 0)),
                  pl.BlockSpec((None, 1, CONV_WIDTH), lambda i: (l, 0, 0)),
                  pl.BlockSpec((tr, ATTN_WIDTH), lambda i: (i, 0)),
                  pl.BlockSpec((tr, ATTN_WIDTH), lambda i: (i, 0)),
                  pl.BlockSpec((tr, ATTN_WIDTH), lambda i: (i, 0))],
        out_specs=[pl.BlockSpec((tr, PROJ_WIDTH), lambda i: (i, 0)),
                   pl.BlockSpec((8, CONV_WIDTH), lambda i: (0, 0)),
                   pl.BlockSpec((8, CONV_WIDTH), lambda i: (0, 0))],
        out_shape=[jax.ShapeDtypeStruct((s, PROJ_WIDTH), BF16), jax.ShapeDtypeStruct((8, CONV_WIDTH), F32),
                   jax.ShapeDtypeStruct((8, CONV_WIDTH), F32)],
        args=[pc, pc, pc, dyc, dyc, wc, g3, dq, dk, dv])


def _attn_bwd(qkvp, biasm, o, lse, do, l, s, pad, comm=None):
    nb = s // Q_BLOCK
    qb0 = pad // Q_BLOCK
    scale = HEAD_DIM ** -0.5
    wide = 128 * ATTN_PAIRS

    def body(q_ref, k_ref, v_ref, b_ref, o_ref, lse_ref, do_ref,
             dq_ref, dk_ref, dv_ref, ds_ref, dk_acc, dv_acc):
        blk = pl.program_id(1)

        @pl.when(blk == 0)
        def _():
            dk_acc[...] = jnp.zeros(dk_acc.shape, F32)
            dv_acc[...] = jnp.zeros(dv_acc.shape, F32)
            ds_ref[...] = jnp.zeros(ds_ref.shape, F32)

        koff = pl.multiple_of(blk * Q_BLOCK + (pad - LEFT), Q_BLOCK)
        lane = lax.broadcasted_iota(jnp.int32, (1, 128), 1)
        kpos = lax.broadcasted_iota(jnp.int32, (1, K_BAND), 1) + (blk * Q_BLOCK - LEFT)
        kmask = jnp.where(kpos >= 0, 0.0, NEG_INF)
        for pr in range(ATTN_PAIRS):
            ls = slice(128 * pr, 128 * (pr + 1))
            q = q_ref[:, ls]
            kb = k_ref[pl.ds(koff, K_BAND), ls]
            vb = v_ref[pl.ds(koff, K_BAND), ls]
            dov = do_ref[:, ls]
            lse_v = lse_ref[:, ls]
            prod = dov * o_ref[:, ls]
            dq_parts = []
            dk_new = jnp.zeros((K_BAND, 128), F32)
            dv_new = jnp.zeros((K_BAND, 128), F32)
            for hh in range(2):
                in_head = (lane >> 6) == hh
                qm = jnp.where(in_head, q, jnp.zeros_like(q)) * jnp.asarray(scale, BF16)
                dom = jnp.where(in_head, dov, 0.0).astype(BF16)
                delta = jnp.sum(jnp.where(in_head, prod, 0.0), axis=1, keepdims=True)
                lse_h = lse_v[:, HEAD_DIM * hh:HEAD_DIM * hh + 1]
                sc = _dot_nt(qm, kb) + b_ref[2 * pr + hh] + kmask
                p = jnp.exp(sc - lse_h)
                dp = _dot_nt(dom, vb)
                ds = p * (dp - delta)
                ds_ref[2 * pr + hh] += ds
                dsb = ds.astype(BF16)
                dq_parts.append(_dot(dsb, kb) * scale)
                dk_new = dk_new + _dot_tn(dsb, qm)
                dv_new = dv_new + _dot_tn(p.astype(BF16), dom)
            dq_ref[:, ls] = jnp.where(lane < HEAD_DIM, dq_parts[0], dq_parts[1]).astype(BF16)
            dk_acc[pl.ds(koff, K_BAND), ls] += dk_new
            dv_acc[pl.ds(koff, K_BAND), ls] += dv_new

        @pl.when(blk == nb - 1)
        def _():
            dk_ref[...] = dk_acc[pad:pad + s, :].astype(BF16)
            dv_ref[...] = dv_acc[pad:pad + s, :].astype(BF16)

    n_grp = ATTN_WIDTH // wide
    qblk = pl.BlockSpec((Q_BLOCK, wide), lambda p, b: (b, p))
    col = pl.BlockSpec((s, wide), lambda p, b: (0, p))
    shp = jax.ShapeDtypeStruct((s, ATTN_WIDTH), BF16)
    return _call(
        body, name="attn_bwd", grid=(n_grp, nb),
        in_specs=[pl.BlockSpec((Q_BLOCK, wide), lambda p, b: (qb0 + b, p)),
                  pl.BlockSpec((s + pad, wide), lambda p, b: (0, n_grp + p)),
                  pl.BlockSpec((s + pad, wide), lambda p, b: (0, 2 * n_grp + p)),
                  pl.BlockSpec((None, 2 * ATTN_PAIRS, Q_BLOCK, K_BAND), lambda p, b: (l, p, 0, 0)),
                  qblk, qblk, qblk],
        out_specs=[qblk, col, col, pl.BlockSpec((2 * ATTN_PAIRS, Q_BLOCK, K_BAND), lambda p, b: (p, 0, 0))],
        out_shape=[shp, shp, shp, jax.ShapeDtypeStruct((N_HEADS, Q_BLOCK, K_BAND), F32)],
        scratch=[pltpu.VMEM((s + pad, wide), F32), pltpu.VMEM((s + pad, wide), F32)],
        args=[qkvp, qkvp, qkvp, biasm, o, lse, do], comm=comm)


def _dw_in(h, dproj, s):
    def body(a_ref, b_ref, o_ref):
        acc = _dot_tn(a_ref[...], b_ref[...])
        o_ref[0] = acc[:, 0:PROJ_SHARD].astype(BF16)
        o_ref[1] = acc[:, PROJ_SHARD:2 * PROJ_SHARD].astype(BF16)

    out, = _call(
        body, name="dw_in", grid=(4,),
        in_specs=[pl.BlockSpec((s, D_MODEL), lambda n: (0, 0)),
                  pl.BlockSpec((s, 2 * PROJ_SHARD), lambda n: (0, n))],
        out_specs=[pl.BlockSpec((2, D_MODEL, PROJ_SHARD), lambda n: (n, 0, 0))],
        out_shape=[jax.ShapeDtypeStruct((N_DEV, D_MODEL, PROJ_SHARD), BF16)], args=[h, dproj])
    return out


def _in_proj_bwd(dproj, win, x, g3, dres, l, s, tm, f_prev=None, g_post3=None, comm=None):
    chain = f_prev is not None

    def body(d_ref, w_ref, x_ref, g_ref, dres_ref, *rest):
        if chain:
            f_ref, gq_ref, dx_ref, dg_ref, df_ref, dgq_ref = rest
            _zero_first((dg_ref, dgq_ref), pl.program_id(0) == 0)
        else:
            dx_ref, dg_ref = rest
            _zero_first((dg_ref,), pl.program_id(0) == 0)
        for rs in _row_subtiles(tm, SUB_ROWS):
            dh = _dot_nt(d_ref[rs, 0:PROJ_SHARD], w_ref[0])
            for j in range(1, N_DEV):
                dh = dh + _dot_nt(d_ref[rs, PROJ_SHARD * j:PROJ_SHARD * (j + 1)], w_ref[j])
            dx, dyn = _norm_bwd_rows(x_ref[rs, :], g_ref[...], dh)
            dx = dres_ref[rs, :] + dx
            dx_ref[rs, :] = dx
            _add_cols(dg_ref, dyn)
            if chain:
                df, dyn2 = _norm_bwd_rows(f_ref[rs, :], gq_ref[...], dx)
                df_ref[rs, :] = df.astype(BF16)
                _add_cols(dgq_ref, dyn2)

    row = pl.BlockSpec((tm, D_MODEL), lambda i: (i, 0))
    dgs = pl.BlockSpec((8, D_MODEL), lambda i: (0, 0))
    in_specs = [pl.BlockSpec((tm, PROJ_WIDTH), lambda i: (i, 0)),
                pl.BlockSpec((N_DEV, D_MODEL, PROJ_SHARD), lambda i: (0, 0, 0)),
                row, pl.BlockSpec((None, 1, D_MODEL), lambda i: (l, 0, 0)), row]
    out_specs = [row, dgs]
    out_shape = [jax.ShapeDtypeStruct((s, D_MODEL), F32), jax.ShapeDtypeStruct((8, D_MODEL), F32)]
    args = [dproj, win, x, g3, dres]
    if chain:
        in_specs += [row, pl.BlockSpec((None, 1, D_MODEL), lambda i: (l - 1, 0, 0))]
        out_specs += [row, dgs]
        out_shape += [jax.ShapeDtypeStruct((s, D_MODEL), BF16), jax.ShapeDtypeStruct((8, D_MODEL), F32)]
        args += [f_prev, g_post3]
    return _call(body, name="in_proj_bwd", grid=(s // tm,), in_specs=in_specs, out_specs=out_specs,
                 out_shape=out_shape, args=args, comm=comm)


def _adamw(name, w, m, v, lands, owns=None, me=None):
    groups, rows, cols = w.shape
    assert len(lands) == groups
    n_part = lands[0].shape[0]
    tr = _row_tile(rows, tuple(c for c in (512, 352, 256, 176, 128, 64, 32, 16, 8) if c * cols <= 256 * 1024))
    c1 = 1.0 - ADAM_B1 ** ADAM_STEP
    c2 = 1.0 - ADAM_B2 ** ADAM_STEP
    n_own = groups if owns is not None else 0

    def body(*refs):
        if n_own:
            me_ref, refs = refs[0], refs[1:]
        w_ref, m_ref, v_ref = refs[:3]
        land_refs = refs[3:3 + groups]
        own_refs = refs[3 + groups:3 + groups + n_own]
        g_ref, d_ref, nm_ref, nv_ref = refs[3 + groups + n_own:]
        grp = pl.program_id(0)
        for gi in range(groups):
            @pl.when(grp == gi)
            def _():
                l_ref = land_refs[gi]
                g = None
                for p in range(n_part):
                    part = l_ref[p].astype(F32)
                    if n_own:
                        part = jnp.where(me_ref[0] == p, own_refs[gi][...].astype(F32), part)
                    g = part if g is None else g + part
                g_ref[...] = g
                m1 = ADAM_B1 * m_ref[...] + (1.0 - ADAM_B1) * g
                v1 = ADAM_B2 * v_ref[...] + (1.0 - ADAM_B2) * (g * g)
                nm_ref[...] = m1
                nv_ref[...] = v1
                d_ref[...] = -ADAM_LR * ((m1 / c1) / (jnp.sqrt(v1 / c2) + ADAM_EPS) + ADAM_WD * w_ref[...])

    blk = pl.BlockSpec((None, tr, cols), lambda g, i, *_: (g, i, 0))
    shp = jax.ShapeDtypeStruct((groups, rows, cols), F32)

    def land_spec(gi):
        return pl.BlockSpec((n_part, tr, cols), lambda g, i, *_: (0, jnp.where(g == gi, i, 0), 0))

    def own_spec(gi):
        if owns[gi].ndim == 3:
            return pl.BlockSpec((None, tr, cols), lambda g, i, me_ref: (me_ref[0], jnp.where(g == gi, i, 0), 0))
        return pl.BlockSpec((tr, cols), lambda g, i, me_ref: (jnp.where(g == gi, i, 0), 0))

    in_specs = [blk, blk, blk] + [land_spec(gi) for gi in range(groups)] + [own_spec(gi) for gi in range(n_own)]
    args = [w, m, v] + list(lands) + (list(owns) if n_own else [])
    if not n_own:
        return _call(body, name=name, grid=(groups, rows // tr), in_specs=in_specs,
                     out_specs=[blk, blk, blk, blk], out_shape=[shp, shp, shp, shp], args=args)
    return pl.pallas_call(
        body, name=name,
        grid_spec=pltpu.PrefetchScalarGridSpec(
            num_scalar_prefetch=1, grid=(groups, rows // tr), in_specs=in_specs, out_specs=[blk, blk, blk, blk]),
        out_shape=[shp, shp, shp, shp],
        compiler_params=pltpu.CompilerParams(dimension_semantics=("arbitrary", "arbitrary"),
                                             vmem_limit_bytes=VMEM_LIMIT),
    )(me, *args)


def _pack_small(rel, gco, gao, gpm, gqm, gpf, gqf):
    n_layers = rel.shape[0]
    relp = jnp.pad(rel, ((0, 0), (0, 0), (0, REL_PAD - rel.shape[2])))
    parts = [relp.reshape(n_layers * N_HEADS * REL_PAD // 128, 128)]
    parts += [a.reshape(-1, 128) for a in (gco, gao, gpm, gqm, gpf, gqf)]
    return jnp.concatenate(parts, axis=0)


def _unpack_small(p, n_layers):
    n_rel = n_layers * N_HEADS * REL_PAD // 128
    rel = p[:n_rel].reshape(n_layers, N_HEADS, REL_PAD)[:, :, :2 * REL_CLIP + 1]
    outs = [rel]
    r0 = n_rel
    for width in (CONV_WIDTH, ATTN_WIDTH, D_MODEL, D_MODEL, D_MODEL, D_MODEL):
        nr = n_layers * width // 128
        outs.append(p[r0:r0 + nr].reshape(n_layers, width))
        r0 += nr
    return outs


def kernel(x, w_in, w_conv, rel_bias, g_conv_out, g_attn_out, w_out, g_pre_mix, g_post_mix, g_pre_ffn, g_post_ffn, w_ffn_in, w_ffn_out, loss_target, m_w_in, m_w_conv, m_rel_bias, m_g_conv_out, m_g_attn_out, m_w_out, m_g_pre_mix, m_g_post_mix, m_g_pre_ffn, m_g_post_ffn, m_w_ffn_in, m_w_ffn_out, v_w_in, v_w_conv, v_rel_bias, v_g_conv_out, v_g_attn_out, v_w_out, v_g_pre_mix, v_g_post_mix, v_g_pre_ffn, v_g_post_ffn, v_w_ffn_in, v_w_ffn_out):
    n_layers = w_in.shape[0]
    s = x.shape[1]
    assert x.shape == (1, s, D_MODEL) and s % 1024 == 0
    assert w_in.shape == (n_layers, D_MODEL, PROJ_SHARD) and w_ffn_in.shape == (n_layers, D_MODEL, FF_SHARD)
    tm = 512
    tq = 1024 if s >= 2048 else 512
    tf = min(1024, s)
    x0 = x.reshape(s, D_MODEL)
    target = loss_target.reshape(s, D_MODEL)
    dev = _dev_index(lax.axis_index("x"), lax.axis_index("y"), lax.axis_index("c"))

    wt_ffn_in, mt_ffn_in, vt_ffn_in = (jnp.transpose(a, (0, 2, 1)) for a in (w_ffn_in, m_w_ffn_in, v_w_ffn_in))
    local_w = [_cast_bf16(w_in, "cast_w_in"), _cast_bf16(w_out, "cast_w_out"),
               _cast_bf16(wt_ffn_in, "cast_w_ffn_in"), _cast_bf16(w_ffn_out, "cast_w_ffn_out")]
    wc_local = jnp.pad(jnp.transpose(w_conv, (0, 2, 1)).reshape(-1), (0, 1024 - n_layers * 3 * 64)).reshape(8, 128)
    win_next, wc_g = _comm_only("gather_first", _Gather([(local_w[0], 0), (wc_local, None)]))
    weights = [None] * n_layers
    wc_full = wc_g.reshape(N_DEV, 1024)[:, :n_layers * 3 * 64].reshape(N_DEV, n_layers, 3, 64)
    wc_full = jnp.transpose(wc_full, (1, 2, 0, 3)).reshape(n_layers, 3, CONV_WIDTH)
    wc_full = jnp.pad(wc_full, ((0, 0), (0, 5), (0, 0)))

    g3 = {k: v.reshape(n_layers, 1, -1) for k, v in dict(
        conv=g_conv_out, attn=g_attn_out, pre_mix=g_pre_mix, post_mix=g_post_mix,
        pre_ffn=g_pre_ffn, post_ffn=g_post_ffn).items()}
    biasm = _bias_build(jnp.pad(rel_bias, ((0, 0), (0, 0), (0, REL_PAD - rel_bias.shape[2]))))

    saved = []
    xl = x0
    h = _norm_cast(x0, g3["pre_mix"], 0, tm)
    for l in range(n_layers):
        win = win_next
        pc, wout = _in_proj(h, win, s, tq, 0, comm=_Gather([(local_w[1], l)]))
        qkvp, = _in_proj(h, win, s, tq, 1)
        ync = _conv_fwd(pc, wc_full, g3["conv"], l, s, tm)
        o, lse, yna, wfin = _attn_fwd(qkvp, biasm, g3["attn"], l, s, tq, comm=_Gather([(local_w[2], l)]))
        wout = wout.reshape(D_MODEL, D_MODEL)
        z, xm, h2 = _out_proj_fwd(ync, yna, wout, xl, g3["post_mix"], g3["pre_ffn"], l, s, tq)
        gu, act, wfout = _ffn_in_fwd(h2, wfin.reshape(4, FF_PAIR, D_MODEL), s, tf, comm=_Gather([(local_w[3], l)]))
        wfo = wfout.reshape(D_FF, D_MODEL)
        l_next = min(l + 1, n_layers - 1)
        f, xo, h_next, *got = _ffn_out_fwd(act, wfo, xm, g3["post_ffn"], g3["pre_mix"], l, l_next, s, tq,
                                           comm=_Gather([(local_w[0], l + 1)]) if l + 1 < n_layers else None)
        weights[l] = [win, wout, wfin.reshape(2 * D_FF, D_MODEL), wfo]
        win_next = got[0] if got else None
        saved.append(dict(x=xl, h=h, pc=pc, qkvp=qkvp, ync=ync, yna=yna, o=o, lse=lse, z=z, xm=xm,
                          h2=h2, gu=gu, act=act, f=f))
        xl, h = xo, h_next

    dx, sq, df, dg_post_ffn = _loss_grad(xl, target, saved[-1]["f"], g3["post_ffn"], n_layers - 1, s, tm)
    loss = lax.psum(jnp.sum(sq) * (0.5 / D_MODEL), ("x", "y", "c"))

    lands = dict(win=[None] * n_layers, wout=[None] * n_layers, wfin=[None] * n_layers, wfout=[None] * n_layers)
    small = {k: [None] * n_layers for k in ("gco", "gao", "gpm", "gqm", "gpf", "gqf", "wc")}
    d_rel = [None] * n_layers
    started = []

    def start(name, keys, l, arrays):
        items = [(a, False) for a in arrays]
        send_sems, recv_sems, srcs, zones, token = _exchange_start(name + "_start", items)
        started.append((name, keys, l, items, send_sems, recv_sems, srcs, zones))
        return token[0:1, 0:1].reshape(1, 1, 1)

    for l in reversed(range(n_layers)):
        sv = saved[l]
        win, wout, wfin, wfo = weights[l]
        small["gqf"][l] = dg_post_ffn[0]
        dgu, = _ffn_out_bwd(df, wfo, sv["gu"], s, tf)
        d_wfout = _dw_ffn_out(sv["act"], df, s).reshape(N_DEV, FFO_SHARD, D_MODEL)
        d_wfin = _dw_ffn_in(sv["h2"], dgu, s).reshape(N_DEV, FF_SHARD, D_MODEL)
        token = start(f"exchange_ffn{l}", ("wfout", "wfin"), l, [d_wfout, d_wfin])
        dxm, dz, dg_pre_ffn, dg_post_mix = _ffn_in_bwd(
            dgu, wfin, sv["xm"], g3["pre_ffn"] + token, dx, sv["z"], g3["post_mix"], l, s, tm)
        small["gpf"][l] = dg_pre_ffn[0]
        small["gqm"][l] = dg_post_mix[0]
        d_wout = _dw_out(sv["ync"], sv["yna"], dz, s).reshape(N_DEV, D_MODEL // N_DEV, D_MODEL)
        dyc, do, dg_attn = _out_proj_bwd(dz, wout, sv["o"], g3["attn"], l, s, tq)
        small["gao"][l] = dg_attn[0]
        dq, dk, dv, ds_sum = _attn_bwd(sv["qkvp"], biasm, sv["o"], sv["lse"], do, l, s, tq)
        d_rel[l] = _bias_bwd(ds_sum[None])
        dproj, dwc, dg_conv = _conv_bwd(sv["pc"], dyc, wc_full, g3["conv"], dq, dk, dv, l, s, tm)
        small["wc"][l] = dwc[0:3]
        small["gco"][l] = dg_conv[0]
        d_win = _dw_in(sv["h"], dproj, s)
        token = start(f"exchange_mix{l}", ("wout", "win"), l, [d_wout, d_win])
        if l > 0:
            dx, dg_pre_mix, df, dg_post_ffn = _in_proj_bwd(
                dproj, win, sv["x"], g3["pre_mix"] + token, dxm, l, s, tm, f_prev=saved[l - 1]["f"],
                g_post3=g3["post_ffn"])
        else:
            dx, dg_pre_mix = _in_proj_bwd(dproj, win, sv["x"], g3["pre_mix"] + token, dxm, l, s, tm)
        small["gpm"][l] = dg_pre_mix[0]
    grad_x = dx.reshape(1, s, D_MODEL)

    small_vec = jnp.concatenate(
        [_pack_small(jnp.concatenate(d_rel)[:, :, :2 * REL_CLIP + 1],
                     *[jnp.stack(small[k]) for k in ("gco", "gao", "gpm", "gqm", "gpf", "gqf")]),
         jnp.stack(small["wc"]).reshape(-1, 128)], axis=0)
    small_vec = jnp.pad(small_vec, ((0, (-small_vec.shape[0]) % 8), (0, 0)))
    small_items = [(small_vec, True)]
    small_sems = _exchange_start("exchange_small_start", small_items)

    owns = dict(win=[None] * n_layers, wout=[None] * n_layers, wfin=[None] * n_layers, wfout=[None] * n_layers)

    def wait(which, after):
        for name, keys, l, items, send_sems, recv_sems, srcs, zones in started:
            if keys == which:
                srcs, zones = _exchange_wait(name + "_wait", items, send_sems, recv_sems, srcs, zones, after)
                for key, src, zone in zip(keys, srcs, zones):
                    owns[key][l], lands[key][l] = src, zone

    me = dev.astype(jnp.int32).reshape(1)
    wait(("wfout", "wfin"), dx)
    r_fin = [jnp.transpose(t, (0, 2, 1)) for t in _adamw(
        "adamw_w_ffn_in", wt_ffn_in, mt_ffn_in, vt_ffn_in, lands["wfin"], owns["wfin"], me)]
    r_fout = _adamw("adamw_w_ffn_out", w_ffn_out, m_w_ffn_out, v_w_ffn_out, lands["wfout"], owns["wfout"], me)
    wait(("wout", "win"), r_fout[0])
    r_out = _adamw("adamw_w_out", w_out, m_w_out, v_w_out, lands["wout"], owns["wout"], me)
    r_in = _adamw("adamw_w_in", w_in, m_w_in, v_w_in, lands["win"], owns["win"], me)
    (small_own,), (land_small,) = _exchange_wait(
        "exchange_small_wait", small_items, small_sems[0], small_sems[1], small_sems[2], small_sems[3], r_in[0])

    n_rep = 64 * n_layers
    rep = _adamw(
        "adamw_replicated",
        _pack_small(rel_bias, g_conv_out, g_attn_out, g_pre_mix, g_post_mix, g_pre_ffn, g_post_ffn)[None],
        _pack_small(m_rel_bias, m_g_conv_out, m_g_attn_out, m_g_pre_mix, m_g_post_mix, m_g_pre_ffn, m_g_post_ffn)[None],
        _pack_small(v_rel_bias, v_g_conv_out, v_g_attn_out, v_g_pre_mix, v_g_post_mix, v_g_pre_ffn, v_g_post_ffn)[None],
        [land_small[:, :n_rep]], [small_own[:n_rep]], me)
    rep = [_unpack_small(t[0], n_layers) for t in rep]

    wc_rows = n_layers * 3 * CONV_WIDTH // 128
    zeros_wc = jnp.zeros((1, wc_rows, 128), F32)
    g_wc_full = _adamw("sum_w_conv", zeros_wc, zeros_wc, zeros_wc, [land_small[:, n_rep:n_rep + wc_rows]],
                       [small_own[n_rep:n_rep + wc_rows]], me)[0]
    g_wc_full = g_wc_full.reshape(n_layers, 3, CONV_WIDTH)
    g_wc = lax.dynamic_slice_in_dim(g_wc_full, dev * (CONV_WIDTH // N_DEV), CONV_WIDTH // N_DEV, axis=2)
    g_wc = jnp.transpose(g_wc, (0, 2, 1))

    def tiny(a):
        flat = a.reshape(-1)
        return jnp.pad(flat, (0, (-flat.shape[0]) % 1024)).reshape(1, -1, 128)

    r_wc = _adamw("adamw_w_conv", tiny(w_conv), tiny(m_w_conv), tiny(v_w_conv), [tiny(g_wc)])
    r_wc = [t.reshape(-1)[:w_conv.size].reshape(w_conv.shape) for t in r_wc]

    def leaf(kind):
        return [r_in[kind], r_wc[kind], rep[kind][0], rep[kind][1], rep[kind][2], r_out[kind],
                rep[kind][3], rep[kind][4], rep[kind][5], rep[kind][6], r_fin[kind], r_fout[kind]]

    return (loss, grad_x, *leaf(0), *leaf(1), *leaf(2), *leaf(3))
```

```python
import math

import jax
import jax.numpy as jnp
from jax import lax
from jax.experimental import pallas as pl
from jax.experimental.pallas import tpu as pltpu

F32 = jnp.float32
BF16 = jnp.bfloat16

D_MODEL = 1024
N_DEV = 8
CHUNK = 64
N_LEFT_CHUNKS = 8
CONV_WIDTH = 512
ATTN_WIDTH = 512
HEAD_DIM = 64
N_HEADS = 8
REL_CLIP = 128
REL_PAD = 384
PROJ_WIDTH = 3072
PROJ_SHARD = PROJ_WIDTH // N_DEV
D_FF = 2816
FF_SHARD = 2 * D_FF // N_DEV
FFO_SHARD = D_FF // N_DEV
FF_PAIR = 2 * FF_SHARD
EPS = 1e-6
NEG_INF = -1e30
ATTN_PAIRS = 2
Q_BLOCK = 4 * CHUNK
K_BAND = Q_BLOCK + N_LEFT_CHUNKS * CHUNK
LEFT = N_LEFT_CHUNKS * CHUNK
TOEP = 1024

ADAM_LR = 0.001
ADAM_B1 = 0.9
ADAM_B2 = 0.999
ADAM_EPS = 1e-08
ADAM_WD = 0.01
ADAM_STEP = 10

VMEM_LIMIT = 52 * 1024 * 1024
SUB_ROWS = 256
MESH = pl.DeviceIdType.MESH
ANY = pl.BlockSpec(memory_space=pl.ANY)

NT = (((1,), (1,)), ((), ()))
TN = (((0,), (0,)), ((), ()))


def _dot(a, b):
    return jnp.dot(a, b, preferred_element_type=F32)


def _dot_nt(a, b):
    return lax.dot_general(a, b, NT, preferred_element_type=F32)


def _dot_tn(a, b):
    return lax.dot_general(a, b, TN, preferred_element_type=F32)


def _rstd(v):
    return lax.rsqrt(jnp.mean(v * v, axis=-1, keepdims=True) + EPS)


def _group_matrix():
    r = lax.broadcasted_iota(jnp.int32, (128, 128), 0) >> 6
    c = lax.broadcasted_iota(jnp.int32, (128, 128), 1) >> 6
    return jnp.where(r == c, 1.0, 0.0).astype(BF16)


def _group_mean(v, gmat):
    hi = v.astype(BF16)
    lo = (v - hi.astype(F32)).astype(BF16)
    return (_dot(hi, gmat) + _dot(lo, gmat)) * (1.0 / HEAD_DIM)


def _split3(v):
    hi = v.astype(BF16)
    r1 = v - hi.astype(F32)
    mid = r1.astype(BF16)
    lo = (r1 - mid.astype(F32)).astype(BF16)
    return hi, mid, lo


def _row_tile(rows, cands=(1024, 512, 704, 256, 128, 64, 32, 16)):
    for c in cands:
        if rows % c == 0:
            return c
    return rows


def _dev_index(px, py, pc):
    return 4 * px + 2 * py + pc


def _when(cond):
    if cond is True:
        return lambda fn: fn()
    return pl.when(cond)


def _phases(grid):
    def phases():
        if not grid:
            return True, True, True
        lin = pl.program_id(0)
        for a in range(1, len(grid)):
            lin = lin * grid[a] + pl.program_id(a)
        total = math.prod(grid)
        return lin == 0, lin == (3 * total) // 4, lin == total - 1
    return phases


class _Gather:
    def __init__(self, items):
        self.items = items
        self.args = [a for a, _ in items]
        n = len(items)
        self.out_shape = [jax.ShapeDtypeStruct((N_DEV,) + (a.shape if lay is None else a.shape[1:]), a.dtype)
                          for a, lay in items]
        self.scratch = [pltpu.SemaphoreType.DMA((n, 7)), pltpu.SemaphoreType.DMA((n, 7)),
                        pltpu.SemaphoreType.DMA((n,))]

    def _ctx(self, ins, outs, sems):
        send_sems, recv_sems, local_sems = sems
        x, y, c = lax.axis_index("x"), lax.axis_index("y"), lax.axis_index("c")
        chips = [(1 - x, y), (x, 1 - y), (1 - x, 1 - y)]

        def src(k):
            lay = self.items[k][1]
            return ins[k] if lay is None else ins[k].at[lay]

        def copy(k, s, idx, to, from_src=False):
            return pltpu.make_async_remote_copy(
                src_ref=src(k) if from_src else outs[k].at[idx], dst_ref=outs[k].at[idx],
                send_sem=send_sems.at[k, s], recv_sem=recv_sems.at[k, s],
                device_id=to, device_id_type=MESH)

        def local(k):
            return pltpu.make_async_copy(src(k), outs[k].at[_dev_index(x, y, c)], local_sems.at[k])

        return x, y, c, chips, copy, local

    def start(self, ins, outs, sems, cond):
        n = len(self.items)

        @_when(cond)
        def _():
            x, y, c, chips, copy, local = self._ctx(ins, outs, sems)
            me = _dev_index(x, y, c)
            for k in range(n):
                local(k).start()
                copy(k, 0, me, (x, y, 1 - c), from_src=True).start()
                for j, chip in enumerate(chips):
                    copy(k, 1 + j, me, (chip[0], chip[1], c), from_src=True).start()

    def forward(self, ins, outs, sems, cond):
        n = len(self.items)

        @_when(cond)
        def _():
            x, y, c, chips, copy, local = self._ctx(ins, outs, sems)
            for j, chip in enumerate(chips):
                idx = _dev_index(chip[0], chip[1], c)
                for k in range(n):
                    copy(k, 1 + j, idx, (x, y, c)).wait_recv()
                    copy(k, 4 + j, idx, (x, y, 1 - c)).start()

    def finish(self, ins, outs, sems, cond):
        n = len(self.items)

        @_when(cond)
        def _():
            x, y, c, chips, copy, local = self._ctx(ins, outs, sems)
            me = _dev_index(x, y, c)
            for k in range(n):
                copy(k, 0, _dev_index(x, y, 1 - c), (x, y, c)).wait_recv()
            for j, chip in enumerate(chips):
                idx = _dev_index(chip[0], chip[1], 1 - c)
                for k in range(n):
                    copy(k, 4 + j, idx, (x, y, c)).wait_recv()
            for k in range(n):
                for s in range(4):
                    copy(k, s, me, (x, y, c), from_src=True).wait_send()
                for j, chip in enumerate(chips):
                    copy(k, 4 + j, _dev_index(chip[0], chip[1], c), (x, y, c)).wait_send()
                local(k).wait()


_PEER_FLIPS = [(0, 0, 1), (1, 0, 0), (0, 1, 0), (1, 1, 0), (1, 0, 1), (0, 1, 1), (1, 1, 1)]


def _call(body, *, name, grid, in_specs, out_specs, out_shape, args, scratch=(), comm=None):
    n_hi, n_ho, n_hs = len(args), len(out_shape), len(scratch)
    c_args = list(comm.args) if comm else []
    c_out = list(comm.out_shape) if comm else []
    c_scr = list(comm.scratch) if comm else []
    phases = _phases(grid)

    def kern(*refs):
        cuts = [n_hi, len(c_args), n_ho, len(c_out), n_hs, len(c_scr)]
        parts, pos = [], 0
        for n in cuts:
            parts.append(refs[pos:pos + n])
            pos += n
        hi, ci, ho, co, hs, cs = parts
        if comm:
            first, mid, last = phases()
            comm.start(ci, co, cs, first)
            comm.forward(ci, co, cs, mid)
        body(*hi, *ho, *hs)
        if comm:
            comm.finish(ci, co, cs, last)

    sem = ("arbitrary",) * len(grid) if grid else None
    return pl.pallas_call(
        kern, name=name, grid=grid,
        in_specs=list(in_specs) + [ANY] * len(c_args),
        out_specs=list(out_specs) + [ANY] * len(c_out),
        out_shape=list(out_shape) + c_out,
        scratch_shapes=list(scratch) + c_scr,
        compiler_params=pltpu.CompilerParams(dimension_semantics=sem, vmem_limit_bytes=VMEM_LIMIT),
    )(*args, *c_args)


def _comm_only(name, comm):
    return _call(lambda: None, name=name, grid=(), in_specs=[], out_specs=[], out_shape=[], args=[], comm=comm)


HBM_SPEC = pl.BlockSpec(memory_space=pltpu.HBM)
SEM_SPEC = pl.BlockSpec(memory_space=pltpu.SEMAPHORE)
SIDE_EFFECT = pltpu.SideEffectType.DATAFLOW_SIDE_EFFECTING


def _exchange_peer(x, y, c, s):
    fx, fy, fc = _PEER_FLIPS[s]
    return x ^ fx, y ^ fy, c ^ fc


def _exchange_start(name, items):
    n = len(items)
    srcs = [pltpu.with_memory_space_constraint(a, pltpu.HBM) for a, _ in items]
    land_shapes = [(N_DEV,) + (a.shape if whole else a.shape[1:]) for a, whole in items]
    lands = [pltpu.with_memory_space_constraint(lax.empty(shp, a.dtype), pltpu.HBM)
             for shp, (a, _) in zip(land_shapes, items)]

    n_sem = 7 * n

    def body(*refs):
        src_refs, land_refs = refs[:n], refs[n:2 * n]
        send_sems = refs[2 * n:2 * n + n_sem]
        recv_sems = refs[2 * n + n_sem:2 * n + 2 * n_sem]
        token = refs[-1]
        x, y, c = lax.axis_index("x"), lax.axis_index("y"), lax.axis_index("c")
        me = _dev_index(x, y, c)
        for s in range(7):
            px, py, pc = _exchange_peer(x, y, c, s)
            for k in range(n):
                src = src_refs[k] if items[k][1] else src_refs[k].at[_dev_index(px, py, pc)]
                pltpu.make_async_remote_copy(
                    src_ref=src, dst_ref=land_refs[k].at[me],
                    send_sem=send_sems[7 * k + s], recv_sem=recv_sems[7 * k + s],
                    device_id=(px, py, pc), device_id_type=MESH).start()
        token[...] = jnp.zeros(token.shape, token.dtype)

    outs = pl.pallas_call(
        body, name=name,
        out_shape=(*[pltpu.SemaphoreType.DMA(())] * (2 * n_sem),
                   *[pltpu.HBM(a.shape, a.dtype) for a in srcs],
                   *[pltpu.HBM(shp, a.dtype) for shp, a in zip(land_shapes, srcs)],
                   jax.ShapeDtypeStruct((8, 128), F32)),
        in_specs=[HBM_SPEC] * (2 * n),
        out_specs=(*[SEM_SPEC] * (2 * n_sem), *[HBM_SPEC] * (2 * n), pl.BlockSpec(memory_space=pltpu.VMEM)),
        input_output_aliases={i: 2 * n_sem + i for i in range(2 * n)},
        compiler_params=pltpu.CompilerParams(has_side_effects=SIDE_EFFECT),
    )(*srcs, *lands)
    base = 2 * n_sem
    return (list(outs[:n_sem]), list(outs[n_sem:base]), list(outs[base:base + n]),
            list(outs[base + n:base + 2 * n]), outs[-1])


def _exchange_wait(name, items, send_sems, recv_sems, srcs, lands, after):
    n = len(items)

    n_sem = 7 * n

    def body(*refs):
        src_refs, land_refs = refs[:n], refs[n:2 * n]
        send_refs = refs[2 * n:2 * n + n_sem]
        recv_refs = refs[2 * n + n_sem:2 * n + 2 * n_sem]
        x, y, c = lax.axis_index("x"), lax.axis_index("y"), lax.axis_index("c")
        for s in range(7):
            for k in range(n):
                copy = pltpu.make_async_remote_copy(
                    src_ref=src_refs[k] if items[k][1] else src_refs[k].at[0], dst_ref=land_refs[k].at[0],
                    send_sem=send_refs[7 * k + s], recv_sem=recv_refs[7 * k + s],
                    device_id=(x, y, c), device_id_type=MESH)
                copy.wait_send()
                copy.wait_recv()

    outs = pl.pallas_call(
        body, name=name,
        out_shape=(*[pltpu.HBM(a.shape, a.dtype) for a in srcs], *[pltpu.HBM(a.shape, a.dtype) for a in lands]),
        in_specs=[HBM_SPEC] * (2 * n) + [SEM_SPEC] * (2 * n_sem) + [ANY],
        out_specs=tuple([HBM_SPEC] * (2 * n)),
        input_output_aliases={i: i for i in range(2 * n)},
        compiler_params=pltpu.CompilerParams(has_side_effects=SIDE_EFFECT),
    )(*srcs, *lands, *send_sems, *recv_sems, after)
    return list(outs[:n]), list(outs[n:])


def _cast_bf16(x, name):
    shape = x.shape
    x2 = x.reshape(-1, shape[-1])
    rows, cols = x2.shape
    tr = _row_tile(rows)

    def body(x_ref, o_ref):
        o_ref[...] = x_ref[...].astype(BF16)

    blk = pl.BlockSpec((tr, cols), lambda i: (i, 0))
    out, = _call(body, name=name, grid=(rows // tr,), in_specs=[blk], out_specs=[blk],
                 out_shape=[jax.ShapeDtypeStruct((rows, cols), BF16)], args=[x2])
    return out.reshape(shape)


def _norm_cast(x, g3, l, tm):
    s = x.shape[0]

    def body(x_ref, g_ref, o_ref):
        v = x_ref[...]
        o_ref[...] = (v * _rstd(v) * g_ref[...]).astype(BF16)

    row = pl.BlockSpec((tm, D_MODEL), lambda i: (i, 0))
    out, = _call(body, name="norm_cast", grid=(s // tm,),
                 in_specs=[row, pl.BlockSpec((None, 1, D_MODEL), lambda i: (l, 0, 0))], out_specs=[row],
                 out_shape=[jax.ShapeDtypeStruct((s, D_MODEL), BF16)], args=[x, g3])
    return out


def _in_proj(h, win, s, tq, part, comm=None):
    pad = part
    dtype = BF16 if part else F32

    def body(a_ref, b_ref, o_ref):
        def compute():
            w = jnp.concatenate([b_ref[j] for j in range(4)], axis=1)
            o_ref[...] = _dot(a_ref[...], w).astype(dtype)

        if pad:
            i = pl.program_id(0)

            @pl.when(i == 0)
            def _():
                o_ref[...] = jnp.zeros(o_ref.shape, dtype)

            pl.when(i > 0)(compute)
        else:
            compute()

    return _call(
        body, name="in_proj_qkv" if part else "in_proj_conv", grid=(s // tq + pad,),
        in_specs=[pl.BlockSpec((tq, D_MODEL), lambda i: (jnp.maximum(i - pad, 0), 0)),
                  pl.BlockSpec((4, D_MODEL, PROJ_SHARD), lambda i: (part, 0, 0))],
        out_specs=[pl.BlockSpec((tq, 4 * PROJ_SHARD), lambda i: (i, 0))],
        out_shape=[jax.ShapeDtypeStruct((s + pad * tq, PROJ_WIDTH // 2), dtype)], args=[h, win], comm=comm)


def _conv_fwd(pc, wc, g3, l, s, tr):
    hb = tr // 8

    def body(pc_ref, prev_ref, wc_ref, g_ref, o_ref):
        i = pl.program_id(0)
        gmat = _group_matrix()
        for j in range(CONV_WIDTH // 128):
            c0, c1, c2 = 128 * j, CONV_WIDTH + 128 * j, 2 * CONV_WIDTH + 128 * j
            hc = pc_ref[:, c0:c0 + 128]
            bg = pc_ref[:, c1:c1 + 128]
            cg = pc_ref[:, c2:c2 + 128]
            u_prev = jnp.where(i > 0, prev_ref[:, c2:c2 + 128] * prev_ref[:, c0:c0 + 128], 0.0)
            u = cg * hc
            full = jnp.concatenate([u_prev, u], axis=0)
            u1 = pltpu.roll(full, 1, 0)[8:]
            u2 = pltpu.roll(full, 2, 0)[8:]
            out = (u2 * wc_ref[0:1, c0:c0 + 128] + u1 * wc_ref[1:2, c0:c0 + 128]
                   + u * wc_ref[2:3, c0:c0 + 128])
            yc = bg * out
            r = lax.rsqrt(_group_mean(yc * yc, gmat) + EPS)
            o_ref[:, c0:c0 + 128] = (yc * r * g_ref[:, c0:c0 + 128]).astype(BF16)

    out, = _call(
        body, name="conv_fwd", grid=(s // tr,),
        in_specs=[pl.BlockSpec((tr, 3 * CONV_WIDTH), lambda i: (i, 0)),
                  pl.BlockSpec((8, 3 * CONV_WIDTH), lambda i: (jnp.maximum(i * hb - 1, 0), 0)),
                  pl.BlockSpec((None, 8, CONV_WIDTH), lambda i: (l, 0, 0)),
                  pl.BlockSpec((None, 1, CONV_WIDTH), lambda i: (l, 0, 0))],
        out_specs=[pl.BlockSpec((tr, CONV_WIDTH), lambda i: (i, 0))],
        out_shape=[jax.ShapeDtypeStruct((s, CONV_WIDTH), BF16)], args=[pc, pc, wc, g3])
    return out


def _toeplitz_source():
    r_i = lax.broadcasted_iota(jnp.int32, (REL_PAD, TOEP), 0)
    m_i = lax.broadcasted_iota(jnp.int32, (REL_PAD, TOEP), 1)
    idx = jnp.clip((K_BAND - 1) - m_i, -REL_CLIP, REL_CLIP) + REL_CLIP
    return jnp.where(r_i == idx, 1.0, 0.0).astype(BF16)


def _bias_build(rbp, comm=None):
    n_layers = rbp.shape[0]

    def body(rb_ref, o_ref, t_ref):
        pmat = _toeplitz_source()
        hi, mid, lo = _split3(rb_ref[...])
        t_ref[...] = _dot(hi, pmat) + _dot(mid, pmat) + _dot(lo, pmat)
        shift = (CHUNK - 1) - lax.broadcasted_iota(jnp.int32, (CHUNK, TOEP), 0)
        kchunk = lax.broadcasted_iota(jnp.int32, (CHUNK, K_BAND), 1) >> 6
        for h in range(N_HEADS):
            b = jnp.broadcast_to(t_ref[pl.ds(h, 1), :], (CHUNK, TOEP))
            for bit in range(6):
                rolled = pltpu.roll(b, TOEP - (1 << bit), 1)
                b = jnp.where(((shift >> bit) & 1) == 1, rolled, b)
            for cq in range(Q_BLOCK // CHUNK):
                off = CHUNK * (Q_BLOCK // CHUNK - 1 - cq)
                band = pltpu.roll(b, TOEP - off, 1) if off else b
                dchunk = kchunk - cq
                in_band = jnp.where(dchunk >= 0, jnp.where(dchunk <= N_LEFT_CHUNKS, 1, 0), 0) == 1
                o_ref[h, CHUNK * cq:CHUNK * (cq + 1), :] = jnp.where(in_band, band[:, :K_BAND], NEG_INF)

    return _call(
        body, name="bias_build", grid=(n_layers,),
        in_specs=[pl.BlockSpec((None, N_HEADS, REL_PAD), lambda l: (l, 0, 0))],
        out_specs=[pl.BlockSpec((None, N_HEADS, Q_BLOCK, K_BAND), lambda l: (l, 0, 0, 0))],
        out_shape=[jax.ShapeDtypeStruct((n_layers, N_HEADS, Q_BLOCK, K_BAND), F32)],
        scratch=[pltpu.VMEM((N_HEADS, TOEP), F32)], args=[rbp], comm=comm)


def _bias_bwd(ds_sum):
    n_layers = ds_sum.shape[0]

    def body(ds_ref, o_ref, t_ref):
        pmat = _toeplitz_source()
        shift = (CHUNK - 1) - lax.broadcasted_iota(jnp.int32, (CHUNK, TOEP), 0)
        for h in range(N_HEADS):
            d = None
            for cq in range(Q_BLOCK // CHUNK):
                off = CHUNK * (Q_BLOCK // CHUNK - 1 - cq)
                part = jnp.concatenate([ds_ref[h, CHUNK * cq:CHUNK * (cq + 1), :],
                                        jnp.zeros((CHUNK, TOEP - K_BAND), F32)], axis=1)
                part = pltpu.roll(part, off, 1) if off else part
                d = part if d is None else d + part
            for bit in range(6):
                rolled = pltpu.roll(d, 1 << bit, 1)
                d = jnp.where(((shift >> bit) & 1) == 1, rolled, d)
            t_ref[pl.ds(h, 1), :] = jnp.sum(d, axis=0, keepdims=True)
        hi, mid, lo = _split3(t_ref[...])
        o_ref[...] = _dot_nt(hi, pmat) + _dot_nt(mid, pmat) + _dot_nt(lo, pmat)

    out, = _call(
        body, name="bias_bwd", grid=(n_layers,),
        in_specs=[pl.BlockSpec((None, N_HEADS, Q_BLOCK, K_BAND), lambda l: (l, 0, 0, 0))],
        out_specs=[pl.BlockSpec((None, N_HEADS, REL_PAD), lambda l: (l, 0, 0))],
        out_shape=[jax.ShapeDtypeStruct((n_layers, N_HEADS, REL_PAD), F32)],
        scratch=[pltpu.VMEM((N_HEADS, TOEP), F32)], args=[ds_sum])
    return out


def _attn_fwd(qkvp, biasm, g3, l, s, pad, comm=None):
    nb = s // Q_BLOCK
    qb0 = pad // Q_BLOCK
    scale = HEAD_DIM ** -0.5
    wide = 128 * ATTN_PAIRS

    def body(q_ref, k_ref, v_ref, b_ref, g_ref, o_ref, lse_ref, yn_ref):
        blk = pl.program_id(1)
        koff = pl.multiple_of(blk * Q_BLOCK + (pad - LEFT), Q_BLOCK)
        lane = lax.broadcasted_iota(jnp.int32, (1, 128), 1)
        kpos = lax.broadcasted_iota(jnp.int32, (1, K_BAND), 1) + (blk * Q_BLOCK - LEFT)
        kmask = jnp.where(kpos >= 0, 0.0, NEG_INF)
        gmat = _group_matrix()
        for pr in range(ATTN_PAIRS):
            ls = slice(128 * pr, 128 * (pr + 1))
            q = q_ref[:, ls]
            kb = k_ref[pl.ds(koff, K_BAND), ls]
            vb = v_ref[pl.ds(koff, K_BAND), ls]
            outs, lses = [], []
            for hh in range(2):
                in_head = (lane >> 6) == hh
                qm = jnp.where(in_head, q, jnp.zeros_like(q)) * jnp.asarray(scale, BF16)
                sc = _dot_nt(qm, kb) + b_ref[2 * pr + hh] + kmask
                m = jnp.max(sc, axis=1, keepdims=True)
                e = jnp.exp(sc - m)
                den = jnp.sum(e, axis=1, keepdims=True)
                outs.append(_dot(e.astype(BF16), vb) * (1.0 / den))
                lses.append(m + jnp.log(den))
            first = lane < HEAD_DIM
            o = jnp.where(first, outs[0], outs[1])
            o_ref[:, ls] = o
            lse_ref[:, ls] = jnp.where(first, lses[0], lses[1])
            r = lax.rsqrt(_group_mean(o * o, gmat) + EPS)
            yn_ref[:, ls] = (o * r * g_ref[:, ls]).astype(BF16)

    blk_out = pl.BlockSpec((Q_BLOCK, wide), lambda p, b: (b, p))
    n_grp = ATTN_WIDTH // wide
    return _call(
        body, name="attn_fwd", grid=(n_grp, nb),
        in_specs=[pl.BlockSpec((Q_BLOCK, wide), lambda p, b: (qb0 + b, p)),
                  pl.BlockSpec((s + pad, wide), lambda p, b: (0, n_grp + p)),
                  pl.BlockSpec((s + pad, wide), lambda p, b: (0, 2 * n_grp + p)),
                  pl.BlockSpec((None, 2 * ATTN_PAIRS, Q_BLOCK, K_BAND), lambda p, b: (l, p, 0, 0)),
                  pl.BlockSpec((None, 1, wide), lambda p, b: (l, 0, p))],
        out_specs=[blk_out, blk_out, blk_out],
        out_shape=[jax.ShapeDtypeStruct((s, ATTN_WIDTH), F32),
                   jax.ShapeDtypeStruct((s, ATTN_WIDTH), F32),
                   jax.ShapeDtypeStruct((s, ATTN_WIDTH), BF16)],
        args=[qkvp, qkvp, qkvp, biasm, g3], comm=comm)


def _out_proj_fwd(ync, yna, wout, x, g_post3, g_next3, l, s, tm):
    half = D_MODEL // 2

    def body(a1_ref, a2_ref, w_ref, x_ref, gp_ref, gn_ref, z_ref, xm_ref, h_ref):
        for rs in _row_subtiles(tm, SUB_ROWS):
            z = _dot(a1_ref[rs, :], w_ref[0:half, :]) + _dot(a2_ref[rs, :], w_ref[half:D_MODEL, :])
            z_ref[rs, :] = z
            xm = x_ref[rs, :] + z * _rstd(z) * gp_ref[...]
            xm_ref[rs, :] = xm
            h_ref[rs, :] = (xm * _rstd(xm) * gn_ref[...]).astype(BF16)

    row = pl.BlockSpec((tm, D_MODEL), lambda i: (i, 0))
    gain = pl.BlockSpec((None, 1, D_MODEL), lambda i: (l, 0, 0))
    return _call(
        body, name="out_proj_fwd", grid=(s // tm,),
        in_specs=[pl.BlockSpec((tm, half), lambda i: (i, 0)), pl.BlockSpec((tm, half), lambda i: (i, 0)),
                  pl.BlockSpec((D_MODEL, D_MODEL), lambda i: (0, 0)), row, gain, gain],
        out_specs=[row, row, row],
        out_shape=[jax.ShapeDtypeStruct((s, D_MODEL), F32), jax.ShapeDtypeStruct((s, D_MODEL), F32),
                   jax.ShapeDtypeStruct((s, D_MODEL), BF16)],
        args=[ync, yna, wout, x, g_post3, g_next3])


def _ffn_in_fwd(h2, wfin4, s, tm, comm=None):
    def body(h_ref, wg_ref, wu_ref, gu_ref, act_ref):
        h = h_ref[...]
        gate = _dot_nt(h, wg_ref[...])
        up = _dot_nt(h, wu_ref[...])
        gu_ref[0] = gate.astype(BF16)
        gu_ref[1] = up.astype(BF16)
        act_ref[...] = (gate * (1.0 / (1.0 + jnp.exp(-gate))) * up).astype(BF16)

    return _call(
        body, name="ffn_in_fwd", grid=(2, s // tm),
        in_specs=[pl.BlockSpec((tm, D_MODEL), lambda b, i: (i, 0)),
                  pl.BlockSpec((None, FF_PAIR, D_MODEL), lambda b, i: (b, 0, 0)),
                  pl.BlockSpec((None, FF_PAIR, D_MODEL), lambda b, i: (2 + b, 0, 0))],
        out_specs=[pl.BlockSpec((2, tm, FF_PAIR), lambda b, i: (0, i, b)),
                   pl.BlockSpec((tm, FF_PAIR), lambda b, i: (i, b))],
        out_shape=[jax.ShapeDtypeStruct((2, s, D_FF), BF16), jax.ShapeDtypeStruct((s, D_FF), BF16)],
        args=[h2, wfin4, wfin4], comm=comm)


def _ffn_out_fwd(act, wfo, xm, g_post3, g_next3, l, l_next, s, tm, comm=None):
    def body(a_ref, w_ref, x_ref, gp_ref, gn_ref, f_ref, xo_ref, h_ref):
        for rs in _row_subtiles(tm, SUB_ROWS):
            f = _dot(a_ref[rs, :], w_ref[...])
            f_ref[rs, :] = f
            xo = x_ref[rs, :] + f * _rstd(f) * gp_ref[...]
            xo_ref[rs, :] = xo
            h_ref[rs, :] = (xo * _rstd(xo) * gn_ref[...]).astype(BF16)

    row = pl.BlockSpec((tm, D_MODEL), lambda i: (i, 0))
    return _call(
        body, name="ffn_out_fwd", grid=(s // tm,),
        in_specs=[pl.BlockSpec((tm, D_FF), lambda i: (i, 0)),
                  pl.BlockSpec((D_FF, D_MODEL), lambda i: (0, 0), pipeline_mode=pl.Buffered(1)), row,
                  pl.BlockSpec((None, 1, D_MODEL), lambda i: (l, 0, 0)),
                  pl.BlockSpec((None, 1, D_MODEL), lambda i: (l_next, 0, 0))],
        out_specs=[row, row, row],
        out_shape=[jax.ShapeDtypeStruct((s, D_MODEL), F32), jax.ShapeDtypeStruct((s, D_MODEL), F32),
                   jax.ShapeDtypeStruct((s, D_MODEL), BF16)],
        args=[act, wfo, xm, g_post3, g_next3], comm=comm)


def _loss_grad(xf, target, f, g3, l, s, tm):
    def body(x_ref, t_ref, f_ref, g_ref, dx_ref, sq_ref, df_ref, dg_ref):
        i = pl.program_id(0)
        err = x_ref[...] - t_ref[...]
        dx = err * (1.0 / D_MODEL)
        dx_ref[...] = dx
        df, dyn = _norm_bwd_rows(f_ref[...], g_ref[...], dx)
        df_ref[...] = df.astype(BF16)
        _accum_cols(dg_ref, dyn, i == 0)
        cs = jnp.sum(err * err, axis=0, keepdims=True)
        part = cs[:, 0:128]
        for k in range(1, D_MODEL // 128):
            part = part + cs[:, 128 * k:128 * (k + 1)]

        @pl.when(i == 0)
        def _():
            sq_ref[...] = jnp.zeros(sq_ref.shape, F32)

        sq_ref[0:1, :] += part

    row = pl.BlockSpec((tm, D_MODEL), lambda i: (i, 0))
    return _call(
        body, name="loss_grad", grid=(s // tm,),
        in_specs=[row, row, row, pl.BlockSpec((None, 1, D_MODEL), lambda i: (l, 0, 0))],
        out_specs=[row, pl.BlockSpec((8, 128), lambda i: (0, 0)), row, pl.BlockSpec((8, D_MODEL), lambda i: (0, 0))],
        out_shape=[jax.ShapeDtypeStruct((s, D_MODEL), F32), jax.ShapeDtypeStruct((8, 128), F32),
                   jax.ShapeDtypeStruct((s, D_MODEL), BF16), jax.ShapeDtypeStruct((8, D_MODEL), F32)],
        args=[xf, target, f, g3])


def _norm_bwd_rows(v, g, dy):
    r = _rstd(v)
    vn = v * r
    gd = dy * g
    dv = r * (gd - vn * jnp.mean(vn * gd, axis=-1, keepdims=True))
    return dv, dy * vn


def _zero_first(refs, first):
    @pl.when(first)
    def _():
        for ref in refs:
            ref[...] = jnp.zeros(ref.shape, F32)


def _add_cols(ref, val):
    ref[0:1, :] += jnp.sum(val, axis=0, keepdims=True)


def _accum_cols(ref, val, first):
    _zero_first((ref,), first)
    _add_cols(ref, val)


def _row_subtiles(rows, sub):
    sub = min(sub, rows)
    return [slice(r, r + sub) for r in range(0, rows, sub)]


def _ffn_out_bwd(df, wfo, gu, s, tm, comm=None):
    def body(df_ref, w_ref, gu_ref, dgu_ref):
        da = _dot_nt(df_ref[...], w_ref[...])
        g = gu_ref[0].astype(F32)
        u = gu_ref[1].astype(F32)
        sg = 1.0 / (1.0 + jnp.exp(-g))
        dgu_ref[0] = (da * u * (sg * (1.0 + g * (1.0 - sg)))).astype(BF16)
        dgu_ref[1] = (da * (g * sg)).astype(BF16)

    blk = pl.BlockSpec((2, tm, FF_PAIR), lambda b, i: (0, i, b))
    return _call(
        body, name="ffn_out_bwd", grid=(2, s // tm),
        in_specs=[pl.BlockSpec((tm, D_MODEL), lambda b, i: (i, 0)),
                  pl.BlockSpec((FF_PAIR, D_MODEL), lambda b, i: (b, 0)), blk],
        out_specs=[blk], out_shape=[jax.ShapeDtypeStruct((2, s, D_FF), BF16)],
        args=[df, wfo, gu], comm=comm)


def _dw_ffn_out(act, df, s):
    def body(a_ref, b_ref, o_ref):
        o_ref[...] = _dot_tn(a_ref[...], b_ref[...]).astype(BF16)

    out, = _call(
        body, name="dw_ffn_out", grid=(2,),
        in_specs=[pl.BlockSpec((s, FF_PAIR), lambda n: (0, n)),
                  pl.BlockSpec((s, D_MODEL), lambda n: (0, 0), pipeline_mode=pl.Buffered(1))],
        out_specs=[pl.BlockSpec((FF_PAIR, D_MODEL), lambda n: (n, 0))],
        out_shape=[jax.ShapeDtypeStruct((D_FF, D_MODEL), BF16)], args=[act, df])
    return out


def _dw_ffn_in(h2, dgu, s):
    def body(a_ref, b_ref, o_ref):
        o_ref[...] = _dot_tn(b_ref[...], a_ref[...]).astype(BF16)

    out, = _call(
        body, name="dw_ffn_in", grid=(4,),
        in_specs=[pl.BlockSpec((s, D_MODEL), lambda n: (0, 0), pipeline_mode=pl.Buffered(1)),
                  pl.BlockSpec((None, s, FF_PAIR), lambda n: (n // 2, 0, n % 2))],
        out_specs=[pl.BlockSpec((None, FF_PAIR, D_MODEL), lambda n: (n, 0, 0))],
        out_shape=[jax.ShapeDtypeStruct((4, FF_PAIR, D_MODEL), BF16)], args=[h2, dgu])
    return out


def _ffn_in_bwd(dgu, wfin, xm, g_pre3, dres, z, g_post3, l, s, tm, comm=None):
    def body(d_ref, w_ref, xm_ref, gp_ref, dres_ref, z_ref, gq_ref, dxm_ref, dz_ref, dgp_ref, dgq_ref):
        _zero_first((dgp_ref, dgq_ref), pl.program_id(0) == 0)
        for rs in _row_subtiles(tm, SUB_ROWS):
            dh = _dot(d_ref[0, rs, :], w_ref[0:D_FF, :]) + _dot(d_ref[1, rs, :], w_ref[D_FF:2 * D_FF, :])
            dx, dyn = _norm_bwd_rows(xm_ref[rs, :], gp_ref[...], dh)
            dxm = dres_ref[rs, :] + dx
            dxm_ref[rs, :] = dxm
            _add_cols(dgp_ref, dyn)
            dz, dyn2 = _norm_bwd_rows(z_ref[rs, :], gq_ref[...], dxm)
            dz_ref[rs, :] = dz.astype(BF16)
            _add_cols(dgq_ref, dyn2)

    row = pl.BlockSpec((tm, D_MODEL), lambda i: (i, 0))
    gain = pl.BlockSpec((None, 1, D_MODEL), lambda i: (l, 0, 0))
    dgs = pl.BlockSpec((8, D_MODEL), lambda i: (0, 0))
    return _call(
        body, name="ffn_in_bwd", grid=(s // tm,),
        in_specs=[pl.BlockSpec((2, tm, D_FF), lambda i: (0, i, 0)),
                  pl.BlockSpec((2 * D_FF, D_MODEL), lambda i: (0, 0), pipeline_mode=pl.Buffered(1)),
                  row, gain, row, row, gain],
        out_specs=[row, row, dgs, dgs],
        out_shape=[jax.ShapeDtypeStruct((s, D_MODEL), F32), jax.ShapeDtypeStruct((s, D_MODEL), BF16),
                   jax.ShapeDtypeStruct((8, D_MODEL), F32), jax.ShapeDtypeStruct((8, D_MODEL), F32)],
        args=[dgu, wfin, xm, g_pre3, dres, z, g_post3], comm=comm)


def _dw_out(ync, yna, dz, s):
    half = D_MODEL // 2

    def body(a1_ref, a2_ref, b_ref, o_ref):
        b = b_ref[...]
        o_ref[0:half, :] = _dot_tn(a1_ref[...], b).astype(BF16)
        o_ref[half:D_MODEL, :] = _dot_tn(a2_ref[...], b).astype(BF16)

    out, = _call(
        body, name="dw_out", grid=(2,),
        in_specs=[pl.BlockSpec((s, half), lambda n: (0, 0)), pl.BlockSpec((s, half), lambda n: (0, 0)),
                  pl.BlockSpec((s, half), lambda n: (0, n))],
        out_specs=[pl.BlockSpec((D_MODEL, half), lambda n: (0, n))],
        out_shape=[jax.ShapeDtypeStruct((D_MODEL, D_MODEL), BF16)], args=[ync, yna, dz])
    return out


def _out_proj_bwd(dz, wout, o, g3, l, s, tm):
    def body(dz_ref, w_ref, o_ref, g_ref, dyc_ref, do_ref, dg_ref):
        gmat = _group_matrix()
        _zero_first((dg_ref,), pl.program_id(0) == 0)
        for rs in _row_subtiles(tm, SUB_ROWS):
            dy = _dot_nt(dz_ref[rs, :], w_ref[...])
            dyc_ref[rs, :] = dy[:, 0:CONV_WIDTH]
            for j in range(ATTN_WIDTH // 128):
                c0 = 128 * j
                ov = o_ref[rs, c0:c0 + 128]
                dyn = dy[:, CONV_WIDTH + c0:CONV_WIDTH + c0 + 128]
                r = lax.rsqrt(_group_mean(ov * ov, gmat) + EPS)
                on = ov * r
                gd = dyn * g_ref[:, c0:c0 + 128]
                do_ref[rs, c0:c0 + 128] = r * (gd - on * _group_mean(on * gd, gmat))
                dg_ref[0:1, c0:c0 + 128] += jnp.sum(dyn * on, axis=0, keepdims=True)

    halfrow = pl.BlockSpec((tm, ATTN_WIDTH), lambda i: (i, 0))
    return _call(
        body, name="out_proj_bwd", grid=(s // tm,),
        in_specs=[pl.BlockSpec((tm, D_MODEL), lambda i: (i, 0)),
                  pl.BlockSpec((D_MODEL, D_MODEL), lambda i: (0, 0)), halfrow,
                  pl.BlockSpec((None, 1, ATTN_WIDTH), lambda i: (l, 0, 0))],
        out_specs=[halfrow, halfrow, pl.BlockSpec((8, ATTN_WIDTH), lambda i: (0, 0))],
        out_shape=[jax.ShapeDtypeStruct((s, CONV_WIDTH), F32), jax.ShapeDtypeStruct((s, ATTN_WIDTH), F32),
                   jax.ShapeDtypeStruct((8, ATTN_WIDTH), F32)],
        args=[dz, wout, o, g3])


def _conv_bwd(pc, dyc, wc, g3, dq, dk, dv, l, s, tr):
    hb = tr // 8
    nt = s // tr
    ext = tr + 16
    last_hb = s // 8 - 1

    def body(pc_ref, prev_ref, next_ref, dy_ref, dyn_ref, wc_ref, g_ref, dq_ref, dk_ref, dv_ref,
             dpc_ref, dw_ref, dg_ref):
        i = pl.program_id(0)
        for part, ref in enumerate((dq_ref, dk_ref, dv_ref)):
            c = 3 * CONV_WIDTH + ATTN_WIDTH * part
            dpc_ref[:, c:c + ATTN_WIDTH] = ref[...]
        gmat = _group_matrix()
        row = lax.broadcasted_iota(jnp.int32, (ext, 128), 0) + (i * tr - 8)
        inside = jnp.where(row >= 0, jnp.where(row < s, 1, 0), 0) == 1

        @pl.when(i == 0)
        def _():
            dw_ref[...] = jnp.zeros(dw_ref.shape, F32)
            dg_ref[...] = jnp.zeros(dg_ref.shape, F32)

        def extend(ref_prev, ref_mid, ref_next, c):
            parts = [ref_prev[:, c:c + 128] if ref_prev is not None else jnp.zeros((8, 128), F32),
                     ref_mid[:, c:c + 128], ref_next[:, c:c + 128]]
            return jnp.concatenate(parts, axis=0)

        for j in range(CONV_WIDTH // 128):
            c0, c1, c2 = 128 * j, CONV_WIDTH + 128 * j, 2 * CONV_WIDTH + 128 * j
            hc = extend(prev_ref, pc_ref, next_ref, c0)
            bg = extend(prev_ref, pc_ref, next_ref, c1)
            cg = extend(prev_ref, pc_ref, next_ref, c2)
            dyn = extend(None, dy_ref, dyn_ref, c0)
            w0, w1, w2 = (wc_ref[0:1, c0:c0 + 128], wc_ref[1:2, c0:c0 + 128], wc_ref[2:3, c0:c0 + 128])
            gain = g_ref[:, c0:c0 + 128]
            u = jnp.where(inside, cg * hc, 0.0)
            u1 = pltpu.roll(u, 1, 0)
            u2 = pltpu.roll(u, 2, 0)
            out = u2 * w0 + u1 * w1 + u * w2
            yc = bg * out
            r = lax.rsqrt(_group_mean(yc * yc, gmat) + EPS)
            ycn = yc * r
            gd = dyn * gain
            dyc = r * (gd - ycn * _group_mean(ycn * gd, gmat))
            dout = jnp.where(inside, dyc * bg, 0.0)
            du = dout * w2 + pltpu.roll(dout, ext - 1, 0) * w1 + pltpu.roll(dout, ext - 2, 0) * w0
            sl = slice(8, 8 + tr)
            dpc_ref[:, c0:c0 + 128] = (du[sl] * cg[sl]).astype(BF16)
            dpc_ref[:, c1:c1 + 128] = (dyc[sl] * out[sl]).astype(BF16)
            dpc_ref[:, c2:c2 + 128] = (du[sl] * hc[sl]).astype(BF16)
            dw_ref[0:1, c0:c0 + 128] += jnp.sum(dout[sl] * u2[sl], axis=0, keepdims=True)
            dw_ref[1:2, c0:c0 + 128] += jnp.sum(dout[sl] * u1[sl], axis=0, keepdims=True)
            dw_ref[2:3, c0:c0 + 128] += jnp.sum(dout[sl] * u[sl], axis=0, keepdims=True)
            dg_ref[0:1, c0:c0 + 128] += jnp.sum(dyn[sl] * ycn[sl], axis=0, keepdims=True)

    wide = 3 * CONV_WIDTH
    return _call(
        body, name="conv_bwd", grid=(nt,),
        in_specs=[pl.BlockSpec((tr, wide), lambda i: (i, 0)),
                  pl.BlockSpec((8, wide), lambda i: (jnp.maximum(i * hb - 1, 0), 0)),
                  pl.BlockSpec((8, wide), lambda i: (jnp.minimum((i + 1) * hb, last_hb), 0)),
                  pl.BlockSpec((tr, CONV_WIDTH), lambda i: (i, 0)),
                  pl.BlockSpec((8, CONV_WIDTH), lambda i: (jnp.minimum((i + 1) * hb, last_hb), 0)),
                  pl.BlockSpec((None, 8, CONV_WIDTH), lambda i: (l, 0, 0)),
                  pl.BlockSpec((None, 1, CONV_WIDTH), lambda i: (l, 0, 0)),
                  pl.BlockSpec((tr, ATTN_WIDTH), lambda i: (i, 0)),
                  pl.BlockSpec((tr, ATTN_WIDTH), lambda i: (i, 0)),
                  pl.BlockSpec((tr, ATTN_WIDTH), lambda i: (i, 0))],
        out_specs=[pl.BlockSpec((tr, PROJ_WIDTH), lambda i: (i, 0)),
                   pl.BlockSpec((8, CONV_WIDTH), lambda i: (0, 0)),
                   pl.BlockSpec((8, CONV_WIDTH), lambda i: (0, 0))],
        out_shape=[jax.ShapeDtypeStruct((s, PROJ_WIDTH), BF16), jax.ShapeDtypeStruct((8, CONV_WIDTH), F32),
                   jax.ShapeDtypeStruct((8, CONV_WIDTH), F32)],
        args=[pc, pc, pc, dyc, dyc, wc, g3, dq, dk, dv])


def _attn_bwd(qkvp, biasm, o, lse, do, l, s, pad, comm=None):
    nb = s // Q_BLOCK
    qb0 = pad // Q_BLOCK
    scale = HEAD_DIM ** -0.5
    wide = 128 * ATTN_PAIRS

    def body(q_ref, k_ref, v_ref, b_ref, o_ref, lse_ref, do_ref,
             dq_ref, dk_ref, dv_ref, ds_ref, dk_acc, dv_acc):
        blk = pl.program_id(1)

        @pl.when(blk == 0)
        def _():
            dk_acc[...] = jnp.zeros(dk_acc.shape, F32)
            dv_acc[...] = jnp.zeros(dv_acc.shape, F32)
            ds_ref[...] = jnp.zeros(ds_ref.shape, F32)

        koff = pl.multiple_of(blk * Q_BLOCK + (pad - LEFT), Q_BLOCK)
        lane = lax.broadcasted_iota(jnp.int32, (1, 128), 1)
        kpos = lax.broadcasted_iota(jnp.int32, (1, K_BAND), 1) + (blk * Q_BLOCK - LEFT)
        kmask = jnp.where(kpos >= 0, 0.0, NEG_INF)
        for pr in range(ATTN_PAIRS):
            ls = slice(128 * pr, 128 * (pr + 1))
            q = q_ref[:, ls]
            kb = k_ref[pl.ds(koff, K_BAND), ls]
            vb = v_ref[pl.ds(koff, K_BAND), ls]
            dov = do_ref[:, ls]
            lse_v = lse_ref[:, ls]
            prod = dov * o_ref[:, ls]
            dq_parts = []
            dk_new = jnp.zeros((K_BAND, 128), F32)
            dv_new = jnp.zeros((K_BAND, 128), F32)
            for hh in range(2):
                in_head = (lane >> 6) == hh
                qm = jnp.where(in_head, q, jnp.zeros_like(q)) * jnp.asarray(scale, BF16)
                dom = jnp.where(in_head, dov, 0.0).astype(BF16)
                delta = jnp.sum(jnp.where(in_head, prod, 0.0), axis=1, keepdims=True)
                lse_h = lse_v[:, HEAD_DIM * hh:HEAD_DIM * hh + 1]
                sc = _dot_nt(qm, kb) + b_ref[2 * pr + hh] + kmask
                p = jnp.exp(sc - lse_h)
                dp = _dot_nt(dom, vb)
                ds = p * (dp - delta)
                ds_ref[2 * pr + hh] += ds
                dsb = ds.astype(BF16)
                dq_parts.append(_dot(dsb, kb) * scale)
                dk_new = dk_new + _dot_tn(dsb, qm)
                dv_new = dv_new + _dot_tn(p.astype(BF16), dom)
            dq_ref[:, ls] = jnp.where(lane < HEAD_DIM, dq_parts[0], dq_parts[1]).astype(BF16)
            dk_acc[pl.ds(koff, K_BAND), ls] += dk_new
            dv_acc[pl.ds(koff, K_BAND), ls] += dv_new

        @pl.when(blk == nb - 1)
        def _():
            dk_ref[...] = dk_acc[pad:pad + s, :].astype(BF16)
            dv_ref[...] = dv_acc[pad:pad + s, :].astype(BF16)

    n_grp = ATTN_WIDTH // wide
    qblk = pl.BlockSpec((Q_BLOCK, wide), lambda p, b: (b, p))
    col = pl.BlockSpec((s, wide), lambda p, b: (0, p))
    shp = jax.ShapeDtypeStruct((s, ATTN_WIDTH), BF16)
    return _call(
        body, name="attn_bwd", grid=(n_grp, nb),
        in_specs=[pl.BlockSpec((Q_BLOCK, wide), lambda p, b: (qb0 + b, p)),
                  pl.BlockSpec((s + pad, wide), lambda p, b: (0, n_grp + p)),
                  pl.BlockSpec((s + pad, wide), lambda p, b: (0, 2 * n_grp + p)),
                  pl.BlockSpec((None, 2 * ATTN_PAIRS, Q_BLOCK, K_BAND), lambda p, b: (l, p, 0, 0)),
                  qblk, qblk, qblk],
        out_specs=[qblk, col, col, pl.BlockSpec((2 * ATTN_PAIRS, Q_BLOCK, K_BAND), lambda p, b: (p, 0, 0))],
        out_shape=[shp, shp, shp, jax.ShapeDtypeStruct((N_HEADS, Q_BLOCK, K_BAND), F32)],
        scratch=[pltpu.VMEM((s + pad, wide), F32), pltpu.VMEM((s + pad, wide), F32)],
        args=[qkvp, qkvp, qkvp, biasm, o, lse, do], comm=comm)


def _dw_in(h, dproj, s):
    def body(a_ref, b_ref, o_ref):
        acc = _dot_tn(a_ref[...], b_ref[...])
        o_ref[0] = acc[:, 0:PROJ_SHARD].astype(BF16)
        o_ref[1] = acc[:, PROJ_SHARD:2 * PROJ_SHARD].astype(BF16)

    out, = _call(
        body, name="dw_in", grid=(4,),
        in_specs=[pl.BlockSpec((s, D_MODEL), lambda n: (0, 0)),
                  pl.BlockSpec((s, 2 * PROJ_SHARD), lambda n: (0, n))],
        out_specs=[pl.BlockSpec((2, D_MODEL, PROJ_SHARD), lambda n: (n, 0, 0))],
        out_shape=[jax.ShapeDtypeStruct((N_DEV, D_MODEL, PROJ_SHARD), BF16)], args=[h, dproj])
    return out


def _in_proj_bwd(dproj, win, x, g3, dres, l, s, tm, f_prev=None, g_post3=None, comm=None):
    chain = f_prev is not None

    def body(d_ref, w_ref, x_ref, g_ref, dres_ref, *rest):
        if chain:
            f_ref, gq_ref, dx_ref, dg_ref, df_ref, dgq_ref = rest
            _zero_first((dg_ref, dgq_ref), pl.program_id(0) == 0)
        else:
            dx_ref, dg_ref = rest
            _zero_first((dg_ref,), pl.program_id(0) == 0)
        w = jnp.concatenate([w_ref[j] for j in range(N_DEV)], axis=1)
        for rs in _row_subtiles(tm, SUB_ROWS):
            dh = _dot_nt(d_ref[rs, :], w)
            dx, dyn = _norm_bwd_rows(x_ref[rs, :], g_ref[...], dh)
            dx = dres_ref[rs, :] + dx
            dx_ref[rs, :] = dx
            _add_cols(dg_ref, dyn)
            if chain:
                df, dyn2 = _norm_bwd_rows(f_ref[rs, :], gq_ref[...], dx)
                df_ref[rs, :] = df.astype(BF16)
                _add_cols(dgq_ref, dyn2)

    row = pl.BlockSpec((tm, D_MODEL), lambda i: (i, 0))
    dgs = pl.BlockSpec((8, D_MODEL), lambda i: (0, 0))
    in_specs = [pl.BlockSpec((tm, PROJ_WIDTH), lambda i: (i, 0)),
                pl.BlockSpec((N_DEV, D_MODEL, PROJ_SHARD), lambda i: (0, 0, 0), pipeline_mode=pl.Buffered(1)),
                row, pl.BlockSpec((None, 1, D_MODEL), lambda i: (l, 0, 0)), row]
    out_specs = [row, dgs]
    out_shape = [jax.ShapeDtypeStruct((s, D_MODEL), F32), jax.ShapeDtypeStruct((8, D_MODEL), F32)]
    args = [dproj, win, x, g3, dres]
    if chain:
        in_specs += [row, pl.BlockSpec((None, 1, D_MODEL), lambda i: (l - 1, 0, 0))]
        out_specs += [row, dgs]
        out_shape += [jax.ShapeDtypeStruct((s, D_MODEL), BF16), jax.ShapeDtypeStruct((8, D_MODEL), F32)]
        args += [f_prev, g_post3]
    return _call(body, name="in_proj_bwd", grid=(s // tm,), in_specs=in_specs, out_specs=out_specs,
                 out_shape=out_shape, args=args, comm=comm)


def _adamw(name, w, m, v, lands, owns=None, me=None):
    groups, rows, cols = w.shape
    assert len(lands) == groups
    n_part = lands[0].shape[0]
    tr = _row_tile(rows, tuple(c for c in (512, 352, 256, 176, 128, 64, 32, 16, 8) if c * cols <= 256 * 1024))
    c1 = 1.0 - ADAM_B1 ** ADAM_STEP
    c2 = 1.0 - ADAM_B2 ** ADAM_STEP
    n_own = groups if owns is not None else 0

    def body(*refs):
        if n_own:
            me_ref, refs = refs[0], refs[1:]
        w_ref, m_ref, v_ref = refs[:3]
        land_refs = refs[3:3 + groups]
        own_refs = refs[3 + groups:3 + groups + n_own]
        g_ref, d_ref, nm_ref, nv_ref = refs[3 + groups + n_own:]
        grp = pl.program_id(0)
        for gi in range(groups):
            @pl.when(grp == gi)
            def _():
                l_ref = land_refs[gi]
                g = None
                for p in range(n_part):
                    part = l_ref[p].astype(F32)
                    if n_own:
                        part = jnp.where(me_ref[0] == p, own_refs[gi][...].astype(F32), part)
                    g = part if g is None else g + part
                g_ref[...] = g
                m1 = ADAM_B1 * m_ref[...] + (1.0 - ADAM_B1) * g
                v1 = ADAM_B2 * v_ref[...] + (1.0 - ADAM_B2) * (g * g)
                nm_ref[...] = m1
                nv_ref[...] = v1
                d_ref[...] = -ADAM_LR * ((m1 / c1) / (jnp.sqrt(v1 / c2) + ADAM_EPS) + ADAM_WD * w_ref[...])

    blk = pl.BlockSpec((None, tr, cols), lambda g, i, *_: (g, i, 0))
    shp = jax.ShapeDtypeStruct((groups, rows, cols), F32)

    def land_spec(gi):
        return pl.BlockSpec((n_part, tr, cols), lambda g, i, *_: (0, jnp.where(g == gi, i, 0), 0))

    def own_spec(gi):
        if owns[gi].ndim == 3:
            return pl.BlockSpec((None, tr, cols), lambda g, i, me_ref: (me_ref[0], jnp.where(g == gi, i, 0), 0))
        return pl.BlockSpec((tr, cols), lambda g, i, me_ref: (jnp.where(g == gi, i, 0), 0))

    in_specs = [blk, blk, blk] + [land_spec(gi) for gi in range(groups)] + [own_spec(gi) for gi in range(n_own)]
    args = [w, m, v] + list(lands) + (list(owns) if n_own else [])
    if not n_own:
        return _call(body, name=name, grid=(groups, rows // tr), in_specs=in_specs,
                     out_specs=[blk, blk, blk, blk], out_shape=[shp, shp, shp, shp], args=args)
    return pl.pallas_call(
        body, name=name,
        grid_spec=pltpu.PrefetchScalarGridSpec(
            num_scalar_prefetch=1, grid=(groups, rows // tr), in_specs=in_specs, out_specs=[blk, blk, blk, blk]),
        out_shape=[shp, shp, shp, shp],
        compiler_params=pltpu.CompilerParams(dimension_semantics=("arbitrary", "arbitrary"),
                                             vmem_limit_bytes=VMEM_LIMIT),
    )(me, *args)


def _pack_small(rel, gco, gao, gpm, gqm, gpf, gqf):
    n_layers = rel.shape[0]
    relp = jnp.pad(rel, ((0, 0), (0, 0), (0, REL_PAD - rel.shape[2])))
    parts = [relp.reshape(n_layers * N_HEADS * REL_PAD // 128, 128)]
    parts += [a.reshape(-1, 128) for a in (gco, gao, gpm, gqm, gpf, gqf)]
    return jnp.concatenate(parts, axis=0)


def _unpack_small(p, n_layers):
    n_rel = n_layers * N_HEADS * REL_PAD // 128
    rel = p[:n_rel].reshape(n_layers, N_HEADS, REL_PAD)[:, :, :2 * REL_CLIP + 1]
    outs = [rel]
    r0 = n_rel
    for width in (CONV_WIDTH, ATTN_WIDTH, D_MODEL, D_MODEL, D_MODEL, D_MODEL):
        nr = n_layers * width // 128
        outs.append(p[r0:r0 + nr].reshape(n_layers, width))
        r0 += nr
    return outs


def kernel(x, w_in, w_conv, rel_bias, g_conv_out, g_attn_out, w_out, g_pre_mix, g_post_mix, g_pre_ffn, g_post_ffn, w_ffn_in, w_ffn_out, loss_target, m_w_in, m_w_conv, m_rel_bias, m_g_conv_out, m_g_attn_out, m_w_out, m_g_pre_mix, m_g_post_mix, m_g_pre_ffn, m_g_post_ffn, m_w_ffn_in, m_w_ffn_out, v_w_in, v_w_conv, v_rel_bias, v_g_conv_out, v_g_attn_out, v_w_out, v_g_pre_mix, v_g_post_mix, v_g_pre_ffn, v_g_post_ffn, v_w_ffn_in, v_w_ffn_out):
    n_layers = w_in.shape[0]
    s = x.shape[1]
    assert x.shape == (1, s, D_MODEL) and s % 1024 == 0
    assert w_in.shape == (n_layers, D_MODEL, PROJ_SHARD) and w_ffn_in.shape == (n_layers, D_MODEL, FF_SHARD)
    tm = 512
    tq = 1024 if s >= 2048 else 512
    tf = min(1024, s)
    x0 = x.reshape(s, D_MODEL)
    target = loss_target.reshape(s, D_MODEL)
    dev = _dev_index(lax.axis_index("x"), lax.axis_index("y"), lax.axis_index("c"))

    wt_ffn_in, mt_ffn_in, vt_ffn_in = (jnp.transpose(a, (0, 2, 1)) for a in (w_ffn_in, m_w_ffn_in, v_w_ffn_in))
    local_w = [_cast_bf16(w_in, "cast_w_in"), _cast_bf16(w_out, "cast_w_out"),
               _cast_bf16(wt_ffn_in, "cast_w_ffn_in"), _cast_bf16(w_ffn_out, "cast_w_ffn_out")]
    wc_local = jnp.pad(jnp.transpose(w_conv, (0, 2, 1)).reshape(-1), (0, 1024 - n_layers * 3 * 64)).reshape(8, 128)
    biasm, win_next, wc_g = _bias_build(jnp.pad(rel_bias, ((0, 0), (0, 0), (0, REL_PAD - rel_bias.shape[2]))),
                                        comm=_Gather([(local_w[0], 0), (wc_local, None)]))
    weights = [None] * n_layers
    wc_full = wc_g.reshape(N_DEV, 1024)[:, :n_layers * 3 * 64].reshape(N_DEV, n_layers, 3, 64)
    wc_full = jnp.transpose(wc_full, (1, 2, 0, 3)).reshape(n_layers, 3, CONV_WIDTH)
    wc_full = jnp.pad(wc_full, ((0, 0), (0, 5), (0, 0)))

    g3 = {k: v.reshape(n_layers, 1, -1) for k, v in dict(
        conv=g_conv_out, attn=g_attn_out, pre_mix=g_pre_mix, post_mix=g_post_mix,
        pre_ffn=g_pre_ffn, post_ffn=g_post_ffn).items()}

    saved = []
    xl = x0
    h = _norm_cast(x0, g3["pre_mix"], 0, tm)
    for l in range(n_layers):
        win = win_next
        pc, wout = _in_proj(h, win, s, tq, 0, comm=_Gather([(local_w[1], l)]))
        qkvp, = _in_proj(h, win, s, tq, 1)
        ync = _conv_fwd(pc, wc_full, g3["conv"], l, s, tm)
        o, lse, yna, wfin = _attn_fwd(qkvp, biasm, g3["attn"], l, s, tq, comm=_Gather([(local_w[2], l)]))
        wout = wout.reshape(D_MODEL, D_MODEL)
        z, xm, h2 = _out_proj_fwd(ync, yna, wout, xl, g3["post_mix"], g3["pre_ffn"], l, s, tq)
        gu, act, wfout = _ffn_in_fwd(h2, wfin.reshape(4, FF_PAIR, D_MODEL), s, tf, comm=_Gather([(local_w[3], l)]))
        wfo = wfout.reshape(D_FF, D_MODEL)
        l_next = min(l + 1, n_layers - 1)
        f, xo, h_next, *got = _ffn_out_fwd(act, wfo, xm, g3["post_ffn"], g3["pre_mix"], l, l_next, s, tm,
                                           comm=_Gather([(local_w[0], l + 1)]) if l + 1 < n_layers else None)
        weights[l] = [win, wout, wfin.reshape(2 * D_FF, D_MODEL), wfo]
        win_next = got[0] if got else None
        saved.append(dict(x=xl, h=h, pc=pc, qkvp=qkvp, ync=ync, yna=yna, o=o, lse=lse, z=z, xm=xm,
                          h2=h2, gu=gu, act=act, f=f))
        xl, h = xo, h_next

    dx, sq, df, dg_post_ffn = _loss_grad(xl, target, saved[-1]["f"], g3["post_ffn"], n_layers - 1, s, tm)
    loss = lax.psum(jnp.sum(sq) * (0.5 / D_MODEL), ("x", "y", "c"))

    lands = dict(win=[None] * n_layers, wout=[None] * n_layers, wfin=[None] * n_layers, wfout=[None] * n_layers)
    small = {k: [None] * n_layers for k in ("gco", "gao", "gpm", "gqm", "gpf", "gqf", "wc")}
    d_rel = [None] * n_layers
    started = []

    def start(name, keys, l, arrays):
        items = [(a, False) for a in arrays]
        send_sems, recv_sems, srcs, zones, token = _exchange_start(name + "_start", items)
        started.append((name, keys, l, items, send_sems, recv_sems, srcs, zones))
        return token[0:1, 0:1].reshape(1, 1, 1)

    for l in reversed(range(n_layers)):
        sv = saved[l]
        win, wout, wfin, wfo = weights[l]
        small["gqf"][l] = dg_post_ffn[0]
        dgu, = _ffn_out_bwd(df, wfo, sv["gu"], s, tf)
        d_wfout = _dw_ffn_out(sv["act"], df, s).reshape(N_DEV, FFO_SHARD, D_MODEL)
        d_wfin = _dw_ffn_in(sv["h2"], dgu, s).reshape(N_DEV, FF_SHARD, D_MODEL)
        token = start(f"exchange_ffn{l}", ("wfout", "wfin"), l, [d_wfout, d_wfin])
        dxm, dz, dg_pre_ffn, dg_post_mix = _ffn_in_bwd(
            dgu, wfin, sv["xm"], g3["pre_ffn"] + token, dx, sv["z"], g3["post_mix"], l, s, tm)
        small["gpf"][l] = dg_pre_ffn[0]
        small["gqm"][l] = dg_post_mix[0]
        d_wout = _dw_out(sv["ync"], sv["yna"], dz, s).reshape(N_DEV, D_MODEL // N_DEV, D_MODEL)
        dyc, do, dg_attn = _out_proj_bwd(dz, wout, sv["o"], g3["attn"], l, s, tq)
        small["gao"][l] = dg_attn[0]
        dq, dk, dv, ds_sum = _attn_bwd(sv["qkvp"], biasm, sv["o"], sv["lse"], do, l, s, tq)
        d_rel[l] = _bias_bwd(ds_sum[None])
        dproj, dwc, dg_conv = _conv_bwd(sv["pc"], dyc, wc_full, g3["conv"], dq, dk, dv, l, s, tm)
        small["wc"][l] = dwc[0:3]
        small["gco"][l] = dg_conv[0]
        d_win = _dw_in(sv["h"], dproj, s)
        token = start(f"exchange_mix{l}", ("wout", "win"), l, [d_wout, d_win])
        if l > 0:
            dx, dg_pre_mix, df, dg_post_ffn = _in_proj_bwd(
                dproj, win, sv["x"], g3["pre_mix"] + token, dxm, l, s, tm, f_prev=saved[l - 1]["f"],
                g_post3=g3["post_ffn"])
        else:
            dx, dg_pre_mix = _in_proj_bwd(dproj, win, sv["x"], g3["pre_mix"] + token, dxm, l, s, tm)
        small["gpm"][l] = dg_pre_mix[0]
    grad_x = dx.reshape(1, s, D_MODEL)

    small_vec = jnp.concatenate(
        [_pack_small(jnp.concatenate(d_rel)[:, :, :2 * REL_CLIP + 1],
                     *[jnp.stack(small[k]) for k in ("gco", "gao", "gpm", "gqm", "gpf", "gqf")]),
         jnp.stack(small["wc"]).reshape(-1, 128)], axis=0)
    small_vec = jnp.pad(small_vec, ((0, (-small_vec.shape[0]) % 8), (0, 0)))
    small_items = [(small_vec, True)]
    small_sems = _exchange_start("exchange_small_start", small_items)

    owns = dict(win=[None] * n_layers, wout=[None] * n_layers, wfin=[None] * n_layers, wfout=[None] * n_layers)

    def wait(which, after):
        for name, keys, l, items, send_sems, recv_sems, srcs, zones in started:
            if keys == which:
                srcs, zones = _exchange_wait(name + "_wait", items, send_sems, recv_sems, srcs, zones, after)
                for key, src, zone in zip(keys, srcs, zones):
                    owns[key][l], lands[key][l] = src, zone

    me = dev.astype(jnp.int32).reshape(1)
    wait(("wfout", "wfin"), dx)
    r_fin = [jnp.transpose(t, (0, 2, 1)) for t in _adamw(
        "adamw_w_ffn_in", wt_ffn_in, mt_ffn_in, vt_ffn_in, lands["wfin"], owns["wfin"], me)]
    r_fout = _adamw("adamw_w_ffn_out", w_ffn_out, m_w_ffn_out, v_w_ffn_out, lands["wfout"], owns["wfout"], me)
    wait(("wout", "win"), r_fout[0])
    r_out = _adamw("adamw_w_out", w_out, m_w_out, v_w_out, lands["wout"], owns["wout"], me)
    r_in = _adamw("adamw_w_in", w_in, m_w_in, v_w_in, lands["win"], owns["win"], me)
    (small_own,), (land_small,) = _exchange_wait(
        "exchange_small_wait", small_items, small_sems[0], small_sems[1], small_sems[2], small_sems[3], r_in[0])

    n_rep = 64 * n_layers
    rep = _adamw(
        "adamw_replicated",
        _pack_small(rel_bias, g_conv_out, g_attn_out, g_pre_mix, g_post_mix, g_pre_ffn, g_post_ffn)[None],
        _pack_small(m_rel_bias, m_g_conv_out, m_g_attn_out, m_g_pre_mix, m_g_post_mix, m_g_pre_ffn, m_g_post_ffn)[None],
        _pack_small(v_rel_bias, v_g_conv_out, v_g_attn_out, v_g_pre_mix, v_g_post_mix, v_g_pre_ffn, v_g_post_ffn)[None],
        [land_small[:, :n_rep]], [small_own[:n_rep]], me)
    rep = [_unpack_small(t[0], n_layers) for t in rep]

    wc_rows = n_layers * 3 * CONV_WIDTH // 128
    zeros_wc = jnp.zeros((1, wc_rows, 128), F32)
    g_wc_full = _adamw("sum_w_conv", zeros_wc, zeros_wc, zeros_wc, [land_small[:, n_rep:n_rep + wc_rows]],
                       [small_own[n_rep:n_rep + wc_rows]], me)[0]
    g_wc_full = g_wc_full.reshape(n_layers, 3, CONV_WIDTH)
    g_wc = lax.dynamic_slice_in_dim(g_wc_full, dev * (CONV_WIDTH // N_DEV), CONV_WIDTH // N_DEV, axis=2)
    g_wc = jnp.transpose(g_wc, (0, 2, 1))

    def tiny(a):
        flat = a.reshape(-1)
        return jnp.pad(flat, (0, (-flat.shape[0]) % 1024)).reshape(1, -1, 128)

    r_wc = _adamw("adamw_w_conv", tiny(w_conv), tiny(m_w_conv), tiny(v_w_conv), [tiny(g_wc)])
    r_wc = [t.reshape(-1)[:w_conv.size].reshape(w_conv.shape) for t in r_wc]

    def leaf(kind):
        return [r_in[kind], r_wc[kind], rep[kind][0], rep[kind][1], rep[kind][2], r_out[kind],
                rep[kind][3], rep[kind][4], rep[kind][5], rep[kind][6], r_fin[kind], r_fout[kind]]

    return (loss, grad_x, *leaf(0), *leaf(1), *leaf(2), *leaf(3))
```

```python
import math

import jax
import jax.numpy as jnp
from jax import lax
from jax.experimental import pallas as pl
from jax.experimental.pallas import tpu as pltpu

F32 = jnp.float32
BF16 = jnp.bfloat16

D_MODEL = 1024
N_DEV = 8
CHUNK = 64
N_LEFT_CHUNKS = 8
CONV_WIDTH = 512
ATTN_WIDTH = 512
HEAD_DIM = 64
N_HEADS = 8
REL_CLIP = 128
REL_PAD = 384
PROJ_WIDTH = 3072
PROJ_SHARD = PROJ_WIDTH // N_DEV
D_FF = 2816
FF_SHARD = 2 * D_FF // N_DEV
FFO_SHARD = D_FF // N_DEV
FF_PAIR = 2 * FF_SHARD
_COL_SUBTILES = (slice(0, 768), slice(768, FF_PAIR))
EPS = 1e-6
NEG_INF = -1e30
ATTN_PAIRS = 2
Q_BLOCK = 4 * CHUNK
K_BAND = Q_BLOCK + N_LEFT_CHUNKS * CHUNK
LEFT = N_LEFT_CHUNKS * CHUNK
TOEP = 1024

ADAM_LR = 0.001
ADAM_B1 = 0.9
ADAM_B2 = 0.999
ADAM_EPS = 1e-08
ADAM_WD = 0.01
ADAM_STEP = 10

VMEM_LIMIT = 52 * 1024 * 1024
SUB_ROWS = 256
MESH = pl.DeviceIdType.MESH
ANY = pl.BlockSpec(memory_space=pl.ANY)

NT = (((1,), (1,)), ((), ()))
TN = (((0,), (0,)), ((), ()))


def _dot(a, b):
    return jnp.dot(a, b, preferred_element_type=F32)


def _dot_nt(a, b):
    return lax.dot_general(a, b, NT, preferred_element_type=F32)


def _dot_tn(a, b):
    return lax.dot_general(a, b, TN, preferred_element_type=F32)


def _rstd(v):
    return lax.rsqrt(jnp.mean(v * v, axis=-1, keepdims=True) + EPS)


def _group_matrix():
    r = lax.broadcasted_iota(jnp.int32, (128, 128), 0) >> 6
    c = lax.broadcasted_iota(jnp.int32, (128, 128), 1) >> 6
    return jnp.where(r == c, 1.0, 0.0).astype(BF16)


def _group_mean(v, gmat):
    hi = v.astype(BF16)
    lo = (v - hi.astype(F32)).astype(BF16)
    return (_dot(hi, gmat) + _dot(lo, gmat)) * (1.0 / HEAD_DIM)


def _split3(v):
    hi = v.astype(BF16)
    r1 = v - hi.astype(F32)
    mid = r1.astype(BF16)
    lo = (r1 - mid.astype(F32)).astype(BF16)
    return hi, mid, lo


def _row_tile(rows, cands=(1024, 512, 704, 256, 128, 64, 32, 16)):
    for c in cands:
        if rows % c == 0:
            return c
    return rows


def _dev_index(px, py, pc):
    return 4 * px + 2 * py + pc


def _when(cond):
    if cond is True:
        return lambda fn: fn()
    return pl.when(cond)


def _phases(grid):
    def phases():
        if not grid:
            return True, True, True
        lin = pl.program_id(0)
        for a in range(1, len(grid)):
            lin = lin * grid[a] + pl.program_id(a)
        total = math.prod(grid)
        return lin == 0, lin == (3 * total) // 4, lin == total - 1
    return phases


class _Gather:
    def __init__(self, items):
        self.items = items
        self.args = [a for a, _ in items]
        n = len(items)
        self.out_shape = [jax.ShapeDtypeStruct((N_DEV,) + (a.shape if lay is None else a.shape[1:]), a.dtype)
                          for a, lay in items]
        self.scratch = [pltpu.SemaphoreType.DMA((n, 7)), pltpu.SemaphoreType.DMA((n, 7)),
                        pltpu.SemaphoreType.DMA((n,))]

    def _ctx(self, ins, outs, sems):
        send_sems, recv_sems, local_sems = sems
        x, y, c = lax.axis_index("x"), lax.axis_index("y"), lax.axis_index("c")
        chips = [(1 - x, y), (x, 1 - y), (1 - x, 1 - y)]

        def src(k):
            lay = self.items[k][1]
            return ins[k] if lay is None else ins[k].at[lay]

        def copy(k, s, idx, to, from_src=False):
            return pltpu.make_async_remote_copy(
                src_ref=src(k) if from_src else outs[k].at[idx], dst_ref=outs[k].at[idx],
                send_sem=send_sems.at[k, s], recv_sem=recv_sems.at[k, s],
                device_id=to, device_id_type=MESH)

        def local(k):
            return pltpu.make_async_copy(src(k), outs[k].at[_dev_index(x, y, c)], local_sems.at[k])

        return x, y, c, chips, copy, local

    def start(self, ins, outs, sems, cond):
        n = len(self.items)

        @_when(cond)
        def _():
            x, y, c, chips, copy, local = self._ctx(ins, outs, sems)
            me = _dev_index(x, y, c)
            for k in range(n):
                local(k).start()
                copy(k, 0, me, (x, y, 1 - c), from_src=True).start()
                for j, chip in enumerate(chips):
                    copy(k, 1 + j, me, (chip[0], chip[1], c), from_src=True).start()

    def forward(self, ins, outs, sems, cond):
        n = len(self.items)

        @_when(cond)
        def _():
            x, y, c, chips, copy, local = self._ctx(ins, outs, sems)
            for j, chip in enumerate(chips):
                idx = _dev_index(chip[0], chip[1], c)
                for k in range(n):
                    copy(k, 1 + j, idx, (x, y, c)).wait_recv()
                    copy(k, 4 + j, idx, (x, y, 1 - c)).start()

    def finish(self, ins, outs, sems, cond):
        n = len(self.items)

        @_when(cond)
        def _():
            x, y, c, chips, copy, local = self._ctx(ins, outs, sems)
            me = _dev_index(x, y, c)
            for k in range(n):
                copy(k, 0, _dev_index(x, y, 1 - c), (x, y, c)).wait_recv()
            for j, chip in enumerate(chips):
                idx = _dev_index(chip[0], chip[1], 1 - c)
                for k in range(n):
                    copy(k, 4 + j, idx, (x, y, c)).wait_recv()
            for k in range(n):
                for s in range(4):
                    copy(k, s, me, (x, y, c), from_src=True).wait_send()
                for j, chip in enumerate(chips):
                    copy(k, 4 + j, _dev_index(chip[0], chip[1], c), (x, y, c)).wait_send()
                local(k).wait()


_PEER_FLIPS = [(0, 0, 1), (1, 0, 0), (0, 1, 0), (1, 1, 0), (1, 0, 1), (0, 1, 1), (1, 1, 1)]


def _call(body, *, name, grid, in_specs, out_specs, out_shape, args, scratch=(), comm=None):
    n_hi, n_ho, n_hs = len(args), len(out_shape), len(scratch)
    c_args = list(comm.args) if comm else []
    c_out = list(comm.out_shape) if comm else []
    c_scr = list(comm.scratch) if comm else []
    phases = _phases(grid)

    def kern(*refs):
        cuts = [n_hi, len(c_args), n_ho, len(c_out), n_hs, len(c_scr)]
        parts, pos = [], 0
        for n in cuts:
            parts.append(refs[pos:pos + n])
            pos += n
        hi, ci, ho, co, hs, cs = parts
        if comm:
            first, mid, last = phases()
            comm.start(ci, co, cs, first)
            comm.forward(ci, co, cs, mid)
        body(*hi, *ho, *hs)
        if comm:
            comm.finish(ci, co, cs, last)

    sem = ("arbitrary",) * len(grid) if grid else None
    return pl.pallas_call(
        kern, name=name, grid=grid,
        in_specs=list(in_specs) + [ANY] * len(c_args),
        out_specs=list(out_specs) + [ANY] * len(c_out),
        out_shape=list(out_shape) + c_out,
        scratch_shapes=list(scratch) + c_scr,
        compiler_params=pltpu.CompilerParams(dimension_semantics=sem, vmem_limit_bytes=VMEM_LIMIT),
    )(*args, *c_args)


def _comm_only(name, comm):
    return _call(lambda: None, name=name, grid=(), in_specs=[], out_specs=[], out_shape=[], args=[], comm=comm)


HBM_SPEC = pl.BlockSpec(memory_space=pltpu.HBM)
SEM_SPEC = pl.BlockSpec(memory_space=pltpu.SEMAPHORE)
SIDE_EFFECT = pltpu.SideEffectType.DATAFLOW_SIDE_EFFECTING


def _exchange_peer(x, y, c, s):
    fx, fy, fc = _PEER_FLIPS[s]
    return x ^ fx, y ^ fy, c ^ fc


def _exchange_start(name, items):
    n = len(items)
    srcs = [pltpu.with_memory_space_constraint(a, pltpu.HBM) for a, _ in items]
    land_shapes = [(N_DEV,) + (a.shape if whole else a.shape[1:]) for a, whole in items]
    lands = [pltpu.with_memory_space_constraint(lax.empty(shp, a.dtype), pltpu.HBM)
             for shp, (a, _) in zip(land_shapes, items)]

    n_sem = 7 * n

    def body(*refs):
        src_refs, land_refs = refs[:n], refs[n:2 * n]
        send_sems = refs[2 * n:2 * n + n_sem]
        recv_sems = refs[2 * n + n_sem:2 * n + 2 * n_sem]
        token = refs[-1]
        x, y, c = lax.axis_index("x"), lax.axis_index("y"), lax.axis_index("c")
        me = _dev_index(x, y, c)
        for s in range(7):
            px, py, pc = _exchange_peer(x, y, c, s)
            for k in range(n):
                src = src_refs[k] if items[k][1] else src_refs[k].at[_dev_index(px, py, pc)]
                pltpu.make_async_remote_copy(
                    src_ref=src, dst_ref=land_refs[k].at[me],
                    send_sem=send_sems[7 * k + s], recv_sem=recv_sems[7 * k + s],
                    device_id=(px, py, pc), device_id_type=MESH).start()
        token[...] = jnp.zeros(token.shape, token.dtype)

    outs = pl.pallas_call(
        body, name=name,
        out_shape=(*[pltpu.SemaphoreType.DMA(())] * (2 * n_sem),
                   *[pltpu.HBM(a.shape, a.dtype) for a in srcs],
                   *[pltpu.HBM(shp, a.dtype) for shp, a in zip(land_shapes, srcs)],
                   jax.ShapeDtypeStruct((8, 128), F32)),
        in_specs=[HBM_SPEC] * (2 * n),
        out_specs=(*[SEM_SPEC] * (2 * n_sem), *[HBM_SPEC] * (2 * n), pl.BlockSpec(memory_space=pltpu.VMEM)),
        input_output_aliases={i: 2 * n_sem + i for i in range(2 * n)},
        compiler_params=pltpu.CompilerParams(has_side_effects=SIDE_EFFECT),
    )(*srcs, *lands)
    base = 2 * n_sem
    return (list(outs[:n_sem]), list(outs[n_sem:base]), list(outs[base:base + n]),
            list(outs[base + n:base + 2 * n]), outs[-1])


def _exchange_wait(name, items, send_sems, recv_sems, srcs, lands, after):
    n = len(items)

    n_sem = 7 * n

    def body(*refs):
        src_refs, land_refs = refs[:n], refs[n:2 * n]
        send_refs = refs[2 * n:2 * n + n_sem]
        recv_refs = refs[2 * n + n_sem:2 * n + 2 * n_sem]
        x, y, c = lax.axis_index("x"), lax.axis_index("y"), lax.axis_index("c")
        for s in range(7):
            for k in range(n):
                copy = pltpu.make_async_remote_copy(
                    src_ref=src_refs[k] if items[k][1] else src_refs[k].at[0], dst_ref=land_refs[k].at[0],
                    send_sem=send_refs[7 * k + s], recv_sem=recv_refs[7 * k + s],
                    device_id=(x, y, c), device_id_type=MESH)
                copy.wait_send()
                copy.wait_recv()

    outs = pl.pallas_call(
        body, name=name,
        out_shape=(*[pltpu.HBM(a.shape, a.dtype) for a in srcs], *[pltpu.HBM(a.shape, a.dtype) for a in lands]),
        in_specs=[HBM_SPEC] * (2 * n) + [SEM_SPEC] * (2 * n_sem) + [ANY],
        out_specs=tuple([HBM_SPEC] * (2 * n)),
        input_output_aliases={i: i for i in range(2 * n)},
        compiler_params=pltpu.CompilerParams(has_side_effects=SIDE_EFFECT),
    )(*srcs, *lands, *send_sems, *recv_sems, after)
    return list(outs[:n]), list(outs[n:])


def _cast_bf16(x, name):
    shape = x.shape
    x2 = x.reshape(-1, shape[-1])
    rows, cols = x2.shape
    tr = _row_tile(rows)

    def body(x_ref, o_ref):
        o_ref[...] = x_ref[...].astype(BF16)

    blk = pl.BlockSpec((tr, cols), lambda i: (i, 0))
    out, = _call(body, name=name, grid=(rows // tr,), in_specs=[blk], out_specs=[blk],
                 out_shape=[jax.ShapeDtypeStruct((rows, cols), BF16)], args=[x2])
    return out.reshape(shape)


def _norm_cast(x, g3, l, tm):
    s = x.shape[0]

    def body(x_ref, g_ref, o_ref):
        v = x_ref[...]
        o_ref[...] = (v * _rstd(v) * g_ref[...]).astype(BF16)

    row = pl.BlockSpec((tm, D_MODEL), lambda i: (i, 0))
    out, = _call(body, name="norm_cast", grid=(s // tm,),
                 in_specs=[row, pl.BlockSpec((None, 1, D_MODEL), lambda i: (l, 0, 0))], out_specs=[row],
                 out_shape=[jax.ShapeDtypeStruct((s, D_MODEL), BF16)], args=[x, g3])
    return out


def _in_proj(h, win, s, tq, part, comm=None):
    pad = part
    dtype = BF16 if part else F32

    def body(a_ref, b_ref, o_ref):
        def compute():
            w = jnp.concatenate([b_ref[j] for j in range(4)], axis=1)
            o_ref[...] = _dot(a_ref[...], w).astype(dtype)

        if pad:
            i = pl.program_id(0)

            @pl.when(i == 0)
            def _():
                o_ref[...] = jnp.zeros(o_ref.shape, dtype)

            pl.when(i > 0)(compute)
        else:
            compute()

    return _call(
        body, name="in_proj_qkv" if part else "in_proj_conv", grid=(s // tq + pad,),
        in_specs=[pl.BlockSpec((tq, D_MODEL), lambda i: (jnp.maximum(i - pad, 0), 0)),
                  pl.BlockSpec((4, D_MODEL, PROJ_SHARD), lambda i: (part, 0, 0))],
        out_specs=[pl.BlockSpec((tq, 4 * PROJ_SHARD), lambda i: (i, 0))],
        out_shape=[jax.ShapeDtypeStruct((s + pad * tq, PROJ_WIDTH // 2), dtype)], args=[h, win], comm=comm)


def _conv_fwd(pc, wc, g3, l, s, tr):
    hb = tr // 8

    def body(pc_ref, prev_ref, wc_ref, g_ref, o_ref):
        i = pl.program_id(0)
        gmat = _group_matrix()
        for j in range(CONV_WIDTH // 128):
            c0, c1, c2 = 128 * j, CONV_WIDTH + 128 * j, 2 * CONV_WIDTH + 128 * j
            hc = pc_ref[:, c0:c0 + 128]
            bg = pc_ref[:, c1:c1 + 128]
            cg = pc_ref[:, c2:c2 + 128]
            u_prev = jnp.where(i > 0, prev_ref[:, c2:c2 + 128] * prev_ref[:, c0:c0 + 128], 0.0)
            u = cg * hc
            full = jnp.concatenate([u_prev, u], axis=0)
            u1 = pltpu.roll(full, 1, 0)[8:]
            u2 = pltpu.roll(full, 2, 0)[8:]
            out = (u2 * wc_ref[0:1, c0:c0 + 128] + u1 * wc_ref[1:2, c0:c0 + 128]
                   + u * wc_ref[2:3, c0:c0 + 128])
            yc = bg * out
            r = lax.rsqrt(_group_mean(yc * yc, gmat) + EPS)
            o_ref[:, c0:c0 + 128] = (yc * r * g_ref[:, c0:c0 + 128]).astype(BF16)

    out, = _call(
        body, name="conv_fwd", grid=(s // tr,),
        in_specs=[pl.BlockSpec((tr, 3 * CONV_WIDTH), lambda i: (i, 0)),
                  pl.BlockSpec((8, 3 * CONV_WIDTH), lambda i: (jnp.maximum(i * hb - 1, 0), 0)),
                  pl.BlockSpec((None, 8, CONV_WIDTH), lambda i: (l, 0, 0)),
                  pl.BlockSpec((None, 1, CONV_WIDTH), lambda i: (l, 0, 0))],
        out_specs=[pl.BlockSpec((tr, CONV_WIDTH), lambda i: (i, 0))],
        out_shape=[jax.ShapeDtypeStruct((s, CONV_WIDTH), BF16)], args=[pc, pc, wc, g3])
    return out


def _toeplitz_source():
    r_i = lax.broadcasted_iota(jnp.int32, (REL_PAD, TOEP), 0)
    m_i = lax.broadcasted_iota(jnp.int32, (REL_PAD, TOEP), 1)
    idx = jnp.clip((K_BAND - 1) - m_i, -REL_CLIP, REL_CLIP) + REL_CLIP
    return jnp.where(r_i == idx, 1.0, 0.0).astype(BF16)


def _bias_build(rbp, comm=None):
    n_layers = rbp.shape[0]

    def body(rb_ref, o_ref, t_ref):
        pmat = _toeplitz_source()
        hi, mid, lo = _split3(rb_ref[...])
        t_ref[...] = _dot(hi, pmat) + _dot(mid, pmat) + _dot(lo, pmat)
        shift = (CHUNK - 1) - lax.broadcasted_iota(jnp.int32, (CHUNK, TOEP), 0)
        kchunk = lax.broadcasted_iota(jnp.int32, (CHUNK, K_BAND), 1) >> 6
        for h in range(N_HEADS):
            b = jnp.broadcast_to(t_ref[pl.ds(h, 1), :], (CHUNK, TOEP))
            for bit in range(6):
                rolled = pltpu.roll(b, TOEP - (1 << bit), 1)
                b = jnp.where(((shift >> bit) & 1) == 1, rolled, b)
            for cq in range(Q_BLOCK // CHUNK):
                off = CHUNK * (Q_BLOCK // CHUNK - 1 - cq)
                band = pltpu.roll(b, TOEP - off, 1) if off else b
                dchunk = kchunk - cq
                in_band = jnp.where(dchunk >= 0, jnp.where(dchunk <= N_LEFT_CHUNKS, 1, 0), 0) == 1
                o_ref[h, CHUNK * cq:CHUNK * (cq + 1), :] = jnp.where(in_band, band[:, :K_BAND], NEG_INF)

    return _call(
        body, name="bias_build", grid=(n_layers,),
        in_specs=[pl.BlockSpec((None, N_HEADS, REL_PAD), lambda l: (l, 0, 0))],
        out_specs=[pl.BlockSpec((None, N_HEADS, Q_BLOCK, K_BAND), lambda l: (l, 0, 0, 0))],
        out_shape=[jax.ShapeDtypeStruct((n_layers, N_HEADS, Q_BLOCK, K_BAND), F32)],
        scratch=[pltpu.VMEM((N_HEADS, TOEP), F32)], args=[rbp], comm=comm)


def _bias_bwd(ds_sum):
    n_layers = ds_sum.shape[0]

    def body(ds_ref, o_ref, t_ref):
        pmat = _toeplitz_source()
        shift = (CHUNK - 1) - lax.broadcasted_iota(jnp.int32, (CHUNK, TOEP), 0)
        for h in range(N_HEADS):
            d = None
            for cq in range(Q_BLOCK // CHUNK):
                off = CHUNK * (Q_BLOCK // CHUNK - 1 - cq)
                part = jnp.concatenate([ds_ref[h, CHUNK * cq:CHUNK * (cq + 1), :],
                                        jnp.zeros((CHUNK, TOEP - K_BAND), F32)], axis=1)
                part = pltpu.roll(part, off, 1) if off else part
                d = part if d is None else d + part
            for bit in range(6):
                rolled = pltpu.roll(d, 1 << bit, 1)
                d = jnp.where(((shift >> bit) & 1) == 1, rolled, d)
            t_ref[pl.ds(h, 1), :] = jnp.sum(d, axis=0, keepdims=True)
        hi, mid, lo = _split3(t_ref[...])
        o_ref[...] = _dot_nt(hi, pmat) + _dot_nt(mid, pmat) + _dot_nt(lo, pmat)

    out, = _call(
        body, name="bias_bwd", grid=(n_layers,),
        in_specs=[pl.BlockSpec((None, N_HEADS, Q_BLOCK, K_BAND), lambda l: (l, 0, 0, 0))],
        out_specs=[pl.BlockSpec((None, N_HEADS, REL_PAD), lambda l: (l, 0, 0))],
        out_shape=[jax.ShapeDtypeStruct((n_layers, N_HEADS, REL_PAD), F32)],
        scratch=[pltpu.VMEM((N_HEADS, TOEP), F32)], args=[ds_sum])
    return out


def _attn_fwd(qkvp, biasm, g3, l, s, pad, comm=None):
    nb = s // Q_BLOCK
    qb0 = pad // Q_BLOCK
    scale = HEAD_DIM ** -0.5
    wide = 128 * ATTN_PAIRS

    def body(q_ref, k_ref, v_ref, b_ref, g_ref, o_ref, lse_ref, yn_ref):
        blk = pl.program_id(1)
        koff = pl.multiple_of(blk * Q_BLOCK + (pad - LEFT), Q_BLOCK)
        lane = lax.broadcasted_iota(jnp.int32, (1, 128), 1)
        kpos = lax.broadcasted_iota(jnp.int32, (1, K_BAND), 1) + (blk * Q_BLOCK - LEFT)
        kmask = jnp.where(kpos >= 0, 0.0, NEG_INF)
        gmat = _group_matrix()
        for pr in range(ATTN_PAIRS):
            ls = slice(128 * pr, 128 * (pr + 1))
            q = q_ref[:, ls]
            kb = k_ref[pl.ds(koff, K_BAND), ls]
            vb = v_ref[pl.ds(koff, K_BAND), ls]
            outs, lses = [], []
            for hh in range(2):
                in_head = (lane >> 6) == hh
                qm = jnp.where(in_head, q, jnp.zeros_like(q)) * jnp.asarray(scale, BF16)
                sc = _dot_nt(qm, kb) + b_ref[2 * pr + hh] + kmask
                m = jnp.max(sc, axis=1, keepdims=True)
                e = jnp.exp(sc - m)
                den = jnp.sum(e, axis=1, keepdims=True)
                outs.append(_dot(e.astype(BF16), vb) * (1.0 / den))
                lses.append(m + jnp.log(den))
            first = lane < HEAD_DIM
            o = jnp.where(first, outs[0], outs[1])
            o_ref[:, ls] = o
            lse_ref[:, ls] = jnp.where(first, lses[0], lses[1])
            r = lax.rsqrt(_group_mean(o * o, gmat) + EPS)
            yn_ref[:, ls] = (o * r * g_ref[:, ls]).astype(BF16)

    blk_out = pl.BlockSpec((Q_BLOCK, wide), lambda p, b: (b, p))
    n_grp = ATTN_WIDTH // wide
    return _call(
        body, name="attn_fwd", grid=(n_grp, nb),
        in_specs=[pl.BlockSpec((Q_BLOCK, wide), lambda p, b: (qb0 + b, p)),
                  pl.BlockSpec((s + pad, wide), lambda p, b: (0, n_grp + p)),
                  pl.BlockSpec((s + pad, wide), lambda p, b: (0, 2 * n_grp + p)),
                  pl.BlockSpec((None, 2 * ATTN_PAIRS, Q_BLOCK, K_BAND), lambda p, b: (l, p, 0, 0)),
                  pl.BlockSpec((None, 1, wide), lambda p, b: (l, 0, p))],
        out_specs=[blk_out, blk_out, blk_out],
        out_shape=[jax.ShapeDtypeStruct((s, ATTN_WIDTH), F32),
                   jax.ShapeDtypeStruct((s, ATTN_WIDTH), F32),
                   jax.ShapeDtypeStruct((s, ATTN_WIDTH), BF16)],
        args=[qkvp, qkvp, qkvp, biasm, g3], comm=comm)


def _out_proj_fwd(ync, yna, wout, x, g_post3, g_next3, l, s, tm):
    half = D_MODEL // 2

    def body(a1_ref, a2_ref, w_ref, x_ref, gp_ref, gn_ref, z_ref, xm_ref, h_ref):
        for rs in _row_subtiles(tm, SUB_ROWS):
            z = _dot(a1_ref[rs, :], w_ref[0:half, :]) + _dot(a2_ref[rs, :], w_ref[half:D_MODEL, :])
            z_ref[rs, :] = z
            xm = x_ref[rs, :] + z * _rstd(z) * gp_ref[...]
            xm_ref[rs, :] = xm
            h_ref[rs, :] = (xm * _rstd(xm) * gn_ref[...]).astype(BF16)

    row = pl.BlockSpec((tm, D_MODEL), lambda i: (i, 0))
    gain = pl.BlockSpec((None, 1, D_MODEL), lambda i: (l, 0, 0))
    return _call(
        body, name="out_proj_fwd", grid=(s // tm,),
        in_specs=[pl.BlockSpec((tm, half), lambda i: (i, 0)), pl.BlockSpec((tm, half), lambda i: (i, 0)),
                  pl.BlockSpec((D_MODEL, D_MODEL), lambda i: (0, 0)), row, gain, gain],
        out_specs=[row, row, row],
        out_shape=[jax.ShapeDtypeStruct((s, D_MODEL), F32), jax.ShapeDtypeStruct((s, D_MODEL), F32),
                   jax.ShapeDtypeStruct((s, D_MODEL), BF16)],
        args=[ync, yna, wout, x, g_post3, g_next3])


def _ffn_in_fwd(h2, wfin4, s, tm, comm=None):
    def body(h_ref, wg_ref, wu_ref, gu_ref, act_ref):
        h = h_ref[...]
        for cs in _COL_SUBTILES:
            gate = _dot_nt(h, wg_ref[cs, :])
            up = _dot_nt(h, wu_ref[cs, :])
            gu_ref[0, :, cs] = gate.astype(BF16)
            gu_ref[1, :, cs] = up.astype(BF16)
            act_ref[:, cs] = (gate * (1.0 / (1.0 + jnp.exp(-gate))) * up).astype(BF16)

    return _call(
        body, name="ffn_in_fwd", grid=(2, s // tm),
        in_specs=[pl.BlockSpec((tm, D_MODEL), lambda b, i: (i, 0)),
                  pl.BlockSpec((None, FF_PAIR, D_MODEL), lambda b, i: (b, 0, 0)),
                  pl.BlockSpec((None, FF_PAIR, D_MODEL), lambda b, i: (2 + b, 0, 0))],
        out_specs=[pl.BlockSpec((2, tm, FF_PAIR), lambda b, i: (0, i, b)),
                   pl.BlockSpec((tm, FF_PAIR), lambda b, i: (i, b))],
        out_shape=[jax.ShapeDtypeStruct((2, s, D_FF), BF16), jax.ShapeDtypeStruct((s, D_FF), BF16)],
        args=[h2, wfin4, wfin4], comm=comm)


def _ffn_out_fwd(act, wfo, xm, g_post3, g_next3, l, l_next, s, tm, comm=None):
    def body(a_ref, w_ref, x_ref, gp_ref, gn_ref, f_ref, xo_ref, h_ref):
        for rs in _row_subtiles(tm, SUB_ROWS):
            f = _dot(a_ref[rs, :], w_ref[...])
            f_ref[rs, :] = f
            xo = x_ref[rs, :] + f * _rstd(f) * gp_ref[...]
            xo_ref[rs, :] = xo
            h_ref[rs, :] = (xo * _rstd(xo) * gn_ref[...]).astype(BF16)

    row = pl.BlockSpec((tm, D_MODEL), lambda i: (i, 0))
    return _call(
        body, name="ffn_out_fwd", grid=(s // tm,),
        in_specs=[pl.BlockSpec((tm, D_FF), lambda i: (i, 0)),
                  pl.BlockSpec((D_FF, D_MODEL), lambda i: (0, 0), pipeline_mode=pl.Buffered(1)), row,
                  pl.BlockSpec((None, 1, D_MODEL), lambda i: (l, 0, 0)),
                  pl.BlockSpec((None, 1, D_MODEL), lambda i: (l_next, 0, 0))],
        out_specs=[row, row, row],
        out_shape=[jax.ShapeDtypeStruct((s, D_MODEL), F32), jax.ShapeDtypeStruct((s, D_MODEL), F32),
                   jax.ShapeDtypeStruct((s, D_MODEL), BF16)],
        args=[act, wfo, xm, g_post3, g_next3], comm=comm)


def _loss_grad(xf, target, f, g3, l, s, tm):
    def body(x_ref, t_ref, f_ref, g_ref, dx_ref, sq_ref, df_ref, dg_ref):
        i = pl.program_id(0)
        err = x_ref[...] - t_ref[...]
        dx = err * (1.0 / D_MODEL)
        dx_ref[...] = dx
        df, dyn = _norm_bwd_rows(f_ref[...], g_ref[...], dx)
        df_ref[...] = df.astype(BF16)
        _accum_cols(dg_ref, dyn, i == 0)
        cs = jnp.sum(err * err, axis=0, keepdims=True)
        part = cs[:, 0:128]
        for k in range(1, D_MODEL // 128):
            part = part + cs[:, 128 * k:128 * (k + 1)]

        @pl.when(i == 0)
        def _():
            sq_ref[...] = jnp.zeros(sq_ref.shape, F32)

        sq_ref[0:1, :] += part

    row = pl.BlockSpec((tm, D_MODEL), lambda i: (i, 0))
    return _call(
        body, name="loss_grad", grid=(s // tm,),
        in_specs=[row, row, row, pl.BlockSpec((None, 1, D_MODEL), lambda i: (l, 0, 0))],
        out_specs=[row, pl.BlockSpec((8, 128), lambda i: (0, 0)), row, pl.BlockSpec((8, D_MODEL), lambda i: (0, 0))],
        out_shape=[jax.ShapeDtypeStruct((s, D_MODEL), F32), jax.ShapeDtypeStruct((8, 128), F32),
                   jax.ShapeDtypeStruct((s, D_MODEL), BF16), jax.ShapeDtypeStruct((8, D_MODEL), F32)],
        args=[xf, target, f, g3])


def _norm_bwd_rows(v, g, dy):
    r = _rstd(v)
    vn = v * r
    gd = dy * g
    dv = r * (gd - vn * jnp.mean(vn * gd, axis=-1, keepdims=True))
    return dv, dy * vn


def _zero_first(refs, first):
    @pl.when(first)
    def _():
        for ref in refs:
            ref[...] = jnp.zeros(ref.shape, F32)


def _add_cols(ref, val):
    ref[0:1, :] += jnp.sum(val, axis=0, keepdims=True)


def _accum_cols(ref, val, first):
    _zero_first((ref,), first)
    _add_cols(ref, val)


def _row_subtiles(rows, sub):
    sub = min(sub, rows)
    return [slice(r, r + sub) for r in range(0, rows, sub)]


def _ffn_out_bwd(df, wfo, gu, s, tm, comm=None):
    def body(df_ref, w_ref, gu_ref, dgu_ref):
        df = df_ref[...]
        for cs in _COL_SUBTILES:
            da = _dot_nt(df, w_ref[cs, :])
            g = gu_ref[0, :, cs].astype(F32)
            u = gu_ref[1, :, cs].astype(F32)
            sg = 1.0 / (1.0 + jnp.exp(-g))
            dgu_ref[0, :, cs] = (da * u * (sg * (1.0 + g * (1.0 - sg)))).astype(BF16)
            dgu_ref[1, :, cs] = (da * (g * sg)).astype(BF16)

    blk = pl.BlockSpec((2, tm, FF_PAIR), lambda b, i: (0, i, b))
    return _call(
        body, name="ffn_out_bwd", grid=(2, s // tm),
        in_specs=[pl.BlockSpec((tm, D_MODEL), lambda b, i: (i, 0)),
                  pl.BlockSpec((FF_PAIR, D_MODEL), lambda b, i: (b, 0)), blk],
        out_specs=[blk], out_shape=[jax.ShapeDtypeStruct((2, s, D_FF), BF16)],
        args=[df, wfo, gu], comm=comm)


def _dw_ffn_out(act, df, s):
    def body(a_ref, b_ref, o_ref):
        o_ref[...] = _dot_tn(a_ref[...], b_ref[...]).astype(BF16)

    out, = _call(
        body, name="dw_ffn_out", grid=(2,),
        in_specs=[pl.BlockSpec((s, FF_PAIR), lambda n: (0, n)),
                  pl.BlockSpec((s, D_MODEL), lambda n: (0, 0), pipeline_mode=pl.Buffered(1))],
        out_specs=[pl.BlockSpec((FF_PAIR, D_MODEL), lambda n: (n, 0))],
        out_shape=[jax.ShapeDtypeStruct((D_FF, D_MODEL), BF16)], args=[act, df])
    return out


def _dw_ffn_in(h2, dgu, s):
    def body(a_ref, b_ref, o_ref):
        o_ref[...] = _dot_tn(b_ref[...], a_ref[...]).astype(BF16)

    out, = _call(
        body, name="dw_ffn_in", grid=(4,),
        in_specs=[pl.BlockSpec((s, D_MODEL), lambda n: (0, 0), pipeline_mode=pl.Buffered(1)),
                  pl.BlockSpec((None, s, FF_PAIR), lambda n: (n // 2, 0, n % 2))],
        out_specs=[pl.BlockSpec((None, FF_PAIR, D_MODEL), lambda n: (n, 0, 0))],
        out_shape=[jax.ShapeDtypeStruct((4, FF_PAIR, D_MODEL), BF16)], args=[h2, dgu])
    return out


def _ffn_in_bwd(dgu, wfin, xm, g_pre3, dres, z, g_post3, l, s, tm, comm=None):
    def body(d_ref, w_ref, xm_ref, gp_ref, dres_ref, z_ref, gq_ref, dxm_ref, dz_ref, dgp_ref, dgq_ref):
        _zero_first((dgp_ref, dgq_ref), pl.program_id(0) == 0)
        for rs in _row_subtiles(tm, SUB_ROWS):
            dh = _dot(d_ref[0, rs, :], w_ref[0:D_FF, :]) + _dot(d_ref[1, rs, :], w_ref[D_FF:2 * D_FF, :])
            dx, dyn = _norm_bwd_rows(xm_ref[rs, :], gp_ref[...], dh)
            dxm = dres_ref[rs, :] + dx
            dxm_ref[rs, :] = dxm
            _add_cols(dgp_ref, dyn)
            dz, dyn2 = _norm_bwd_rows(z_ref[rs, :], gq_ref[...], dxm)
            dz_ref[rs, :] = dz.astype(BF16)
            _add_cols(dgq_ref, dyn2)

    row = pl.BlockSpec((tm, D_MODEL), lambda i: (i, 0))
    gain = pl.BlockSpec((None, 1, D_MODEL), lambda i: (l, 0, 0))
    dgs = pl.BlockSpec((8, D_MODEL), lambda i: (0, 0))
    return _call(
        body, name="ffn_in_bwd", grid=(s // tm,),
        in_specs=[pl.BlockSpec((2, tm, D_FF), lambda i: (0, i, 0)),
                  pl.BlockSpec((2 * D_FF, D_MODEL), lambda i: (0, 0), pipeline_mode=pl.Buffered(1)),
                  row, gain, row, row, gain],
        out_specs=[row, row, dgs, dgs],
        out_shape=[jax.ShapeDtypeStruct((s, D_MODEL), F32), jax.ShapeDtypeStruct((s, D_MODEL), BF16),
                   jax.ShapeDtypeStruct((8, D_MODEL), F32), jax.ShapeDtypeStruct((8, D_MODEL), F32)],
        args=[dgu, wfin, xm, g_pre3, dres, z, g_post3], comm=comm)


def _dw_out(ync, yna, dz, s):
    half = D_MODEL // 2

    def body(a1_ref, a2_ref, b_ref, o_ref):
        b = b_ref[...]
        o_ref[0:half, :] = _dot_tn(a1_ref[...], b).astype(BF16)
        o_ref[half:D_MODEL, :] = _dot_tn(a2_ref[...], b).astype(BF16)

    out, = _call(
        body, name="dw_out", grid=(2,),
        in_specs=[pl.BlockSpec((s, half), lambda n: (0, 0)), pl.BlockSpec((s, half), lambda n: (0, 0)),
                  pl.BlockSpec((s, half), lambda n: (0, n))],
        out_specs=[pl.BlockSpec((D_MODEL, half), lambda n: (0, n))],
        out_shape=[jax.ShapeDtypeStruct((D_MODEL, D_MODEL), BF16)], args=[ync, yna, dz])
    return out


def _out_proj_bwd(dz, wout, o, g3, l, s, tm):
    def body(dz_ref, w_ref, o_ref, g_ref, dyc_ref, do_ref, dg_ref):
        gmat = _group_matrix()
        _zero_first((dg_ref,), pl.program_id(0) == 0)
        for rs in _row_subtiles(tm, SUB_ROWS):
            dy = _dot_nt(dz_ref[rs, :], w_ref[...])
            dyc_ref[rs, :] = dy[:, 0:CONV_WIDTH]
            for j in range(ATTN_WIDTH // 128):
                c0 = 128 * j
                ov = o_ref[rs, c0:c0 + 128]
                dyn = dy[:, CONV_WIDTH + c0:CONV_WIDTH + c0 + 128]
                r = lax.rsqrt(_group_mean(ov * ov, gmat) + EPS)
                on = ov * r
                gd = dyn * g_ref[:, c0:c0 + 128]
                do_ref[rs, c0:c0 + 128] = r * (gd - on * _group_mean(on * gd, gmat))
                dg_ref[0:1, c0:c0 + 128] += jnp.sum(dyn * on, axis=0, keepdims=True)

    halfrow = pl.BlockSpec((tm, ATTN_WIDTH), lambda i: (i, 0))
    return _call(
        body, name="out_proj_bwd", grid=(s // tm,),
        in_specs=[pl.BlockSpec((tm, D_MODEL), lambda i: (i, 0)),
                  pl.BlockSpec((D_MODEL, D_MODEL), lambda i: (0, 0)), halfrow,
                  pl.BlockSpec((None, 1, ATTN_WIDTH), lambda i: (l, 0, 0))],
        out_specs=[halfrow, halfrow, pl.BlockSpec((8, ATTN_WIDTH), lambda i: (0, 0))],
        out_shape=[jax.ShapeDtypeStruct((s, CONV_WIDTH), F32), jax.ShapeDtypeStruct((s, ATTN_WIDTH), F32),
                   jax.ShapeDtypeStruct((8, ATTN_WIDTH), F32)],
        args=[dz, wout, o, g3])


def _conv_bwd(pc, dyc, wc, g3, dq, dk, dv, l, s, tr):
    hb = tr // 8
    nt = s // tr
    ext = tr + 16
    last_hb = s // 8 - 1

    def body(pc_ref, prev_ref, next_ref, dy_ref, dyn_ref, wc_ref, g_ref, dq_ref, dk_ref, dv_ref,
             dpc_ref, dw_ref, dg_ref):
        i = pl.program_id(0)
        for part, ref in enumerate((dq_ref, dk_ref, dv_ref)):
            c = 3 * CONV_WIDTH + ATTN_WIDTH * part
            dpc_ref[:, c:c + ATTN_WIDTH] = ref[...]
        gmat = _group_matrix()
        row = lax.broadcasted_iota(jnp.int32, (ext, 128), 0) + (i * tr - 8)
        inside = jnp.where(row >= 0, jnp.where(row < s, 1, 0), 0) == 1

        @pl.when(i == 0)
        def _():
            dw_ref[...] = jnp.zeros(dw_ref.shape, F32)
            dg_ref[...] = jnp.zeros(dg_ref.shape, F32)

        def extend(ref_prev, ref_mid, ref_next, c):
            parts = [ref_prev[:, c:c + 128] if ref_prev is not None else jnp.zeros((8, 128), F32),
                     ref_mid[:, c:c + 128], ref_next[:, c:c + 128]]
            return jnp.concatenate(parts, axis=0)

        for j in range(CONV_WIDTH // 128):
            c0, c1, c2 = 128 * j, CONV_WIDTH + 128 * j, 2 * CONV_WIDTH + 128 * j
            hc = extend(prev_ref, pc_ref, next_ref, c0)
            bg = extend(prev_ref, pc_ref, next_ref, c1)
            cg = extend(prev_ref, pc_ref, next_ref, c2)
            dyn = extend(None, dy_ref, dyn_ref, c0)
            w0, w1, w2 = (wc_ref[0:1, c0:c0 + 128], wc_ref[1:2, c0:c0 + 128], wc_ref[2:3, c0:c0 + 128])
            gain = g_ref[:, c0:c0 + 128]
            u = jnp.where(inside, cg * hc, 0.0)
            u1 = pltpu.roll(u, 1, 0)
            u2 = pltpu.roll(u, 2, 0)
            out = u2 * w0 + u1 * w1 + u * w2
            yc = bg * out
            r = lax.rsqrt(_group_mean(yc * yc, gmat) + EPS)
            ycn = yc * r
            gd = dyn * gain
            dyc = r * (gd - ycn * _group_mean(ycn * gd, gmat))
            dout = jnp.where(inside, dyc * bg, 0.0)
            du = dout * w2 + pltpu.roll(dout, ext - 1, 0) * w1 + pltpu.roll(dout, ext - 2, 0) * w0
            sl = slice(8, 8 + tr)
            dpc_ref[:, c0:c0 + 128] = (du[sl] * cg[sl]).astype(BF16)
            dpc_ref[:, c1:c1 + 128] = (dyc[sl] * out[sl]).astype(BF16)
            dpc_ref[:, c2:c2 + 128] = (du[sl] * hc[sl]).astype(BF16)
            dw_ref[0:1, c0:c0 + 128] += jnp.sum(dout[sl] * u2[sl], axis=0, keepdims=True)
            dw_ref[1:2, c0:c0 + 128] += jnp.sum(dout[sl] * u1[sl], axis=0, keepdims=True)
            dw_ref[2:3, c0:c0 + 128] += jnp.sum(dout[sl] * u[sl], axis=0, keepdims=True)
            dg_ref[0:1, c0:c0 + 128] += jnp.sum(dyn[sl] * ycn[sl], axis=0, keepdims=True)

    wide = 3 * CONV_WIDTH
    return _call(
        body, name="conv_bwd", grid=(nt,),
        in_specs=[pl.BlockSpec((tr, wide), lambda i: (i, 0)),
                  pl.BlockSpec((8, wide), lambda i: (jnp.maximum(i * hb - 1, 0), 0)),
                  pl.BlockSpec((8, wide), lambda i: (jnp.minimum((i + 1) * hb, last_hb), 0)),
                  pl.BlockSpec((tr, CONV_WIDTH), lambda i: (i, 0)),
                  pl.BlockSpec((8, CONV_WIDTH), lambda i: (jnp.minimum((i + 1) * hb, last_hb), 0)),
                  pl.BlockSpec((None, 8, CONV_WIDTH), lambda i: (l, 0, 0)),
                  pl.BlockSpec((None, 1, CONV_WIDTH), lambda i: (l, 0, 0)),
                  pl.BlockSpec((tr, ATTN_WIDTH), lambda i: (i, 0)),
                  pl.BlockSpec((tr, ATTN_WIDTH), lambda i: (i, 0)),
                  pl.BlockSpec((tr, ATTN_WIDTH), lambda i: (i, 0))],
        out_specs=[pl.BlockSpec((tr, PROJ_WIDTH), lambda i: (i, 0)),
                   pl.BlockSpec((8, CONV_WIDTH), lambda i: (0, 0)),
                   pl.BlockSpec((8, CONV_WIDTH), lambda i: (0, 0))],
        out_shape=[jax.ShapeDtypeStruct((s, PROJ_WIDTH), BF16), jax.ShapeDtypeStruct((8, CONV_WIDTH), F32),
                   jax.ShapeDtypeStruct((8, CONV_WIDTH), F32)],
        args=[pc, pc, pc, dyc, dyc, wc, g3, dq, dk, dv])


def _attn_bwd(qkvp, biasm, o, lse, do, l, s, pad, comm=None):
    nb = s // Q_BLOCK
    qb0 = pad // Q_BLOCK
    scale = HEAD_DIM ** -0.5
    wide = 128 * ATTN_PAIRS

    def body(q_ref, k_ref, v_ref, b_ref, o_ref, lse_ref, do_ref,
             dq_ref, dk_ref, dv_ref, ds_ref, dk_acc, dv_acc):
        blk = pl.program_id(1)

        @pl.when(blk == 0)
        def _():
            dk_acc[...] = jnp.zeros(dk_acc.shape, F32)
            dv_acc[...] = jnp.zeros(dv_acc.shape, F32)
            ds_ref[...] = jnp.zeros(ds_ref.shape, F32)

        koff = pl.multiple_of(blk * Q_BLOCK + (pad - LEFT), Q_BLOCK)
        lane = lax.broadcasted_iota(jnp.int32, (1, 128), 1)
        kpos = lax.broadcasted_iota(jnp.int32, (1, K_BAND), 1) + (blk * Q_BLOCK - LEFT)
        kmask = jnp.where(kpos >= 0, 0.0, NEG_INF)
        for pr in range(ATTN_PAIRS):
            ls = slice(128 * pr, 128 * (pr + 1))
            q = q_ref[:, ls]
            kb = k_ref[pl.ds(koff, K_BAND), ls]
            vb = v_ref[pl.ds(koff, K_BAND), ls]
            dov = do_ref[:, ls]
            lse_v = lse_ref[:, ls]
            prod = dov * o_ref[:, ls]
            dq_parts = []
            dk_new = jnp.zeros((K_BAND, 128), F32)
            dv_new = jnp.zeros((K_BAND, 128), F32)
            for hh in range(2):
                in_head = (lane >> 6) == hh
                qm = jnp.where(in_head, q, jnp.zeros_like(q)) * jnp.asarray(scale, BF16)
                dom = jnp.where(in_head, dov, 0.0).astype(BF16)
                delta = jnp.sum(jnp.where(in_head, prod, 0.0), axis=1, keepdims=True)
                lse_h = lse_v[:, HEAD_DIM * hh:HEAD_DIM * hh + 1]
                sc = _dot_nt(qm, kb) + b_ref[2 * pr + hh] + kmask
                p = jnp.exp(sc - lse_h)
                dp = _dot_nt(dom, vb)
                ds = p * (dp - delta)
                ds_ref[2 * pr + hh] += ds
                dsb = ds.astype(BF16)
                dq_parts.append(_dot(dsb, kb) * scale)
                dk_new = dk_new + _dot_tn(dsb, qm)
                dv_new = dv_new + _dot_tn(p.astype(BF16), dom)
            dq_ref[:, ls] = jnp.where(lane < HEAD_DIM, dq_parts[0], dq_parts[1]).astype(BF16)
            dk_acc[pl.ds(koff, K_BAND), ls] += dk_new
            dv_acc[pl.ds(koff, K_BAND), ls] += dv_new

        @pl.when(blk == nb - 1)
        def _():
            dk_ref[...] = dk_acc[pad:pad + s, :].astype(BF16)
            dv_ref[...] = dv_acc[pad:pad + s, :].astype(BF16)

    n_grp = ATTN_WIDTH // wide
    qblk = pl.BlockSpec((Q_BLOCK, wide), lambda p, b: (b, p))
    col = pl.BlockSpec((s, wide), lambda p, b: (0, p))
    shp = jax.ShapeDtypeStruct((s, ATTN_WIDTH), BF16)
    return _call(
        body, name="attn_bwd", grid=(n_grp, nb),
        in_specs=[pl.BlockSpec((Q_BLOCK, wide), lambda p, b: (qb0 + b, p)),
                  pl.BlockSpec((s + pad, wide), lambda p, b: (0, n_grp + p)),
                  pl.BlockSpec((s + pad, wide), lambda p, b: (0, 2 * n_grp + p)),
                  pl.BlockSpec((None, 2 * ATTN_PAIRS, Q_BLOCK, K_BAND), lambda p, b: (l, p, 0, 0)),
                  qblk, qblk, qblk],
        out_specs=[qblk, col, col, pl.BlockSpec((2 * ATTN_PAIRS, Q_BLOCK, K_BAND), lambda p, b: (p, 0, 0))],
        out_shape=[shp, shp, shp, jax.ShapeDtypeStruct((N_HEADS, Q_BLOCK, K_BAND), F32)],
        scratch=[pltpu.VMEM((s + pad, wide), F32), pltpu.VMEM((s + pad, wide), F32)],
        args=[qkvp, qkvp, qkvp, biasm, o, lse, do], comm=comm)


def _dw_in(h, dproj, s):
    def body(a_ref, b_ref, o_ref):
        acc = _dot_tn(a_ref[...], b_ref[...])
        o_ref[0] = acc[:, 0:PROJ_SHARD].astype(BF16)
        o_ref[1] = acc[:, PROJ_SHARD:2 * PROJ_SHARD].astype(BF16)

    out, = _call(
        body, name="dw_in", grid=(4,),
        in_specs=[pl.BlockSpec((s, D_MODEL), lambda n: (0, 0)),
                  pl.BlockSpec((s, 2 * PROJ_SHARD), lambda n: (0, n))],
        out_specs=[pl.BlockSpec((2, D_MODEL, PROJ_SHARD), lambda n: (n, 0, 0))],
        out_shape=[jax.ShapeDtypeStruct((N_DEV, D_MODEL, PROJ_SHARD), BF16)], args=[h, dproj])
    return out


def _in_proj_bwd(dproj, win, x, g3, dres, l, s, tm, f_prev=None, g_post3=None, comm=None):
    chain = f_prev is not None

    def body(d_ref, w_ref, x_ref, g_ref, dres_ref, *rest):
        if chain:
            f_ref, gq_ref, dx_ref, dg_ref, df_ref, dgq_ref = rest
            _zero_first((dg_ref, dgq_ref), pl.program_id(0) == 0)
        else:
            dx_ref, dg_ref = rest
            _zero_first((dg_ref,), pl.program_id(0) == 0)
        w = jnp.concatenate([w_ref[j] for j in range(N_DEV)], axis=1)
        for rs in _row_subtiles(tm, SUB_ROWS):
            dh = _dot_nt(d_ref[rs, :], w)
            dx, dyn = _norm_bwd_rows(x_ref[rs, :], g_ref[...], dh)
            dx = dres_ref[rs, :] + dx
            dx_ref[rs, :] = dx
            _add_cols(dg_ref, dyn)
            if chain:
                df, dyn2 = _norm_bwd_rows(f_ref[rs, :], gq_ref[...], dx)
                df_ref[rs, :] = df.astype(BF16)
                _add_cols(dgq_ref, dyn2)

    row = pl.BlockSpec((tm, D_MODEL), lambda i: (i, 0))
    dgs = pl.BlockSpec((8, D_MODEL), lambda i: (0, 0))
    in_specs = [pl.BlockSpec((tm, PROJ_WIDTH), lambda i: (i, 0)),
                pl.BlockSpec((N_DEV, D_MODEL, PROJ_SHARD), lambda i: (0, 0, 0), pipeline_mode=pl.Buffered(1)),
                row, pl.BlockSpec((None, 1, D_MODEL), lambda i: (l, 0, 0)), row]
    out_specs = [row, dgs]
    out_shape = [jax.ShapeDtypeStruct((s, D_MODEL), F32), jax.ShapeDtypeStruct((8, D_MODEL), F32)]
    args = [dproj, win, x, g3, dres]
    if chain:
        in_specs += [row, pl.BlockSpec((None, 1, D_MODEL), lambda i: (l - 1, 0, 0))]
        out_specs += [row, dgs]
        out_shape += [jax.ShapeDtypeStruct((s, D_MODEL), BF16), jax.ShapeDtypeStruct((8, D_MODEL), F32)]
        args += [f_prev, g_post3]
    return _call(body, name="in_proj_bwd", grid=(s // tm,), in_specs=in_specs, out_specs=out_specs,
                 out_shape=out_shape, args=args, comm=comm)


def _adamw(name, w, m, v, lands, owns=None, me=None):
    groups, rows, cols = w.shape
    assert len(lands) == groups
    n_part = lands[0].shape[0]
    tr = _row_tile(rows, tuple(c for c in (512, 352, 256, 176, 128, 64, 32, 16, 8) if c * cols <= 256 * 1024))
    c1 = 1.0 - ADAM_B1 ** ADAM_STEP
    c2 = 1.0 - ADAM_B2 ** ADAM_STEP
    n_own = groups if owns is not None else 0

    def body(*refs):
        if n_own:
            me_ref, refs = refs[0], refs[1:]
        w_ref, m_ref, v_ref = refs[:3]
        land_refs = refs[3:3 + groups]
        own_refs = refs[3 + groups:3 + groups + n_own]
        g_ref, d_ref, nm_ref, nv_ref = refs[3 + groups + n_own:]
        grp = pl.program_id(0)
        for gi in range(groups):
            @pl.when(grp == gi)
            def _():
                l_ref = land_refs[gi]
                g = None
                for p in range(n_part):
                    part = l_ref[p].astype(F32)
                    if n_own:
                        part = jnp.where(me_ref[0] == p, own_refs[gi][...].astype(F32), part)
                    g = part if g is None else g + part
                g_ref[...] = g
                m1 = ADAM_B1 * m_ref[...] + (1.0 - ADAM_B1) * g
                v1 = ADAM_B2 * v_ref[...] + (1.0 - ADAM_B2) * (g * g)
                nm_ref[...] = m1
                nv_ref[...] = v1
                d_ref[...] = -ADAM_LR * ((m1 / c1) / (jnp.sqrt(v1 / c2) + ADAM_EPS) + ADAM_WD * w_ref[...])

    blk = pl.BlockSpec((None, tr, cols), lambda g, i, *_: (g, i, 0))
    shp = jax.ShapeDtypeStruct((groups, rows, cols), F32)

    def land_spec(gi):
        return pl.BlockSpec((n_part, tr, cols), lambda g, i, *_: (0, jnp.where(g == gi, i, 0), 0))

    def own_spec(gi):
        if owns[gi].ndim == 3:
            return pl.BlockSpec((None, tr, cols), lambda g, i, me_ref: (me_ref[0], jnp.where(g == gi, i, 0), 0))
        return pl.BlockSpec((tr, cols), lambda g, i, me_ref: (jnp.where(g == gi, i, 0), 0))

    in_specs = [blk, blk, blk] + [land_spec(gi) for gi in range(groups)] + [own_spec(gi) for gi in range(n_own)]
    args = [w, m, v] + list(lands) + (list(owns) if n_own else [])
    if not n_own:
        return _call(body, name=name, grid=(groups, rows // tr), in_specs=in_specs,
                     out_specs=[blk, blk, blk, blk], out_shape=[shp, shp, shp, shp], args=args)
    return pl.pallas_call(
        body, name=name,
        grid_spec=pltpu.PrefetchScalarGridSpec(
            num_scalar_prefetch=1, grid=(groups, rows // tr), in_specs=in_specs, out_specs=[blk, blk, blk, blk]),
        out_shape=[shp, shp, shp, shp],
        compiler_params=pltpu.CompilerParams(dimension_semantics=("arbitrary", "arbitrary"),
                                             vmem_limit_bytes=VMEM_LIMIT),
    )(me, *args)


def _pack_small(rel, gco, gao, gpm, gqm, gpf, gqf):
    n_layers = rel.shape[0]
    relp = jnp.pad(rel, ((0, 0), (0, 0), (0, REL_PAD - rel.shape[2])))
    parts = [relp.reshape(n_layers * N_HEADS * REL_PAD // 128, 128)]
    parts += [a.reshape(-1, 128) for a in (gco, gao, gpm, gqm, gpf, gqf)]
    return jnp.concatenate(parts, axis=0)


def _unpack_small(p, n_layers):
    n_rel = n_layers * N_HEADS * REL_PAD // 128
    rel = p[:n_rel].reshape(n_layers, N_HEADS, REL_PAD)[:, :, :2 * REL_CLIP + 1]
    outs = [rel]
    r0 = n_rel
    for width in (CONV_WIDTH, ATTN_WIDTH, D_MODEL, D_MODEL, D_MODEL, D_MODEL):
        nr = n_layers * width // 128
        outs.append(p[r0:r0 + nr].reshape(n_layers, width))
        r0 += nr
    return outs


def kernel(x, w_in, w_conv, rel_bias, g_conv_out, g_attn_out, w_out, g_pre_mix, g_post_mix, g_pre_ffn, g_post_ffn, w_ffn_in, w_ffn_out, loss_target, m_w_in, m_w_conv, m_rel_bias, m_g_conv_out, m_g_attn_out, m_w_out, m_g_pre_mix, m_g_post_mix, m_g_pre_ffn, m_g_post_ffn, m_w_ffn_in, m_w_ffn_out, v_w_in, v_w_conv, v_rel_bias, v_g_conv_out, v_g_attn_out, v_w_out, v_g_pre_mix, v_g_post_mix, v_g_pre_ffn, v_g_post_ffn, v_w_ffn_in, v_w_ffn_out):
    n_layers = w_in.shape[0]
    s = x.shape[1]
    assert x.shape == (1, s, D_MODEL) and s % 1024 == 0
    assert w_in.shape == (n_layers, D_MODEL, PROJ_SHARD) and w_ffn_in.shape == (n_layers, D_MODEL, FF_SHARD)
    tm = 512
    tq = 1024 if s >= 2048 else 512
    tf = min(1024, s)
    x0 = x.reshape(s, D_MODEL)
    target = loss_target.reshape(s, D_MODEL)
    dev = _dev_index(lax.axis_index("x"), lax.axis_index("y"), lax.axis_index("c"))

    wt_ffn_in, mt_ffn_in, vt_ffn_in = (jnp.transpose(a, (0, 2, 1)) for a in (w_ffn_in, m_w_ffn_in, v_w_ffn_in))
    local_w = [_cast_bf16(w_in, "cast_w_in"), _cast_bf16(w_out, "cast_w_out"),
               _cast_bf16(wt_ffn_in, "cast_w_ffn_in"), _cast_bf16(w_ffn_out, "cast_w_ffn_out")]
    wc_local = jnp.pad(jnp.transpose(w_conv, (0, 2, 1)).reshape(-1), (0, 1024 - n_layers * 3 * 64)).reshape(8, 128)
    biasm, win_next, wc_g = _bias_build(jnp.pad(rel_bias, ((0, 0), (0, 0), (0, REL_PAD - rel_bias.shape[2]))),
                                        comm=_Gather([(local_w[0], 0), (wc_local, None)]))
    weights = [None] * n_layers
    wc_full = wc_g.reshape(N_DEV, 1024)[:, :n_layers * 3 * 64].reshape(N_DEV, n_layers, 3, 64)
    wc_full = jnp.transpose(wc_full, (1, 2, 0, 3)).reshape(n_layers, 3, CONV_WIDTH)
    wc_full = jnp.pad(wc_full, ((0, 0), (0, 5), (0, 0)))

    g3 = {k: v.reshape(n_layers, 1, -1) for k, v in dict(
        conv=g_conv_out, attn=g_attn_out, pre_mix=g_pre_mix, post_mix=g_post_mix,
        pre_ffn=g_pre_ffn, post_ffn=g_post_ffn).items()}

    saved = []
    xl = x0
    h = _norm_cast(x0, g3["pre_mix"], 0, tm)
    for l in range(n_layers):
        win = win_next
        pc, wout = _in_proj(h, win, s, tq, 0, comm=_Gather([(local_w[1], l)]))
        qkvp, = _in_proj(h, win, s, tq, 1)
        ync = _conv_fwd(pc, wc_full, g3["conv"], l, s, tm)
        o, lse, yna, wfin = _attn_fwd(qkvp, biasm, g3["attn"], l, s, tq, comm=_Gather([(local_w[2], l)]))
        wout = wout.reshape(D_MODEL, D_MODEL)
        z, xm, h2 = _out_proj_fwd(ync, yna, wout, xl, g3["post_mix"], g3["pre_ffn"], l, s, tq)
        gu, act, wfout = _ffn_in_fwd(h2, wfin.reshape(4, FF_PAIR, D_MODEL), s, tf, comm=_Gather([(local_w[3], l)]))
        wfo = wfout.reshape(D_FF, D_MODEL)
        l_next = min(l + 1, n_layers - 1)
        f, xo, h_next, *got = _ffn_out_fwd(act, wfo, xm, g3["post_ffn"], g3["pre_mix"], l, l_next, s, tm,
                                           comm=_Gather([(local_w[0], l + 1)]) if l + 1 < n_layers else None)
        weights[l] = [win, wout, wfin.reshape(2 * D_FF, D_MODEL), wfo]
        win_next = got[0] if got else None
        saved.append(dict(x=xl, h=h, pc=pc, qkvp=qkvp, ync=ync, yna=yna, o=o, lse=lse, z=z, xm=xm,
                          h2=h2, gu=gu, act=act, f=f))
        xl, h = xo, h_next

    dx, sq, df, dg_post_ffn = _loss_grad(xl, target, saved[-1]["f"], g3["post_ffn"], n_layers - 1, s, tm)
    loss = lax.psum(jnp.sum(sq) * (0.5 / D_MODEL), ("x", "y", "c"))

    lands = dict(win=[None] * n_layers, wout=[None] * n_layers, wfin=[None] * n_layers, wfout=[None] * n_layers)
    small = {k: [None] * n_layers for k in ("gco", "gao", "gpm", "gqm", "gpf", "gqf", "wc")}
    d_rel = [None] * n_layers
    started = []

    def start(name, keys, l, arrays):
        items = [(a, False) for a in arrays]
        send_sems, recv_sems, srcs, zones, token = _exchange_start(name + "_start", items)
        started.append((name, keys, l, items, send_sems, recv_sems, srcs, zones))
        return token[0:1, 0:1].reshape(1, 1, 1)

    for l in reversed(range(n_layers)):
        sv = saved[l]
        win, wout, wfin, wfo = weights[l]
        small["gqf"][l] = dg_post_ffn[0]
        dgu, = _ffn_out_bwd(df, wfo, sv["gu"], s, tf)
        d_wfout = _dw_ffn_out(sv["act"], df, s).reshape(N_DEV, FFO_SHARD, D_MODEL)
        d_wfin = _dw_ffn_in(sv["h2"], dgu, s).reshape(N_DEV, FF_SHARD, D_MODEL)
        token = start(f"exchange_ffn{l}", ("wfout", "wfin"), l, [d_wfout, d_wfin])
        dxm, dz, dg_pre_ffn, dg_post_mix = _ffn_in_bwd(
            dgu, wfin, sv["xm"], g3["pre_ffn"] + token, dx, sv["z"], g3["post_mix"], l, s, tm)
        small["gpf"][l] = dg_pre_ffn[0]
        small["gqm"][l] = dg_post_mix[0]
        d_wout = _dw_out(sv["ync"], sv["yna"], dz, s).reshape(N_DEV, D_MODEL // N_DEV, D_MODEL)
        dyc, do, dg_attn = _out_proj_bwd(dz, wout, sv["o"], g3["attn"], l, s, tq)
        small["gao"][l] = dg_attn[0]
        dq, dk, dv, ds_sum = _attn_bwd(sv["qkvp"], biasm, sv["o"], sv["lse"], do, l, s, tq)
        d_rel[l] = _bias_bwd(ds_sum[None])
        dproj, dwc, dg_conv = _conv_bwd(sv["pc"], dyc, wc_full, g3["conv"], dq, dk, dv, l, s, tm)
        small["wc"][l] = dwc[0:3]
        small["gco"][l] = dg_conv[0]
        d_win = _dw_in(sv["h"], dproj, s)
        token = start(f"exchange_mix{l}", ("wout", "win"), l, [d_wout, d_win])
        if l > 0:
            dx, dg_pre_mix, df, dg_post_ffn = _in_proj_bwd(
                dproj, win, sv["x"], g3["pre_mix"] + token, dxm, l, s, tm, f_prev=saved[l - 1]["f"],
                g_post3=g3["post_ffn"])
        else:
            dx, dg_pre_mix = _in_proj_bwd(dproj, win, sv["x"], g3["pre_mix"] + token, dxm, l, s, tm)
        small["gpm"][l] = dg_pre_mix[0]
    grad_x = dx.reshape(1, s, D_MODEL)

    small_vec = jnp.concatenate(
        [_pack_small(jnp.concatenate(d_rel)[:, :, :2 * REL_CLIP + 1],
                     *[jnp.stack(small[k]) for k in ("gco", "gao", "gpm", "gqm", "gpf", "gqf")]),
         jnp.stack(small["wc"]).reshape(-1, 128)], axis=0)
    small_vec = jnp.pad(small_vec, ((0, (-small_vec.shape[0]) % 8), (0, 0)))
    small_items = [(small_vec, True)]
    small_sems = _exchange_start("exchange_small_start", small_items)

    owns = dict(win=[None] * n_layers, wout=[None] * n_layers, wfin=[None] * n_layers, wfout=[None] * n_layers)

    def wait(which, after):
        for name, keys, l, items, send_sems, recv_sems, srcs, zones in started:
            if keys == which:
                srcs, zones = _exchange_wait(name + "_wait", items, send_sems, recv_sems, srcs, zones, after)
                for key, src, zone in zip(keys, srcs, zones):
                    owns[key][l], lands[key][l] = src, zone

    me = dev.astype(jnp.int32).reshape(1)
    wait(("wfout", "wfin"), dx)
    r_fin = [jnp.transpose(t, (0, 2, 1)) for t in _adamw(
        "adamw_w_ffn_in", wt_ffn_in, mt_ffn_in, vt_ffn_in, lands["wfin"], owns["wfin"], me)]
    r_fout = _adamw("adamw_w_ffn_out", w_ffn_out, m_w_ffn_out, v_w_ffn_out, lands["wfout"], owns["wfout"], me)
    wait(("wout", "win"), r_fout[0])
    r_out = _adamw("adamw_w_out", w_out, m_w_out, v_w_out, lands["wout"], owns["wout"], me)
    r_in = _adamw("adamw_w_in", w_in, m_w_in, v_w_in, lands["win"], owns["win"], me)
    (small_own,), (land_small,) = _exchange_wait(
        "exchange_small_wait", small_items, small_sems[0], small_sems[1], small_sems[2], small_sems[3], r_in[0])

    n_rep = 64 * n_layers
    rep = _adamw(
        "adamw_replicated",
        _pack_small(rel_bias, g_conv_out, g_attn_out, g_pre_mix, g_post_mix, g_pre_ffn, g_post_ffn)[None],
        _pack_small(m_rel_bias, m_g_conv_out, m_g_attn_out, m_g_pre_mix, m_g_post_mix, m_g_pre_ffn, m_g_post_ffn)[None],
        _pack_small(v_rel_bias, v_g_conv_out, v_g_attn_out, v_g_pre_mix, v_g_post_mix, v_g_pre_ffn, v_g_post_ffn)[None],
        [land_small[:, :n_rep]], [small_own[:n_rep]], me)
    rep = [_unpack_small(t[0], n_layers) for t in rep]

    wc_rows = n_layers * 3 * CONV_WIDTH // 128
    zeros_wc = jnp.zeros((1, wc_rows, 128), F32)
    g_wc_full = _adamw("sum_w_conv", zeros_wc, zeros_wc, zeros_wc, [land_small[:, n_rep:n_rep + wc_rows]],
                       [small_own[n_rep:n_rep + wc_rows]], me)[0]
    g_wc_full = g_wc_full.reshape(n_layers, 3, CONV_WIDTH)
    g_wc = lax.dynamic_slice_in_dim(g_wc_full, dev * (CONV_WIDTH // N_DEV), CONV_WIDTH // N_DEV, axis=2)
    g_wc = jnp.transpose(g_wc, (0, 2, 1))

    def tiny(a):
        flat = a.reshape(-1)
        return jnp.pad(flat, (0, (-flat.shape[0]) % 1024)).reshape(1, -1, 128)

    r_wc = _adamw("adamw_w_conv", tiny(w_conv), tiny(m_w_conv), tiny(v_w_conv), [tiny(g_wc)])
    r_wc = [t.reshape(-1)[:w_conv.size].reshape(w_conv.shape) for t in r_wc]

    def leaf(kind):
        return [r_in[kind], r_wc[kind], rep[kind][0], rep[kind][1], rep[kind][2], r_out[kind],
                rep[kind][3], rep[kind][4], rep[kind][5], rep[kind][6], r_fin[kind], r_fout[kind]]

    return (loss, grad_x, *leaf(0), *leaf(1), *leaf(2), *leaf(3))
```

```python
import math

import jax
import jax.numpy as jnp
from jax import lax
from jax.experimental import pallas as pl
from jax.experimental.pallas import tpu as pltpu

F32 = jnp.float32
BF16 = jnp.bfloat16

D_MODEL = 1024
N_DEV = 8
CHUNK = 64
N_LEFT_CHUNKS = 8
CONV_WIDTH = 512
ATTN_WIDTH = 512
HEAD_DIM = 64
N_HEADS = 8
REL_CLIP = 128
REL_PAD = 384
PROJ_WIDTH = 3072
PROJ_SHARD = PROJ_WIDTH // N_DEV
D_FF = 2816
FF_SHARD = 2 * D_FF // N_DEV
FFO_SHARD = D_FF // N_DEV
FF_PAIR = 2 * FF_SHARD
_COL_SUBTILES = (slice(0, 768), slice(768, FF_PAIR))
EPS = 1e-6
NEG_INF = -1e30
ATTN_PAIRS = 2
Q_BLOCK = 4 * CHUNK
K_BAND = Q_BLOCK + N_LEFT_CHUNKS * CHUNK
LEFT = N_LEFT_CHUNKS * CHUNK
TOEP = 1024

ADAM_LR = 0.001
ADAM_B1 = 0.9
ADAM_B2 = 0.999
ADAM_EPS = 1e-08
ADAM_WD = 0.01
ADAM_STEP = 10

VMEM_LIMIT = 52 * 1024 * 1024
SUB_ROWS = 256
MESH = pl.DeviceIdType.MESH
ANY = pl.BlockSpec(memory_space=pl.ANY)

NT = (((1,), (1,)), ((), ()))
TN = (((0,), (0,)), ((), ()))


def _dot(a, b):
    return jnp.dot(a, b, preferred_element_type=F32)


def _dot_nt(a, b):
    return lax.dot_general(a, b, NT, preferred_element_type=F32)


def _dot_tn(a, b):
    return lax.dot_general(a, b, TN, preferred_element_type=F32)


def _rstd(v):
    return lax.rsqrt(jnp.mean(v * v, axis=-1, keepdims=True) + EPS)


def _group_matrix():
    r = lax.broadcasted_iota(jnp.int32, (128, 128), 0) >> 6
    c = lax.broadcasted_iota(jnp.int32, (128, 128), 1) >> 6
    return jnp.where(r == c, 1.0, 0.0).astype(BF16)


def _group_mean(v, gmat):
    hi = v.astype(BF16)
    lo = (v - hi.astype(F32)).astype(BF16)
    return (_dot(hi, gmat) + _dot(lo, gmat)) * (1.0 / HEAD_DIM)


def _split3(v):
    hi = v.astype(BF16)
    r1 = v - hi.astype(F32)
    mid = r1.astype(BF16)
    lo = (r1 - mid.astype(F32)).astype(BF16)
    return hi, mid, lo


def _row_tile(rows, cands=(1024, 512, 704, 256, 128, 64, 32, 16)):
    for c in cands:
        if rows % c == 0:
            return c
    return rows


def _dev_index(px, py, pc):
    return 4 * px + 2 * py + pc


def _when(cond):
    if cond is True:
        return lambda fn: fn()
    return pl.when(cond)


def _phases(grid):
    def phases():
        if not grid:
            return True, True, True
        lin = pl.program_id(0)
        for a in range(1, len(grid)):
            lin = lin * grid[a] + pl.program_id(a)
        total = math.prod(grid)
        return lin == 0, lin == (3 * total) // 4, lin == total - 1
    return phases


class _Gather:
    def __init__(self, items):
        self.items = items
        self.args = [a for a, _ in items]
        n = len(items)
        self.out_shape = [jax.ShapeDtypeStruct((N_DEV,) + (a.shape if lay is None else a.shape[1:]), a.dtype)
                          for a, lay in items]
        self.scratch = [pltpu.SemaphoreType.DMA((n, 7)), pltpu.SemaphoreType.DMA((n, 7)),
                        pltpu.SemaphoreType.DMA((n,))]

    def _ctx(self, ins, outs, sems):
        send_sems, recv_sems, local_sems = sems
        x, y, c = lax.axis_index("x"), lax.axis_index("y"), lax.axis_index("c")
        chips = [(1 - x, y), (x, 1 - y), (1 - x, 1 - y)]

        def src(k):
            lay = self.items[k][1]
            return ins[k] if lay is None else ins[k].at[lay]

        def copy(k, s, idx, to, from_src=False):
            return pltpu.make_async_remote_copy(
                src_ref=src(k) if from_src else outs[k].at[idx], dst_ref=outs[k].at[idx],
                send_sem=send_sems.at[k, s], recv_sem=recv_sems.at[k, s],
                device_id=to, device_id_type=MESH)

        def local(k):
            return pltpu.make_async_copy(src(k), outs[k].at[_dev_index(x, y, c)], local_sems.at[k])

        return x, y, c, chips, copy, local

    def start(self, ins, outs, sems, cond):
        n = len(self.items)

        @_when(cond)
        def _():
            x, y, c, chips, copy, local = self._ctx(ins, outs, sems)
            me = _dev_index(x, y, c)
            for k in range(n):
                local(k).start()
                copy(k, 0, me, (x, y, 1 - c), from_src=True).start()
                for j, chip in enumerate(chips):
                    copy(k, 1 + j, me, (chip[0], chip[1], c), from_src=True).start()

    def forward(self, ins, outs, sems, cond):
        n = len(self.items)

        @_when(cond)
        def _():
            x, y, c, chips, copy, local = self._ctx(ins, outs, sems)
            for j, chip in enumerate(chips):
                idx = _dev_index(chip[0], chip[1], c)
                for k in range(n):
                    copy(k, 1 + j, idx, (x, y, c)).wait_recv()
                    copy(k, 4 + j, idx, (x, y, 1 - c)).start()

    def finish(self, ins, outs, sems, cond):
        n = len(self.items)

        @_when(cond)
        def _():
            x, y, c, chips, copy, local = self._ctx(ins, outs, sems)
            me = _dev_index(x, y, c)
            for k in range(n):
                copy(k, 0, _dev_index(x, y, 1 - c), (x, y, c)).wait_recv()
            for j, chip in enumerate(chips):
                idx = _dev_index(chip[0], chip[1], 1 - c)
                for k in range(n):
                    copy(k, 4 + j, idx, (x, y, c)).wait_recv()
            for k in range(n):
                for s in range(4):
                    copy(k, s, me, (x, y, c), from_src=True).wait_send()
                for j, chip in enumerate(chips):
                    copy(k, 4 + j, _dev_index(chip[0], chip[1], c), (x, y, c)).wait_send()
                local(k).wait()


_PEER_FLIPS = [(0, 0, 1), (1, 0, 0), (0, 1, 0), (1, 1, 0), (1, 0, 1), (0, 1, 1), (1, 1, 1)]


def _call(body, *, name, grid, in_specs, out_specs, out_shape, args, scratch=(), comm=None):
    n_hi, n_ho, n_hs = len(args), len(out_shape), len(scratch)
    c_args = list(comm.args) if comm else []
    c_out = list(comm.out_shape) if comm else []
    c_scr = list(comm.scratch) if comm else []
    phases = _phases(grid)

    def kern(*refs):
        cuts = [n_hi, len(c_args), n_ho, len(c_out), n_hs, len(c_scr)]
        parts, pos = [], 0
        for n in cuts:
            parts.append(refs[pos:pos + n])
            pos += n
        hi, ci, ho, co, hs, cs = parts
        if comm:
            first, mid, last = phases()
            comm.start(ci, co, cs, first)
            comm.forward(ci, co, cs, mid)
        body(*hi, *ho, *hs)
        if comm:
            comm.finish(ci, co, cs, last)

    sem = ("arbitrary",) * len(grid) if grid else None
    return pl.pallas_call(
        kern, name=name, grid=grid,
        in_specs=list(in_specs) + [ANY] * len(c_args),
        out_specs=list(out_specs) + [ANY] * len(c_out),
        out_shape=list(out_shape) + c_out,
        scratch_shapes=list(scratch) + c_scr,
        compiler_params=pltpu.CompilerParams(dimension_semantics=sem, vmem_limit_bytes=VMEM_LIMIT),
    )(*args, *c_args)


def _comm_only(name, comm):
    return _call(lambda: None, name=name, grid=(), in_specs=[], out_specs=[], out_shape=[], args=[], comm=comm)


HBM_SPEC = pl.BlockSpec(memory_space=pltpu.HBM)
SEM_SPEC = pl.BlockSpec(memory_space=pltpu.SEMAPHORE)
SIDE_EFFECT = pltpu.SideEffectType.DATAFLOW_SIDE_EFFECTING


def _exchange_peer(x, y, c, s):
    fx, fy, fc = _PEER_FLIPS[s]
    return x ^ fx, y ^ fy, c ^ fc


def _exchange_start(name, items):
    n = len(items)
    srcs = [pltpu.with_memory_space_constraint(a, pltpu.HBM) for a, _ in items]
    land_shapes = [(N_DEV,) + (a.shape if whole else a.shape[1:]) for a, whole in items]
    lands = [pltpu.with_memory_space_constraint(lax.empty(shp, a.dtype), pltpu.HBM)
             for shp, (a, _) in zip(land_shapes, items)]

    n_sem = 7 * n

    def body(*refs):
        src_refs, land_refs = refs[:n], refs[n:2 * n]
        send_sems = refs[2 * n:2 * n + n_sem]
        recv_sems = refs[2 * n + n_sem:2 * n + 2 * n_sem]
        token = refs[-1]
        x, y, c = lax.axis_index("x"), lax.axis_index("y"), lax.axis_index("c")
        me = _dev_index(x, y, c)
        for s in range(7):
            px, py, pc = _exchange_peer(x, y, c, s)
            for k in range(n):
                src = src_refs[k] if items[k][1] else src_refs[k].at[_dev_index(px, py, pc)]
                pltpu.make_async_remote_copy(
                    src_ref=src, dst_ref=land_refs[k].at[me],
                    send_sem=send_sems[7 * k + s], recv_sem=recv_sems[7 * k + s],
                    device_id=(px, py, pc), device_id_type=MESH).start()
        token[...] = jnp.zeros(token.shape, token.dtype)

    outs = pl.pallas_call(
        body, name=name,
        out_shape=(*[pltpu.SemaphoreType.DMA(())] * (2 * n_sem),
                   *[pltpu.HBM(a.shape, a.dtype) for a in srcs],
                   *[pltpu.HBM(shp, a.dtype) for shp, a in zip(land_shapes, srcs)],
                   jax.ShapeDtypeStruct((8, 128), F32)),
        in_specs=[HBM_SPEC] * (2 * n),
        out_specs=(*[SEM_SPEC] * (2 * n_sem), *[HBM_SPEC] * (2 * n), pl.BlockSpec(memory_space=pltpu.VMEM)),
        input_output_aliases={i: 2 * n_sem + i for i in range(2 * n)},
        compiler_params=pltpu.CompilerParams(has_side_effects=SIDE_EFFECT),
    )(*srcs, *lands)
    base = 2 * n_sem
    return (list(outs[:n_sem]), list(outs[n_sem:base]), list(outs[base:base + n]),
            list(outs[base + n:base + 2 * n]), outs[-1])


def _exchange_wait(name, items, send_sems, recv_sems, srcs, lands, after):
    n = len(items)

    n_sem = 7 * n

    def body(*refs):
        src_refs, land_refs = refs[:n], refs[n:2 * n]
        send_refs = refs[2 * n:2 * n + n_sem]
        recv_refs = refs[2 * n + n_sem:2 * n + 2 * n_sem]
        x, y, c = lax.axis_index("x"), lax.axis_index("y"), lax.axis_index("c")
        for s in range(7):
            for k in range(n):
                copy = pltpu.make_async_remote_copy(
                    src_ref=src_refs[k] if items[k][1] else src_refs[k].at[0], dst_ref=land_refs[k].at[0],
                    send_sem=send_refs[7 * k + s], recv_sem=recv_refs[7 * k + s],
                    device_id=(x, y, c), device_id_type=MESH)
                copy.wait_send()
                copy.wait_recv()

    outs = pl.pallas_call(
        body, name=name,
        out_shape=(*[pltpu.HBM(a.shape, a.dtype) for a in srcs], *[pltpu.HBM(a.shape, a.dtype) for a in lands]),
        in_specs=[HBM_SPEC] * (2 * n) + [SEM_SPEC] * (2 * n_sem) + [ANY],
        out_specs=tuple([HBM_SPEC] * (2 * n)),
        input_output_aliases={i: i for i in range(2 * n)},
        compiler_params=pltpu.CompilerParams(has_side_effects=SIDE_EFFECT),
    )(*srcs, *lands, *send_sems, *recv_sems, after)
    return list(outs[:n]), list(outs[n:])


def _cast_bf16(x, name):
    shape = x.shape
    x2 = x.reshape(-1, shape[-1])
    rows, cols = x2.shape
    tr = _row_tile(rows)

    def body(x_ref, o_ref):
        o_ref[...] = x_ref[...].astype(BF16)

    blk = pl.BlockSpec((tr, cols), lambda i: (i, 0))
    out, = _call(body, name=name, grid=(rows // tr,), in_specs=[blk], out_specs=[blk],
                 out_shape=[jax.ShapeDtypeStruct((rows, cols), BF16)], args=[x2])
    return out.reshape(shape)


def _norm_cast(x, g3, l, tm):
    s = x.shape[0]

    def body(x_ref, g_ref, o_ref):
        v = x_ref[...]
        o_ref[...] = (v * _rstd(v) * g_ref[...]).astype(BF16)

    row = pl.BlockSpec((tm, D_MODEL), lambda i: (i, 0))
    out, = _call(body, name="norm_cast", grid=(s // tm,),
                 in_specs=[row, pl.BlockSpec((None, 1, D_MODEL), lambda i: (l, 0, 0))], out_specs=[row],
                 out_shape=[jax.ShapeDtypeStruct((s, D_MODEL), BF16)], args=[x, g3])
    return out


def _in_proj(h, win, s, tq, part, comm=None):
    pad = part
    dtype = BF16 if part else F32

    def body(a_ref, b_ref, o_ref):
        def compute():
            w = jnp.concatenate([b_ref[j] for j in range(4)], axis=1)
            o_ref[...] = _dot(a_ref[...], w).astype(dtype)

        if pad:
            i = pl.program_id(0)

            @pl.when(i == 0)
            def _():
                o_ref[...] = jnp.zeros(o_ref.shape, dtype)

            pl.when(i > 0)(compute)
        else:
            compute()

    return _call(
        body, name="in_proj_qkv" if part else "in_proj_conv", grid=(s // tq + pad,),
        in_specs=[pl.BlockSpec((tq, D_MODEL), lambda i: (jnp.maximum(i - pad, 0), 0)),
                  pl.BlockSpec((4, D_MODEL, PROJ_SHARD), lambda i: (part, 0, 0))],
        out_specs=[pl.BlockSpec((tq, 4 * PROJ_SHARD), lambda i: (i, 0))],
        out_shape=[jax.ShapeDtypeStruct((s + pad * tq, PROJ_WIDTH // 2), dtype)], args=[h, win], comm=comm)


def _conv_fwd(pc, wc, g3, l, s, tr):
    hb = tr // 8

    def body(pc_ref, prev_ref, wc_ref, g_ref, o_ref):
        i = pl.program_id(0)
        gmat = _group_matrix()
        for j in range(CONV_WIDTH // 128):
            c0, c1, c2 = 128 * j, CONV_WIDTH + 128 * j, 2 * CONV_WIDTH + 128 * j
            hc = pc_ref[:, c0:c0 + 128]
            bg = pc_ref[:, c1:c1 + 128]
            cg = pc_ref[:, c2:c2 + 128]
            u_prev = jnp.where(i > 0, prev_ref[:, c2:c2 + 128] * prev_ref[:, c0:c0 + 128], 0.0)
            u = cg * hc
            full = jnp.concatenate([u_prev, u], axis=0)
            u1 = pltpu.roll(full, 1, 0)[8:]
            u2 = pltpu.roll(full, 2, 0)[8:]
            out = (u2 * wc_ref[0:1, c0:c0 + 128] + u1 * wc_ref[1:2, c0:c0 + 128]
                   + u * wc_ref[2:3, c0:c0 + 128])
            yc = bg * out
            r = lax.rsqrt(_group_mean(yc * yc, gmat) + EPS)
            o_ref[:, c0:c0 + 128] = (yc * r * g_ref[:, c0:c0 + 128]).astype(BF16)

    out, = _call(
        body, name="conv_fwd", grid=(s // tr,),
        in_specs=[pl.BlockSpec((tr, 3 * CONV_WIDTH), lambda i: (i, 0)),
                  pl.BlockSpec((8, 3 * CONV_WIDTH), lambda i: (jnp.maximum(i * hb - 1, 0), 0)),
                  pl.BlockSpec((None, 8, CONV_WIDTH), lambda i: (l, 0, 0)),
                  pl.BlockSpec((None, 1, CONV_WIDTH), lambda i: (l, 0, 0))],
        out_specs=[pl.BlockSpec((tr, CONV_WIDTH), lambda i: (i, 0))],
        out_shape=[jax.ShapeDtypeStruct((s, CONV_WIDTH), BF16)], args=[pc, pc, wc, g3])
    return out


def _toeplitz_source():
    r_i = lax.broadcasted_iota(jnp.int32, (REL_PAD, TOEP), 0)
    m_i = lax.broadcasted_iota(jnp.int32, (REL_PAD, TOEP), 1)
    idx = jnp.clip((K_BAND - 1) - m_i, -REL_CLIP, REL_CLIP) + REL_CLIP
    return jnp.where(r_i == idx, 1.0, 0.0).astype(BF16)


def _bias_build(rbp, comm=None):
    n_layers = rbp.shape[0]

    def body(rb_ref, o_ref, t_ref):
        pmat = _toeplitz_source()
        hi, mid, lo = _split3(rb_ref[...])
        t_ref[...] = _dot(hi, pmat) + _dot(mid, pmat) + _dot(lo, pmat)
        shift = (CHUNK - 1) - lax.broadcasted_iota(jnp.int32, (CHUNK, TOEP), 0)
        kchunk = lax.broadcasted_iota(jnp.int32, (CHUNK, K_BAND), 1) >> 6
        for h in range(N_HEADS):
            b = jnp.broadcast_to(t_ref[pl.ds(h, 1), :], (CHUNK, TOEP))
            for bit in range(6):
                rolled = pltpu.roll(b, TOEP - (1 << bit), 1)
                b = jnp.where(((shift >> bit) & 1) == 1, rolled, b)
            for cq in range(Q_BLOCK // CHUNK):
                off = CHUNK * (Q_BLOCK // CHUNK - 1 - cq)
                band = pltpu.roll(b, TOEP - off, 1) if off else b
                dchunk = kchunk - cq
                in_band = jnp.where(dchunk >= 0, jnp.where(dchunk <= N_LEFT_CHUNKS, 1, 0), 0) == 1
                o_ref[h, CHUNK * cq:CHUNK * (cq + 1), :] = jnp.where(in_band, band[:, :K_BAND], NEG_INF)

    return _call(
        body, name="bias_build", grid=(n_layers,),
        in_specs=[pl.BlockSpec((None, N_HEADS, REL_PAD), lambda l: (l, 0, 0))],
        out_specs=[pl.BlockSpec((None, N_HEADS, Q_BLOCK, K_BAND), lambda l: (l, 0, 0, 0))],
        out_shape=[jax.ShapeDtypeStruct((n_layers, N_HEADS, Q_BLOCK, K_BAND), F32)],
        scratch=[pltpu.VMEM((N_HEADS, TOEP), F32)], args=[rbp], comm=comm)


def _bias_bwd(ds_sum):
    n_layers = ds_sum.shape[0]

    def body(ds_ref, o_ref, t_ref):
        pmat = _toeplitz_source()
        shift = (CHUNK - 1) - lax.broadcasted_iota(jnp.int32, (CHUNK, TOEP), 0)
        for h in range(N_HEADS):
            d = None
            for cq in range(Q_BLOCK // CHUNK):
                off = CHUNK * (Q_BLOCK // CHUNK - 1 - cq)
                part = jnp.concatenate([ds_ref[h, CHUNK * cq:CHUNK * (cq + 1), :],
                                        jnp.zeros((CHUNK, TOEP - K_BAND), F32)], axis=1)
                part = pltpu.roll(part, off, 1) if off else part
                d = part if d is None else d + part
            for bit in range(6):
                rolled = pltpu.roll(d, 1 << bit, 1)
                d = jnp.where(((shift >> bit) & 1) == 1, rolled, d)
            t_ref[pl.ds(h, 1), :] = jnp.sum(d, axis=0, keepdims=True)
        hi, mid, lo = _split3(t_ref[...])
        o_ref[...] = _dot_nt(hi, pmat) + _dot_nt(mid, pmat) + _dot_nt(lo, pmat)

    out, = _call(
        body, name="bias_bwd", grid=(n_layers,),
        in_specs=[pl.BlockSpec((None, N_HEADS, Q_BLOCK, K_BAND), lambda l: (l, 0, 0, 0))],
        out_specs=[pl.BlockSpec((None, N_HEADS, REL_PAD), lambda l: (l, 0, 0))],
        out_shape=[jax.ShapeDtypeStruct((n_layers, N_HEADS, REL_PAD), F32)],
        scratch=[pltpu.VMEM((N_HEADS, TOEP), F32)], args=[ds_sum])
    return out


def _attn_fwd(qkvp, biasm, g3, l, s, pad, comm=None):
    nb = s // Q_BLOCK
    qb0 = pad // Q_BLOCK
    scale = HEAD_DIM ** -0.5
    wide = 128 * ATTN_PAIRS

    def body(q_ref, k_ref, v_ref, b_ref, g_ref, o_ref, lse_ref, yn_ref):
        blk = pl.program_id(1)
        koff = pl.multiple_of(blk * Q_BLOCK + (pad - LEFT), Q_BLOCK)
        lane = lax.broadcasted_iota(jnp.int32, (1, 128), 1)
        kpos = lax.broadcasted_iota(jnp.int32, (1, K_BAND), 1) + (blk * Q_BLOCK - LEFT)
        kmask = jnp.where(kpos >= 0, 0.0, NEG_INF)
        gmat = _group_matrix()
        for pr in range(ATTN_PAIRS):
            ls = slice(128 * pr, 128 * (pr + 1))
            q = q_ref[:, ls]
            kb = k_ref[pl.ds(koff, K_BAND), ls]
            vb = v_ref[pl.ds(koff, K_BAND), ls]
            outs, lses = [], []
            for hh in range(2):
                in_head = (lane >> 6) == hh
                qm = jnp.where(in_head, q, jnp.zeros_like(q)) * jnp.asarray(scale, BF16)
                sc = _dot_nt(qm, kb) + b_ref[2 * pr + hh] + kmask
                m = jnp.max(sc, axis=1, keepdims=True)
                e = jnp.exp(sc - m)
                den = jnp.sum(e, axis=1, keepdims=True)
                outs.append(_dot(e.astype(BF16), vb) * (1.0 / den))
                lses.append(m + jnp.log(den))
            first = lane < HEAD_DIM
            o = jnp.where(first, outs[0], outs[1])
            o_ref[:, ls] = o
            lse_ref[:, ls] = jnp.where(first, lses[0], lses[1])
            r = lax.rsqrt(_group_mean(o * o, gmat) + EPS)
            yn_ref[:, ls] = (o * r * g_ref[:, ls]).astype(BF16)

    blk_out = pl.BlockSpec((Q_BLOCK, wide), lambda p, b: (b, p))
    n_grp = ATTN_WIDTH // wide
    return _call(
        body, name="attn_fwd", grid=(n_grp, nb),
        in_specs=[pl.BlockSpec((Q_BLOCK, wide), lambda p, b: (qb0 + b, p)),
                  pl.BlockSpec((s + pad, wide), lambda p, b: (0, n_grp + p)),
                  pl.BlockSpec((s + pad, wide), lambda p, b: (0, 2 * n_grp + p)),
                  pl.BlockSpec((None, 2 * ATTN_PAIRS, Q_BLOCK, K_BAND), lambda p, b: (l, p, 0, 0)),
                  pl.BlockSpec((None, 1, wide), lambda p, b: (l, 0, p))],
        out_specs=[blk_out, blk_out, blk_out],
        out_shape=[jax.ShapeDtypeStruct((s, ATTN_WIDTH), F32),
                   jax.ShapeDtypeStruct((s, ATTN_WIDTH), F32),
                   jax.ShapeDtypeStruct((s, ATTN_WIDTH), BF16)],
        args=[qkvp, qkvp, qkvp, biasm, g3], comm=comm)


def _out_proj_fwd(ync, yna, wout, x, g_post3, g_next3, l, s, tm):
    half = D_MODEL // 2

    def body(a1_ref, a2_ref, w_ref, x_ref, gp_ref, gn_ref, z_ref, xm_ref, h_ref):
        for rs in _row_subtiles(tm, SUB_ROWS):
            z = _dot(a1_ref[rs, :], w_ref[0:half, :]) + _dot(a2_ref[rs, :], w_ref[half:D_MODEL, :])
            z_ref[rs, :] = z
            xm = x_ref[rs, :] + z * _rstd(z) * gp_ref[...]
            xm_ref[rs, :] = xm
            h_ref[rs, :] = (xm * _rstd(xm) * gn_ref[...]).astype(BF16)

    row = pl.BlockSpec((tm, D_MODEL), lambda i: (i, 0))
    gain = pl.BlockSpec((None, 1, D_MODEL), lambda i: (l, 0, 0))
    return _call(
        body, name="out_proj_fwd", grid=(s // tm,),
        in_specs=[pl.BlockSpec((tm, half), lambda i: (i, 0)), pl.BlockSpec((tm, half), lambda i: (i, 0)),
                  pl.BlockSpec((D_MODEL, D_MODEL), lambda i: (0, 0)), row, gain, gain],
        out_specs=[row, row, row],
        out_shape=[jax.ShapeDtypeStruct((s, D_MODEL), F32), jax.ShapeDtypeStruct((s, D_MODEL), F32),
                   jax.ShapeDtypeStruct((s, D_MODEL), BF16)],
        args=[ync, yna, wout, x, g_post3, g_next3])


def _ffn_in_fwd(h2, wfin4, s, tm, comm=None):
    def body(h_ref, wg_ref, wu_ref, gu_ref, act_ref):
        h = h_ref[...]
        for cs in _COL_SUBTILES:
            gate = _dot_nt(h, wg_ref[cs, :])
            up = _dot_nt(h, wu_ref[cs, :])
            gu_ref[0, :, cs] = gate.astype(BF16)
            gu_ref[1, :, cs] = up.astype(BF16)
            act_ref[:, cs] = (gate * (1.0 / (1.0 + jnp.exp(-gate))) * up).astype(BF16)

    return _call(
        body, name="ffn_in_fwd", grid=(2, s // tm),
        in_specs=[pl.BlockSpec((tm, D_MODEL), lambda b, i: (i, 0)),
                  pl.BlockSpec((None, FF_PAIR, D_MODEL), lambda b, i: (b, 0, 0)),
                  pl.BlockSpec((None, FF_PAIR, D_MODEL), lambda b, i: (2 + b, 0, 0))],
        out_specs=[pl.BlockSpec((2, tm, FF_PAIR), lambda b, i: (0, i, b)),
                   pl.BlockSpec((tm, FF_PAIR), lambda b, i: (i, b))],
        out_shape=[jax.ShapeDtypeStruct((2, s, D_FF), BF16), jax.ShapeDtypeStruct((s, D_FF), BF16)],
        args=[h2, wfin4, wfin4], comm=comm)


def _ffn_out_fwd(act, wfo, xm, g_post3, g_next3, l, l_next, s, tm, comm=None):
    def body(a_ref, w_ref, x_ref, gp_ref, gn_ref, f_ref, xo_ref, h_ref):
        for rs in _row_subtiles(tm, SUB_ROWS):
            f = _dot(a_ref[rs, :], w_ref[...])
            f_ref[rs, :] = f
            xo = x_ref[rs, :] + f * _rstd(f) * gp_ref[...]
            xo_ref[rs, :] = xo
            h_ref[rs, :] = (xo * _rstd(xo) * gn_ref[...]).astype(BF16)

    row = pl.BlockSpec((tm, D_MODEL), lambda i: (i, 0))
    return _call(
        body, name="ffn_out_fwd", grid=(s // tm,),
        in_specs=[pl.BlockSpec((tm, D_FF), lambda i: (i, 0)),
                  pl.BlockSpec((D_FF, D_MODEL), lambda i: (0, 0), pipeline_mode=pl.Buffered(1)), row,
                  pl.BlockSpec((None, 1, D_MODEL), lambda i: (l, 0, 0)),
                  pl.BlockSpec((None, 1, D_MODEL), lambda i: (l_next, 0, 0))],
        out_specs=[row, row, row],
        out_shape=[jax.ShapeDtypeStruct((s, D_MODEL), F32), jax.ShapeDtypeStruct((s, D_MODEL), F32),
                   jax.ShapeDtypeStruct((s, D_MODEL), BF16)],
        args=[act, wfo, xm, g_post3, g_next3], comm=comm)


def _loss_grad(xf, target, f, g3, l, s, tm):
    def body(x_ref, t_ref, f_ref, g_ref, dx_ref, sq_ref, df_ref, dg_ref):
        i = pl.program_id(0)
        err = x_ref[...] - t_ref[...]
        dx = err * (1.0 / D_MODEL)
        dx_ref[...] = dx
        df, dyn = _norm_bwd_rows(f_ref[...], g_ref[...], dx)
        df_ref[...] = df.astype(BF16)
        _accum_cols(dg_ref, dyn, i == 0)
        cs = jnp.sum(err * err, axis=0, keepdims=True)
        part = cs[:, 0:128]
        for k in range(1, D_MODEL // 128):
            part = part + cs[:, 128 * k:128 * (k + 1)]

        @pl.when(i == 0)
        def _():
            sq_ref[...] = jnp.zeros(sq_ref.shape, F32)

        sq_ref[0:1, :] += part

    row = pl.BlockSpec((tm, D_MODEL), lambda i: (i, 0))
    return _call(
        body, name="loss_grad", grid=(s // tm,),
        in_specs=[row, row, row, pl.BlockSpec((None, 1, D_MODEL), lambda i: (l, 0, 0))],
        out_specs=[row, pl.BlockSpec((8, 128), lambda i: (0, 0)), row, pl.BlockSpec((8, D_MODEL), lambda i: (0, 0))],
        out_shape=[jax.ShapeDtypeStruct((s, D_MODEL), F32), jax.ShapeDtypeStruct((8, 128), F32),
                   jax.ShapeDtypeStruct((s, D_MODEL), BF16), jax.ShapeDtypeStruct((8, D_MODEL), F32)],
        args=[xf, target, f, g3])


def _norm_bwd_rows(v, g, dy):
    r = _rstd(v)
    vn = v * r
    gd = dy * g
    dv = r * (gd - vn * jnp.mean(vn * gd, axis=-1, keepdims=True))
    return dv, dy * vn


def _zero_first(refs, first):
    @pl.when(first)
    def _():
        for ref in refs:
            ref[...] = jnp.zeros(ref.shape, F32)


def _add_cols(ref, val):
    ref[0:1, :] += jnp.sum(val, axis=0, keepdims=True)


def _accum_cols(ref, val, first):
    _zero_first((ref,), first)
    _add_cols(ref, val)


def _row_subtiles(rows, sub):
    sub = min(sub, rows)
    return [slice(r, r + sub) for r in range(0, rows, sub)]


def _ffn_out_bwd(df, wfo, gu, s, tm, comm=None):
    def body(df_ref, w_ref, gu_ref, dgu_ref):
        df = df_ref[...]
        for cs in _COL_SUBTILES:
            da = _dot_nt(df, w_ref[cs, :])
            g = gu_ref[0, :, cs].astype(F32)
            u = gu_ref[1, :, cs].astype(F32)
            sg = 1.0 / (1.0 + jnp.exp(-g))
            dgu_ref[0, :, cs] = (da * u * (sg * (1.0 + g * (1.0 - sg)))).astype(BF16)
            dgu_ref[1, :, cs] = (da * (g * sg)).astype(BF16)

    blk = pl.BlockSpec((2, tm, FF_PAIR), lambda b, i: (0, i, b))
    return _call(
        body, name="ffn_out_bwd", grid=(2, s // tm),
        in_specs=[pl.BlockSpec((tm, D_MODEL), lambda b, i: (i, 0)),
                  pl.BlockSpec((FF_PAIR, D_MODEL), lambda b, i: (b, 0)), blk],
        out_specs=[blk], out_shape=[jax.ShapeDtypeStruct((2, s, D_FF), BF16)],
        args=[df, wfo, gu], comm=comm)


def _dw_ffn_out(act, df, s):
    def body(a_ref, b_ref, o_ref):
        o_ref[...] = _dot_tn(a_ref[...], b_ref[...]).astype(BF16)

    out, = _call(
        body, name="dw_ffn_out", grid=(2,),
        in_specs=[pl.BlockSpec((s, FF_PAIR), lambda n: (0, n)),
                  pl.BlockSpec((s, D_MODEL), lambda n: (0, 0), pipeline_mode=pl.Buffered(1))],
        out_specs=[pl.BlockSpec((FF_PAIR, D_MODEL), lambda n: (n, 0))],
        out_shape=[jax.ShapeDtypeStruct((D_FF, D_MODEL), BF16)], args=[act, df])
    return out


def _dw_ffn_in(h2, dgu, s):
    def body(a_ref, b_ref, o_ref):
        o_ref[...] = _dot_tn(b_ref[...], a_ref[...]).astype(BF16)

    out, = _call(
        body, name="dw_ffn_in", grid=(4,),
        in_specs=[pl.BlockSpec((s, D_MODEL), lambda n: (0, 0), pipeline_mode=pl.Buffered(1)),
                  pl.BlockSpec((None, s, FF_PAIR), lambda n: (n // 2, 0, n % 2))],
        out_specs=[pl.BlockSpec((None, FF_PAIR, D_MODEL), lambda n: (n, 0, 0))],
        out_shape=[jax.ShapeDtypeStruct((4, FF_PAIR, D_MODEL), BF16)], args=[h2, dgu])
    return out


def _ffn_in_bwd(dgu, wfin, xm, g_pre3, dres, z, g_post3, l, s, tm, comm=None):
    def body(d_ref, w_ref, xm_ref, gp_ref, dres_ref, z_ref, gq_ref, dxm_ref, dz_ref, dgp_ref, dgq_ref):
        _zero_first((dgp_ref, dgq_ref), pl.program_id(0) == 0)
        for rs in _row_subtiles(tm, SUB_ROWS):
            dh = _dot(d_ref[0, rs, :], w_ref[0:D_FF, :]) + _dot(d_ref[1, rs, :], w_ref[D_FF:2 * D_FF, :])
            dx, dyn = _norm_bwd_rows(xm_ref[rs, :], gp_ref[...], dh)
            dxm = dres_ref[rs, :] + dx
            dxm_ref[rs, :] = dxm
            _add_cols(dgp_ref, dyn)
            dz, dyn2 = _norm_bwd_rows(z_ref[rs, :], gq_ref[...], dxm)
            dz_ref[rs, :] = dz.astype(BF16)
            _add_cols(dgq_ref, dyn2)

    row = pl.BlockSpec((tm, D_MODEL), lambda i: (i, 0))
    gain = pl.BlockSpec((None, 1, D_MODEL), lambda i: (l, 0, 0))
    dgs = pl.BlockSpec((8, D_MODEL), lambda i: (0, 0))
    return _call(
        body, name="ffn_in_bwd", grid=(s // tm,),
        in_specs=[pl.BlockSpec((2, tm, D_FF), lambda i: (0, i, 0)),
                  pl.BlockSpec((2 * D_FF, D_MODEL), lambda i: (0, 0), pipeline_mode=pl.Buffered(1)),
                  row, gain, row, row, gain],
        out_specs=[row, row, dgs, dgs],
        out_shape=[jax.ShapeDtypeStruct((s, D_MODEL), F32), jax.ShapeDtypeStruct((s, D_MODEL), BF16),
                   jax.ShapeDtypeStruct((8, D_MODEL), F32), jax.ShapeDtypeStruct((8, D_MODEL), F32)],
        args=[dgu, wfin, xm, g_pre3, dres, z, g_post3], comm=comm)


def _dw_out(ync, yna, dz, s):
    half = D_MODEL // 2

    def body(a1_ref, a2_ref, b_ref, o_ref):
        b = b_ref[...]
        o_ref[0:half, :] = _dot_tn(a1_ref[...], b).astype(BF16)
        o_ref[half:D_MODEL, :] = _dot_tn(a2_ref[...], b).astype(BF16)

    out, = _call(
        body, name="dw_out", grid=(2,),
        in_specs=[pl.BlockSpec((s, half), lambda n: (0, 0)), pl.BlockSpec((s, half), lambda n: (0, 0)),
                  pl.BlockSpec((s, half), lambda n: (0, n))],
        out_specs=[pl.BlockSpec((D_MODEL, half), lambda n: (0, n))],
        out_shape=[jax.ShapeDtypeStruct((D_MODEL, D_MODEL), BF16)], args=[ync, yna, dz])
    return out


def _out_proj_bwd(dz, wout, o, g3, l, s, tm):
    def body(dz_ref, w_ref, o_ref, g_ref, dyc_ref, do_ref, dg_ref):
        gmat = _group_matrix()
        _zero_first((dg_ref,), pl.program_id(0) == 0)
        for rs in _row_subtiles(tm, SUB_ROWS):
            dy = _dot_nt(dz_ref[rs, :], w_ref[...])
            dyc_ref[rs, :] = dy[:, 0:CONV_WIDTH]
            for j in range(ATTN_WIDTH // 128):
                c0 = 128 * j
                ov = o_ref[rs, c0:c0 + 128]
                dyn = dy[:, CONV_WIDTH + c0:CONV_WIDTH + c0 + 128]
                r = lax.rsqrt(_group_mean(ov * ov, gmat) + EPS)
                on = ov * r
                gd = dyn * g_ref[:, c0:c0 + 128]
                do_ref[rs, c0:c0 + 128] = r * (gd - on * _group_mean(on * gd, gmat))
                dg_ref[0:1, c0:c0 + 128] += jnp.sum(dyn * on, axis=0, keepdims=True)

    halfrow = pl.BlockSpec((tm, ATTN_WIDTH), lambda i: (i, 0))
    return _call(
        body, name="out_proj_bwd", grid=(s // tm,),
        in_specs=[pl.BlockSpec((tm, D_MODEL), lambda i: (i, 0)),
                  pl.BlockSpec((D_MODEL, D_MODEL), lambda i: (0, 0)), halfrow,
                  pl.BlockSpec((None, 1, ATTN_WIDTH), lambda i: (l, 0, 0))],
        out_specs=[halfrow, halfrow, pl.BlockSpec((8, ATTN_WIDTH), lambda i: (0, 0))],
        out_shape=[jax.ShapeDtypeStruct((s, CONV_WIDTH), F32), jax.ShapeDtypeStruct((s, ATTN_WIDTH), F32),
                   jax.ShapeDtypeStruct((8, ATTN_WIDTH), F32)],
        args=[dz, wout, o, g3])


def _conv_bwd(pc, dyc, wc, g3, dq, dk, dv, l, s, tr):
    hb = tr // 8
    nt = s // tr
    ext = tr + 16
    last_hb = s // 8 - 1

    def body(pc_ref, prev_ref, next_ref, dy_ref, dyn_ref, wc_ref, g_ref, dq_ref, dk_ref, dv_ref,
             dpc_ref, dw_ref, dg_ref):
        i = pl.program_id(0)
        for part, ref in enumerate((dq_ref, dk_ref, dv_ref)):
            c = 3 * CONV_WIDTH + ATTN_WIDTH * part
            dpc_ref[:, c:c + ATTN_WIDTH] = ref[...]
        gmat = _group_matrix()
        row = lax.broadcasted_iota(jnp.int32, (ext, 128), 0) + (i * tr - 8)
        inside = jnp.where(row >= 0, jnp.where(row < s, 1, 0), 0) == 1

        @pl.when(i == 0)
        def _():
            dw_ref[...] = jnp.zeros(dw_ref.shape, F32)
            dg_ref[...] = jnp.zeros(dg_ref.shape, F32)

        def extend(ref_prev, ref_mid, ref_next, c):
            parts = [ref_prev[:, c:c + 128] if ref_prev is not None else jnp.zeros((8, 128), F32),
                     ref_mid[:, c:c + 128], ref_next[:, c:c + 128]]
            return jnp.concatenate(parts, axis=0)

        for j in range(CONV_WIDTH // 128):
            c0, c1, c2 = 128 * j, CONV_WIDTH + 128 * j, 2 * CONV_WIDTH + 128 * j
            hc = extend(prev_ref, pc_ref, next_ref, c0)
            bg = extend(prev_ref, pc_ref, next_ref, c1)
            cg = extend(prev_ref, pc_ref, next_ref, c2)
            dyn = extend(None, dy_ref, dyn_ref, c0)
            w0, w1, w2 = (wc_ref[0:1, c0:c0 + 128], wc_ref[1:2, c0:c0 + 128], wc_ref[2:3, c0:c0 + 128])
            gain = g_ref[:, c0:c0 + 128]
            u = jnp.where(inside, cg * hc, 0.0)
            u1 = pltpu.roll(u, 1, 0)
            u2 = pltpu.roll(u, 2, 0)
            out = u2 * w0 + u1 * w1 + u * w2
            yc = bg * out
            r = lax.rsqrt(_group_mean(yc * yc, gmat) + EPS)
            ycn = yc * r
            gd = dyn * gain
            dyc = r * (gd - ycn * _group_mean(ycn * gd, gmat))
            dout = jnp.where(inside, dyc * bg, 0.0)
            du = dout * w2 + pltpu.roll(dout, ext - 1, 0) * w1 + pltpu.roll(dout, ext - 2, 0) * w0
            sl = slice(8, 8 + tr)
            dpc_ref[:, c0:c0 + 128] = (du[sl] * cg[sl]).astype(BF16)
            dpc_ref[:, c1:c1 + 128] = (dyc[sl] * out[sl]).astype(BF16)
            dpc_ref[:, c2:c2 + 128] = (du[sl] * hc[sl]).astype(BF16)
            dw_ref[0:1, c0:c0 + 128] += jnp.sum(dout[sl] * u2[sl], axis=0, keepdims=True)
            dw_ref[1:2, c0:c0 + 128] += jnp.sum(dout[sl] * u1[sl], axis=0, keepdims=True)
            dw_ref[2:3, c0:c0 + 128] += jnp.sum(dout[sl] * u[sl], axis=0, keepdims=True)
            dg_ref[0:1, c0:c0 + 128] += jnp.sum(dyn[sl] * ycn[sl], axis=0, keepdims=True)

    wide = 3 * CONV_WIDTH
    return _call(
        body, name="conv_bwd", grid=(nt,),
        in_specs=[pl.BlockSpec((tr, wide), lambda i: (i, 0)),
                  pl.BlockSpec((8, wide), lambda i: (jnp.maximum(i * hb - 1, 0), 0)),
                  pl.BlockSpec((8, wide), lambda i: (jnp.minimum((i + 1) * hb, last_hb), 0)),
                  pl.BlockSpec((tr, CONV_WIDTH), lambda i: (i, 0)),
                  pl.BlockSpec((8, CONV_WIDTH), lambda i: (jnp.minimum((i + 1) * hb, last_hb), 0)),
                  pl.BlockSpec((None, 8, CONV_WIDTH), lambda i: (l, 0, 0)),
                  pl.BlockSpec((None, 1, CONV_WIDTH), lambda i: (l, 0, 0)),
                  pl.BlockSpec((tr, ATTN_WIDTH), lambda i: (i, 0)),
                  pl.BlockSpec((tr, ATTN_WIDTH), lambda i: (i, 0)),
                  pl.BlockSpec((tr, ATTN_WIDTH), lambda i: (i, 0))],
        out_specs=[pl.BlockSpec((tr, PROJ_WIDTH), lambda i: (i, 0)),
                   pl.BlockSpec((8, CONV_WIDTH), lambda i: (0, 0)),
                   pl.BlockSpec((8, CONV_WIDTH), lambda i: (0, 0))],
        out_shape=[jax.ShapeDtypeStruct((s, PROJ_WIDTH), BF16), jax.ShapeDtypeStruct((8, CONV_WIDTH), F32),
                   jax.ShapeDtypeStruct((8, CONV_WIDTH), F32)],
        args=[pc, pc, pc, dyc, dyc, wc, g3, dq, dk, dv])


def _attn_bwd(qkvp, biasm, o, lse, do, l, s, pad, comm=None):
    nb = s // Q_BLOCK
    qb0 = pad // Q_BLOCK
    scale = HEAD_DIM ** -0.5
    wide = 128 * ATTN_PAIRS

    def body(q_ref, k_ref, v_ref, b_ref, o_ref, lse_ref, do_ref,
             dq_ref, dk_ref, dv_ref, ds_ref, dk_acc, dv_acc):
        blk = pl.program_id(1)

        @pl.when(blk == 0)
        def _():
            dk_acc[...] = jnp.zeros(dk_acc.shape, F32)
            dv_acc[...] = jnp.zeros(dv_acc.shape, F32)
            ds_ref[...] = jnp.zeros(ds_ref.shape, F32)

        koff = pl.multiple_of(blk * Q_BLOCK + (pad - LEFT), Q_BLOCK)
        lane = lax.broadcasted_iota(jnp.int32, (1, 128), 1)
        kpos = lax.broadcasted_iota(jnp.int32, (1, K_BAND), 1) + (blk * Q_BLOCK - LEFT)
        kmask = jnp.where(kpos >= 0, 0.0, NEG_INF)
        for pr in range(ATTN_PAIRS):
            ls = slice(128 * pr, 128 * (pr + 1))
            q = q_ref[:, ls]
            kb = k_ref[pl.ds(koff, K_BAND), ls]
            vb = v_ref[pl.ds(koff, K_BAND), ls]
            dov = do_ref[:, ls]
            lse_v = lse_ref[:, ls]
            prod = dov * o_ref[:, ls]
            dq_parts = []
            dk_new = jnp.zeros((K_BAND, 128), F32)
            dv_new = jnp.zeros((K_BAND, 128), F32)
            for hh in range(2):
                in_head = (lane >> 6) == hh
                qm = jnp.where(in_head, q, jnp.zeros_like(q)) * jnp.asarray(scale, BF16)
                dom = jnp.where(in_head, dov, 0.0).astype(BF16)
                delta = jnp.sum(jnp.where(in_head, prod, 0.0), axis=1, keepdims=True)
                lse_h = lse_v[:, HEAD_DIM * hh:HEAD_DIM * hh + 1]
                sc = _dot_nt(qm, kb) + b_ref[2 * pr + hh] + kmask
                p = jnp.exp(sc - lse_h)
                dp = _dot_nt(dom, vb)
                ds = p * (dp - delta)
                ds_ref[2 * pr + hh] += ds
                dsb = ds.astype(BF16)
                dq_parts.append(_dot(dsb, kb) * scale)
                dk_new = dk_new + _dot_tn(dsb, qm)
                dv_new = dv_new + _dot_tn(p.astype(BF16), dom)
            dq_ref[:, ls] = jnp.where(lane < HEAD_DIM, dq_parts[0], dq_parts[1]).astype(BF16)
            dk_acc[pl.ds(koff, K_BAND), ls] += dk_new
            dv_acc[pl.ds(koff, K_BAND), ls] += dv_new

        @pl.when(blk == nb - 1)
        def _():
            dk_ref[...] = dk_acc[pad:pad + s, :].astype(BF16)
            dv_ref[...] = dv_acc[pad:pad + s, :].astype(BF16)

    n_grp = ATTN_WIDTH // wide
    qblk = pl.BlockSpec((Q_BLOCK, wide), lambda p, b: (b, p))
    col = pl.BlockSpec((s, wide), lambda p, b: (0, p))
    shp = jax.ShapeDtypeStruct((s, ATTN_WIDTH), BF16)
    return _call(
        body, name="attn_bwd", grid=(n_grp, nb),
        in_specs=[pl.BlockSpec((Q_BLOCK, wide), lambda p, b: (qb0 + b, p)),
                  pl.BlockSpec((s + pad, wide), lambda p, b: (0, n_grp + p)),
                  pl.BlockSpec((s + pad, wide), lambda p, b: (0, 2 * n_grp + p)),
                  pl.BlockSpec((None, 2 * ATTN_PAIRS, Q_BLOCK, K_BAND), lambda p, b: (l, p, 0, 0)),
                  qblk, qblk, qblk],
        out_specs=[qblk, col, col, pl.BlockSpec((2 * ATTN_PAIRS, Q_BLOCK, K_BAND), lambda p, b: (p, 0, 0))],
        out_shape=[shp, shp, shp, jax.ShapeDtypeStruct((N_HEADS, Q_BLOCK, K_BAND), F32)],
        scratch=[pltpu.VMEM((s + pad, wide), F32), pltpu.VMEM((s + pad, wide), F32)],
        args=[qkvp, qkvp, qkvp, biasm, o, lse, do], comm=comm)


def _dw_in(h, dproj, s):
    def body(a_ref, b_ref, o_ref):
        acc = _dot_tn(a_ref[...], b_ref[...])
        o_ref[0] = acc[:, 0:PROJ_SHARD].astype(BF16)
        o_ref[1] = acc[:, PROJ_SHARD:2 * PROJ_SHARD].astype(BF16)

    out, = _call(
        body, name="dw_in", grid=(4,),
        in_specs=[pl.BlockSpec((s, D_MODEL), lambda n: (0, 0)),
                  pl.BlockSpec((s, 2 * PROJ_SHARD), lambda n: (0, n))],
        out_specs=[pl.BlockSpec((2, D_MODEL, PROJ_SHARD), lambda n: (n, 0, 0))],
        out_shape=[jax.ShapeDtypeStruct((N_DEV, D_MODEL, PROJ_SHARD), BF16)], args=[h, dproj])
    return out


def _in_proj_bwd(dproj, win, x, g3, dres, l, s, tm, f_prev=None, g_post3=None, comm=None):
    chain = f_prev is not None

    def body(d_ref, w_ref, x_ref, g_ref, dres_ref, *rest):
        if chain:
            f_ref, gq_ref, dx_ref, dg_ref, df_ref, dgq_ref = rest
            _zero_first((dg_ref, dgq_ref), pl.program_id(0) == 0)
        else:
            dx_ref, dg_ref = rest
            _zero_first((dg_ref,), pl.program_id(0) == 0)
        w = jnp.concatenate([w_ref[j] for j in range(N_DEV)], axis=1)
        for rs in _row_subtiles(tm, SUB_ROWS):
            dh = _dot_nt(d_ref[rs, :], w)
            dx, dyn = _norm_bwd_rows(x_ref[rs, :], g_ref[...], dh)
            dx = dres_ref[rs, :] + dx
            dx_ref[rs, :] = dx
            _add_cols(dg_ref, dyn)
            if chain:
                df, dyn2 = _norm_bwd_rows(f_ref[rs, :], gq_ref[...], dx)
                df_ref[rs, :] = df.astype(BF16)
                _add_cols(dgq_ref, dyn2)

    row = pl.BlockSpec((tm, D_MODEL), lambda i: (i, 0))
    dgs = pl.BlockSpec((8, D_MODEL), lambda i: (0, 0))
    in_specs = [pl.BlockSpec((tm, PROJ_WIDTH), lambda i: (i, 0)),
                pl.BlockSpec((N_DEV, D_MODEL, PROJ_SHARD), lambda i: (0, 0, 0), pipeline_mode=pl.Buffered(1)),
                row, pl.BlockSpec((None, 1, D_MODEL), lambda i: (l, 0, 0)), row]
    out_specs = [row, dgs]
    out_shape = [jax.ShapeDtypeStruct((s, D_MODEL), F32), jax.ShapeDtypeStruct((8, D_MODEL), F32)]
    args = [dproj, win, x, g3, dres]
    if chain:
        in_specs += [row, pl.BlockSpec((None, 1, D_MODEL), lambda i: (l - 1, 0, 0))]
        out_specs += [row, dgs]
        out_shape += [jax.ShapeDtypeStruct((s, D_MODEL), BF16), jax.ShapeDtypeStruct((8, D_MODEL), F32)]
        args += [f_prev, g_post3]
    return _call(body, name="in_proj_bwd", grid=(s // tm,), in_specs=in_specs, out_specs=out_specs,
                 out_shape=out_shape, args=args, comm=comm)


def _adamw(name, w, m, v, lands, owns=None, me=None):
    groups, rows, cols = w.shape
    assert len(lands) == groups
    n_part = lands[0].shape[0]
    tr = _row_tile(rows, tuple(c for c in (512, 352, 256, 176, 128, 64, 32, 16, 8) if c * cols <= 256 * 1024))
    c1 = 1.0 - ADAM_B1 ** ADAM_STEP
    c2 = 1.0 - ADAM_B2 ** ADAM_STEP
    n_own = groups if owns is not None else 0

    def body(*refs):
        if n_own:
            me_ref, refs = refs[0], refs[1:]
        w_ref, m_ref, v_ref = refs[:3]
        land_refs = refs[3:3 + groups]
        own_refs = refs[3 + groups:3 + groups + n_own]
        g_ref, d_ref, nm_ref, nv_ref = refs[3 + groups + n_own:]
        grp = pl.program_id(0)
        for gi in range(groups):
            @pl.when(grp == gi)
            def _():
                l_ref = land_refs[gi]
                g = None
                for p in range(n_part):
                    part = l_ref[p].astype(F32)
                    if n_own:
                        part = jnp.where(me_ref[0] == p, own_refs[gi][...].astype(F32), part)
                    g = part if g is None else g + part
                g_ref[...] = g
                m1 = ADAM_B1 * m_ref[...] + (1.0 - ADAM_B1) * g
                v1 = ADAM_B2 * v_ref[...] + (1.0 - ADAM_B2) * (g * g)
                nm_ref[...] = m1
                nv_ref[...] = v1
                d_ref[...] = -ADAM_LR * ((m1 / c1) / (jnp.sqrt(v1 / c2) + ADAM_EPS) + ADAM_WD * w_ref[...])

    blk = pl.BlockSpec((None, tr, cols), lambda g, i, *_: (g, i, 0))
    shp = jax.ShapeDtypeStruct((groups, rows, cols), F32)

    def land_spec(gi):
        return pl.BlockSpec((n_part, tr, cols), lambda g, i, *_: (0, jnp.where(g == gi, i, 0), 0))

    def own_spec(gi):
        if owns[gi].ndim == 3:
            return pl.BlockSpec((None, tr, cols), lambda g, i, me_ref: (me_ref[0], jnp.where(g == gi, i, 0), 0))
        return pl.BlockSpec((tr, cols), lambda g, i, me_ref: (jnp.where(g == gi, i, 0), 0))

    in_specs = [blk, blk, blk] + [land_spec(gi) for gi in range(groups)] + [own_spec(gi) for gi in range(n_own)]
    args = [w, m, v] + list(lands) + (list(owns) if n_own else [])
    if not n_own:
        return _call(body, name=name, grid=(groups, rows // tr), in_specs=in_specs,
                     out_specs=[blk, blk, blk, blk], out_shape=[shp, shp, shp, shp], args=args)
    return pl.pallas_call(
        body, name=name,
        grid_spec=pltpu.PrefetchScalarGridSpec(
            num_scalar_prefetch=1, grid=(groups, rows // tr), in_specs=in_specs, out_specs=[blk, blk, blk, blk]),
        out_shape=[shp, shp, shp, shp],
        compiler_params=pltpu.CompilerParams(dimension_semantics=("arbitrary", "arbitrary"),
                                             vmem_limit_bytes=VMEM_LIMIT),
    )(me, *args)


def _pack_small(rel, gco, gao, gpm, gqm, gpf, gqf):
    n_layers = rel.shape[0]
    relp = jnp.pad(rel, ((0, 0), (0, 0), (0, REL_PAD - rel.shape[2])))
    parts = [relp.reshape(n_layers * N_HEADS * REL_PAD // 128, 128)]
    parts += [a.reshape(-1, 128) for a in (gco, gao, gpm, gqm, gpf, gqf)]
    return jnp.concatenate(parts, axis=0)


def _pack_small_grads(d_rel, parts):
    n_layers = len(d_rel)
    keys = ("gco", "gao", "gpm", "gqm", "gpf", "gqf")
    arrays = list(d_rel) + [parts[k][l] for k in keys for l in range(n_layers)] + list(parts["wc"])
    rows = 0
    plan = []
    for l in range(n_layers):
        for h in range(N_HEADS):
            for t in range(REL_PAD // 128):
                plan.append((l, (0, h), t, rows))
                rows += 1
    for ki, k in enumerate(keys):
        for l in range(n_layers):
            for t in range(parts[k][l].shape[1] // 128):
                plan.append((n_layers * (1 + ki) + l, (0,), t, rows))
                rows += 1
    for l in range(n_layers):
        for tap in range(3):
            for t in range(CONV_WIDTH // 128):
                plan.append((n_layers * (1 + len(keys)) + l, (tap,), t, rows))
                rows += 1
    total = rows + (-rows) % 8

    def body(*refs):
        o_ref = refs[-1]
        if total > rows:
            o_ref[rows:total, :] = jnp.zeros((total - rows, 128), F32)
        for op, idx, t, dst in plan:
            lanes = slice(128 * t, 128 * (t + 1))
            if len(idx) == 2:
                o_ref[dst:dst + 1, :] = refs[op][idx[0], idx[1]:idx[1] + 1, lanes]
            else:
                o_ref[dst:dst + 1, :] = refs[op][idx[0]:idx[0] + 1, lanes]

    vmem = pl.BlockSpec(memory_space=pltpu.VMEM)
    out, = _call(body, name="pack_small_grads", grid=(), in_specs=[vmem] * len(arrays), out_specs=[vmem],
                 out_shape=[jax.ShapeDtypeStruct((total, 128), F32)], args=arrays)
    return out


def _unpack_small(p, n_layers):
    n_rel = n_layers * N_HEADS * REL_PAD // 128
    rel = p[:n_rel].reshape(n_layers, N_HEADS, REL_PAD)[:, :, :2 * REL_CLIP + 1]
    outs = [rel]
    r0 = n_rel
    for width in (CONV_WIDTH, ATTN_WIDTH, D_MODEL, D_MODEL, D_MODEL, D_MODEL):
        nr = n_layers * width // 128
        outs.append(p[r0:r0 + nr].reshape(n_layers, width))
        r0 += nr
    return outs


def kernel(x, w_in, w_conv, rel_bias, g_conv_out, g_attn_out, w_out, g_pre_mix, g_post_mix, g_pre_ffn, g_post_ffn, w_ffn_in, w_ffn_out, loss_target, m_w_in, m_w_conv, m_rel_bias, m_g_conv_out, m_g_attn_out, m_w_out, m_g_pre_mix, m_g_post_mix, m_g_pre_ffn, m_g_post_ffn, m_w_ffn_in, m_w_ffn_out, v_w_in, v_w_conv, v_rel_bias, v_g_conv_out, v_g_attn_out, v_w_out, v_g_pre_mix, v_g_post_mix, v_g_pre_ffn, v_g_post_ffn, v_w_ffn_in, v_w_ffn_out):
    n_layers = w_in.shape[0]
    s = x.shape[1]
    assert x.shape == (1, s, D_MODEL) and s % 1024 == 0
    assert w_in.shape == (n_layers, D_MODEL, PROJ_SHARD) and w_ffn_in.shape == (n_layers, D_MODEL, FF_SHARD)
    tm = 512
    tq = 1024 if s >= 2048 else 512
    tf = min(1024, s)
    x0 = x.reshape(s, D_MODEL)
    target = loss_target.reshape(s, D_MODEL)
    dev = _dev_index(lax.axis_index("x"), lax.axis_index("y"), lax.axis_index("c"))

    wt_ffn_in, mt_ffn_in, vt_ffn_in = (jnp.transpose(a, (0, 2, 1)) for a in (w_ffn_in, m_w_ffn_in, v_w_ffn_in))
    local_w = [_cast_bf16(w_in, "cast_w_in"), _cast_bf16(w_out, "cast_w_out"),
               _cast_bf16(wt_ffn_in, "cast_w_ffn_in"), _cast_bf16(w_ffn_out, "cast_w_ffn_out")]
    wc_local = jnp.pad(jnp.transpose(w_conv, (0, 2, 1)).reshape(-1), (0, 1024 - n_layers * 3 * 64)).reshape(8, 128)
    biasm, win_next, wc_g = _bias_build(jnp.pad(rel_bias, ((0, 0), (0, 0), (0, REL_PAD - rel_bias.shape[2]))),
                                        comm=_Gather([(local_w[0], 0), (wc_local, None)]))
    weights = [None] * n_layers
    wc_full = wc_g.reshape(N_DEV, 1024)[:, :n_layers * 3 * 64].reshape(N_DEV, n_layers, 3, 64)
    wc_full = jnp.transpose(wc_full, (1, 2, 0, 3)).reshape(n_layers, 3, CONV_WIDTH)
    wc_full = jnp.pad(wc_full, ((0, 0), (0, 5), (0, 0)))

    g3 = {k: v.reshape(n_layers, 1, -1) for k, v in dict(
        conv=g_conv_out, attn=g_attn_out, pre_mix=g_pre_mix, post_mix=g_post_mix,
        pre_ffn=g_pre_ffn, post_ffn=g_post_ffn).items()}

    saved = []
    xl = x0
    h = _norm_cast(x0, g3["pre_mix"], 0, tm)
    for l in range(n_layers):
        win = win_next
        pc, wout = _in_proj(h, win, s, tq, 0, comm=_Gather([(local_w[1], l)]))
        qkvp, = _in_proj(h, win, s, tq, 1)
        ync = _conv_fwd(pc, wc_full, g3["conv"], l, s, tm)
        o, lse, yna, wfin = _attn_fwd(qkvp, biasm, g3["attn"], l, s, tq, comm=_Gather([(local_w[2], l)]))
        wout = wout.reshape(D_MODEL, D_MODEL)
        z, xm, h2 = _out_proj_fwd(ync, yna, wout, xl, g3["post_mix"], g3["pre_ffn"], l, s, tq)
        gu, act, wfout = _ffn_in_fwd(h2, wfin.reshape(4, FF_PAIR, D_MODEL), s, tf, comm=_Gather([(local_w[3], l)]))
        wfo = wfout.reshape(D_FF, D_MODEL)
        l_next = min(l + 1, n_layers - 1)
        f, xo, h_next, *got = _ffn_out_fwd(act, wfo, xm, g3["post_ffn"], g3["pre_mix"], l, l_next, s, tm,
                                           comm=_Gather([(local_w[0], l + 1)]) if l + 1 < n_layers else None)
        weights[l] = [win, wout, wfin.reshape(2 * D_FF, D_MODEL), wfo]
        win_next = got[0] if got else None
        saved.append(dict(x=xl, h=h, pc=pc, qkvp=qkvp, ync=ync, yna=yna, o=o, lse=lse, z=z, xm=xm,
                          h2=h2, gu=gu, act=act, f=f))
        xl, h = xo, h_next

    dx, sq, df, dg_post_ffn = _loss_grad(xl, target, saved[-1]["f"], g3["post_ffn"], n_layers - 1, s, tm)
    loss = lax.psum(jnp.sum(sq) * (0.5 / D_MODEL), ("x", "y", "c"))

    lands = dict(win=[None] * n_layers, wout=[None] * n_layers, wfin=[None] * n_layers, wfout=[None] * n_layers)
    small = {k: [None] * n_layers for k in ("gco", "gao", "gpm", "gqm", "gpf", "gqf", "wc")}
    d_rel = [None] * n_layers
    started = []

    def start(name, keys, l, arrays):
        items = [(a, False) for a in arrays]
        send_sems, recv_sems, srcs, zones, token = _exchange_start(name + "_start", items)
        started.append((name, keys, l, items, send_sems, recv_sems, srcs, zones))
        return token[0:1, 0:1].reshape(1, 1, 1)

    for l in reversed(range(n_layers)):
        sv = saved[l]
        win, wout, wfin, wfo = weights[l]
        small["gqf"][l] = dg_post_ffn
        dgu, = _ffn_out_bwd(df, wfo, sv["gu"], s, tf)
        d_wfout = _dw_ffn_out(sv["act"], df, s).reshape(N_DEV, FFO_SHARD, D_MODEL)
        d_wfin = _dw_ffn_in(sv["h2"], dgu, s).reshape(N_DEV, FF_SHARD, D_MODEL)
        token = start(f"exchange_ffn{l}", ("wfout", "wfin"), l, [d_wfout, d_wfin])
        dxm, dz, dg_pre_ffn, dg_post_mix = _ffn_in_bwd(
            dgu, wfin, sv["xm"], g3["pre_ffn"] + token, dx, sv["z"], g3["post_mix"], l, s, tm)
        small["gpf"][l] = dg_pre_ffn
        small["gqm"][l] = dg_post_mix
        d_wout = _dw_out(sv["ync"], sv["yna"], dz, s).reshape(N_DEV, D_MODEL // N_DEV, D_MODEL)
        dyc, do, dg_attn = _out_proj_bwd(dz, wout, sv["o"], g3["attn"], l, s, tq)
        small["gao"][l] = dg_attn
        dq, dk, dv, ds_sum = _attn_bwd(sv["qkvp"], biasm, sv["o"], sv["lse"], do, l, s, tq)
        d_rel[l] = _bias_bwd(ds_sum[None])
        dproj, dwc, dg_conv = _conv_bwd(sv["pc"], dyc, wc_full, g3["conv"], dq, dk, dv, l, s, tm)
        small["wc"][l] = dwc
        small["gco"][l] = dg_conv
        d_win = _dw_in(sv["h"], dproj, s)
        token = start(f"exchange_mix{l}", ("wout", "win"), l, [d_wout, d_win])
        if l > 0:
            dx, dg_pre_mix, df, dg_post_ffn = _in_proj_bwd(
                dproj, win, sv["x"], g3["pre_mix"] + token, dxm, l, s, tm, f_prev=saved[l - 1]["f"],
                g_post3=g3["post_ffn"])
        else:
            dx, dg_pre_mix = _in_proj_bwd(dproj, win, sv["x"], g3["pre_mix"] + token, dxm, l, s, tm)
        small["gpm"][l] = dg_pre_mix
    grad_x = dx.reshape(1, s, D_MODEL)

    small_vec = _pack_small_grads(d_rel, small)
    small_items = [(small_vec, True)]
    small_sems = _exchange_start("exchange_small_start", small_items)

    owns = dict(win=[None] * n_layers, wout=[None] * n_layers, wfin=[None] * n_layers, wfout=[None] * n_layers)

    def wait(which, after):
        for name, keys, l, items, send_sems, recv_sems, srcs, zones in started:
            if keys == which:
                srcs, zones = _exchange_wait(name + "_wait", items, send_sems, recv_sems, srcs, zones, after)
                for key, src, zone in zip(keys, srcs, zones):
                    owns[key][l], lands[key][l] = src, zone

    me = dev.astype(jnp.int32).reshape(1)
    wait(("wfout", "wfin"), small_sems[4])
    r_fin = [jnp.transpose(t, (0, 2, 1)) for t in _adamw(
        "adamw_w_ffn_in", wt_ffn_in, mt_ffn_in, vt_ffn_in, lands["wfin"], owns["wfin"], me)]
    r_fout = _adamw("adamw_w_ffn_out", w_ffn_out, m_w_ffn_out, v_w_ffn_out, lands["wfout"], owns["wfout"], me)
    wait(("wout", "win"), r_fout[0])
    r_out = _adamw("adamw_w_out", w_out, m_w_out, v_w_out, lands["wout"], owns["wout"], me)
    r_in = _adamw("adamw_w_in", w_in, m_w_in, v_w_in, lands["win"], owns["win"], me)
    (small_own,), (land_small,) = _exchange_wait(
        "exchange_small_wait", small_items, small_sems[0], small_sems[1], small_sems[2], small_sems[3], r_in[0])

    n_rep = 64 * n_layers
    rep = _adamw(
        "adamw_replicated",
        _pack_small(rel_bias, g_conv_out, g_attn_out, g_pre_mix, g_post_mix, g_pre_ffn, g_post_ffn)[None],
        _pack_small(m_rel_bias, m_g_conv_out, m_g_attn_out, m_g_pre_mix, m_g_post_mix, m_g_pre_ffn, m_g_post_ffn)[None],
        _pack_small(v_rel_bias, v_g_conv_out, v_g_attn_out, v_g_pre_mix, v_g_post_mix, v_g_pre_ffn, v_g_post_ffn)[None],
        [land_small[:, :n_rep]], [small_own[:n_rep]], me)
    rep = [_unpack_small(t[0], n_layers) for t in rep]

    wc_rows = n_layers * 3 * CONV_WIDTH // 128
    zeros_wc = jnp.zeros((1, wc_rows, 128), F32)
    g_wc_full = _adamw("sum_w_conv", zeros_wc, zeros_wc, zeros_wc, [land_small[:, n_rep:n_rep + wc_rows]],
                       [small_own[n_rep:n_rep + wc_rows]], me)[0]
    g_wc_full = g_wc_full.reshape(n_layers, 3, CONV_WIDTH)
    g_wc = lax.dynamic_slice_in_dim(g_wc_full, dev * (CONV_WIDTH // N_DEV), CONV_WIDTH // N_DEV, axis=2)
    g_wc = jnp.transpose(g_wc, (0, 2, 1))

    def tiny(a):
        flat = a.reshape(-1)
        return jnp.pad(flat, (0, (-flat.shape[0]) % 1024)).reshape(1, -1, 128)

    r_wc = _adamw("adamw_w_conv", tiny(w_conv), tiny(m_w_conv), tiny(v_w_conv), [tiny(g_wc)])
    r_wc = [t.reshape(-1)[:w_conv.size].reshape(w_conv.shape) for t in r_wc]

    def leaf(kind):
        return [r_in[kind], r_wc[kind], rep[kind][0], rep[kind][1], rep[kind][2], r_out[kind],
                rep[kind][3], rep[kind][4], rep[kind][5], rep[kind][6], r_fin[kind], r_fout[kind]]

    return (loss, grad_x, *leaf(0), *leaf(1), *leaf(2), *leaf(3))
```

```python
import math

import jax
import jax.numpy as jnp
from jax import lax
from jax.experimental import pallas as pl
from jax.experimental.pallas import tpu as pltpu

F32 = jnp.float32
BF16 = jnp.bfloat16

D_MODEL = 1024
N_DEV = 8
CHUNK = 64
N_LEFT_CHUNKS = 8
CONV_WIDTH = 512
ATTN_WIDTH = 512
HEAD_DIM = 64
N_HEADS = 8
REL_CLIP = 128
REL_PAD = 384
PROJ_WIDTH = 3072
PROJ_SHARD = PROJ_WIDTH // N_DEV
D_FF = 2816
FF_SHARD = 2 * D_FF // N_DEV
FFO_SHARD = D_FF // N_DEV
FF_PAIR = 2 * FF_SHARD
_COL_SUBTILES = (slice(0, 768), slice(768, FF_PAIR))
EPS = 1e-6
NEG_INF = -1e30
ATTN_PAIRS = 2
Q_BLOCK = 4 * CHUNK
K_BAND = Q_BLOCK + N_LEFT_CHUNKS * CHUNK
LEFT = N_LEFT_CHUNKS * CHUNK
TOEP = 1024

ADAM_LR = 0.001
ADAM_B1 = 0.9
ADAM_B2 = 0.999
ADAM_EPS = 1e-08
ADAM_WD = 0.01
ADAM_STEP = 10

VMEM_LIMIT = 52 * 1024 * 1024
SUB_ROWS = 256
MESH = pl.DeviceIdType.MESH
ANY = pl.BlockSpec(memory_space=pl.ANY)

NT = (((1,), (1,)), ((), ()))
TN = (((0,), (0,)), ((), ()))


def _dot(a, b):
    return jnp.dot(a, b, preferred_element_type=F32)


def _dot_nt(a, b):
    return lax.dot_general(a, b, NT, preferred_element_type=F32)


def _dot_tn(a, b):
    return lax.dot_general(a, b, TN, preferred_element_type=F32)


def _rstd(v):
    return lax.rsqrt(jnp.mean(v * v, axis=-1, keepdims=True) + EPS)


def _group_matrix():
    r = lax.broadcasted_iota(jnp.int32, (128, 128), 0) >> 6
    c = lax.broadcasted_iota(jnp.int32, (128, 128), 1) >> 6
    return jnp.where(r == c, 1.0, 0.0).astype(BF16)


def _group_mean(v, gmat):
    hi = v.astype(BF16)
    lo = (v - hi.astype(F32)).astype(BF16)
    return (_dot(hi, gmat) + _dot(lo, gmat)) * (1.0 / HEAD_DIM)


def _split3(v):
    hi = v.astype(BF16)
    r1 = v - hi.astype(F32)
    mid = r1.astype(BF16)
    lo = (r1 - mid.astype(F32)).astype(BF16)
    return hi, mid, lo


def _row_tile(rows, cands=(1024, 512, 704, 256, 128, 64, 32, 16)):
    for c in cands:
        if rows % c == 0:
            return c
    return rows


def _dev_index(px, py, pc):
    return 4 * px + 2 * py + pc


def _when(cond):
    if cond is True:
        return lambda fn: fn()
    return pl.when(cond)


def _phases(grid):
    def phases():
        if not grid:
            return True, True, True
        lin = pl.program_id(0)
        for a in range(1, len(grid)):
            lin = lin * grid[a] + pl.program_id(a)
        total = math.prod(grid)
        return lin == 0, lin == (3 * total) // 4, lin == total - 1
    return phases


class _Gather:
    def __init__(self, items):
        self.items = items
        self.args = [a for a, _ in items]
        n = len(items)
        self.out_shape = [jax.ShapeDtypeStruct((N_DEV,) + (a.shape if lay is None else a.shape[1:]), a.dtype)
                          for a, lay in items]
        self.scratch = [pltpu.SemaphoreType.DMA((n, 7)), pltpu.SemaphoreType.DMA((n, 7)),
                        pltpu.SemaphoreType.DMA((n,))]

    def _ctx(self, ins, outs, sems):
        send_sems, recv_sems, local_sems = sems
        x, y, c = lax.axis_index("x"), lax.axis_index("y"), lax.axis_index("c")
        chips = [(1 - x, y), (x, 1 - y), (1 - x, 1 - y)]

        def src(k):
            lay = self.items[k][1]
            return ins[k] if lay is None else ins[k].at[lay]

        def copy(k, s, idx, to, from_src=False):
            return pltpu.make_async_remote_copy(
                src_ref=src(k) if from_src else outs[k].at[idx], dst_ref=outs[k].at[idx],
                send_sem=send_sems.at[k, s], recv_sem=recv_sems.at[k, s],
                device_id=to, device_id_type=MESH)

        def local(k):
            return pltpu.make_async_copy(src(k), outs[k].at[_dev_index(x, y, c)], local_sems.at[k])

        return x, y, c, chips, copy, local

    def start(self, ins, outs, sems, cond):
        n = len(self.items)

        @_when(cond)
        def _():
            x, y, c, chips, copy, local = self._ctx(ins, outs, sems)
            me = _dev_index(x, y, c)
            for k in range(n):
                local(k).start()
                copy(k, 0, me, (x, y, 1 - c), from_src=True).start()
                for j, chip in enumerate(chips):
                    copy(k, 1 + j, me, (chip[0], chip[1], c), from_src=True).start()

    def forward(self, ins, outs, sems, cond):
        n = len(self.items)

        @_when(cond)
        def _():
            x, y, c, chips, copy, local = self._ctx(ins, outs, sems)
            for j, chip in enumerate(chips):
                idx = _dev_index(chip[0], chip[1], c)
                for k in range(n):
                    copy(k, 1 + j, idx, (x, y, c)).wait_recv()
                    copy(k, 4 + j, idx, (x, y, 1 - c)).start()

    def finish(self, ins, outs, sems, cond):
        n = len(self.items)

        @_when(cond)
        def _():
            x, y, c, chips, copy, local = self._ctx(ins, outs, sems)
            me = _dev_index(x, y, c)
            for k in range(n):
                copy(k, 0, _dev_index(x, y, 1 - c), (x, y, c)).wait_recv()
            for j, chip in enumerate(chips):
                idx = _dev_index(chip[0], chip[1], 1 - c)
                for k in range(n):
                    copy(k, 4 + j, idx, (x, y, c)).wait_recv()
            for k in range(n):
                for s in range(4):
                    copy(k, s, me, (x, y, c), from_src=True).wait_send()
                for j, chip in enumerate(chips):
                    copy(k, 4 + j, _dev_index(chip[0], chip[1], c), (x, y, c)).wait_send()
                local(k).wait()


_PEER_FLIPS = [(0, 0, 1), (1, 0, 0), (0, 1, 0), (1, 1, 0), (1, 0, 1), (0, 1, 1), (1, 1, 1)]


def _call(body, *, name, grid, in_specs, out_specs, out_shape, args, scratch=(), comm=None):
    n_hi, n_ho, n_hs = len(args), len(out_shape), len(scratch)
    c_args = list(comm.args) if comm else []
    c_out = list(comm.out_shape) if comm else []
    c_scr = list(comm.scratch) if comm else []
    phases = _phases(grid)

    def kern(*refs):
        cuts = [n_hi, len(c_args), n_ho, len(c_out), n_hs, len(c_scr)]
        parts, pos = [], 0
        for n in cuts:
            parts.append(refs[pos:pos + n])
            pos += n
        hi, ci, ho, co, hs, cs = parts
        if comm:
            first, mid, last = phases()
            comm.start(ci, co, cs, first)
            comm.forward(ci, co, cs, mid)
        body(*hi, *ho, *hs)
        if comm:
            comm.finish(ci, co, cs, last)

    sem = ("arbitrary",) * len(grid) if grid else None
    return pl.pallas_call(
        kern, name=name, grid=grid,
        in_specs=list(in_specs) + [ANY] * len(c_args),
        out_specs=list(out_specs) + [ANY] * len(c_out),
        out_shape=list(out_shape) + c_out,
        scratch_shapes=list(scratch) + c_scr,
        compiler_params=pltpu.CompilerParams(dimension_semantics=sem, vmem_limit_bytes=VMEM_LIMIT),
    )(*args, *c_args)


def _comm_only(name, comm):
    return _call(lambda: None, name=name, grid=(), in_specs=[], out_specs=[], out_shape=[], args=[], comm=comm)


HBM_SPEC = pl.BlockSpec(memory_space=pltpu.HBM)
SEM_SPEC = pl.BlockSpec(memory_space=pltpu.SEMAPHORE)
SIDE_EFFECT = pltpu.SideEffectType.DATAFLOW_SIDE_EFFECTING


def _exchange_peer(x, y, c, s):
    fx, fy, fc = _PEER_FLIPS[s]
    return x ^ fx, y ^ fy, c ^ fc


def _exchange_start(name, items):
    n = len(items)
    srcs = [pltpu.with_memory_space_constraint(a, pltpu.HBM) for a, _ in items]
    land_shapes = [(N_DEV,) + (a.shape if whole else a.shape[1:]) for a, whole in items]
    lands = [pltpu.with_memory_space_constraint(lax.empty(shp, a.dtype), pltpu.HBM)
             for shp, (a, _) in zip(land_shapes, items)]

    n_sem = 7 * n

    def body(*refs):
        src_refs, land_refs = refs[:n], refs[n:2 * n]
        send_sems = refs[2 * n:2 * n + n_sem]
        recv_sems = refs[2 * n + n_sem:2 * n + 2 * n_sem]
        token = refs[-1]
        x, y, c = lax.axis_index("x"), lax.axis_index("y"), lax.axis_index("c")
        me = _dev_index(x, y, c)
        for s in range(7):
            px, py, pc = _exchange_peer(x, y, c, s)
            for k in range(n):
                src = src_refs[k] if items[k][1] else src_refs[k].at[_dev_index(px, py, pc)]
                pltpu.make_async_remote_copy(
                    src_ref=src, dst_ref=land_refs[k].at[me],
                    send_sem=send_sems[7 * k + s], recv_sem=recv_sems[7 * k + s],
                    device_id=(px, py, pc), device_id_type=MESH).start()
        token[...] = jnp.zeros(token.shape, token.dtype)

    outs = pl.pallas_call(
        body, name=name,
        out_shape=(*[pltpu.SemaphoreType.DMA(())] * (2 * n_sem),
                   *[pltpu.HBM(a.shape, a.dtype) for a in srcs],
                   *[pltpu.HBM(shp, a.dtype) for shp, a in zip(land_shapes, srcs)],
                   jax.ShapeDtypeStruct((8, 128), F32)),
        in_specs=[HBM_SPEC] * (2 * n),
        out_specs=(*[SEM_SPEC] * (2 * n_sem), *[HBM_SPEC] * (2 * n), pl.BlockSpec(memory_space=pltpu.VMEM)),
        input_output_aliases={i: 2 * n_sem + i for i in range(2 * n)},
        compiler_params=pltpu.CompilerParams(has_side_effects=SIDE_EFFECT),
    )(*srcs, *lands)
    base = 2 * n_sem
    return (list(outs[:n_sem]), list(outs[n_sem:base]), list(outs[base:base + n]),
            list(outs[base + n:base + 2 * n]), outs[-1])


def _exchange_wait(name, items, send_sems, recv_sems, srcs, lands, after):
    n = len(items)

    n_sem = 7 * n

    def body(*refs):
        src_refs, land_refs = refs[:n], refs[n:2 * n]
        send_refs = refs[2 * n:2 * n + n_sem]
        recv_refs = refs[2 * n + n_sem:2 * n + 2 * n_sem]
        x, y, c = lax.axis_index("x"), lax.axis_index("y"), lax.axis_index("c")
        for s in range(7):
            for k in range(n):
                copy = pltpu.make_async_remote_copy(
                    src_ref=src_refs[k] if items[k][1] else src_refs[k].at[0], dst_ref=land_refs[k].at[0],
                    send_sem=send_refs[7 * k + s], recv_sem=recv_refs[7 * k + s],
                    device_id=(x, y, c), device_id_type=MESH)
                copy.wait_send()
                copy.wait_recv()

    outs = pl.pallas_call(
        body, name=name,
        out_shape=(*[pltpu.HBM(a.shape, a.dtype) for a in srcs], *[pltpu.HBM(a.shape, a.dtype) for a in lands]),
        in_specs=[HBM_SPEC] * (2 * n) + [SEM_SPEC] * (2 * n_sem) + [ANY],
        out_specs=tuple([HBM_SPEC] * (2 * n)),
        input_output_aliases={i: i for i in range(2 * n)},
        compiler_params=pltpu.CompilerParams(has_side_effects=SIDE_EFFECT),
    )(*srcs, *lands, *send_sems, *recv_sems, after)
    return list(outs[:n]), list(outs[n:])


def _cast_bf16(x, name):
    shape = x.shape
    x2 = x.reshape(-1, shape[-1])
    rows, cols = x2.shape
    tr = _row_tile(rows)

    def body(x_ref, o_ref):
        o_ref[...] = x_ref[...].astype(BF16)

    blk = pl.BlockSpec((tr, cols), lambda i: (i, 0))
    out, = _call(body, name=name, grid=(rows // tr,), in_specs=[blk], out_specs=[blk],
                 out_shape=[jax.ShapeDtypeStruct((rows, cols), BF16)], args=[x2])
    return out.reshape(shape)


def _norm_cast(x, g3, l, tm):
    s = x.shape[0]

    def body(x_ref, g_ref, o_ref):
        v = x_ref[...]
        o_ref[...] = (v * _rstd(v) * g_ref[...]).astype(BF16)

    row = pl.BlockSpec((tm, D_MODEL), lambda i: (i, 0))
    out, = _call(body, name="norm_cast", grid=(s // tm,),
                 in_specs=[row, pl.BlockSpec((None, 1, D_MODEL), lambda i: (l, 0, 0))], out_specs=[row],
                 out_shape=[jax.ShapeDtypeStruct((s, D_MODEL), BF16)], args=[x, g3])
    return out


def _in_proj(h, win, s, tq, part, comm=None):
    pad = part
    dtype = BF16 if part else F32

    def body(a_ref, b_ref, o_ref):
        def compute():
            w = jnp.concatenate([b_ref[j] for j in range(4)], axis=1)
            o_ref[...] = _dot(a_ref[...], w).astype(dtype)

        if pad:
            i = pl.program_id(0)

            @pl.when(i == 0)
            def _():
                o_ref[...] = jnp.zeros(o_ref.shape, dtype)

            pl.when(i > 0)(compute)
        else:
            compute()

    return _call(
        body, name="in_proj_qkv" if part else "in_proj_conv", grid=(s // tq + pad,),
        in_specs=[pl.BlockSpec((tq, D_MODEL), lambda i: (jnp.maximum(i - pad, 0), 0)),
                  pl.BlockSpec((4, D_MODEL, PROJ_SHARD), lambda i: (part, 0, 0))],
        out_specs=[pl.BlockSpec((tq, 4 * PROJ_SHARD), lambda i: (i, 0))],
        out_shape=[jax.ShapeDtypeStruct((s + pad * tq, PROJ_WIDTH // 2), dtype)], args=[h, win], comm=comm)


def _conv_fwd(pc, wc, g3, l, s, tr):
    hb = tr // 8

    def body(pc_ref, prev_ref, wc_ref, g_ref, o_ref):
        i = pl.program_id(0)
        gmat = _group_matrix()
        for j in range(CONV_WIDTH // 128):
            c0, c1, c2 = 128 * j, CONV_WIDTH + 128 * j, 2 * CONV_WIDTH + 128 * j
            hc = pc_ref[:, c0:c0 + 128]
            bg = pc_ref[:, c1:c1 + 128]
            cg = pc_ref[:, c2:c2 + 128]
            u_prev = jnp.where(i > 0, prev_ref[:, c2:c2 + 128] * prev_ref[:, c0:c0 + 128], 0.0)
            u = cg * hc
            full = jnp.concatenate([u_prev, u], axis=0)
            u1 = pltpu.roll(full, 1, 0)[8:]
            u2 = pltpu.roll(full, 2, 0)[8:]
            out = (u2 * wc_ref[0:1, c0:c0 + 128] + u1 * wc_ref[1:2, c0:c0 + 128]
                   + u * wc_ref[2:3, c0:c0 + 128])
            yc = bg * out
            r = lax.rsqrt(_group_mean(yc * yc, gmat) + EPS)
            o_ref[:, c0:c0 + 128] = (yc * r * g_ref[:, c0:c0 + 128]).astype(BF16)

    out, = _call(
        body, name="conv_fwd", grid=(s // tr,),
        in_specs=[pl.BlockSpec((tr, 3 * CONV_WIDTH), lambda i: (i, 0)),
                  pl.BlockSpec((8, 3 * CONV_WIDTH), lambda i: (jnp.maximum(i * hb - 1, 0), 0)),
                  pl.BlockSpec((None, 8, CONV_WIDTH), lambda i: (l, 0, 0)),
                  pl.BlockSpec((None, 1, CONV_WIDTH), lambda i: (l, 0, 0))],
        out_specs=[pl.BlockSpec((tr, CONV_WIDTH), lambda i: (i, 0))],
        out_shape=[jax.ShapeDtypeStruct((s, CONV_WIDTH), BF16)], args=[pc, pc, wc, g3])
    return out


def _toeplitz_source():
    r_i = lax.broadcasted_iota(jnp.int32, (REL_PAD, TOEP), 0)
    m_i = lax.broadcasted_iota(jnp.int32, (REL_PAD, TOEP), 1)
    idx = jnp.clip((K_BAND - 1) - m_i, -REL_CLIP, REL_CLIP) + REL_CLIP
    return jnp.where(r_i == idx, 1.0, 0.0).astype(BF16)


def _bias_build(rbp, comm=None):
    n_layers = rbp.shape[0]

    def body(rb_ref, o_ref, t_ref):
        pmat = _toeplitz_source()
        hi, mid, lo = _split3(rb_ref[...])
        t_ref[...] = _dot(hi, pmat) + _dot(mid, pmat) + _dot(lo, pmat)
        shift = (CHUNK - 1) - lax.broadcasted_iota(jnp.int32, (CHUNK, TOEP), 0)
        kchunk = lax.broadcasted_iota(jnp.int32, (CHUNK, K_BAND), 1) >> 6
        for h in range(N_HEADS):
            b = jnp.broadcast_to(t_ref[pl.ds(h, 1), :], (CHUNK, TOEP))
            for bit in range(6):
                rolled = pltpu.roll(b, TOEP - (1 << bit), 1)
                b = jnp.where(((shift >> bit) & 1) == 1, rolled, b)
            for cq in range(Q_BLOCK // CHUNK):
                off = CHUNK * (Q_BLOCK // CHUNK - 1 - cq)
                band = pltpu.roll(b, TOEP - off, 1) if off else b
                dchunk = kchunk - cq
                in_band = jnp.where(dchunk >= 0, jnp.where(dchunk <= N_LEFT_CHUNKS, 1, 0), 0) == 1
                o_ref[h, CHUNK * cq:CHUNK * (cq + 1), :] = jnp.where(in_band, band[:, :K_BAND], NEG_INF)

    return _call(
        body, name="bias_build", grid=(n_layers,),
        in_specs=[pl.BlockSpec((None, N_HEADS, REL_PAD), lambda l: (l, 0, 0))],
        out_specs=[pl.BlockSpec((None, N_HEADS, Q_BLOCK, K_BAND), lambda l: (l, 0, 0, 0))],
        out_shape=[jax.ShapeDtypeStruct((n_layers, N_HEADS, Q_BLOCK, K_BAND), F32)],
        scratch=[pltpu.VMEM((N_HEADS, TOEP), F32)], args=[rbp], comm=comm)


def _bias_bwd(ds_sum):
    n_layers = ds_sum.shape[0]

    def body(ds_ref, o_ref, t_ref):
        pmat = _toeplitz_source()
        shift = (CHUNK - 1) - lax.broadcasted_iota(jnp.int32, (CHUNK, TOEP), 0)
        for h in range(N_HEADS):
            d = None
            for cq in range(Q_BLOCK // CHUNK):
                off = CHUNK * (Q_BLOCK // CHUNK - 1 - cq)
                part = jnp.concatenate([ds_ref[h, CHUNK * cq:CHUNK * (cq + 1), :],
                                        jnp.zeros((CHUNK, TOEP - K_BAND), F32)], axis=1)
                part = pltpu.roll(part, off, 1) if off else part
                d = part if d is None else d + part
            for bit in range(6):
                rolled = pltpu.roll(d, 1 << bit, 1)
                d = jnp.where(((shift >> bit) & 1) == 1, rolled, d)
            t_ref[pl.ds(h, 1), :] = jnp.sum(d, axis=0, keepdims=True)
        hi, mid, lo = _split3(t_ref[...])
        o_ref[...] = _dot_nt(hi, pmat) + _dot_nt(mid, pmat) + _dot_nt(lo, pmat)

    out, = _call(
        body, name="bias_bwd", grid=(n_layers,),
        in_specs=[pl.BlockSpec((None, N_HEADS, Q_BLOCK, K_BAND), lambda l: (l, 0, 0, 0))],
        out_specs=[pl.BlockSpec((None, N_HEADS, REL_PAD), lambda l: (l, 0, 0))],
        out_shape=[jax.ShapeDtypeStruct((n_layers, N_HEADS, REL_PAD), F32)],
        scratch=[pltpu.VMEM((N_HEADS, TOEP), F32)], args=[ds_sum])
    return out


def _attn_fwd(qkvp, biasm, g3, l, s, pad, comm=None):
    nb = s // Q_BLOCK
    qb0 = pad // Q_BLOCK
    scale = HEAD_DIM ** -0.5
    wide = 128 * ATTN_PAIRS

    def body(q_ref, k_ref, v_ref, b_ref, g_ref, o_ref, lse_ref, yn_ref):
        blk = pl.program_id(1)
        koff = pl.multiple_of(blk * Q_BLOCK + (pad - LEFT), Q_BLOCK)
        lane = lax.broadcasted_iota(jnp.int32, (1, 128), 1)
        kpos = lax.broadcasted_iota(jnp.int32, (1, K_BAND), 1) + (blk * Q_BLOCK - LEFT)
        kmask = jnp.where(kpos >= 0, 0.0, NEG_INF)
        gmat = _group_matrix()
        for pr in range(ATTN_PAIRS):
            ls = slice(128 * pr, 128 * (pr + 1))
            q = q_ref[:, ls]
            kb = k_ref[pl.ds(koff, K_BAND), ls]
            vb = v_ref[pl.ds(koff, K_BAND), ls]
            outs, lses = [], []
            for hh in range(2):
                in_head = (lane >> 6) == hh
                qm = jnp.where(in_head, q, jnp.zeros_like(q)) * jnp.asarray(scale, BF16)
                sc = _dot_nt(qm, kb) + b_ref[2 * pr + hh] + kmask
                m = jnp.max(sc, axis=1, keepdims=True)
                e = jnp.exp(sc - m)
                den = jnp.sum(e, axis=1, keepdims=True)
                outs.append(_dot(e.astype(BF16), vb) * (1.0 / den))
                lses.append(m + jnp.log(den))
            first = lane < HEAD_DIM
            o = jnp.where(first, outs[0], outs[1])
            o_ref[:, ls] = o
            lse_ref[:, ls] = jnp.where(first, lses[0], lses[1])
            r = lax.rsqrt(_group_mean(o * o, gmat) + EPS)
            yn_ref[:, ls] = (o * r * g_ref[:, ls]).astype(BF16)

    blk_out = pl.BlockSpec((Q_BLOCK, wide), lambda p, b: (b, p))
    n_grp = ATTN_WIDTH // wide
    return _call(
        body, name="attn_fwd", grid=(n_grp, nb),
        in_specs=[pl.BlockSpec((Q_BLOCK, wide), lambda p, b: (qb0 + b, p)),
                  pl.BlockSpec((s + pad, wide), lambda p, b: (0, n_grp + p)),
                  pl.BlockSpec((s + pad, wide), lambda p, b: (0, 2 * n_grp + p)),
                  pl.BlockSpec((None, 2 * ATTN_PAIRS, Q_BLOCK, K_BAND), lambda p, b: (l, p, 0, 0)),
                  pl.BlockSpec((None, 1, wide), lambda p, b: (l, 0, p))],
        out_specs=[blk_out, blk_out, blk_out],
        out_shape=[jax.ShapeDtypeStruct((s, ATTN_WIDTH), F32),
                   jax.ShapeDtypeStruct((s, ATTN_WIDTH), F32),
                   jax.ShapeDtypeStruct((s, ATTN_WIDTH), BF16)],
        args=[qkvp, qkvp, qkvp, biasm, g3], comm=comm)


def _out_proj_fwd(ync, yna, wout, x, g_post3, g_next3, l, s, tm):
    half = D_MODEL // 2

    def body(a1_ref, a2_ref, w_ref, x_ref, gp_ref, gn_ref, z_ref, xm_ref, h_ref):
        for rs in _row_subtiles(tm, SUB_ROWS):
            z = _dot(a1_ref[rs, :], w_ref[0:half, :]) + _dot(a2_ref[rs, :], w_ref[half:D_MODEL, :])
            z_ref[rs, :] = z
            xm = x_ref[rs, :] + z * _rstd(z) * gp_ref[...]
            xm_ref[rs, :] = xm
            h_ref[rs, :] = (xm * _rstd(xm) * gn_ref[...]).astype(BF16)

    row = pl.BlockSpec((tm, D_MODEL), lambda i: (i, 0))
    gain = pl.BlockSpec((None, 1, D_MODEL), lambda i: (l, 0, 0))
    return _call(
        body, name="out_proj_fwd", grid=(s // tm,),
        in_specs=[pl.BlockSpec((tm, half), lambda i: (i, 0)), pl.BlockSpec((tm, half), lambda i: (i, 0)),
                  pl.BlockSpec((D_MODEL, D_MODEL), lambda i: (0, 0)), row, gain, gain],
        out_specs=[row, row, row],
        out_shape=[jax.ShapeDtypeStruct((s, D_MODEL), F32), jax.ShapeDtypeStruct((s, D_MODEL), F32),
                   jax.ShapeDtypeStruct((s, D_MODEL), BF16)],
        args=[ync, yna, wout, x, g_post3, g_next3])


def _ffn_in_fwd(h2, wfin4, s, tm, comm=None):
    def body(h_ref, wg_ref, wu_ref, gu_ref, act_ref):
        h = h_ref[...]
        for cs in _COL_SUBTILES:
            gate = _dot_nt(h, wg_ref[cs, :])
            up = _dot_nt(h, wu_ref[cs, :])
            gu_ref[0, :, cs] = gate.astype(BF16)
            gu_ref[1, :, cs] = up.astype(BF16)
            act_ref[:, cs] = (gate * (1.0 / (1.0 + jnp.exp(-gate))) * up).astype(BF16)

    return _call(
        body, name="ffn_in_fwd", grid=(2, s // tm),
        in_specs=[pl.BlockSpec((tm, D_MODEL), lambda b, i: (i, 0)),
                  pl.BlockSpec((None, FF_PAIR, D_MODEL), lambda b, i: (b, 0, 0)),
                  pl.BlockSpec((None, FF_PAIR, D_MODEL), lambda b, i: (2 + b, 0, 0))],
        out_specs=[pl.BlockSpec((2, tm, FF_PAIR), lambda b, i: (0, i, b)),
                   pl.BlockSpec((tm, FF_PAIR), lambda b, i: (i, b))],
        out_shape=[jax.ShapeDtypeStruct((2, s, D_FF), BF16), jax.ShapeDtypeStruct((s, D_FF), BF16)],
        args=[h2, wfin4, wfin4], comm=comm)


def _ffn_out_fwd(act, wfo, xm, g_post3, g_next3, l, l_next, s, tm, comm=None):
    def body(a_ref, w_ref, x_ref, gp_ref, gn_ref, f_ref, xo_ref, h_ref):
        for rs in _row_subtiles(tm, SUB_ROWS):
            f = _dot(a_ref[rs, :], w_ref[...])
            f_ref[rs, :] = f
            xo = x_ref[rs, :] + f * _rstd(f) * gp_ref[...]
            xo_ref[rs, :] = xo
            h_ref[rs, :] = (xo * _rstd(xo) * gn_ref[...]).astype(BF16)

    row = pl.BlockSpec((tm, D_MODEL), lambda i: (i, 0))
    return _call(
        body, name="ffn_out_fwd", grid=(s // tm,),
        in_specs=[pl.BlockSpec((tm, D_FF), lambda i: (i, 0)),
                  pl.BlockSpec((D_FF, D_MODEL), lambda i: (0, 0), pipeline_mode=pl.Buffered(1)), row,
                  pl.BlockSpec((None, 1, D_MODEL), lambda i: (l, 0, 0)),
                  pl.BlockSpec((None, 1, D_MODEL), lambda i: (l_next, 0, 0))],
        out_specs=[row, row, row],
        out_shape=[jax.ShapeDtypeStruct((s, D_MODEL), F32), jax.ShapeDtypeStruct((s, D_MODEL), F32),
                   jax.ShapeDtypeStruct((s, D_MODEL), BF16)],
        args=[act, wfo, xm, g_post3, g_next3], comm=comm)


def _loss_grad(xf, target, f, g3, l, s, tm):
    def body(x_ref, t_ref, f_ref, g_ref, dx_ref, sq_ref, df_ref, dg_ref):
        i = pl.program_id(0)
        err = x_ref[...] - t_ref[...]
        dx = err * (1.0 / D_MODEL)
        dx_ref[...] = dx
        df, dyn = _norm_bwd_rows(f_ref[...], g_ref[...], dx)
        df_ref[...] = df.astype(BF16)
        _accum_cols(dg_ref, dyn, i == 0)
        cs = jnp.sum(err * err, axis=0, keepdims=True)
        part = cs[:, 0:128]
        for k in range(1, D_MODEL // 128):
            part = part + cs[:, 128 * k:128 * (k + 1)]

        @pl.when(i == 0)
        def _():
            sq_ref[...] = jnp.zeros(sq_ref.shape, F32)

        sq_ref[0:1, :] += part

    row = pl.BlockSpec((tm, D_MODEL), lambda i: (i, 0))
    return _call(
        body, name="loss_grad", grid=(s // tm,),
        in_specs=[row, row, row, pl.BlockSpec((None, 1, D_MODEL), lambda i: (l, 0, 0))],
        out_specs=[row, pl.BlockSpec((8, 128), lambda i: (0, 0)), row, pl.BlockSpec((8, D_MODEL), lambda i: (0, 0))],
        out_shape=[jax.ShapeDtypeStruct((s, D_MODEL), F32), jax.ShapeDtypeStruct((8, 128), F32),
                   jax.ShapeDtypeStruct((s, D_MODEL), BF16), jax.ShapeDtypeStruct((8, D_MODEL), F32)],
        args=[xf, target, f, g3])


def _norm_bwd_rows(v, g, dy):
    r = _rstd(v)
    vn = v * r
    gd = dy * g
    dv = r * (gd - vn * jnp.mean(vn * gd, axis=-1, keepdims=True))
    return dv, dy * vn


def _zero_first(refs, first):
    @pl.when(first)
    def _():
        for ref in refs:
            ref[...] = jnp.zeros(ref.shape, F32)


def _add_cols(ref, val):
    ref[0:1, :] += jnp.sum(val, axis=0, keepdims=True)


def _accum_cols(ref, val, first):
    _zero_first((ref,), first)
    _add_cols(ref, val)


def _row_subtiles(rows, sub):
    sub = min(sub, rows)
    return [slice(r, r + sub) for r in range(0, rows, sub)]


def _ffn_out_bwd(df, wfo, gu, act, s, tm, comm=None):
    nm = s // tm

    def body(df_ref, w_ref, gu_ref, act_ref, dgu_ref, dw_ref, acc_ref):
        i = pl.program_id(1)
        df = df_ref[...]
        _zero_first((acc_ref,), i == 0)
        acc_ref[...] += _dot_tn(act_ref[...], df)

        @pl.when(i == nm - 1)
        def _():
            dw_ref[...] = acc_ref[...].astype(BF16)

        for cs in _COL_SUBTILES:
            da = _dot_nt(df, w_ref[cs, :])
            g = gu_ref[0, :, cs].astype(F32)
            u = gu_ref[1, :, cs].astype(F32)
            sg = 1.0 / (1.0 + jnp.exp(-g))
            dgu_ref[0, :, cs] = (da * u * (sg * (1.0 + g * (1.0 - sg)))).astype(BF16)
            dgu_ref[1, :, cs] = (da * (g * sg)).astype(BF16)

    blk = pl.BlockSpec((2, tm, FF_PAIR), lambda b, i: (0, i, b))
    wblk = pl.BlockSpec((FF_PAIR, D_MODEL), lambda b, i: (b, 0))
    return _call(
        body, name="ffn_out_bwd", grid=(2, nm),
        in_specs=[pl.BlockSpec((tm, D_MODEL), lambda b, i: (i, 0)), wblk, blk,
                  pl.BlockSpec((tm, FF_PAIR), lambda b, i: (i, b))],
        out_specs=[blk, wblk],
        out_shape=[jax.ShapeDtypeStruct((2, s, D_FF), BF16), jax.ShapeDtypeStruct((D_FF, D_MODEL), BF16)],
        scratch=[pltpu.VMEM((FF_PAIR, D_MODEL), F32)],
        args=[df, wfo, gu, act], comm=comm)


def _dw_ffn_in(h2, dgu, s):
    def body(a_ref, b_ref, o_ref):
        o_ref[...] = _dot_tn(b_ref[...], a_ref[...]).astype(BF16)

    out, = _call(
        body, name="dw_ffn_in", grid=(4,),
        in_specs=[pl.BlockSpec((s, D_MODEL), lambda n: (0, 0), pipeline_mode=pl.Buffered(1)),
                  pl.BlockSpec((None, s, FF_PAIR), lambda n: (n // 2, 0, n % 2))],
        out_specs=[pl.BlockSpec((None, FF_PAIR, D_MODEL), lambda n: (n, 0, 0))],
        out_shape=[jax.ShapeDtypeStruct((4, FF_PAIR, D_MODEL), BF16)], args=[h2, dgu])
    return out


def _ffn_in_bwd(dgu, wfin, xm, g_pre3, dres, z, g_post3, l, s, tm, comm=None):
    def body(d_ref, w_ref, xm_ref, gp_ref, dres_ref, z_ref, gq_ref, dxm_ref, dz_ref, dgp_ref, dgq_ref):
        _zero_first((dgp_ref, dgq_ref), pl.program_id(0) == 0)
        for rs in _row_subtiles(tm, SUB_ROWS):
            dh = _dot(d_ref[0, rs, :], w_ref[0:D_FF, :]) + _dot(d_ref[1, rs, :], w_ref[D_FF:2 * D_FF, :])
            dx, dyn = _norm_bwd_rows(xm_ref[rs, :], gp_ref[...], dh)
            dxm = dres_ref[rs, :] + dx
            dxm_ref[rs, :] = dxm
            _add_cols(dgp_ref, dyn)
            dz, dyn2 = _norm_bwd_rows(z_ref[rs, :], gq_ref[...], dxm)
            dz_ref[rs, :] = dz.astype(BF16)
            _add_cols(dgq_ref, dyn2)

    row = pl.BlockSpec((tm, D_MODEL), lambda i: (i, 0))
    gain = pl.BlockSpec((None, 1, D_MODEL), lambda i: (l, 0, 0))
    dgs = pl.BlockSpec((8, D_MODEL), lambda i: (0, 0))
    return _call(
        body, name="ffn_in_bwd", grid=(s // tm,),
        in_specs=[pl.BlockSpec((2, tm, D_FF), lambda i: (0, i, 0)),
                  pl.BlockSpec((2 * D_FF, D_MODEL), lambda i: (0, 0), pipeline_mode=pl.Buffered(1)),
                  row, gain, row, row, gain],
        out_specs=[row, row, dgs, dgs],
        out_shape=[jax.ShapeDtypeStruct((s, D_MODEL), F32), jax.ShapeDtypeStruct((s, D_MODEL), BF16),
                   jax.ShapeDtypeStruct((8, D_MODEL), F32), jax.ShapeDtypeStruct((8, D_MODEL), F32)],
        args=[dgu, wfin, xm, g_pre3, dres, z, g_post3], comm=comm)


def _out_proj_bwd(dz, wout, o, g3, ync, yna, l, s, tm):
    nm = s // tm
    half = D_MODEL // 2

    def body(dz_ref, w_ref, o_ref, g_ref, a1_ref, a2_ref, dyc_ref, do_ref, dg_ref, dw_ref, acc_ref):
        i = pl.program_id(0)
        gmat = _group_matrix()
        _zero_first((dg_ref, acc_ref), i == 0)
        dzv = dz_ref[...]
        acc_ref[0:half, :] += _dot_tn(a1_ref[...], dzv)
        acc_ref[half:D_MODEL, :] += _dot_tn(a2_ref[...], dzv)

        @pl.when(i == nm - 1)
        def _():
            dw_ref[...] = acc_ref[...].astype(BF16)

        for rs in _row_subtiles(tm, SUB_ROWS):
            dy = _dot_nt(dz_ref[rs, :], w_ref[...])
            dyc_ref[rs, :] = dy[:, 0:CONV_WIDTH]
            for j in range(ATTN_WIDTH // 128):
                c0 = 128 * j
                ov = o_ref[rs, c0:c0 + 128]
                dyn = dy[:, CONV_WIDTH + c0:CONV_WIDTH + c0 + 128]
                r = lax.rsqrt(_group_mean(ov * ov, gmat) + EPS)
                on = ov * r
                gd = dyn * g_ref[:, c0:c0 + 128]
                do_ref[rs, c0:c0 + 128] = r * (gd - on * _group_mean(on * gd, gmat))
                dg_ref[0:1, c0:c0 + 128] += jnp.sum(dyn * on, axis=0, keepdims=True)

    halfrow = pl.BlockSpec((tm, ATTN_WIDTH), lambda i: (i, 0))
    return _call(
        body, name="out_proj_bwd", grid=(nm,),
        in_specs=[pl.BlockSpec((tm, D_MODEL), lambda i: (i, 0)),
                  pl.BlockSpec((D_MODEL, D_MODEL), lambda i: (0, 0)), halfrow,
                  pl.BlockSpec((None, 1, ATTN_WIDTH), lambda i: (l, 0, 0)), halfrow, halfrow],
        out_specs=[halfrow, halfrow, pl.BlockSpec((8, ATTN_WIDTH), lambda i: (0, 0)),
                   pl.BlockSpec((D_MODEL, D_MODEL), lambda i: (0, 0))],
        out_shape=[jax.ShapeDtypeStruct((s, CONV_WIDTH), F32), jax.ShapeDtypeStruct((s, ATTN_WIDTH), F32),
                   jax.ShapeDtypeStruct((8, ATTN_WIDTH), F32), jax.ShapeDtypeStruct((D_MODEL, D_MODEL), BF16)],
        scratch=[pltpu.VMEM((D_MODEL, D_MODEL), F32)],
        args=[dz, wout, o, g3, ync, yna])


def _conv_bwd(pc, dyc, wc, g3, dq, dk, dv, l, s, tr):
    hb = tr // 8
    nt = s // tr
    ext = tr + 16
    last_hb = s // 8 - 1

    def body(pc_ref, prev_ref, next_ref, dy_ref, dyn_ref, wc_ref, g_ref, dq_ref, dk_ref, dv_ref,
             dpc_ref, dw_ref, dg_ref):
        i = pl.program_id(0)
        for part, ref in enumerate((dq_ref, dk_ref, dv_ref)):
            c = 3 * CONV_WIDTH + ATTN_WIDTH * part
            dpc_ref[:, c:c + ATTN_WIDTH] = ref[...]
        gmat = _group_matrix()
        row = lax.broadcasted_iota(jnp.int32, (ext, 128), 0) + (i * tr - 8)
        inside = jnp.where(row >= 0, jnp.where(row < s, 1, 0), 0) == 1

        @pl.when(i == 0)
        def _():
            dw_ref[...] = jnp.zeros(dw_ref.shape, F32)
            dg_ref[...] = jnp.zeros(dg_ref.shape, F32)

        def extend(ref_prev, ref_mid, ref_next, c):
            parts = [ref_prev[:, c:c + 128] if ref_prev is not None else jnp.zeros((8, 128), F32),
                     ref_mid[:, c:c + 128], ref_next[:, c:c + 128]]
            return jnp.concatenate(parts, axis=0)

        for j in range(CONV_WIDTH // 128):
            c0, c1, c2 = 128 * j, CONV_WIDTH + 128 * j, 2 * CONV_WIDTH + 128 * j
            hc = extend(prev_ref, pc_ref, next_ref, c0)
            bg = extend(prev_ref, pc_ref, next_ref, c1)
            cg = extend(prev_ref, pc_ref, next_ref, c2)
            dyn = extend(None, dy_ref, dyn_ref, c0)
            w0, w1, w2 = (wc_ref[0:1, c0:c0 + 128], wc_ref[1:2, c0:c0 + 128], wc_ref[2:3, c0:c0 + 128])
            gain = g_ref[:, c0:c0 + 128]
            u = jnp.where(inside, cg * hc, 0.0)
            u1 = pltpu.roll(u, 1, 0)
            u2 = pltpu.roll(u, 2, 0)
            out = u2 * w0 + u1 * w1 + u * w2
            yc = bg * out
            r = lax.rsqrt(_group_mean(yc * yc, gmat) + EPS)
            ycn = yc * r
            gd = dyn * gain
            dyc = r * (gd - ycn * _group_mean(ycn * gd, gmat))
            dout = jnp.where(inside, dyc * bg, 0.0)
            du = dout * w2 + pltpu.roll(dout, ext - 1, 0) * w1 + pltpu.roll(dout, ext - 2, 0) * w0
            sl = slice(8, 8 + tr)
            dpc_ref[:, c0:c0 + 128] = (du[sl] * cg[sl]).astype(BF16)
            dpc_ref[:, c1:c1 + 128] = (dyc[sl] * out[sl]).astype(BF16)
            dpc_ref[:, c2:c2 + 128] = (du[sl] * hc[sl]).astype(BF16)
            dw_ref[0:1, c0:c0 + 128] += jnp.sum(dout[sl] * u2[sl], axis=0, keepdims=True)
            dw_ref[1:2, c0:c0 + 128] += jnp.sum(dout[sl] * u1[sl], axis=0, keepdims=True)
            dw_ref[2:3, c0:c0 + 128] += jnp.sum(dout[sl] * u[sl], axis=0, keepdims=True)
            dg_ref[0:1, c0:c0 + 128] += jnp.sum(dyn[sl] * ycn[sl], axis=0, keepdims=True)

    wide = 3 * CONV_WIDTH
    return _call(
        body, name="conv_bwd", grid=(nt,),
        in_specs=[pl.BlockSpec((tr, wide), lambda i: (i, 0)),
                  pl.BlockSpec((8, wide), lambda i: (jnp.maximum(i * hb - 1, 0), 0)),
                  pl.BlockSpec((8, wide), lambda i: (jnp.minimum((i + 1) * hb, last_hb), 0)),
                  pl.BlockSpec((tr, CONV_WIDTH), lambda i: (i, 0)),
                  pl.BlockSpec((8, CONV_WIDTH), lambda i: (jnp.minimum((i + 1) * hb, last_hb), 0)),
                  pl.BlockSpec((None, 8, CONV_WIDTH), lambda i: (l, 0, 0)),
                  pl.BlockSpec((None, 1, CONV_WIDTH), lambda i: (l, 0, 0)),
                  pl.BlockSpec((tr, ATTN_WIDTH), lambda i: (i, 0)),
                  pl.BlockSpec((tr, ATTN_WIDTH), lambda i: (i, 0)),
                  pl.BlockSpec((tr, ATTN_WIDTH), lambda i: (i, 0))],
        out_specs=[pl.BlockSpec((tr, PROJ_WIDTH), lambda i: (i, 0)),
                   pl.BlockSpec((8, CONV_WIDTH), lambda i: (0, 0)),
                   pl.BlockSpec((8, CONV_WIDTH), lambda i: (0, 0))],
        out_shape=[jax.ShapeDtypeStruct((s, PROJ_WIDTH), BF16), jax.ShapeDtypeStruct((8, CONV_WIDTH), F32),
                   jax.ShapeDtypeStruct((8, CONV_WIDTH), F32)],
        args=[pc, pc, pc, dyc, dyc, wc, g3, dq, dk, dv])


def _attn_bwd(qkvp, biasm, o, lse, do, l, s, pad, comm=None):
    nb = s // Q_BLOCK
    qb0 = pad // Q_BLOCK
    scale = HEAD_DIM ** -0.5
    wide = 128 * ATTN_PAIRS

    def body(q_ref, k_ref, v_ref, b_ref, o_ref, lse_ref, do_ref,
             dq_ref, dk_ref, dv_ref, ds_ref, dk_acc, dv_acc):
        blk = pl.program_id(1)

        @pl.when(blk == 0)
        def _():
            dk_acc[...] = jnp.zeros(dk_acc.shape, F32)
            dv_acc[...] = jnp.zeros(dv_acc.shape, F32)
            ds_ref[...] = jnp.zeros(ds_ref.shape, F32)

        koff = pl.multiple_of(blk * Q_BLOCK + (pad - LEFT), Q_BLOCK)
        lane = lax.broadcasted_iota(jnp.int32, (1, 128), 1)
        kpos = lax.broadcasted_iota(jnp.int32, (1, K_BAND), 1) + (blk * Q_BLOCK - LEFT)
        kmask = jnp.where(kpos >= 0, 0.0, NEG_INF)
        for pr in range(ATTN_PAIRS):
            ls = slice(128 * pr, 128 * (pr + 1))
            q = q_ref[:, ls]
            kb = k_ref[pl.ds(koff, K_BAND), ls]
            vb = v_ref[pl.ds(koff, K_BAND), ls]
            dov = do_ref[:, ls]
            lse_v = lse_ref[:, ls]
            prod = dov * o_ref[:, ls]
            dq_parts = []
            dk_new = jnp.zeros((K_BAND, 128), F32)
            dv_new = jnp.zeros((K_BAND, 128), F32)
            for hh in range(2):
                in_head = (lane >> 6) == hh
                qm = jnp.where(in_head, q, jnp.zeros_like(q)) * jnp.asarray(scale, BF16)
                dom = jnp.where(in_head, dov, 0.0).astype(BF16)
                delta = jnp.sum(jnp.where(in_head, prod, 0.0), axis=1, keepdims=True)
                lse_h = lse_v[:, HEAD_DIM * hh:HEAD_DIM * hh + 1]
                sc = _dot_nt(qm, kb) + b_ref[2 * pr + hh] + kmask
                p = jnp.exp(sc - lse_h)
                dp = _dot_nt(dom, vb)
                ds = p * (dp - delta)
                ds_ref[2 * pr + hh] += ds
                dsb = ds.astype(BF16)
                dq_parts.append(_dot(dsb, kb) * scale)
                dk_new = dk_new + _dot_tn(dsb, qm)
                dv_new = dv_new + _dot_tn(p.astype(BF16), dom)
            dq_ref[:, ls] = jnp.where(lane < HEAD_DIM, dq_parts[0], dq_parts[1]).astype(BF16)
            dk_acc[pl.ds(koff, K_BAND), ls] += dk_new
            dv_acc[pl.ds(koff, K_BAND), ls] += dv_new

        @pl.when(blk == nb - 1)
        def _():
            dk_ref[...] = dk_acc[pad:pad + s, :].astype(BF16)
            dv_ref[...] = dv_acc[pad:pad + s, :].astype(BF16)

    n_grp = ATTN_WIDTH // wide
    qblk = pl.BlockSpec((Q_BLOCK, wide), lambda p, b: (b, p))
    col = pl.BlockSpec((s, wide), lambda p, b: (0, p))
    shp = jax.ShapeDtypeStruct((s, ATTN_WIDTH), BF16)
    return _call(
        body, name="attn_bwd", grid=(n_grp, nb),
        in_specs=[pl.BlockSpec((Q_BLOCK, wide), lambda p, b: (qb0 + b, p)),
                  pl.BlockSpec((s + pad, wide), lambda p, b: (0, n_grp + p)),
                  pl.BlockSpec((s + pad, wide), lambda p, b: (0, 2 * n_grp + p)),
                  pl.BlockSpec((None, 2 * ATTN_PAIRS, Q_BLOCK, K_BAND), lambda p, b: (l, p, 0, 0)),
                  qblk, qblk, qblk],
        out_specs=[qblk, col, col, pl.BlockSpec((2 * ATTN_PAIRS, Q_BLOCK, K_BAND), lambda p, b: (p, 0, 0))],
        out_shape=[shp, shp, shp, jax.ShapeDtypeStruct((N_HEADS, Q_BLOCK, K_BAND), F32)],
        scratch=[pltpu.VMEM((s + pad, wide), F32), pltpu.VMEM((s + pad, wide), F32)],
        args=[qkvp, qkvp, qkvp, biasm, o, lse, do], comm=comm)


def _dw_in(h, dproj, s):
    def body(a_ref, b_ref, o_ref):
        acc = _dot_tn(a_ref[...], b_ref[...])
        o_ref[0] = acc[:, 0:PROJ_SHARD].astype(BF16)
        o_ref[1] = acc[:, PROJ_SHARD:2 * PROJ_SHARD].astype(BF16)

    out, = _call(
        body, name="dw_in", grid=(4,),
        in_specs=[pl.BlockSpec((s, D_MODEL), lambda n: (0, 0)),
                  pl.BlockSpec((s, 2 * PROJ_SHARD), lambda n: (0, n))],
        out_specs=[pl.BlockSpec((2, D_MODEL, PROJ_SHARD), lambda n: (n, 0, 0))],
        out_shape=[jax.ShapeDtypeStruct((N_DEV, D_MODEL, PROJ_SHARD), BF16)], args=[h, dproj])
    return out


def _in_proj_bwd(dproj, win, x, g3, dres, l, s, tm, f_prev=None, g_post3=None, comm=None):
    chain = f_prev is not None

    def body(d_ref, w_ref, x_ref, g_ref, dres_ref, *rest):
        if chain:
            f_ref, gq_ref, dx_ref, dg_ref, df_ref, dgq_ref = rest
            _zero_first((dg_ref, dgq_ref), pl.program_id(0) == 0)
        else:
            dx_ref, dg_ref = rest
            _zero_first((dg_ref,), pl.program_id(0) == 0)
        w = jnp.concatenate([w_ref[j] for j in range(N_DEV)], axis=1)
        for rs in _row_subtiles(tm, SUB_ROWS):
            dh = _dot_nt(d_ref[rs, :], w)
            dx, dyn = _norm_bwd_rows(x_ref[rs, :], g_ref[...], dh)
            dx = dres_ref[rs, :] + dx
            dx_ref[rs, :] = dx
            _add_cols(dg_ref, dyn)
            if chain:
                df, dyn2 = _norm_bwd_rows(f_ref[rs, :], gq_ref[...], dx)
                df_ref[rs, :] = df.astype(BF16)
                _add_cols(dgq_ref, dyn2)

    row = pl.BlockSpec((tm, D_MODEL), lambda i: (i, 0))
    dgs = pl.BlockSpec((8, D_MODEL), lambda i: (0, 0))
    in_specs = [pl.BlockSpec((tm, PROJ_WIDTH), lambda i: (i, 0)),
                pl.BlockSpec((N_DEV, D_MODEL, PROJ_SHARD), lambda i: (0, 0, 0), pipeline_mode=pl.Buffered(1)),
                row, pl.BlockSpec((None, 1, D_MODEL), lambda i: (l, 0, 0)), row]
    out_specs = [row, dgs]
    out_shape = [jax.ShapeDtypeStruct((s, D_MODEL), F32), jax.ShapeDtypeStruct((8, D_MODEL), F32)]
    args = [dproj, win, x, g3, dres]
    if chain:
        in_specs += [row, pl.BlockSpec((None, 1, D_MODEL), lambda i: (l - 1, 0, 0))]
        out_specs += [row, dgs]
        out_shape += [jax.ShapeDtypeStruct((s, D_MODEL), BF16), jax.ShapeDtypeStruct((8, D_MODEL), F32)]
        args += [f_prev, g_post3]
    return _call(body, name="in_proj_bwd", grid=(s // tm,), in_specs=in_specs, out_specs=out_specs,
                 out_shape=out_shape, args=args, comm=comm)


def _adamw(name, w, m, v, lands, owns=None, me=None):
    groups, rows, cols = w.shape
    assert len(lands) == groups
    n_part = lands[0].shape[0]
    tr = _row_tile(rows, tuple(c for c in (512, 352, 256, 176, 128, 64, 32, 16, 8) if c * cols <= 256 * 1024))
    c1 = 1.0 - ADAM_B1 ** ADAM_STEP
    c2 = 1.0 - ADAM_B2 ** ADAM_STEP
    n_own = groups if owns is not None else 0

    def body(*refs):
        if n_own:
            me_ref, refs = refs[0], refs[1:]
        w_ref, m_ref, v_ref = refs[:3]
        land_refs = refs[3:3 + groups]
        own_refs = refs[3 + groups:3 + groups + n_own]
        g_ref, d_ref, nm_ref, nv_ref = refs[3 + groups + n_own:]
        grp = pl.program_id(0)
        for gi in range(groups):
            @pl.when(grp == gi)
            def _():
                l_ref = land_refs[gi]
                g = None
                for p in range(n_part):
                    part = l_ref[p].astype(F32)
                    if n_own:
                        part = jnp.where(me_ref[0] == p, own_refs[gi][...].astype(F32), part)
                    g = part if g is None else g + part
                g_ref[...] = g
                m1 = ADAM_B1 * m_ref[...] + (1.0 - ADAM_B1) * g
                v1 = ADAM_B2 * v_ref[...] + (1.0 - ADAM_B2) * (g * g)
                nm_ref[...] = m1
                nv_ref[...] = v1
                d_ref[...] = -ADAM_LR * ((m1 / c1) / (jnp.sqrt(v1 / c2) + ADAM_EPS) + ADAM_WD * w_ref[...])

    blk = pl.BlockSpec((None, tr, cols), lambda g, i, *_: (g, i, 0))
    shp = jax.ShapeDtypeStruct((groups, rows, cols), F32)

    def land_spec(gi):
        return pl.BlockSpec((n_part, tr, cols), lambda g, i, *_: (0, jnp.where(g == gi, i, 0), 0))

    def own_spec(gi):
        if owns[gi].ndim == 3:
            return pl.BlockSpec((None, tr, cols), lambda g, i, me_ref: (me_ref[0], jnp.where(g == gi, i, 0), 0))
        return pl.BlockSpec((tr, cols), lambda g, i, me_ref: (jnp.where(g == gi, i, 0), 0))

    in_specs = [blk, blk, blk] + [land_spec(gi) for gi in range(groups)] + [own_spec(gi) for gi in range(n_own)]
    args = [w, m, v] + list(lands) + (list(owns) if n_own else [])
    if not n_own:
        return _call(body, name=name, grid=(groups, rows // tr), in_specs=in_specs,
                     out_specs=[blk, blk, blk, blk], out_shape=[shp, shp, shp, shp], args=args)
    return pl.pallas_call(
        body, name=name,
        grid_spec=pltpu.PrefetchScalarGridSpec(
            num_scalar_prefetch=1, grid=(groups, rows // tr), in_specs=in_specs, out_specs=[blk, blk, blk, blk]),
        out_shape=[shp, shp, shp, shp],
        compiler_params=pltpu.CompilerParams(dimension_semantics=("arbitrary", "arbitrary"),
                                             vmem_limit_bytes=VMEM_LIMIT),
    )(me, *args)


def _pack_small(rel, gco, gao, gpm, gqm, gpf, gqf):
    n_layers = rel.shape[0]
    relp = jnp.pad(rel, ((0, 0), (0, 0), (0, REL_PAD - rel.shape[2])))
    parts = [relp.reshape(n_layers * N_HEADS * REL_PAD // 128, 128)]
    parts += [a.reshape(-1, 128) for a in (gco, gao, gpm, gqm, gpf, gqf)]
    return jnp.concatenate(parts, axis=0)


def _pack_small_grads(d_rel, parts):
    n_layers = len(d_rel)
    keys = ("gco", "gao", "gpm", "gqm", "gpf", "gqf")
    arrays = list(d_rel) + [parts[k][l] for k in keys for l in range(n_layers)] + list(parts["wc"])
    rows = 0
    plan = []
    for l in range(n_layers):
        for h in range(N_HEADS):
            for t in range(REL_PAD // 128):
                plan.append((l, (0, h), t, rows))
                rows += 1
    for ki, k in enumerate(keys):
        for l in range(n_layers):
            for t in range(parts[k][l].shape[1] // 128):
                plan.append((n_layers * (1 + ki) + l, (0,), t, rows))
                rows += 1
    for l in range(n_layers):
        for tap in range(3):
            for t in range(CONV_WIDTH // 128):
                plan.append((n_layers * (1 + len(keys)) + l, (tap,), t, rows))
                rows += 1
    total = rows + (-rows) % 8

    def body(*refs):
        o_ref = refs[-1]
        if total > rows:
            o_ref[rows:total, :] = jnp.zeros((total - rows, 128), F32)
        for op, idx, t, dst in plan:
            lanes = slice(128 * t, 128 * (t + 1))
            if len(idx) == 2:
                o_ref[dst:dst + 1, :] = refs[op][idx[0], idx[1]:idx[1] + 1, lanes]
            else:
                o_ref[dst:dst + 1, :] = refs[op][idx[0]:idx[0] + 1, lanes]

    vmem = pl.BlockSpec(memory_space=pltpu.VMEM)
    out, = _call(body, name="pack_small_grads", grid=(), in_specs=[vmem] * len(arrays), out_specs=[vmem],
                 out_shape=[jax.ShapeDtypeStruct((total, 128), F32)], args=arrays)
    return out


def _unpack_small(p, n_layers):
    n_rel = n_layers * N_HEADS * REL_PAD // 128
    rel = p[:n_rel].reshape(n_layers, N_HEADS, REL_PAD)[:, :, :2 * REL_CLIP + 1]
    outs = [rel]
    r0 = n_rel
    for width in (CONV_WIDTH, ATTN_WIDTH, D_MODEL, D_MODEL, D_MODEL, D_MODEL):
        nr = n_layers * width // 128
        outs.append(p[r0:r0 + nr].reshape(n_layers, width))
        r0 += nr
    return outs


def kernel(x, w_in, w_conv, rel_bias, g_conv_out, g_attn_out, w_out, g_pre_mix, g_post_mix, g_pre_ffn, g_post_ffn, w_ffn_in, w_ffn_out, loss_target, m_w_in, m_w_conv, m_rel_bias, m_g_conv_out, m_g_attn_out, m_w_out, m_g_pre_mix, m_g_post_mix, m_g_pre_ffn, m_g_post_ffn, m_w_ffn_in, m_w_ffn_out, v_w_in, v_w_conv, v_rel_bias, v_g_conv_out, v_g_attn_out, v_w_out, v_g_pre_mix, v_g_post_mix, v_g_pre_ffn, v_g_post_ffn, v_w_ffn_in, v_w_ffn_out):
    n_layers = w_in.shape[0]
    s = x.shape[1]
    assert x.shape == (1, s, D_MODEL) and s % 1024 == 0
    assert w_in.shape == (n_layers, D_MODEL, PROJ_SHARD) and w_ffn_in.shape == (n_layers, D_MODEL, FF_SHARD)
    tm = 512
    tq = 1024 if s >= 2048 else 512
    tf = min(1024, s)
    x0 = x.reshape(s, D_MODEL)
    target = loss_target.reshape(s, D_MODEL)
    dev = _dev_index(lax.axis_index("x"), lax.axis_index("y"), lax.axis_index("c"))

    wt_ffn_in, mt_ffn_in, vt_ffn_in = (jnp.transpose(a, (0, 2, 1)) for a in (w_ffn_in, m_w_ffn_in, v_w_ffn_in))
    local_w = [_cast_bf16(w_in, "cast_w_in"), _cast_bf16(w_out, "cast_w_out"),
               _cast_bf16(wt_ffn_in, "cast_w_ffn_in"), _cast_bf16(w_ffn_out, "cast_w_ffn_out")]
    wc_local = jnp.pad(jnp.transpose(w_conv, (0, 2, 1)).reshape(-1), (0, 1024 - n_layers * 3 * 64)).reshape(8, 128)
    biasm, win_next, wc_g = _bias_build(jnp.pad(rel_bias, ((0, 0), (0, 0), (0, REL_PAD - rel_bias.shape[2]))),
                                        comm=_Gather([(local_w[0], 0), (wc_local, None)]))
    weights = [None] * n_layers
    wc_full = wc_g.reshape(N_DEV, 1024)[:, :n_layers * 3 * 64].reshape(N_DEV, n_layers, 3, 64)
    wc_full = jnp.transpose(wc_full, (1, 2, 0, 3)).reshape(n_layers, 3, CONV_WIDTH)
    wc_full = jnp.pad(wc_full, ((0, 0), (0, 5), (0, 0)))

    g3 = {k: v.reshape(n_layers, 1, -1) for k, v in dict(
        conv=g_conv_out, attn=g_attn_out, pre_mix=g_pre_mix, post_mix=g_post_mix,
        pre_ffn=g_pre_ffn, post_ffn=g_post_ffn).items()}

    saved = []
    xl = x0
    h = _norm_cast(x0, g3["pre_mix"], 0, tm)
    for l in range(n_layers):
        win = win_next
        pc, wout = _in_proj(h, win, s, tq, 0, comm=_Gather([(local_w[1], l)]))
        qkvp, = _in_proj(h, win, s, tq, 1)
        ync = _conv_fwd(pc, wc_full, g3["conv"], l, s, tm)
        o, lse, yna, wfin = _attn_fwd(qkvp, biasm, g3["attn"], l, s, tq, comm=_Gather([(local_w[2], l)]))
        wout = wout.reshape(D_MODEL, D_MODEL)
        z, xm, h2 = _out_proj_fwd(ync, yna, wout, xl, g3["post_mix"], g3["pre_ffn"], l, s, tq)
        gu, act, wfout = _ffn_in_fwd(h2, wfin.reshape(4, FF_PAIR, D_MODEL), s, tf, comm=_Gather([(local_w[3], l)]))
        wfo = wfout.reshape(D_FF, D_MODEL)
        l_next = min(l + 1, n_layers - 1)
        f, xo, h_next, *got = _ffn_out_fwd(act, wfo, xm, g3["post_ffn"], g3["pre_mix"], l, l_next, s, tm,
                                           comm=_Gather([(local_w[0], l + 1)]) if l + 1 < n_layers else None)
        weights[l] = [win, wout, wfin.reshape(2 * D_FF, D_MODEL), wfo]
        win_next = got[0] if got else None
        saved.append(dict(x=xl, h=h, pc=pc, qkvp=qkvp, ync=ync, yna=yna, o=o, lse=lse, z=z, xm=xm,
                          h2=h2, gu=gu, act=act, f=f))
        xl, h = xo, h_next

    dx, sq, df, dg_post_ffn = _loss_grad(xl, target, saved[-1]["f"], g3["post_ffn"], n_layers - 1, s, tm)
    loss = lax.psum(jnp.sum(sq) * (0.5 / D_MODEL), ("x", "y", "c"))

    lands = dict(win=[None] * n_layers, wout=[None] * n_layers, wfin=[None] * n_layers, wfout=[None] * n_layers)
    small = {k: [None] * n_layers for k in ("gco", "gao", "gpm", "gqm", "gpf", "gqf", "wc")}
    d_rel = [None] * n_layers
    started = []

    def start(name, keys, l, arrays):
        items = [(a, False) for a in arrays]
        send_sems, recv_sems, srcs, zones, token = _exchange_start(name + "_start", items)
        started.append((name, keys, l, items, send_sems, recv_sems, srcs, zones))
        return token[0:1, 0:1].reshape(1, 1, 1)

    for l in reversed(range(n_layers)):
        sv = saved[l]
        win, wout, wfin, wfo = weights[l]
        small["gqf"][l] = dg_post_ffn
        dgu, d_wfout = _ffn_out_bwd(df, wfo, sv["gu"], sv["act"], s, tm)
        d_wfout = d_wfout.reshape(N_DEV, FFO_SHARD, D_MODEL)
        d_wfin = _dw_ffn_in(sv["h2"], dgu, s).reshape(N_DEV, FF_SHARD, D_MODEL)
        token = start(f"exchange_ffn{l}", ("wfout", "wfin"), l, [d_wfout, d_wfin])
        dxm, dz, dg_pre_ffn, dg_post_mix = _ffn_in_bwd(
            dgu, wfin, sv["xm"], g3["pre_ffn"] + token, dx, sv["z"], g3["post_mix"], l, s, tm)
        small["gpf"][l] = dg_pre_ffn
        small["gqm"][l] = dg_post_mix
        dyc, do, dg_attn, d_wout = _out_proj_bwd(dz, wout, sv["o"], g3["attn"], sv["ync"], sv["yna"], l, s, tq)
        d_wout = d_wout.reshape(N_DEV, D_MODEL // N_DEV, D_MODEL)
        small["gao"][l] = dg_attn
        dq, dk, dv, ds_sum = _attn_bwd(sv["qkvp"], biasm, sv["o"], sv["lse"], do, l, s, tq)
        d_rel[l] = _bias_bwd(ds_sum[None])
        dproj, dwc, dg_conv = _conv_bwd(sv["pc"], dyc, wc_full, g3["conv"], dq, dk, dv, l, s, tm)
        small["wc"][l] = dwc
        small["gco"][l] = dg_conv
        d_win = _dw_in(sv["h"], dproj, s)
        token = start(f"exchange_mix{l}", ("wout", "win"), l, [d_wout, d_win])
        if l > 0:
            dx, dg_pre_mix, df, dg_post_ffn = _in_proj_bwd(
                dproj, win, sv["x"], g3["pre_mix"] + token, dxm, l, s, tm, f_prev=saved[l - 1]["f"],
                g_post3=g3["post_ffn"])
        else:
            dx, dg_pre_mix = _in_proj_bwd(dproj, win, sv["x"], g3["pre_mix"] + token, dxm, l, s, tm)
        small["gpm"][l] = dg_pre_mix
    grad_x = dx.reshape(1, s, D_MODEL)

    small_vec = _pack_small_grads(d_rel, small)
    small_items = [(small_vec, True)]
    small_sems = _exchange_start("exchange_small_start", small_items)

    owns = dict(win=[None] * n_layers, wout=[None] * n_layers, wfin=[None] * n_layers, wfout=[None] * n_layers)

    def wait(which, after):
        for name, keys, l, items, send_sems, recv_sems, srcs, zones in started:
            if keys == which:
                srcs, zones = _exchange_wait(name + "_wait", items, send_sems, recv_sems, srcs, zones, after)
                for key, src, zone in zip(keys, srcs, zones):
                    owns[key][l], lands[key][l] = src, zone

    me = dev.astype(jnp.int32).reshape(1)
    wait(("wfout", "wfin"), small_sems[4])
    r_fin = [jnp.transpose(t, (0, 2, 1)) for t in _adamw(
        "adamw_w_ffn_in", wt_ffn_in, mt_ffn_in, vt_ffn_in, lands["wfin"], owns["wfin"], me)]
    r_fout = _adamw("adamw_w_ffn_out", w_ffn_out, m_w_ffn_out, v_w_ffn_out, lands["wfout"], owns["wfout"], me)
    wait(("wout", "win"), r_fout[0])
    r_out = _adamw("adamw_w_out", w_out, m_w_out, v_w_out, lands["wout"], owns["wout"], me)
    r_in = _adamw("adamw_w_in", w_in, m_w_in, v_w_in, lands["win"], owns["win"], me)
    (small_own,), (land_small,) = _exchange_wait(
        "exchange_small_wait", small_items, small_sems[0], small_sems[1], small_sems[2], small_sems[3], r_in[0])

    n_rep = 64 * n_layers
    rep = _adamw(
        "adamw_replicated",
        _pack_small(rel_bias, g_conv_out, g_attn_out, g_pre_mix, g_post_mix, g_pre_ffn, g_post_ffn)[None],
        _pack_small(m_rel_bias, m_g_conv_out, m_g_attn_out, m_g_pre_mix, m_g_post_mix, m_g_pre_ffn, m_g_post_ffn)[None],
        _pack_small(v_rel_bias, v_g_conv_out, v_g_attn_out, v_g_pre_mix, v_g_post_mix, v_g_pre_ffn, v_g_post_ffn)[None],
        [land_small[:, :n_rep]], [small_own[:n_rep]], me)
    rep = [_unpack_small(t[0], n_layers) for t in rep]

    wc_rows = n_layers * 3 * CONV_WIDTH // 128
    zeros_wc = jnp.zeros((1, wc_rows, 128), F32)
    g_wc_full = _adamw("sum_w_conv", zeros_wc, zeros_wc, zeros_wc, [land_small[:, n_rep:n_rep + wc_rows]],
                       [small_own[n_rep:n_rep + wc_rows]], me)[0]
    g_wc_full = g_wc_full.reshape(n_layers, 3, CONV_WIDTH)
    g_wc = lax.dynamic_slice_in_dim(g_wc_full, dev * (CONV_WIDTH // N_DEV), CONV_WIDTH // N_DEV, axis=2)
    g_wc = jnp.transpose(g_wc, (0, 2, 1))

    def tiny(a):
        flat = a.reshape(-1)
        return jnp.pad(flat, (0, (-flat.shape[0]) % 1024)).reshape(1, -1, 128)

    r_wc = _adamw("adamw_w_conv", tiny(w_conv), tiny(m_w_conv), tiny(v_w_conv), [tiny(g_wc)])
    r_wc = [t.reshape(-1)[:w_conv.size].reshape(w_conv.shape) for t in r_wc]

    def leaf(kind):
        return [r_in[kind], r_wc[kind], rep[kind][0], rep[kind][1], rep[kind][2], r_out[kind],
                rep[kind][3], rep[kind][4], rep[kind][5], rep[kind][6], r_fin[kind], r_fout[kind]]

    return (loss, grad_x, *leaf(0), *leaf(1), *leaf(2), *leaf(3))
```

```python
import math

import jax
import jax.numpy as jnp
from jax import lax
from jax.experimental import pallas as pl
from jax.experimental.pallas import tpu as pltpu

F32 = jnp.float32
BF16 = jnp.bfloat16

D_MODEL = 1024
N_DEV = 8
CHUNK = 64
N_LEFT_CHUNKS = 8
CONV_WIDTH = 512
ATTN_WIDTH = 512
HEAD_DIM = 64
N_HEADS = 8
REL_CLIP = 128
REL_PAD = 384
PROJ_WIDTH = 3072
PROJ_SHARD = PROJ_WIDTH // N_DEV
D_FF = 2816
FF_SHARD = 2 * D_FF // N_DEV
FFO_SHARD = D_FF // N_DEV
FF_PAIR = 2 * FF_SHARD
_COL_SUBTILES = (slice(0, 768), slice(768, FF_PAIR))
EPS = 1e-6
NEG_INF = -1e30
ATTN_PAIRS = 2
Q_BLOCK = 4 * CHUNK
K_BAND = Q_BLOCK + N_LEFT_CHUNKS * CHUNK
LEFT = N_LEFT_CHUNKS * CHUNK
TOEP = 1024

ADAM_LR = 0.001
ADAM_B1 = 0.9
ADAM_B2 = 0.999
ADAM_EPS = 1e-08
ADAM_WD = 0.01
ADAM_STEP = 10

VMEM_LIMIT = 52 * 1024 * 1024
SUB_ROWS = 256
MESH = pl.DeviceIdType.MESH
ANY = pl.BlockSpec(memory_space=pl.ANY)

NT = (((1,), (1,)), ((), ()))
TN = (((0,), (0,)), ((), ()))


def _dot(a, b):
    return jnp.dot(a, b, preferred_element_type=F32)


def _dot_nt(a, b):
    return lax.dot_general(a, b, NT, preferred_element_type=F32)


def _dot_tn(a, b):
    return lax.dot_general(a, b, TN, preferred_element_type=F32)


def _rstd(v):
    return lax.rsqrt(jnp.mean(v * v, axis=-1, keepdims=True) + EPS)


def _group_matrix():
    r = lax.broadcasted_iota(jnp.int32, (128, 128), 0) >> 6
    c = lax.broadcasted_iota(jnp.int32, (128, 128), 1) >> 6
    return jnp.where(r == c, 1.0, 0.0).astype(BF16)


def _group_mean(v, gmat):
    hi = v.astype(BF16)
    lo = (v - hi.astype(F32)).astype(BF16)
    return (_dot(hi, gmat) + _dot(lo, gmat)) * (1.0 / HEAD_DIM)


def _split3(v):
    hi = v.astype(BF16)
    r1 = v - hi.astype(F32)
    mid = r1.astype(BF16)
    lo = (r1 - mid.astype(F32)).astype(BF16)
    return hi, mid, lo


def _row_tile(rows, cands=(1024, 512, 704, 256, 128, 64, 32, 16)):
    for c in cands:
        if rows % c == 0:
            return c
    return rows


def _dev_index(px, py, pc):
    return 4 * px + 2 * py + pc


def _when(cond):
    if cond is True:
        return lambda fn: fn()
    return pl.when(cond)


def _phases(grid):
    def phases():
        if not grid:
            return True, True, True
        lin = pl.program_id(0)
        for a in range(1, len(grid)):
            lin = lin * grid[a] + pl.program_id(a)
        total = math.prod(grid)
        return lin == 0, lin == total - 1, lin == total - 1
    return phases


class _Gather:
    def __init__(self, items):
        self.items = items
        self.args = [a for a, _ in items]
        n = len(items)
        self.out_shape = [jax.ShapeDtypeStruct((N_DEV,) + (a.shape if lay is None else a.shape[1:]), a.dtype)
                          for a, lay in items]
        self.scratch = [pltpu.SemaphoreType.DMA((n, 7)), pltpu.SemaphoreType.DMA((n, 7)),
                        pltpu.SemaphoreType.DMA((n,))]

    def _ctx(self, ins, outs, sems):
        send_sems, recv_sems, local_sems = sems
        x, y, c = lax.axis_index("x"), lax.axis_index("y"), lax.axis_index("c")
        chips = [(1 - x, y), (x, 1 - y), (1 - x, 1 - y)]

        def src(k):
            lay = self.items[k][1]
            return ins[k] if lay is None else ins[k].at[lay]

        def copy(k, s, idx, to, from_src=False):
            return pltpu.make_async_remote_copy(
                src_ref=src(k) if from_src else outs[k].at[idx], dst_ref=outs[k].at[idx],
                send_sem=send_sems.at[k, s], recv_sem=recv_sems.at[k, s],
                device_id=to, device_id_type=MESH)

        def local(k):
            return pltpu.make_async_copy(src(k), outs[k].at[_dev_index(x, y, c)], local_sems.at[k])

        return x, y, c, chips, copy, local

    def start(self, ins, outs, sems, cond):
        n = len(self.items)

        @_when(cond)
        def _():
            x, y, c, chips, copy, local = self._ctx(ins, outs, sems)
            me = _dev_index(x, y, c)
            for k in range(n):
                local(k).start()
                copy(k, 0, me, (x, y, 1 - c), from_src=True).start()
                for j, chip in enumerate(chips):
                    copy(k, 1 + j, me, (chip[0], chip[1], c), from_src=True).start()

    def forward(self, ins, outs, sems, cond):
        n = len(self.items)

        @_when(cond)
        def _():
            x, y, c, chips, copy, local = self._ctx(ins, outs, sems)
            for j, chip in enumerate(chips):
                idx = _dev_index(chip[0], chip[1], c)
                for k in range(n):
                    copy(k, 1 + j, idx, (x, y, c)).wait_recv()
                    copy(k, 4 + j, idx, (x, y, 1 - c)).start()

    def finish(self, ins, outs, sems, cond):
        n = len(self.items)

        @_when(cond)
        def _():
            x, y, c, chips, copy, local = self._ctx(ins, outs, sems)
            me = _dev_index(x, y, c)
            for k in range(n):
                copy(k, 0, _dev_index(x, y, 1 - c), (x, y, c)).wait_recv()
            for j, chip in enumerate(chips):
                idx = _dev_index(chip[0], chip[1], 1 - c)
                for k in range(n):
                    copy(k, 4 + j, idx, (x, y, c)).wait_recv()
            for k in range(n):
                for s in range(4):
                    copy(k, s, me, (x, y, c), from_src=True).wait_send()
                for j, chip in enumerate(chips):
                    copy(k, 4 + j, _dev_index(chip[0], chip[1], c), (x, y, c)).wait_send()
                local(k).wait()


_PEER_FLIPS = [(0, 0, 1), (1, 0, 0), (0, 1, 0), (1, 1, 0), (1, 0, 1), (0, 1, 1), (1, 1, 1)]


def _call(body, *, name, grid, in_specs, out_specs, out_shape, args, scratch=(), comm=None):
    n_hi, n_ho, n_hs = len(args), len(out_shape), len(scratch)
    c_args = list(comm.args) if comm else []
    c_out = list(comm.out_shape) if comm else []
    c_scr = list(comm.scratch) if comm else []
    phases = _phases(grid)

    def kern(*refs):
        cuts = [n_hi, len(c_args), n_ho, len(c_out), n_hs, len(c_scr)]
        parts, pos = [], 0
        for n in cuts:
            parts.append(refs[pos:pos + n])
            pos += n
        hi, ci, ho, co, hs, cs = parts
        if comm:
            first, mid, last = phases()
            comm.start(ci, co, cs, first)
            comm.forward(ci, co, cs, mid)
        body(*hi, *ho, *hs)
        if comm:
            comm.finish(ci, co, cs, last)

    sem = ("arbitrary",) * len(grid) if grid else None
    return pl.pallas_call(
        kern, name=name, grid=grid,
        in_specs=list(in_specs) + [ANY] * len(c_args),
        out_specs=list(out_specs) + [ANY] * len(c_out),
        out_shape=list(out_shape) + c_out,
        scratch_shapes=list(scratch) + c_scr,
        compiler_params=pltpu.CompilerParams(dimension_semantics=sem, vmem_limit_bytes=VMEM_LIMIT),
    )(*args, *c_args)


def _comm_only(name, comm):
    return _call(lambda: None, name=name, grid=(), in_specs=[], out_specs=[], out_shape=[], args=[], comm=comm)


HBM_SPEC = pl.BlockSpec(memory_space=pltpu.HBM)
SEM_SPEC = pl.BlockSpec(memory_space=pltpu.SEMAPHORE)
SIDE_EFFECT = pltpu.SideEffectType.DATAFLOW_SIDE_EFFECTING


def _exchange_peer(x, y, c, s):
    fx, fy, fc = _PEER_FLIPS[s]
    return x ^ fx, y ^ fy, c ^ fc


def _exchange_start(name, items):
    n = len(items)
    srcs = [pltpu.with_memory_space_constraint(a, pltpu.HBM) for a, _ in items]
    land_shapes = [(N_DEV,) + (a.shape if whole else a.shape[1:]) for a, whole in items]
    lands = [pltpu.with_memory_space_constraint(lax.empty(shp, a.dtype), pltpu.HBM)
             for shp, (a, _) in zip(land_shapes, items)]

    n_sem = 7 * n

    def body(*refs):
        src_refs, land_refs = refs[:n], refs[n:2 * n]
        send_sems = refs[2 * n:2 * n + n_sem]
        recv_sems = refs[2 * n + n_sem:2 * n + 2 * n_sem]
        token = refs[-1]
        x, y, c = lax.axis_index("x"), lax.axis_index("y"), lax.axis_index("c")
        me = _dev_index(x, y, c)
        for s in range(7):
            px, py, pc = _exchange_peer(x, y, c, s)
            for k in range(n):
                src = src_refs[k] if items[k][1] else src_refs[k].at[_dev_index(px, py, pc)]
                pltpu.make_async_remote_copy(
                    src_ref=src, dst_ref=land_refs[k].at[me],
                    send_sem=send_sems[7 * k + s], recv_sem=recv_sems[7 * k + s],
                    device_id=(px, py, pc), device_id_type=MESH).start()
        token[...] = jnp.zeros(token.shape, token.dtype)

    outs = pl.pallas_call(
        body, name=name,
        out_shape=(*[pltpu.SemaphoreType.DMA(())] * (2 * n_sem),
                   *[pltpu.HBM(a.shape, a.dtype) for a in srcs],
                   *[pltpu.HBM(shp, a.dtype) for shp, a in zip(land_shapes, srcs)],
                   jax.ShapeDtypeStruct((8, 128), F32)),
        in_specs=[HBM_SPEC] * (2 * n),
        out_specs=(*[SEM_SPEC] * (2 * n_sem), *[HBM_SPEC] * (2 * n), pl.BlockSpec(memory_space=pltpu.VMEM)),
        input_output_aliases={i: 2 * n_sem + i for i in range(2 * n)},
        compiler_params=pltpu.CompilerParams(has_side_effects=SIDE_EFFECT),
    )(*srcs, *lands)
    base = 2 * n_sem
    return (list(outs[:n_sem]), list(outs[n_sem:base]), list(outs[base:base + n]),
            list(outs[base + n:base + 2 * n]), outs[-1])


def _exchange_wait(name, items, send_sems, recv_sems, srcs, lands, after):
    n = len(items)

    n_sem = 7 * n

    def body(*refs):
        src_refs, land_refs = refs[:n], refs[n:2 * n]
        send_refs = refs[2 * n:2 * n + n_sem]
        recv_refs = refs[2 * n + n_sem:2 * n + 2 * n_sem]
        x, y, c = lax.axis_index("x"), lax.axis_index("y"), lax.axis_index("c")
        for s in range(7):
            for k in range(n):
                copy = pltpu.make_async_remote_copy(
                    src_ref=src_refs[k] if items[k][1] else src_refs[k].at[0], dst_ref=land_refs[k].at[0],
                    send_sem=send_refs[7 * k + s], recv_sem=recv_refs[7 * k + s],
                    device_id=(x, y, c), device_id_type=MESH)
                copy.wait_send()
                copy.wait_recv()

    outs = pl.pallas_call(
        body, name=name,
        out_shape=(*[pltpu.HBM(a.shape, a.dtype) for a in srcs], *[pltpu.HBM(a.shape, a.dtype) for a in lands]),
        in_specs=[HBM_SPEC] * (2 * n) + [SEM_SPEC] * (2 * n_sem) + [ANY],
        out_specs=tuple([HBM_SPEC] * (2 * n)),
        input_output_aliases={i: i for i in range(2 * n)},
        compiler_params=pltpu.CompilerParams(has_side_effects=SIDE_EFFECT),
    )(*srcs, *lands, *send_sems, *recv_sems, after)
    return list(outs[:n]), list(outs[n:])


def _cast_bf16(x, name):
    shape = x.shape
    x2 = x.reshape(-1, shape[-1])
    rows, cols = x2.shape
    tr = _row_tile(rows)

    def body(x_ref, o_ref):
        o_ref[...] = x_ref[...].astype(BF16)

    blk = pl.BlockSpec((tr, cols), lambda i: (i, 0))
    out, = _call(body, name=name, grid=(rows // tr,), in_specs=[blk], out_specs=[blk],
                 out_shape=[jax.ShapeDtypeStruct((rows, cols), BF16)], args=[x2])
    return out.reshape(shape)


def _norm_cast(x, g3, l, tm):
    s = x.shape[0]

    def body(x_ref, g_ref, o_ref):
        v = x_ref[...]
        o_ref[...] = (v * _rstd(v) * g_ref[...]).astype(BF16)

    row = pl.BlockSpec((tm, D_MODEL), lambda i: (i, 0))
    out, = _call(body, name="norm_cast", grid=(s // tm,),
                 in_specs=[row, pl.BlockSpec((None, 1, D_MODEL), lambda i: (l, 0, 0))], out_specs=[row],
                 out_shape=[jax.ShapeDtypeStruct((s, D_MODEL), BF16)], args=[x, g3])
    return out


def _in_proj(h, win, s, tq, part, comm=None):
    pad = part
    dtype = BF16 if part else F32

    def body(a_ref, b_ref, o_ref):
        def compute():
            w = jnp.concatenate([b_ref[j] for j in range(4)], axis=1)
            o_ref[...] = _dot(a_ref[...], w).astype(dtype)

        if pad:
            i = pl.program_id(0)

            @pl.when(i == 0)
            def _():
                o_ref[...] = jnp.zeros(o_ref.shape, dtype)

            pl.when(i > 0)(compute)
        else:
            compute()

    return _call(
        body, name="in_proj_qkv" if part else "in_proj_conv", grid=(s // tq + pad,),
        in_specs=[pl.BlockSpec((tq, D_MODEL), lambda i: (jnp.maximum(i - pad, 0), 0)),
                  pl.BlockSpec((4, D_MODEL, PROJ_SHARD), lambda i: (part, 0, 0))],
        out_specs=[pl.BlockSpec((tq, 4 * PROJ_SHARD), lambda i: (i, 0))],
        out_shape=[jax.ShapeDtypeStruct((s + pad * tq, PROJ_WIDTH // 2), dtype)], args=[h, win], comm=comm)


def _conv_fwd(pc, wc, g3, l, s, tr):
    hb = tr // 8

    def body(pc_ref, prev_ref, wc_ref, g_ref, o_ref):
        i = pl.program_id(0)
        gmat = _group_matrix()
        for j in range(CONV_WIDTH // 128):
            c0, c1, c2 = 128 * j, CONV_WIDTH + 128 * j, 2 * CONV_WIDTH + 128 * j
            hc = pc_ref[:, c0:c0 + 128]
            bg = pc_ref[:, c1:c1 + 128]
            cg = pc_ref[:, c2:c2 + 128]
            u_prev = jnp.where(i > 0, prev_ref[:, c2:c2 + 128] * prev_ref[:, c0:c0 + 128], 0.0)
            u = cg * hc
            full = jnp.concatenate([u_prev, u], axis=0)
            u1 = pltpu.roll(full, 1, 0)[8:]
            u2 = pltpu.roll(full, 2, 0)[8:]
            out = (u2 * wc_ref[0:1, c0:c0 + 128] + u1 * wc_ref[1:2, c0:c0 + 128]
                   + u * wc_ref[2:3, c0:c0 + 128])
            yc = bg * out
            r = lax.rsqrt(_group_mean(yc * yc, gmat) + EPS)
            o_ref[:, c0:c0 + 128] = (yc * r * g_ref[:, c0:c0 + 128]).astype(BF16)

    out, = _call(
        body, name="conv_fwd", grid=(s // tr,),
        in_specs=[pl.BlockSpec((tr, 3 * CONV_WIDTH), lambda i: (i, 0)),
                  pl.BlockSpec((8, 3 * CONV_WIDTH), lambda i: (jnp.maximum(i * hb - 1, 0), 0)),
                  pl.BlockSpec((None, 8, CONV_WIDTH), lambda i: (l, 0, 0)),
                  pl.BlockSpec((None, 1, CONV_WIDTH), lambda i: (l, 0, 0))],
        out_specs=[pl.BlockSpec((tr, CONV_WIDTH), lambda i: (i, 0))],
        out_shape=[jax.ShapeDtypeStruct((s, CONV_WIDTH), BF16)], args=[pc, pc, wc, g3])
    return out


def _toeplitz_source():
    r_i = lax.broadcasted_iota(jnp.int32, (REL_PAD, TOEP), 0)
    m_i = lax.broadcasted_iota(jnp.int32, (REL_PAD, TOEP), 1)
    idx = jnp.clip((K_BAND - 1) - m_i, -REL_CLIP, REL_CLIP) + REL_CLIP
    return jnp.where(r_i == idx, 1.0, 0.0).astype(BF16)


def _bias_build(rbp, comm=None):
    n_layers = rbp.shape[0]

    def body(rb_ref, o_ref, t_ref):
        pmat = _toeplitz_source()
        hi, mid, lo = _split3(rb_ref[...])
        t_ref[...] = _dot(hi, pmat) + _dot(mid, pmat) + _dot(lo, pmat)
        shift = (CHUNK - 1) - lax.broadcasted_iota(jnp.int32, (CHUNK, TOEP), 0)
        kchunk = lax.broadcasted_iota(jnp.int32, (CHUNK, K_BAND), 1) >> 6
        for h in range(N_HEADS):
            b = jnp.broadcast_to(t_ref[pl.ds(h, 1), :], (CHUNK, TOEP))
            for bit in range(6):
                rolled = pltpu.roll(b, TOEP - (1 << bit), 1)
                b = jnp.where(((shift >> bit) & 1) == 1, rolled, b)
            for cq in range(Q_BLOCK // CHUNK):
                off = CHUNK * (Q_BLOCK // CHUNK - 1 - cq)
                band = pltpu.roll(b, TOEP - off, 1) if off else b
                dchunk = kchunk - cq
                in_band = jnp.where(dchunk >= 0, jnp.where(dchunk <= N_LEFT_CHUNKS, 1, 0), 0) == 1
                o_ref[h, CHUNK * cq:CHUNK * (cq + 1), :] = jnp.where(in_band, band[:, :K_BAND], NEG_INF)

    return _call(
        body, name="bias_build", grid=(n_layers,),
        in_specs=[pl.BlockSpec((None, N_HEADS, REL_PAD), lambda l: (l, 0, 0))],
        out_specs=[pl.BlockSpec((None, N_HEADS, Q_BLOCK, K_BAND), lambda l: (l, 0, 0, 0))],
        out_shape=[jax.ShapeDtypeStruct((n_layers, N_HEADS, Q_BLOCK, K_BAND), F32)],
        scratch=[pltpu.VMEM((N_HEADS, TOEP), F32)], args=[rbp], comm=comm)


def _bias_bwd(ds_sum):
    n_layers = ds_sum.shape[0]

    def body(ds_ref, o_ref, t_ref):
        pmat = _toeplitz_source()
        shift = (CHUNK - 1) - lax.broadcasted_iota(jnp.int32, (CHUNK, TOEP), 0)
        for h in range(N_HEADS):
            d = None
            for cq in range(Q_BLOCK // CHUNK):
                off = CHUNK * (Q_BLOCK // CHUNK - 1 - cq)
                part = jnp.concatenate([ds_ref[h, CHUNK * cq:CHUNK * (cq + 1), :],
                                        jnp.zeros((CHUNK, TOEP - K_BAND), F32)], axis=1)
                part = pltpu.roll(part, off, 1) if off else part
                d = part if d is None else d + part
            for bit in range(6):
                rolled = pltpu.roll(d, 1 << bit, 1)
                d = jnp.where(((shift >> bit) & 1) == 1, rolled, d)
            t_ref[pl.ds(h, 1), :] = jnp.sum(d, axis=0, keepdims=True)
        hi, mid, lo = _split3(t_ref[...])
        o_ref[...] = _dot_nt(hi, pmat) + _dot_nt(mid, pmat) + _dot_nt(lo, pmat)

    out, = _call(
        body, name="bias_bwd", grid=(n_layers,),
        in_specs=[pl.BlockSpec((None, N_HEADS, Q_BLOCK, K_BAND), lambda l: (l, 0, 0, 0))],
        out_specs=[pl.BlockSpec((None, N_HEADS, REL_PAD), lambda l: (l, 0, 0))],
        out_shape=[jax.ShapeDtypeStruct((n_layers, N_HEADS, REL_PAD), F32)],
        scratch=[pltpu.VMEM((N_HEADS, TOEP), F32)], args=[ds_sum])
    return out


def _attn_fwd(qkvp, biasm, g3, l, s, pad, comm=None):
    nb = s // Q_BLOCK
    qb0 = pad // Q_BLOCK
    scale = HEAD_DIM ** -0.5
    wide = 128 * ATTN_PAIRS

    def body(q_ref, k_ref, v_ref, b_ref, g_ref, o_ref, lse_ref, yn_ref):
        blk = pl.program_id(1)
        koff = pl.multiple_of(blk * Q_BLOCK + (pad - LEFT), Q_BLOCK)
        lane = lax.broadcasted_iota(jnp.int32, (1, 128), 1)
        kpos = lax.broadcasted_iota(jnp.int32, (1, K_BAND), 1) + (blk * Q_BLOCK - LEFT)
        kmask = jnp.where(kpos >= 0, 0.0, NEG_INF)
        gmat = _group_matrix()
        for pr in range(ATTN_PAIRS):
            ls = slice(128 * pr, 128 * (pr + 1))
            q = q_ref[:, ls]
            kb = k_ref[pl.ds(koff, K_BAND), ls]
            vb = v_ref[pl.ds(koff, K_BAND), ls]
            outs, lses = [], []
            for hh in range(2):
                in_head = (lane >> 6) == hh
                qm = jnp.where(in_head, q, jnp.zeros_like(q)) * jnp.asarray(scale, BF16)
                sc = _dot_nt(qm, kb) + b_ref[2 * pr + hh] + kmask
                m = jnp.max(sc, axis=1, keepdims=True)
                e = jnp.exp(sc - m)
                den = jnp.sum(e, axis=1, keepdims=True)
                outs.append(_dot(e.astype(BF16), vb) * (1.0 / den))
                lses.append(m + jnp.log(den))
            first = lane < HEAD_DIM
            o = jnp.where(first, outs[0], outs[1])
            o_ref[:, ls] = o
            lse_ref[:, ls] = jnp.where(first, lses[0], lses[1])
            r = lax.rsqrt(_group_mean(o * o, gmat) + EPS)
            yn_ref[:, ls] = (o * r * g_ref[:, ls]).astype(BF16)

    blk_out = pl.BlockSpec((Q_BLOCK, wide), lambda p, b: (b, p))
    n_grp = ATTN_WIDTH // wide
    return _call(
        body, name="attn_fwd", grid=(n_grp, nb),
        in_specs=[pl.BlockSpec((Q_BLOCK, wide), lambda p, b: (qb0 + b, p)),
                  pl.BlockSpec((s + pad, wide), lambda p, b: (0, n_grp + p)),
                  pl.BlockSpec((s + pad, wide), lambda p, b: (0, 2 * n_grp + p)),
                  pl.BlockSpec((None, 2 * ATTN_PAIRS, Q_BLOCK, K_BAND), lambda p, b: (l, p, 0, 0)),
                  pl.BlockSpec((None, 1, wide), lambda p, b: (l, 0, p))],
        out_specs=[blk_out, blk_out, blk_out],
        out_shape=[jax.ShapeDtypeStruct((s, ATTN_WIDTH), F32),
                   jax.ShapeDtypeStruct((s, ATTN_WIDTH), F32),
                   jax.ShapeDtypeStruct((s, ATTN_WIDTH), BF16)],
        args=[qkvp, qkvp, qkvp, biasm, g3], comm=comm)


def _out_proj_fwd(ync, yna, wout, x, g_post3, g_next3, l, s, tm):
    half = D_MODEL // 2

    def body(a1_ref, a2_ref, w_ref, x_ref, gp_ref, gn_ref, z_ref, xm_ref, h_ref):
        for rs in _row_subtiles(tm, SUB_ROWS):
            z = _dot(a1_ref[rs, :], w_ref[0:half, :]) + _dot(a2_ref[rs, :], w_ref[half:D_MODEL, :])
            z_ref[rs, :] = z
            xm = x_ref[rs, :] + z * _rstd(z) * gp_ref[...]
            xm_ref[rs, :] = xm
            h_ref[rs, :] = (xm * _rstd(xm) * gn_ref[...]).astype(BF16)

    row = pl.BlockSpec((tm, D_MODEL), lambda i: (i, 0))
    gain = pl.BlockSpec((None, 1, D_MODEL), lambda i: (l, 0, 0))
    return _call(
        body, name="out_proj_fwd", grid=(s // tm,),
        in_specs=[pl.BlockSpec((tm, half), lambda i: (i, 0)), pl.BlockSpec((tm, half), lambda i: (i, 0)),
                  pl.BlockSpec((D_MODEL, D_MODEL), lambda i: (0, 0)), row, gain, gain],
        out_specs=[row, row, row],
        out_shape=[jax.ShapeDtypeStruct((s, D_MODEL), F32), jax.ShapeDtypeStruct((s, D_MODEL), F32),
                   jax.ShapeDtypeStruct((s, D_MODEL), BF16)],
        args=[ync, yna, wout, x, g_post3, g_next3])


def _ffn_in_fwd(h2, wfin4, s, tm, comm=None):
    def body(h_ref, wg_ref, wu_ref, gu_ref, act_ref):
        h = h_ref[...]
        for cs in _COL_SUBTILES:
            gate = _dot_nt(h, wg_ref[cs, :])
            up = _dot_nt(h, wu_ref[cs, :])
            gu_ref[0, :, cs] = gate.astype(BF16)
            gu_ref[1, :, cs] = up.astype(BF16)
            act_ref[:, cs] = (gate * (1.0 / (1.0 + jnp.exp(-gate))) * up).astype(BF16)

    return _call(
        body, name="ffn_in_fwd", grid=(2, s // tm),
        in_specs=[pl.BlockSpec((tm, D_MODEL), lambda b, i: (i, 0)),
                  pl.BlockSpec((None, FF_PAIR, D_MODEL), lambda b, i: (b, 0, 0)),
                  pl.BlockSpec((None, FF_PAIR, D_MODEL), lambda b, i: (2 + b, 0, 0))],
        out_specs=[pl.BlockSpec((2, tm, FF_PAIR), lambda b, i: (0, i, b)),
                   pl.BlockSpec((tm, FF_PAIR), lambda b, i: (i, b))],
        out_shape=[jax.ShapeDtypeStruct((2, s, D_FF), BF16), jax.ShapeDtypeStruct((s, D_FF), BF16)],
        args=[h2, wfin4, wfin4], comm=comm)


def _ffn_out_fwd(act, wfo, xm, g_post3, g_next3, l, l_next, s, tm, comm=None):
    def body(a_ref, w_ref, x_ref, gp_ref, gn_ref, f_ref, xo_ref, h_ref):
        for rs in _row_subtiles(tm, SUB_ROWS):
            f = _dot(a_ref[rs, :], w_ref[...])
            f_ref[rs, :] = f
            xo = x_ref[rs, :] + f * _rstd(f) * gp_ref[...]
            xo_ref[rs, :] = xo
            h_ref[rs, :] = (xo * _rstd(xo) * gn_ref[...]).astype(BF16)

    row = pl.BlockSpec((tm, D_MODEL), lambda i: (i, 0))
    return _call(
        body, name="ffn_out_fwd", grid=(s // tm,),
        in_specs=[pl.BlockSpec((tm, D_FF), lambda i: (i, 0)),
                  pl.BlockSpec((D_FF, D_MODEL), lambda i: (0, 0), pipeline_mode=pl.Buffered(1)), row,
                  pl.BlockSpec((None, 1, D_MODEL), lambda i: (l, 0, 0)),
                  pl.BlockSpec((None, 1, D_MODEL), lambda i: (l_next, 0, 0))],
        out_specs=[row, row, row],
        out_shape=[jax.ShapeDtypeStruct((s, D_MODEL), F32), jax.ShapeDtypeStruct((s, D_MODEL), F32),
                   jax.ShapeDtypeStruct((s, D_MODEL), BF16)],
        args=[act, wfo, xm, g_post3, g_next3], comm=comm)


def _loss_grad(xf, target, f, g3, l, s, tm):
    def body(x_ref, t_ref, f_ref, g_ref, dx_ref, sq_ref, df_ref, dg_ref):
        i = pl.program_id(0)
        err = x_ref[...] - t_ref[...]
        dx = err * (1.0 / D_MODEL)
        dx_ref[...] = dx
        df, dyn = _norm_bwd_rows(f_ref[...], g_ref[...], dx)
        df_ref[...] = df.astype(BF16)
        _accum_cols(dg_ref, dyn, i == 0)
        cs = jnp.sum(err * err, axis=0, keepdims=True)
        part = cs[:, 0:128]
        for k in range(1, D_MODEL // 128):
            part = part + cs[:, 128 * k:128 * (k + 1)]

        @pl.when(i == 0)
        def _():
            sq_ref[...] = jnp.zeros(sq_ref.shape, F32)

        sq_ref[0:1, :] += part

    row = pl.BlockSpec((tm, D_MODEL), lambda i: (i, 0))
    return _call(
        body, name="loss_grad", grid=(s // tm,),
        in_specs=[row, row, row, pl.BlockSpec((None, 1, D_MODEL), lambda i: (l, 0, 0))],
        out_specs=[row, pl.BlockSpec((8, 128), lambda i: (0, 0)), row, pl.BlockSpec((8, D_MODEL), lambda i: (0, 0))],
        out_shape=[jax.ShapeDtypeStruct((s, D_MODEL), F32), jax.ShapeDtypeStruct((8, 128), F32),
                   jax.ShapeDtypeStruct((s, D_MODEL), BF16), jax.ShapeDtypeStruct((8, D_MODEL), F32)],
        args=[xf, target, f, g3])


def _norm_bwd_rows(v, g, dy):
    r = _rstd(v)
    vn = v * r
    gd = dy * g
    dv = r * (gd - vn * jnp.mean(vn * gd, axis=-1, keepdims=True))
    return dv, dy * vn


def _zero_first(refs, first):
    @pl.when(first)
    def _():
        for ref in refs:
            ref[...] = jnp.zeros(ref.shape, F32)


def _add_cols(ref, val):
    ref[0:1, :] += jnp.sum(val, axis=0, keepdims=True)


def _accum_cols(ref, val, first):
    _zero_first((ref,), first)
    _add_cols(ref, val)


def _row_subtiles(rows, sub):
    sub = min(sub, rows)
    return [slice(r, r + sub) for r in range(0, rows, sub)]


def _ffn_out_bwd(df, wfo, gu, act, s, tm, comm=None):
    nm = s // tm

    def body(df_ref, w_ref, gu_ref, act_ref, dgu_ref, dw_ref, acc_ref):
        i = pl.program_id(1)
        df = df_ref[...]
        _zero_first((acc_ref,), i == 0)
        acc_ref[...] += _dot_tn(act_ref[...], df)

        @pl.when(i == nm - 1)
        def _():
            dw_ref[...] = acc_ref[...].astype(BF16)

        for cs in _COL_SUBTILES:
            da = _dot_nt(df, w_ref[cs, :])
            g = gu_ref[0, :, cs].astype(F32)
            u = gu_ref[1, :, cs].astype(F32)
            sg = 1.0 / (1.0 + jnp.exp(-g))
            dgu_ref[0, :, cs] = (da * u * (sg * (1.0 + g * (1.0 - sg)))).astype(BF16)
            dgu_ref[1, :, cs] = (da * (g * sg)).astype(BF16)

    blk = pl.BlockSpec((2, tm, FF_PAIR), lambda b, i: (0, i, b))
    wblk = pl.BlockSpec((FF_PAIR, D_MODEL), lambda b, i: (b, 0))
    return _call(
        body, name="ffn_out_bwd", grid=(2, nm),
        in_specs=[pl.BlockSpec((tm, D_MODEL), lambda b, i: (i, 0)), wblk, blk,
                  pl.BlockSpec((tm, FF_PAIR), lambda b, i: (i, b))],
        out_specs=[blk, wblk],
        out_shape=[jax.ShapeDtypeStruct((2, s, D_FF), BF16), jax.ShapeDtypeStruct((D_FF, D_MODEL), BF16)],
        scratch=[pltpu.VMEM((FF_PAIR, D_MODEL), F32)],
        args=[df, wfo, gu, act], comm=comm)


def _dw_ffn_in(h2, dgu, s):
    def body(a_ref, b_ref, o_ref):
        o_ref[...] = _dot_tn(b_ref[...], a_ref[...]).astype(BF16)

    out, = _call(
        body, name="dw_ffn_in", grid=(4,),
        in_specs=[pl.BlockSpec((s, D_MODEL), lambda n: (0, 0), pipeline_mode=pl.Buffered(1)),
                  pl.BlockSpec((None, s, FF_PAIR), lambda n: (n // 2, 0, n % 2))],
        out_specs=[pl.BlockSpec((None, FF_PAIR, D_MODEL), lambda n: (n, 0, 0))],
        out_shape=[jax.ShapeDtypeStruct((4, FF_PAIR, D_MODEL), BF16)], args=[h2, dgu])
    return out


def _ffn_in_bwd(dgu, wfin, xm, g_pre3, dres, z, g_post3, l, s, tm, comm=None):
    def body(d_ref, w_ref, xm_ref, gp_ref, dres_ref, z_ref, gq_ref, dxm_ref, dz_ref, dgp_ref, dgq_ref):
        _zero_first((dgp_ref, dgq_ref), pl.program_id(0) == 0)
        for rs in _row_subtiles(tm, SUB_ROWS):
            dh = _dot(d_ref[0, rs, :], w_ref[0:D_FF, :]) + _dot(d_ref[1, rs, :], w_ref[D_FF:2 * D_FF, :])
            dx, dyn = _norm_bwd_rows(xm_ref[rs, :], gp_ref[...], dh)
            dxm = dres_ref[rs, :] + dx
            dxm_ref[rs, :] = dxm
            _add_cols(dgp_ref, dyn)
            dz, dyn2 = _norm_bwd_rows(z_ref[rs, :], gq_ref[...], dxm)
            dz_ref[rs, :] = dz.astype(BF16)
            _add_cols(dgq_ref, dyn2)

    row = pl.BlockSpec((tm, D_MODEL), lambda i: (i, 0))
    gain = pl.BlockSpec((None, 1, D_MODEL), lambda i: (l, 0, 0))
    dgs = pl.BlockSpec((8, D_MODEL), lambda i: (0, 0))
    return _call(
        body, name="ffn_in_bwd", grid=(s // tm,),
        in_specs=[pl.BlockSpec((2, tm, D_FF), lambda i: (0, i, 0)),
                  pl.BlockSpec((2 * D_FF, D_MODEL), lambda i: (0, 0), pipeline_mode=pl.Buffered(1)),
                  row, gain, row, row, gain],
        out_specs=[row, row, dgs, dgs],
        out_shape=[jax.ShapeDtypeStruct((s, D_MODEL), F32), jax.ShapeDtypeStruct((s, D_MODEL), BF16),
                   jax.ShapeDtypeStruct((8, D_MODEL), F32), jax.ShapeDtypeStruct((8, D_MODEL), F32)],
        args=[dgu, wfin, xm, g_pre3, dres, z, g_post3], comm=comm)


def _out_proj_bwd(dz, wout, o, g3, ync, yna, l, s, tm):
    nm = s // tm
    half = D_MODEL // 2

    def body(dz_ref, w_ref, o_ref, g_ref, a1_ref, a2_ref, dyc_ref, do_ref, dg_ref, dw_ref, acc_ref):
        i = pl.program_id(0)
        gmat = _group_matrix()
        _zero_first((dg_ref, acc_ref), i == 0)
        dzv = dz_ref[...]
        acc_ref[0:half, :] += _dot_tn(a1_ref[...], dzv)
        acc_ref[half:D_MODEL, :] += _dot_tn(a2_ref[...], dzv)

        @pl.when(i == nm - 1)
        def _():
            dw_ref[...] = acc_ref[...].astype(BF16)

        for rs in _row_subtiles(tm, SUB_ROWS):
            dy = _dot_nt(dz_ref[rs, :], w_ref[...])
            dyc_ref[rs, :] = dy[:, 0:CONV_WIDTH]
            for j in range(ATTN_WIDTH // 128):
                c0 = 128 * j
                ov = o_ref[rs, c0:c0 + 128]
                dyn = dy[:, CONV_WIDTH + c0:CONV_WIDTH + c0 + 128]
                r = lax.rsqrt(_group_mean(ov * ov, gmat) + EPS)
                on = ov * r
                gd = dyn * g_ref[:, c0:c0 + 128]
                do_ref[rs, c0:c0 + 128] = r * (gd - on * _group_mean(on * gd, gmat))
                dg_ref[0:1, c0:c0 + 128] += jnp.sum(dyn * on, axis=0, keepdims=True)

    halfrow = pl.BlockSpec((tm, ATTN_WIDTH), lambda i: (i, 0))
    return _call(
        body, name="out_proj_bwd", grid=(nm,),
        in_specs=[pl.BlockSpec((tm, D_MODEL), lambda i: (i, 0)),
                  pl.BlockSpec((D_MODEL, D_MODEL), lambda i: (0, 0)), halfrow,
                  pl.BlockSpec((None, 1, ATTN_WIDTH), lambda i: (l, 0, 0)), halfrow, halfrow],
        out_specs=[halfrow, halfrow, pl.BlockSpec((8, ATTN_WIDTH), lambda i: (0, 0)),
                   pl.BlockSpec((D_MODEL, D_MODEL), lambda i: (0, 0))],
        out_shape=[jax.ShapeDtypeStruct((s, CONV_WIDTH), F32), jax.ShapeDtypeStruct((s, ATTN_WIDTH), F32),
                   jax.ShapeDtypeStruct((8, ATTN_WIDTH), F32), jax.ShapeDtypeStruct((D_MODEL, D_MODEL), BF16)],
        scratch=[pltpu.VMEM((D_MODEL, D_MODEL), F32)],
        args=[dz, wout, o, g3, ync, yna])


def _conv_bwd(pc, dyc, wc, g3, dq, dk, dv, l, s, tr):
    hb = tr // 8
    nt = s // tr
    ext = tr + 16
    last_hb = s // 8 - 1

    def body(pc_ref, prev_ref, next_ref, dy_ref, dyn_ref, wc_ref, g_ref, dq_ref, dk_ref, dv_ref,
             dpc_ref, dw_ref, dg_ref):
        i = pl.program_id(0)
        for part, ref in enumerate((dq_ref, dk_ref, dv_ref)):
            c = 3 * CONV_WIDTH + ATTN_WIDTH * part
            dpc_ref[:, c:c + ATTN_WIDTH] = ref[...]
        gmat = _group_matrix()
        row = lax.broadcasted_iota(jnp.int32, (ext, 128), 0) + (i * tr - 8)
        inside = jnp.where(row >= 0, jnp.where(row < s, 1, 0), 0) == 1

        @pl.when(i == 0)
        def _():
            dw_ref[...] = jnp.zeros(dw_ref.shape, F32)
            dg_ref[...] = jnp.zeros(dg_ref.shape, F32)

        def extend(ref_prev, ref_mid, ref_next, c):
            parts = [ref_prev[:, c:c + 128] if ref_prev is not None else jnp.zeros((8, 128), F32),
                     ref_mid[:, c:c + 128], ref_next[:, c:c + 128]]
            return jnp.concatenate(parts, axis=0)

        for j in range(CONV_WIDTH // 128):
            c0, c1, c2 = 128 * j, CONV_WIDTH + 128 * j, 2 * CONV_WIDTH + 128 * j
            hc = extend(prev_ref, pc_ref, next_ref, c0)
            bg = extend(prev_ref, pc_ref, next_ref, c1)
            cg = extend(prev_ref, pc_ref, next_ref, c2)
            dyn = extend(None, dy_ref, dyn_ref, c0)
            w0, w1, w2 = (wc_ref[0:1, c0:c0 + 128], wc_ref[1:2, c0:c0 + 128], wc_ref[2:3, c0:c0 + 128])
            gain = g_ref[:, c0:c0 + 128]
            u = jnp.where(inside, cg * hc, 0.0)
            u1 = pltpu.roll(u, 1, 0)
            u2 = pltpu.roll(u, 2, 0)
            out = u2 * w0 + u1 * w1 + u * w2
            yc = bg * out
            r = lax.rsqrt(_group_mean(yc * yc, gmat) + EPS)
            ycn = yc * r
            gd = dyn * gain
            dyc = r * (gd - ycn * _group_mean(ycn * gd, gmat))
            dout = jnp.where(inside, dyc * bg, 0.0)
            du = dout * w2 + pltpu.roll(dout, ext - 1, 0) * w1 + pltpu.roll(dout, ext - 2, 0) * w0
            sl = slice(8, 8 + tr)
            dpc_ref[:, c0:c0 + 128] = (du[sl] * cg[sl]).astype(BF16)
            dpc_ref[:, c1:c1 + 128] = (dyc[sl] * out[sl]).astype(BF16)
            dpc_ref[:, c2:c2 + 128] = (du[sl] * hc[sl]).astype(BF16)
            dw_ref[0:1, c0:c0 + 128] += jnp.sum(dout[sl] * u2[sl], axis=0, keepdims=True)
            dw_ref[1:2, c0:c0 + 128] += jnp.sum(dout[sl] * u1[sl], axis=0, keepdims=True)
            dw_ref[2:3, c0:c0 + 128] += jnp.sum(dout[sl] * u[sl], axis=0, keepdims=True)
            dg_ref[0:1, c0:c0 + 128] += jnp.sum(dyn[sl] * ycn[sl], axis=0, keepdims=True)

    wide = 3 * CONV_WIDTH
    return _call(
        body, name="conv_bwd", grid=(nt,),
        in_specs=[pl.BlockSpec((tr, wide), lambda i: (i, 0)),
                  pl.BlockSpec((8, wide), lambda i: (jnp.maximum(i * hb - 1, 0), 0)),
                  pl.BlockSpec((8, wide), lambda i: (jnp.minimum((i + 1) * hb, last_hb), 0)),
                  pl.BlockSpec((tr, CONV_WIDTH), lambda i: (i, 0)),
                  pl.BlockSpec((8, CONV_WIDTH), lambda i: (jnp.minimum((i + 1) * hb, last_hb), 0)),
                  pl.BlockSpec((None, 8, CONV_WIDTH), lambda i: (l, 0, 0)),
                  pl.BlockSpec((None, 1, CONV_WIDTH), lambda i: (l, 0, 0)),
                  pl.BlockSpec((tr, ATTN_WIDTH), lambda i: (i, 0)),
                  pl.BlockSpec((tr, ATTN_WIDTH), lambda i: (i, 0)),
                  pl.BlockSpec((tr, ATTN_WIDTH), lambda i: (i, 0))],
        out_specs=[pl.BlockSpec((tr, PROJ_WIDTH), lambda i: (i, 0)),
                   pl.BlockSpec((8, CONV_WIDTH), lambda i: (0, 0)),
                   pl.BlockSpec((8, CONV_WIDTH), lambda i: (0, 0))],
        out_shape=[jax.ShapeDtypeStruct((s, PROJ_WIDTH), BF16), jax.ShapeDtypeStruct((8, CONV_WIDTH), F32),
                   jax.ShapeDtypeStruct((8, CONV_WIDTH), F32)],
        args=[pc, pc, pc, dyc, dyc, wc, g3, dq, dk, dv])


def _attn_bwd(qkvp, biasm, o, lse, do, l, s, pad, comm=None):
    nb = s // Q_BLOCK
    qb0 = pad // Q_BLOCK
    scale = HEAD_DIM ** -0.5
    wide = 128 * ATTN_PAIRS

    def body(q_ref, k_ref, v_ref, b_ref, o_ref, lse_ref, do_ref,
             dq_ref, dk_ref, dv_ref, ds_ref, dk_acc, dv_acc):
        blk = pl.program_id(1)

        @pl.when(blk == 0)
        def _():
            dk_acc[...] = jnp.zeros(dk_acc.shape, F32)
            dv_acc[...] = jnp.zeros(dv_acc.shape, F32)
            ds_ref[...] = jnp.zeros(ds_ref.shape, F32)

        koff = pl.multiple_of(blk * Q_BLOCK + (pad - LEFT), Q_BLOCK)
        lane = lax.broadcasted_iota(jnp.int32, (1, 128), 1)
        kpos = lax.broadcasted_iota(jnp.int32, (1, K_BAND), 1) + (blk * Q_BLOCK - LEFT)
        kmask = jnp.where(kpos >= 0, 0.0, NEG_INF)
        for pr in range(ATTN_PAIRS):
            ls = slice(128 * pr, 128 * (pr + 1))
            q = q_ref[:, ls]
            kb = k_ref[pl.ds(koff, K_BAND), ls]
            vb = v_ref[pl.ds(koff, K_BAND), ls]
            dov = do_ref[:, ls]
            lse_v = lse_ref[:, ls]
            prod = dov * o_ref[:, ls]
            dq_parts = []
            dk_new = jnp.zeros((K_BAND, 128), F32)
            dv_new = jnp.zeros((K_BAND, 128), F32)
            for hh in range(2):
                in_head = (lane >> 6) == hh
                qm = jnp.where(in_head, q, jnp.zeros_like(q)) * jnp.asarray(scale, BF16)
                dom = jnp.where(in_head, dov, 0.0).astype(BF16)
                delta = jnp.sum(jnp.where(in_head, prod, 0.0), axis=1, keepdims=True)
                lse_h = lse_v[:, HEAD_DIM * hh:HEAD_DIM * hh + 1]
                sc = _dot_nt(qm, kb) + b_ref[2 * pr + hh] + kmask
                p = jnp.exp(sc - lse_h)
                dp = _dot_nt(dom, vb)
                ds = p * (dp - delta)
                ds_ref[2 * pr + hh] += ds
                dsb = ds.astype(BF16)
                dq_parts.append(_dot(dsb, kb) * scale)
                dk_new = dk_new + _dot_tn(dsb, qm)
                dv_new = dv_new + _dot_tn(p.astype(BF16), dom)
            dq_ref[:, ls] = jnp.where(lane < HEAD_DIM, dq_parts[0], dq_parts[1]).astype(BF16)
            dk_acc[pl.ds(koff, K_BAND), ls] += dk_new
            dv_acc[pl.ds(koff, K_BAND), ls] += dv_new

        @pl.when(blk == nb - 1)
        def _():
            dk_ref[...] = dk_acc[pad:pad + s, :].astype(BF16)
            dv_ref[...] = dv_acc[pad:pad + s, :].astype(BF16)

    n_grp = ATTN_WIDTH // wide
    qblk = pl.BlockSpec((Q_BLOCK, wide), lambda p, b: (b, p))
    col = pl.BlockSpec((s, wide), lambda p, b: (0, p))
    shp = jax.ShapeDtypeStruct((s, ATTN_WIDTH), BF16)
    return _call(
        body, name="attn_bwd", grid=(n_grp, nb),
        in_specs=[pl.BlockSpec((Q_BLOCK, wide), lambda p, b: (qb0 + b, p)),
                  pl.BlockSpec((s + pad, wide), lambda p, b: (0, n_grp + p)),
                  pl.BlockSpec((s + pad, wide), lambda p, b: (0, 2 * n_grp + p)),
                  pl.BlockSpec((None, 2 * ATTN_PAIRS, Q_BLOCK, K_BAND), lambda p, b: (l, p, 0, 0)),
                  qblk, qblk, qblk],
        out_specs=[qblk, col, col, pl.BlockSpec((2 * ATTN_PAIRS, Q_BLOCK, K_BAND), lambda p, b: (p, 0, 0))],
        out_shape=[shp, shp, shp, jax.ShapeDtypeStruct((N_HEADS, Q_BLOCK, K_BAND), F32)],
        scratch=[pltpu.VMEM((s + pad, wide), F32), pltpu.VMEM((s + pad, wide), F32)],
        args=[qkvp, qkvp, qkvp, biasm, o, lse, do], comm=comm)


def _dw_in(h, dproj, s):
    def body(a_ref, b_ref, o_ref):
        acc = _dot_tn(a_ref[...], b_ref[...])
        o_ref[0] = acc[:, 0:PROJ_SHARD].astype(BF16)
        o_ref[1] = acc[:, PROJ_SHARD:2 * PROJ_SHARD].astype(BF16)

    out, = _call(
        body, name="dw_in", grid=(4,),
        in_specs=[pl.BlockSpec((s, D_MODEL), lambda n: (0, 0)),
                  pl.BlockSpec((s, 2 * PROJ_SHARD), lambda n: (0, n))],
        out_specs=[pl.BlockSpec((2, D_MODEL, PROJ_SHARD), lambda n: (n, 0, 0))],
        out_shape=[jax.ShapeDtypeStruct((N_DEV, D_MODEL, PROJ_SHARD), BF16)], args=[h, dproj])
    return out


def _in_proj_bwd(dproj, win, x, g3, dres, l, s, tm, f_prev=None, g_post3=None, comm=None):
    chain = f_prev is not None

    def body(d_ref, w_ref, x_ref, g_ref, dres_ref, *rest):
        if chain:
            f_ref, gq_ref, dx_ref, dg_ref, df_ref, dgq_ref = rest
            _zero_first((dg_ref, dgq_ref), pl.program_id(0) == 0)
        else:
            dx_ref, dg_ref = rest
            _zero_first((dg_ref,), pl.program_id(0) == 0)
        w = jnp.concatenate([w_ref[j] for j in range(N_DEV)], axis=1)
        for rs in _row_subtiles(tm, SUB_ROWS):
            dh = _dot_nt(d_ref[rs, :], w)
            dx, dyn = _norm_bwd_rows(x_ref[rs, :], g_ref[...], dh)
            dx = dres_ref[rs, :] + dx
            dx_ref[rs, :] = dx
            _add_cols(dg_ref, dyn)
            if chain:
                df, dyn2 = _norm_bwd_rows(f_ref[rs, :], gq_ref[...], dx)
                df_ref[rs, :] = df.astype(BF16)
                _add_cols(dgq_ref, dyn2)

    row = pl.BlockSpec((tm, D_MODEL), lambda i: (i, 0))
    dgs = pl.BlockSpec((8, D_MODEL), lambda i: (0, 0))
    in_specs = [pl.BlockSpec((tm, PROJ_WIDTH), lambda i: (i, 0)),
                pl.BlockSpec((N_DEV, D_MODEL, PROJ_SHARD), lambda i: (0, 0, 0), pipeline_mode=pl.Buffered(1)),
                row, pl.BlockSpec((None, 1, D_MODEL), lambda i: (l, 0, 0)), row]
    out_specs = [row, dgs]
    out_shape = [jax.ShapeDtypeStruct((s, D_MODEL), F32), jax.ShapeDtypeStruct((8, D_MODEL), F32)]
    args = [dproj, win, x, g3, dres]
    if chain:
        in_specs += [row, pl.BlockSpec((None, 1, D_MODEL), lambda i: (l - 1, 0, 0))]
        out_specs += [row, dgs]
        out_shape += [jax.ShapeDtypeStruct((s, D_MODEL), BF16), jax.ShapeDtypeStruct((8, D_MODEL), F32)]
        args += [f_prev, g_post3]
    return _call(body, name="in_proj_bwd", grid=(s // tm,), in_specs=in_specs, out_specs=out_specs,
                 out_shape=out_shape, args=args, comm=comm)


def _adamw(name, w, m, v, lands, owns=None, me=None):
    groups, rows, cols = w.shape
    assert len(lands) == groups
    n_part = lands[0].shape[0]
    tr = _row_tile(rows, tuple(c for c in (512, 352, 256, 176, 128, 64, 32, 16, 8) if c * cols <= 256 * 1024))
    c1 = 1.0 - ADAM_B1 ** ADAM_STEP
    c2 = 1.0 - ADAM_B2 ** ADAM_STEP
    n_own = groups if owns is not None else 0

    def body(*refs):
        if n_own:
            me_ref, refs = refs[0], refs[1:]
        w_ref, m_ref, v_ref = refs[:3]
        land_refs = refs[3:3 + groups]
        own_refs = refs[3 + groups:3 + groups + n_own]
        g_ref, d_ref, nm_ref, nv_ref = refs[3 + groups + n_own:]
        grp = pl.program_id(0)
        for gi in range(groups):
            @pl.when(grp == gi)
            def _():
                l_ref = land_refs[gi]
                g = None
                for p in range(n_part):
                    part = l_ref[p].astype(F32)
                    if n_own:
                        part = jnp.where(me_ref[0] == p, own_refs[gi][...].astype(F32), part)
                    g = part if g is None else g + part
                g_ref[...] = g
                m1 = ADAM_B1 * m_ref[...] + (1.0 - ADAM_B1) * g
                v1 = ADAM_B2 * v_ref[...] + (1.0 - ADAM_B2) * (g * g)
                nm_ref[...] = m1
                nv_ref[...] = v1
                d_ref[...] = -ADAM_LR * ((m1 / c1) / (jnp.sqrt(v1 / c2) + ADAM_EPS) + ADAM_WD * w_ref[...])

    blk = pl.BlockSpec((None, tr, cols), lambda g, i, *_: (g, i, 0))
    shp = jax.ShapeDtypeStruct((groups, rows, cols), F32)

    def land_spec(gi):
        return pl.BlockSpec((n_part, tr, cols), lambda g, i, *_: (0, jnp.where(g == gi, i, 0), 0))

    def own_spec(gi):
        if owns[gi].ndim == 3:
            return pl.BlockSpec((None, tr, cols), lambda g, i, me_ref: (me_ref[0], jnp.where(g == gi, i, 0), 0))
        return pl.BlockSpec((tr, cols), lambda g, i, me_ref: (jnp.where(g == gi, i, 0), 0))

    in_specs = [blk, blk, blk] + [land_spec(gi) for gi in range(groups)] + [own_spec(gi) for gi in range(n_own)]
    args = [w, m, v] + list(lands) + (list(owns) if n_own else [])
    if not n_own:
        return _call(body, name=name, grid=(groups, rows // tr), in_specs=in_specs,
                     out_specs=[blk, blk, blk, blk], out_shape=[shp, shp, shp, shp], args=args)
    return pl.pallas_call(
        body, name=name,
        grid_spec=pltpu.PrefetchScalarGridSpec(
            num_scalar_prefetch=1, grid=(groups, rows // tr), in_specs=in_specs, out_specs=[blk, blk, blk, blk]),
        out_shape=[shp, shp, shp, shp],
        compiler_params=pltpu.CompilerParams(dimension_semantics=("arbitrary", "arbitrary"),
                                             vmem_limit_bytes=VMEM_LIMIT),
    )(me, *args)


def _pack_small(rel, gco, gao, gpm, gqm, gpf, gqf):
    n_layers = rel.shape[0]
    relp = jnp.pad(rel, ((0, 0), (0, 0), (0, REL_PAD - rel.shape[2])))
    parts = [relp.reshape(n_layers * N_HEADS * REL_PAD // 128, 128)]
    parts += [a.reshape(-1, 128) for a in (gco, gao, gpm, gqm, gpf, gqf)]
    return jnp.concatenate(parts, axis=0)


def _pack_small_grads(d_rel, parts):
    n_layers = len(d_rel)
    keys = ("gco", "gao", "gpm", "gqm", "gpf", "gqf")
    arrays = list(d_rel) + [parts[k][l] for k in keys for l in range(n_layers)] + list(parts["wc"])
    rows = 0
    plan = []
    for l in range(n_layers):
        for h in range(N_HEADS):
            for t in range(REL_PAD // 128):
                plan.append((l, (0, h), t, rows))
                rows += 1
    for ki, k in enumerate(keys):
        for l in range(n_layers):
            for t in range(parts[k][l].shape[1] // 128):
                plan.append((n_layers * (1 + ki) + l, (0,), t, rows))
                rows += 1
    for l in range(n_layers):
        for tap in range(3):
            for t in range(CONV_WIDTH // 128):
                plan.append((n_layers * (1 + len(keys)) + l, (tap,), t, rows))
                rows += 1
    total = rows + (-rows) % 8

    def body(*refs):
        o_ref = refs[-1]
        if total > rows:
            o_ref[rows:total, :] = jnp.zeros((total - rows, 128), F32)
        for op, idx, t, dst in plan:
            lanes = slice(128 * t, 128 * (t + 1))
            if len(idx) == 2:
                o_ref[dst:dst + 1, :] = refs[op][idx[0], idx[1]:idx[1] + 1, lanes]
            else:
                o_ref[dst:dst + 1, :] = refs[op][idx[0]:idx[0] + 1, lanes]

    vmem = pl.BlockSpec(memory_space=pltpu.VMEM)
    out, = _call(body, name="pack_small_grads", grid=(), in_specs=[vmem] * len(arrays), out_specs=[vmem],
                 out_shape=[jax.ShapeDtypeStruct((total, 128), F32)], args=arrays)
    return out


def _unpack_small(p, n_layers):
    n_rel = n_layers * N_HEADS * REL_PAD // 128
    rel = p[:n_rel].reshape(n_layers, N_HEADS, REL_PAD)[:, :, :2 * REL_CLIP + 1]
    outs = [rel]
    r0 = n_rel
    for width in (CONV_WIDTH, ATTN_WIDTH, D_MODEL, D_MODEL, D_MODEL, D_MODEL):
        nr = n_layers * width // 128
        outs.append(p[r0:r0 + nr].reshape(n_layers, width))
        r0 += nr
    return outs


def kernel(x, w_in, w_conv, rel_bias, g_conv_out, g_attn_out, w_out, g_pre_mix, g_post_mix, g_pre_ffn, g_post_ffn, w_ffn_in, w_ffn_out, loss_target, m_w_in, m_w_conv, m_rel_bias, m_g_conv_out, m_g_attn_out, m_w_out, m_g_pre_mix, m_g_post_mix, m_g_pre_ffn, m_g_post_ffn, m_w_ffn_in, m_w_ffn_out, v_w_in, v_w_conv, v_rel_bias, v_g_conv_out, v_g_attn_out, v_w_out, v_g_pre_mix, v_g_post_mix, v_g_pre_ffn, v_g_post_ffn, v_w_ffn_in, v_w_ffn_out):
    n_layers = w_in.shape[0]
    s = x.shape[1]
    assert x.shape == (1, s, D_MODEL) and s % 1024 == 0
    assert w_in.shape == (n_layers, D_MODEL, PROJ_SHARD) and w_ffn_in.shape == (n_layers, D_MODEL, FF_SHARD)
    tm = 512
    tq = 1024 if s >= 2048 else 512
    tf = min(1024, s)
    x0 = x.reshape(s, D_MODEL)
    target = loss_target.reshape(s, D_MODEL)
    dev = _dev_index(lax.axis_index("x"), lax.axis_index("y"), lax.axis_index("c"))

    wt_ffn_in, mt_ffn_in, vt_ffn_in = (jnp.transpose(a, (0, 2, 1)) for a in (w_ffn_in, m_w_ffn_in, v_w_ffn_in))
    local_w = [_cast_bf16(w_in, "cast_w_in"), _cast_bf16(w_out, "cast_w_out"),
               _cast_bf16(wt_ffn_in, "cast_w_ffn_in"), _cast_bf16(w_ffn_out, "cast_w_ffn_out")]
    wc_local = jnp.pad(jnp.transpose(w_conv, (0, 2, 1)).reshape(-1), (0, 1024 - n_layers * 3 * 64)).reshape(8, 128)
    biasm, win_next, wc_g = _bias_build(jnp.pad(rel_bias, ((0, 0), (0, 0), (0, REL_PAD - rel_bias.shape[2]))),
                                        comm=_Gather([(local_w[0], 0), (wc_local, None)]))
    weights = [None] * n_layers
    wc_full = wc_g.reshape(N_DEV, 1024)[:, :n_layers * 3 * 64].reshape(N_DEV, n_layers, 3, 64)
    wc_full = jnp.transpose(wc_full, (1, 2, 0, 3)).reshape(n_layers, 3, CONV_WIDTH)
    wc_full = jnp.pad(wc_full, ((0, 0), (0, 5), (0, 0)))

    g3 = {k: v.reshape(n_layers, 1, -1) for k, v in dict(
        conv=g_conv_out, attn=g_attn_out, pre_mix=g_pre_mix, post_mix=g_post_mix,
        pre_ffn=g_pre_ffn, post_ffn=g_post_ffn).items()}

    saved = []
    xl = x0
    h = _norm_cast(x0, g3["pre_mix"], 0, tm)
    for l in range(n_layers):
        win = win_next
        pc, wout = _in_proj(h, win, s, tq, 0, comm=_Gather([(local_w[1], l)]))
        qkvp, = _in_proj(h, win, s, tq, 1)
        ync = _conv_fwd(pc, wc_full, g3["conv"], l, s, tm)
        o, lse, yna, wfin = _attn_fwd(qkvp, biasm, g3["attn"], l, s, tq, comm=_Gather([(local_w[2], l)]))
        wout = wout.reshape(D_MODEL, D_MODEL)
        z, xm, h2 = _out_proj_fwd(ync, yna, wout, xl, g3["post_mix"], g3["pre_ffn"], l, s, tq)
        gu, act, wfout = _ffn_in_fwd(h2, wfin.reshape(4, FF_PAIR, D_MODEL), s, tf, comm=_Gather([(local_w[3], l)]))
        wfo = wfout.reshape(D_FF, D_MODEL)
        l_next = min(l + 1, n_layers - 1)
        f, xo, h_next, *got = _ffn_out_fwd(act, wfo, xm, g3["post_ffn"], g3["pre_mix"], l, l_next, s, tm,
                                           comm=_Gather([(local_w[0], l + 1)]) if l + 1 < n_layers else None)
        weights[l] = [win, wout, wfin.reshape(2 * D_FF, D_MODEL), wfo]
        win_next = got[0] if got else None
        saved.append(dict(x=xl, h=h, pc=pc, qkvp=qkvp, ync=ync, yna=yna, o=o, lse=lse, z=z, xm=xm,
                          h2=h2, gu=gu, act=act, f=f))
        xl, h = xo, h_next

    dx, sq, df, dg_post_ffn = _loss_grad(xl, target, saved[-1]["f"], g3["post_ffn"], n_layers - 1, s, tm)
    loss = lax.psum(jnp.sum(sq) * (0.5 / D_MODEL), ("x", "y", "c"))

    lands = dict(win=[None] * n_layers, wout=[None] * n_layers, wfin=[None] * n_layers, wfout=[None] * n_layers)
    small = {k: [None] * n_layers for k in ("gco", "gao", "gpm", "gqm", "gpf", "gqf", "wc")}
    d_rel = [None] * n_layers
    started = []

    def start(name, keys, l, arrays):
        items = [(a, False) for a in arrays]
        send_sems, recv_sems, srcs, zones, token = _exchange_start(name + "_start", items)
        started.append((name, keys, l, items, send_sems, recv_sems, srcs, zones))
        return token[0:1, 0:1].reshape(1, 1, 1)

    for l in reversed(range(n_layers)):
        sv = saved[l]
        win, wout, wfin, wfo = weights[l]
        small["gqf"][l] = dg_post_ffn
        dgu, d_wfout = _ffn_out_bwd(df, wfo, sv["gu"], sv["act"], s, tm)
        d_wfout = d_wfout.reshape(N_DEV, FFO_SHARD, D_MODEL)
        d_wfin = _dw_ffn_in(sv["h2"], dgu, s).reshape(N_DEV, FF_SHARD, D_MODEL)
        token = start(f"exchange_ffn{l}", ("wfout", "wfin"), l, [d_wfout, d_wfin])
        dxm, dz, dg_pre_ffn, dg_post_mix = _ffn_in_bwd(
            dgu, wfin, sv["xm"], g3["pre_ffn"] + token, dx, sv["z"], g3["post_mix"], l, s, tm)
        small["gpf"][l] = dg_pre_ffn
        small["gqm"][l] = dg_post_mix
        dyc, do, dg_attn, d_wout = _out_proj_bwd(dz, wout, sv["o"], g3["attn"], sv["ync"], sv["yna"], l, s, tq)
        d_wout = d_wout.reshape(N_DEV, D_MODEL // N_DEV, D_MODEL)
        small["gao"][l] = dg_attn
        dq, dk, dv, ds_sum = _attn_bwd(sv["qkvp"], biasm, sv["o"], sv["lse"], do, l, s, tq)
        d_rel[l] = _bias_bwd(ds_sum[None])
        dproj, dwc, dg_conv = _conv_bwd(sv["pc"], dyc, wc_full, g3["conv"], dq, dk, dv, l, s, tm)
        small["wc"][l] = dwc
        small["gco"][l] = dg_conv
        d_win = _dw_in(sv["h"], dproj, s)
        token = start(f"exchange_mix{l}", ("wout", "win"), l, [d_wout, d_win])
        if l > 0:
            dx, dg_pre_mix, df, dg_post_ffn = _in_proj_bwd(
                dproj, win, sv["x"], g3["pre_mix"] + token, dxm, l, s, tm, f_prev=saved[l - 1]["f"],
                g_post3=g3["post_ffn"])
        else:
            dx, dg_pre_mix = _in_proj_bwd(dproj, win, sv["x"], g3["pre_mix"] + token, dxm, l, s, tm)
        small["gpm"][l] = dg_pre_mix
    grad_x = dx.reshape(1, s, D_MODEL)

    small_vec = _pack_small_grads(d_rel, small)
    small_items = [(small_vec, True)]
    small_sems = _exchange_start("exchange_small_start", small_items)

    owns = dict(win=[None] * n_layers, wout=[None] * n_layers, wfin=[None] * n_layers, wfout=[None] * n_layers)

    def wait(which, after):
        for name, keys, l, items, send_sems, recv_sems, srcs, zones in started:
            if keys == which:
                srcs, zones = _exchange_wait(name + "_wait", items, send_sems, recv_sems, srcs, zones, after)
                for key, src, zone in zip(keys, srcs, zones):
                    owns[key][l], lands[key][l] = src, zone

    me = dev.astype(jnp.int32).reshape(1)
    wait(("wfout", "wfin"), small_sems[4])
    r_fin = [jnp.transpose(t, (0, 2, 1)) for t in _adamw(
        "adamw_w_ffn_in", wt_ffn_in, mt_ffn_in, vt_ffn_in, lands["wfin"], owns["wfin"], me)]
    r_fout = _adamw("adamw_w_ffn_out", w_ffn_out, m_w_ffn_out, v_w_ffn_out, lands["wfout"], owns["wfout"], me)
    wait(("wout", "win"), r_fout[0])
    r_out = _adamw("adamw_w_out", w_out, m_w_out, v_w_out, lands["wout"], owns["wout"], me)
    r_in = _adamw("adamw_w_in", w_in, m_w_in, v_w_in, lands["win"], owns["win"], me)
    (small_own,), (land_small,) = _exchange_wait(
        "exchange_small_wait", small_items, small_sems[0], small_sems[1], small_sems[2], small_sems[3], r_in[0])

    n_rep = 64 * n_layers
    rep = _adamw(
        "adamw_replicated",
        _pack_small(rel_bias, g_conv_out, g_attn_out, g_pre_mix, g_post_mix, g_pre_ffn, g_post_ffn)[None],
        _pack_small(m_rel_bias, m_g_conv_out, m_g_attn_out, m_g_pre_mix, m_g_post_mix, m_g_pre_ffn, m_g_post_ffn)[None],
        _pack_small(v_rel_bias, v_g_conv_out, v_g_attn_out, v_g_pre_mix, v_g_post_mix, v_g_pre_ffn, v_g_post_ffn)[None],
        [land_small[:, :n_rep]], [small_own[:n_rep]], me)
    rep = [_unpack_small(t[0], n_layers) for t in rep]

    wc_rows = n_layers * 3 * CONV_WIDTH // 128
    zeros_wc = jnp.zeros((1, wc_rows, 128), F32)
    g_wc_full = _adamw("sum_w_conv", zeros_wc, zeros_wc, zeros_wc, [land_small[:, n_rep:n_rep + wc_rows]],
                       [small_own[n_rep:n_rep + wc_rows]], me)[0]
    g_wc_full = g_wc_full.reshape(n_layers, 3, CONV_WIDTH)
    g_wc = lax.dynamic_slice_in_dim(g_wc_full, dev * (CONV_WIDTH // N_DEV), CONV_WIDTH // N_DEV, axis=2)
    g_wc = jnp.transpose(g_wc, (0, 2, 1))

    def tiny(a):
        flat = a.reshape(-1)
        return jnp.pad(flat, (0, (-flat.shape[0]) % 1024)).reshape(1, -1, 128)

    r_wc = _adamw("adamw_w_conv", tiny(w_conv), tiny(m_w_conv), tiny(v_w_conv), [tiny(g_wc)])
    r_wc = [t.reshape(-1)[:w_conv.size].reshape(w_conv.shape) for t in r_wc]

    def leaf(kind):
        return [r_in[kind], r_wc[kind], rep[kind][0], rep[kind][1], rep[kind][2], r_out[kind],
                rep[kind][3], rep[kind][4], rep[kind][5], rep[kind][6], r_fin[kind], r_fout[kind]]

    return (loss, grad_x, *leaf(0), *leaf(1), *leaf(2), *leaf(3))
```

```python
import math

import jax
import jax.numpy as jnp
from jax import lax
from jax.experimental import pallas as pl
from jax.experimental.pallas import tpu as pltpu

F32 = jnp.float32
BF16 = jnp.bfloat16

D_MODEL = 1024
N_DEV = 8
CHUNK = 64
N_LEFT_CHUNKS = 8
CONV_WIDTH = 512
ATTN_WIDTH = 512
HEAD_DIM = 64
N_HEADS = 8
REL_CLIP = 128
REL_PAD = 384
PROJ_WIDTH = 3072
PROJ_SHARD = PROJ_WIDTH // N_DEV
D_FF = 2816
FF_SHARD = 2 * D_FF // N_DEV
FFO_SHARD = D_FF // N_DEV
FF_PAIR = 2 * FF_SHARD
_COL_SUBTILES = (slice(0, 768), slice(768, FF_PAIR))
EPS = 1e-6
NEG_INF = -1e30
ATTN_PAIRS = 2
Q_BLOCK = 4 * CHUNK
K_BAND = Q_BLOCK + N_LEFT_CHUNKS * CHUNK
LEFT = N_LEFT_CHUNKS * CHUNK
TOEP = 1024

ADAM_LR = 0.001
ADAM_B1 = 0.9
ADAM_B2 = 0.999
ADAM_EPS = 1e-08
ADAM_WD = 0.01
ADAM_STEP = 10

VMEM_LIMIT = 52 * 1024 * 1024
SUB_ROWS = 256
MESH = pl.DeviceIdType.MESH
ANY = pl.BlockSpec(memory_space=pl.ANY)

NT = (((1,), (1,)), ((), ()))
TN = (((0,), (0,)), ((), ()))


def _dot(a, b):
    return jnp.dot(a, b, preferred_element_type=F32)


def _dot_nt(a, b):
    return lax.dot_general(a, b, NT, preferred_element_type=F32)


def _dot_tn(a, b):
    return lax.dot_general(a, b, TN, preferred_element_type=F32)


def _rstd(v):
    return lax.rsqrt(jnp.mean(v * v, axis=-1, keepdims=True) + EPS)


def _group_matrix():
    r = lax.broadcasted_iota(jnp.int32, (128, 128), 0) >> 6
    c = lax.broadcasted_iota(jnp.int32, (128, 128), 1) >> 6
    return jnp.where(r == c, 1.0, 0.0).astype(BF16)


def _group_mean(v, gmat):
    hi = v.astype(BF16)
    lo = (v - hi.astype(F32)).astype(BF16)
    return (_dot(hi, gmat) + _dot(lo, gmat)) * (1.0 / HEAD_DIM)


def _split3(v):
    hi = v.astype(BF16)
    r1 = v - hi.astype(F32)
    mid = r1.astype(BF16)
    lo = (r1 - mid.astype(F32)).astype(BF16)
    return hi, mid, lo


def _row_tile(rows, cands=(1024, 512, 704, 256, 128, 64, 32, 16)):
    for c in cands:
        if rows % c == 0:
            return c
    return rows


def _dev_index(px, py, pc):
    return 4 * px + 2 * py + pc


def _when(cond):
    if cond is True:
        return lambda fn: fn()
    return pl.when(cond)


def _phases(grid):
    def phases():
        if not grid:
            return True, True, True
        lin = pl.program_id(0)
        for a in range(1, len(grid)):
            lin = lin * grid[a] + pl.program_id(a)
        total = math.prod(grid)
        return lin == 0, lin == total - 1, lin == total - 1
    return phases


class _Gather:
    def __init__(self, items):
        self.items = items
        self.args = [a for a, _ in items]
        n = len(items)
        self.out_shape = [jax.ShapeDtypeStruct((N_DEV,) + (a.shape if lay is None else a.shape[1:]), a.dtype)
                          for a, lay in items]
        self.scratch = [pltpu.SemaphoreType.DMA((n, 7)), pltpu.SemaphoreType.DMA((n, 7)),
                        pltpu.SemaphoreType.DMA((n,))]

    def _ctx(self, ins, outs, sems):
        send_sems, recv_sems, local_sems = sems
        x, y, c = lax.axis_index("x"), lax.axis_index("y"), lax.axis_index("c")
        chips = [(1 - x, y), (x, 1 - y), (1 - x, 1 - y)]

        def src(k):
            lay = self.items[k][1]
            return ins[k] if lay is None else ins[k].at[lay]

        def copy(k, s, idx, to, from_src=False):
            return pltpu.make_async_remote_copy(
                src_ref=src(k) if from_src else outs[k].at[idx], dst_ref=outs[k].at[idx],
                send_sem=send_sems.at[k, s], recv_sem=recv_sems.at[k, s],
                device_id=to, device_id_type=MESH)

        def local(k):
            return pltpu.make_async_copy(src(k), outs[k].at[_dev_index(x, y, c)], local_sems.at[k])

        return x, y, c, chips, copy, local

    def start(self, ins, outs, sems, cond):
        n = len(self.items)

        @_when(cond)
        def _():
            x, y, c, chips, copy, local = self._ctx(ins, outs, sems)
            me = _dev_index(x, y, c)
            for k in range(n):
                local(k).start()
                copy(k, 0, me, (x, y, 1 - c), from_src=True).start()
                for j, chip in enumerate(chips):
                    copy(k, 1 + j, me, (chip[0], chip[1], c), from_src=True).start()

    def forward(self, ins, outs, sems, cond):
        n = len(self.items)

        @_when(cond)
        def _():
            x, y, c, chips, copy, local = self._ctx(ins, outs, sems)
            for j, chip in enumerate(chips):
                idx = _dev_index(chip[0], chip[1], c)
                for k in range(n):
                    copy(k, 1 + j, idx, (x, y, c)).wait_recv()
                    copy(k, 4 + j, idx, (x, y, 1 - c)).start()

    def finish(self, ins, outs, sems, cond):
        n = len(self.items)

        @_when(cond)
        def _():
            x, y, c, chips, copy, local = self._ctx(ins, outs, sems)
            me = _dev_index(x, y, c)
            for k in range(n):
                copy(k, 0, _dev_index(x, y, 1 - c), (x, y, c)).wait_recv()
            for j, chip in enumerate(chips):
                idx = _dev_index(chip[0], chip[1], 1 - c)
                for k in range(n):
                    copy(k, 4 + j, idx, (x, y, c)).wait_recv()
            for k in range(n):
                for s in range(4):
                    copy(k, s, me, (x, y, c), from_src=True).wait_send()
                for j, chip in enumerate(chips):
                    copy(k, 4 + j, _dev_index(chip[0], chip[1], c), (x, y, c)).wait_send()
                local(k).wait()


_PEER_FLIPS = [(0, 0, 1), (1, 0, 0), (0, 1, 0), (1, 1, 0), (1, 0, 1), (0, 1, 1), (1, 1, 1)]


def _call(body, *, name, grid, in_specs, out_specs, out_shape, args, scratch=(), comm=None):
    n_hi, n_ho, n_hs = len(args), len(out_shape), len(scratch)
    c_args = list(comm.args) if comm else []
    c_out = list(comm.out_shape) if comm else []
    c_scr = list(comm.scratch) if comm else []
    phases = _phases(grid)

    def kern(*refs):
        cuts = [n_hi, len(c_args), n_ho, len(c_out), n_hs, len(c_scr)]
        parts, pos = [], 0
        for n in cuts:
            parts.append(refs[pos:pos + n])
            pos += n
        hi, ci, ho, co, hs, cs = parts
        if comm:
            first, mid, last = phases()
            comm.start(ci, co, cs, first)
            comm.forward(ci, co, cs, mid)
        body(*hi, *ho, *hs)
        if comm:
            comm.finish(ci, co, cs, last)

    sem = ("arbitrary",) * len(grid) if grid else None
    return pl.pallas_call(
        kern, name=name, grid=grid,
        in_specs=list(in_specs) + [ANY] * len(c_args),
        out_specs=list(out_specs) + [ANY] * len(c_out),
        out_shape=list(out_shape) + c_out,
        scratch_shapes=list(scratch) + c_scr,
        compiler_params=pltpu.CompilerParams(dimension_semantics=sem, vmem_limit_bytes=VMEM_LIMIT),
    )(*args, *c_args)


def _comm_only(name, comm):
    return _call(lambda: None, name=name, grid=(), in_specs=[], out_specs=[], out_shape=[], args=[], comm=comm)


HBM_SPEC = pl.BlockSpec(memory_space=pltpu.HBM)
SEM_SPEC = pl.BlockSpec(memory_space=pltpu.SEMAPHORE)
SIDE_EFFECT = pltpu.SideEffectType.DATAFLOW_SIDE_EFFECTING


def _exchange_peer(x, y, c, s):
    fx, fy, fc = _PEER_FLIPS[s]
    return x ^ fx, y ^ fy, c ^ fc


def _exchange_start(name, items):
    n = len(items)
    srcs = [pltpu.with_memory_space_constraint(a, pltpu.HBM) for a, _ in items]
    land_shapes = [(N_DEV,) + (a.shape if whole else a.shape[1:]) for a, whole in items]
    lands = [pltpu.with_memory_space_constraint(lax.empty(shp, a.dtype), pltpu.HBM)
             for shp, (a, _) in zip(land_shapes, items)]

    n_sem = 7 * n

    def body(*refs):
        src_refs, land_refs = refs[:n], refs[n:2 * n]
        send_sems = refs[2 * n:2 * n + n_sem]
        recv_sems = refs[2 * n + n_sem:2 * n + 2 * n_sem]
        token = refs[-1]
        x, y, c = lax.axis_index("x"), lax.axis_index("y"), lax.axis_index("c")
        me = _dev_index(x, y, c)
        for s in range(7):
            px, py, pc = _exchange_peer(x, y, c, s)
            for k in range(n):
                src = src_refs[k] if items[k][1] else src_refs[k].at[_dev_index(px, py, pc)]
                pltpu.make_async_remote_copy(
                    src_ref=src, dst_ref=land_refs[k].at[me],
                    send_sem=send_sems[7 * k + s], recv_sem=recv_sems[7 * k + s],
                    device_id=(px, py, pc), device_id_type=MESH).start()
        token[...] = jnp.zeros(token.shape, token.dtype)

    outs = pl.pallas_call(
        body, name=name,
        out_shape=(*[pltpu.SemaphoreType.DMA(())] * (2 * n_sem),
                   *[pltpu.HBM(a.shape, a.dtype) for a in srcs],
                   *[pltpu.HBM(shp, a.dtype) for shp, a in zip(land_shapes, srcs)],
                   jax.ShapeDtypeStruct((8, 128), F32)),
        in_specs=[HBM_SPEC] * (2 * n),
        out_specs=(*[SEM_SPEC] * (2 * n_sem), *[HBM_SPEC] * (2 * n), pl.BlockSpec(memory_space=pltpu.VMEM)),
        input_output_aliases={i: 2 * n_sem + i for i in range(2 * n)},
        compiler_params=pltpu.CompilerParams(has_side_effects=SIDE_EFFECT),
    )(*srcs, *lands)
    base = 2 * n_sem
    return (list(outs[:n_sem]), list(outs[n_sem:base]), list(outs[base:base + n]),
            list(outs[base + n:base + 2 * n]), outs[-1])


def _exchange_wait(name, items, send_sems, recv_sems, srcs, lands, after):
    n = len(items)

    n_sem = 7 * n

    def body(*refs):
        src_refs, land_refs = refs[:n], refs[n:2 * n]
        send_refs = refs[2 * n:2 * n + n_sem]
        recv_refs = refs[2 * n + n_sem:2 * n + 2 * n_sem]
        x, y, c = lax.axis_index("x"), lax.axis_index("y"), lax.axis_index("c")
        for s in range(7):
            for k in range(n):
                copy = pltpu.make_async_remote_copy(
                    src_ref=src_refs[k] if items[k][1] else src_refs[k].at[0], dst_ref=land_refs[k].at[0],
                    send_sem=send_refs[7 * k + s], recv_sem=recv_refs[7 * k + s],
                    device_id=(x, y, c), device_id_type=MESH)
                copy.wait_send()
                copy.wait_recv()

    outs = pl.pallas_call(
        body, name=name,
        out_shape=(*[pltpu.HBM(a.shape, a.dtype) for a in srcs], *[pltpu.HBM(a.shape, a.dtype) for a in lands]),
        in_specs=[HBM_SPEC] * (2 * n) + [SEM_SPEC] * (2 * n_sem) + [ANY],
        out_specs=tuple([HBM_SPEC] * (2 * n)),
        input_output_aliases={i: i for i in range(2 * n)},
        compiler_params=pltpu.CompilerParams(has_side_effects=SIDE_EFFECT),
    )(*srcs, *lands, *send_sems, *recv_sems, after)
    return list(outs[:n]), list(outs[n:])


def _cast_bf16(x, name):
    shape = x.shape
    x2 = x.reshape(-1, shape[-1])
    rows, cols = x2.shape
    tr = _row_tile(rows)

    def body(x_ref, o_ref):
        o_ref[...] = x_ref[...].astype(BF16)

    blk = pl.BlockSpec((tr, cols), lambda i: (i, 0))
    out, = _call(body, name=name, grid=(rows // tr,), in_specs=[blk], out_specs=[blk],
                 out_shape=[jax.ShapeDtypeStruct((rows, cols), BF16)], args=[x2])
    return out.reshape(shape)


def _norm_cast(x, g3, l, tm):
    s = x.shape[0]

    def body(x_ref, g_ref, o_ref):
        v = x_ref[...]
        o_ref[...] = (v * _rstd(v) * g_ref[...]).astype(BF16)

    row = pl.BlockSpec((tm, D_MODEL), lambda i: (i, 0))
    out, = _call(body, name="norm_cast", grid=(s // tm,),
                 in_specs=[row, pl.BlockSpec((None, 1, D_MODEL), lambda i: (l, 0, 0))], out_specs=[row],
                 out_shape=[jax.ShapeDtypeStruct((s, D_MODEL), BF16)], args=[x, g3])
    return out


def _in_proj(h, win, s, tq, part, comm=None):
    pad = part
    dtype = BF16 if part else F32

    def body(a_ref, b_ref, o_ref):
        def compute():
            w = jnp.concatenate([b_ref[j] for j in range(4)], axis=1)
            o_ref[...] = _dot(a_ref[...], w).astype(dtype)

        if pad:
            i = pl.program_id(0)

            @pl.when(i == 0)
            def _():
                o_ref[...] = jnp.zeros(o_ref.shape, dtype)

            pl.when(i > 0)(compute)
        else:
            compute()

    return _call(
        body, name="in_proj_qkv" if part else "in_proj_conv", grid=(s // tq + pad,),
        in_specs=[pl.BlockSpec((tq, D_MODEL), lambda i: (jnp.maximum(i - pad, 0), 0)),
                  pl.BlockSpec((4, D_MODEL, PROJ_SHARD), lambda i: (part, 0, 0))],
        out_specs=[pl.BlockSpec((tq, 4 * PROJ_SHARD), lambda i: (i, 0))],
        out_shape=[jax.ShapeDtypeStruct((s + pad * tq, PROJ_WIDTH // 2), dtype)], args=[h, win], comm=comm)


def _conv_fwd(pc, wc, g3, l, s, tr):
    hb = tr // 8

    def body(pc_ref, prev_ref, wc_ref, g_ref, o_ref):
        i = pl.program_id(0)
        gmat = _group_matrix()
        for j in range(CONV_WIDTH // 128):
            c0, c1, c2 = 128 * j, CONV_WIDTH + 128 * j, 2 * CONV_WIDTH + 128 * j
            hc = pc_ref[:, c0:c0 + 128]
            bg = pc_ref[:, c1:c1 + 128]
            cg = pc_ref[:, c2:c2 + 128]
            u_prev = jnp.where(i > 0, prev_ref[:, c2:c2 + 128] * prev_ref[:, c0:c0 + 128], 0.0)
            u = cg * hc
            full = jnp.concatenate([u_prev, u], axis=0)
            u1 = pltpu.roll(full, 1, 0)[8:]
            u2 = pltpu.roll(full, 2, 0)[8:]
            out = (u2 * wc_ref[0:1, c0:c0 + 128] + u1 * wc_ref[1:2, c0:c0 + 128]
                   + u * wc_ref[2:3, c0:c0 + 128])
            yc = bg * out
            r = lax.rsqrt(_group_mean(yc * yc, gmat) + EPS)
            o_ref[:, c0:c0 + 128] = (yc * r * g_ref[:, c0:c0 + 128]).astype(BF16)

    out, = _call(
        body, name="conv_fwd", grid=(s // tr,),
        in_specs=[pl.BlockSpec((tr, 3 * CONV_WIDTH), lambda i: (i, 0)),
                  pl.BlockSpec((8, 3 * CONV_WIDTH), lambda i: (jnp.maximum(i * hb - 1, 0), 0)),
                  pl.BlockSpec((None, 8, CONV_WIDTH), lambda i: (l, 0, 0)),
                  pl.BlockSpec((None, 1, CONV_WIDTH), lambda i: (l, 0, 0))],
        out_specs=[pl.BlockSpec((tr, CONV_WIDTH), lambda i: (i, 0))],
        out_shape=[jax.ShapeDtypeStruct((s, CONV_WIDTH), BF16)], args=[pc, pc, wc, g3])
    return out


def _toeplitz_source():
    r_i = lax.broadcasted_iota(jnp.int32, (REL_PAD, TOEP), 0)
    m_i = lax.broadcasted_iota(jnp.int32, (REL_PAD, TOEP), 1)
    idx = jnp.clip((K_BAND - 1) - m_i, -REL_CLIP, REL_CLIP) + REL_CLIP
    return jnp.where(r_i == idx, 1.0, 0.0).astype(BF16)


def _bias_build(rbp, comm=None):
    n_layers = rbp.shape[0]

    def body(rb_ref, o_ref, t_ref):
        pmat = _toeplitz_source()
        hi, mid, lo = _split3(rb_ref[...])
        t_ref[...] = _dot(hi, pmat) + _dot(mid, pmat) + _dot(lo, pmat)
        shift = (CHUNK - 1) - lax.broadcasted_iota(jnp.int32, (CHUNK, TOEP), 0)
        kchunk = lax.broadcasted_iota(jnp.int32, (CHUNK, K_BAND), 1) >> 6
        for h in range(N_HEADS):
            b = jnp.broadcast_to(t_ref[pl.ds(h, 1), :], (CHUNK, TOEP))
            for bit in range(6):
                rolled = pltpu.roll(b, TOEP - (1 << bit), 1)
                b = jnp.where(((shift >> bit) & 1) == 1, rolled, b)
            for cq in range(Q_BLOCK // CHUNK):
                off = CHUNK * (Q_BLOCK // CHUNK - 1 - cq)
                band = pltpu.roll(b, TOEP - off, 1) if off else b
                dchunk = kchunk - cq
                in_band = jnp.where(dchunk >= 0, jnp.where(dchunk <= N_LEFT_CHUNKS, 1, 0), 0) == 1
                o_ref[h, CHUNK * cq:CHUNK * (cq + 1), :] = jnp.where(in_band, band[:, :K_BAND], NEG_INF)

    return _call(
        body, name="bias_build", grid=(n_layers,),
        in_specs=[pl.BlockSpec((None, N_HEADS, REL_PAD), lambda l: (l, 0, 0))],
        out_specs=[pl.BlockSpec((None, N_HEADS, Q_BLOCK, K_BAND), lambda l: (l, 0, 0, 0))],
        out_shape=[jax.ShapeDtypeStruct((n_layers, N_HEADS, Q_BLOCK, K_BAND), F32)],
        scratch=[pltpu.VMEM((N_HEADS, TOEP), F32)], args=[rbp], comm=comm)


def _bias_bwd(ds_sum):
    n_layers = ds_sum.shape[0]

    def body(ds_ref, o_ref, t_ref):
        pmat = _toeplitz_source()
        shift = (CHUNK - 1) - lax.broadcasted_iota(jnp.int32, (CHUNK, TOEP), 0)
        for h in range(N_HEADS):
            d = None
            for cq in range(Q_BLOCK // CHUNK):
                off = CHUNK * (Q_BLOCK // CHUNK - 1 - cq)
                part = jnp.concatenate([ds_ref[h, CHUNK * cq:CHUNK * (cq + 1), :],
                                        jnp.zeros((CHUNK, TOEP - K_BAND), F32)], axis=1)
                part = pltpu.roll(part, off, 1) if off else part
                d = part if d is None else d + part
            for bit in range(6):
                rolled = pltpu.roll(d, 1 << bit, 1)
                d = jnp.where(((shift >> bit) & 1) == 1, rolled, d)
            t_ref[pl.ds(h, 1), :] = jnp.sum(d, axis=0, keepdims=True)
        hi, mid, lo = _split3(t_ref[...])
        o_ref[...] = _dot_nt(hi, pmat) + _dot_nt(mid, pmat) + _dot_nt(lo, pmat)

    out, = _call(
        body, name="bias_bwd", grid=(n_layers,),
        in_specs=[pl.BlockSpec((None, N_HEADS, Q_BLOCK, K_BAND), lambda l: (l, 0, 0, 0))],
        out_specs=[pl.BlockSpec((None, N_HEADS, REL_PAD), lambda l: (l, 0, 0))],
        out_shape=[jax.ShapeDtypeStruct((n_layers, N_HEADS, REL_PAD), F32)],
        scratch=[pltpu.VMEM((N_HEADS, TOEP), F32)], args=[ds_sum])
    return out


def _attn_fwd(qkvp, biasm, g3, l, s, pad, comm=None):
    nb = s // Q_BLOCK
    qb0 = pad // Q_BLOCK
    scale = HEAD_DIM ** -0.5
    wide = 128 * ATTN_PAIRS

    def body(q_ref, k_ref, v_ref, b_ref, g_ref, o_ref, lse_ref, yn_ref):
        blk = pl.program_id(1)
        koff = pl.multiple_of(blk * Q_BLOCK + (pad - LEFT), Q_BLOCK)
        lane = lax.broadcasted_iota(jnp.int32, (1, 128), 1)
        kpos = lax.broadcasted_iota(jnp.int32, (1, K_BAND), 1) + (blk * Q_BLOCK - LEFT)
        kmask = jnp.where(kpos >= 0, 0.0, NEG_INF)
        gmat = _group_matrix()
        for pr in range(ATTN_PAIRS):
            ls = slice(128 * pr, 128 * (pr + 1))
            q = q_ref[:, ls]
            kb = k_ref[pl.ds(koff, K_BAND), ls]
            vb = v_ref[pl.ds(koff, K_BAND), ls]
            outs, lses = [], []
            for hh in range(2):
                in_head = (lane >> 6) == hh
                qm = jnp.where(in_head, q, jnp.zeros_like(q)) * jnp.asarray(scale, BF16)
                sc = _dot_nt(qm, kb) + b_ref[2 * pr + hh] + kmask
                m = jnp.max(sc, axis=1, keepdims=True)
                e = jnp.exp(sc - m)
                den = jnp.sum(e, axis=1, keepdims=True)
                outs.append(_dot(e.astype(BF16), vb) * (1.0 / den))
                lses.append(m + jnp.log(den))
            first = lane < HEAD_DIM
            o = jnp.where(first, outs[0], outs[1])
            o_ref[:, ls] = o
            lse_ref[:, ls] = jnp.where(first, lses[0], lses[1])
            r = lax.rsqrt(_group_mean(o * o, gmat) + EPS)
            yn_ref[:, ls] = (o * r * g_ref[:, ls]).astype(BF16)

    blk_out = pl.BlockSpec((Q_BLOCK, wide), lambda p, b: (b, p))
    n_grp = ATTN_WIDTH // wide
    return _call(
        body, name="attn_fwd", grid=(n_grp, nb),
        in_specs=[pl.BlockSpec((Q_BLOCK, wide), lambda p, b: (qb0 + b, p)),
                  pl.BlockSpec((s + pad, wide), lambda p, b: (0, n_grp + p)),
                  pl.BlockSpec((s + pad, wide), lambda p, b: (0, 2 * n_grp + p)),
                  pl.BlockSpec((None, 2 * ATTN_PAIRS, Q_BLOCK, K_BAND), lambda p, b: (l, p, 0, 0)),
                  pl.BlockSpec((None, 1, wide), lambda p, b: (l, 0, p))],
        out_specs=[blk_out, blk_out, blk_out],
        out_shape=[jax.ShapeDtypeStruct((s, ATTN_WIDTH), F32),
                   jax.ShapeDtypeStruct((s, ATTN_WIDTH), F32),
                   jax.ShapeDtypeStruct((s, ATTN_WIDTH), BF16)],
        args=[qkvp, qkvp, qkvp, biasm, g3], comm=comm)


def _out_proj_fwd(ync, yna, wout, x, g_post3, g_next3, l, s, tm):
    half = D_MODEL // 2

    def body(a1_ref, a2_ref, w_ref, x_ref, gp_ref, gn_ref, z_ref, xm_ref, h_ref):
        for rs in _row_subtiles(tm, SUB_ROWS):
            z = _dot(a1_ref[rs, :], w_ref[0:half, :]) + _dot(a2_ref[rs, :], w_ref[half:D_MODEL, :])
            z_ref[rs, :] = z
            xm = x_ref[rs, :] + z * _rstd(z) * gp_ref[...]
            xm_ref[rs, :] = xm
            h_ref[rs, :] = (xm * _rstd(xm) * gn_ref[...]).astype(BF16)

    row = pl.BlockSpec((tm, D_MODEL), lambda i: (i, 0))
    gain = pl.BlockSpec((None, 1, D_MODEL), lambda i: (l, 0, 0))
    return _call(
        body, name="out_proj_fwd", grid=(s // tm,),
        in_specs=[pl.BlockSpec((tm, half), lambda i: (i, 0)), pl.BlockSpec((tm, half), lambda i: (i, 0)),
                  pl.BlockSpec((D_MODEL, D_MODEL), lambda i: (0, 0)), row, gain, gain],
        out_specs=[row, row, row],
        out_shape=[jax.ShapeDtypeStruct((s, D_MODEL), F32), jax.ShapeDtypeStruct((s, D_MODEL), F32),
                   jax.ShapeDtypeStruct((s, D_MODEL), BF16)],
        args=[ync, yna, wout, x, g_post3, g_next3])


def _ffn_in_fwd(h2, wfin4, s, tm, comm=None):
    def body(h_ref, wg_ref, wu_ref, gu_ref, act_ref):
        h = h_ref[...]
        for cs in _COL_SUBTILES:
            gate = _dot_nt(h, wg_ref[cs, :])
            up = _dot_nt(h, wu_ref[cs, :])
            gu_ref[0, :, cs] = gate.astype(BF16)
            gu_ref[1, :, cs] = up.astype(BF16)
            act_ref[:, cs] = (gate * (1.0 / (1.0 + jnp.exp(-gate))) * up).astype(BF16)

    return _call(
        body, name="ffn_in_fwd", grid=(2, s // tm),
        in_specs=[pl.BlockSpec((tm, D_MODEL), lambda b, i: (i, 0)),
                  pl.BlockSpec((None, FF_PAIR, D_MODEL), lambda b, i: (b, 0, 0)),
                  pl.BlockSpec((None, FF_PAIR, D_MODEL), lambda b, i: (2 + b, 0, 0))],
        out_specs=[pl.BlockSpec((2, tm, FF_PAIR), lambda b, i: (0, i, b)),
                   pl.BlockSpec((tm, FF_PAIR), lambda b, i: (i, b))],
        out_shape=[jax.ShapeDtypeStruct((2, s, D_FF), BF16), jax.ShapeDtypeStruct((s, D_FF), BF16)],
        args=[h2, wfin4, wfin4], comm=comm)


def _ffn_out_fwd(act, wfo, xm, g_post3, g_next3, l, l_next, s, tm, comm=None):
    def body(a_ref, w_ref, x_ref, gp_ref, gn_ref, f_ref, xo_ref, h_ref):
        for rs in _row_subtiles(tm, SUB_ROWS):
            f = _dot(a_ref[rs, :], w_ref[...])
            f_ref[rs, :] = f
            xo = x_ref[rs, :] + f * _rstd(f) * gp_ref[...]
            xo_ref[rs, :] = xo
            h_ref[rs, :] = (xo * _rstd(xo) * gn_ref[...]).astype(BF16)

    row = pl.BlockSpec((tm, D_MODEL), lambda i: (i, 0))
    return _call(
        body, name="ffn_out_fwd", grid=(s // tm,),
        in_specs=[pl.BlockSpec((tm, D_FF), lambda i: (i, 0)),
                  pl.BlockSpec((D_FF, D_MODEL), lambda i: (0, 0), pipeline_mode=pl.Buffered(1)), row,
                  pl.BlockSpec((None, 1, D_MODEL), lambda i: (l, 0, 0)),
                  pl.BlockSpec((None, 1, D_MODEL), lambda i: (l_next, 0, 0))],
        out_specs=[row, row, row],
        out_shape=[jax.ShapeDtypeStruct((s, D_MODEL), F32), jax.ShapeDtypeStruct((s, D_MODEL), F32),
                   jax.ShapeDtypeStruct((s, D_MODEL), BF16)],
        args=[act, wfo, xm, g_post3, g_next3], comm=comm)


def _ffn_out_loss(act, wfo, xm, g_post3, target, l, s, tm):
    def body(a_ref, w_ref, x_ref, gp_ref, t_ref, dx_ref, sq_ref, df_ref, dg_ref):
        _zero_first((sq_ref, dg_ref), pl.program_id(0) == 0)
        for rs in _row_subtiles(tm, SUB_ROWS):
            f = _dot(a_ref[rs, :], w_ref[...])
            gain = gp_ref[...]
            err = x_ref[rs, :] + f * _rstd(f) * gain - t_ref[rs, :]
            dx = err * (1.0 / D_MODEL)
            dx_ref[rs, :] = dx
            df, dyn = _norm_bwd_rows(f, gain, dx)
            df_ref[rs, :] = df.astype(BF16)
            _add_cols(dg_ref, dyn)
            cs = jnp.sum(err * err, axis=0, keepdims=True)
            part = cs[:, 0:128]
            for k in range(1, D_MODEL // 128):
                part = part + cs[:, 128 * k:128 * (k + 1)]
            sq_ref[0:1, :] += part

    row = pl.BlockSpec((tm, D_MODEL), lambda i: (i, 0))
    return _call(
        body, name="ffn_out_loss", grid=(s // tm,),
        in_specs=[pl.BlockSpec((tm, D_FF), lambda i: (i, 0)),
                  pl.BlockSpec((D_FF, D_MODEL), lambda i: (0, 0), pipeline_mode=pl.Buffered(1)), row,
                  pl.BlockSpec((None, 1, D_MODEL), lambda i: (l, 0, 0)), row],
        out_specs=[row, pl.BlockSpec((8, 128), lambda i: (0, 0)), row, pl.BlockSpec((8, D_MODEL), lambda i: (0, 0))],
        out_shape=[jax.ShapeDtypeStruct((s, D_MODEL), F32), jax.ShapeDtypeStruct((8, 128), F32),
                   jax.ShapeDtypeStruct((s, D_MODEL), BF16), jax.ShapeDtypeStruct((8, D_MODEL), F32)],
        args=[act, wfo, xm, g_post3, target])


def _norm_bwd_rows(v, g, dy):
    r = _rstd(v)
    vn = v * r
    gd = dy * g
    dv = r * (gd - vn * jnp.mean(vn * gd, axis=-1, keepdims=True))
    return dv, dy * vn


def _zero_first(refs, first):
    @pl.when(first)
    def _():
        for ref in refs:
            ref[...] = jnp.zeros(ref.shape, F32)


def _add_cols(ref, val):
    ref[0:1, :] += jnp.sum(val, axis=0, keepdims=True)


def _accum_cols(ref, val, first):
    _zero_first((ref,), first)
    _add_cols(ref, val)


def _row_subtiles(rows, sub):
    sub = min(sub, rows)
    return [slice(r, r + sub) for r in range(0, rows, sub)]


def _ffn_out_bwd(df, wfo, gu, act, s, tm, comm=None):
    nm = s // tm

    def body(df_ref, w_ref, gu_ref, act_ref, dgu_ref, dw_ref, acc_ref):
        i = pl.program_id(1)
        df = df_ref[...]
        _zero_first((acc_ref,), i == 0)
        acc_ref[...] += _dot_tn(act_ref[...], df)

        @pl.when(i == nm - 1)
        def _():
            dw_ref[...] = acc_ref[...].astype(BF16)

        for cs in _COL_SUBTILES:
            da = _dot_nt(df, w_ref[cs, :])
            g = gu_ref[0, :, cs].astype(F32)
            u = gu_ref[1, :, cs].astype(F32)
            sg = 1.0 / (1.0 + jnp.exp(-g))
            dgu_ref[0, :, cs] = (da * u * (sg * (1.0 + g * (1.0 - sg)))).astype(BF16)
            dgu_ref[1, :, cs] = (da * (g * sg)).astype(BF16)

    blk = pl.BlockSpec((2, tm, FF_PAIR), lambda b, i: (0, i, b))
    wblk = pl.BlockSpec((FF_PAIR, D_MODEL), lambda b, i: (b, 0))
    return _call(
        body, name="ffn_out_bwd", grid=(2, nm),
        in_specs=[pl.BlockSpec((tm, D_MODEL), lambda b, i: (i, 0)), wblk, blk,
                  pl.BlockSpec((tm, FF_PAIR), lambda b, i: (i, b))],
        out_specs=[blk, wblk],
        out_shape=[jax.ShapeDtypeStruct((2, s, D_FF), BF16), jax.ShapeDtypeStruct((D_FF, D_MODEL), BF16)],
        scratch=[pltpu.VMEM((FF_PAIR, D_MODEL), F32)],
        args=[df, wfo, gu, act], comm=comm)


def _dw_ffn_in(h2, dgu, s):
    def body(a_ref, b_ref, o_ref):
        o_ref[...] = _dot_tn(b_ref[...], a_ref[...]).astype(BF16)

    out, = _call(
        body, name="dw_ffn_in", grid=(4,),
        in_specs=[pl.BlockSpec((s, D_MODEL), lambda n: (0, 0), pipeline_mode=pl.Buffered(1)),
                  pl.BlockSpec((None, s, FF_PAIR), lambda n: (n // 2, 0, n % 2))],
        out_specs=[pl.BlockSpec((None, FF_PAIR, D_MODEL), lambda n: (n, 0, 0))],
        out_shape=[jax.ShapeDtypeStruct((4, FF_PAIR, D_MODEL), BF16)], args=[h2, dgu])
    return out


def _ffn_in_bwd(dgu, wfin, xm, g_pre3, dres, z, g_post3, l, s, tm, comm=None):
    def body(d_ref, w_ref, xm_ref, gp_ref, dres_ref, z_ref, gq_ref, dxm_ref, dz_ref, dgp_ref, dgq_ref):
        _zero_first((dgp_ref, dgq_ref), pl.program_id(0) == 0)
        for rs in _row_subtiles(tm, SUB_ROWS):
            dh = _dot(d_ref[0, rs, :], w_ref[0:D_FF, :]) + _dot(d_ref[1, rs, :], w_ref[D_FF:2 * D_FF, :])
            dx, dyn = _norm_bwd_rows(xm_ref[rs, :], gp_ref[...], dh)
            dxm = dres_ref[rs, :] + dx
            dxm_ref[rs, :] = dxm
            _add_cols(dgp_ref, dyn)
            dz, dyn2 = _norm_bwd_rows(z_ref[rs, :], gq_ref[...], dxm)
            dz_ref[rs, :] = dz.astype(BF16)
            _add_cols(dgq_ref, dyn2)

    row = pl.BlockSpec((tm, D_MODEL), lambda i: (i, 0))
    gain = pl.BlockSpec((None, 1, D_MODEL), lambda i: (l, 0, 0))
    dgs = pl.BlockSpec((8, D_MODEL), lambda i: (0, 0))
    return _call(
        body, name="ffn_in_bwd", grid=(s // tm,),
        in_specs=[pl.BlockSpec((2, tm, D_FF), lambda i: (0, i, 0)),
                  pl.BlockSpec((2 * D_FF, D_MODEL), lambda i: (0, 0), pipeline_mode=pl.Buffered(1)),
                  row, gain, row, row, gain],
        out_specs=[row, row, dgs, dgs],
        out_shape=[jax.ShapeDtypeStruct((s, D_MODEL), F32), jax.ShapeDtypeStruct((s, D_MODEL), BF16),
                   jax.ShapeDtypeStruct((8, D_MODEL), F32), jax.ShapeDtypeStruct((8, D_MODEL), F32)],
        args=[dgu, wfin, xm, g_pre3, dres, z, g_post3], comm=comm)


def _out_proj_bwd(dz, wout, o, g3, ync, yna, l, s, tm):
    nm = s // tm
    half = D_MODEL // 2

    def body(dz_ref, w_ref, o_ref, g_ref, a1_ref, a2_ref, dyc_ref, do_ref, dg_ref, dw_ref, acc_ref):
        i = pl.program_id(0)
        gmat = _group_matrix()
        _zero_first((dg_ref, acc_ref), i == 0)
        dzv = dz_ref[...]
        acc_ref[0:half, :] += _dot_tn(a1_ref[...], dzv)
        acc_ref[half:D_MODEL, :] += _dot_tn(a2_ref[...], dzv)

        @pl.when(i == nm - 1)
        def _():
            dw_ref[...] = acc_ref[...].astype(BF16)

        for rs in _row_subtiles(tm, SUB_ROWS):
            dy = _dot_nt(dz_ref[rs, :], w_ref[...])
            dyc_ref[rs, :] = dy[:, 0:CONV_WIDTH]
            for j in range(ATTN_WIDTH // 128):
                c0 = 128 * j
                ov = o_ref[rs, c0:c0 + 128]
                dyn = dy[:, CONV_WIDTH + c0:CONV_WIDTH + c0 + 128]
                r = lax.rsqrt(_group_mean(ov * ov, gmat) + EPS)
                on = ov * r
                gd = dyn * g_ref[:, c0:c0 + 128]
                do_ref[rs, c0:c0 + 128] = r * (gd - on * _group_mean(on * gd, gmat))
                dg_ref[0:1, c0:c0 + 128] += jnp.sum(dyn * on, axis=0, keepdims=True)

    halfrow = pl.BlockSpec((tm, ATTN_WIDTH), lambda i: (i, 0))
    return _call(
        body, name="out_proj_bwd", grid=(nm,),
        in_specs=[pl.BlockSpec((tm, D_MODEL), lambda i: (i, 0)),
                  pl.BlockSpec((D_MODEL, D_MODEL), lambda i: (0, 0)), halfrow,
                  pl.BlockSpec((None, 1, ATTN_WIDTH), lambda i: (l, 0, 0)), halfrow, halfrow],
        out_specs=[halfrow, halfrow, pl.BlockSpec((8, ATTN_WIDTH), lambda i: (0, 0)),
                   pl.BlockSpec((D_MODEL, D_MODEL), lambda i: (0, 0))],
        out_shape=[jax.ShapeDtypeStruct((s, CONV_WIDTH), F32), jax.ShapeDtypeStruct((s, ATTN_WIDTH), F32),
                   jax.ShapeDtypeStruct((8, ATTN_WIDTH), F32), jax.ShapeDtypeStruct((D_MODEL, D_MODEL), BF16)],
        scratch=[pltpu.VMEM((D_MODEL, D_MODEL), F32)],
        args=[dz, wout, o, g3, ync, yna])


def _conv_bwd(pc, dyc, wc, g3, dq, dk, dv, l, s, tr):
    hb = tr // 8
    nt = s // tr
    ext = tr + 16
    last_hb = s // 8 - 1

    def body(pc_ref, prev_ref, next_ref, dy_ref, dyn_ref, wc_ref, g_ref, dq_ref, dk_ref, dv_ref,
             dpc_ref, dw_ref, dg_ref):
        i = pl.program_id(0)
        for part, ref in enumerate((dq_ref, dk_ref, dv_ref)):
            c = 3 * CONV_WIDTH + ATTN_WIDTH * part
            dpc_ref[:, c:c + ATTN_WIDTH] = ref[...]
        gmat = _group_matrix()
        row = lax.broadcasted_iota(jnp.int32, (ext, 128), 0) + (i * tr - 8)
        inside = jnp.where(row >= 0, jnp.where(row < s, 1, 0), 0) == 1

        @pl.when(i == 0)
        def _():
            dw_ref[...] = jnp.zeros(dw_ref.shape, F32)
            dg_ref[...] = jnp.zeros(dg_ref.shape, F32)

        def extend(ref_prev, ref_mid, ref_next, c):
            parts = [ref_prev[:, c:c + 128] if ref_prev is not None else jnp.zeros((8, 128), F32),
                     ref_mid[:, c:c + 128], ref_next[:, c:c + 128]]
            return jnp.concatenate(parts, axis=0)

        for j in range(CONV_WIDTH // 128):
            c0, c1, c2 = 128 * j, CONV_WIDTH + 128 * j, 2 * CONV_WIDTH + 128 * j
            hc = extend(prev_ref, pc_ref, next_ref, c0)
            bg = extend(prev_ref, pc_ref, next_ref, c1)
            cg = extend(prev_ref, pc_ref, next_ref, c2)
            dyn = extend(None, dy_ref, dyn_ref, c0)
            w0, w1, w2 = (wc_ref[0:1, c0:c0 + 128], wc_ref[1:2, c0:c0 + 128], wc_ref[2:3, c0:c0 + 128])
            gain = g_ref[:, c0:c0 + 128]
            u = jnp.where(inside, cg * hc, 0.0)
            u1 = pltpu.roll(u, 1, 0)
            u2 = pltpu.roll(u, 2, 0)
            out = u2 * w0 + u1 * w1 + u * w2
            yc = bg * out
            r = lax.rsqrt(_group_mean(yc * yc, gmat) + EPS)
            ycn = yc * r
            gd = dyn * gain
            dyc = r * (gd - ycn * _group_mean(ycn * gd, gmat))
            dout = jnp.where(inside, dyc * bg, 0.0)
            du = dout * w2 + pltpu.roll(dout, ext - 1, 0) * w1 + pltpu.roll(dout, ext - 2, 0) * w0
            sl = slice(8, 8 + tr)
            dpc_ref[:, c0:c0 + 128] = (du[sl] * cg[sl]).astype(BF16)
            dpc_ref[:, c1:c1 + 128] = (dyc[sl] * out[sl]).astype(BF16)
            dpc_ref[:, c2:c2 + 128] = (du[sl] * hc[sl]).astype(BF16)
            dw_ref[0:1, c0:c0 + 128] += jnp.sum(dout[sl] * u2[sl], axis=0, keepdims=True)
            dw_ref[1:2, c0:c0 + 128] += jnp.sum(dout[sl] * u1[sl], axis=0, keepdims=True)
            dw_ref[2:3, c0:c0 + 128] += jnp.sum(dout[sl] * u[sl], axis=0, keepdims=True)
            dg_ref[0:1, c0:c0 + 128] += jnp.sum(dyn[sl] * ycn[sl], axis=0, keepdims=True)

    wide = 3 * CONV_WIDTH
    return _call(
        body, name="conv_bwd", grid=(nt,),
        in_specs=[pl.BlockSpec((tr, wide), lambda i: (i, 0)),
                  pl.BlockSpec((8, wide), lambda i: (jnp.maximum(i * hb - 1, 0), 0)),
                  pl.BlockSpec((8, wide), lambda i: (jnp.minimum((i + 1) * hb, last_hb), 0)),
                  pl.BlockSpec((tr, CONV_WIDTH), lambda i: (i, 0)),
                  pl.BlockSpec((8, CONV_WIDTH), lambda i: (jnp.minimum((i + 1) * hb, last_hb), 0)),
                  pl.BlockSpec((None, 8, CONV_WIDTH), lambda i: (l, 0, 0)),
                  pl.BlockSpec((None, 1, CONV_WIDTH), lambda i: (l, 0, 0)),
                  pl.BlockSpec((tr, ATTN_WIDTH), lambda i: (i, 0)),
                  pl.BlockSpec((tr, ATTN_WIDTH), lambda i: (i, 0)),
                  pl.BlockSpec((tr, ATTN_WIDTH), lambda i: (i, 0))],
        out_specs=[pl.BlockSpec((tr, PROJ_WIDTH), lambda i: (i, 0)),
                   pl.BlockSpec((8, CONV_WIDTH), lambda i: (0, 0)),
                   pl.BlockSpec((8, CONV_WIDTH), lambda i: (0, 0))],
        out_shape=[jax.ShapeDtypeStruct((s, PROJ_WIDTH), BF16), jax.ShapeDtypeStruct((8, CONV_WIDTH), F32),
                   jax.ShapeDtypeStruct((8, CONV_WIDTH), F32)],
        args=[pc, pc, pc, dyc, dyc, wc, g3, dq, dk, dv])


def _attn_bwd(qkvp, biasm, o, lse, do, l, s, pad, comm=None):
    nb = s // Q_BLOCK
    qb0 = pad // Q_BLOCK
    scale = HEAD_DIM ** -0.5
    wide = 128 * ATTN_PAIRS

    def body(q_ref, k_ref, v_ref, b_ref, o_ref, lse_ref, do_ref,
             dq_ref, dk_ref, dv_ref, ds_ref, dk_acc, dv_acc):
        blk = pl.program_id(1)

        @pl.when(blk == 0)
        def _():
            dk_acc[...] = jnp.zeros(dk_acc.shape, F32)
            dv_acc[...] = jnp.zeros(dv_acc.shape, F32)
            ds_ref[...] = jnp.zeros(ds_ref.shape, F32)

        koff = pl.multiple_of(blk * Q_BLOCK + (pad - LEFT), Q_BLOCK)
        lane = lax.broadcasted_iota(jnp.int32, (1, 128), 1)
        kpos = lax.broadcasted_iota(jnp.int32, (1, K_BAND), 1) + (blk * Q_BLOCK - LEFT)
        kmask = jnp.where(kpos >= 0, 0.0, NEG_INF)
        for pr in range(ATTN_PAIRS):
            ls = slice(128 * pr, 128 * (pr + 1))
            q = q_ref[:, ls]
            kb = k_ref[pl.ds(koff, K_BAND), ls]
            vb = v_ref[pl.ds(koff, K_BAND), ls]
            dov = do_ref[:, ls]
            lse_v = lse_ref[:, ls]
            prod = dov * o_ref[:, ls]
            dq_parts = []
            dk_new = jnp.zeros((K_BAND, 128), F32)
            dv_new = jnp.zeros((K_BAND, 128), F32)
            for hh in range(2):
                in_head = (lane >> 6) == hh
                qm = jnp.where(in_head, q, jnp.zeros_like(q)) * jnp.asarray(scale, BF16)
                dom = jnp.where(in_head, dov, 0.0).astype(BF16)
                delta = jnp.sum(jnp.where(in_head, prod, 0.0), axis=1, keepdims=True)
                lse_h = lse_v[:, HEAD_DIM * hh:HEAD_DIM * hh + 1]
                sc = _dot_nt(qm, kb) + b_ref[2 * pr + hh] + kmask
                p = jnp.exp(sc - lse_h)
                dp = _dot_nt(dom, vb)
                ds = p * (dp - delta)
                ds_ref[2 * pr + hh] += ds
                dsb = ds.astype(BF16)
                dq_parts.append(_dot(dsb, kb) * scale)
                dk_new = dk_new + _dot_tn(dsb, qm)
                dv_new = dv_new + _dot_tn(p.astype(BF16), dom)
            dq_ref[:, ls] = jnp.where(lane < HEAD_DIM, dq_parts[0], dq_parts[1]).astype(BF16)
            dk_acc[pl.ds(koff, K_BAND), ls] += dk_new
            dv_acc[pl.ds(koff, K_BAND), ls] += dv_new

        @pl.when(blk == nb - 1)
        def _():
            dk_ref[...] = dk_acc[pad:pad + s, :].astype(BF16)
            dv_ref[...] = dv_acc[pad:pad + s, :].astype(BF16)

    n_grp = ATTN_WIDTH // wide
    qblk = pl.BlockSpec((Q_BLOCK, wide), lambda p, b: (b, p))
    col = pl.BlockSpec((s, wide), lambda p, b: (0, p))
    shp = jax.ShapeDtypeStruct((s, ATTN_WIDTH), BF16)
    return _call(
        body, name="attn_bwd", grid=(n_grp, nb),
        in_specs=[pl.BlockSpec((Q_BLOCK, wide), lambda p, b: (qb0 + b, p)),
                  pl.BlockSpec((s + pad, wide), lambda p, b: (0, n_grp + p)),
                  pl.BlockSpec((s + pad, wide), lambda p, b: (0, 2 * n_grp + p)),
                  pl.BlockSpec((None, 2 * ATTN_PAIRS, Q_BLOCK, K_BAND), lambda p, b: (l, p, 0, 0)),
                  qblk, qblk, qblk],
        out_specs=[qblk, col, col, pl.BlockSpec((2 * ATTN_PAIRS, Q_BLOCK, K_BAND), lambda p, b: (p, 0, 0))],
        out_shape=[shp, shp, shp, jax.ShapeDtypeStruct((N_HEADS, Q_BLOCK, K_BAND), F32)],
        scratch=[pltpu.VMEM((s + pad, wide), F32), pltpu.VMEM((s + pad, wide), F32)],
        args=[qkvp, qkvp, qkvp, biasm, o, lse, do], comm=comm)


def _dw_in(h, dproj, s):
    def body(a_ref, b_ref, o_ref):
        acc = _dot_tn(a_ref[...], b_ref[...])
        o_ref[0] = acc[:, 0:PROJ_SHARD].astype(BF16)
        o_ref[1] = acc[:, PROJ_SHARD:2 * PROJ_SHARD].astype(BF16)

    out, = _call(
        body, name="dw_in", grid=(4,),
        in_specs=[pl.BlockSpec((s, D_MODEL), lambda n: (0, 0)),
                  pl.BlockSpec((s, 2 * PROJ_SHARD), lambda n: (0, n))],
        out_specs=[pl.BlockSpec((2, D_MODEL, PROJ_SHARD), lambda n: (n, 0, 0))],
        out_shape=[jax.ShapeDtypeStruct((N_DEV, D_MODEL, PROJ_SHARD), BF16)], args=[h, dproj])
    return out


def _in_proj_bwd(dproj, win, x, g3, dres, l, s, tm, f_prev=None, g_post3=None, comm=None):
    chain = f_prev is not None

    def body(d_ref, w_ref, x_ref, g_ref, dres_ref, *rest):
        if chain:
            f_ref, gq_ref, dx_ref, dg_ref, df_ref, dgq_ref = rest
            _zero_first((dg_ref, dgq_ref), pl.program_id(0) == 0)
        else:
            dx_ref, dg_ref = rest
            _zero_first((dg_ref,), pl.program_id(0) == 0)
        w = jnp.concatenate([w_ref[j] for j in range(N_DEV)], axis=1)
        for rs in _row_subtiles(tm, SUB_ROWS):
            dh = _dot_nt(d_ref[rs, :], w)
            dx, dyn = _norm_bwd_rows(x_ref[rs, :], g_ref[...], dh)
            dx = dres_ref[rs, :] + dx
            dx_ref[rs, :] = dx
            _add_cols(dg_ref, dyn)
            if chain:
                df, dyn2 = _norm_bwd_rows(f_ref[rs, :], gq_ref[...], dx)
                df_ref[rs, :] = df.astype(BF16)
                _add_cols(dgq_ref, dyn2)

    row = pl.BlockSpec((tm, D_MODEL), lambda i: (i, 0))
    dgs = pl.BlockSpec((8, D_MODEL), lambda i: (0, 0))
    in_specs = [pl.BlockSpec((tm, PROJ_WIDTH), lambda i: (i, 0)),
                pl.BlockSpec((N_DEV, D_MODEL, PROJ_SHARD), lambda i: (0, 0, 0), pipeline_mode=pl.Buffered(1)),
                row, pl.BlockSpec((None, 1, D_MODEL), lambda i: (l, 0, 0)), row]
    out_specs = [row, dgs]
    out_shape = [jax.ShapeDtypeStruct((s, D_MODEL), F32), jax.ShapeDtypeStruct((8, D_MODEL), F32)]
    args = [dproj, win, x, g3, dres]
    if chain:
        in_specs += [row, pl.BlockSpec((None, 1, D_MODEL), lambda i: (l - 1, 0, 0))]
        out_specs += [row, dgs]
        out_shape += [jax.ShapeDtypeStruct((s, D_MODEL), BF16), jax.ShapeDtypeStruct((8, D_MODEL), F32)]
        args += [f_prev, g_post3]
    return _call(body, name="in_proj_bwd", grid=(s // tm,), in_specs=in_specs, out_specs=out_specs,
                 out_shape=out_shape, args=args, comm=comm)


def _adamw(name, w, m, v, lands, owns=None, me=None):
    groups, rows, cols = w.shape
    assert len(lands) == groups
    n_part = lands[0].shape[0]
    tr = _row_tile(rows, tuple(c for c in (512, 352, 256, 176, 128, 64, 32, 16, 8) if c * cols <= 256 * 1024))
    c1 = 1.0 - ADAM_B1 ** ADAM_STEP
    c2 = 1.0 - ADAM_B2 ** ADAM_STEP
    n_own = groups if owns is not None else 0

    def body(*refs):
        if n_own:
            me_ref, refs = refs[0], refs[1:]
        w_ref, m_ref, v_ref = refs[:3]
        land_refs = refs[3:3 + groups]
        own_refs = refs[3 + groups:3 + groups + n_own]
        g_ref, d_ref, nm_ref, nv_ref = refs[3 + groups + n_own:]
        grp = pl.program_id(0)
        for gi in range(groups):
            @pl.when(grp == gi)
            def _():
                l_ref = land_refs[gi]
                g = None
                for p in range(n_part):
                    part = l_ref[p].astype(F32)
                    if n_own:
                        part = jnp.where(me_ref[0] == p, own_refs[gi][...].astype(F32), part)
                    g = part if g is None else g + part
                g_ref[...] = g
                m1 = ADAM_B1 * m_ref[...] + (1.0 - ADAM_B1) * g
                v1 = ADAM_B2 * v_ref[...] + (1.0 - ADAM_B2) * (g * g)
                nm_ref[...] = m1
                nv_ref[...] = v1
                d_ref[...] = -ADAM_LR * ((m1 / c1) / (jnp.sqrt(v1 / c2) + ADAM_EPS) + ADAM_WD * w_ref[...])

    blk = pl.BlockSpec((None, tr, cols), lambda g, i, *_: (g, i, 0))
    shp = jax.ShapeDtypeStruct((groups, rows, cols), F32)

    def land_spec(gi):
        return pl.BlockSpec((n_part, tr, cols), lambda g, i, *_: (0, jnp.where(g == gi, i, 0), 0))

    def own_spec(gi):
        if owns[gi].ndim == 3:
            return pl.BlockSpec((None, tr, cols), lambda g, i, me_ref: (me_ref[0], jnp.where(g == gi, i, 0), 0))
        return pl.BlockSpec((tr, cols), lambda g, i, me_ref: (jnp.where(g == gi, i, 0), 0))

    in_specs = [blk, blk, blk] + [land_spec(gi) for gi in range(groups)] + [own_spec(gi) for gi in range(n_own)]
    args = [w, m, v] + list(lands) + (list(owns) if n_own else [])
    if not n_own:
        return _call(body, name=name, grid=(groups, rows // tr), in_specs=in_specs,
                     out_specs=[blk, blk, blk, blk], out_shape=[shp, shp, shp, shp], args=args)
    return pl.pallas_call(
        body, name=name,
        grid_spec=pltpu.PrefetchScalarGridSpec(
            num_scalar_prefetch=1, grid=(groups, rows // tr), in_specs=in_specs, out_specs=[blk, blk, blk, blk]),
        out_shape=[shp, shp, shp, shp],
        compiler_params=pltpu.CompilerParams(dimension_semantics=("arbitrary", "arbitrary"),
                                             vmem_limit_bytes=VMEM_LIMIT),
    )(me, *args)


def _pack_small(rel, gco, gao, gpm, gqm, gpf, gqf):
    n_layers = rel.shape[0]
    relp = jnp.pad(rel, ((0, 0), (0, 0), (0, REL_PAD - rel.shape[2])))
    parts = [relp.reshape(n_layers * N_HEADS * REL_PAD // 128, 128)]
    parts += [a.reshape(-1, 128) for a in (gco, gao, gpm, gqm, gpf, gqf)]
    return jnp.concatenate(parts, axis=0)


def _pack_small_grads(d_rel, parts):
    n_layers = len(d_rel)
    keys = ("gco", "gao", "gpm", "gqm", "gpf", "gqf")
    arrays = list(d_rel) + [parts[k][l] for k in keys for l in range(n_layers)] + list(parts["wc"])
    rows = 0
    plan = []
    for l in range(n_layers):
        for h in range(N_HEADS):
            for t in range(REL_PAD // 128):
                plan.append((l, (0, h), t, rows))
                rows += 1
    for ki, k in enumerate(keys):
        for l in range(n_layers):
            for t in range(parts[k][l].shape[1] // 128):
                plan.append((n_layers * (1 + ki) + l, (0,), t, rows))
                rows += 1
    for l in range(n_layers):
        for tap in range(3):
            for t in range(CONV_WIDTH // 128):
                plan.append((n_layers * (1 + len(keys)) + l, (tap,), t, rows))
                rows += 1
    total = rows + (-rows) % 8

    def body(*refs):
        o_ref = refs[-1]
        if total > rows:
            o_ref[rows:total, :] = jnp.zeros((total - rows, 128), F32)
        for op, idx, t, dst in plan:
            lanes = slice(128 * t, 128 * (t + 1))
            if len(idx) == 2:
                o_ref[dst:dst + 1, :] = refs[op][idx[0], idx[1]:idx[1] + 1, lanes]
            else:
                o_ref[dst:dst + 1, :] = refs[op][idx[0]:idx[0] + 1, lanes]

    vmem = pl.BlockSpec(memory_space=pltpu.VMEM)
    out, = _call(body, name="pack_small_grads", grid=(), in_specs=[vmem] * len(arrays), out_specs=[vmem],
                 out_shape=[jax.ShapeDtypeStruct((total, 128), F32)], args=arrays)
    return out


def _unpack_small(p, n_layers):
    n_rel = n_layers * N_HEADS * REL_PAD // 128
    rel = p[:n_rel].reshape(n_layers, N_HEADS, REL_PAD)[:, :, :2 * REL_CLIP + 1]
    outs = [rel]
    r0 = n_rel
    for width in (CONV_WIDTH, ATTN_WIDTH, D_MODEL, D_MODEL, D_MODEL, D_MODEL):
        nr = n_layers * width // 128
        outs.append(p[r0:r0 + nr].reshape(n_layers, width))
        r0 += nr
    return outs


def kernel(x, w_in, w_conv, rel_bias, g_conv_out, g_attn_out, w_out, g_pre_mix, g_post_mix, g_pre_ffn, g_post_ffn, w_ffn_in, w_ffn_out, loss_target, m_w_in, m_w_conv, m_rel_bias, m_g_conv_out, m_g_attn_out, m_w_out, m_g_pre_mix, m_g_post_mix, m_g_pre_ffn, m_g_post_ffn, m_w_ffn_in, m_w_ffn_out, v_w_in, v_w_conv, v_rel_bias, v_g_conv_out, v_g_attn_out, v_w_out, v_g_pre_mix, v_g_post_mix, v_g_pre_ffn, v_g_post_ffn, v_w_ffn_in, v_w_ffn_out):
    n_layers = w_in.shape[0]
    s = x.shape[1]
    assert x.shape == (1, s, D_MODEL) and s % 1024 == 0
    assert w_in.shape == (n_layers, D_MODEL, PROJ_SHARD) and w_ffn_in.shape == (n_layers, D_MODEL, FF_SHARD)
    tm = 512
    tq = 1024 if s >= 2048 else 512
    tf = min(1024, s)
    x0 = x.reshape(s, D_MODEL)
    target = loss_target.reshape(s, D_MODEL)
    dev = _dev_index(lax.axis_index("x"), lax.axis_index("y"), lax.axis_index("c"))

    wt_ffn_in, mt_ffn_in, vt_ffn_in = (jnp.transpose(a, (0, 2, 1)) for a in (w_ffn_in, m_w_ffn_in, v_w_ffn_in))
    local_w = [_cast_bf16(w_in, "cast_w_in"), _cast_bf16(w_out, "cast_w_out"),
               _cast_bf16(wt_ffn_in, "cast_w_ffn_in"), _cast_bf16(w_ffn_out, "cast_w_ffn_out")]
    wc_local = jnp.pad(jnp.transpose(w_conv, (0, 2, 1)).reshape(-1), (0, 1024 - n_layers * 3 * 64)).reshape(8, 128)
    biasm, win_next, wc_g = _bias_build(jnp.pad(rel_bias, ((0, 0), (0, 0), (0, REL_PAD - rel_bias.shape[2]))),
                                        comm=_Gather([(local_w[0], 0), (wc_local, None)]))
    weights = [None] * n_layers
    wc_full = wc_g.reshape(N_DEV, 1024)[:, :n_layers * 3 * 64].reshape(N_DEV, n_layers, 3, 64)
    wc_full = jnp.transpose(wc_full, (1, 2, 0, 3)).reshape(n_layers, 3, CONV_WIDTH)
    wc_full = jnp.pad(wc_full, ((0, 0), (0, 5), (0, 0)))

    g3 = {k: v.reshape(n_layers, 1, -1) for k, v in dict(
        conv=g_conv_out, attn=g_attn_out, pre_mix=g_pre_mix, post_mix=g_post_mix,
        pre_ffn=g_pre_ffn, post_ffn=g_post_ffn).items()}

    saved = []
    xl = x0
    h = _norm_cast(x0, g3["pre_mix"], 0, tm)
    for l in range(n_layers):
        win = win_next
        pc, wout = _in_proj(h, win, s, tq, 0, comm=_Gather([(local_w[1], l)]))
        qkvp, = _in_proj(h, win, s, tq, 1)
        ync = _conv_fwd(pc, wc_full, g3["conv"], l, s, tm)
        o, lse, yna, wfin = _attn_fwd(qkvp, biasm, g3["attn"], l, s, tq, comm=_Gather([(local_w[2], l)]))
        wout = wout.reshape(D_MODEL, D_MODEL)
        z, xm, h2 = _out_proj_fwd(ync, yna, wout, xl, g3["post_mix"], g3["pre_ffn"], l, s, tq)
        gu, act, wfout = _ffn_in_fwd(h2, wfin.reshape(4, FF_PAIR, D_MODEL), s, tf, comm=_Gather([(local_w[3], l)]))
        wfo = wfout.reshape(D_FF, D_MODEL)
        weights[l] = [win, wout, wfin.reshape(2 * D_FF, D_MODEL), wfo]
        sv = dict(x=xl, h=h, pc=pc, qkvp=qkvp, ync=ync, yna=yna, o=o, lse=lse, z=z, xm=xm, h2=h2, gu=gu, act=act)
        if l + 1 < n_layers:
            sv["f"], xl, h, win_next = _ffn_out_fwd(act, wfo, xm, g3["post_ffn"], g3["pre_mix"], l, l + 1, s, tm,
                                                    comm=_Gather([(local_w[0], l + 1)]))
        else:
            dx, sq, df, dg_post_ffn = _ffn_out_loss(act, wfo, xm, g3["post_ffn"], target, l, s, tm)
        saved.append(sv)

    loss = lax.psum(jnp.sum(sq) * (0.5 / D_MODEL), ("x", "y", "c"))

    lands = dict(win=[None] * n_layers, wout=[None] * n_layers, wfin=[None] * n_layers, wfout=[None] * n_layers)
    small = {k: [None] * n_layers for k in ("gco", "gao", "gpm", "gqm", "gpf", "gqf", "wc")}
    d_rel = [None] * n_layers
    started = []

    def start(name, keys, l, arrays):
        items = [(a, False) for a in arrays]
        send_sems, recv_sems, srcs, zones, token = _exchange_start(name + "_start", items)
        started.append((name, keys, l, items, send_sems, recv_sems, srcs, zones))
        return token[0:1, 0:1].reshape(1, 1, 1)

    for l in reversed(range(n_layers)):
        sv = saved[l]
        win, wout, wfin, wfo = weights[l]
        small["gqf"][l] = dg_post_ffn
        dgu, d_wfout = _ffn_out_bwd(df, wfo, sv["gu"], sv["act"], s, tm)
        d_wfout = d_wfout.reshape(N_DEV, FFO_SHARD, D_MODEL)
        d_wfin = _dw_ffn_in(sv["h2"], dgu, s).reshape(N_DEV, FF_SHARD, D_MODEL)
        g_pre_ffn = g3["pre_ffn"]
        if l == 0:
            g_pre_ffn = g_pre_ffn + start("exchange_ffn0", ("wfout", "wfin"), l, [d_wfout, d_wfin])
        dxm, dz, dg_pre_ffn, dg_post_mix = _ffn_in_bwd(
            dgu, wfin, sv["xm"], g_pre_ffn, dx, sv["z"], g3["post_mix"], l, s, tm)
        small["gpf"][l] = dg_pre_ffn
        small["gqm"][l] = dg_post_mix
        dyc, do, dg_attn, d_wout = _out_proj_bwd(dz, wout, sv["o"], g3["attn"], sv["ync"], sv["yna"], l, s, tq)
        d_wout = d_wout.reshape(N_DEV, D_MODEL // N_DEV, D_MODEL)
        small["gao"][l] = dg_attn
        dq, dk, dv, ds_sum = _attn_bwd(sv["qkvp"], biasm, sv["o"], sv["lse"], do, l, s, tq)
        d_rel[l] = _bias_bwd(ds_sum[None])
        dproj, dwc, dg_conv = _conv_bwd(sv["pc"], dyc, wc_full, g3["conv"], dq, dk, dv, l, s, tm)
        small["wc"][l] = dwc
        small["gco"][l] = dg_conv
        d_win = _dw_in(sv["h"], dproj, s)
        if l == 0:
            token = start("exchange_mix0", ("wout", "win"), l, [d_wout, d_win])
        else:
            token = start(f"exchange_layer{l}", ("wfout", "wfin", "wout", "win"), l, [d_wfout, d_wfin, d_wout, d_win])
        if l > 0:
            dx, dg_pre_mix, df, dg_post_ffn = _in_proj_bwd(
                dproj, win, sv["x"], g3["pre_mix"] + token, dxm, l, s, tm, f_prev=saved[l - 1]["f"],
                g_post3=g3["post_ffn"])
        else:
            dx, dg_pre_mix = _in_proj_bwd(dproj, win, sv["x"], g3["pre_mix"] + token, dxm, l, s, tm)
        small["gpm"][l] = dg_pre_mix
    grad_x = dx.reshape(1, s, D_MODEL)

    small_vec = _pack_small_grads(d_rel, small)
    small_items = [(small_vec, True)]
    small_sems = _exchange_start("exchange_small_start", small_items)

    owns = dict(win=[None] * n_layers, wout=[None] * n_layers, wfin=[None] * n_layers, wfout=[None] * n_layers)

    def wait(last, after):
        for name, keys, l, items, send_sems, recv_sems, srcs, zones in started:
            if (name == "exchange_mix0") == last:
                srcs, zones = _exchange_wait(name + "_wait", items, send_sems, recv_sems, srcs, zones, after)
                for key, src, zone in zip(keys, srcs, zones):
                    owns[key][l], lands[key][l] = src, zone

    me = dev.astype(jnp.int32).reshape(1)
    wait(False, small_sems[4])
    r_fin = [jnp.transpose(t, (0, 2, 1)) for t in _adamw(
        "adamw_w_ffn_in", wt_ffn_in, mt_ffn_in, vt_ffn_in, lands["wfin"], owns["wfin"], me)]
    r_fout = _adamw("adamw_w_ffn_out", w_ffn_out, m_w_ffn_out, v_w_ffn_out, lands["wfout"], owns["wfout"], me)
    wait(True, r_fout[0])
    r_out = _adamw("adamw_w_out", w_out, m_w_out, v_w_out, lands["wout"], owns["wout"], me)
    r_in = _adamw("adamw_w_in", w_in, m_w_in, v_w_in, lands["win"], owns["win"], me)
    (small_own,), (land_small,) = _exchange_wait(
        "exchange_small_wait", small_items, small_sems[0], small_sems[1], small_sems[2], small_sems[3], r_in[0])

    n_rep = 64 * n_layers
    rep = _adamw(
        "adamw_replicated",
        _pack_small(rel_bias, g_conv_out, g_attn_out, g_pre_mix, g_post_mix, g_pre_ffn, g_post_ffn)[None],
        _pack_small(m_rel_bias, m_g_conv_out, m_g_attn_out, m_g_pre_mix, m_g_post_mix, m_g_pre_ffn, m_g_post_ffn)[None],
        _pack_small(v_rel_bias, v_g_conv_out, v_g_attn_out, v_g_pre_mix, v_g_post_mix, v_g_pre_ffn, v_g_post_ffn)[None],
        [land_small[:, :n_rep]], [small_own[:n_rep]], me)
    rep = [_unpack_small(t[0], n_layers) for t in rep]

    wc_rows = n_layers * 3 * CONV_WIDTH // 128
    zeros_wc = jnp.zeros((1, wc_rows, 128), F32)
    g_wc_full = _adamw("sum_w_conv", zeros_wc, zeros_wc, zeros_wc, [land_small[:, n_rep:n_rep + wc_rows]],
                       [small_own[n_rep:n_rep + wc_rows]], me)[0]
    g_wc_full = g_wc_full.reshape(n_layers, 3, CONV_WIDTH)
    g_wc = lax.dynamic_slice_in_dim(g_wc_full, dev * (CONV_WIDTH // N_DEV), CONV_WIDTH // N_DEV, axis=2)
    g_wc = jnp.transpose(g_wc, (0, 2, 1))

    def tiny(a):
        flat = a.reshape(-1)
        return jnp.pad(flat, (0, (-flat.shape[0]) % 1024)).reshape(1, -1, 128)

    r_wc = _adamw("adamw_w_conv", tiny(w_conv), tiny(m_w_conv), tiny(v_w_conv), [tiny(g_wc)])
    r_wc = [t.reshape(-1)[:w_conv.size].reshape(w_conv.shape) for t in r_wc]

    def leaf(kind):
        return [r_in[kind], r_wc[kind], rep[kind][0], rep[kind][1], rep[kind][2], r_out[kind],
                rep[kind][3], rep[kind][4], rep[kind][5], rep[kind][6], r_fin[kind], r_fout[kind]]

    return (loss, grad_x, *leaf(0), *leaf(1), *leaf(2), *leaf(3))
```

```python
import math

import jax
import jax.numpy as jnp
from jax import lax
from jax.experimental import pallas as pl
from jax.experimental.pallas import tpu as pltpu

F32 = jnp.float32
BF16 = jnp.bfloat16

D_MODEL = 1024
N_DEV = 8
CHUNK = 64
N_LEFT_CHUNKS = 8
CONV_WIDTH = 512
ATTN_WIDTH = 512
HEAD_DIM = 64
N_HEADS = 8
REL_CLIP = 128
REL_PAD = 384
PROJ_WIDTH = 3072
PROJ_SHARD = PROJ_WIDTH // N_DEV
D_FF = 2816
FF_SHARD = 2 * D_FF // N_DEV
FFO_SHARD = D_FF // N_DEV
FF_PAIR = 2 * FF_SHARD
_COL_SUBTILES = (slice(0, 768), slice(768, FF_PAIR))
EPS = 1e-6
NEG_INF = -1e30
ATTN_PAIRS = 2
Q_BLOCK = 4 * CHUNK
K_BAND = Q_BLOCK + N_LEFT_CHUNKS * CHUNK
LEFT = N_LEFT_CHUNKS * CHUNK
TOEP = 1024

ADAM_LR = 0.001
ADAM_B1 = 0.9
ADAM_B2 = 0.999
ADAM_EPS = 1e-08
ADAM_WD = 0.01
ADAM_STEP = 10

VMEM_LIMIT = 52 * 1024 * 1024
SUB_ROWS = 256
MESH = pl.DeviceIdType.MESH
ANY = pl.BlockSpec(memory_space=pl.ANY)

NT = (((1,), (1,)), ((), ()))
TN = (((0,), (0,)), ((), ()))


def _dot(a, b):
    return jnp.dot(a, b, preferred_element_type=F32)


def _dot_nt(a, b):
    return lax.dot_general(a, b, NT, preferred_element_type=F32)


def _dot_tn(a, b):
    return lax.dot_general(a, b, TN, preferred_element_type=F32)


def _rstd(v):
    return lax.rsqrt(jnp.mean(v * v, axis=-1, keepdims=True) + EPS)


def _group_matrix():
    r = lax.broadcasted_iota(jnp.int32, (128, 128), 0) >> 6
    c = lax.broadcasted_iota(jnp.int32, (128, 128), 1) >> 6
    return jnp.where(r == c, 1.0, 0.0).astype(BF16)


def _group_mean(v, gmat):
    hi = v.astype(BF16)
    lo = (v - hi.astype(F32)).astype(BF16)
    return (_dot(hi, gmat) + _dot(lo, gmat)) * (1.0 / HEAD_DIM)


def _split3(v):
    hi = v.astype(BF16)
    r1 = v - hi.astype(F32)
    mid = r1.astype(BF16)
    lo = (r1 - mid.astype(F32)).astype(BF16)
    return hi, mid, lo


def _row_tile(rows, cands=(1024, 512, 704, 256, 128, 64, 32, 16)):
    for c in cands:
        if rows % c == 0:
            return c
    return rows


def _dev_index(px, py, pc):
    return 4 * px + 2 * py + pc


def _when(cond):
    if cond is True:
        return lambda fn: fn()
    return pl.when(cond)


def _phases(grid):
    def phases():
        if not grid:
            return True, True, True
        lin = pl.program_id(0)
        for a in range(1, len(grid)):
            lin = lin * grid[a] + pl.program_id(a)
        total = math.prod(grid)
        return lin == 0, lin == total - 1, lin == total - 1
    return phases


class _Gather:
    def __init__(self, items):
        self.items = items
        self.args = [a for a, _ in items]
        n = len(items)
        self.out_shape = [jax.ShapeDtypeStruct((N_DEV,) + (a.shape if lay is None else a.shape[1:]), a.dtype)
                          for a, lay in items]
        self.scratch = [pltpu.SemaphoreType.DMA((n, 7)), pltpu.SemaphoreType.DMA((n, 7)),
                        pltpu.SemaphoreType.DMA((n,))]

    def _ctx(self, ins, outs, sems):
        send_sems, recv_sems, local_sems = sems
        x, y, c = lax.axis_index("x"), lax.axis_index("y"), lax.axis_index("c")
        chips = [(1 - x, y), (x, 1 - y), (1 - x, 1 - y)]

        def src(k):
            lay = self.items[k][1]
            return ins[k] if lay is None else ins[k].at[lay]

        def copy(k, s, idx, to, from_src=False):
            return pltpu.make_async_remote_copy(
                src_ref=src(k) if from_src else outs[k].at[idx], dst_ref=outs[k].at[idx],
                send_sem=send_sems.at[k, s], recv_sem=recv_sems.at[k, s],
                device_id=to, device_id_type=MESH)

        def local(k):
            return pltpu.make_async_copy(src(k), outs[k].at[_dev_index(x, y, c)], local_sems.at[k])

        return x, y, c, chips, copy, local

    def start(self, ins, outs, sems, cond):
        n = len(self.items)

        @_when(cond)
        def _():
            x, y, c, chips, copy, local = self._ctx(ins, outs, sems)
            me = _dev_index(x, y, c)
            for k in range(n):
                local(k).start()
                copy(k, 0, me, (x, y, 1 - c), from_src=True).start()
                for j, chip in enumerate(chips):
                    copy(k, 1 + j, me, (chip[0], chip[1], c), from_src=True).start()

    def forward(self, ins, outs, sems, cond):
        n = len(self.items)

        @_when(cond)
        def _():
            x, y, c, chips, copy, local = self._ctx(ins, outs, sems)
            for j, chip in enumerate(chips):
                idx = _dev_index(chip[0], chip[1], c)
                for k in range(n):
                    copy(k, 1 + j, idx, (x, y, c)).wait_recv()
                    copy(k, 4 + j, idx, (x, y, 1 - c)).start()

    def finish(self, ins, outs, sems, cond):
        n = len(self.items)

        @_when(cond)
        def _():
            x, y, c, chips, copy, local = self._ctx(ins, outs, sems)
            me = _dev_index(x, y, c)
            for k in range(n):
                copy(k, 0, _dev_index(x, y, 1 - c), (x, y, c)).wait_recv()
            for j, chip in enumerate(chips):
                idx = _dev_index(chip[0], chip[1], 1 - c)
                for k in range(n):
                    copy(k, 4 + j, idx, (x, y, c)).wait_recv()
            for k in range(n):
                for s in range(4):
                    copy(k, s, me, (x, y, c), from_src=True).wait_send()
                for j, chip in enumerate(chips):
                    copy(k, 4 + j, _dev_index(chip[0], chip[1], c), (x, y, c)).wait_send()
                local(k).wait()


_PEER_FLIPS = [(0, 0, 1), (1, 0, 0), (0, 1, 0), (1, 1, 0), (1, 0, 1), (0, 1, 1), (1, 1, 1)]


def _call(body, *, name, grid, in_specs, out_specs, out_shape, args, scratch=(), comm=None):
    n_hi, n_ho, n_hs = len(args), len(out_shape), len(scratch)
    c_args = list(comm.args) if comm else []
    c_out = list(comm.out_shape) if comm else []
    c_scr = list(comm.scratch) if comm else []
    phases = _phases(grid)

    def kern(*refs):
        cuts = [n_hi, len(c_args), n_ho, len(c_out), n_hs, len(c_scr)]
        parts, pos = [], 0
        for n in cuts:
            parts.append(refs[pos:pos + n])
            pos += n
        hi, ci, ho, co, hs, cs = parts
        if comm:
            first, mid, last = phases()
            comm.start(ci, co, cs, first)
            comm.forward(ci, co, cs, mid)
        body(*hi, *ho, *hs)
        if comm:
            comm.finish(ci, co, cs, last)

    sem = ("arbitrary",) * len(grid) if grid else None
    return pl.pallas_call(
        kern, name=name, grid=grid,
        in_specs=list(in_specs) + [ANY] * len(c_args),
        out_specs=list(out_specs) + [ANY] * len(c_out),
        out_shape=list(out_shape) + c_out,
        scratch_shapes=list(scratch) + c_scr,
        compiler_params=pltpu.CompilerParams(dimension_semantics=sem, vmem_limit_bytes=VMEM_LIMIT),
    )(*args, *c_args)


def _comm_only(name, comm):
    return _call(lambda: None, name=name, grid=(), in_specs=[], out_specs=[], out_shape=[], args=[], comm=comm)


HBM_SPEC = pl.BlockSpec(memory_space=pltpu.HBM)
SEM_SPEC = pl.BlockSpec(memory_space=pltpu.SEMAPHORE)
SIDE_EFFECT = pltpu.SideEffectType.DATAFLOW_SIDE_EFFECTING


def _exchange_peer(x, y, c, s):
    fx, fy, fc = _PEER_FLIPS[s]
    return x ^ fx, y ^ fy, c ^ fc


def _exchange_start(name, items):
    n = len(items)
    srcs = [pltpu.with_memory_space_constraint(a, pltpu.HBM) for a, _ in items]
    land_shapes = [(N_DEV,) + (a.shape if whole else a.shape[1:]) for a, whole in items]
    lands = [pltpu.with_memory_space_constraint(lax.empty(shp, a.dtype), pltpu.HBM)
             for shp, (a, _) in zip(land_shapes, items)]

    n_sem = 7 * n

    def body(*refs):
        src_refs, land_refs = refs[:n], refs[n:2 * n]
        send_sems = refs[2 * n:2 * n + n_sem]
        recv_sems = refs[2 * n + n_sem:2 * n + 2 * n_sem]
        token = refs[-1]
        x, y, c = lax.axis_index("x"), lax.axis_index("y"), lax.axis_index("c")
        me = _dev_index(x, y, c)
        for s in range(7):
            px, py, pc = _exchange_peer(x, y, c, s)
            for k in range(n):
                src = src_refs[k] if items[k][1] else src_refs[k].at[_dev_index(px, py, pc)]
                pltpu.make_async_remote_copy(
                    src_ref=src, dst_ref=land_refs[k].at[me],
                    send_sem=send_sems[7 * k + s], recv_sem=recv_sems[7 * k + s],
                    device_id=(px, py, pc), device_id_type=MESH).start()
        token[...] = jnp.zeros(token.shape, token.dtype)

    outs = pl.pallas_call(
        body, name=name,
        out_shape=(*[pltpu.SemaphoreType.DMA(())] * (2 * n_sem),
                   *[pltpu.HBM(a.shape, a.dtype) for a in srcs],
                   *[pltpu.HBM(shp, a.dtype) for shp, a in zip(land_shapes, srcs)],
                   jax.ShapeDtypeStruct((8, 128), F32)),
        in_specs=[HBM_SPEC] * (2 * n),
        out_specs=(*[SEM_SPEC] * (2 * n_sem), *[HBM_SPEC] * (2 * n), pl.BlockSpec(memory_space=pltpu.VMEM)),
        input_output_aliases={i: 2 * n_sem + i for i in range(2 * n)},
        compiler_params=pltpu.CompilerParams(has_side_effects=SIDE_EFFECT),
    )(*srcs, *lands)
    base = 2 * n_sem
    return (list(outs[:n_sem]), list(outs[n_sem:base]), list(outs[base:base + n]),
            list(outs[base + n:base + 2 * n]), outs[-1])


def _exchange_wait(name, items, send_sems, recv_sems, srcs, lands, after):
    n = len(items)

    n_sem = 7 * n

    def body(*refs):
        src_refs, land_refs = refs[:n], refs[n:2 * n]
        send_refs = refs[2 * n:2 * n + n_sem]
        recv_refs = refs[2 * n + n_sem:2 * n + 2 * n_sem]
        x, y, c = lax.axis_index("x"), lax.axis_index("y"), lax.axis_index("c")
        for s in range(7):
            for k in range(n):
                copy = pltpu.make_async_remote_copy(
                    src_ref=src_refs[k] if items[k][1] else src_refs[k].at[0], dst_ref=land_refs[k].at[0],
                    send_sem=send_refs[7 * k + s], recv_sem=recv_refs[7 * k + s],
                    device_id=(x, y, c), device_id_type=MESH)
                copy.wait_send()
                copy.wait_recv()

    outs = pl.pallas_call(
        body, name=name,
        out_shape=(*[pltpu.HBM(a.shape, a.dtype) for a in srcs], *[pltpu.HBM(a.shape, a.dtype) for a in lands]),
        in_specs=[HBM_SPEC] * (2 * n) + [SEM_SPEC] * (2 * n_sem) + [ANY],
        out_specs=tuple([HBM_SPEC] * (2 * n)),
        input_output_aliases={i: i for i in range(2 * n)},
        compiler_params=pltpu.CompilerParams(has_side_effects=SIDE_EFFECT),
    )(*srcs, *lands, *send_sems, *recv_sems, after)
    return list(outs[:n]), list(outs[n:])


def _cast_bf16(x, name):
    shape = x.shape
    x2 = x.reshape(-1, shape[-1])
    rows, cols = x2.shape
    tr = _row_tile(rows)

    def body(x_ref, o_ref):
        o_ref[...] = x_ref[...].astype(BF16)

    blk = pl.BlockSpec((tr, cols), lambda i: (i, 0))
    out, = _call(body, name=name, grid=(rows // tr,), in_specs=[blk], out_specs=[blk],
                 out_shape=[jax.ShapeDtypeStruct((rows, cols), BF16)], args=[x2])
    return out.reshape(shape)


def _norm_cast(x, g3, l, tm):
    s = x.shape[0]

    def body(x_ref, g_ref, o_ref):
        v = x_ref[...]
        o_ref[...] = (v * _rstd(v) * g_ref[...]).astype(BF16)

    row = pl.BlockSpec((tm, D_MODEL), lambda i: (i, 0))
    out, = _call(body, name="norm_cast", grid=(s // tm,),
                 in_specs=[row, pl.BlockSpec((None, 1, D_MODEL), lambda i: (l, 0, 0))], out_specs=[row],
                 out_shape=[jax.ShapeDtypeStruct((s, D_MODEL), BF16)], args=[x, g3])
    return out


def _in_proj_qkv(h, win, s, tq):
    def body(a_ref, b_ref, o_ref):
        i = pl.program_id(0)

        @pl.when(i == 0)
        def _():
            o_ref[...] = jnp.zeros(o_ref.shape, BF16)

        @pl.when(i > 0)
        def _():
            w = jnp.concatenate([b_ref[j] for j in range(4)], axis=1)
            o_ref[...] = _dot(a_ref[...], w).astype(BF16)

    out, = _call(
        body, name="in_proj_qkv", grid=(s // tq + 1,),
        in_specs=[pl.BlockSpec((tq, D_MODEL), lambda i: (jnp.maximum(i - 1, 0), 0)),
                  pl.BlockSpec((4, D_MODEL, PROJ_SHARD), lambda i: (1, 0, 0))],
        out_specs=[pl.BlockSpec((tq, 4 * PROJ_SHARD), lambda i: (i, 0))],
        out_shape=[jax.ShapeDtypeStruct((s + tq, PROJ_WIDTH // 2), BF16)], args=[h, win])
    return out


def _in_proj_conv(h, win, wc, g3, l, s, tq, comm=None):
    def body(a_ref, b_ref, wc_ref, g_ref, pc_ref, y_ref, carry_ref):
        i = pl.program_id(0)
        w = jnp.concatenate([b_ref[j] for j in range(4)], axis=1)
        pc_ref[...] = _dot(a_ref[...], w)
        gmat = _group_matrix()
        for j in range(CONV_WIDTH // 128):
            c0, c1, c2 = 128 * j, CONV_WIDTH + 128 * j, 2 * CONV_WIDTH + 128 * j
            hc = pc_ref[:, c0:c0 + 128]
            bg = pc_ref[:, c1:c1 + 128]
            cg = pc_ref[:, c2:c2 + 128]
            u_prev = jnp.where(i > 0, carry_ref[:, c0:c0 + 128], 0.0)
            u = cg * hc
            carry_ref[:, c0:c0 + 128] = u[tq - 8:tq, :]
            full = jnp.concatenate([u_prev, u], axis=0)
            u1 = pltpu.roll(full, 1, 0)[8:]
            u2 = pltpu.roll(full, 2, 0)[8:]
            out = (u2 * wc_ref[0:1, c0:c0 + 128] + u1 * wc_ref[1:2, c0:c0 + 128]
                   + u * wc_ref[2:3, c0:c0 + 128])
            yc = bg * out
            r = lax.rsqrt(_group_mean(yc * yc, gmat) + EPS)
            y_ref[:, c0:c0 + 128] = (yc * r * g_ref[:, c0:c0 + 128]).astype(BF16)

    return _call(
        body, name="in_proj_conv", grid=(s // tq,),
        in_specs=[pl.BlockSpec((tq, D_MODEL), lambda i: (i, 0)),
                  pl.BlockSpec((4, D_MODEL, PROJ_SHARD), lambda i: (0, 0, 0)),
                  pl.BlockSpec((None, 8, CONV_WIDTH), lambda i: (l, 0, 0)),
                  pl.BlockSpec((None, 1, CONV_WIDTH), lambda i: (l, 0, 0))],
        out_specs=[pl.BlockSpec((tq, 3 * CONV_WIDTH), lambda i: (i, 0)),
                   pl.BlockSpec((tq, CONV_WIDTH), lambda i: (i, 0))],
        out_shape=[jax.ShapeDtypeStruct((s, 3 * CONV_WIDTH), F32), jax.ShapeDtypeStruct((s, CONV_WIDTH), BF16)],
        scratch=[pltpu.VMEM((8, CONV_WIDTH), F32)], args=[h, win, wc, g3], comm=comm)


def _toeplitz_source():
    r_i = lax.broadcasted_iota(jnp.int32, (REL_PAD, TOEP), 0)
    m_i = lax.broadcasted_iota(jnp.int32, (REL_PAD, TOEP), 1)
    idx = jnp.clip((K_BAND - 1) - m_i, -REL_CLIP, REL_CLIP) + REL_CLIP
    return jnp.where(r_i == idx, 1.0, 0.0).astype(BF16)


def _bias_build(rbp, comm=None):
    n_layers = rbp.shape[0]

    def body(rb_ref, o_ref, t_ref):
        pmat = _toeplitz_source()
        hi, mid, lo = _split3(rb_ref[...])
        t_ref[...] = _dot(hi, pmat) + _dot(mid, pmat) + _dot(lo, pmat)
        shift = (CHUNK - 1) - lax.broadcasted_iota(jnp.int32, (CHUNK, TOEP), 0)
        kchunk = lax.broadcasted_iota(jnp.int32, (CHUNK, K_BAND), 1) >> 6
        for h in range(N_HEADS):
            b = jnp.broadcast_to(t_ref[pl.ds(h, 1), :], (CHUNK, TOEP))
            for bit in range(6):
                rolled = pltpu.roll(b, TOEP - (1 << bit), 1)
                b = jnp.where(((shift >> bit) & 1) == 1, rolled, b)
            for cq in range(Q_BLOCK // CHUNK):
                off = CHUNK * (Q_BLOCK // CHUNK - 1 - cq)
                band = pltpu.roll(b, TOEP - off, 1) if off else b
                dchunk = kchunk - cq
                in_band = jnp.where(dchunk >= 0, jnp.where(dchunk <= N_LEFT_CHUNKS, 1, 0), 0) == 1
                o_ref[h, CHUNK * cq:CHUNK * (cq + 1), :] = jnp.where(in_band, band[:, :K_BAND], NEG_INF)

    return _call(
        body, name="bias_build", grid=(n_layers,),
        in_specs=[pl.BlockSpec((None, N_HEADS, REL_PAD), lambda l: (l, 0, 0))],
        out_specs=[pl.BlockSpec((None, N_HEADS, Q_BLOCK, K_BAND), lambda l: (l, 0, 0, 0))],
        out_shape=[jax.ShapeDtypeStruct((n_layers, N_HEADS, Q_BLOCK, K_BAND), F32)],
        scratch=[pltpu.VMEM((N_HEADS, TOEP), F32)], args=[rbp], comm=comm)


def _bias_bwd(ds_sum):
    n_layers = ds_sum.shape[0]

    def body(ds_ref, o_ref, t_ref):
        pmat = _toeplitz_source()
        shift = (CHUNK - 1) - lax.broadcasted_iota(jnp.int32, (CHUNK, TOEP), 0)
        for h in range(N_HEADS):
            d = None
            for cq in range(Q_BLOCK // CHUNK):
                off = CHUNK * (Q_BLOCK // CHUNK - 1 - cq)
                part = jnp.concatenate([ds_ref[h, CHUNK * cq:CHUNK * (cq + 1), :],
                                        jnp.zeros((CHUNK, TOEP - K_BAND), F32)], axis=1)
                part = pltpu.roll(part, off, 1) if off else part
                d = part if d is None else d + part
            for bit in range(6):
                rolled = pltpu.roll(d, 1 << bit, 1)
                d = jnp.where(((shift >> bit) & 1) == 1, rolled, d)
            t_ref[pl.ds(h, 1), :] = jnp.sum(d, axis=0, keepdims=True)
        hi, mid, lo = _split3(t_ref[...])
        o_ref[...] = _dot_nt(hi, pmat) + _dot_nt(mid, pmat) + _dot_nt(lo, pmat)

    out, = _call(
        body, name="bias_bwd", grid=(n_layers,),
        in_specs=[pl.BlockSpec((None, N_HEADS, Q_BLOCK, K_BAND), lambda l: (l, 0, 0, 0))],
        out_specs=[pl.BlockSpec((None, N_HEADS, REL_PAD), lambda l: (l, 0, 0))],
        out_shape=[jax.ShapeDtypeStruct((n_layers, N_HEADS, REL_PAD), F32)],
        scratch=[pltpu.VMEM((N_HEADS, TOEP), F32)], args=[ds_sum])
    return out


def _attn_fwd(qkvp, biasm, g3, l, s, pad, comm=None):
    nb = s // Q_BLOCK
    qb0 = pad // Q_BLOCK
    scale = HEAD_DIM ** -0.5
    wide = 128 * ATTN_PAIRS

    def body(q_ref, k_ref, v_ref, b_ref, g_ref, o_ref, lse_ref, yn_ref):
        blk = pl.program_id(1)
        koff = pl.multiple_of(blk * Q_BLOCK + (pad - LEFT), Q_BLOCK)
        lane = lax.broadcasted_iota(jnp.int32, (1, 128), 1)
        kpos = lax.broadcasted_iota(jnp.int32, (1, K_BAND), 1) + (blk * Q_BLOCK - LEFT)
        kmask = jnp.where(kpos >= 0, 0.0, NEG_INF)
        gmat = _group_matrix()
        for pr in range(ATTN_PAIRS):
            ls = slice(128 * pr, 128 * (pr + 1))
            q = q_ref[:, ls]
            kb = k_ref[pl.ds(koff, K_BAND), ls]
            vb = v_ref[pl.ds(koff, K_BAND), ls]
            outs, lses = [], []
            for hh in range(2):
                in_head = (lane >> 6) == hh
                qm = jnp.where(in_head, q, jnp.zeros_like(q)) * jnp.asarray(scale, BF16)
                sc = _dot_nt(qm, kb) + b_ref[2 * pr + hh] + kmask
                m = jnp.max(sc, axis=1, keepdims=True)
                e = jnp.exp(sc - m)
                den = jnp.sum(e, axis=1, keepdims=True)
                outs.append(_dot(e.astype(BF16), vb) * (1.0 / den))
                lses.append(m + jnp.log(den))
            first = lane < HEAD_DIM
            o = jnp.where(first, outs[0], outs[1])
            o_ref[:, ls] = o
            lse_ref[:, ls] = jnp.where(first, lses[0], lses[1])
            r = lax.rsqrt(_group_mean(o * o, gmat) + EPS)
            yn_ref[:, ls] = (o * r * g_ref[:, ls]).astype(BF16)

    blk_out = pl.BlockSpec((Q_BLOCK, wide), lambda p, b: (b, p))
    n_grp = ATTN_WIDTH // wide
    return _call(
        body, name="attn_fwd", grid=(n_grp, nb),
        in_specs=[pl.BlockSpec((Q_BLOCK, wide), lambda p, b: (qb0 + b, p)),
                  pl.BlockSpec((s + pad, wide), lambda p, b: (0, n_grp + p)),
                  pl.BlockSpec((s + pad, wide), lambda p, b: (0, 2 * n_grp + p)),
                  pl.BlockSpec((None, 2 * ATTN_PAIRS, Q_BLOCK, K_BAND), lambda p, b: (l, p, 0, 0)),
                  pl.BlockSpec((None, 1, wide), lambda p, b: (l, 0, p))],
        out_specs=[blk_out, blk_out, blk_out],
        out_shape=[jax.ShapeDtypeStruct((s, ATTN_WIDTH), F32),
                   jax.ShapeDtypeStruct((s, ATTN_WIDTH), F32),
                   jax.ShapeDtypeStruct((s, ATTN_WIDTH), BF16)],
        args=[qkvp, qkvp, qkvp, biasm, g3], comm=comm)


def _out_proj_fwd(ync, yna, wout, x, g_post3, g_next3, l, s, tm):
    half = D_MODEL // 2

    def body(a1_ref, a2_ref, w_ref, x_ref, gp_ref, gn_ref, z_ref, xm_ref, h_ref):
        for rs in _row_subtiles(tm, SUB_ROWS):
            z = _dot(a1_ref[rs, :], w_ref[0:half, :]) + _dot(a2_ref[rs, :], w_ref[half:D_MODEL, :])
            z_ref[rs, :] = z
            xm = x_ref[rs, :] + z * _rstd(z) * gp_ref[...]
            xm_ref[rs, :] = xm
            h_ref[rs, :] = (xm * _rstd(xm) * gn_ref[...]).astype(BF16)

    row = pl.BlockSpec((tm, D_MODEL), lambda i: (i, 0))
    gain = pl.BlockSpec((None, 1, D_MODEL), lambda i: (l, 0, 0))
    return _call(
        body, name="out_proj_fwd", grid=(s // tm,),
        in_specs=[pl.BlockSpec((tm, half), lambda i: (i, 0)), pl.BlockSpec((tm, half), lambda i: (i, 0)),
                  pl.BlockSpec((D_MODEL, D_MODEL), lambda i: (0, 0)), row, gain, gain],
        out_specs=[row, row, row],
        out_shape=[jax.ShapeDtypeStruct((s, D_MODEL), F32), jax.ShapeDtypeStruct((s, D_MODEL), F32),
                   jax.ShapeDtypeStruct((s, D_MODEL), BF16)],
        args=[ync, yna, wout, x, g_post3, g_next3])


def _ffn_in_fwd(h2, wfin4, s, tm, comm=None):
    def body(h_ref, wg_ref, wu_ref, gu_ref, act_ref):
        h = h_ref[...]
        for cs in _COL_SUBTILES:
            gate = _dot_nt(h, wg_ref[cs, :])
            up = _dot_nt(h, wu_ref[cs, :])
            gu_ref[0, :, cs] = gate.astype(BF16)
            gu_ref[1, :, cs] = up.astype(BF16)
            act_ref[:, cs] = (gate * (1.0 / (1.0 + jnp.exp(-gate))) * up).astype(BF16)

    return _call(
        body, name="ffn_in_fwd", grid=(2, s // tm),
        in_specs=[pl.BlockSpec((tm, D_MODEL), lambda b, i: (i, 0)),
                  pl.BlockSpec((None, FF_PAIR, D_MODEL), lambda b, i: (b, 0, 0)),
                  pl.BlockSpec((None, FF_PAIR, D_MODEL), lambda b, i: (2 + b, 0, 0))],
        out_specs=[pl.BlockSpec((2, tm, FF_PAIR), lambda b, i: (0, i, b)),
                   pl.BlockSpec((tm, FF_PAIR), lambda b, i: (i, b))],
        out_shape=[jax.ShapeDtypeStruct((2, s, D_FF), BF16), jax.ShapeDtypeStruct((s, D_FF), BF16)],
        args=[h2, wfin4, wfin4], comm=comm)


def _ffn_out_fwd(act, wfo, xm, g_post3, g_next3, l, l_next, s, tm, comm=None):
    def body(a_ref, w_ref, x_ref, gp_ref, gn_ref, f_ref, xo_ref, h_ref):
        for rs in _row_subtiles(tm, SUB_ROWS):
            f = _dot(a_ref[rs, :], w_ref[...])
            f_ref[rs, :] = f
            xo = x_ref[rs, :] + f * _rstd(f) * gp_ref[...]
            xo_ref[rs, :] = xo
            h_ref[rs, :] = (xo * _rstd(xo) * gn_ref[...]).astype(BF16)

    row = pl.BlockSpec((tm, D_MODEL), lambda i: (i, 0))
    return _call(
        body, name="ffn_out_fwd", grid=(s // tm,),
        in_specs=[pl.BlockSpec((tm, D_FF), lambda i: (i, 0)),
                  pl.BlockSpec((D_FF, D_MODEL), lambda i: (0, 0), pipeline_mode=pl.Buffered(1)), row,
                  pl.BlockSpec((None, 1, D_MODEL), lambda i: (l, 0, 0)),
                  pl.BlockSpec((None, 1, D_MODEL), lambda i: (l_next, 0, 0))],
        out_specs=[row, row, row],
        out_shape=[jax.ShapeDtypeStruct((s, D_MODEL), F32), jax.ShapeDtypeStruct((s, D_MODEL), F32),
                   jax.ShapeDtypeStruct((s, D_MODEL), BF16)],
        args=[act, wfo, xm, g_post3, g_next3], comm=comm)


def _ffn_out_loss(act, wfo, xm, g_post3, target, l, s, tm):
    def body(a_ref, w_ref, x_ref, gp_ref, t_ref, dx_ref, sq_ref, df_ref, dg_ref):
        _zero_first((sq_ref, dg_ref), pl.program_id(0) == 0)
        for rs in _row_subtiles(tm, SUB_ROWS):
            f = _dot(a_ref[rs, :], w_ref[...])
            gain = gp_ref[...]
            err = x_ref[rs, :] + f * _rstd(f) * gain - t_ref[rs, :]
            dx = err * (1.0 / D_MODEL)
            dx_ref[rs, :] = dx
            df, dyn = _norm_bwd_rows(f, gain, dx)
            df_ref[rs, :] = df.astype(BF16)
            _add_cols(dg_ref, dyn)
            cs = jnp.sum(err * err, axis=0, keepdims=True)
            part = cs[:, 0:128]
            for k in range(1, D_MODEL // 128):
                part = part + cs[:, 128 * k:128 * (k + 1)]
            sq_ref[0:1, :] += part

    row = pl.BlockSpec((tm, D_MODEL), lambda i: (i, 0))
    return _call(
        body, name="ffn_out_loss", grid=(s // tm,),
        in_specs=[pl.BlockSpec((tm, D_FF), lambda i: (i, 0)),
                  pl.BlockSpec((D_FF, D_MODEL), lambda i: (0, 0), pipeline_mode=pl.Buffered(1)), row,
                  pl.BlockSpec((None, 1, D_MODEL), lambda i: (l, 0, 0)), row],
        out_specs=[row, pl.BlockSpec((8, 128), lambda i: (0, 0)), row, pl.BlockSpec((8, D_MODEL), lambda i: (0, 0))],
        out_shape=[jax.ShapeDtypeStruct((s, D_MODEL), F32), jax.ShapeDtypeStruct((8, 128), F32),
                   jax.ShapeDtypeStruct((s, D_MODEL), BF16), jax.ShapeDtypeStruct((8, D_MODEL), F32)],
        args=[act, wfo, xm, g_post3, target])


def _norm_bwd_rows(v, g, dy):
    r = _rstd(v)
    vn = v * r
    gd = dy * g
    dv = r * (gd - vn * jnp.mean(vn * gd, axis=-1, keepdims=True))
    return dv, dy * vn


def _zero_first(refs, first):
    @pl.when(first)
    def _():
        for ref in refs:
            ref[...] = jnp.zeros(ref.shape, F32)


def _add_cols(ref, val):
    ref[0:1, :] += jnp.sum(val, axis=0, keepdims=True)


def _accum_cols(ref, val, first):
    _zero_first((ref,), first)
    _add_cols(ref, val)


def _row_subtiles(rows, sub):
    sub = min(sub, rows)
    return [slice(r, r + sub) for r in range(0, rows, sub)]


def _ffn_out_bwd(df, wfo, gu, act, s, tm, comm=None):
    nm = s // tm

    def body(df_ref, w_ref, gu_ref, act_ref, dgu_ref, dw_ref, acc_ref):
        i = pl.program_id(1)
        df = df_ref[...]
        _zero_first((acc_ref,), i == 0)
        acc_ref[...] += _dot_tn(act_ref[...], df)

        @pl.when(i == nm - 1)
        def _():
            dw_ref[...] = acc_ref[...].astype(BF16)

        for cs in _COL_SUBTILES:
            da = _dot_nt(df, w_ref[cs, :])
            g = gu_ref[0, :, cs].astype(F32)
            u = gu_ref[1, :, cs].astype(F32)
            sg = 1.0 / (1.0 + jnp.exp(-g))
            dgu_ref[0, :, cs] = (da * u * (sg * (1.0 + g * (1.0 - sg)))).astype(BF16)
            dgu_ref[1, :, cs] = (da * (g * sg)).astype(BF16)

    blk = pl.BlockSpec((2, tm, FF_PAIR), lambda b, i: (0, i, b))
    wblk = pl.BlockSpec((FF_PAIR, D_MODEL), lambda b, i: (b, 0))
    return _call(
        body, name="ffn_out_bwd", grid=(2, nm),
        in_specs=[pl.BlockSpec((tm, D_MODEL), lambda b, i: (i, 0)), wblk, blk,
                  pl.BlockSpec((tm, FF_PAIR), lambda b, i: (i, b))],
        out_specs=[blk, wblk],
        out_shape=[jax.ShapeDtypeStruct((2, s, D_FF), BF16), jax.ShapeDtypeStruct((D_FF, D_MODEL), BF16)],
        scratch=[pltpu.VMEM((FF_PAIR, D_MODEL), F32)],
        args=[df, wfo, gu, act], comm=comm)


def _dw_ffn_in(h2, dgu, s):
    def body(a_ref, b_ref, o_ref):
        o_ref[...] = _dot_tn(b_ref[...], a_ref[...]).astype(BF16)

    out, = _call(
        body, name="dw_ffn_in", grid=(4,),
        in_specs=[pl.BlockSpec((s, D_MODEL), lambda n: (0, 0), pipeline_mode=pl.Buffered(1)),
                  pl.BlockSpec((None, s, FF_PAIR), lambda n: (n // 2, 0, n % 2))],
        out_specs=[pl.BlockSpec((None, FF_PAIR, D_MODEL), lambda n: (n, 0, 0))],
        out_shape=[jax.ShapeDtypeStruct((4, FF_PAIR, D_MODEL), BF16)], args=[h2, dgu])
    return out


def _ffn_in_bwd(dgu, wfin, xm, g_pre3, dres, z, g_post3, l, s, tm, comm=None):
    def body(d_ref, w_ref, xm_ref, gp_ref, dres_ref, z_ref, gq_ref, dxm_ref, dz_ref, dgp_ref, dgq_ref):
        _zero_first((dgp_ref, dgq_ref), pl.program_id(0) == 0)
        for rs in _row_subtiles(tm, SUB_ROWS):
            dh = _dot(d_ref[0, rs, :], w_ref[0:D_FF, :]) + _dot(d_ref[1, rs, :], w_ref[D_FF:2 * D_FF, :])
            dx, dyn = _norm_bwd_rows(xm_ref[rs, :], gp_ref[...], dh)
            dxm = dres_ref[rs, :] + dx
            dxm_ref[rs, :] = dxm
            _add_cols(dgp_ref, dyn)
            dz, dyn2 = _norm_bwd_rows(z_ref[rs, :], gq_ref[...], dxm)
            dz_ref[rs, :] = dz.astype(BF16)
            _add_cols(dgq_ref, dyn2)

    row = pl.BlockSpec((tm, D_MODEL), lambda i: (i, 0))
    gain = pl.BlockSpec((None, 1, D_MODEL), lambda i: (l, 0, 0))
    dgs = pl.BlockSpec((8, D_MODEL), lambda i: (0, 0))
    return _call(
        body, name="ffn_in_bwd", grid=(s // tm,),
        in_specs=[pl.BlockSpec((2, tm, D_FF), lambda i: (0, i, 0)),
                  pl.BlockSpec((2 * D_FF, D_MODEL), lambda i: (0, 0), pipeline_mode=pl.Buffered(1)),
                  row, gain, row, row, gain],
        out_specs=[row, row, dgs, dgs],
        out_shape=[jax.ShapeDtypeStruct((s, D_MODEL), F32), jax.ShapeDtypeStruct((s, D_MODEL), BF16),
                   jax.ShapeDtypeStruct((8, D_MODEL), F32), jax.ShapeDtypeStruct((8, D_MODEL), F32)],
        args=[dgu, wfin, xm, g_pre3, dres, z, g_post3], comm=comm)


def _out_proj_bwd(dz, wout, o, g3, ync, yna, l, s, tm):
    nm = s // tm
    half = D_MODEL // 2

    def body(dz_ref, w_ref, o_ref, g_ref, a1_ref, a2_ref, dyc_ref, do_ref, dg_ref, dw_ref, acc_ref):
        i = pl.program_id(0)
        gmat = _group_matrix()
        _zero_first((dg_ref, acc_ref), i == 0)
        dzv = dz_ref[...]
        acc_ref[0:half, :] += _dot_tn(a1_ref[...], dzv)
        acc_ref[half:D_MODEL, :] += _dot_tn(a2_ref[...], dzv)

        @pl.when(i == nm - 1)
        def _():
            dw_ref[...] = acc_ref[...].astype(BF16)

        for rs in _row_subtiles(tm, SUB_ROWS):
            dy = _dot_nt(dz_ref[rs, :], w_ref[...])
            dyc_ref[rs, :] = dy[:, 0:CONV_WIDTH]
            for j in range(ATTN_WIDTH // 128):
                c0 = 128 * j
                ov = o_ref[rs, c0:c0 + 128]
                dyn = dy[:, CONV_WIDTH + c0:CONV_WIDTH + c0 + 128]
                r = lax.rsqrt(_group_mean(ov * ov, gmat) + EPS)
                on = ov * r
                gd = dyn * g_ref[:, c0:c0 + 128]
                do_ref[rs, c0:c0 + 128] = r * (gd - on * _group_mean(on * gd, gmat))
                dg_ref[0:1, c0:c0 + 128] += jnp.sum(dyn * on, axis=0, keepdims=True)

    halfrow = pl.BlockSpec((tm, ATTN_WIDTH), lambda i: (i, 0))
    return _call(
        body, name="out_proj_bwd", grid=(nm,),
        in_specs=[pl.BlockSpec((tm, D_MODEL), lambda i: (i, 0)),
                  pl.BlockSpec((D_MODEL, D_MODEL), lambda i: (0, 0)), halfrow,
                  pl.BlockSpec((None, 1, ATTN_WIDTH), lambda i: (l, 0, 0)), halfrow, halfrow],
        out_specs=[halfrow, halfrow, pl.BlockSpec((8, ATTN_WIDTH), lambda i: (0, 0)),
                   pl.BlockSpec((D_MODEL, D_MODEL), lambda i: (0, 0))],
        out_shape=[jax.ShapeDtypeStruct((s, CONV_WIDTH), F32), jax.ShapeDtypeStruct((s, ATTN_WIDTH), F32),
                   jax.ShapeDtypeStruct((8, ATTN_WIDTH), F32), jax.ShapeDtypeStruct((D_MODEL, D_MODEL), BF16)],
        scratch=[pltpu.VMEM((D_MODEL, D_MODEL), F32)],
        args=[dz, wout, o, g3, ync, yna])


def _conv_bwd(pc, dyc, wc, g3, dq, dk, dv, l, s, tr):
    hb = tr // 8
    nt = s // tr
    ext = tr + 16
    last_hb = s // 8 - 1

    def body(pc_ref, prev_ref, next_ref, dy_ref, dyn_ref, wc_ref, g_ref, dq_ref, dk_ref, dv_ref,
             dpc_ref, dw_ref, dg_ref):
        i = pl.program_id(0)
        for part, ref in enumerate((dq_ref, dk_ref, dv_ref)):
            c = 3 * CONV_WIDTH + ATTN_WIDTH * part
            dpc_ref[:, c:c + ATTN_WIDTH] = ref[...]
        gmat = _group_matrix()
        row = lax.broadcasted_iota(jnp.int32, (ext, 128), 0) + (i * tr - 8)
        inside = jnp.where(row >= 0, jnp.where(row < s, 1, 0), 0) == 1

        @pl.when(i == 0)
        def _():
            dw_ref[...] = jnp.zeros(dw_ref.shape, F32)
            dg_ref[...] = jnp.zeros(dg_ref.shape, F32)

        def extend(ref_prev, ref_mid, ref_next, c):
            parts = [ref_prev[:, c:c + 128] if ref_prev is not None else jnp.zeros((8, 128), F32),
                     ref_mid[:, c:c + 128], ref_next[:, c:c + 128]]
            return jnp.concatenate(parts, axis=0)

        for j in range(CONV_WIDTH // 128):
            c0, c1, c2 = 128 * j, CONV_WIDTH + 128 * j, 2 * CONV_WIDTH + 128 * j
            hc = extend(prev_ref, pc_ref, next_ref, c0)
            bg = extend(prev_ref, pc_ref, next_ref, c1)
            cg = extend(prev_ref, pc_ref, next_ref, c2)
            dyn = extend(None, dy_ref, dyn_ref, c0)
            w0, w1, w2 = (wc_ref[0:1, c0:c0 + 128], wc_ref[1:2, c0:c0 + 128], wc_ref[2:3, c0:c0 + 128])
            gain = g_ref[:, c0:c0 + 128]
            u = jnp.where(inside, cg * hc, 0.0)
            u1 = pltpu.roll(u, 1, 0)
            u2 = pltpu.roll(u, 2, 0)
            out = u2 * w0 + u1 * w1 + u * w2
            yc = bg * out
            r = lax.rsqrt(_group_mean(yc * yc, gmat) + EPS)
            ycn = yc * r
            gd = dyn * gain
            dyc = r * (gd - ycn * _group_mean(ycn * gd, gmat))
            dout = jnp.where(inside, dyc * bg, 0.0)
            du = dout * w2 + pltpu.roll(dout, ext - 1, 0) * w1 + pltpu.roll(dout, ext - 2, 0) * w0
            sl = slice(8, 8 + tr)
            dpc_ref[:, c0:c0 + 128] = (du[sl] * cg[sl]).astype(BF16)
            dpc_ref[:, c1:c1 + 128] = (dyc[sl] * out[sl]).astype(BF16)
            dpc_ref[:, c2:c2 + 128] = (du[sl] * hc[sl]).astype(BF16)
            dw_ref[0:1, c0:c0 + 128] += jnp.sum(dout[sl] * u2[sl], axis=0, keepdims=True)
            dw_ref[1:2, c0:c0 + 128] += jnp.sum(dout[sl] * u1[sl], axis=0, keepdims=True)
            dw_ref[2:3, c0:c0 + 128] += jnp.sum(dout[sl] * u[sl], axis=0, keepdims=True)
            dg_ref[0:1, c0:c0 + 128] += jnp.sum(dyn[sl] * ycn[sl], axis=0, keepdims=True)

    wide = 3 * CONV_WIDTH
    return _call(
        body, name="conv_bwd", grid=(nt,),
        in_specs=[pl.BlockSpec((tr, wide), lambda i: (i, 0)),
                  pl.BlockSpec((8, wide), lambda i: (jnp.maximum(i * hb - 1, 0), 0)),
                  pl.BlockSpec((8, wide), lambda i: (jnp.minimum((i + 1) * hb, last_hb), 0)),
                  pl.BlockSpec((tr, CONV_WIDTH), lambda i: (i, 0)),
                  pl.BlockSpec((8, CONV_WIDTH), lambda i: (jnp.minimum((i + 1) * hb, last_hb), 0)),
                  pl.BlockSpec((None, 8, CONV_WIDTH), lambda i: (l, 0, 0)),
                  pl.BlockSpec((None, 1, CONV_WIDTH), lambda i: (l, 0, 0)),
                  pl.BlockSpec((tr, ATTN_WIDTH), lambda i: (i, 0)),
                  pl.BlockSpec((tr, ATTN_WIDTH), lambda i: (i, 0)),
                  pl.BlockSpec((tr, ATTN_WIDTH), lambda i: (i, 0))],
        out_specs=[pl.BlockSpec((tr, PROJ_WIDTH), lambda i: (i, 0)),
                   pl.BlockSpec((8, CONV_WIDTH), lambda i: (0, 0)),
                   pl.BlockSpec((8, CONV_WIDTH), lambda i: (0, 0))],
        out_shape=[jax.ShapeDtypeStruct((s, PROJ_WIDTH), BF16), jax.ShapeDtypeStruct((8, CONV_WIDTH), F32),
                   jax.ShapeDtypeStruct((8, CONV_WIDTH), F32)],
        args=[pc, pc, pc, dyc, dyc, wc, g3, dq, dk, dv])


def _attn_bwd(qkvp, biasm, o, lse, do, l, s, pad, comm=None):
    nb = s // Q_BLOCK
    qb0 = pad // Q_BLOCK
    scale = HEAD_DIM ** -0.5
    wide = 128 * ATTN_PAIRS

    def body(q_ref, k_ref, v_ref, b_ref, o_ref, lse_ref, do_ref,
             dq_ref, dk_ref, dv_ref, ds_ref, dk_acc, dv_acc):
        blk = pl.program_id(1)

        @pl.when(blk == 0)
        def _():
            dk_acc[...] = jnp.zeros(dk_acc.shape, F32)
            dv_acc[...] = jnp.zeros(dv_acc.shape, F32)
            ds_ref[...] = jnp.zeros(ds_ref.shape, F32)

        koff = pl.multiple_of(blk * Q_BLOCK + (pad - LEFT), Q_BLOCK)
        lane = lax.broadcasted_iota(jnp.int32, (1, 128), 1)
        kpos = lax.broadcasted_iota(jnp.int32, (1, K_BAND), 1) + (blk * Q_BLOCK - LEFT)
        kmask = jnp.where(kpos >= 0, 0.0, NEG_INF)
        for pr in range(ATTN_PAIRS):
            ls = slice(128 * pr, 128 * (pr + 1))
            q = q_ref[:, ls]
            kb = k_ref[pl.ds(koff, K_BAND), ls]
            vb = v_ref[pl.ds(koff, K_BAND), ls]
            dov = do_ref[:, ls]
            lse_v = lse_ref[:, ls]
            prod = dov * o_ref[:, ls]
            dq_parts = []
            dk_new = jnp.zeros((K_BAND, 128), F32)
            dv_new = jnp.zeros((K_BAND, 128), F32)
            for hh in range(2):
                in_head = (lane >> 6) == hh
                qm = jnp.where(in_head, q, jnp.zeros_like(q)) * jnp.asarray(scale, BF16)
                dom = jnp.where(in_head, dov, 0.0).astype(BF16)
                delta = jnp.sum(jnp.where(in_head, prod, 0.0), axis=1, keepdims=True)
                lse_h = lse_v[:, HEAD_DIM * hh:HEAD_DIM * hh + 1]
                sc = _dot_nt(qm, kb) + b_ref[2 * pr + hh] + kmask
                p = jnp.exp(sc - lse_h)
                dp = _dot_nt(dom, vb)
                ds = p * (dp - delta)
                ds_ref[2 * pr + hh] += ds
                dsb = ds.astype(BF16)
                dq_parts.append(_dot(dsb, kb) * scale)
                dk_new = dk_new + _dot_tn(dsb, qm)
                dv_new = dv_new + _dot_tn(p.astype(BF16), dom)
            dq_ref[:, ls] = jnp.where(lane < HEAD_DIM, dq_parts[0], dq_parts[1]).astype(BF16)
            dk_acc[pl.ds(koff, K_BAND), ls] += dk_new
            dv_acc[pl.ds(koff, K_BAND), ls] += dv_new

        @pl.when(blk == nb - 1)
        def _():
            dk_ref[...] = dk_acc[pad:pad + s, :].astype(BF16)
            dv_ref[...] = dv_acc[pad:pad + s, :].astype(BF16)

    n_grp = ATTN_WIDTH // wide
    qblk = pl.BlockSpec((Q_BLOCK, wide), lambda p, b: (b, p))
    col = pl.BlockSpec((s, wide), lambda p, b: (0, p))
    shp = jax.ShapeDtypeStruct((s, ATTN_WIDTH), BF16)
    return _call(
        body, name="attn_bwd", grid=(n_grp, nb),
        in_specs=[pl.BlockSpec((Q_BLOCK, wide), lambda p, b: (qb0 + b, p)),
                  pl.BlockSpec((s + pad, wide), lambda p, b: (0, n_grp + p)),
                  pl.BlockSpec((s + pad, wide), lambda p, b: (0, 2 * n_grp + p)),
                  pl.BlockSpec((None, 2 * ATTN_PAIRS, Q_BLOCK, K_BAND), lambda p, b: (l, p, 0, 0)),
                  qblk, qblk, qblk],
        out_specs=[qblk, col, col, pl.BlockSpec((2 * ATTN_PAIRS, Q_BLOCK, K_BAND), lambda p, b: (p, 0, 0))],
        out_shape=[shp, shp, shp, jax.ShapeDtypeStruct((N_HEADS, Q_BLOCK, K_BAND), F32)],
        scratch=[pltpu.VMEM((s + pad, wide), F32), pltpu.VMEM((s + pad, wide), F32)],
        args=[qkvp, qkvp, qkvp, biasm, o, lse, do], comm=comm)


def _dw_in(h, dproj, s):
    def body(a_ref, b_ref, o_ref):
        acc = _dot_tn(a_ref[...], b_ref[...])
        o_ref[0] = acc[:, 0:PROJ_SHARD].astype(BF16)
        o_ref[1] = acc[:, PROJ_SHARD:2 * PROJ_SHARD].astype(BF16)

    out, = _call(
        body, name="dw_in", grid=(4,),
        in_specs=[pl.BlockSpec((s, D_MODEL), lambda n: (0, 0)),
                  pl.BlockSpec((s, 2 * PROJ_SHARD), lambda n: (0, n))],
        out_specs=[pl.BlockSpec((2, D_MODEL, PROJ_SHARD), lambda n: (n, 0, 0))],
        out_shape=[jax.ShapeDtypeStruct((N_DEV, D_MODEL, PROJ_SHARD), BF16)], args=[h, dproj])
    return out


def _in_proj_bwd(dproj, win, x, g3, dres, l, s, tm, f_prev=None, g_post3=None, comm=None):
    chain = f_prev is not None

    def body(d_ref, w_ref, x_ref, g_ref, dres_ref, *rest):
        if chain:
            f_ref, gq_ref, dx_ref, dg_ref, df_ref, dgq_ref = rest
            _zero_first((dg_ref, dgq_ref), pl.program_id(0) == 0)
        else:
            dx_ref, dg_ref = rest
            _zero_first((dg_ref,), pl.program_id(0) == 0)
        w = jnp.concatenate([w_ref[j] for j in range(N_DEV)], axis=1)
        for rs in _row_subtiles(tm, SUB_ROWS):
            dh = _dot_nt(d_ref[rs, :], w)
            dx, dyn = _norm_bwd_rows(x_ref[rs, :], g_ref[...], dh)
            dx = dres_ref[rs, :] + dx
            dx_ref[rs, :] = dx
            _add_cols(dg_ref, dyn)
            if chain:
                df, dyn2 = _norm_bwd_rows(f_ref[rs, :], gq_ref[...], dx)
                df_ref[rs, :] = df.astype(BF16)
                _add_cols(dgq_ref, dyn2)

    row = pl.BlockSpec((tm, D_MODEL), lambda i: (i, 0))
    dgs = pl.BlockSpec((8, D_MODEL), lambda i: (0, 0))
    in_specs = [pl.BlockSpec((tm, PROJ_WIDTH), lambda i: (i, 0)),
                pl.BlockSpec((N_DEV, D_MODEL, PROJ_SHARD), lambda i: (0, 0, 0), pipeline_mode=pl.Buffered(1)),
                row, pl.BlockSpec((None, 1, D_MODEL), lambda i: (l, 0, 0)), row]
    out_specs = [row, dgs]
    out_shape = [jax.ShapeDtypeStruct((s, D_MODEL), F32), jax.ShapeDtypeStruct((8, D_MODEL), F32)]
    args = [dproj, win, x, g3, dres]
    if chain:
        in_specs += [row, pl.BlockSpec((None, 1, D_MODEL), lambda i: (l - 1, 0, 0))]
        out_specs += [row, dgs]
        out_shape += [jax.ShapeDtypeStruct((s, D_MODEL), BF16), jax.ShapeDtypeStruct((8, D_MODEL), F32)]
        args += [f_prev, g_post3]
    return _call(body, name="in_proj_bwd", grid=(s // tm,), in_specs=in_specs, out_specs=out_specs,
                 out_shape=out_shape, args=args, comm=comm)


def _adamw(name, w, m, v, lands, owns=None, me=None):
    groups, rows, cols = w.shape
    assert len(lands) == groups
    n_part = lands[0].shape[0]
    tr = _row_tile(rows, tuple(c for c in (512, 352, 256, 176, 128, 64, 32, 16, 8) if c * cols <= 256 * 1024))
    c1 = 1.0 - ADAM_B1 ** ADAM_STEP
    c2 = 1.0 - ADAM_B2 ** ADAM_STEP
    n_own = groups if owns is not None else 0

    def body(*refs):
        if n_own:
            me_ref, refs = refs[0], refs[1:]
        w_ref, m_ref, v_ref = refs[:3]
        land_refs = refs[3:3 + groups]
        own_refs = refs[3 + groups:3 + groups + n_own]
        g_ref, d_ref, nm_ref, nv_ref = refs[3 + groups + n_own:]
        grp = pl.program_id(0)
        for gi in range(groups):
            @pl.when(grp == gi)
            def _():
                l_ref = land_refs[gi]
                g = None
                for p in range(n_part):
                    part = l_ref[p].astype(F32)
                    if n_own:
                        part = jnp.where(me_ref[0] == p, own_refs[gi][...].astype(F32), part)
                    g = part if g is None else g + part
                g_ref[...] = g
                m1 = ADAM_B1 * m_ref[...] + (1.0 - ADAM_B1) * g
                v1 = ADAM_B2 * v_ref[...] + (1.0 - ADAM_B2) * (g * g)
                nm_ref[...] = m1
                nv_ref[...] = v1
                d_ref[...] = -ADAM_LR * ((m1 / c1) / (jnp.sqrt(v1 / c2) + ADAM_EPS) + ADAM_WD * w_ref[...])

    blk = pl.BlockSpec((None, tr, cols), lambda g, i, *_: (g, i, 0))
    shp = jax.ShapeDtypeStruct((groups, rows, cols), F32)

    def land_spec(gi):
        return pl.BlockSpec((n_part, tr, cols), lambda g, i, *_: (0, jnp.where(g == gi, i, 0), 0))

    def own_spec(gi):
        if owns[gi].ndim == 3:
            return pl.BlockSpec((None, tr, cols), lambda g, i, me_ref: (me_ref[0], jnp.where(g == gi, i, 0), 0))
        return pl.BlockSpec((tr, cols), lambda g, i, me_ref: (jnp.where(g == gi, i, 0), 0))

    in_specs = [blk, blk, blk] + [land_spec(gi) for gi in range(groups)] + [own_spec(gi) for gi in range(n_own)]
    args = [w, m, v] + list(lands) + (list(owns) if n_own else [])
    if not n_own:
        return _call(body, name=name, grid=(groups, rows // tr), in_specs=in_specs,
                     out_specs=[blk, blk, blk, blk], out_shape=[shp, shp, shp, shp], args=args)
    return pl.pallas_call(
        body, name=name,
        grid_spec=pltpu.PrefetchScalarGridSpec(
            num_scalar_prefetch=1, grid=(groups, rows // tr), in_specs=in_specs, out_specs=[blk, blk, blk, blk]),
        out_shape=[shp, shp, shp, shp],
        compiler_params=pltpu.CompilerParams(dimension_semantics=("arbitrary", "arbitrary"),
                                             vmem_limit_bytes=VMEM_LIMIT),
    )(me, *args)


def _pack_small(rel, gco, gao, gpm, gqm, gpf, gqf):
    n_layers = rel.shape[0]
    relp = jnp.pad(rel, ((0, 0), (0, 0), (0, REL_PAD - rel.shape[2])))
    parts = [relp.reshape(n_layers * N_HEADS * REL_PAD // 128, 128)]
    parts += [a.reshape(-1, 128) for a in (gco, gao, gpm, gqm, gpf, gqf)]
    return jnp.concatenate(parts, axis=0)


def _pack_small_grads(d_rel, parts):
    n_layers = len(d_rel)
    keys = ("gco", "gao", "gpm", "gqm", "gpf", "gqf")
    arrays = list(d_rel) + [parts[k][l] for k in keys for l in range(n_layers)] + list(parts["wc"])
    rows = 0
    plan = []
    for l in range(n_layers):
        for h in range(N_HEADS):
            for t in range(REL_PAD // 128):
                plan.append((l, (0, h), t, rows))
                rows += 1
    for ki, k in enumerate(keys):
        for l in range(n_layers):
            for t in range(parts[k][l].shape[1] // 128):
                plan.append((n_layers * (1 + ki) + l, (0,), t, rows))
                rows += 1
    for l in range(n_layers):
        for tap in range(3):
            for t in range(CONV_WIDTH // 128):
                plan.append((n_layers * (1 + len(keys)) + l, (tap,), t, rows))
                rows += 1
    total = rows + (-rows) % 8

    def body(*refs):
        o_ref = refs[-1]
        if total > rows:
            o_ref[rows:total, :] = jnp.zeros((total - rows, 128), F32)
        for op, idx, t, dst in plan:
            lanes = slice(128 * t, 128 * (t + 1))
            if len(idx) == 2:
                o_ref[dst:dst + 1, :] = refs[op][idx[0], idx[1]:idx[1] + 1, lanes]
            else:
                o_ref[dst:dst + 1, :] = refs[op][idx[0]:idx[0] + 1, lanes]

    vmem = pl.BlockSpec(memory_space=pltpu.VMEM)
    out, = _call(body, name="pack_small_grads", grid=(), in_specs=[vmem] * len(arrays), out_specs=[vmem],
                 out_shape=[jax.ShapeDtypeStruct((total, 128), F32)], args=arrays)
    return out


def _unpack_small(p, n_layers):
    n_rel = n_layers * N_HEADS * REL_PAD // 128
    rel = p[:n_rel].reshape(n_layers, N_HEADS, REL_PAD)[:, :, :2 * REL_CLIP + 1]
    outs = [rel]
    r0 = n_rel
    for width in (CONV_WIDTH, ATTN_WIDTH, D_MODEL, D_MODEL, D_MODEL, D_MODEL):
        nr = n_layers * width // 128
        outs.append(p[r0:r0 + nr].reshape(n_layers, width))
        r0 += nr
    return outs


def kernel(x, w_in, w_conv, rel_bias, g_conv_out, g_attn_out, w_out, g_pre_mix, g_post_mix, g_pre_ffn, g_post_ffn, w_ffn_in, w_ffn_out, loss_target, m_w_in, m_w_conv, m_rel_bias, m_g_conv_out, m_g_attn_out, m_w_out, m_g_pre_mix, m_g_post_mix, m_g_pre_ffn, m_g_post_ffn, m_w_ffn_in, m_w_ffn_out, v_w_in, v_w_conv, v_rel_bias, v_g_conv_out, v_g_attn_out, v_w_out, v_g_pre_mix, v_g_post_mix, v_g_pre_ffn, v_g_post_ffn, v_w_ffn_in, v_w_ffn_out):
    n_layers = w_in.shape[0]
    s = x.shape[1]
    assert x.shape == (1, s, D_MODEL) and s % 1024 == 0
    assert w_in.shape == (n_layers, D_MODEL, PROJ_SHARD) and w_ffn_in.shape == (n_layers, D_MODEL, FF_SHARD)
    tm = 512
    tq = 1024 if s >= 2048 else 512
    tf = min(1024, s)
    x0 = x.reshape(s, D_MODEL)
    target = loss_target.reshape(s, D_MODEL)
    dev = _dev_index(lax.axis_index("x"), lax.axis_index("y"), lax.axis_index("c"))

    wt_ffn_in, mt_ffn_in, vt_ffn_in = (jnp.transpose(a, (0, 2, 1)) for a in (w_ffn_in, m_w_ffn_in, v_w_ffn_in))
    local_w = [_cast_bf16(w_in, "cast_w_in"), _cast_bf16(w_out, "cast_w_out"),
               _cast_bf16(wt_ffn_in, "cast_w_ffn_in"), _cast_bf16(w_ffn_out, "cast_w_ffn_out")]
    wc_local = jnp.pad(jnp.transpose(w_conv, (0, 2, 1)).reshape(-1), (0, 1024 - n_layers * 3 * 64)).reshape(8, 128)
    biasm, win_next, wc_g = _bias_build(jnp.pad(rel_bias, ((0, 0), (0, 0), (0, REL_PAD - rel_bias.shape[2]))),
                                        comm=_Gather([(local_w[0], 0), (wc_local, None)]))
    weights = [None] * n_layers
    wc_full = wc_g.reshape(N_DEV, 1024)[:, :n_layers * 3 * 64].reshape(N_DEV, n_layers, 3, 64)
    wc_full = jnp.transpose(wc_full, (1, 2, 0, 3)).reshape(n_layers, 3, CONV_WIDTH)
    wc_full = jnp.pad(wc_full, ((0, 0), (0, 5), (0, 0)))

    g3 = {k: v.reshape(n_layers, 1, -1) for k, v in dict(
        conv=g_conv_out, attn=g_attn_out, pre_mix=g_pre_mix, post_mix=g_post_mix,
        pre_ffn=g_pre_ffn, post_ffn=g_post_ffn).items()}

    saved = []
    xl = x0
    h = _norm_cast(x0, g3["pre_mix"], 0, tm)
    for l in range(n_layers):
        win = win_next
        pc, ync, wout = _in_proj_conv(h, win, wc_full, g3["conv"], l, s, tq, comm=_Gather([(local_w[1], l)]))
        qkvp = _in_proj_qkv(h, win, s, tq)
        o, lse, yna, wfin = _attn_fwd(qkvp, biasm, g3["attn"], l, s, tq, comm=_Gather([(local_w[2], l)]))
        wout = wout.reshape(D_MODEL, D_MODEL)
        z, xm, h2 = _out_proj_fwd(ync, yna, wout, xl, g3["post_mix"], g3["pre_ffn"], l, s, tq)
        gu, act, wfout = _ffn_in_fwd(h2, wfin.reshape(4, FF_PAIR, D_MODEL), s, tf, comm=_Gather([(local_w[3], l)]))
        wfo = wfout.reshape(D_FF, D_MODEL)
        weights[l] = [win, wout, wfin.reshape(2 * D_FF, D_MODEL), wfo]
        sv = dict(x=xl, h=h, pc=pc, qkvp=qkvp, ync=ync, yna=yna, o=o, lse=lse, z=z, xm=xm, h2=h2, gu=gu, act=act)
        if l + 1 < n_layers:
            sv["f"], xl, h, win_next = _ffn_out_fwd(act, wfo, xm, g3["post_ffn"], g3["pre_mix"], l, l + 1, s, tm,
                                                    comm=_Gather([(local_w[0], l + 1)]))
        else:
            dx, sq, df, dg_post_ffn = _ffn_out_loss(act, wfo, xm, g3["post_ffn"], target, l, s, tm)
        saved.append(sv)

    loss = lax.psum(jnp.sum(sq) * (0.5 / D_MODEL), ("x", "y", "c"))

    lands = dict(win=[None] * n_layers, wout=[None] * n_layers, wfin=[None] * n_layers, wfout=[None] * n_layers)
    small = {k: [None] * n_layers for k in ("gco", "gao", "gpm", "gqm", "gpf", "gqf", "wc")}
    d_rel = [None] * n_layers
    started = []

    def start(name, keys, l, arrays):
        items = [(a, False) for a in arrays]
        send_sems, recv_sems, srcs, zones, token = _exchange_start(name + "_start", items)
        started.append((name, keys, l, items, send_sems, recv_sems, srcs, zones))
        return token[0:1, 0:1].reshape(1, 1, 1)

    for l in reversed(range(n_layers)):
        sv = saved[l]
        win, wout, wfin, wfo = weights[l]
        small["gqf"][l] = dg_post_ffn
        dgu, d_wfout = _ffn_out_bwd(df, wfo, sv["gu"], sv["act"], s, tm)
        d_wfout = d_wfout.reshape(N_DEV, FFO_SHARD, D_MODEL)
        d_wfin = _dw_ffn_in(sv["h2"], dgu, s).reshape(N_DEV, FF_SHARD, D_MODEL)
        g_pre_ffn = g3["pre_ffn"]
        if l == 0:
            g_pre_ffn = g_pre_ffn + start("exchange_ffn0", ("wfout", "wfin"), l, [d_wfout, d_wfin])
        dxm, dz, dg_pre_ffn, dg_post_mix = _ffn_in_bwd(
            dgu, wfin, sv["xm"], g_pre_ffn, dx, sv["z"], g3["post_mix"], l, s, tm)
        small["gpf"][l] = dg_pre_ffn
        small["gqm"][l] = dg_post_mix
        dyc, do, dg_attn, d_wout = _out_proj_bwd(dz, wout, sv["o"], g3["attn"], sv["ync"], sv["yna"], l, s, tq)
        d_wout = d_wout.reshape(N_DEV, D_MODEL // N_DEV, D_MODEL)
        small["gao"][l] = dg_attn
        dq, dk, dv, ds_sum = _attn_bwd(sv["qkvp"], biasm, sv["o"], sv["lse"], do, l, s, tq)
        d_rel[l] = _bias_bwd(ds_sum[None])
        dproj, dwc, dg_conv = _conv_bwd(sv["pc"], dyc, wc_full, g3["conv"], dq, dk, dv, l, s, tm)
        small["wc"][l] = dwc
        small["gco"][l] = dg_conv
        d_win = _dw_in(sv["h"], dproj, s)
        if l == 0:
            token = start("exchange_mix0", ("wout", "win"), l, [d_wout, d_win])
        else:
            token = start(f"exchange_layer{l}", ("wfout", "wfin", "wout", "win"), l, [d_wfout, d_wfin, d_wout, d_win])
        if l > 0:
            dx, dg_pre_mix, df, dg_post_ffn = _in_proj_bwd(
                dproj, win, sv["x"], g3["pre_mix"] + token, dxm, l, s, tm, f_prev=saved[l - 1]["f"],
                g_post3=g3["post_ffn"])
        else:
            dx, dg_pre_mix = _in_proj_bwd(dproj, win, sv["x"], g3["pre_mix"] + token, dxm, l, s, tm)
        small["gpm"][l] = dg_pre_mix
    grad_x = dx.reshape(1, s, D_MODEL)

    small_vec = _pack_small_grads(d_rel, small)
    small_items = [(small_vec, True)]
    small_sems = _exchange_start("exchange_small_start", small_items)

    owns = dict(win=[None] * n_layers, wout=[None] * n_layers, wfin=[None] * n_layers, wfout=[None] * n_layers)

    def wait(last, after):
        for name, keys, l, items, send_sems, recv_sems, srcs, zones in started:
            if (name == "exchange_mix0") == last:
                srcs, zones = _exchange_wait(name + "_wait", items, send_sems, recv_sems, srcs, zones, after)
                for key, src, zone in zip(keys, srcs, zones):
                    owns[key][l], lands[key][l] = src, zone

    me = dev.astype(jnp.int32).reshape(1)
    wait(False, small_sems[4])
    r_fin = [jnp.transpose(t, (0, 2, 1)) for t in _adamw(
        "adamw_w_ffn_in", wt_ffn_in, mt_ffn_in, vt_ffn_in, lands["wfin"], owns["wfin"], me)]
    r_fout = _adamw("adamw_w_ffn_out", w_ffn_out, m_w_ffn_out, v_w_ffn_out, lands["wfout"], owns["wfout"], me)
    wait(True, r_fout[0])
    r_out = _adamw("adamw_w_out", w_out, m_w_out, v_w_out, lands["wout"], owns["wout"], me)
    r_in = _adamw("adamw_w_in", w_in, m_w_in, v_w_in, lands["win"], owns["win"], me)
    (small_own,), (land_small,) = _exchange_wait(
        "exchange_small_wait", small_items, small_sems[0], small_sems[1], small_sems[2], small_sems[3], r_in[0])

    n_rep = 64 * n_layers
    rep = _adamw(
        "adamw_replicated",
        _pack_small(rel_bias, g_conv_out, g_attn_out, g_pre_mix, g_post_mix, g_pre_ffn, g_post_ffn)[None],
        _pack_small(m_rel_bias, m_g_conv_out, m_g_attn_out, m_g_pre_mix, m_g_post_mix, m_g_pre_ffn, m_g_post_ffn)[None],
        _pack_small(v_rel_bias, v_g_conv_out, v_g_attn_out, v_g_pre_mix, v_g_post_mix, v_g_pre_ffn, v_g_post_ffn)[None],
        [land_small[:, :n_rep]], [small_own[:n_rep]], me)
    rep = [_unpack_small(t[0], n_layers) for t in rep]

    wc_rows = n_layers * 3 * CONV_WIDTH // 128
    zeros_wc = jnp.zeros((1, wc_rows, 128), F32)
    g_wc_full = _adamw("sum_w_conv", zeros_wc, zeros_wc, zeros_wc, [land_small[:, n_rep:n_rep + wc_rows]],
                       [small_own[n_rep:n_rep + wc_rows]], me)[0]
    g_wc_full = g_wc_full.reshape(n_layers, 3, CONV_WIDTH)
    g_wc = lax.dynamic_slice_in_dim(g_wc_full, dev * (CONV_WIDTH // N_DEV), CONV_WIDTH // N_DEV, axis=2)
    g_wc = jnp.transpose(g_wc, (0, 2, 1))

    def tiny(a):
        flat = a.reshape(-1)
        return jnp.pad(flat, (0, (-flat.shape[0]) % 1024)).reshape(1, -1, 128)

    r_wc = _adamw("adamw_w_conv", tiny(w_conv), tiny(m_w_conv), tiny(v_w_conv), [tiny(g_wc)])
    r_wc = [t.reshape(-1)[:w_conv.size].reshape(w_conv.shape) for t in r_wc]

    def leaf(kind):
        return [r_in[kind], r_wc[kind], rep[kind][0], rep[kind][1], rep[kind][2], r_out[kind],
                rep[kind][3], rep[kind][4], rep[kind][5], rep[kind][6], r_fin[kind], r_fout[kind]]

    return (loss, grad_x, *leaf(0), *leaf(1), *leaf(2), *leaf(3))
```

```python
import math

import jax
import jax.numpy as jnp
from jax import lax
from jax.experimental import pallas as pl
from jax.experimental.pallas import tpu as pltpu

F32 = jnp.float32
BF16 = jnp.bfloat16

D_MODEL = 1024
N_DEV = 8
CHUNK = 64
N_LEFT_CHUNKS = 8
CONV_WIDTH = 512
ATTN_WIDTH = 512
HEAD_DIM = 64
N_HEADS = 8
REL_CLIP = 128
REL_PAD = 384
PROJ_WIDTH = 3072
PROJ_SHARD = PROJ_WIDTH // N_DEV
D_FF = 2816
FF_SHARD = 2 * D_FF // N_DEV
FFO_SHARD = D_FF // N_DEV
FF_PAIR = 2 * FF_SHARD
_COL_SUBTILES = (slice(0, 768), slice(768, FF_PAIR))
EPS = 1e-6
NEG_INF = -1e30
ATTN_PAIRS = 2
Q_BLOCK = 4 * CHUNK
K_BAND = Q_BLOCK + N_LEFT_CHUNKS * CHUNK
LEFT = N_LEFT_CHUNKS * CHUNK
TOEP = 1024

ADAM_LR = 0.001
ADAM_B1 = 0.9
ADAM_B2 = 0.999
ADAM_EPS = 1e-08
ADAM_WD = 0.01
ADAM_STEP = 10

VMEM_LIMIT = 52 * 1024 * 1024
SUB_ROWS = 256
MESH = pl.DeviceIdType.MESH
ANY = pl.BlockSpec(memory_space=pl.ANY)

NT = (((1,), (1,)), ((), ()))
TN = (((0,), (0,)), ((), ()))


def _dot(a, b):
    return jnp.dot(a, b, preferred_element_type=F32)


def _dot_nt(a, b):
    return lax.dot_general(a, b, NT, preferred_element_type=F32)


def _dot_tn(a, b):
    return lax.dot_general(a, b, TN, preferred_element_type=F32)


def _rstd(v):
    return lax.rsqrt(jnp.mean(v * v, axis=-1, keepdims=True) + EPS)


def _group_matrix():
    r = lax.broadcasted_iota(jnp.int32, (128, 128), 0) >> 6
    c = lax.broadcasted_iota(jnp.int32, (128, 128), 1) >> 6
    return jnp.where(r == c, 1.0, 0.0).astype(BF16)


def _group_mean(v, gmat):
    hi = v.astype(BF16)
    lo = (v - hi.astype(F32)).astype(BF16)
    return (_dot(hi, gmat) + _dot(lo, gmat)) * (1.0 / HEAD_DIM)


def _split3(v):
    hi = v.astype(BF16)
    r1 = v - hi.astype(F32)
    mid = r1.astype(BF16)
    lo = (r1 - mid.astype(F32)).astype(BF16)
    return hi, mid, lo


def _row_tile(rows, cands=(1024, 512, 704, 256, 128, 64, 32, 16)):
    for c in cands:
        if rows % c == 0:
            return c
    return rows


def _dev_index(px, py, pc):
    return 4 * px + 2 * py + pc


def _when(cond):
    if cond is True:
        return lambda fn: fn()
    return pl.when(cond)


def _phases(grid):
    def phases():
        if not grid:
            return True, True, True
        lin = pl.program_id(0)
        for a in range(1, len(grid)):
            lin = lin * grid[a] + pl.program_id(a)
        total = math.prod(grid)
        return lin == 0, lin == total - 1, lin == total - 1
    return phases


class _Gather:
    def __init__(self, items):
        self.items = items
        self.args = [a for a, _ in items]
        n = len(items)
        self.out_shape = [jax.ShapeDtypeStruct((N_DEV,) + (a.shape if lay is None else a.shape[1:]), a.dtype)
                          for a, lay in items]
        self.scratch = [pltpu.SemaphoreType.DMA((n, 7)), pltpu.SemaphoreType.DMA((n, 7)),
                        pltpu.SemaphoreType.DMA((n,))]

    def _ctx(self, ins, outs, sems):
        send_sems, recv_sems, local_sems = sems
        x, y, c = lax.axis_index("x"), lax.axis_index("y"), lax.axis_index("c")
        chips = [(1 - x, y), (x, 1 - y), (1 - x, 1 - y)]

        def src(k):
            lay = self.items[k][1]
            return ins[k] if lay is None else ins[k].at[lay]

        def copy(k, s, idx, to, from_src=False):
            return pltpu.make_async_remote_copy(
                src_ref=src(k) if from_src else outs[k].at[idx], dst_ref=outs[k].at[idx],
                send_sem=send_sems.at[k, s], recv_sem=recv_sems.at[k, s],
                device_id=to, device_id_type=MESH)

        def local(k):
            return pltpu.make_async_copy(src(k), outs[k].at[_dev_index(x, y, c)], local_sems.at[k])

        return x, y, c, chips, copy, local

    def start(self, ins, outs, sems, cond):
        n = len(self.items)

        @_when(cond)
        def _():
            x, y, c, chips, copy, local = self._ctx(ins, outs, sems)
            me = _dev_index(x, y, c)
            for k in range(n):
                local(k).start()
                copy(k, 0, me, (x, y, 1 - c), from_src=True).start()
                for j, chip in enumerate(chips):
                    copy(k, 1 + j, me, (chip[0], chip[1], c), from_src=True).start()

    def forward(self, ins, outs, sems, cond):
        n = len(self.items)

        @_when(cond)
        def _():
            x, y, c, chips, copy, local = self._ctx(ins, outs, sems)
            for j, chip in enumerate(chips):
                idx = _dev_index(chip[0], chip[1], c)
                for k in range(n):
                    copy(k, 1 + j, idx, (x, y, c)).wait_recv()
                    copy(k, 4 + j, idx, (x, y, 1 - c)).start()

    def finish(self, ins, outs, sems, cond):
        n = len(self.items)

        @_when(cond)
        def _():
            x, y, c, chips, copy, local = self._ctx(ins, outs, sems)
            me = _dev_index(x, y, c)
            for k in range(n):
                copy(k, 0, _dev_index(x, y, 1 - c), (x, y, c)).wait_recv()
            for j, chip in enumerate(chips):
                idx = _dev_index(chip[0], chip[1], 1 - c)
                for k in range(n):
                    copy(k, 4 + j, idx, (x, y, c)).wait_recv()
            for k in range(n):
                for s in range(4):
                    copy(k, s, me, (x, y, c), from_src=True).wait_send()
                for j, chip in enumerate(chips):
                    copy(k, 4 + j, _dev_index(chip[0], chip[1], c), (x, y, c)).wait_send()
                local(k).wait()


_PEER_FLIPS = [(0, 0, 1), (1, 0, 0), (0, 1, 0), (1, 1, 0), (1, 0, 1), (0, 1, 1), (1, 1, 1)]


def _call(body, *, name, grid, in_specs, out_specs, out_shape, args, scratch=(), comm=None):
    n_hi, n_ho, n_hs = len(args), len(out_shape), len(scratch)
    c_args = list(comm.args) if comm else []
    c_out = list(comm.out_shape) if comm else []
    c_scr = list(comm.scratch) if comm else []
    phases = _phases(grid)

    def kern(*refs):
        cuts = [n_hi, len(c_args), n_ho, len(c_out), n_hs, len(c_scr)]
        parts, pos = [], 0
        for n in cuts:
            parts.append(refs[pos:pos + n])
            pos += n
        hi, ci, ho, co, hs, cs = parts
        if comm:
            first, mid, last = phases()
            comm.start(ci, co, cs, first)
            comm.forward(ci, co, cs, mid)
        body(*hi, *ho, *hs)
        if comm:
            comm.finish(ci, co, cs, last)

    sem = ("arbitrary",) * len(grid) if grid else None
    return pl.pallas_call(
        kern, name=name, grid=grid,
        in_specs=list(in_specs) + [ANY] * len(c_args),
        out_specs=list(out_specs) + [ANY] * len(c_out),
        out_shape=list(out_shape) + c_out,
        scratch_shapes=list(scratch) + c_scr,
        compiler_params=pltpu.CompilerParams(dimension_semantics=sem, vmem_limit_bytes=VMEM_LIMIT),
    )(*args, *c_args)


def _comm_only(name, comm):
    return _call(lambda: None, name=name, grid=(), in_specs=[], out_specs=[], out_shape=[], args=[], comm=comm)


HBM_SPEC = pl.BlockSpec(memory_space=pltpu.HBM)
SEM_SPEC = pl.BlockSpec(memory_space=pltpu.SEMAPHORE)
SIDE_EFFECT = pltpu.SideEffectType.DATAFLOW_SIDE_EFFECTING


def _exchange_peer(x, y, c, s):
    fx, fy, fc = _PEER_FLIPS[s]
    return x ^ fx, y ^ fy, c ^ fc


def _exchange_start(name, items):
    n = len(items)
    srcs = [pltpu.with_memory_space_constraint(a, pltpu.HBM) for a, _ in items]
    land_shapes = [(N_DEV,) + (a.shape if whole else a.shape[1:]) for a, whole in items]
    lands = [pltpu.with_memory_space_constraint(lax.empty(shp, a.dtype), pltpu.HBM)
             for shp, (a, _) in zip(land_shapes, items)]

    n_sem = 7 * n

    def body(*refs):
        src_refs, land_refs = refs[:n], refs[n:2 * n]
        send_sems = refs[2 * n:2 * n + n_sem]
        recv_sems = refs[2 * n + n_sem:2 * n + 2 * n_sem]
        token = refs[-1]
        x, y, c = lax.axis_index("x"), lax.axis_index("y"), lax.axis_index("c")
        me = _dev_index(x, y, c)
        for s in range(7):
            px, py, pc = _exchange_peer(x, y, c, s)
            for k in range(n):
                src = src_refs[k] if items[k][1] else src_refs[k].at[_dev_index(px, py, pc)]
                pltpu.make_async_remote_copy(
                    src_ref=src, dst_ref=land_refs[k].at[me],
                    send_sem=send_sems[7 * k + s], recv_sem=recv_sems[7 * k + s],
                    device_id=(px, py, pc), device_id_type=MESH).start()
        token[...] = jnp.zeros(token.shape, token.dtype)

    outs = pl.pallas_call(
        body, name=name,
        out_shape=(*[pltpu.SemaphoreType.DMA(())] * (2 * n_sem),
                   *[pltpu.HBM(a.shape, a.dtype) for a in srcs],
                   *[pltpu.HBM(shp, a.dtype) for shp, a in zip(land_shapes, srcs)],
                   jax.ShapeDtypeStruct((8, 128), F32)),
        in_specs=[HBM_SPEC] * (2 * n),
        out_specs=(*[SEM_SPEC] * (2 * n_sem), *[HBM_SPEC] * (2 * n), pl.BlockSpec(memory_space=pltpu.VMEM)),
        input_output_aliases={i: 2 * n_sem + i for i in range(2 * n)},
        compiler_params=pltpu.CompilerParams(has_side_effects=SIDE_EFFECT),
    )(*srcs, *lands)
    base = 2 * n_sem
    return (list(outs[:n_sem]), list(outs[n_sem:base]), list(outs[base:base + n]),
            list(outs[base + n:base + 2 * n]), outs[-1])


def _exchange_wait(name, items, send_sems, recv_sems, srcs, lands, after):
    n = len(items)

    n_sem = 7 * n

    def body(*refs):
        src_refs, land_refs = refs[:n], refs[n:2 * n]
        send_refs = refs[2 * n:2 * n + n_sem]
        recv_refs = refs[2 * n + n_sem:2 * n + 2 * n_sem]
        x, y, c = lax.axis_index("x"), lax.axis_index("y"), lax.axis_index("c")
        for s in range(7):
            for k in range(n):
                copy = pltpu.make_async_remote_copy(
                    src_ref=src_refs[k] if items[k][1] else src_refs[k].at[0], dst_ref=land_refs[k].at[0],
                    send_sem=send_refs[7 * k + s], recv_sem=recv_refs[7 * k + s],
                    device_id=(x, y, c), device_id_type=MESH)
                copy.wait_send()
                copy.wait_recv()

    outs = pl.pallas_call(
        body, name=name,
        out_shape=(*[pltpu.HBM(a.shape, a.dtype) for a in srcs], *[pltpu.HBM(a.shape, a.dtype) for a in lands]),
        in_specs=[HBM_SPEC] * (2 * n) + [SEM_SPEC] * (2 * n_sem) + [ANY],
        out_specs=tuple([HBM_SPEC] * (2 * n)),
        input_output_aliases={i: i for i in range(2 * n)},
        compiler_params=pltpu.CompilerParams(has_side_effects=SIDE_EFFECT),
    )(*srcs, *lands, *send_sems, *recv_sems, after)
    return list(outs[:n]), list(outs[n:])


def _cast_bf16(x, name):
    shape = x.shape
    x2 = x.reshape(-1, shape[-1])
    rows, cols = x2.shape
    tr = _row_tile(rows)

    def body(x_ref, o_ref):
        o_ref[...] = x_ref[...].astype(BF16)

    blk = pl.BlockSpec((tr, cols), lambda i: (i, 0))
    out, = _call(body, name=name, grid=(rows // tr,), in_specs=[blk], out_specs=[blk],
                 out_shape=[jax.ShapeDtypeStruct((rows, cols), BF16)], args=[x2])
    return out.reshape(shape)


def _norm_cast(x, g3, l, tm):
    s = x.shape[0]

    def body(x_ref, g_ref, o_ref):
        v = x_ref[...]
        o_ref[...] = (v * _rstd(v) * g_ref[...]).astype(BF16)

    row = pl.BlockSpec((tm, D_MODEL), lambda i: (i, 0))
    out, = _call(body, name="norm_cast", grid=(s // tm,),
                 in_specs=[row, pl.BlockSpec((None, 1, D_MODEL), lambda i: (l, 0, 0))], out_specs=[row],
                 out_shape=[jax.ShapeDtypeStruct((s, D_MODEL), BF16)], args=[x, g3])
    return out


def _in_proj_qkv(h, win, s, tq):
    def body(a_ref, b_ref, o_ref):
        i = pl.program_id(0)

        @pl.when(i == 0)
        def _():
            o_ref[...] = jnp.zeros(o_ref.shape, BF16)

        @pl.when(i > 0)
        def _():
            w = jnp.concatenate([b_ref[j] for j in range(4)], axis=1)
            o_ref[...] = _dot(a_ref[...], w).astype(BF16)

    out, = _call(
        body, name="in_proj_qkv", grid=(s // tq + 1,),
        in_specs=[pl.BlockSpec((tq, D_MODEL), lambda i: (jnp.maximum(i - 1, 0), 0)),
                  pl.BlockSpec((4, D_MODEL, PROJ_SHARD), lambda i: (1, 0, 0))],
        out_specs=[pl.BlockSpec((tq, 4 * PROJ_SHARD), lambda i: (i, 0))],
        out_shape=[jax.ShapeDtypeStruct((s + tq, PROJ_WIDTH // 2), BF16)], args=[h, win])
    return out


def _in_proj_conv(h, win, wc, g3, l, s, tq, comm=None):
    def body(a_ref, b_ref, wc_ref, g_ref, pc_ref, y_ref, carry_ref, acc_ref):
        i = pl.program_id(0)
        w = jnp.concatenate([b_ref[j] for j in range(4)], axis=1)
        acc_ref[...] = _dot(a_ref[...], w)
        pc_ref[...] = acc_ref[...].astype(BF16)
        gmat = _group_matrix()
        for j in range(CONV_WIDTH // 128):
            c0, c1, c2 = 128 * j, CONV_WIDTH + 128 * j, 2 * CONV_WIDTH + 128 * j
            hc = acc_ref[:, c0:c0 + 128]
            bg = acc_ref[:, c1:c1 + 128]
            cg = acc_ref[:, c2:c2 + 128]
            u_prev = jnp.where(i > 0, carry_ref[:, c0:c0 + 128], 0.0)
            u = cg * hc
            carry_ref[:, c0:c0 + 128] = u[tq - 8:tq, :]
            full = jnp.concatenate([u_prev, u], axis=0)
            u1 = pltpu.roll(full, 1, 0)[8:]
            u2 = pltpu.roll(full, 2, 0)[8:]
            out = (u2 * wc_ref[0:1, c0:c0 + 128] + u1 * wc_ref[1:2, c0:c0 + 128]
                   + u * wc_ref[2:3, c0:c0 + 128])
            yc = bg * out
            r = lax.rsqrt(_group_mean(yc * yc, gmat) + EPS)
            y_ref[:, c0:c0 + 128] = (yc * r * g_ref[:, c0:c0 + 128]).astype(BF16)

    return _call(
        body, name="in_proj_conv", grid=(s // tq,),
        in_specs=[pl.BlockSpec((tq, D_MODEL), lambda i: (i, 0)),
                  pl.BlockSpec((4, D_MODEL, PROJ_SHARD), lambda i: (0, 0, 0)),
                  pl.BlockSpec((None, 8, CONV_WIDTH), lambda i: (l, 0, 0)),
                  pl.BlockSpec((None, 1, CONV_WIDTH), lambda i: (l, 0, 0))],
        out_specs=[pl.BlockSpec((tq, 3 * CONV_WIDTH), lambda i: (i, 0)),
                   pl.BlockSpec((tq, CONV_WIDTH), lambda i: (i, 0))],
        out_shape=[jax.ShapeDtypeStruct((s, 3 * CONV_WIDTH), BF16), jax.ShapeDtypeStruct((s, CONV_WIDTH), BF16)],
        scratch=[pltpu.VMEM((8, CONV_WIDTH), F32), pltpu.VMEM((tq, 3 * CONV_WIDTH), F32)],
        args=[h, win, wc, g3], comm=comm)


def _toeplitz_source():
    r_i = lax.broadcasted_iota(jnp.int32, (REL_PAD, TOEP), 0)
    m_i = lax.broadcasted_iota(jnp.int32, (REL_PAD, TOEP), 1)
    idx = jnp.clip((K_BAND - 1) - m_i, -REL_CLIP, REL_CLIP) + REL_CLIP
    return jnp.where(r_i == idx, 1.0, 0.0).astype(BF16)


def _bias_build(rbp, comm=None):
    n_layers = rbp.shape[0]

    def body(rb_ref, o_ref, t_ref):
        pmat = _toeplitz_source()
        hi, mid, lo = _split3(rb_ref[...])
        t_ref[...] = _dot(hi, pmat) + _dot(mid, pmat) + _dot(lo, pmat)
        shift = (CHUNK - 1) - lax.broadcasted_iota(jnp.int32, (CHUNK, TOEP), 0)
        kchunk = lax.broadcasted_iota(jnp.int32, (CHUNK, K_BAND), 1) >> 6
        for h in range(N_HEADS):
            b = jnp.broadcast_to(t_ref[pl.ds(h, 1), :], (CHUNK, TOEP))
            for bit in range(6):
                rolled = pltpu.roll(b, TOEP - (1 << bit), 1)
                b = jnp.where(((shift >> bit) & 1) == 1, rolled, b)
            for cq in range(Q_BLOCK // CHUNK):
                off = CHUNK * (Q_BLOCK // CHUNK - 1 - cq)
                band = pltpu.roll(b, TOEP - off, 1) if off else b
                dchunk = kchunk - cq
                in_band = jnp.where(dchunk >= 0, jnp.where(dchunk <= N_LEFT_CHUNKS, 1, 0), 0) == 1
                o_ref[h, CHUNK * cq:CHUNK * (cq + 1), :] = jnp.where(in_band, band[:, :K_BAND], NEG_INF)

    return _call(
        body, name="bias_build", grid=(n_layers,),
        in_specs=[pl.BlockSpec((None, N_HEADS, REL_PAD), lambda l: (l, 0, 0))],
        out_specs=[pl.BlockSpec((None, N_HEADS, Q_BLOCK, K_BAND), lambda l: (l, 0, 0, 0))],
        out_shape=[jax.ShapeDtypeStruct((n_layers, N_HEADS, Q_BLOCK, K_BAND), F32)],
        scratch=[pltpu.VMEM((N_HEADS, TOEP), F32)], args=[rbp], comm=comm)


def _bias_bwd(ds_sum):
    n_layers = ds_sum.shape[0]

    def body(ds_ref, o_ref, t_ref):
        pmat = _toeplitz_source()
        shift = (CHUNK - 1) - lax.broadcasted_iota(jnp.int32, (CHUNK, TOEP), 0)
        for h in range(N_HEADS):
            d = None
            for cq in range(Q_BLOCK // CHUNK):
                off = CHUNK * (Q_BLOCK // CHUNK - 1 - cq)
                part = jnp.concatenate([ds_ref[h, CHUNK * cq:CHUNK * (cq + 1), :],
                                        jnp.zeros((CHUNK, TOEP - K_BAND), F32)], axis=1)
                part = pltpu.roll(part, off, 1) if off else part
                d = part if d is None else d + part
            for bit in range(6):
                rolled = pltpu.roll(d, 1 << bit, 1)
                d = jnp.where(((shift >> bit) & 1) == 1, rolled, d)
            t_ref[pl.ds(h, 1), :] = jnp.sum(d, axis=0, keepdims=True)
        hi, mid, lo = _split3(t_ref[...])
        o_ref[...] = _dot_nt(hi, pmat) + _dot_nt(mid, pmat) + _dot_nt(lo, pmat)

    out, = _call(
        body, name="bias_bwd", grid=(n_layers,),
        in_specs=[pl.BlockSpec((None, N_HEADS, Q_BLOCK, K_BAND), lambda l: (l, 0, 0, 0))],
        out_specs=[pl.BlockSpec((None, N_HEADS, REL_PAD), lambda l: (l, 0, 0))],
        out_shape=[jax.ShapeDtypeStruct((n_layers, N_HEADS, REL_PAD), F32)],
        scratch=[pltpu.VMEM((N_HEADS, TOEP), F32)], args=[ds_sum])
    return out


def _attn_fwd(qkvp, biasm, g3, l, s, pad, comm=None):
    nb = s // Q_BLOCK
    qb0 = pad // Q_BLOCK
    scale = HEAD_DIM ** -0.5
    wide = 128 * ATTN_PAIRS

    def body(q_ref, k_ref, v_ref, b_ref, g_ref, o_ref, lse_ref, yn_ref):
        blk = pl.program_id(1)
        koff = pl.multiple_of(blk * Q_BLOCK + (pad - LEFT), Q_BLOCK)
        lane = lax.broadcasted_iota(jnp.int32, (1, 128), 1)
        kpos = lax.broadcasted_iota(jnp.int32, (1, K_BAND), 1) + (blk * Q_BLOCK - LEFT)
        kmask = jnp.where(kpos >= 0, 0.0, NEG_INF)
        gmat = _group_matrix()
        for pr in range(ATTN_PAIRS):
            ls = slice(128 * pr, 128 * (pr + 1))
            q = q_ref[:, ls]
            kb = k_ref[pl.ds(koff, K_BAND), ls]
            vb = v_ref[pl.ds(koff, K_BAND), ls]
            outs, lses = [], []
            for hh in range(2):
                in_head = (lane >> 6) == hh
                qm = jnp.where(in_head, q, jnp.zeros_like(q)) * jnp.asarray(scale, BF16)
                sc = _dot_nt(qm, kb) + b_ref[2 * pr + hh] + kmask
                m = jnp.max(sc, axis=1, keepdims=True)
                e = jnp.exp(sc - m)
                den = jnp.sum(e, axis=1, keepdims=True)
                outs.append(_dot(e.astype(BF16), vb) * (1.0 / den))
                lses.append(m + jnp.log(den))
            first = lane < HEAD_DIM
            o = jnp.where(first, outs[0], outs[1])
            o_ref[:, ls] = o
            lse_ref[:, ls] = jnp.where(first, lses[0], lses[1])
            r = lax.rsqrt(_group_mean(o * o, gmat) + EPS)
            yn_ref[:, ls] = (o * r * g_ref[:, ls]).astype(BF16)

    blk_out = pl.BlockSpec((Q_BLOCK, wide), lambda p, b: (b, p))
    n_grp = ATTN_WIDTH // wide
    return _call(
        body, name="attn_fwd", grid=(n_grp, nb),
        in_specs=[pl.BlockSpec((Q_BLOCK, wide), lambda p, b: (qb0 + b, p)),
                  pl.BlockSpec((s + pad, wide), lambda p, b: (0, n_grp + p)),
                  pl.BlockSpec((s + pad, wide), lambda p, b: (0, 2 * n_grp + p)),
                  pl.BlockSpec((None, 2 * ATTN_PAIRS, Q_BLOCK, K_BAND), lambda p, b: (l, p, 0, 0)),
                  pl.BlockSpec((None, 1, wide), lambda p, b: (l, 0, p))],
        out_specs=[blk_out, blk_out, blk_out],
        out_shape=[jax.ShapeDtypeStruct((s, ATTN_WIDTH), F32),
                   jax.ShapeDtypeStruct((s, ATTN_WIDTH), F32),
                   jax.ShapeDtypeStruct((s, ATTN_WIDTH), BF16)],
        args=[qkvp, qkvp, qkvp, biasm, g3], comm=comm)


def _out_proj_fwd(ync, yna, wout, x, g_post3, g_next3, l, s, tm):
    half = D_MODEL // 2

    def body(a1_ref, a2_ref, w_ref, x_ref, gp_ref, gn_ref, z_ref, xm_ref, h_ref):
        for rs in _row_subtiles(tm, SUB_ROWS):
            z = _dot(a1_ref[rs, :], w_ref[0:half, :]) + _dot(a2_ref[rs, :], w_ref[half:D_MODEL, :])
            z_ref[rs, :] = z.astype(BF16)
            xm = x_ref[rs, :] + z * _rstd(z) * gp_ref[...]
            xm_ref[rs, :] = xm
            h_ref[rs, :] = (xm * _rstd(xm) * gn_ref[...]).astype(BF16)

    row = pl.BlockSpec((tm, D_MODEL), lambda i: (i, 0))
    gain = pl.BlockSpec((None, 1, D_MODEL), lambda i: (l, 0, 0))
    return _call(
        body, name="out_proj_fwd", grid=(s // tm,),
        in_specs=[pl.BlockSpec((tm, half), lambda i: (i, 0)), pl.BlockSpec((tm, half), lambda i: (i, 0)),
                  pl.BlockSpec((D_MODEL, D_MODEL), lambda i: (0, 0)), row, gain, gain],
        out_specs=[row, row, row],
        out_shape=[jax.ShapeDtypeStruct((s, D_MODEL), BF16), jax.ShapeDtypeStruct((s, D_MODEL), F32),
                   jax.ShapeDtypeStruct((s, D_MODEL), BF16)],
        args=[ync, yna, wout, x, g_post3, g_next3])


def _ffn_in_fwd(h2, wfin4, s, tm, comm=None):
    def body(h_ref, wg_ref, wu_ref, gu_ref, act_ref):
        h = h_ref[...]
        for cs in _COL_SUBTILES:
            gate = _dot_nt(h, wg_ref[cs, :])
            up = _dot_nt(h, wu_ref[cs, :])
            gu_ref[0, :, cs] = gate.astype(BF16)
            gu_ref[1, :, cs] = up.astype(BF16)
            act_ref[:, cs] = (gate * (1.0 / (1.0 + jnp.exp(-gate))) * up).astype(BF16)

    return _call(
        body, name="ffn_in_fwd", grid=(2, s // tm),
        in_specs=[pl.BlockSpec((tm, D_MODEL), lambda b, i: (i, 0)),
                  pl.BlockSpec((None, FF_PAIR, D_MODEL), lambda b, i: (b, 0, 0)),
                  pl.BlockSpec((None, FF_PAIR, D_MODEL), lambda b, i: (2 + b, 0, 0))],
        out_specs=[pl.BlockSpec((2, tm, FF_PAIR), lambda b, i: (0, i, b)),
                   pl.BlockSpec((tm, FF_PAIR), lambda b, i: (i, b))],
        out_shape=[jax.ShapeDtypeStruct((2, s, D_FF), BF16), jax.ShapeDtypeStruct((s, D_FF), BF16)],
        args=[h2, wfin4, wfin4], comm=comm)


def _ffn_out_fwd(act, wfo, xm, g_post3, g_next3, l, l_next, s, tm, comm=None):
    def body(a_ref, w_ref, x_ref, gp_ref, gn_ref, f_ref, xo_ref, h_ref):
        for rs in _row_subtiles(tm, SUB_ROWS):
            f = _dot(a_ref[rs, :], w_ref[...])
            f_ref[rs, :] = f.astype(BF16)
            xo = x_ref[rs, :] + f * _rstd(f) * gp_ref[...]
            xo_ref[rs, :] = xo
            h_ref[rs, :] = (xo * _rstd(xo) * gn_ref[...]).astype(BF16)

    row = pl.BlockSpec((tm, D_MODEL), lambda i: (i, 0))
    return _call(
        body, name="ffn_out_fwd", grid=(s // tm,),
        in_specs=[pl.BlockSpec((tm, D_FF), lambda i: (i, 0)),
                  pl.BlockSpec((D_FF, D_MODEL), lambda i: (0, 0), pipeline_mode=pl.Buffered(1)), row,
                  pl.BlockSpec((None, 1, D_MODEL), lambda i: (l, 0, 0)),
                  pl.BlockSpec((None, 1, D_MODEL), lambda i: (l_next, 0, 0))],
        out_specs=[row, row, row],
        out_shape=[jax.ShapeDtypeStruct((s, D_MODEL), BF16), jax.ShapeDtypeStruct((s, D_MODEL), F32),
                   jax.ShapeDtypeStruct((s, D_MODEL), BF16)],
        args=[act, wfo, xm, g_post3, g_next3], comm=comm)


def _ffn_out_loss(act, wfo, xm, g_post3, target, l, s, tm):
    def body(a_ref, w_ref, x_ref, gp_ref, t_ref, dx_ref, sq_ref, df_ref, dg_ref):
        _zero_first((sq_ref, dg_ref), pl.program_id(0) == 0)
        for rs in _row_subtiles(tm, SUB_ROWS):
            f = _dot(a_ref[rs, :], w_ref[...])
            gain = gp_ref[...]
            err = x_ref[rs, :] + f * _rstd(f) * gain - t_ref[rs, :]
            dx = err * (1.0 / D_MODEL)
            dx_ref[rs, :] = dx
            df, dyn = _norm_bwd_rows(f, gain, dx)
            df_ref[rs, :] = df.astype(BF16)
            _add_cols(dg_ref, dyn)
            cs = jnp.sum(err * err, axis=0, keepdims=True)
            part = cs[:, 0:128]
            for k in range(1, D_MODEL // 128):
                part = part + cs[:, 128 * k:128 * (k + 1)]
            sq_ref[0:1, :] += part

    row = pl.BlockSpec((tm, D_MODEL), lambda i: (i, 0))
    return _call(
        body, name="ffn_out_loss", grid=(s // tm,),
        in_specs=[pl.BlockSpec((tm, D_FF), lambda i: (i, 0)),
                  pl.BlockSpec((D_FF, D_MODEL), lambda i: (0, 0), pipeline_mode=pl.Buffered(1)), row,
                  pl.BlockSpec((None, 1, D_MODEL), lambda i: (l, 0, 0)), row],
        out_specs=[row, pl.BlockSpec((8, 128), lambda i: (0, 0)), row, pl.BlockSpec((8, D_MODEL), lambda i: (0, 0))],
        out_shape=[jax.ShapeDtypeStruct((s, D_MODEL), F32), jax.ShapeDtypeStruct((8, 128), F32),
                   jax.ShapeDtypeStruct((s, D_MODEL), BF16), jax.ShapeDtypeStruct((8, D_MODEL), F32)],
        args=[act, wfo, xm, g_post3, target])


def _norm_bwd_rows(v, g, dy):
    r = _rstd(v)
    vn = v * r
    gd = dy * g
    dv = r * (gd - vn * jnp.mean(vn * gd, axis=-1, keepdims=True))
    return dv, dy * vn


def _zero_first(refs, first):
    @pl.when(first)
    def _():
        for ref in refs:
            ref[...] = jnp.zeros(ref.shape, F32)


def _add_cols(ref, val):
    ref[0:1, :] += jnp.sum(val, axis=0, keepdims=True)


def _accum_cols(ref, val, first):
    _zero_first((ref,), first)
    _add_cols(ref, val)


def _row_subtiles(rows, sub):
    sub = min(sub, rows)
    return [slice(r, r + sub) for r in range(0, rows, sub)]


def _ffn_out_bwd(df, wfo, gu, act, s, tm, comm=None):
    nm = s // tm

    def body(df_ref, w_ref, gu_ref, act_ref, dgu_ref, dw_ref, acc_ref):
        i = pl.program_id(1)
        df = df_ref[...]
        _zero_first((acc_ref,), i == 0)
        acc_ref[...] += _dot_tn(act_ref[...], df)

        @pl.when(i == nm - 1)
        def _():
            dw_ref[...] = acc_ref[...].astype(BF16)

        for cs in _COL_SUBTILES:
            da = _dot_nt(df, w_ref[cs, :])
            g = gu_ref[0, :, cs].astype(F32)
            u = gu_ref[1, :, cs].astype(F32)
            sg = 1.0 / (1.0 + jnp.exp(-g))
            dgu_ref[0, :, cs] = (da * u * (sg * (1.0 + g * (1.0 - sg)))).astype(BF16)
            dgu_ref[1, :, cs] = (da * (g * sg)).astype(BF16)

    blk = pl.BlockSpec((2, tm, FF_PAIR), lambda b, i: (0, i, b))
    wblk = pl.BlockSpec((FF_PAIR, D_MODEL), lambda b, i: (b, 0))
    return _call(
        body, name="ffn_out_bwd", grid=(2, nm),
        in_specs=[pl.BlockSpec((tm, D_MODEL), lambda b, i: (i, 0)), wblk, blk,
                  pl.BlockSpec((tm, FF_PAIR), lambda b, i: (i, b))],
        out_specs=[blk, wblk],
        out_shape=[jax.ShapeDtypeStruct((2, s, D_FF), BF16), jax.ShapeDtypeStruct((D_FF, D_MODEL), BF16)],
        scratch=[pltpu.VMEM((FF_PAIR, D_MODEL), F32)],
        args=[df, wfo, gu, act], comm=comm)


def _dw_ffn_in(h2, dgu, s):
    def body(a_ref, b_ref, o_ref):
        o_ref[...] = _dot_tn(b_ref[...], a_ref[...]).astype(BF16)

    out, = _call(
        body, name="dw_ffn_in", grid=(4,),
        in_specs=[pl.BlockSpec((s, D_MODEL), lambda n: (0, 0), pipeline_mode=pl.Buffered(1)),
                  pl.BlockSpec((None, s, FF_PAIR), lambda n: (n // 2, 0, n % 2))],
        out_specs=[pl.BlockSpec((None, FF_PAIR, D_MODEL), lambda n: (n, 0, 0))],
        out_shape=[jax.ShapeDtypeStruct((4, FF_PAIR, D_MODEL), BF16)], args=[h2, dgu])
    return out


def _ffn_in_bwd(dgu, wfin, xm, g_pre3, dres, z, g_post3, l, s, tm, comm=None):
    def body(d_ref, w_ref, xm_ref, gp_ref, dres_ref, z_ref, gq_ref, dxm_ref, dz_ref, dgp_ref, dgq_ref):
        _zero_first((dgp_ref, dgq_ref), pl.program_id(0) == 0)
        for rs in _row_subtiles(tm, SUB_ROWS):
            dh = _dot(d_ref[0, rs, :], w_ref[0:D_FF, :]) + _dot(d_ref[1, rs, :], w_ref[D_FF:2 * D_FF, :])
            dx, dyn = _norm_bwd_rows(xm_ref[rs, :], gp_ref[...], dh)
            dxm = dres_ref[rs, :] + dx
            dxm_ref[rs, :] = dxm
            _add_cols(dgp_ref, dyn)
            dz, dyn2 = _norm_bwd_rows(z_ref[rs, :].astype(F32), gq_ref[...], dxm)
            dz_ref[rs, :] = dz.astype(BF16)
            _add_cols(dgq_ref, dyn2)

    row = pl.BlockSpec((tm, D_MODEL), lambda i: (i, 0))
    gain = pl.BlockSpec((None, 1, D_MODEL), lambda i: (l, 0, 0))
    dgs = pl.BlockSpec((8, D_MODEL), lambda i: (0, 0))
    return _call(
        body, name="ffn_in_bwd", grid=(s // tm,),
        in_specs=[pl.BlockSpec((2, tm, D_FF), lambda i: (0, i, 0)),
                  pl.BlockSpec((2 * D_FF, D_MODEL), lambda i: (0, 0), pipeline_mode=pl.Buffered(1)),
                  row, gain, row, row, gain],
        out_specs=[row, row, dgs, dgs],
        out_shape=[jax.ShapeDtypeStruct((s, D_MODEL), F32), jax.ShapeDtypeStruct((s, D_MODEL), BF16),
                   jax.ShapeDtypeStruct((8, D_MODEL), F32), jax.ShapeDtypeStruct((8, D_MODEL), F32)],
        args=[dgu, wfin, xm, g_pre3, dres, z, g_post3], comm=comm)


def _out_proj_bwd(dz, wout, o, g3, ync, yna, l, s, tm):
    nm = s // tm
    half = D_MODEL // 2

    def body(dz_ref, w_ref, o_ref, g_ref, a1_ref, a2_ref, dyc_ref, do_ref, dg_ref, dw_ref, acc_ref):
        i = pl.program_id(0)
        gmat = _group_matrix()
        _zero_first((dg_ref, acc_ref), i == 0)
        dzv = dz_ref[...]
        acc_ref[0:half, :] += _dot_tn(a1_ref[...], dzv)
        acc_ref[half:D_MODEL, :] += _dot_tn(a2_ref[...], dzv)

        @pl.when(i == nm - 1)
        def _():
            dw_ref[...] = acc_ref[...].astype(BF16)

        for rs in _row_subtiles(tm, SUB_ROWS):
            dy = _dot_nt(dz_ref[rs, :], w_ref[...])
            dyc_ref[rs, :] = dy[:, 0:CONV_WIDTH]
            for j in range(ATTN_WIDTH // 128):
                c0 = 128 * j
                ov = o_ref[rs, c0:c0 + 128]
                dyn = dy[:, CONV_WIDTH + c0:CONV_WIDTH + c0 + 128]
                r = lax.rsqrt(_group_mean(ov * ov, gmat) + EPS)
                on = ov * r
                gd = dyn * g_ref[:, c0:c0 + 128]
                do_ref[rs, c0:c0 + 128] = r * (gd - on * _group_mean(on * gd, gmat))
                dg_ref[0:1, c0:c0 + 128] += jnp.sum(dyn * on, axis=0, keepdims=True)

    halfrow = pl.BlockSpec((tm, ATTN_WIDTH), lambda i: (i, 0))
    return _call(
        body, name="out_proj_bwd", grid=(nm,),
        in_specs=[pl.BlockSpec((tm, D_MODEL), lambda i: (i, 0)),
                  pl.BlockSpec((D_MODEL, D_MODEL), lambda i: (0, 0)), halfrow,
                  pl.BlockSpec((None, 1, ATTN_WIDTH), lambda i: (l, 0, 0)), halfrow, halfrow],
        out_specs=[halfrow, halfrow, pl.BlockSpec((8, ATTN_WIDTH), lambda i: (0, 0)),
                   pl.BlockSpec((D_MODEL, D_MODEL), lambda i: (0, 0))],
        out_shape=[jax.ShapeDtypeStruct((s, CONV_WIDTH), F32), jax.ShapeDtypeStruct((s, ATTN_WIDTH), F32),
                   jax.ShapeDtypeStruct((8, ATTN_WIDTH), F32), jax.ShapeDtypeStruct((D_MODEL, D_MODEL), BF16)],
        scratch=[pltpu.VMEM((D_MODEL, D_MODEL), F32)],
        args=[dz, wout, o, g3, ync, yna])


def _conv_bwd(pc, dyc, wc, g3, dq, dk, dv, l, s, tr):
    hb = tr // 8
    nt = s // tr
    ext = tr + 16
    last_hb = s // 8 - 1

    def body(pc_ref, prev_ref, next_ref, dy_ref, dyn_ref, wc_ref, g_ref, dq_ref, dk_ref, dv_ref,
             dpc_ref, dw_ref, dg_ref):
        i = pl.program_id(0)
        for part, ref in enumerate((dq_ref, dk_ref, dv_ref)):
            c = 3 * CONV_WIDTH + ATTN_WIDTH * part
            dpc_ref[:, c:c + ATTN_WIDTH] = ref[...]
        gmat = _group_matrix()
        row = lax.broadcasted_iota(jnp.int32, (ext, 128), 0) + (i * tr - 8)
        inside = jnp.where(row >= 0, jnp.where(row < s, 1, 0), 0) == 1

        @pl.when(i == 0)
        def _():
            dw_ref[...] = jnp.zeros(dw_ref.shape, F32)
            dg_ref[...] = jnp.zeros(dg_ref.shape, F32)

        def extend(ref_prev, ref_mid, ref_next, c):
            if ref_prev is None:
                before = jnp.zeros((8, 128), F32)
            else:
                before = ref_prev[:, c:c + 128].astype(F32)[ref_prev.shape[0] - 8:]
            after = ref_next[:, c:c + 128].astype(F32)[0:8]
            return jnp.concatenate([before, ref_mid[:, c:c + 128].astype(F32), after], axis=0)

        for j in range(CONV_WIDTH // 128):
            c0, c1, c2 = 128 * j, CONV_WIDTH + 128 * j, 2 * CONV_WIDTH + 128 * j
            hc = extend(prev_ref, pc_ref, next_ref, c0)
            bg = extend(prev_ref, pc_ref, next_ref, c1)
            cg = extend(prev_ref, pc_ref, next_ref, c2)
            dyn = extend(None, dy_ref, dyn_ref, c0)
            w0, w1, w2 = (wc_ref[0:1, c0:c0 + 128], wc_ref[1:2, c0:c0 + 128], wc_ref[2:3, c0:c0 + 128])
            gain = g_ref[:, c0:c0 + 128]
            u = jnp.where(inside, cg * hc, 0.0)
            u1 = pltpu.roll(u, 1, 0)
            u2 = pltpu.roll(u, 2, 0)
            out = u2 * w0 + u1 * w1 + u * w2
            yc = bg * out
            r = lax.rsqrt(_group_mean(yc * yc, gmat) + EPS)
            ycn = yc * r
            gd = dyn * gain
            dyc = r * (gd - ycn * _group_mean(ycn * gd, gmat))
            dout = jnp.where(inside, dyc * bg, 0.0)
            du = dout * w2 + pltpu.roll(dout, ext - 1, 0) * w1 + pltpu.roll(dout, ext - 2, 0) * w0
            sl = slice(8, 8 + tr)
            dpc_ref[:, c0:c0 + 128] = (du[sl] * cg[sl]).astype(BF16)
            dpc_ref[:, c1:c1 + 128] = (dyc[sl] * out[sl]).astype(BF16)
            dpc_ref[:, c2:c2 + 128] = (du[sl] * hc[sl]).astype(BF16)
            dw_ref[0:1, c0:c0 + 128] += jnp.sum(dout[sl] * u2[sl], axis=0, keepdims=True)
            dw_ref[1:2, c0:c0 + 128] += jnp.sum(dout[sl] * u1[sl], axis=0, keepdims=True)
            dw_ref[2:3, c0:c0 + 128] += jnp.sum(dout[sl] * u[sl], axis=0, keepdims=True)
            dg_ref[0:1, c0:c0 + 128] += jnp.sum(dyn[sl] * ycn[sl], axis=0, keepdims=True)

    wide = 3 * CONV_WIDTH
    return _call(
        body, name="conv_bwd", grid=(nt,),
        in_specs=[pl.BlockSpec((tr, wide), lambda i: (i, 0)),
                  pl.BlockSpec((16, wide), lambda i: (jnp.maximum(i * (hb // 2) - 1, 0), 0)),
                  pl.BlockSpec((16, wide), lambda i: (jnp.minimum((i + 1) * (hb // 2), last_hb // 2), 0)),
                  pl.BlockSpec((tr, CONV_WIDTH), lambda i: (i, 0)),
                  pl.BlockSpec((8, CONV_WIDTH), lambda i: (jnp.minimum((i + 1) * hb, last_hb), 0)),
                  pl.BlockSpec((None, 8, CONV_WIDTH), lambda i: (l, 0, 0)),
                  pl.BlockSpec((None, 1, CONV_WIDTH), lambda i: (l, 0, 0)),
                  pl.BlockSpec((tr, ATTN_WIDTH), lambda i: (i, 0)),
                  pl.BlockSpec((tr, ATTN_WIDTH), lambda i: (i, 0)),
                  pl.BlockSpec((tr, ATTN_WIDTH), lambda i: (i, 0))],
        out_specs=[pl.BlockSpec((tr, PROJ_WIDTH), lambda i: (i, 0)),
                   pl.BlockSpec((8, CONV_WIDTH), lambda i: (0, 0)),
                   pl.BlockSpec((8, CONV_WIDTH), lambda i: (0, 0))],
        out_shape=[jax.ShapeDtypeStruct((s, PROJ_WIDTH), BF16), jax.ShapeDtypeStruct((8, CONV_WIDTH), F32),
                   jax.ShapeDtypeStruct((8, CONV_WIDTH), F32)],
        args=[pc, pc, pc, dyc, dyc, wc, g3, dq, dk, dv])


def _attn_bwd(qkvp, biasm, o, lse, do, l, s, pad, comm=None):
    nb = s // Q_BLOCK
    qb0 = pad // Q_BLOCK
    scale = HEAD_DIM ** -0.5
    wide = 128 * ATTN_PAIRS

    def body(q_ref, k_ref, v_ref, b_ref, o_ref, lse_ref, do_ref,
             dq_ref, dk_ref, dv_ref, ds_ref, dk_acc, dv_acc):
        blk = pl.program_id(1)

        @pl.when(blk == 0)
        def _():
            dk_acc[...] = jnp.zeros(dk_acc.shape, F32)
            dv_acc[...] = jnp.zeros(dv_acc.shape, F32)
            ds_ref[...] = jnp.zeros(ds_ref.shape, F32)

        koff = pl.multiple_of(blk * Q_BLOCK + (pad - LEFT), Q_BLOCK)
        lane = lax.broadcasted_iota(jnp.int32, (1, 128), 1)
        kpos = lax.broadcasted_iota(jnp.int32, (1, K_BAND), 1) + (blk * Q_BLOCK - LEFT)
        kmask = jnp.where(kpos >= 0, 0.0, NEG_INF)
        for pr in range(ATTN_PAIRS):
            ls = slice(128 * pr, 128 * (pr + 1))
            q = q_ref[:, ls]
            kb = k_ref[pl.ds(koff, K_BAND), ls]
            vb = v_ref[pl.ds(koff, K_BAND), ls]
            dov = do_ref[:, ls]
            lse_v = lse_ref[:, ls]
            prod = dov * o_ref[:, ls]
            dq_parts = []
            dk_new = jnp.zeros((K_BAND, 128), F32)
            dv_new = jnp.zeros((K_BAND, 128), F32)
            for hh in range(2):
                in_head = (lane >> 6) == hh
                qm = jnp.where(in_head, q, jnp.zeros_like(q)) * jnp.asarray(scale, BF16)
                dom = jnp.where(in_head, dov, 0.0).astype(BF16)
                delta = jnp.sum(jnp.where(in_head, prod, 0.0), axis=1, keepdims=True)
                lse_h = lse_v[:, HEAD_DIM * hh:HEAD_DIM * hh + 1]
                sc = _dot_nt(qm, kb) + b_ref[2 * pr + hh] + kmask
                p = jnp.exp(sc - lse_h)
                dp = _dot_nt(dom, vb)
                ds = p * (dp - delta)
                ds_ref[2 * pr + hh] += ds
                dsb = ds.astype(BF16)
                dq_parts.append(_dot(dsb, kb) * scale)
                dk_new = dk_new + _dot_tn(dsb, qm)
                dv_new = dv_new + _dot_tn(p.astype(BF16), dom)
            dq_ref[:, ls] = jnp.where(lane < HEAD_DIM, dq_parts[0], dq_parts[1]).astype(BF16)
            dk_acc[pl.ds(koff, K_BAND), ls] += dk_new
            dv_acc[pl.ds(koff, K_BAND), ls] += dv_new

        @pl.when(blk == nb - 1)
        def _():
            dk_ref[...] = dk_acc[pad:pad + s, :].astype(BF16)
            dv_ref[...] = dv_acc[pad:pad + s, :].astype(BF16)

    n_grp = ATTN_WIDTH // wide
    qblk = pl.BlockSpec((Q_BLOCK, wide), lambda p, b: (b, p))
    col = pl.BlockSpec((s, wide), lambda p, b: (0, p))
    shp = jax.ShapeDtypeStruct((s, ATTN_WIDTH), BF16)
    return _call(
        body, name="attn_bwd", grid=(n_grp, nb),
        in_specs=[pl.BlockSpec((Q_BLOCK, wide), lambda p, b: (qb0 + b, p)),
                  pl.BlockSpec((s + pad, wide), lambda p, b: (0, n_grp + p)),
                  pl.BlockSpec((s + pad, wide), lambda p, b: (0, 2 * n_grp + p)),
                  pl.BlockSpec((None, 2 * ATTN_PAIRS, Q_BLOCK, K_BAND), lambda p, b: (l, p, 0, 0)),
                  qblk, qblk, qblk],
        out_specs=[qblk, col, col, pl.BlockSpec((2 * ATTN_PAIRS, Q_BLOCK, K_BAND), lambda p, b: (p, 0, 0))],
        out_shape=[shp, shp, shp, jax.ShapeDtypeStruct((N_HEADS, Q_BLOCK, K_BAND), F32)],
        scratch=[pltpu.VMEM((s + pad, wide), F32), pltpu.VMEM((s + pad, wide), F32)],
        args=[qkvp, qkvp, qkvp, biasm, o, lse, do], comm=comm)


def _dw_in(h, dproj, s):
    def body(a_ref, b_ref, o_ref):
        acc = _dot_tn(a_ref[...], b_ref[...])
        o_ref[0] = acc[:, 0:PROJ_SHARD].astype(BF16)
        o_ref[1] = acc[:, PROJ_SHARD:2 * PROJ_SHARD].astype(BF16)

    out, = _call(
        body, name="dw_in", grid=(4,),
        in_specs=[pl.BlockSpec((s, D_MODEL), lambda n: (0, 0)),
                  pl.BlockSpec((s, 2 * PROJ_SHARD), lambda n: (0, n))],
        out_specs=[pl.BlockSpec((2, D_MODEL, PROJ_SHARD), lambda n: (n, 0, 0))],
        out_shape=[jax.ShapeDtypeStruct((N_DEV, D_MODEL, PROJ_SHARD), BF16)], args=[h, dproj])
    return out


def _in_proj_bwd(dproj, win, x, g3, dres, l, s, tm, f_prev=None, g_post3=None, comm=None):
    chain = f_prev is not None

    def body(d_ref, w_ref, x_ref, g_ref, dres_ref, *rest):
        if chain:
            f_ref, gq_ref, dx_ref, dg_ref, df_ref, dgq_ref = rest
            _zero_first((dg_ref, dgq_ref), pl.program_id(0) == 0)
        else:
            dx_ref, dg_ref = rest
            _zero_first((dg_ref,), pl.program_id(0) == 0)
        w = jnp.concatenate([w_ref[j] for j in range(N_DEV)], axis=1)
        for rs in _row_subtiles(tm, SUB_ROWS):
            dh = _dot_nt(d_ref[rs, :], w)
            dx, dyn = _norm_bwd_rows(x_ref[rs, :], g_ref[...], dh)
            dx = dres_ref[rs, :] + dx
            dx_ref[rs, :] = dx
            _add_cols(dg_ref, dyn)
            if chain:
                df, dyn2 = _norm_bwd_rows(f_ref[rs, :].astype(F32), gq_ref[...], dx)
                df_ref[rs, :] = df.astype(BF16)
                _add_cols(dgq_ref, dyn2)

    row = pl.BlockSpec((tm, D_MODEL), lambda i: (i, 0))
    dgs = pl.BlockSpec((8, D_MODEL), lambda i: (0, 0))
    in_specs = [pl.BlockSpec((tm, PROJ_WIDTH), lambda i: (i, 0)),
                pl.BlockSpec((N_DEV, D_MODEL, PROJ_SHARD), lambda i: (0, 0, 0), pipeline_mode=pl.Buffered(1)),
                row, pl.BlockSpec((None, 1, D_MODEL), lambda i: (l, 0, 0)), row]
    out_specs = [row, dgs]
    out_shape = [jax.ShapeDtypeStruct((s, D_MODEL), F32), jax.ShapeDtypeStruct((8, D_MODEL), F32)]
    args = [dproj, win, x, g3, dres]
    if chain:
        in_specs += [row, pl.BlockSpec((None, 1, D_MODEL), lambda i: (l - 1, 0, 0))]
        out_specs += [row, dgs]
        out_shape += [jax.ShapeDtypeStruct((s, D_MODEL), BF16), jax.ShapeDtypeStruct((8, D_MODEL), F32)]
        args += [f_prev, g_post3]
    return _call(body, name="in_proj_bwd", grid=(s // tm,), in_specs=in_specs, out_specs=out_specs,
                 out_shape=out_shape, args=args, comm=comm)


def _adamw(name, w, m, v, lands, owns=None, me=None):
    groups, rows, cols = w.shape
    assert len(lands) == groups
    n_part = lands[0].shape[0]
    tr = _row_tile(rows, tuple(c for c in (512, 352, 256, 176, 128, 64, 32, 16, 8) if c * cols <= 256 * 1024))
    c1 = 1.0 - ADAM_B1 ** ADAM_STEP
    c2 = 1.0 - ADAM_B2 ** ADAM_STEP
    n_own = groups if owns is not None else 0

    def body(*refs):
        if n_own:
            me_ref, refs = refs[0], refs[1:]
        w_ref, m_ref, v_ref = refs[:3]
        land_refs = refs[3:3 + groups]
        own_refs = refs[3 + groups:3 + groups + n_own]
        g_ref, d_ref, nm_ref, nv_ref = refs[3 + groups + n_own:]
        grp = pl.program_id(0)
        for gi in range(groups):
            @pl.when(grp == gi)
            def _():
                l_ref = land_refs[gi]
                g = None
                for p in range(n_part):
                    part = l_ref[p].astype(F32)
                    if n_own:
                        part = jnp.where(me_ref[0] == p, own_refs[gi][...].astype(F32), part)
                    g = part if g is None else g + part
                g_ref[...] = g
                m1 = ADAM_B1 * m_ref[...] + (1.0 - ADAM_B1) * g
                v1 = ADAM_B2 * v_ref[...] + (1.0 - ADAM_B2) * (g * g)
                nm_ref[...] = m1
                nv_ref[...] = v1
                d_ref[...] = -ADAM_LR * ((m1 / c1) / (jnp.sqrt(v1 / c2) + ADAM_EPS) + ADAM_WD * w_ref[...])

    blk = pl.BlockSpec((None, tr, cols), lambda g, i, *_: (g, i, 0))
    shp = jax.ShapeDtypeStruct((groups, rows, cols), F32)

    def land_spec(gi):
        return pl.BlockSpec((n_part, tr, cols), lambda g, i, *_: (0, jnp.where(g == gi, i, 0), 0))

    def own_spec(gi):
        if owns[gi].ndim == 3:
            return pl.BlockSpec((None, tr, cols), lambda g, i, me_ref: (me_ref[0], jnp.where(g == gi, i, 0), 0))
        return pl.BlockSpec((tr, cols), lambda g, i, me_ref: (jnp.where(g == gi, i, 0), 0))

    in_specs = [blk, blk, blk] + [land_spec(gi) for gi in range(groups)] + [own_spec(gi) for gi in range(n_own)]
    args = [w, m, v] + list(lands) + (list(owns) if n_own else [])
    if not n_own:
        return _call(body, name=name, grid=(groups, rows // tr), in_specs=in_specs,
                     out_specs=[blk, blk, blk, blk], out_shape=[shp, shp, shp, shp], args=args)
    return pl.pallas_call(
        body, name=name,
        grid_spec=pltpu.PrefetchScalarGridSpec(
            num_scalar_prefetch=1, grid=(groups, rows // tr), in_specs=in_specs, out_specs=[blk, blk, blk, blk]),
        out_shape=[shp, shp, shp, shp],
        compiler_params=pltpu.CompilerParams(dimension_semantics=("arbitrary", "arbitrary"),
                                             vmem_limit_bytes=VMEM_LIMIT),
    )(me, *args)


def _pack_small(rel, gco, gao, gpm, gqm, gpf, gqf):
    n_layers = rel.shape[0]
    relp = jnp.pad(rel, ((0, 0), (0, 0), (0, REL_PAD - rel.shape[2])))
    parts = [relp.reshape(n_layers * N_HEADS * REL_PAD // 128, 128)]
    parts += [a.reshape(-1, 128) for a in (gco, gao, gpm, gqm, gpf, gqf)]
    return jnp.concatenate(parts, axis=0)


def _pack_small_grads(d_rel, parts):
    n_layers = len(d_rel)
    keys = ("gco", "gao", "gpm", "gqm", "gpf", "gqf")
    arrays = list(d_rel) + [parts[k][l] for k in keys for l in range(n_layers)] + list(parts["wc"])
    rows = 0
    plan = []
    for l in range(n_layers):
        for h in range(N_HEADS):
            for t in range(REL_PAD // 128):
                plan.append((l, (0, h), t, rows))
                rows += 1
    for ki, k in enumerate(keys):
        for l in range(n_layers):
            for t in range(parts[k][l].shape[1] // 128):
                plan.append((n_layers * (1 + ki) + l, (0,), t, rows))
                rows += 1
    for l in range(n_layers):
        for tap in range(3):
            for t in range(CONV_WIDTH // 128):
                plan.append((n_layers * (1 + len(keys)) + l, (tap,), t, rows))
                rows += 1
    total = rows + (-rows) % 8

    def body(*refs):
        o_ref = refs[-1]
        if total > rows:
            o_ref[rows:total, :] = jnp.zeros((total - rows, 128), F32)
        for op, idx, t, dst in plan:
            lanes = slice(128 * t, 128 * (t + 1))
            if len(idx) == 2:
                o_ref[dst:dst + 1, :] = refs[op][idx[0], idx[1]:idx[1] + 1, lanes]
            else:
                o_ref[dst:dst + 1, :] = refs[op][idx[0]:idx[0] + 1, lanes]

    vmem = pl.BlockSpec(memory_space=pltpu.VMEM)
    out, = _call(body, name="pack_small_grads", grid=(), in_specs=[vmem] * len(arrays), out_specs=[vmem],
                 out_shape=[jax.ShapeDtypeStruct((total, 128), F32)], args=arrays)
    return out


def _unpack_small(p, n_layers):
    n_rel = n_layers * N_HEADS * REL_PAD // 128
    rel = p[:n_rel].reshape(n_layers, N_HEADS, REL_PAD)[:, :, :2 * REL_CLIP + 1]
    outs = [rel]
    r0 = n_rel
    for width in (CONV_WIDTH, ATTN_WIDTH, D_MODEL, D_MODEL, D_MODEL, D_MODEL):
        nr = n_layers * width // 128
        outs.append(p[r0:r0 + nr].reshape(n_layers, width))
        r0 += nr
    return outs


def kernel(x, w_in, w_conv, rel_bias, g_conv_out, g_attn_out, w_out, g_pre_mix, g_post_mix, g_pre_ffn, g_post_ffn, w_ffn_in, w_ffn_out, loss_target, m_w_in, m_w_conv, m_rel_bias, m_g_conv_out, m_g_attn_out, m_w_out, m_g_pre_mix, m_g_post_mix, m_g_pre_ffn, m_g_post_ffn, m_w_ffn_in, m_w_ffn_out, v_w_in, v_w_conv, v_rel_bias, v_g_conv_out, v_g_attn_out, v_w_out, v_g_pre_mix, v_g_post_mix, v_g_pre_ffn, v_g_post_ffn, v_w_ffn_in, v_w_ffn_out):
    n_layers = w_in.shape[0]
    s = x.shape[1]
    assert x.shape == (1, s, D_MODEL) and s % 1024 == 0
    assert w_in.shape == (n_layers, D_MODEL, PROJ_SHARD) and w_ffn_in.shape == (n_layers, D_MODEL, FF_SHARD)
    tm = 512
    tq = 1024 if s >= 2048 else 512
    tf = min(1024, s)
    x0 = x.reshape(s, D_MODEL)
    target = loss_target.reshape(s, D_MODEL)
    dev = _dev_index(lax.axis_index("x"), lax.axis_index("y"), lax.axis_index("c"))

    wt_ffn_in, mt_ffn_in, vt_ffn_in = (jnp.transpose(a, (0, 2, 1)) for a in (w_ffn_in, m_w_ffn_in, v_w_ffn_in))
    local_w = [_cast_bf16(w_in, "cast_w_in"), _cast_bf16(w_out, "cast_w_out"),
               _cast_bf16(wt_ffn_in, "cast_w_ffn_in"), _cast_bf16(w_ffn_out, "cast_w_ffn_out")]
    wc_local = jnp.pad(jnp.transpose(w_conv, (0, 2, 1)).reshape(-1), (0, 1024 - n_layers * 3 * 64)).reshape(8, 128)
    biasm, win_next, wc_g = _bias_build(jnp.pad(rel_bias, ((0, 0), (0, 0), (0, REL_PAD - rel_bias.shape[2]))),
                                        comm=_Gather([(local_w[0], 0), (wc_local, None)]))
    weights = [None] * n_layers
    wc_full = wc_g.reshape(N_DEV, 1024)[:, :n_layers * 3 * 64].reshape(N_DEV, n_layers, 3, 64)
    wc_full = jnp.transpose(wc_full, (1, 2, 0, 3)).reshape(n_layers, 3, CONV_WIDTH)
    wc_full = jnp.pad(wc_full, ((0, 0), (0, 5), (0, 0)))

    g3 = {k: v.reshape(n_layers, 1, -1) for k, v in dict(
        conv=g_conv_out, attn=g_attn_out, pre_mix=g_pre_mix, post_mix=g_post_mix,
        pre_ffn=g_pre_ffn, post_ffn=g_post_ffn).items()}

    saved = []
    xl = x0
    h = _norm_cast(x0, g3["pre_mix"], 0, tm)
    for l in range(n_layers):
        win = win_next
        pc, ync, wout = _in_proj_conv(h, win, wc_full, g3["conv"], l, s, tq, comm=_Gather([(local_w[1], l)]))
        qkvp = _in_proj_qkv(h, win, s, tq)
        o, lse, yna, wfin = _attn_fwd(qkvp, biasm, g3["attn"], l, s, tq, comm=_Gather([(local_w[2], l)]))
        wout = wout.reshape(D_MODEL, D_MODEL)
        z, xm, h2 = _out_proj_fwd(ync, yna, wout, xl, g3["post_mix"], g3["pre_ffn"], l, s, tq)
        gu, act, wfout = _ffn_in_fwd(h2, wfin.reshape(4, FF_PAIR, D_MODEL), s, tf, comm=_Gather([(local_w[3], l)]))
        wfo = wfout.reshape(D_FF, D_MODEL)
        weights[l] = [win, wout, wfin.reshape(2 * D_FF, D_MODEL), wfo]
        sv = dict(x=xl, h=h, pc=pc, qkvp=qkvp, ync=ync, yna=yna, o=o, lse=lse, z=z, xm=xm, h2=h2, gu=gu, act=act)
        if l + 1 < n_layers:
            sv["f"], xl, h, win_next = _ffn_out_fwd(act, wfo, xm, g3["post_ffn"], g3["pre_mix"], l, l + 1, s, tm,
                                                    comm=_Gather([(local_w[0], l + 1)]))
        else:
            dx, sq, df, dg_post_ffn = _ffn_out_loss(act, wfo, xm, g3["post_ffn"], target, l, s, tm)
        saved.append(sv)

    loss = lax.psum(jnp.sum(sq) * (0.5 / D_MODEL), ("x", "y", "c"))

    lands = dict(win=[None] * n_layers, wout=[None] * n_layers, wfin=[None] * n_layers, wfout=[None] * n_layers)
    small = {k: [None] * n_layers for k in ("gco", "gao", "gpm", "gqm", "gpf", "gqf", "wc")}
    d_rel = [None] * n_layers
    started = []

    def start(name, keys, l, arrays):
        items = [(a, False) for a in arrays]
        send_sems, recv_sems, srcs, zones, token = _exchange_start(name + "_start", items)
        started.append((name, keys, l, items, send_sems, recv_sems, srcs, zones))
        return token[0:1, 0:1].reshape(1, 1, 1)

    for l in reversed(range(n_layers)):
        sv = saved[l]
        win, wout, wfin, wfo = weights[l]
        small["gqf"][l] = dg_post_ffn
        dgu, d_wfout = _ffn_out_bwd(df, wfo, sv["gu"], sv["act"], s, tm)
        d_wfout = d_wfout.reshape(N_DEV, FFO_SHARD, D_MODEL)
        d_wfin = _dw_ffn_in(sv["h2"], dgu, s).reshape(N_DEV, FF_SHARD, D_MODEL)
        g_pre_ffn = g3["pre_ffn"]
        if l == 0:
            g_pre_ffn = g_pre_ffn + start("exchange_ffn0", ("wfout", "wfin"), l, [d_wfout, d_wfin])
        dxm, dz, dg_pre_ffn, dg_post_mix = _ffn_in_bwd(
            dgu, wfin, sv["xm"], g_pre_ffn, dx, sv["z"], g3["post_mix"], l, s, tm)
        small["gpf"][l] = dg_pre_ffn
        small["gqm"][l] = dg_post_mix
        dyc, do, dg_attn, d_wout = _out_proj_bwd(dz, wout, sv["o"], g3["attn"], sv["ync"], sv["yna"], l, s, tq)
        d_wout = d_wout.reshape(N_DEV, D_MODEL // N_DEV, D_MODEL)
        small["gao"][l] = dg_attn
        dq, dk, dv, ds_sum = _attn_bwd(sv["qkvp"], biasm, sv["o"], sv["lse"], do, l, s, tq)
        d_rel[l] = _bias_bwd(ds_sum[None])
        dproj, dwc, dg_conv = _conv_bwd(sv["pc"], dyc, wc_full, g3["conv"], dq, dk, dv, l, s, tm)
        small["wc"][l] = dwc
        small["gco"][l] = dg_conv
        d_win = _dw_in(sv["h"], dproj, s)
        if l == 0:
            token = start("exchange_mix0", ("wout", "win"), l, [d_wout, d_win])
        else:
            token = start(f"exchange_layer{l}", ("wfout", "wfin", "wout", "win"), l, [d_wfout, d_wfin, d_wout, d_win])
        if l > 0:
            dx, dg_pre_mix, df, dg_post_ffn = _in_proj_bwd(
                dproj, win, sv["x"], g3["pre_mix"] + token, dxm, l, s, tm, f_prev=saved[l - 1]["f"],
                g_post3=g3["post_ffn"])
        else:
            dx, dg_pre_mix = _in_proj_bwd(dproj, win, sv["x"], g3["pre_mix"] + token, dxm, l, s, tm)
        small["gpm"][l] = dg_pre_mix
    grad_x = dx.reshape(1, s, D_MODEL)

    small_vec = _pack_small_grads(d_rel, small)
    small_items = [(small_vec, True)]
    small_sems = _exchange_start("exchange_small_start", small_items)

    owns = dict(win=[None] * n_layers, wout=[None] * n_layers, wfin=[None] * n_layers, wfout=[None] * n_layers)

    def wait(last, after):
        for name, keys, l, items, send_sems, recv_sems, srcs, zones in started:
            if (name == "exchange_mix0") == last:
                srcs, zones = _exchange_wait(name + "_wait", items, send_sems, recv_sems, srcs, zones, after)
                for key, src, zone in zip(keys, srcs, zones):
                    owns[key][l], lands[key][l] = src, zone

    me = dev.astype(jnp.int32).reshape(1)
    wait(False, small_sems[4])
    r_fin = [jnp.transpose(t, (0, 2, 1)) for t in _adamw(
        "adamw_w_ffn_in", wt_ffn_in, mt_ffn_in, vt_ffn_in, lands["wfin"], owns["wfin"], me)]
    r_fout = _adamw("adamw_w_ffn_out", w_ffn_out, m_w_ffn_out, v_w_ffn_out, lands["wfout"], owns["wfout"], me)
    wait(True, r_fout[0])
    r_out = _adamw("adamw_w_out", w_out, m_w_out, v_w_out, lands["wout"], owns["wout"], me)
    r_in = _adamw("adamw_w_in", w_in, m_w_in, v_w_in, lands["win"], owns["win"], me)
    (small_own,), (land_small,) = _exchange_wait(
        "exchange_small_wait", small_items, small_sems[0], small_sems[1], small_sems[2], small_sems[3], r_in[0])

    n_rep = 64 * n_layers
    rep = _adamw(
        "adamw_replicated",
        _pack_small(rel_bias, g_conv_out, g_attn_out, g_pre_mix, g_post_mix, g_pre_ffn, g_post_ffn)[None],
        _pack_small(m_rel_bias, m_g_conv_out, m_g_attn_out, m_g_pre_mix, m_g_post_mix, m_g_pre_ffn, m_g_post_ffn)[None],
        _pack_small(v_rel_bias, v_g_conv_out, v_g_attn_out, v_g_pre_mix, v_g_post_mix, v_g_pre_ffn, v_g_post_ffn)[None],
        [land_small[:, :n_rep]], [small_own[:n_rep]], me)
    rep = [_unpack_small(t[0], n_layers) for t in rep]

    wc_rows = n_layers * 3 * CONV_WIDTH // 128
    zeros_wc = jnp.zeros((1, wc_rows, 128), F32)
    g_wc_full = _adamw("sum_w_conv", zeros_wc, zeros_wc, zeros_wc, [land_small[:, n_rep:n_rep + wc_rows]],
                       [small_own[n_rep:n_rep + wc_rows]], me)[0]
    g_wc_full = g_wc_full.reshape(n_layers, 3, CONV_WIDTH)
    g_wc = lax.dynamic_slice_in_dim(g_wc_full, dev * (CONV_WIDTH // N_DEV), CONV_WIDTH // N_DEV, axis=2)
    g_wc = jnp.transpose(g_wc, (0, 2, 1))

    def tiny(a):
        flat = a.reshape(-1)
        return jnp.pad(flat, (0, (-flat.shape[0]) % 1024)).reshape(1, -1, 128)

    r_wc = _adamw("adamw_w_conv", tiny(w_conv), tiny(m_w_conv), tiny(v_w_conv), [tiny(g_wc)])
    r_wc = [t.reshape(-1)[:w_conv.size].reshape(w_conv.shape) for t in r_wc]

    def leaf(kind):
        return [r_in[kind], r_wc[kind], rep[kind][0], rep[kind][1], rep[kind][2], r_out[kind],
                rep[kind][3], rep[kind][4], rep[kind][5], rep[kind][6], r_fin[kind], r_fout[kind]]

    return (loss, grad_x, *leaf(0), *leaf(1), *leaf(2), *leaf(3))
```

```python
import math

import jax
import jax.numpy as jnp
from jax import lax
from jax.experimental import pallas as pl
from jax.experimental.pallas import tpu as pltpu

F32 = jnp.float32
BF16 = jnp.bfloat16

D_MODEL = 1024
N_DEV = 8
CHUNK = 64
N_LEFT_CHUNKS = 8
CONV_WIDTH = 512
ATTN_WIDTH = 512
HEAD_DIM = 64
N_HEADS = 8
REL_CLIP = 128
REL_PAD = 384
PROJ_WIDTH = 3072
PROJ_SHARD = PROJ_WIDTH // N_DEV
D_FF = 2816
FF_SHARD = 2 * D_FF // N_DEV
FFO_SHARD = D_FF // N_DEV
FF_PAIR = 2 * FF_SHARD
_COL_SUBTILES = (slice(0, 768), slice(768, FF_PAIR))
EPS = 1e-6
NEG_INF = -1e30
ATTN_PAIRS = 2
Q_BLOCK = 4 * CHUNK
K_BAND = Q_BLOCK + N_LEFT_CHUNKS * CHUNK
LEFT = N_LEFT_CHUNKS * CHUNK
TOEP = 1024

ADAM_LR = 0.001
ADAM_B1 = 0.9
ADAM_B2 = 0.999
ADAM_EPS = 1e-08
ADAM_WD = 0.01
ADAM_STEP = 10

VMEM_LIMIT = 52 * 1024 * 1024
SUB_ROWS = 256
MESH = pl.DeviceIdType.MESH
ANY = pl.BlockSpec(memory_space=pl.ANY)

NT = (((1,), (1,)), ((), ()))
TN = (((0,), (0,)), ((), ()))


def _dot(a, b):
    return jnp.dot(a, b, preferred_element_type=F32)


def _dot_nt(a, b):
    return lax.dot_general(a, b, NT, preferred_element_type=F32)


def _dot_tn(a, b):
    return lax.dot_general(a, b, TN, preferred_element_type=F32)


def _rstd(v):
    return lax.rsqrt(jnp.mean(v * v, axis=-1, keepdims=True) + EPS)


def _group_matrix():
    r = lax.broadcasted_iota(jnp.int32, (128, 128), 0) >> 6
    c = lax.broadcasted_iota(jnp.int32, (128, 128), 1) >> 6
    return jnp.where(r == c, 1.0, 0.0).astype(BF16)


def _group_mean(v, gmat):
    hi = v.astype(BF16)
    lo = (v - hi.astype(F32)).astype(BF16)
    return (_dot(hi, gmat) + _dot(lo, gmat)) * (1.0 / HEAD_DIM)


def _split3(v):
    hi = v.astype(BF16)
    r1 = v - hi.astype(F32)
    mid = r1.astype(BF16)
    lo = (r1 - mid.astype(F32)).astype(BF16)
    return hi, mid, lo


def _row_tile(rows, cands=(1024, 512, 704, 256, 128, 64, 32, 16)):
    for c in cands:
        if rows % c == 0:
            return c
    return rows


def _dev_index(px, py, pc):
    return 4 * px + 2 * py + pc


def _when(cond):
    if cond is True:
        return lambda fn: fn()
    return pl.when(cond)


def _phases(grid):
    def phases():
        if not grid:
            return True, True, True
        lin = pl.program_id(0)
        for a in range(1, len(grid)):
            lin = lin * grid[a] + pl.program_id(a)
        total = math.prod(grid)
        return lin == 0, lin == total - 1, lin == total - 1
    return phases


class _Gather:
    def __init__(self, items):
        self.items = items
        self.args = [a for a, _ in items]
        n = len(items)
        self.out_shape = [jax.ShapeDtypeStruct((N_DEV,) + (a.shape if lay is None else a.shape[1:]), a.dtype)
                          for a, lay in items]
        self.scratch = [pltpu.SemaphoreType.DMA((n, 7)), pltpu.SemaphoreType.DMA((n, 7)),
                        pltpu.SemaphoreType.DMA((n,))]

    def _ctx(self, ins, outs, sems):
        send_sems, recv_sems, local_sems = sems
        x, y, c = lax.axis_index("x"), lax.axis_index("y"), lax.axis_index("c")
        chips = [(1 - x, y), (x, 1 - y), (1 - x, 1 - y)]

        def src(k):
            lay = self.items[k][1]
            return ins[k] if lay is None else ins[k].at[lay]

        def copy(k, s, idx, to, from_src=False):
            return pltpu.make_async_remote_copy(
                src_ref=src(k) if from_src else outs[k].at[idx], dst_ref=outs[k].at[idx],
                send_sem=send_sems.at[k, s], recv_sem=recv_sems.at[k, s],
                device_id=to, device_id_type=MESH)

        def local(k):
            return pltpu.make_async_copy(src(k), outs[k].at[_dev_index(x, y, c)], local_sems.at[k])

        return x, y, c, chips, copy, local

    def start(self, ins, outs, sems, cond):
        n = len(self.items)

        @_when(cond)
        def _():
            x, y, c, chips, copy, local = self._ctx(ins, outs, sems)
            me = _dev_index(x, y, c)
            for k in range(n):
                local(k).start()
                copy(k, 0, me, (x, y, 1 - c), from_src=True).start()
                for j, chip in enumerate(chips):
                    copy(k, 1 + j, me, (chip[0], chip[1], c), from_src=True).start()

    def forward(self, ins, outs, sems, cond):
        n = len(self.items)

        @_when(cond)
        def _():
            x, y, c, chips, copy, local = self._ctx(ins, outs, sems)
            for j, chip in enumerate(chips):
                idx = _dev_index(chip[0], chip[1], c)
                for k in range(n):
                    copy(k, 1 + j, idx, (x, y, c)).wait_recv()
                    copy(k, 4 + j, idx, (x, y, 1 - c)).start()

    def finish(self, ins, outs, sems, cond):
        n = len(self.items)

        @_when(cond)
        def _():
            x, y, c, chips, copy, local = self._ctx(ins, outs, sems)
            me = _dev_index(x, y, c)
            for k in range(n):
                copy(k, 0, _dev_index(x, y, 1 - c), (x, y, c)).wait_recv()
            for j, chip in enumerate(chips):
                idx = _dev_index(chip[0], chip[1], 1 - c)
                for k in range(n):
                    copy(k, 4 + j, idx, (x, y, c)).wait_recv()
            for k in range(n):
                for s in range(4):
                    copy(k, s, me, (x, y, c), from_src=True).wait_send()
                for j, chip in enumerate(chips):
                    copy(k, 4 + j, _dev_index(chip[0], chip[1], c), (x, y, c)).wait_send()
                local(k).wait()


_PEER_FLIPS = [(0, 0, 1), (1, 0, 0), (0, 1, 0), (1, 1, 0), (1, 0, 1), (0, 1, 1), (1, 1, 1)]


def _call(body, *, name, grid, in_specs, out_specs, out_shape, args, scratch=(), comm=None):
    n_hi, n_ho, n_hs = len(args), len(out_shape), len(scratch)
    c_args = list(comm.args) if comm else []
    c_out = list(comm.out_shape) if comm else []
    c_scr = list(comm.scratch) if comm else []
    phases = _phases(grid)

    def kern(*refs):
        cuts = [n_hi, len(c_args), n_ho, len(c_out), n_hs, len(c_scr)]
        parts, pos = [], 0
        for n in cuts:
            parts.append(refs[pos:pos + n])
            pos += n
        hi, ci, ho, co, hs, cs = parts
        if comm:
            first, mid, last = phases()
            comm.start(ci, co, cs, first)
            comm.forward(ci, co, cs, mid)
        body(*hi, *ho, *hs)
        if comm:
            comm.finish(ci, co, cs, last)

    sem = ("arbitrary",) * len(grid) if grid else None
    return pl.pallas_call(
        kern, name=name, grid=grid,
        in_specs=list(in_specs) + [ANY] * len(c_args),
        out_specs=list(out_specs) + [ANY] * len(c_out),
        out_shape=list(out_shape) + c_out,
        scratch_shapes=list(scratch) + c_scr,
        compiler_params=pltpu.CompilerParams(dimension_semantics=sem, vmem_limit_bytes=VMEM_LIMIT),
    )(*args, *c_args)


def _comm_only(name, comm):
    return _call(lambda: None, name=name, grid=(), in_specs=[], out_specs=[], out_shape=[], args=[], comm=comm)


HBM_SPEC = pl.BlockSpec(memory_space=pltpu.HBM)
SEM_SPEC = pl.BlockSpec(memory_space=pltpu.SEMAPHORE)
SIDE_EFFECT = pltpu.SideEffectType.DATAFLOW_SIDE_EFFECTING


def _exchange_peer(x, y, c, s):
    fx, fy, fc = _PEER_FLIPS[s]
    return x ^ fx, y ^ fy, c ^ fc


def _exchange_start(name, items):
    n = len(items)
    srcs = [pltpu.with_memory_space_constraint(a, pltpu.HBM) for a, _ in items]
    land_shapes = [(N_DEV,) + (a.shape if whole else a.shape[1:]) for a, whole in items]
    lands = [pltpu.with_memory_space_constraint(lax.empty(shp, a.dtype), pltpu.HBM)
             for shp, (a, _) in zip(land_shapes, items)]

    n_sem = 7 * n

    def body(*refs):
        src_refs, land_refs = refs[:n], refs[n:2 * n]
        send_sems = refs[2 * n:2 * n + n_sem]
        recv_sems = refs[2 * n + n_sem:2 * n + 2 * n_sem]
        token = refs[-1]
        x, y, c = lax.axis_index("x"), lax.axis_index("y"), lax.axis_index("c")
        me = _dev_index(x, y, c)
        for s in range(7):
            px, py, pc = _exchange_peer(x, y, c, s)
            for k in range(n):
                src = src_refs[k] if items[k][1] else src_refs[k].at[_dev_index(px, py, pc)]
                pltpu.make_async_remote_copy(
                    src_ref=src, dst_ref=land_refs[k].at[me],
                    send_sem=send_sems[7 * k + s], recv_sem=recv_sems[7 * k + s],
                    device_id=(px, py, pc), device_id_type=MESH).start()
        token[...] = jnp.zeros(token.shape, token.dtype)

    outs = pl.pallas_call(
        body, name=name,
        out_shape=(*[pltpu.SemaphoreType.DMA(())] * (2 * n_sem),
                   *[pltpu.HBM(a.shape, a.dtype) for a in srcs],
                   *[pltpu.HBM(shp, a.dtype) for shp, a in zip(land_shapes, srcs)],
                   jax.ShapeDtypeStruct((8, 128), F32)),
        in_specs=[HBM_SPEC] * (2 * n),
        out_specs=(*[SEM_SPEC] * (2 * n_sem), *[HBM_SPEC] * (2 * n), pl.BlockSpec(memory_space=pltpu.VMEM)),
        input_output_aliases={i: 2 * n_sem + i for i in range(2 * n)},
        compiler_params=pltpu.CompilerParams(has_side_effects=SIDE_EFFECT),
    )(*srcs, *lands)
    base = 2 * n_sem
    return (list(outs[:n_sem]), list(outs[n_sem:base]), list(outs[base:base + n]),
            list(outs[base + n:base + 2 * n]), outs[-1])


def _exchange_wait(name, items, send_sems, recv_sems, srcs, lands, after):
    n = len(items)

    n_sem = 7 * n

    def body(*refs):
        src_refs, land_refs = refs[:n], refs[n:2 * n]
        send_refs = refs[2 * n:2 * n + n_sem]
        recv_refs = refs[2 * n + n_sem:2 * n + 2 * n_sem]
        x, y, c = lax.axis_index("x"), lax.axis_index("y"), lax.axis_index("c")
        for s in range(7):
            for k in range(n):
                copy = pltpu.make_async_remote_copy(
                    src_ref=src_refs[k] if items[k][1] else src_refs[k].at[0], dst_ref=land_refs[k].at[0],
                    send_sem=send_refs[7 * k + s], recv_sem=recv_refs[7 * k + s],
                    device_id=(x, y, c), device_id_type=MESH)
                copy.wait_send()
                copy.wait_recv()

    outs = pl.pallas_call(
        body, name=name,
        out_shape=(*[pltpu.HBM(a.shape, a.dtype) for a in srcs], *[pltpu.HBM(a.shape, a.dtype) for a in lands]),
        in_specs=[HBM_SPEC] * (2 * n) + [SEM_SPEC] * (2 * n_sem) + [ANY],
        out_specs=tuple([HBM_SPEC] * (2 * n)),
        input_output_aliases={i: i for i in range(2 * n)},
        compiler_params=pltpu.CompilerParams(has_side_effects=SIDE_EFFECT),
    )(*srcs, *lands, *send_sems, *recv_sems, after)
    return list(outs[:n]), list(outs[n:])


def _cast_bf16(x, name):
    shape = x.shape
    x2 = x.reshape(-1, shape[-1])
    rows, cols = x2.shape
    tr = _row_tile(rows)

    def body(x_ref, o_ref):
        o_ref[...] = x_ref[...].astype(BF16)

    blk = pl.BlockSpec((tr, cols), lambda i: (i, 0))
    out, = _call(body, name=name, grid=(rows // tr,), in_specs=[blk], out_specs=[blk],
                 out_shape=[jax.ShapeDtypeStruct((rows, cols), BF16)], args=[x2])
    return out.reshape(shape)


def _norm_cast(x, g3, l, tm):
    s = x.shape[0]

    def body(x_ref, g_ref, o_ref):
        v = x_ref[...]
        o_ref[...] = (v * _rstd(v) * g_ref[...]).astype(BF16)

    row = pl.BlockSpec((tm, D_MODEL), lambda i: (i, 0))
    out, = _call(body, name="norm_cast", grid=(s // tm,),
                 in_specs=[row, pl.BlockSpec((None, 1, D_MODEL), lambda i: (l, 0, 0))], out_specs=[row],
                 out_shape=[jax.ShapeDtypeStruct((s, D_MODEL), BF16)], args=[x, g3])
    return out


def _in_proj_qkv(h, win, s, tq):
    def body(a_ref, b_ref, o_ref):
        i = pl.program_id(0)

        @pl.when(i == 0)
        def _():
            o_ref[...] = jnp.zeros(o_ref.shape, BF16)

        @pl.when(i > 0)
        def _():
            w = jnp.concatenate([b_ref[j] for j in range(4)], axis=1)
            o_ref[...] = _dot(a_ref[...], w).astype(BF16)

    out, = _call(
        body, name="in_proj_qkv", grid=(s // tq + 1,),
        in_specs=[pl.BlockSpec((tq, D_MODEL), lambda i: (jnp.maximum(i - 1, 0), 0)),
                  pl.BlockSpec((4, D_MODEL, PROJ_SHARD), lambda i: (1, 0, 0))],
        out_specs=[pl.BlockSpec((tq, 4 * PROJ_SHARD), lambda i: (i, 0))],
        out_shape=[jax.ShapeDtypeStruct((s + tq, PROJ_WIDTH // 2), BF16)], args=[h, win])
    return out


def _in_proj_conv(h, win, wc, g3, l, s, tq, comm=None):
    def body(a_ref, b_ref, wc_ref, g_ref, pc_ref, y_ref, carry_ref, acc_ref):
        i = pl.program_id(0)
        w = jnp.concatenate([b_ref[j] for j in range(4)], axis=1)
        acc_ref[...] = _dot(a_ref[...], w)
        pc_ref[...] = acc_ref[...].astype(BF16)
        gmat = _group_matrix()
        for j in range(CONV_WIDTH // 128):
            c0, c1, c2 = 128 * j, CONV_WIDTH + 128 * j, 2 * CONV_WIDTH + 128 * j
            hc = acc_ref[:, c0:c0 + 128]
            bg = acc_ref[:, c1:c1 + 128]
            cg = acc_ref[:, c2:c2 + 128]
            u_prev = jnp.where(i > 0, carry_ref[:, c0:c0 + 128], 0.0)
            u = cg * hc
            carry_ref[:, c0:c0 + 128] = u[tq - 8:tq, :]
            full = jnp.concatenate([u_prev, u], axis=0)
            u1 = pltpu.roll(full, 1, 0)[8:]
            u2 = pltpu.roll(full, 2, 0)[8:]
            out = (u2 * wc_ref[0:1, c0:c0 + 128] + u1 * wc_ref[1:2, c0:c0 + 128]
                   + u * wc_ref[2:3, c0:c0 + 128])
            yc = bg * out
            r = lax.rsqrt(_group_mean(yc * yc, gmat) + EPS)
            y_ref[:, c0:c0 + 128] = (yc * r * g_ref[:, c0:c0 + 128]).astype(BF16)

    return _call(
        body, name="in_proj_conv", grid=(s // tq,),
        in_specs=[pl.BlockSpec((tq, D_MODEL), lambda i: (i, 0)),
                  pl.BlockSpec((4, D_MODEL, PROJ_SHARD), lambda i: (0, 0, 0)),
                  pl.BlockSpec((None, 8, CONV_WIDTH), lambda i: (l, 0, 0)),
                  pl.BlockSpec((None, 1, CONV_WIDTH), lambda i: (l, 0, 0))],
        out_specs=[pl.BlockSpec((tq, 3 * CONV_WIDTH), lambda i: (i, 0)),
                   pl.BlockSpec((tq, CONV_WIDTH), lambda i: (i, 0))],
        out_shape=[jax.ShapeDtypeStruct((s, 3 * CONV_WIDTH), BF16), jax.ShapeDtypeStruct((s, CONV_WIDTH), BF16)],
        scratch=[pltpu.VMEM((8, CONV_WIDTH), F32), pltpu.VMEM((tq, 3 * CONV_WIDTH), F32)],
        args=[h, win, wc, g3], comm=comm)


def _toeplitz_source():
    r_i = lax.broadcasted_iota(jnp.int32, (REL_PAD, TOEP), 0)
    m_i = lax.broadcasted_iota(jnp.int32, (REL_PAD, TOEP), 1)
    idx = jnp.clip((K_BAND - 1) - m_i, -REL_CLIP, REL_CLIP) + REL_CLIP
    return jnp.where(r_i == idx, 1.0, 0.0).astype(BF16)


def _bias_build(rbp, comm=None):
    n_layers = rbp.shape[0]

    def body(rb_ref, o_ref, t_ref):
        pmat = _toeplitz_source()
        hi, mid, lo = _split3(rb_ref[...])
        t_ref[...] = _dot(hi, pmat) + _dot(mid, pmat) + _dot(lo, pmat)
        shift = (CHUNK - 1) - lax.broadcasted_iota(jnp.int32, (CHUNK, TOEP), 0)
        kchunk = lax.broadcasted_iota(jnp.int32, (CHUNK, K_BAND), 1) >> 6
        for h in range(N_HEADS):
            b = jnp.broadcast_to(t_ref[pl.ds(h, 1), :], (CHUNK, TOEP))
            for bit in range(6):
                rolled = pltpu.roll(b, TOEP - (1 << bit), 1)
                b = jnp.where(((shift >> bit) & 1) == 1, rolled, b)
            for cq in range(Q_BLOCK // CHUNK):
                off = CHUNK * (Q_BLOCK // CHUNK - 1 - cq)
                band = pltpu.roll(b, TOEP - off, 1) if off else b
                dchunk = kchunk - cq
                in_band = jnp.where(dchunk >= 0, jnp.where(dchunk <= N_LEFT_CHUNKS, 1, 0), 0) == 1
                o_ref[h, CHUNK * cq:CHUNK * (cq + 1), :] = jnp.where(in_band, band[:, :K_BAND], NEG_INF)

    return _call(
        body, name="bias_build", grid=(n_layers,),
        in_specs=[pl.BlockSpec((None, N_HEADS, REL_PAD), lambda l: (l, 0, 0))],
        out_specs=[pl.BlockSpec((None, N_HEADS, Q_BLOCK, K_BAND), lambda l: (l, 0, 0, 0))],
        out_shape=[jax.ShapeDtypeStruct((n_layers, N_HEADS, Q_BLOCK, K_BAND), F32)],
        scratch=[pltpu.VMEM((N_HEADS, TOEP), F32)], args=[rbp], comm=comm)


def _bias_bwd(ds_sum):
    n_layers = ds_sum.shape[0]

    def body(ds_ref, o_ref, t_ref):
        pmat = _toeplitz_source()
        shift = (CHUNK - 1) - lax.broadcasted_iota(jnp.int32, (CHUNK, TOEP), 0)
        for h in range(N_HEADS):
            d = None
            for cq in range(Q_BLOCK // CHUNK):
                off = CHUNK * (Q_BLOCK // CHUNK - 1 - cq)
                part = jnp.concatenate([ds_ref[h, CHUNK * cq:CHUNK * (cq + 1), :],
                                        jnp.zeros((CHUNK, TOEP - K_BAND), F32)], axis=1)
                part = pltpu.roll(part, off, 1) if off else part
                d = part if d is None else d + part
            for bit in range(6):
                rolled = pltpu.roll(d, 1 << bit, 1)
                d = jnp.where(((shift >> bit) & 1) == 1, rolled, d)
            t_ref[pl.ds(h, 1), :] = jnp.sum(d, axis=0, keepdims=True)
        hi, mid, lo = _split3(t_ref[...])
        o_ref[...] = _dot_nt(hi, pmat) + _dot_nt(mid, pmat) + _dot_nt(lo, pmat)

    out, = _call(
        body, name="bias_bwd", grid=(n_layers,),
        in_specs=[pl.BlockSpec((None, N_HEADS, Q_BLOCK, K_BAND), lambda l: (l, 0, 0, 0))],
        out_specs=[pl.BlockSpec((None, N_HEADS, REL_PAD), lambda l: (l, 0, 0))],
        out_shape=[jax.ShapeDtypeStruct((n_layers, N_HEADS, REL_PAD), F32)],
        scratch=[pltpu.VMEM((N_HEADS, TOEP), F32)], args=[ds_sum])
    return out


def _attn_fwd(qkvp, biasm, g3, l, s, pad, comm=None):
    nb = s // Q_BLOCK
    qb0 = pad // Q_BLOCK
    scale = HEAD_DIM ** -0.5
    wide = 128 * ATTN_PAIRS

    def body(q_ref, k_ref, v_ref, b_ref, g_ref, o_ref, lse_ref, yn_ref):
        blk = pl.program_id(1)
        koff = pl.multiple_of(blk * Q_BLOCK + (pad - LEFT), Q_BLOCK)
        lane = lax.broadcasted_iota(jnp.int32, (1, 128), 1)
        kpos = lax.broadcasted_iota(jnp.int32, (1, K_BAND), 1) + (blk * Q_BLOCK - LEFT)
        kmask = jnp.where(kpos >= 0, 0.0, NEG_INF)
        gmat = _group_matrix()
        for pr in range(ATTN_PAIRS):
            ls = slice(128 * pr, 128 * (pr + 1))
            q = q_ref[:, ls]
            kb = k_ref[pl.ds(koff, K_BAND), ls]
            vb = v_ref[pl.ds(koff, K_BAND), ls]
            outs, lses = [], []
            for hh in range(2):
                in_head = (lane >> 6) == hh
                qm = jnp.where(in_head, q, jnp.zeros_like(q)) * jnp.asarray(scale, BF16)
                sc = _dot_nt(qm, kb) + b_ref[2 * pr + hh] + kmask
                m = jnp.max(sc, axis=1, keepdims=True)
                e = jnp.exp(sc - m)
                den = jnp.sum(e, axis=1, keepdims=True)
                outs.append(_dot(e.astype(BF16), vb) * (1.0 / den))
                lses.append(m + jnp.log(den))
            first = lane < HEAD_DIM
            o = jnp.where(first, outs[0], outs[1])
            o_ref[:, ls] = o
            lse_ref[:, ls] = jnp.where(first, lses[0], lses[1])
            r = lax.rsqrt(_group_mean(o * o, gmat) + EPS)
            yn_ref[:, ls] = (o * r * g_ref[:, ls]).astype(BF16)

    blk_out = pl.BlockSpec((Q_BLOCK, wide), lambda p, b: (b, p))
    n_grp = ATTN_WIDTH // wide
    return _call(
        body, name="attn_fwd", grid=(n_grp, nb),
        in_specs=[pl.BlockSpec((Q_BLOCK, wide), lambda p, b: (qb0 + b, p)),
                  pl.BlockSpec((s + pad, wide), lambda p, b: (0, n_grp + p)),
                  pl.BlockSpec((s + pad, wide), lambda p, b: (0, 2 * n_grp + p)),
                  pl.BlockSpec((None, 2 * ATTN_PAIRS, Q_BLOCK, K_BAND), lambda p, b: (l, p, 0, 0)),
                  pl.BlockSpec((None, 1, wide), lambda p, b: (l, 0, p))],
        out_specs=[blk_out, blk_out, blk_out],
        out_shape=[jax.ShapeDtypeStruct((s, ATTN_WIDTH), F32),
                   jax.ShapeDtypeStruct((s, ATTN_WIDTH), F32),
                   jax.ShapeDtypeStruct((s, ATTN_WIDTH), BF16)],
        args=[qkvp, qkvp, qkvp, biasm, g3], comm=comm)


def _out_proj_fwd(ync, yna, wout, x, g_post3, g_next3, l, s, tm):
    half = D_MODEL // 2

    def body(a1_ref, a2_ref, w_ref, x_ref, gp_ref, gn_ref, z_ref, xm_ref, h_ref):
        for rs in _row_subtiles(tm, SUB_ROWS):
            z = _dot(a1_ref[rs, :], w_ref[0:half, :]) + _dot(a2_ref[rs, :], w_ref[half:D_MODEL, :])
            z_ref[rs, :] = z.astype(BF16)
            xm = x_ref[rs, :] + z * _rstd(z) * gp_ref[...]
            xm_ref[rs, :] = xm
            h_ref[rs, :] = (xm * _rstd(xm) * gn_ref[...]).astype(BF16)

    row = pl.BlockSpec((tm, D_MODEL), lambda i: (i, 0))
    gain = pl.BlockSpec((None, 1, D_MODEL), lambda i: (l, 0, 0))
    return _call(
        body, name="out_proj_fwd", grid=(s // tm,),
        in_specs=[pl.BlockSpec((tm, half), lambda i: (i, 0)), pl.BlockSpec((tm, half), lambda i: (i, 0)),
                  pl.BlockSpec((D_MODEL, D_MODEL), lambda i: (0, 0)), row, gain, gain],
        out_specs=[row, row, row],
        out_shape=[jax.ShapeDtypeStruct((s, D_MODEL), BF16), jax.ShapeDtypeStruct((s, D_MODEL), F32),
                   jax.ShapeDtypeStruct((s, D_MODEL), BF16)],
        args=[ync, yna, wout, x, g_post3, g_next3])


def _ffn_in_fwd(h2, wfin4, s, tm, comm=None):
    def body(h_ref, wg_ref, wu_ref, gu_ref, act_ref):
        h = h_ref[...]
        for cs in _COL_SUBTILES:
            gate = _dot_nt(h, wg_ref[cs, :])
            up = _dot_nt(h, wu_ref[cs, :])
            gu_ref[0, :, cs] = gate.astype(BF16)
            gu_ref[1, :, cs] = up.astype(BF16)
            act_ref[:, cs] = (gate * (1.0 / (1.0 + jnp.exp(-gate))) * up).astype(BF16)

    return _call(
        body, name="ffn_in_fwd", grid=(2, s // tm),
        in_specs=[pl.BlockSpec((tm, D_MODEL), lambda b, i: (i, 0)),
                  pl.BlockSpec((None, FF_PAIR, D_MODEL), lambda b, i: (b, 0, 0)),
                  pl.BlockSpec((None, FF_PAIR, D_MODEL), lambda b, i: (2 + b, 0, 0))],
        out_specs=[pl.BlockSpec((2, tm, FF_PAIR), lambda b, i: (0, i, b)),
                   pl.BlockSpec((tm, FF_PAIR), lambda b, i: (i, b))],
        out_shape=[jax.ShapeDtypeStruct((2, s, D_FF), BF16), jax.ShapeDtypeStruct((s, D_FF), BF16)],
        args=[h2, wfin4, wfin4], comm=comm)


def _ffn_out_fwd(act, wfo, xm, g_post3, g_next3, l, l_next, s, tm, comm=None):
    def body(a_ref, w_ref, x_ref, gp_ref, gn_ref, f_ref, xo_ref, h_ref):
        for rs in _row_subtiles(tm, SUB_ROWS):
            f = _dot(a_ref[rs, :], w_ref[...])
            f_ref[rs, :] = f.astype(BF16)
            xo = x_ref[rs, :] + f * _rstd(f) * gp_ref[...]
            xo_ref[rs, :] = xo
            h_ref[rs, :] = (xo * _rstd(xo) * gn_ref[...]).astype(BF16)

    row = pl.BlockSpec((tm, D_MODEL), lambda i: (i, 0))
    return _call(
        body, name="ffn_out_fwd", grid=(s // tm,),
        in_specs=[pl.BlockSpec((tm, D_FF), lambda i: (i, 0)),
                  pl.BlockSpec((D_FF, D_MODEL), lambda i: (0, 0), pipeline_mode=pl.Buffered(1)), row,
                  pl.BlockSpec((None, 1, D_MODEL), lambda i: (l, 0, 0)),
                  pl.BlockSpec((None, 1, D_MODEL), lambda i: (l_next, 0, 0))],
        out_specs=[row, row, row],
        out_shape=[jax.ShapeDtypeStruct((s, D_MODEL), BF16), jax.ShapeDtypeStruct((s, D_MODEL), F32),
                   jax.ShapeDtypeStruct((s, D_MODEL), BF16)],
        args=[act, wfo, xm, g_post3, g_next3], comm=comm)


def _ffn_out_loss(act, wfo, xm, g_post3, target, l, s, tm):
    def body(a_ref, w_ref, x_ref, gp_ref, t_ref, dx_ref, sq_ref, df_ref, dg_ref):
        _zero_first((sq_ref, dg_ref), pl.program_id(0) == 0)
        for rs in _row_subtiles(tm, SUB_ROWS):
            f = _dot(a_ref[rs, :], w_ref[...])
            gain = gp_ref[...]
            err = x_ref[rs, :] + f * _rstd(f) * gain - t_ref[rs, :]
            dx = err * (1.0 / D_MODEL)
            dx_ref[rs, :] = dx
            df, dyn = _norm_bwd_rows(f, gain, dx)
            df_ref[rs, :] = df.astype(BF16)
            _add_cols(dg_ref, dyn)
            cs = jnp.sum(err * err, axis=0, keepdims=True)
            part = cs[:, 0:128]
            for k in range(1, D_MODEL // 128):
                part = part + cs[:, 128 * k:128 * (k + 1)]
            sq_ref[0:1, :] += part

    row = pl.BlockSpec((tm, D_MODEL), lambda i: (i, 0))
    return _call(
        body, name="ffn_out_loss", grid=(s // tm,),
        in_specs=[pl.BlockSpec((tm, D_FF), lambda i: (i, 0)),
                  pl.BlockSpec((D_FF, D_MODEL), lambda i: (0, 0), pipeline_mode=pl.Buffered(1)), row,
                  pl.BlockSpec((None, 1, D_MODEL), lambda i: (l, 0, 0)), row],
        out_specs=[row, pl.BlockSpec((8, 128), lambda i: (0, 0)), row, pl.BlockSpec((8, D_MODEL), lambda i: (0, 0))],
        out_shape=[jax.ShapeDtypeStruct((s, D_MODEL), F32), jax.ShapeDtypeStruct((8, 128), F32),
                   jax.ShapeDtypeStruct((s, D_MODEL), BF16), jax.ShapeDtypeStruct((8, D_MODEL), F32)],
        args=[act, wfo, xm, g_post3, target])


def _norm_bwd_rows(v, g, dy):
    r = _rstd(v)
    vn = v * r
    gd = dy * g
    dv = r * (gd - vn * jnp.mean(vn * gd, axis=-1, keepdims=True))
    return dv, dy * vn


def _zero_first(refs, first):
    @pl.when(first)
    def _():
        for ref in refs:
            ref[...] = jnp.zeros(ref.shape, F32)


def _add_cols(ref, val):
    ref[0:1, :] += jnp.sum(val, axis=0, keepdims=True)


def _accum_cols(ref, val, first):
    _zero_first((ref,), first)
    _add_cols(ref, val)


def _row_subtiles(rows, sub):
    sub = min(sub, rows)
    return [slice(r, r + sub) for r in range(0, rows, sub)]


def _ffn_out_bwd(df, wfo, gu, act, s, tm, comm=None):
    nm = s // tm

    def body(df_ref, w_ref, gu_ref, act_ref, dgu_ref, dw_ref, acc_ref):
        i = pl.program_id(1)
        df = df_ref[...]
        _zero_first((acc_ref,), i == 0)
        acc_ref[...] += _dot_tn(act_ref[...], df)

        @pl.when(i == nm - 1)
        def _():
            dw_ref[...] = acc_ref[...].astype(BF16)

        for cs in _COL_SUBTILES:
            da = _dot_nt(df, w_ref[cs, :])
            g = gu_ref[0, :, cs].astype(F32)
            u = gu_ref[1, :, cs].astype(F32)
            sg = 1.0 / (1.0 + jnp.exp(-g))
            dgu_ref[0, :, cs] = (da * u * (sg * (1.0 + g * (1.0 - sg)))).astype(BF16)
            dgu_ref[1, :, cs] = (da * (g * sg)).astype(BF16)

    blk = pl.BlockSpec((2, tm, FF_PAIR), lambda b, i: (0, i, b))
    wblk = pl.BlockSpec((FF_PAIR, D_MODEL), lambda b, i: (b, 0))
    return _call(
        body, name="ffn_out_bwd", grid=(2, nm),
        in_specs=[pl.BlockSpec((tm, D_MODEL), lambda b, i: (i, 0)), wblk, blk,
                  pl.BlockSpec((tm, FF_PAIR), lambda b, i: (i, b))],
        out_specs=[blk, wblk],
        out_shape=[jax.ShapeDtypeStruct((2, s, D_FF), BF16), jax.ShapeDtypeStruct((D_FF, D_MODEL), BF16)],
        scratch=[pltpu.VMEM((FF_PAIR, D_MODEL), F32)],
        args=[df, wfo, gu, act], comm=comm)


def _dw_ffn_in(h2, dgu, s):
    def body(a_ref, b_ref, o_ref):
        o_ref[...] = _dot_tn(b_ref[...], a_ref[...]).astype(BF16)

    out, = _call(
        body, name="dw_ffn_in", grid=(4,),
        in_specs=[pl.BlockSpec((s, D_MODEL), lambda n: (0, 0), pipeline_mode=pl.Buffered(1)),
                  pl.BlockSpec((None, s, FF_PAIR), lambda n: (n // 2, 0, n % 2))],
        out_specs=[pl.BlockSpec((None, FF_PAIR, D_MODEL), lambda n: (n, 0, 0))],
        out_shape=[jax.ShapeDtypeStruct((4, FF_PAIR, D_MODEL), BF16)], args=[h2, dgu])
    return out


def _ffn_in_bwd(dgu, wfin, xm, g_pre3, dres, z, g_post3, l, s, tm, comm=None):
    def body(d_ref, w_ref, xm_ref, gp_ref, dres_ref, z_ref, gq_ref, dxm_ref, dz_ref, dgp_ref, dgq_ref):
        _zero_first((dgp_ref, dgq_ref), pl.program_id(0) == 0)
        for rs in _row_subtiles(tm, SUB_ROWS):
            dh = _dot(d_ref[0, rs, :], w_ref[0:D_FF, :]) + _dot(d_ref[1, rs, :], w_ref[D_FF:2 * D_FF, :])
            dx, dyn = _norm_bwd_rows(xm_ref[rs, :], gp_ref[...], dh)
            dxm = dres_ref[rs, :] + dx
            dxm_ref[rs, :] = dxm
            _add_cols(dgp_ref, dyn)
            dz, dyn2 = _norm_bwd_rows(z_ref[rs, :].astype(F32), gq_ref[...], dxm)
            dz_ref[rs, :] = dz.astype(BF16)
            _add_cols(dgq_ref, dyn2)

    row = pl.BlockSpec((tm, D_MODEL), lambda i: (i, 0))
    gain = pl.BlockSpec((None, 1, D_MODEL), lambda i: (l, 0, 0))
    dgs = pl.BlockSpec((8, D_MODEL), lambda i: (0, 0))
    return _call(
        body, name="ffn_in_bwd", grid=(s // tm,),
        in_specs=[pl.BlockSpec((2, tm, D_FF), lambda i: (0, i, 0)),
                  pl.BlockSpec((2 * D_FF, D_MODEL), lambda i: (0, 0), pipeline_mode=pl.Buffered(1)),
                  row, gain, row, row, gain],
        out_specs=[row, row, dgs, dgs],
        out_shape=[jax.ShapeDtypeStruct((s, D_MODEL), F32), jax.ShapeDtypeStruct((s, D_MODEL), BF16),
                   jax.ShapeDtypeStruct((8, D_MODEL), F32), jax.ShapeDtypeStruct((8, D_MODEL), F32)],
        args=[dgu, wfin, xm, g_pre3, dres, z, g_post3], comm=comm)


def _out_proj_bwd(dz, wout, o, g3, ync, yna, l, s, tm):
    nm = s // tm
    half = D_MODEL // 2

    def body(dz_ref, w_ref, o_ref, g_ref, a1_ref, a2_ref, dyc_ref, do_ref, dg_ref, dw_ref, acc_ref):
        i = pl.program_id(0)
        gmat = _group_matrix()
        _zero_first((dg_ref, acc_ref), i == 0)
        dzv = dz_ref[...]
        acc_ref[0:half, :] += _dot_tn(a1_ref[...], dzv)
        acc_ref[half:D_MODEL, :] += _dot_tn(a2_ref[...], dzv)

        @pl.when(i == nm - 1)
        def _():
            dw_ref[...] = acc_ref[...].astype(BF16)

        for rs in _row_subtiles(tm, SUB_ROWS):
            dy = _dot_nt(dz_ref[rs, :], w_ref[...])
            dyc_ref[rs, :] = dy[:, 0:CONV_WIDTH]
            for j in range(ATTN_WIDTH // 128):
                c0 = 128 * j
                ov = o_ref[rs, c0:c0 + 128]
                dyn = dy[:, CONV_WIDTH + c0:CONV_WIDTH + c0 + 128]
                r = lax.rsqrt(_group_mean(ov * ov, gmat) + EPS)
                on = ov * r
                gd = dyn * g_ref[:, c0:c0 + 128]
                do_ref[rs, c0:c0 + 128] = r * (gd - on * _group_mean(on * gd, gmat))
                dg_ref[0:1, c0:c0 + 128] += jnp.sum(dyn * on, axis=0, keepdims=True)

    halfrow = pl.BlockSpec((tm, ATTN_WIDTH), lambda i: (i, 0))
    return _call(
        body, name="out_proj_bwd", grid=(nm,),
        in_specs=[pl.BlockSpec((tm, D_MODEL), lambda i: (i, 0)),
                  pl.BlockSpec((D_MODEL, D_MODEL), lambda i: (0, 0)), halfrow,
                  pl.BlockSpec((None, 1, ATTN_WIDTH), lambda i: (l, 0, 0)), halfrow, halfrow],
        out_specs=[halfrow, halfrow, pl.BlockSpec((8, ATTN_WIDTH), lambda i: (0, 0)),
                   pl.BlockSpec((D_MODEL, D_MODEL), lambda i: (0, 0))],
        out_shape=[jax.ShapeDtypeStruct((s, CONV_WIDTH), F32), jax.ShapeDtypeStruct((s, ATTN_WIDTH), F32),
                   jax.ShapeDtypeStruct((8, ATTN_WIDTH), F32), jax.ShapeDtypeStruct((D_MODEL, D_MODEL), BF16)],
        scratch=[pltpu.VMEM((D_MODEL, D_MODEL), F32)],
        args=[dz, wout, o, g3, ync, yna])


def _conv_bwd(pc, dyc, wc, g3, dq, dk, dv, l, s, tr):
    hb = tr // 8
    nt = s // tr
    ext = tr + 16
    last_hb = s // 8 - 1

    def body(pc_ref, prev_ref, next_ref, dy_ref, dyn_ref, wc_ref, g_ref, dq_ref, dk_ref, dv_ref,
             dpc_ref, dw_ref, dg_ref):
        i = pl.program_id(0)
        for part, ref in enumerate((dq_ref, dk_ref, dv_ref)):
            c = 3 * CONV_WIDTH + ATTN_WIDTH * part
            dpc_ref[:, c:c + ATTN_WIDTH] = ref[...]
        gmat = _group_matrix()
        row = lax.broadcasted_iota(jnp.int32, (ext, 128), 0) + (i * tr - 8)
        inside = jnp.where(row >= 0, jnp.where(row < s, 1, 0), 0) == 1

        @pl.when(i == 0)
        def _():
            dw_ref[...] = jnp.zeros(dw_ref.shape, F32)
            dg_ref[...] = jnp.zeros(dg_ref.shape, F32)

        def extend(ref_prev, ref_mid, ref_next, c):
            if ref_prev is None:
                before = jnp.zeros((8, 128), F32)
            else:
                before = ref_prev[:, c:c + 128].astype(F32)[ref_prev.shape[0] - 8:]
            after = ref_next[:, c:c + 128].astype(F32)[0:8]
            return jnp.concatenate([before, ref_mid[:, c:c + 128].astype(F32), after], axis=0)

        for j in range(CONV_WIDTH // 128):
            c0, c1, c2 = 128 * j, CONV_WIDTH + 128 * j, 2 * CONV_WIDTH + 128 * j
            hc = extend(prev_ref, pc_ref, next_ref, c0)
            bg = extend(prev_ref, pc_ref, next_ref, c1)
            cg = extend(prev_ref, pc_ref, next_ref, c2)
            dyn = extend(None, dy_ref, dyn_ref, c0)
            w0, w1, w2 = (wc_ref[0:1, c0:c0 + 128], wc_ref[1:2, c0:c0 + 128], wc_ref[2:3, c0:c0 + 128])
            gain = g_ref[:, c0:c0 + 128]
            u = jnp.where(inside, cg * hc, 0.0)
            u1 = pltpu.roll(u, 1, 0)
            u2 = pltpu.roll(u, 2, 0)
            out = u2 * w0 + u1 * w1 + u * w2
            yc = bg * out
            r = lax.rsqrt(_group_mean(yc * yc, gmat) + EPS)
            ycn = yc * r
            gd = dyn * gain
            dyc = r * (gd - ycn * _group_mean(ycn * gd, gmat))
            dout = jnp.where(inside, dyc * bg, 0.0)
            du = dout * w2 + pltpu.roll(dout, ext - 1, 0) * w1 + pltpu.roll(dout, ext - 2, 0) * w0
            sl = slice(8, 8 + tr)
            dpc_ref[:, c0:c0 + 128] = (du[sl] * cg[sl]).astype(BF16)
            dpc_ref[:, c1:c1 + 128] = (dyc[sl] * out[sl]).astype(BF16)
            dpc_ref[:, c2:c2 + 128] = (du[sl] * hc[sl]).astype(BF16)
            dw_ref[0:1, c0:c0 + 128] += jnp.sum(dout[sl] * u2[sl], axis=0, keepdims=True)
            dw_ref[1:2, c0:c0 + 128] += jnp.sum(dout[sl] * u1[sl], axis=0, keepdims=True)
            dw_ref[2:3, c0:c0 + 128] += jnp.sum(dout[sl] * u[sl], axis=0, keepdims=True)
            dg_ref[0:1, c0:c0 + 128] += jnp.sum(dyn[sl] * ycn[sl], axis=0, keepdims=True)

    wide = 3 * CONV_WIDTH
    return _call(
        body, name="conv_bwd", grid=(nt,),
        in_specs=[pl.BlockSpec((tr, wide), lambda i: (i, 0)),
                  pl.BlockSpec((16, wide), lambda i: (jnp.maximum(i * (hb // 2) - 1, 0), 0)),
                  pl.BlockSpec((16, wide), lambda i: (jnp.minimum((i + 1) * (hb // 2), last_hb // 2), 0)),
                  pl.BlockSpec((tr, CONV_WIDTH), lambda i: (i, 0)),
                  pl.BlockSpec((8, CONV_WIDTH), lambda i: (jnp.minimum((i + 1) * hb, last_hb), 0)),
                  pl.BlockSpec((None, 8, CONV_WIDTH), lambda i: (l, 0, 0)),
                  pl.BlockSpec((None, 1, CONV_WIDTH), lambda i: (l, 0, 0)),
                  pl.BlockSpec((tr, ATTN_WIDTH), lambda i: (i, 0)),
                  pl.BlockSpec((tr, ATTN_WIDTH), lambda i: (i, 0)),
                  pl.BlockSpec((tr, ATTN_WIDTH), lambda i: (i, 0))],
        out_specs=[pl.BlockSpec((tr, PROJ_WIDTH), lambda i: (i, 0)),
                   pl.BlockSpec((8, CONV_WIDTH), lambda i: (0, 0)),
                   pl.BlockSpec((8, CONV_WIDTH), lambda i: (0, 0))],
        out_shape=[jax.ShapeDtypeStruct((s, PROJ_WIDTH), BF16), jax.ShapeDtypeStruct((8, CONV_WIDTH), F32),
                   jax.ShapeDtypeStruct((8, CONV_WIDTH), F32)],
        args=[pc, pc, pc, dyc, dyc, wc, g3, dq, dk, dv])


def _attn_bwd(qkvp, biasm, o, lse, do, l, s, pad, comm=None):
    nb = s // Q_BLOCK
    qb0 = pad // Q_BLOCK
    scale = HEAD_DIM ** -0.5
    wide = 128 * ATTN_PAIRS

    def body(q_ref, k_ref, v_ref, b_ref, o_ref, lse_ref, do_ref,
             dq_ref, dk_ref, dv_ref, ds_ref, dk_acc, dv_acc):
        blk = pl.program_id(1)

        @pl.when(blk == 0)
        def _():
            dk_acc[...] = jnp.zeros(dk_acc.shape, F32)
            dv_acc[...] = jnp.zeros(dv_acc.shape, F32)
            ds_ref[...] = jnp.zeros(ds_ref.shape, F32)

        koff = pl.multiple_of(blk * Q_BLOCK + (pad - LEFT), Q_BLOCK)
        lane = lax.broadcasted_iota(jnp.int32, (1, 128), 1)
        kpos = lax.broadcasted_iota(jnp.int32, (1, K_BAND), 1) + (blk * Q_BLOCK - LEFT)
        kmask = jnp.where(kpos >= 0, 0.0, NEG_INF)
        for pr in range(ATTN_PAIRS):
            ls = slice(128 * pr, 128 * (pr + 1))
            q = q_ref[:, ls]
            kb = k_ref[pl.ds(koff, K_BAND), ls]
            vb = v_ref[pl.ds(koff, K_BAND), ls]
            dov = do_ref[:, ls]
            lse_v = lse_ref[:, ls]
            prod = dov * o_ref[:, ls]
            dq_parts = []
            dk_new = jnp.zeros((128, K_BAND), F32)
            dv_new = jnp.zeros((128, K_BAND), F32)
            for hh in range(2):
                in_head = (lane >> 6) == hh
                qm = jnp.where(in_head, q, jnp.zeros_like(q)) * jnp.asarray(scale, BF16)
                dom = jnp.where(in_head, dov, 0.0).astype(BF16)
                delta = jnp.sum(jnp.where(in_head, prod, 0.0), axis=1, keepdims=True)
                lse_h = lse_v[:, HEAD_DIM * hh:HEAD_DIM * hh + 1]
                sc = _dot_nt(qm, kb) + b_ref[2 * pr + hh] + kmask
                p = jnp.exp(sc - lse_h)
                dp = _dot_nt(dom, vb)
                ds = p * (dp - delta)
                ds_ref[2 * pr + hh] += ds
                dsb = ds.astype(BF16)
                dq_parts.append(_dot(dsb, kb) * scale)
                dk_new = dk_new + _dot_tn(qm, dsb)
                dv_new = dv_new + _dot_tn(dom, p.astype(BF16))
            dq_ref[:, ls] = jnp.where(lane < HEAD_DIM, dq_parts[0], dq_parts[1]).astype(BF16)
            dk_acc[pr, :, pl.ds(koff, K_BAND)] += dk_new
            dv_acc[pr, :, pl.ds(koff, K_BAND)] += dv_new

        @pl.when(blk == nb - 1)
        def _():
            for pr in range(ATTN_PAIRS):
                ls = slice(128 * pr, 128 * (pr + 1))
                dk_ref[:, ls] = dk_acc[pr, :, pad:pad + s].T.astype(BF16)
                dv_ref[:, ls] = dv_acc[pr, :, pad:pad + s].T.astype(BF16)

    n_grp = ATTN_WIDTH // wide
    qblk = pl.BlockSpec((Q_BLOCK, wide), lambda p, b: (b, p))
    col = pl.BlockSpec((s, wide), lambda p, b: (0, p))
    shp = jax.ShapeDtypeStruct((s, ATTN_WIDTH), BF16)
    return _call(
        body, name="attn_bwd", grid=(n_grp, nb),
        in_specs=[pl.BlockSpec((Q_BLOCK, wide), lambda p, b: (qb0 + b, p)),
                  pl.BlockSpec((s + pad, wide), lambda p, b: (0, n_grp + p)),
                  pl.BlockSpec((s + pad, wide), lambda p, b: (0, 2 * n_grp + p)),
                  pl.BlockSpec((None, 2 * ATTN_PAIRS, Q_BLOCK, K_BAND), lambda p, b: (l, p, 0, 0)),
                  qblk, qblk, qblk],
        out_specs=[qblk, col, col, pl.BlockSpec((2 * ATTN_PAIRS, Q_BLOCK, K_BAND), lambda p, b: (p, 0, 0))],
        out_shape=[shp, shp, shp, jax.ShapeDtypeStruct((N_HEADS, Q_BLOCK, K_BAND), F32)],
        scratch=[pltpu.VMEM((ATTN_PAIRS, 128, s + pad), F32), pltpu.VMEM((ATTN_PAIRS, 128, s + pad), F32)],
        args=[qkvp, qkvp, qkvp, biasm, o, lse, do], comm=comm)


def _dw_in(h, dproj, s):
    def body(a_ref, b_ref, o_ref):
        acc = _dot_tn(a_ref[...], b_ref[...])
        o_ref[0] = acc[:, 0:PROJ_SHARD].astype(BF16)
        o_ref[1] = acc[:, PROJ_SHARD:2 * PROJ_SHARD].astype(BF16)

    out, = _call(
        body, name="dw_in", grid=(4,),
        in_specs=[pl.BlockSpec((s, D_MODEL), lambda n: (0, 0)),
                  pl.BlockSpec((s, 2 * PROJ_SHARD), lambda n: (0, n))],
        out_specs=[pl.BlockSpec((2, D_MODEL, PROJ_SHARD), lambda n: (n, 0, 0))],
        out_shape=[jax.ShapeDtypeStruct((N_DEV, D_MODEL, PROJ_SHARD), BF16)], args=[h, dproj])
    return out


def _in_proj_bwd(dproj, win, x, g3, dres, l, s, tm, f_prev=None, g_post3=None, comm=None):
    chain = f_prev is not None

    def body(d_ref, w_ref, x_ref, g_ref, dres_ref, *rest):
        if chain:
            f_ref, gq_ref, dx_ref, dg_ref, df_ref, dgq_ref = rest
            _zero_first((dg_ref, dgq_ref), pl.program_id(0) == 0)
        else:
            dx_ref, dg_ref = rest
            _zero_first((dg_ref,), pl.program_id(0) == 0)
        w = jnp.concatenate([w_ref[j] for j in range(N_DEV)], axis=1)
        for rs in _row_subtiles(tm, SUB_ROWS):
            dh = _dot_nt(d_ref[rs, :], w)
            dx, dyn = _norm_bwd_rows(x_ref[rs, :], g_ref[...], dh)
            dx = dres_ref[rs, :] + dx
            dx_ref[rs, :] = dx
            _add_cols(dg_ref, dyn)
            if chain:
                df, dyn2 = _norm_bwd_rows(f_ref[rs, :].astype(F32), gq_ref[...], dx)
                df_ref[rs, :] = df.astype(BF16)
                _add_cols(dgq_ref, dyn2)

    row = pl.BlockSpec((tm, D_MODEL), lambda i: (i, 0))
    dgs = pl.BlockSpec((8, D_MODEL), lambda i: (0, 0))
    in_specs = [pl.BlockSpec((tm, PROJ_WIDTH), lambda i: (i, 0)),
                pl.BlockSpec((N_DEV, D_MODEL, PROJ_SHARD), lambda i: (0, 0, 0), pipeline_mode=pl.Buffered(1)),
                row, pl.BlockSpec((None, 1, D_MODEL), lambda i: (l, 0, 0)), row]
    out_specs = [row, dgs]
    out_shape = [jax.ShapeDtypeStruct((s, D_MODEL), F32), jax.ShapeDtypeStruct((8, D_MODEL), F32)]
    args = [dproj, win, x, g3, dres]
    if chain:
        in_specs += [row, pl.BlockSpec((None, 1, D_MODEL), lambda i: (l - 1, 0, 0))]
        out_specs += [row, dgs]
        out_shape += [jax.ShapeDtypeStruct((s, D_MODEL), BF16), jax.ShapeDtypeStruct((8, D_MODEL), F32)]
        args += [f_prev, g_post3]
    return _call(body, name="in_proj_bwd", grid=(s // tm,), in_specs=in_specs, out_specs=out_specs,
                 out_shape=out_shape, args=args, comm=comm)


def _adamw(name, w, m, v, lands, owns=None, me=None):
    groups, rows, cols = w.shape
    assert len(lands) == groups
    n_part = lands[0].shape[0]
    tr = _row_tile(rows, tuple(c for c in (512, 352, 256, 176, 128, 64, 32, 16, 8) if c * cols <= 256 * 1024))
    c1 = 1.0 - ADAM_B1 ** ADAM_STEP
    c2 = 1.0 - ADAM_B2 ** ADAM_STEP
    n_own = groups if owns is not None else 0

    def body(*refs):
        if n_own:
            me_ref, refs = refs[0], refs[1:]
        w_ref, m_ref, v_ref = refs[:3]
        land_refs = refs[3:3 + groups]
        own_refs = refs[3 + groups:3 + groups + n_own]
        g_ref, d_ref, nm_ref, nv_ref = refs[3 + groups + n_own:]
        grp = pl.program_id(0)
        for gi in range(groups):
            @pl.when(grp == gi)
            def _():
                l_ref = land_refs[gi]
                g = None
                for p in range(n_part):
                    part = l_ref[p].astype(F32)
                    if n_own:
                        part = jnp.where(me_ref[0] == p, own_refs[gi][...].astype(F32), part)
                    g = part if g is None else g + part
                g_ref[...] = g
                m1 = ADAM_B1 * m_ref[...] + (1.0 - ADAM_B1) * g
                v1 = ADAM_B2 * v_ref[...] + (1.0 - ADAM_B2) * (g * g)
                nm_ref[...] = m1
                nv_ref[...] = v1
                d_ref[...] = -ADAM_LR * ((m1 / c1) / (jnp.sqrt(v1 / c2) + ADAM_EPS) + ADAM_WD * w_ref[...])

    blk = pl.BlockSpec((None, tr, cols), lambda g, i, *_: (g, i, 0))
    shp = jax.ShapeDtypeStruct((groups, rows, cols), F32)

    def land_spec(gi):
        return pl.BlockSpec((n_part, tr, cols), lambda g, i, *_: (0, jnp.where(g == gi, i, 0), 0))

    def own_spec(gi):
        if owns[gi].ndim == 3:
            return pl.BlockSpec((None, tr, cols), lambda g, i, me_ref: (me_ref[0], jnp.where(g == gi, i, 0), 0))
        return pl.BlockSpec((tr, cols), lambda g, i, me_ref: (jnp.where(g == gi, i, 0), 0))

    in_specs = [blk, blk, blk] + [land_spec(gi) for gi in range(groups)] + [own_spec(gi) for gi in range(n_own)]
    args = [w, m, v] + list(lands) + (list(owns) if n_own else [])
    if not n_own:
        return _call(body, name=name, grid=(groups, rows // tr), in_specs=in_specs,
                     out_specs=[blk, blk, blk, blk], out_shape=[shp, shp, shp, shp], args=args)
    return pl.pallas_call(
        body, name=name,
        grid_spec=pltpu.PrefetchScalarGridSpec(
            num_scalar_prefetch=1, grid=(groups, rows // tr), in_specs=in_specs, out_specs=[blk, blk, blk, blk]),
        out_shape=[shp, shp, shp, shp],
        compiler_params=pltpu.CompilerParams(dimension_semantics=("arbitrary", "arbitrary"),
                                             vmem_limit_bytes=VMEM_LIMIT),
    )(me, *args)


def _pack_small(rel, gco, gao, gpm, gqm, gpf, gqf):
    n_layers = rel.shape[0]
    relp = jnp.pad(rel, ((0, 0), (0, 0), (0, REL_PAD - rel.shape[2])))
    parts = [relp.reshape(n_layers * N_HEADS * REL_PAD // 128, 128)]
    parts += [a.reshape(-1, 128) for a in (gco, gao, gpm, gqm, gpf, gqf)]
    return jnp.concatenate(parts, axis=0)


def _pack_small_grads(d_rel, parts):
    n_layers = len(d_rel)
    keys = ("gco", "gao", "gpm", "gqm", "gpf", "gqf")
    arrays = list(d_rel) + [parts[k][l] for k in keys for l in range(n_layers)] + list(parts["wc"])
    rows = 0
    plan = []
    for l in range(n_layers):
        for h in range(N_HEADS):
            for t in range(REL_PAD // 128):
                plan.append((l, (0, h), t, rows))
                rows += 1
    for ki, k in enumerate(keys):
        for l in range(n_layers):
            for t in range(parts[k][l].shape[1] // 128):
                plan.append((n_layers * (1 + ki) + l, (0,), t, rows))
                rows += 1
    for l in range(n_layers):
        for tap in range(3):
            for t in range(CONV_WIDTH // 128):
                plan.append((n_layers * (1 + len(keys)) + l, (tap,), t, rows))
                rows += 1
    total = rows + (-rows) % 8

    def body(*refs):
        o_ref = refs[-1]
        if total > rows:
            o_ref[rows:total, :] = jnp.zeros((total - rows, 128), F32)
        for op, idx, t, dst in plan:
            lanes = slice(128 * t, 128 * (t + 1))
            if len(idx) == 2:
                o_ref[dst:dst + 1, :] = refs[op][idx[0], idx[1]:idx[1] + 1, lanes]
            else:
                o_ref[dst:dst + 1, :] = refs[op][idx[0]:idx[0] + 1, lanes]

    vmem = pl.BlockSpec(memory_space=pltpu.VMEM)
    out, = _call(body, name="pack_small_grads", grid=(), in_specs=[vmem] * len(arrays), out_specs=[vmem],
                 out_shape=[jax.ShapeDtypeStruct((total, 128), F32)], args=arrays)
    return out


def _unpack_small(p, n_layers):
    n_rel = n_layers * N_HEADS * REL_PAD // 128
    rel = p[:n_rel].reshape(n_layers, N_HEADS, REL_PAD)[:, :, :2 * REL_CLIP + 1]
    outs = [rel]
    r0 = n_rel
    for width in (CONV_WIDTH, ATTN_WIDTH, D_MODEL, D_MODEL, D_MODEL, D_MODEL):
        nr = n_layers * width // 128
        outs.append(p[r0:r0 + nr].reshape(n_layers, width))
        r0 += nr
    return outs


def kernel(x, w_in, w_conv, rel_bias, g_conv_out, g_attn_out, w_out, g_pre_mix, g_post_mix, g_pre_ffn, g_post_ffn, w_ffn_in, w_ffn_out, loss_target, m_w_in, m_w_conv, m_rel_bias, m_g_conv_out, m_g_attn_out, m_w_out, m_g_pre_mix, m_g_post_mix, m_g_pre_ffn, m_g_post_ffn, m_w_ffn_in, m_w_ffn_out, v_w_in, v_w_conv, v_rel_bias, v_g_conv_out, v_g_attn_out, v_w_out, v_g_pre_mix, v_g_post_mix, v_g_pre_ffn, v_g_post_ffn, v_w_ffn_in, v_w_ffn_out):
    n_layers = w_in.shape[0]
    s = x.shape[1]
    assert x.shape == (1, s, D_MODEL) and s % 1024 == 0
    assert w_in.shape == (n_layers, D_MODEL, PROJ_SHARD) and w_ffn_in.shape == (n_layers, D_MODEL, FF_SHARD)
    tm = 512
    tq = 1024 if s >= 2048 else 512
    tf = min(1024, s)
    x0 = x.reshape(s, D_MODEL)
    target = loss_target.reshape(s, D_MODEL)
    dev = _dev_index(lax.axis_index("x"), lax.axis_index("y"), lax.axis_index("c"))

    wt_ffn_in, mt_ffn_in, vt_ffn_in = (jnp.transpose(a, (0, 2, 1)) for a in (w_ffn_in, m_w_ffn_in, v_w_ffn_in))
    local_w = [_cast_bf16(w_in, "cast_w_in"), _cast_bf16(w_out, "cast_w_out"),
               _cast_bf16(wt_ffn_in, "cast_w_ffn_in"), _cast_bf16(w_ffn_out, "cast_w_ffn_out")]
    wc_local = jnp.pad(jnp.transpose(w_conv, (0, 2, 1)).reshape(-1), (0, 1024 - n_layers * 3 * 64)).reshape(8, 128)
    biasm, win_next, wc_g = _bias_build(jnp.pad(rel_bias, ((0, 0), (0, 0), (0, REL_PAD - rel_bias.shape[2]))),
                                        comm=_Gather([(local_w[0], 0), (wc_local, None)]))
    weights = [None] * n_layers
    wc_full = wc_g.reshape(N_DEV, 1024)[:, :n_layers * 3 * 64].reshape(N_DEV, n_layers, 3, 64)
    wc_full = jnp.transpose(wc_full, (1, 2, 0, 3)).reshape(n_layers, 3, CONV_WIDTH)
    wc_full = jnp.pad(wc_full, ((0, 0), (0, 5), (0, 0)))

    g3 = {k: v.reshape(n_layers, 1, -1) for k, v in dict(
        conv=g_conv_out, attn=g_attn_out, pre_mix=g_pre_mix, post_mix=g_post_mix,
        pre_ffn=g_pre_ffn, post_ffn=g_post_ffn).items()}

    saved = []
    xl = x0
    h = _norm_cast(x0, g3["pre_mix"], 0, tm)
    for l in range(n_layers):
        win = win_next
        pc, ync, wout = _in_proj_conv(h, win, wc_full, g3["conv"], l, s, tq, comm=_Gather([(local_w[1], l)]))
        qkvp = _in_proj_qkv(h, win, s, tq)
        o, lse, yna, wfin = _attn_fwd(qkvp, biasm, g3["attn"], l, s, tq, comm=_Gather([(local_w[2], l)]))
        wout = wout.reshape(D_MODEL, D_MODEL)
        z, xm, h2 = _out_proj_fwd(ync, yna, wout, xl, g3["post_mix"], g3["pre_ffn"], l, s, tq)
        gu, act, wfout = _ffn_in_fwd(h2, wfin.reshape(4, FF_PAIR, D_MODEL), s, tf, comm=_Gather([(local_w[3], l)]))
        wfo = wfout.reshape(D_FF, D_MODEL)
        weights[l] = [win, wout, wfin.reshape(2 * D_FF, D_MODEL), wfo]
        sv = dict(x=xl, h=h, pc=pc, qkvp=qkvp, ync=ync, yna=yna, o=o, lse=lse, z=z, xm=xm, h2=h2, gu=gu, act=act)
        if l + 1 < n_layers:
            sv["f"], xl, h, win_next = _ffn_out_fwd(act, wfo, xm, g3["post_ffn"], g3["pre_mix"], l, l + 1, s, tm,
                                                    comm=_Gather([(local_w[0], l + 1)]))
        else:
            dx, sq, df, dg_post_ffn = _ffn_out_loss(act, wfo, xm, g3["post_ffn"], target, l, s, tm)
        saved.append(sv)

    loss = lax.psum(jnp.sum(sq) * (0.5 / D_MODEL), ("x", "y", "c"))

    lands = dict(win=[None] * n_layers, wout=[None] * n_layers, wfin=[None] * n_layers, wfout=[None] * n_layers)
    small = {k: [None] * n_layers for k in ("gco", "gao", "gpm", "gqm", "gpf", "gqf", "wc")}
    d_rel = [None] * n_layers
    started = []

    def start(name, keys, l, arrays):
        items = [(a, False) for a in arrays]
        send_sems, recv_sems, srcs, zones, token = _exchange_start(name + "_start", items)
        started.append((name, keys, l, items, send_sems, recv_sems, srcs, zones))
        return token[0:1, 0:1].reshape(1, 1, 1)

    for l in reversed(range(n_layers)):
        sv = saved[l]
        win, wout, wfin, wfo = weights[l]
        small["gqf"][l] = dg_post_ffn
        dgu, d_wfout = _ffn_out_bwd(df, wfo, sv["gu"], sv["act"], s, tm)
        d_wfout = d_wfout.reshape(N_DEV, FFO_SHARD, D_MODEL)
        d_wfin = _dw_ffn_in(sv["h2"], dgu, s).reshape(N_DEV, FF_SHARD, D_MODEL)
        g_pre_ffn = g3["pre_ffn"]
        if l == 0:
            g_pre_ffn = g_pre_ffn + start("exchange_ffn0", ("wfout", "wfin"), l, [d_wfout, d_wfin])
        dxm, dz, dg_pre_ffn, dg_post_mix = _ffn_in_bwd(
            dgu, wfin, sv["xm"], g_pre_ffn, dx, sv["z"], g3["post_mix"], l, s, tm)
        small["gpf"][l] = dg_pre_ffn
        small["gqm"][l] = dg_post_mix
        dyc, do, dg_attn, d_wout = _out_proj_bwd(dz, wout, sv["o"], g3["attn"], sv["ync"], sv["yna"], l, s, tq)
        d_wout = d_wout.reshape(N_DEV, D_MODEL // N_DEV, D_MODEL)
        small["gao"][l] = dg_attn
        dq, dk, dv, ds_sum = _attn_bwd(sv["qkvp"], biasm, sv["o"], sv["lse"], do, l, s, tq)
        d_rel[l] = _bias_bwd(ds_sum[None])
        dproj, dwc, dg_conv = _conv_bwd(sv["pc"], dyc, wc_full, g3["conv"], dq, dk, dv, l, s, tm)
        small["wc"][l] = dwc
        small["gco"][l] = dg_conv
        d_win = _dw_in(sv["h"], dproj, s)
        if l == 0:
            token = start("exchange_mix0", ("wout", "win"), l, [d_wout, d_win])
        else:
            token = start(f"exchange_layer{l}", ("wfout", "wfin", "wout", "win"), l, [d_wfout, d_wfin, d_wout, d_win])
        if l > 0:
            dx, dg_pre_mix, df, dg_post_ffn = _in_proj_bwd(
                dproj, win, sv["x"], g3["pre_mix"] + token, dxm, l, s, tm, f_prev=saved[l - 1]["f"],
                g_post3=g3["post_ffn"])
        else:
            dx, dg_pre_mix = _in_proj_bwd(dproj, win, sv["x"], g3["pre_mix"] + token, dxm, l, s, tm)
        small["gpm"][l] = dg_pre_mix
    grad_x = dx.reshape(1, s, D_MODEL)

    small_vec = _pack_small_grads(d_rel, small)
    small_items = [(small_vec, True)]
    small_sems = _exchange_start("exchange_small_start", small_items)

    owns = dict(win=[None] * n_layers, wout=[None] * n_layers, wfin=[None] * n_layers, wfout=[None] * n_layers)

    def wait(last, after):
        for name, keys, l, items, send_sems, recv_sems, srcs, zones in started:
            if (name == "exchange_mix0") == last:
                srcs, zones = _exchange_wait(name + "_wait", items, send_sems, recv_sems, srcs, zones, after)
                for key, src, zone in zip(keys, srcs, zones):
                    owns[key][l], lands[key][l] = src, zone

    me = dev.astype(jnp.int32).reshape(1)
    wait(False, small_sems[4])
    r_fin = [jnp.transpose(t, (0, 2, 1)) for t in _adamw(
        "adamw_w_ffn_in", wt_ffn_in, mt_ffn_in, vt_ffn_in, lands["wfin"], owns["wfin"], me)]
    r_fout = _adamw("adamw_w_ffn_out", w_ffn_out, m_w_ffn_out, v_w_ffn_out, lands["wfout"], owns["wfout"], me)
    wait(True, r_fout[0])
    r_out = _adamw("adamw_w_out", w_out, m_w_out, v_w_out, lands["wout"], owns["wout"], me)
    r_in = _adamw("adamw_w_in", w_in, m_w_in, v_w_in, lands["win"], owns["win"], me)
    (small_own,), (land_small,) = _exchange_wait(
        "exchange_small_wait", small_items, small_sems[0], small_sems[1], small_sems[2], small_sems[3], r_in[0])

    n_rep = 64 * n_layers
    rep = _adamw(
        "adamw_replicated",
        _pack_small(rel_bias, g_conv_out, g_attn_out, g_pre_mix, g_post_mix, g_pre_ffn, g_post_ffn)[None],
        _pack_small(m_rel_bias, m_g_conv_out, m_g_attn_out, m_g_pre_mix, m_g_post_mix, m_g_pre_ffn, m_g_post_ffn)[None],
        _pack_small(v_rel_bias, v_g_conv_out, v_g_attn_out, v_g_pre_mix, v_g_post_mix, v_g_pre_ffn, v_g_post_ffn)[None],
        [land_small[:, :n_rep]], [small_own[:n_rep]], me)
    rep = [_unpack_small(t[0], n_layers) for t in rep]

    wc_rows = n_layers * 3 * CONV_WIDTH // 128
    zeros_wc = jnp.zeros((1, wc_rows, 128), F32)
    g_wc_full = _adamw("sum_w_conv", zeros_wc, zeros_wc, zeros_wc, [land_small[:, n_rep:n_rep + wc_rows]],
                       [small_own[n_rep:n_rep + wc_rows]], me)[0]
    g_wc_full = g_wc_full.reshape(n_layers, 3, CONV_WIDTH)
    g_wc = lax.dynamic_slice_in_dim(g_wc_full, dev * (CONV_WIDTH // N_DEV), CONV_WIDTH // N_DEV, axis=2)
    g_wc = jnp.transpose(g_wc, (0, 2, 1))

    def tiny(a):
        flat = a.reshape(-1)
        return jnp.pad(flat, (0, (-flat.shape[0]) % 1024)).reshape(1, -1, 128)

    r_wc = _adamw("adamw_w_conv", tiny(w_conv), tiny(m_w_conv), tiny(v_w_conv), [tiny(g_wc)])
    r_wc = [t.reshape(-1)[:w_conv.size].reshape(w_conv.shape) for t in r_wc]

    def leaf(kind):
        return [r_in[kind], r_wc[kind], rep[kind][0], rep[kind][1], rep[kind][2], r_out[kind],
                rep[kind][3], rep[kind][4], rep[kind][5], rep[kind][6], r_fin[kind], r_fout[kind]]

    return (loss, grad_x, *leaf(0), *leaf(1), *leaf(2), *leaf(3))
```

```python
import math

import jax
import jax.numpy as jnp
from jax import lax
from jax.experimental import pallas as pl
from jax.experimental.pallas import tpu as pltpu

F32 = jnp.float32
BF16 = jnp.bfloat16

D_MODEL = 1024
N_DEV = 8
CHUNK = 64
N_LEFT_CHUNKS = 8
CONV_WIDTH = 512
ATTN_WIDTH = 512
HEAD_DIM = 64
N_HEADS = 8
REL_CLIP = 128
REL_PAD = 384
PROJ_WIDTH = 3072
PROJ_SHARD = PROJ_WIDTH // N_DEV
D_FF = 2816
FF_SHARD = 2 * D_FF // N_DEV
FFO_SHARD = D_FF // N_DEV
FF_PAIR = 2 * FF_SHARD
_COL_SUBTILES = (slice(0, 768), slice(768, FF_PAIR))
EPS = 1e-6
NEG_INF = -1e30
ATTN_PAIRS = 2
Q_BLOCK = 4 * CHUNK
K_BAND = Q_BLOCK + N_LEFT_CHUNKS * CHUNK
LEFT = N_LEFT_CHUNKS * CHUNK
TOEP = 1024

ADAM_LR = 0.001
ADAM_B1 = 0.9
ADAM_B2 = 0.999
ADAM_EPS = 1e-08
ADAM_WD = 0.01
ADAM_STEP = 10

VMEM_LIMIT = 52 * 1024 * 1024
SUB_ROWS = 256
MESH = pl.DeviceIdType.MESH
ANY = pl.BlockSpec(memory_space=pl.ANY)

NT = (((1,), (1,)), ((), ()))
TN = (((0,), (0,)), ((), ()))


def _dot(a, b):
    return jnp.dot(a, b, preferred_element_type=F32)


def _dot_nt(a, b):
    return lax.dot_general(a, b, NT, preferred_element_type=F32)


def _dot_tn(a, b):
    return lax.dot_general(a, b, TN, preferred_element_type=F32)


def _rstd(v):
    return lax.rsqrt(jnp.mean(v * v, axis=-1, keepdims=True) + EPS)


def _group_matrix():
    r = lax.broadcasted_iota(jnp.int32, (128, 128), 0) >> 6
    c = lax.broadcasted_iota(jnp.int32, (128, 128), 1) >> 6
    return jnp.where(r == c, 1.0, 0.0).astype(BF16)


def _group_mean(v, gmat):
    hi = v.astype(BF16)
    lo = (v - hi.astype(F32)).astype(BF16)
    return (_dot(hi, gmat) + _dot(lo, gmat)) * (1.0 / HEAD_DIM)


def _split3(v):
    hi = v.astype(BF16)
    r1 = v - hi.astype(F32)
    mid = r1.astype(BF16)
    lo = (r1 - mid.astype(F32)).astype(BF16)
    return hi, mid, lo


def _row_tile(rows, cands=(1024, 512, 704, 256, 128, 64, 32, 16)):
    for c in cands:
        if rows % c == 0:
            return c
    return rows


def _dev_index(px, py, pc):
    return 4 * px + 2 * py + pc


def _when(cond):
    if cond is True:
        return lambda fn: fn()
    return pl.when(cond)


def _phases(grid):
    def phases():
        if not grid:
            return True, True, True
        lin = pl.program_id(0)
        for a in range(1, len(grid)):
            lin = lin * grid[a] + pl.program_id(a)
        total = math.prod(grid)
        return lin == 0, lin == total - 1, lin == total - 1
    return phases


class _Gather:
    def __init__(self, items):
        self.items = items
        self.args = [a for a, _ in items]
        n = len(items)
        self.out_shape = [jax.ShapeDtypeStruct((N_DEV,) + (a.shape if lay is None else a.shape[1:]), a.dtype)
                          for a, lay in items]
        self.scratch = [pltpu.SemaphoreType.DMA((n, 7)), pltpu.SemaphoreType.DMA((n, 7)),
                        pltpu.SemaphoreType.DMA((n,))]

    def _ctx(self, ins, outs, sems):
        send_sems, recv_sems, local_sems = sems
        x, y, c = lax.axis_index("x"), lax.axis_index("y"), lax.axis_index("c")
        chips = [(1 - x, y), (x, 1 - y), (1 - x, 1 - y)]

        def src(k):
            lay = self.items[k][1]
            return ins[k] if lay is None else ins[k].at[lay]

        def copy(k, s, idx, to, from_src=False):
            return pltpu.make_async_remote_copy(
                src_ref=src(k) if from_src else outs[k].at[idx], dst_ref=outs[k].at[idx],
                send_sem=send_sems.at[k, s], recv_sem=recv_sems.at[k, s],
                device_id=to, device_id_type=MESH)

        def local(k):
            return pltpu.make_async_copy(src(k), outs[k].at[_dev_index(x, y, c)], local_sems.at[k])

        return x, y, c, chips, copy, local

    def start(self, ins, outs, sems, cond):
        n = len(self.items)

        @_when(cond)
        def _():
            x, y, c, chips, copy, local = self._ctx(ins, outs, sems)
            me = _dev_index(x, y, c)
            for k in range(n):
                local(k).start()
                copy(k, 0, me, (x, y, 1 - c), from_src=True).start()
                for j, chip in enumerate(chips):
                    copy(k, 1 + j, me, (chip[0], chip[1], c), from_src=True).start()

    def forward(self, ins, outs, sems, cond):
        n = len(self.items)

        @_when(cond)
        def _():
            x, y, c, chips, copy, local = self._ctx(ins, outs, sems)
            for j, chip in enumerate(chips):
                idx = _dev_index(chip[0], chip[1], c)
                for k in range(n):
                    copy(k, 1 + j, idx, (x, y, c)).wait_recv()
                    copy(k, 4 + j, idx, (x, y, 1 - c)).start()

    def finish(self, ins, outs, sems, cond):
        n = len(self.items)

        @_when(cond)
        def _():
            x, y, c, chips, copy, local = self._ctx(ins, outs, sems)
            me = _dev_index(x, y, c)
            for k in range(n):
                copy(k, 0, _dev_index(x, y, 1 - c), (x, y, c)).wait_recv()
            for j, chip in enumerate(chips):
                idx = _dev_index(chip[0], chip[1], 1 - c)
                for k in range(n):
                    copy(k, 4 + j, idx, (x, y, c)).wait_recv()
            for k in range(n):
                for s in range(4):
                    copy(k, s, me, (x, y, c), from_src=True).wait_send()
                for j, chip in enumerate(chips):
                    copy(k, 4 + j, _dev_index(chip[0], chip[1], c), (x, y, c)).wait_send()
                local(k).wait()


_PEER_FLIPS = [(0, 0, 1), (1, 0, 0), (0, 1, 0), (1, 1, 0), (1, 0, 1), (0, 1, 1), (1, 1, 1)]


def _call(body, *, name, grid, in_specs, out_specs, out_shape, args, scratch=(), comm=None):
    n_hi, n_ho, n_hs = len(args), len(out_shape), len(scratch)
    c_args = list(comm.args) if comm else []
    c_out = list(comm.out_shape) if comm else []
    c_scr = list(comm.scratch) if comm else []
    phases = _phases(grid)

    def kern(*refs):
        cuts = [n_hi, len(c_args), n_ho, len(c_out), n_hs, len(c_scr)]
        parts, pos = [], 0
        for n in cuts:
            parts.append(refs[pos:pos + n])
            pos += n
        hi, ci, ho, co, hs, cs = parts
        if comm:
            first, mid, last = phases()
            comm.start(ci, co, cs, first)
            comm.forward(ci, co, cs, mid)
        body(*hi, *ho, *hs)
        if comm:
            comm.finish(ci, co, cs, last)

    sem = ("arbitrary",) * len(grid) if grid else None
    return pl.pallas_call(
        kern, name=name, grid=grid,
        in_specs=list(in_specs) + [ANY] * len(c_args),
        out_specs=list(out_specs) + [ANY] * len(c_out),
        out_shape=list(out_shape) + c_out,
        scratch_shapes=list(scratch) + c_scr,
        compiler_params=pltpu.CompilerParams(dimension_semantics=sem, vmem_limit_bytes=VMEM_LIMIT),
    )(*args, *c_args)


def _comm_only(name, comm):
    return _call(lambda: None, name=name, grid=(), in_specs=[], out_specs=[], out_shape=[], args=[], comm=comm)


HBM_SPEC = pl.BlockSpec(memory_space=pltpu.HBM)
SEM_SPEC = pl.BlockSpec(memory_space=pltpu.SEMAPHORE)
SIDE_EFFECT = pltpu.SideEffectType.DATAFLOW_SIDE_EFFECTING


def _exchange_peer(x, y, c, s):
    fx, fy, fc = _PEER_FLIPS[s]
    return x ^ fx, y ^ fy, c ^ fc


def _exchange_start(name, items):
    n = len(items)
    srcs = [pltpu.with_memory_space_constraint(a, pltpu.HBM) for a, _ in items]
    land_shapes = [(N_DEV,) + (a.shape if whole else a.shape[1:]) for a, whole in items]
    lands = [pltpu.with_memory_space_constraint(lax.empty(shp, a.dtype), pltpu.HBM)
             for shp, (a, _) in zip(land_shapes, items)]

    n_sem = 7 * n

    def body(*refs):
        src_refs, land_refs = refs[:n], refs[n:2 * n]
        send_sems = refs[2 * n:2 * n + n_sem]
        recv_sems = refs[2 * n + n_sem:2 * n + 2 * n_sem]
        token = refs[-1]
        x, y, c = lax.axis_index("x"), lax.axis_index("y"), lax.axis_index("c")
        me = _dev_index(x, y, c)
        for s in range(7):
            px, py, pc = _exchange_peer(x, y, c, s)
            for k in range(n):
                src = src_refs[k] if items[k][1] else src_refs[k].at[_dev_index(px, py, pc)]
                pltpu.make_async_remote_copy(
                    src_ref=src, dst_ref=land_refs[k].at[me],
                    send_sem=send_sems[7 * k + s], recv_sem=recv_sems[7 * k + s],
                    device_id=(px, py, pc), device_id_type=MESH).start()
        token[...] = jnp.zeros(token.shape, token.dtype)

    outs = pl.pallas_call(
        body, name=name,
        out_shape=(*[pltpu.SemaphoreType.DMA(())] * (2 * n_sem),
                   *[pltpu.HBM(a.shape, a.dtype) for a in srcs],
                   *[pltpu.HBM(shp, a.dtype) for shp, a in zip(land_shapes, srcs)],
                   jax.ShapeDtypeStruct((8, 128), F32)),
        in_specs=[HBM_SPEC] * (2 * n),
        out_specs=(*[SEM_SPEC] * (2 * n_sem), *[HBM_SPEC] * (2 * n), pl.BlockSpec(memory_space=pltpu.VMEM)),
        input_output_aliases={i: 2 * n_sem + i for i in range(2 * n)},
        compiler_params=pltpu.CompilerParams(has_side_effects=SIDE_EFFECT),
    )(*srcs, *lands)
    base = 2 * n_sem
    return (list(outs[:n_sem]), list(outs[n_sem:base]), list(outs[base:base + n]),
            list(outs[base + n:base + 2 * n]), outs[-1])


def _exchange_wait(name, items, send_sems, recv_sems, srcs, lands, after):
    n = len(items)

    n_sem = 7 * n

    def body(*refs):
        src_refs, land_refs = refs[:n], refs[n:2 * n]
        send_refs = refs[2 * n:2 * n + n_sem]
        recv_refs = refs[2 * n + n_sem:2 * n + 2 * n_sem]
        x, y, c = lax.axis_index("x"), lax.axis_index("y"), lax.axis_index("c")
        for s in range(7):
            for k in range(n):
                copy = pltpu.make_async_remote_copy(
                    src_ref=src_refs[k] if items[k][1] else src_refs[k].at[0], dst_ref=land_refs[k].at[0],
                    send_sem=send_refs[7 * k + s], recv_sem=recv_refs[7 * k + s],
                    device_id=(x, y, c), device_id_type=MESH)
                copy.wait_send()
                copy.wait_recv()

    outs = pl.pallas_call(
        body, name=name,
        out_shape=(*[pltpu.HBM(a.shape, a.dtype) for a in srcs], *[pltpu.HBM(a.shape, a.dtype) for a in lands]),
        in_specs=[HBM_SPEC] * (2 * n) + [SEM_SPEC] * (2 * n_sem) + [ANY],
        out_specs=tuple([HBM_SPEC] * (2 * n)),
        input_output_aliases={i: i for i in range(2 * n)},
        compiler_params=pltpu.CompilerParams(has_side_effects=SIDE_EFFECT),
    )(*srcs, *lands, *send_sems, *recv_sems, after)
    return list(outs[:n]), list(outs[n:])


def _cast_bf16(x, name):
    shape = x.shape
    x2 = x.reshape(-1, shape[-1])
    rows, cols = x2.shape
    tr = _row_tile(rows)

    def body(x_ref, o_ref):
        o_ref[...] = x_ref[...].astype(BF16)

    blk = pl.BlockSpec((tr, cols), lambda i: (i, 0))
    out, = _call(body, name=name, grid=(rows // tr,), in_specs=[blk], out_specs=[blk],
                 out_shape=[jax.ShapeDtypeStruct((rows, cols), BF16)], args=[x2])
    return out.reshape(shape)


def _norm_cast(x, g3, l, tm):
    s = x.shape[0]

    def body(x_ref, g_ref, o_ref):
        v = x_ref[...]
        o_ref[...] = (v * _rstd(v) * g_ref[...]).astype(BF16)

    row = pl.BlockSpec((tm, D_MODEL), lambda i: (i, 0))
    out, = _call(body, name="norm_cast", grid=(s // tm,),
                 in_specs=[row, pl.BlockSpec((None, 1, D_MODEL), lambda i: (l, 0, 0))], out_specs=[row],
                 out_shape=[jax.ShapeDtypeStruct((s, D_MODEL), BF16)], args=[x, g3])
    return out


def _in_proj_qkv(h, win, s, tq):
    def body(a_ref, b_ref, o_ref):
        i = pl.program_id(0)

        @pl.when(i == 0)
        def _():
            o_ref[...] = jnp.zeros(o_ref.shape, BF16)

        @pl.when(i > 0)
        def _():
            w = jnp.concatenate([b_ref[j] for j in range(4)], axis=1)
            o_ref[...] = _dot(a_ref[...], w).astype(BF16)

    out, = _call(
        body, name="in_proj_qkv", grid=(s // tq + 1,),
        in_specs=[pl.BlockSpec((tq, D_MODEL), lambda i: (jnp.maximum(i - 1, 0), 0)),
                  pl.BlockSpec((4, D_MODEL, PROJ_SHARD), lambda i: (1, 0, 0))],
        out_specs=[pl.BlockSpec((tq, 4 * PROJ_SHARD), lambda i: (i, 0))],
        out_shape=[jax.ShapeDtypeStruct((s + tq, PROJ_WIDTH // 2), BF16)], args=[h, win])
    return out


def _in_proj_conv(h, win, wc, g3, l, s, tq, comm=None):
    def body(a_ref, b_ref, wc_ref, g_ref, pc_ref, y_ref, carry_ref, acc_ref):
        i = pl.program_id(0)
        w = jnp.concatenate([b_ref[j] for j in range(4)], axis=1)
        acc_ref[...] = _dot(a_ref[...], w)
        pc_ref[...] = acc_ref[...].astype(BF16)
        gmat = _group_matrix()
        for j in range(CONV_WIDTH // 128):
            c0, c1, c2 = 128 * j, CONV_WIDTH + 128 * j, 2 * CONV_WIDTH + 128 * j
            hc = acc_ref[:, c0:c0 + 128]
            bg = acc_ref[:, c1:c1 + 128]
            cg = acc_ref[:, c2:c2 + 128]
            u_prev = jnp.where(i > 0, carry_ref[:, c0:c0 + 128], 0.0)
            u = cg * hc
            carry_ref[:, c0:c0 + 128] = u[tq - 8:tq, :]
            full = jnp.concatenate([u_prev, u], axis=0)
            u1 = pltpu.roll(full, 1, 0)[8:]
            u2 = pltpu.roll(full, 2, 0)[8:]
            out = (u2 * wc_ref[0:1, c0:c0 + 128] + u1 * wc_ref[1:2, c0:c0 + 128]
                   + u * wc_ref[2:3, c0:c0 + 128])
            yc = bg * out
            r = lax.rsqrt(_group_mean(yc * yc, gmat) + EPS)
            y_ref[:, c0:c0 + 128] = (yc * r * g_ref[:, c0:c0 + 128]).astype(BF16)

    return _call(
        body, name="in_proj_conv", grid=(s // tq,),
        in_specs=[pl.BlockSpec((tq, D_MODEL), lambda i: (i, 0)),
                  pl.BlockSpec((4, D_MODEL, PROJ_SHARD), lambda i: (0, 0, 0)),
                  pl.BlockSpec((None, 8, CONV_WIDTH), lambda i: (l, 0, 0)),
                  pl.BlockSpec((None, 1, CONV_WIDTH), lambda i: (l, 0, 0))],
        out_specs=[pl.BlockSpec((tq, 3 * CONV_WIDTH), lambda i: (i, 0)),
                   pl.BlockSpec((tq, CONV_WIDTH), lambda i: (i, 0))],
        out_shape=[jax.ShapeDtypeStruct((s, 3 * CONV_WIDTH), BF16), jax.ShapeDtypeStruct((s, CONV_WIDTH), BF16)],
        scratch=[pltpu.VMEM((8, CONV_WIDTH), F32), pltpu.VMEM((tq, 3 * CONV_WIDTH), F32)],
        args=[h, win, wc, g3], comm=comm)


def _toeplitz_source():
    r_i = lax.broadcasted_iota(jnp.int32, (REL_PAD, TOEP), 0)
    m_i = lax.broadcasted_iota(jnp.int32, (REL_PAD, TOEP), 1)
    idx = jnp.clip((K_BAND - 1) - m_i, -REL_CLIP, REL_CLIP) + REL_CLIP
    return jnp.where(r_i == idx, 1.0, 0.0).astype(BF16)


def _bias_build(rbp, comm=None):
    n_layers = rbp.shape[0]

    def body(rb_ref, o_ref, t_ref):
        pmat = _toeplitz_source()
        hi, mid, lo = _split3(rb_ref[...])
        t_ref[...] = _dot(hi, pmat) + _dot(mid, pmat) + _dot(lo, pmat)
        shift = (CHUNK - 1) - lax.broadcasted_iota(jnp.int32, (CHUNK, TOEP), 0)
        kchunk = lax.broadcasted_iota(jnp.int32, (CHUNK, K_BAND), 1) >> 6
        for h in range(N_HEADS):
            b = jnp.broadcast_to(t_ref[pl.ds(h, 1), :], (CHUNK, TOEP))
            for bit in range(6):
                rolled = pltpu.roll(b, TOEP - (1 << bit), 1)
                b = jnp.where(((shift >> bit) & 1) == 1, rolled, b)
            for cq in range(Q_BLOCK // CHUNK):
                off = CHUNK * (Q_BLOCK // CHUNK - 1 - cq)
                band = pltpu.roll(b, TOEP - off, 1) if off else b
                dchunk = kchunk - cq
                in_band = jnp.where(dchunk >= 0, jnp.where(dchunk <= N_LEFT_CHUNKS, 1, 0), 0) == 1
                o_ref[h, CHUNK * cq:CHUNK * (cq + 1), :] = jnp.where(in_band, band[:, :K_BAND], NEG_INF)

    return _call(
        body, name="bias_build", grid=(n_layers,),
        in_specs=[pl.BlockSpec((None, N_HEADS, REL_PAD), lambda l: (l, 0, 0))],
        out_specs=[pl.BlockSpec((None, N_HEADS, Q_BLOCK, K_BAND), lambda l: (l, 0, 0, 0))],
        out_shape=[jax.ShapeDtypeStruct((n_layers, N_HEADS, Q_BLOCK, K_BAND), F32)],
        scratch=[pltpu.VMEM((N_HEADS, TOEP), F32)], args=[rbp], comm=comm)


def _bias_bwd(ds_sum):
    n_layers = ds_sum.shape[0]

    def body(ds_ref, o_ref, t_ref):
        pmat = _toeplitz_source()
        shift = (CHUNK - 1) - lax.broadcasted_iota(jnp.int32, (CHUNK, TOEP), 0)
        for h in range(N_HEADS):
            d = None
            for cq in range(Q_BLOCK // CHUNK):
                off = CHUNK * (Q_BLOCK // CHUNK - 1 - cq)
                part = jnp.concatenate([ds_ref[h, CHUNK * cq:CHUNK * (cq + 1), :],
                                        jnp.zeros((CHUNK, TOEP - K_BAND), F32)], axis=1)
                part = pltpu.roll(part, off, 1) if off else part
                d = part if d is None else d + part
            for bit in range(6):
                rolled = pltpu.roll(d, 1 << bit, 1)
                d = jnp.where(((shift >> bit) & 1) == 1, rolled, d)
            t_ref[pl.ds(h, 1), :] = jnp.sum(d, axis=0, keepdims=True)
        hi, mid, lo = _split3(t_ref[...])
        o_ref[...] = _dot_nt(hi, pmat) + _dot_nt(mid, pmat) + _dot_nt(lo, pmat)

    out, = _call(
        body, name="bias_bwd", grid=(n_layers,),
        in_specs=[pl.BlockSpec((None, N_HEADS, Q_BLOCK, K_BAND), lambda l: (l, 0, 0, 0))],
        out_specs=[pl.BlockSpec((None, N_HEADS, REL_PAD), lambda l: (l, 0, 0))],
        out_shape=[jax.ShapeDtypeStruct((n_layers, N_HEADS, REL_PAD), F32)],
        scratch=[pltpu.VMEM((N_HEADS, TOEP), F32)], args=[ds_sum])
    return out


def _attn_fwd(qkvp, biasm, g3, l, s, pad, comm=None):
    nb = s // Q_BLOCK
    qb0 = pad // Q_BLOCK
    scale = HEAD_DIM ** -0.5
    wide = 128 * ATTN_PAIRS

    def body(q_ref, k_ref, v_ref, b_ref, g_ref, o_ref, lse_ref, yn_ref):
        blk = pl.program_id(1)
        koff = pl.multiple_of(blk * Q_BLOCK + (pad - LEFT), Q_BLOCK)
        lane = lax.broadcasted_iota(jnp.int32, (1, 128), 1)
        kpos = lax.broadcasted_iota(jnp.int32, (1, K_BAND), 1) + (blk * Q_BLOCK - LEFT)
        kmask = jnp.where(kpos >= 0, 0.0, NEG_INF)
        gmat = _group_matrix()

        def step(masked):
            for pr in range(ATTN_PAIRS):
                ls = slice(128 * pr, 128 * (pr + 1))
                q = q_ref[:, ls]
                kb = k_ref[pl.ds(koff, K_BAND), ls]
                vb = v_ref[pl.ds(koff, K_BAND), ls]
                outs, lses = [], []
                for hh in range(2):
                    in_head = (lane >> 6) == hh
                    qm = jnp.where(in_head, q, jnp.zeros_like(q)) * jnp.asarray(scale, BF16)
                    sc = _dot_nt(qm, kb) + b_ref[2 * pr + hh]
                    if masked:
                        sc = sc + kmask
                    m = jnp.max(sc, axis=1, keepdims=True)
                    e = jnp.exp(sc - m)
                    den = jnp.sum(e, axis=1, keepdims=True)
                    outs.append(_dot(e.astype(BF16), vb) * (1.0 / den))
                    lses.append(m + jnp.log(den))
                first = lane < HEAD_DIM
                o = jnp.where(first, outs[0], outs[1])
                o_ref[:, ls] = o
                lse_ref[:, ls] = jnp.where(first, lses[0], lses[1])
                r = lax.rsqrt(_group_mean(o * o, gmat) + EPS)
                yn_ref[:, ls] = (o * r * g_ref[:, ls]).astype(BF16)

        pl.when(blk * Q_BLOCK < LEFT)(lambda: step(True))
        pl.when(blk * Q_BLOCK >= LEFT)(lambda: step(False))

    blk_out = pl.BlockSpec((Q_BLOCK, wide), lambda p, b: (b, p))
    n_grp = ATTN_WIDTH // wide
    return _call(
        body, name="attn_fwd", grid=(n_grp, nb),
        in_specs=[pl.BlockSpec((Q_BLOCK, wide), lambda p, b: (qb0 + b, p)),
                  pl.BlockSpec((s + pad, wide), lambda p, b: (0, n_grp + p)),
                  pl.BlockSpec((s + pad, wide), lambda p, b: (0, 2 * n_grp + p)),
                  pl.BlockSpec((None, 2 * ATTN_PAIRS, Q_BLOCK, K_BAND), lambda p, b: (l, p, 0, 0)),
                  pl.BlockSpec((None, 1, wide), lambda p, b: (l, 0, p))],
        out_specs=[blk_out, blk_out, blk_out],
        out_shape=[jax.ShapeDtypeStruct((s, ATTN_WIDTH), F32),
                   jax.ShapeDtypeStruct((s, ATTN_WIDTH), F32),
                   jax.ShapeDtypeStruct((s, ATTN_WIDTH), BF16)],
        args=[qkvp, qkvp, qkvp, biasm, g3], comm=comm)


def _out_proj_fwd(ync, yna, wout, x, g_post3, g_next3, l, s, tm):
    half = D_MODEL // 2

    def body(a1_ref, a2_ref, w_ref, x_ref, gp_ref, gn_ref, z_ref, xm_ref, h_ref):
        for rs in _row_subtiles(tm, SUB_ROWS):
            z = _dot(a1_ref[rs, :], w_ref[0:half, :]) + _dot(a2_ref[rs, :], w_ref[half:D_MODEL, :])
            z_ref[rs, :] = z.astype(BF16)
            xm = x_ref[rs, :] + z * _rstd(z) * gp_ref[...]
            xm_ref[rs, :] = xm
            h_ref[rs, :] = (xm * _rstd(xm) * gn_ref[...]).astype(BF16)

    row = pl.BlockSpec((tm, D_MODEL), lambda i: (i, 0))
    gain = pl.BlockSpec((None, 1, D_MODEL), lambda i: (l, 0, 0))
    return _call(
        body, name="out_proj_fwd", grid=(s // tm,),
        in_specs=[pl.BlockSpec((tm, half), lambda i: (i, 0)), pl.BlockSpec((tm, half), lambda i: (i, 0)),
                  pl.BlockSpec((D_MODEL, D_MODEL), lambda i: (0, 0)), row, gain, gain],
        out_specs=[row, row, row],
        out_shape=[jax.ShapeDtypeStruct((s, D_MODEL), BF16), jax.ShapeDtypeStruct((s, D_MODEL), F32),
                   jax.ShapeDtypeStruct((s, D_MODEL), BF16)],
        args=[ync, yna, wout, x, g_post3, g_next3])


def _ffn_in_fwd(h2, wfin4, s, tm, comm=None):
    def body(h_ref, wg_ref, wu_ref, gu_ref, act_ref):
        h = h_ref[...]
        for cs in _COL_SUBTILES:
            gate = _dot_nt(h, wg_ref[cs, :])
            up = _dot_nt(h, wu_ref[cs, :])
            gu_ref[0, :, cs] = gate.astype(BF16)
            gu_ref[1, :, cs] = up.astype(BF16)
            act_ref[:, cs] = (gate * (1.0 / (1.0 + jnp.exp(-gate))) * up).astype(BF16)

    return _call(
        body, name="ffn_in_fwd", grid=(2, s // tm),
        in_specs=[pl.BlockSpec((tm, D_MODEL), lambda b, i: (i, 0)),
                  pl.BlockSpec((None, FF_PAIR, D_MODEL), lambda b, i: (b, 0, 0)),
                  pl.BlockSpec((None, FF_PAIR, D_MODEL), lambda b, i: (2 + b, 0, 0))],
        out_specs=[pl.BlockSpec((2, tm, FF_PAIR), lambda b, i: (0, i, b)),
                   pl.BlockSpec((tm, FF_PAIR), lambda b, i: (i, b))],
        out_shape=[jax.ShapeDtypeStruct((2, s, D_FF), BF16), jax.ShapeDtypeStruct((s, D_FF), BF16)],
        args=[h2, wfin4, wfin4], comm=comm)


def _ffn_out_fwd(act, wfo, xm, g_post3, g_next3, l, l_next, s, tm, comm=None):
    def body(a_ref, w_ref, x_ref, gp_ref, gn_ref, f_ref, xo_ref, h_ref):
        for rs in _row_subtiles(tm, SUB_ROWS):
            f = _dot(a_ref[rs, :], w_ref[...])
            f_ref[rs, :] = f.astype(BF16)
            xo = x_ref[rs, :] + f * _rstd(f) * gp_ref[...]
            xo_ref[rs, :] = xo
            h_ref[rs, :] = (xo * _rstd(xo) * gn_ref[...]).astype(BF16)

    row = pl.BlockSpec((tm, D_MODEL), lambda i: (i, 0))
    return _call(
        body, name="ffn_out_fwd", grid=(s // tm,),
        in_specs=[pl.BlockSpec((tm, D_FF), lambda i: (i, 0)),
                  pl.BlockSpec((D_FF, D_MODEL), lambda i: (0, 0), pipeline_mode=pl.Buffered(1)), row,
                  pl.BlockSpec((None, 1, D_MODEL), lambda i: (l, 0, 0)),
                  pl.BlockSpec((None, 1, D_MODEL), lambda i: (l_next, 0, 0))],
        out_specs=[row, row, row],
        out_shape=[jax.ShapeDtypeStruct((s, D_MODEL), BF16), jax.ShapeDtypeStruct((s, D_MODEL), F32),
                   jax.ShapeDtypeStruct((s, D_MODEL), BF16)],
        args=[act, wfo, xm, g_post3, g_next3], comm=comm)


def _ffn_out_loss(act, wfo, xm, g_post3, target, l, s, tm):
    def body(a_ref, w_ref, x_ref, gp_ref, t_ref, dx_ref, sq_ref, df_ref, dg_ref):
        _zero_first((sq_ref, dg_ref), pl.program_id(0) == 0)
        for rs in _row_subtiles(tm, SUB_ROWS):
            f = _dot(a_ref[rs, :], w_ref[...])
            gain = gp_ref[...]
            err = x_ref[rs, :] + f * _rstd(f) * gain - t_ref[rs, :]
            dx = err * (1.0 / D_MODEL)
            dx_ref[rs, :] = dx
            df, dyn = _norm_bwd_rows(f, gain, dx)
            df_ref[rs, :] = df.astype(BF16)
            _add_cols(dg_ref, dyn)
            cs = jnp.sum(err * err, axis=0, keepdims=True)
            part = cs[:, 0:128]
            for k in range(1, D_MODEL // 128):
                part = part + cs[:, 128 * k:128 * (k + 1)]
            sq_ref[0:1, :] += part

    row = pl.BlockSpec((tm, D_MODEL), lambda i: (i, 0))
    return _call(
        body, name="ffn_out_loss", grid=(s // tm,),
        in_specs=[pl.BlockSpec((tm, D_FF), lambda i: (i, 0)),
                  pl.BlockSpec((D_FF, D_MODEL), lambda i: (0, 0), pipeline_mode=pl.Buffered(1)), row,
                  pl.BlockSpec((None, 1, D_MODEL), lambda i: (l, 0, 0)), row],
        out_specs=[row, pl.BlockSpec((8, 128), lambda i: (0, 0)), row, pl.BlockSpec((8, D_MODEL), lambda i: (0, 0))],
        out_shape=[jax.ShapeDtypeStruct((s, D_MODEL), F32), jax.ShapeDtypeStruct((8, 128), F32),
                   jax.ShapeDtypeStruct((s, D_MODEL), BF16), jax.ShapeDtypeStruct((8, D_MODEL), F32)],
        args=[act, wfo, xm, g_post3, target])


def _norm_bwd_rows(v, g, dy):
    r = _rstd(v)
    vn = v * r
    gd = dy * g
    dv = r * (gd - vn * jnp.mean(vn * gd, axis=-1, keepdims=True))
    return dv, dy * vn


def _zero_first(refs, first):
    @pl.when(first)
    def _():
        for ref in refs:
            ref[...] = jnp.zeros(ref.shape, F32)


def _add_cols(ref, val):
    ref[0:1, :] += jnp.sum(val, axis=0, keepdims=True)


def _accum_cols(ref, val, first):
    _zero_first((ref,), first)
    _add_cols(ref, val)


def _row_subtiles(rows, sub):
    sub = min(sub, rows)
    return [slice(r, r + sub) for r in range(0, rows, sub)]


def _ffn_out_bwd(df, wfo, gu, act, s, tm, comm=None):
    nm = s // tm

    def body(df_ref, w_ref, gu_ref, act_ref, dgu_ref, dw_ref, acc_ref):
        i = pl.program_id(1)
        df = df_ref[...]
        _zero_first((acc_ref,), i == 0)
        acc_ref[...] += _dot_tn(act_ref[...], df)

        @pl.when(i == nm - 1)
        def _():
            dw_ref[...] = acc_ref[...].astype(BF16)

        for cs in _COL_SUBTILES:
            da = _dot_nt(df, w_ref[cs, :])
            g = gu_ref[0, :, cs].astype(F32)
            u = gu_ref[1, :, cs].astype(F32)
            sg = 1.0 / (1.0 + jnp.exp(-g))
            dgu_ref[0, :, cs] = (da * u * (sg * (1.0 + g * (1.0 - sg)))).astype(BF16)
            dgu_ref[1, :, cs] = (da * (g * sg)).astype(BF16)

    blk = pl.BlockSpec((2, tm, FF_PAIR), lambda b, i: (0, i, b))
    wblk = pl.BlockSpec((FF_PAIR, D_MODEL), lambda b, i: (b, 0))
    return _call(
        body, name="ffn_out_bwd", grid=(2, nm),
        in_specs=[pl.BlockSpec((tm, D_MODEL), lambda b, i: (i, 0)), wblk, blk,
                  pl.BlockSpec((tm, FF_PAIR), lambda b, i: (i, b))],
        out_specs=[blk, wblk],
        out_shape=[jax.ShapeDtypeStruct((2, s, D_FF), BF16), jax.ShapeDtypeStruct((D_FF, D_MODEL), BF16)],
        scratch=[pltpu.VMEM((FF_PAIR, D_MODEL), F32)],
        args=[df, wfo, gu, act], comm=comm)


def _dw_ffn_in(h2, dgu, s):
    def body(a_ref, b_ref, o_ref):
        o_ref[...] = _dot_tn(b_ref[...], a_ref[...]).astype(BF16)

    out, = _call(
        body, name="dw_ffn_in", grid=(4,),
        in_specs=[pl.BlockSpec((s, D_MODEL), lambda n: (0, 0), pipeline_mode=pl.Buffered(1)),
                  pl.BlockSpec((None, s, FF_PAIR), lambda n: (n // 2, 0, n % 2))],
        out_specs=[pl.BlockSpec((None, FF_PAIR, D_MODEL), lambda n: (n, 0, 0))],
        out_shape=[jax.ShapeDtypeStruct((4, FF_PAIR, D_MODEL), BF16)], args=[h2, dgu])
    return out


def _ffn_in_bwd(dgu, wfin, xm, g_pre3, dres, z, g_post3, l, s, tm, comm=None):
    def body(d_ref, w_ref, xm_ref, gp_ref, dres_ref, z_ref, gq_ref, dxm_ref, dz_ref, dgp_ref, dgq_ref):
        _zero_first((dgp_ref, dgq_ref), pl.program_id(0) == 0)
        for rs in _row_subtiles(tm, SUB_ROWS):
            dh = _dot(d_ref[0, rs, :], w_ref[0:D_FF, :]) + _dot(d_ref[1, rs, :], w_ref[D_FF:2 * D_FF, :])
            dx, dyn = _norm_bwd_rows(xm_ref[rs, :], gp_ref[...], dh)
            dxm = dres_ref[rs, :] + dx
            dxm_ref[rs, :] = dxm
            _add_cols(dgp_ref, dyn)
            dz, dyn2 = _norm_bwd_rows(z_ref[rs, :].astype(F32), gq_ref[...], dxm)
            dz_ref[rs, :] = dz.astype(BF16)
            _add_cols(dgq_ref, dyn2)

    row = pl.BlockSpec((tm, D_MODEL), lambda i: (i, 0))
    gain = pl.BlockSpec((None, 1, D_MODEL), lambda i: (l, 0, 0))
    dgs = pl.BlockSpec((8, D_MODEL), lambda i: (0, 0))
    return _call(
        body, name="ffn_in_bwd", grid=(s // tm,),
        in_specs=[pl.BlockSpec((2, tm, D_FF), lambda i: (0, i, 0)),
                  pl.BlockSpec((2 * D_FF, D_MODEL), lambda i: (0, 0), pipeline_mode=pl.Buffered(1)),
                  row, gain, row, row, gain],
        out_specs=[row, row, dgs, dgs],
        out_shape=[jax.ShapeDtypeStruct((s, D_MODEL), F32), jax.ShapeDtypeStruct((s, D_MODEL), BF16),
                   jax.ShapeDtypeStruct((8, D_MODEL), F32), jax.ShapeDtypeStruct((8, D_MODEL), F32)],
        args=[dgu, wfin, xm, g_pre3, dres, z, g_post3], comm=comm)


def _out_proj_bwd(dz, wout, o, g3, ync, yna, l, s, tm):
    nm = s // tm
    half = D_MODEL // 2

    def body(dz_ref, w_ref, o_ref, g_ref, a1_ref, a2_ref, dyc_ref, do_ref, dg_ref, dw_ref, acc_ref):
        i = pl.program_id(0)
        gmat = _group_matrix()
        _zero_first((dg_ref, acc_ref), i == 0)
        dzv = dz_ref[...]
        acc_ref[0:half, :] += _dot_tn(a1_ref[...], dzv)
        acc_ref[half:D_MODEL, :] += _dot_tn(a2_ref[...], dzv)

        @pl.when(i == nm - 1)
        def _():
            dw_ref[...] = acc_ref[...].astype(BF16)

        for rs in _row_subtiles(tm, SUB_ROWS):
            dy = _dot_nt(dz_ref[rs, :], w_ref[...])
            dyc_ref[rs, :] = dy[:, 0:CONV_WIDTH]
            for j in range(ATTN_WIDTH // 128):
                c0 = 128 * j
                ov = o_ref[rs, c0:c0 + 128]
                dyn = dy[:, CONV_WIDTH + c0:CONV_WIDTH + c0 + 128]
                r = lax.rsqrt(_group_mean(ov * ov, gmat) + EPS)
                on = ov * r
                gd = dyn * g_ref[:, c0:c0 + 128]
                do_ref[rs, c0:c0 + 128] = r * (gd - on * _group_mean(on * gd, gmat))
                dg_ref[0:1, c0:c0 + 128] += jnp.sum(dyn * on, axis=0, keepdims=True)

    halfrow = pl.BlockSpec((tm, ATTN_WIDTH), lambda i: (i, 0))
    return _call(
        body, name="out_proj_bwd", grid=(nm,),
        in_specs=[pl.BlockSpec((tm, D_MODEL), lambda i: (i, 0)),
                  pl.BlockSpec((D_MODEL, D_MODEL), lambda i: (0, 0)), halfrow,
                  pl.BlockSpec((None, 1, ATTN_WIDTH), lambda i: (l, 0, 0)), halfrow, halfrow],
        out_specs=[halfrow, halfrow, pl.BlockSpec((8, ATTN_WIDTH), lambda i: (0, 0)),
                   pl.BlockSpec((D_MODEL, D_MODEL), lambda i: (0, 0))],
        out_shape=[jax.ShapeDtypeStruct((s, CONV_WIDTH), F32), jax.ShapeDtypeStruct((s, ATTN_WIDTH), F32),
                   jax.ShapeDtypeStruct((8, ATTN_WIDTH), F32), jax.ShapeDtypeStruct((D_MODEL, D_MODEL), BF16)],
        scratch=[pltpu.VMEM((D_MODEL, D_MODEL), F32)],
        args=[dz, wout, o, g3, ync, yna])


def _conv_bwd(pc, dyc, wc, g3, dq, dk, dv, l, s, tr):
    hb = tr // 8
    nt = s // tr
    ext = tr + 16
    last_hb = s // 8 - 1

    def body(pc_ref, prev_ref, next_ref, dy_ref, dyn_ref, wc_ref, g_ref, dq_ref, dk_ref, dv_ref,
             dpc_ref, dw_ref, dg_ref):
        i = pl.program_id(0)
        for part, ref in enumerate((dq_ref, dk_ref, dv_ref)):
            c = 3 * CONV_WIDTH + ATTN_WIDTH * part
            dpc_ref[:, c:c + ATTN_WIDTH] = ref[...]
        gmat = _group_matrix()
        row = lax.broadcasted_iota(jnp.int32, (ext, 128), 0) + (i * tr - 8)
        inside = jnp.where(row >= 0, jnp.where(row < s, 1, 0), 0) == 1

        @pl.when(i == 0)
        def _():
            dw_ref[...] = jnp.zeros(dw_ref.shape, F32)
            dg_ref[...] = jnp.zeros(dg_ref.shape, F32)

        def extend(ref_prev, ref_mid, ref_next, c):
            if ref_prev is None:
                before = jnp.zeros((8, 128), F32)
            else:
                before = ref_prev[:, c:c + 128].astype(F32)[ref_prev.shape[0] - 8:]
            after = ref_next[:, c:c + 128].astype(F32)[0:8]
            return jnp.concatenate([before, ref_mid[:, c:c + 128].astype(F32), after], axis=0)

        for j in range(CONV_WIDTH // 128):
            c0, c1, c2 = 128 * j, CONV_WIDTH + 128 * j, 2 * CONV_WIDTH + 128 * j
            hc = extend(prev_ref, pc_ref, next_ref, c0)
            bg = extend(prev_ref, pc_ref, next_ref, c1)
            cg = extend(prev_ref, pc_ref, next_ref, c2)
            dyn = extend(None, dy_ref, dyn_ref, c0)
            w0, w1, w2 = (wc_ref[0:1, c0:c0 + 128], wc_ref[1:2, c0:c0 + 128], wc_ref[2:3, c0:c0 + 128])
            gain = g_ref[:, c0:c0 + 128]
            u = jnp.where(inside, cg * hc, 0.0)
            u1 = pltpu.roll(u, 1, 0)
            u2 = pltpu.roll(u, 2, 0)
            out = u2 * w0 + u1 * w1 + u * w2
            yc = bg * out
            r = lax.rsqrt(_group_mean(yc * yc, gmat) + EPS)
            ycn = yc * r
            gd = dyn * gain
            dyc = r * (gd - ycn * _group_mean(ycn * gd, gmat))
            dout = jnp.where(inside, dyc * bg, 0.0)
            du = dout * w2 + pltpu.roll(dout, ext - 1, 0) * w1 + pltpu.roll(dout, ext - 2, 0) * w0
            sl = slice(8, 8 + tr)
            dpc_ref[:, c0:c0 + 128] = (du[sl] * cg[sl]).astype(BF16)
            dpc_ref[:, c1:c1 + 128] = (dyc[sl] * out[sl]).astype(BF16)
            dpc_ref[:, c2:c2 + 128] = (du[sl] * hc[sl]).astype(BF16)
            dw_ref[0:1, c0:c0 + 128] += jnp.sum(dout[sl] * u2[sl], axis=0, keepdims=True)
            dw_ref[1:2, c0:c0 + 128] += jnp.sum(dout[sl] * u1[sl], axis=0, keepdims=True)
            dw_ref[2:3, c0:c0 + 128] += jnp.sum(dout[sl] * u[sl], axis=0, keepdims=True)
            dg_ref[0:1, c0:c0 + 128] += jnp.sum(dyn[sl] * ycn[sl], axis=0, keepdims=True)

    wide = 3 * CONV_WIDTH
    return _call(
        body, name="conv_bwd", grid=(nt,),
        in_specs=[pl.BlockSpec((tr, wide), lambda i: (i, 0)),
                  pl.BlockSpec((16, wide), lambda i: (jnp.maximum(i * (hb // 2) - 1, 0), 0)),
                  pl.BlockSpec((16, wide), lambda i: (jnp.minimum((i + 1) * (hb // 2), last_hb // 2), 0)),
                  pl.BlockSpec((tr, CONV_WIDTH), lambda i: (i, 0)),
                  pl.BlockSpec((8, CONV_WIDTH), lambda i: (jnp.minimum((i + 1) * hb, last_hb), 0)),
                  pl.BlockSpec((None, 8, CONV_WIDTH), lambda i: (l, 0, 0)),
                  pl.BlockSpec((None, 1, CONV_WIDTH), lambda i: (l, 0, 0)),
                  pl.BlockSpec((tr, ATTN_WIDTH), lambda i: (i, 0)),
                  pl.BlockSpec((tr, ATTN_WIDTH), lambda i: (i, 0)),
                  pl.BlockSpec((tr, ATTN_WIDTH), lambda i: (i, 0))],
        out_specs=[pl.BlockSpec((tr, PROJ_WIDTH), lambda i: (i, 0)),
                   pl.BlockSpec((8, CONV_WIDTH), lambda i: (0, 0)),
                   pl.BlockSpec((8, CONV_WIDTH), lambda i: (0, 0))],
        out_shape=[jax.ShapeDtypeStruct((s, PROJ_WIDTH), BF16), jax.ShapeDtypeStruct((8, CONV_WIDTH), F32),
                   jax.ShapeDtypeStruct((8, CONV_WIDTH), F32)],
        args=[pc, pc, pc, dyc, dyc, wc, g3, dq, dk, dv])


def _attn_bwd(qkvp, biasm, o, lse, do, l, s, pad, comm=None):
    nb = s // Q_BLOCK
    qb0 = pad // Q_BLOCK
    scale = HEAD_DIM ** -0.5
    wide = 128 * ATTN_PAIRS

    def body(q_ref, k_ref, v_ref, b_ref, o_ref, lse_ref, do_ref,
             dq_ref, dk_ref, dv_ref, ds_ref, dk_acc, dv_acc):
        blk = pl.program_id(1)

        @pl.when(blk == 0)
        def _():
            dk_acc[...] = jnp.zeros(dk_acc.shape, F32)
            dv_acc[...] = jnp.zeros(dv_acc.shape, F32)
            ds_ref[...] = jnp.zeros(ds_ref.shape, F32)

        koff = pl.multiple_of(blk * Q_BLOCK + (pad - LEFT), Q_BLOCK)
        lane = lax.broadcasted_iota(jnp.int32, (1, 128), 1)
        kpos = lax.broadcasted_iota(jnp.int32, (1, K_BAND), 1) + (blk * Q_BLOCK - LEFT)
        kmask = jnp.where(kpos >= 0, 0.0, NEG_INF)

        def step(masked):
            for pr in range(ATTN_PAIRS):
                ls = slice(128 * pr, 128 * (pr + 1))
                q = q_ref[:, ls]
                kb = k_ref[pl.ds(koff, K_BAND), ls]
                vb = v_ref[pl.ds(koff, K_BAND), ls]
                dov = do_ref[:, ls]
                lse_v = lse_ref[:, ls]
                prod = dov * o_ref[:, ls]
                dq_parts = []
                dk_new = jnp.zeros((128, K_BAND), F32)
                dv_new = jnp.zeros((128, K_BAND), F32)
                for hh in range(2):
                    in_head = (lane >> 6) == hh
                    qm = jnp.where(in_head, q, jnp.zeros_like(q)) * jnp.asarray(scale, BF16)
                    dom = jnp.where(in_head, dov, 0.0).astype(BF16)
                    delta = jnp.sum(jnp.where(in_head, prod, 0.0), axis=1, keepdims=True)
                    lse_h = lse_v[:, HEAD_DIM * hh:HEAD_DIM * hh + 1]
                    sc = _dot_nt(qm, kb) + b_ref[2 * pr + hh]
                    if masked:
                        sc = sc + kmask
                    p = jnp.exp(sc - lse_h)
                    dp = _dot_nt(dom, vb)
                    ds = p * (dp - delta)
                    ds_ref[2 * pr + hh] += ds
                    dsb = ds.astype(BF16)
                    dq_parts.append(_dot(dsb, kb) * scale)
                    dk_new = dk_new + _dot_tn(qm, dsb)
                    dv_new = dv_new + _dot_tn(dom, p.astype(BF16))
                dq_ref[:, ls] = jnp.where(lane < HEAD_DIM, dq_parts[0], dq_parts[1]).astype(BF16)
                dk_acc[pr, :, pl.ds(koff, K_BAND)] += dk_new
                dv_acc[pr, :, pl.ds(koff, K_BAND)] += dv_new

        pl.when(blk * Q_BLOCK < LEFT)(lambda: step(True))
        pl.when(blk * Q_BLOCK >= LEFT)(lambda: step(False))

        @pl.when(blk == nb - 1)
        def _():
            for pr in range(ATTN_PAIRS):
                ls = slice(128 * pr, 128 * (pr + 1))
                dk_ref[:, ls] = dk_acc[pr, :, pad:pad + s].T.astype(BF16)
                dv_ref[:, ls] = dv_acc[pr, :, pad:pad + s].T.astype(BF16)

    n_grp = ATTN_WIDTH // wide
    qblk = pl.BlockSpec((Q_BLOCK, wide), lambda p, b: (b, p))
    col = pl.BlockSpec((s, wide), lambda p, b: (0, p))
    shp = jax.ShapeDtypeStruct((s, ATTN_WIDTH), BF16)
    return _call(
        body, name="attn_bwd", grid=(n_grp, nb),
        in_specs=[pl.BlockSpec((Q_BLOCK, wide), lambda p, b: (qb0 + b, p)),
                  pl.BlockSpec((s + pad, wide), lambda p, b: (0, n_grp + p)),
                  pl.BlockSpec((s + pad, wide), lambda p, b: (0, 2 * n_grp + p)),
                  pl.BlockSpec((None, 2 * ATTN_PAIRS, Q_BLOCK, K_BAND), lambda p, b: (l, p, 0, 0)),
                  qblk, qblk, qblk],
        out_specs=[qblk, col, col, pl.BlockSpec((2 * ATTN_PAIRS, Q_BLOCK, K_BAND), lambda p, b: (p, 0, 0))],
        out_shape=[shp, shp, shp, jax.ShapeDtypeStruct((N_HEADS, Q_BLOCK, K_BAND), F32)],
        scratch=[pltpu.VMEM((ATTN_PAIRS, 128, s + pad), F32), pltpu.VMEM((ATTN_PAIRS, 128, s + pad), F32)],
        args=[qkvp, qkvp, qkvp, biasm, o, lse, do], comm=comm)


def _dw_in(h, dproj, s):
    def body(a_ref, b_ref, o_ref):
        acc = _dot_tn(a_ref[...], b_ref[...])
        o_ref[0] = acc[:, 0:PROJ_SHARD].astype(BF16)
        o_ref[1] = acc[:, PROJ_SHARD:2 * PROJ_SHARD].astype(BF16)

    out, = _call(
        body, name="dw_in", grid=(4,),
        in_specs=[pl.BlockSpec((s, D_MODEL), lambda n: (0, 0)),
                  pl.BlockSpec((s, 2 * PROJ_SHARD), lambda n: (0, n))],
        out_specs=[pl.BlockSpec((2, D_MODEL, PROJ_SHARD), lambda n: (n, 0, 0))],
        out_shape=[jax.ShapeDtypeStruct((N_DEV, D_MODEL, PROJ_SHARD), BF16)], args=[h, dproj])
    return out


def _in_proj_bwd(dproj, win, x, g3, dres, l, s, tm, f_prev=None, g_post3=None, comm=None):
    chain = f_prev is not None

    def body(d_ref, w_ref, x_ref, g_ref, dres_ref, *rest):
        if chain:
            f_ref, gq_ref, dx_ref, dg_ref, df_ref, dgq_ref = rest
            _zero_first((dg_ref, dgq_ref), pl.program_id(0) == 0)
        else:
            dx_ref, dg_ref = rest
            _zero_first((dg_ref,), pl.program_id(0) == 0)
        w = jnp.concatenate([w_ref[j] for j in range(N_DEV)], axis=1)
        for rs in _row_subtiles(tm, SUB_ROWS):
            dh = _dot_nt(d_ref[rs, :], w)
            dx, dyn = _norm_bwd_rows(x_ref[rs, :], g_ref[...], dh)
            dx = dres_ref[rs, :] + dx
            dx_ref[rs, :] = dx
            _add_cols(dg_ref, dyn)
            if chain:
                df, dyn2 = _norm_bwd_rows(f_ref[rs, :].astype(F32), gq_ref[...], dx)
                df_ref[rs, :] = df.astype(BF16)
                _add_cols(dgq_ref, dyn2)

    row = pl.BlockSpec((tm, D_MODEL), lambda i: (i, 0))
    dgs = pl.BlockSpec((8, D_MODEL), lambda i: (0, 0))
    in_specs = [pl.BlockSpec((tm, PROJ_WIDTH), lambda i: (i, 0)),
                pl.BlockSpec((N_DEV, D_MODEL, PROJ_SHARD), lambda i: (0, 0, 0), pipeline_mode=pl.Buffered(1)),
                row, pl.BlockSpec((None, 1, D_MODEL), lambda i: (l, 0, 0)), row]
    out_specs = [row, dgs]
    out_shape = [jax.ShapeDtypeStruct((s, D_MODEL), F32), jax.ShapeDtypeStruct((8, D_MODEL), F32)]
    args = [dproj, win, x, g3, dres]
    if chain:
        in_specs += [row, pl.BlockSpec((None, 1, D_MODEL), lambda i: (l - 1, 0, 0))]
        out_specs += [row, dgs]
        out_shape += [jax.ShapeDtypeStruct((s, D_MODEL), BF16), jax.ShapeDtypeStruct((8, D_MODEL), F32)]
        args += [f_prev, g_post3]
    return _call(body, name="in_proj_bwd", grid=(s // tm,), in_specs=in_specs, out_specs=out_specs,
                 out_shape=out_shape, args=args, comm=comm)


def _adamw(name, w, m, v, lands, owns=None, me=None):
    groups, rows, cols = w.shape
    assert len(lands) == groups
    n_part = lands[0].shape[0]
    tr = _row_tile(rows, tuple(c for c in (512, 352, 256, 176, 128, 64, 32, 16, 8) if c * cols <= 256 * 1024))
    c1 = 1.0 - ADAM_B1 ** ADAM_STEP
    c2 = 1.0 - ADAM_B2 ** ADAM_STEP
    n_own = groups if owns is not None else 0

    def body(*refs):
        if n_own:
            me_ref, refs = refs[0], refs[1:]
        w_ref, m_ref, v_ref = refs[:3]
        land_refs = refs[3:3 + groups]
        own_refs = refs[3 + groups:3 + groups + n_own]
        g_ref, d_ref, nm_ref, nv_ref = refs[3 + groups + n_own:]
        grp = pl.program_id(0)
        for gi in range(groups):
            @pl.when(grp == gi)
            def _():
                l_ref = land_refs[gi]
                g = None
                for p in range(n_part):
                    part = l_ref[p].astype(F32)
                    if n_own:
                        part = jnp.where(me_ref[0] == p, own_refs[gi][...].astype(F32), part)
                    g = part if g is None else g + part
                g_ref[...] = g
                m1 = ADAM_B1 * m_ref[...] + (1.0 - ADAM_B1) * g
                v1 = ADAM_B2 * v_ref[...] + (1.0 - ADAM_B2) * (g * g)
                nm_ref[...] = m1
                nv_ref[...] = v1
                d_ref[...] = -ADAM_LR * ((m1 / c1) / (jnp.sqrt(v1 / c2) + ADAM_EPS) + ADAM_WD * w_ref[...])

    blk = pl.BlockSpec((None, tr, cols), lambda g, i, *_: (g, i, 0))
    shp = jax.ShapeDtypeStruct((groups, rows, cols), F32)

    def land_spec(gi):
        return pl.BlockSpec((n_part, tr, cols), lambda g, i, *_: (0, jnp.where(g == gi, i, 0), 0))

    def own_spec(gi):
        if owns[gi].ndim == 3:
            return pl.BlockSpec((None, tr, cols), lambda g, i, me_ref: (me_ref[0], jnp.where(g == gi, i, 0), 0))
        return pl.BlockSpec((tr, cols), lambda g, i, me_ref: (jnp.where(g == gi, i, 0), 0))

    in_specs = [blk, blk, blk] + [land_spec(gi) for gi in range(groups)] + [own_spec(gi) for gi in range(n_own)]
    args = [w, m, v] + list(lands) + (list(owns) if n_own else [])
    if not n_own:
        return _call(body, name=name, grid=(groups, rows // tr), in_specs=in_specs,
                     out_specs=[blk, blk, blk, blk], out_shape=[shp, shp, shp, shp], args=args)
    return pl.pallas_call(
        body, name=name,
        grid_spec=pltpu.PrefetchScalarGridSpec(
            num_scalar_prefetch=1, grid=(groups, rows // tr), in_specs=in_specs, out_specs=[blk, blk, blk, blk]),
        out_shape=[shp, shp, shp, shp],
        compiler_params=pltpu.CompilerParams(dimension_semantics=("arbitrary", "arbitrary"),
                                             vmem_limit_bytes=VMEM_LIMIT),
    )(me, *args)


def _pack_small(rel, gco, gao, gpm, gqm, gpf, gqf):
    n_layers = rel.shape[0]
    relp = jnp.pad(rel, ((0, 0), (0, 0), (0, REL_PAD - rel.shape[2])))
    parts = [relp.reshape(n_layers * N_HEADS * REL_PAD // 128, 128)]
    parts += [a.reshape(-1, 128) for a in (gco, gao, gpm, gqm, gpf, gqf)]
    return jnp.concatenate(parts, axis=0)


def _pack_small_grads(d_rel, parts):
    n_layers = len(d_rel)
    keys = ("gco", "gao", "gpm", "gqm", "gpf", "gqf")
    arrays = list(d_rel) + [parts[k][l] for k in keys for l in range(n_layers)] + list(parts["wc"])
    rows = 0
    plan = []
    for l in range(n_layers):
        for h in range(N_HEADS):
            for t in range(REL_PAD // 128):
                plan.append((l, (0, h), t, rows))
                rows += 1
    for ki, k in enumerate(keys):
        for l in range(n_layers):
            for t in range(parts[k][l].shape[1] // 128):
                plan.append((n_layers * (1 + ki) + l, (0,), t, rows))
                rows += 1
    for l in range(n_layers):
        for tap in range(3):
            for t in range(CONV_WIDTH // 128):
                plan.append((n_layers * (1 + len(keys)) + l, (tap,), t, rows))
                rows += 1
    total = rows + (-rows) % 8

    def body(*refs):
        o_ref = refs[-1]
        if total > rows:
            o_ref[rows:total, :] = jnp.zeros((total - rows, 128), F32)
        for op, idx, t, dst in plan:
            lanes = slice(128 * t, 128 * (t + 1))
            if len(idx) == 2:
                o_ref[dst:dst + 1, :] = refs[op][idx[0], idx[1]:idx[1] + 1, lanes]
            else:
                o_ref[dst:dst + 1, :] = refs[op][idx[0]:idx[0] + 1, lanes]

    vmem = pl.BlockSpec(memory_space=pltpu.VMEM)
    out, = _call(body, name="pack_small_grads", grid=(), in_specs=[vmem] * len(arrays), out_specs=[vmem],
                 out_shape=[jax.ShapeDtypeStruct((total, 128), F32)], args=arrays)
    return out


def _unpack_small(p, n_layers):
    n_rel = n_layers * N_HEADS * REL_PAD // 128
    rel = p[:n_rel].reshape(n_layers, N_HEADS, REL_PAD)[:, :, :2 * REL_CLIP + 1]
    outs = [rel]
    r0 = n_rel
    for width in (CONV_WIDTH, ATTN_WIDTH, D_MODEL, D_MODEL, D_MODEL, D_MODEL):
        nr = n_layers * width // 128
        outs.append(p[r0:r0 + nr].reshape(n_layers, width))
        r0 += nr
    return outs


def kernel(x, w_in, w_conv, rel_bias, g_conv_out, g_attn_out, w_out, g_pre_mix, g_post_mix, g_pre_ffn, g_post_ffn, w_ffn_in, w_ffn_out, loss_target, m_w_in, m_w_conv, m_rel_bias, m_g_conv_out, m_g_attn_out, m_w_out, m_g_pre_mix, m_g_post_mix, m_g_pre_ffn, m_g_post_ffn, m_w_ffn_in, m_w_ffn_out, v_w_in, v_w_conv, v_rel_bias, v_g_conv_out, v_g_attn_out, v_w_out, v_g_pre_mix, v_g_post_mix, v_g_pre_ffn, v_g_post_ffn, v_w_ffn_in, v_w_ffn_out):
    n_layers = w_in.shape[0]
    s = x.shape[1]
    assert x.shape == (1, s, D_MODEL) and s % 1024 == 0
    assert w_in.shape == (n_layers, D_MODEL, PROJ_SHARD) and w_ffn_in.shape == (n_layers, D_MODEL, FF_SHARD)
    tm = 512
    tq = 1024 if s >= 2048 else 512
    tf = min(1024, s)
    x0 = x.reshape(s, D_MODEL)
    target = loss_target.reshape(s, D_MODEL)
    dev = _dev_index(lax.axis_index("x"), lax.axis_index("y"), lax.axis_index("c"))

    wt_ffn_in, mt_ffn_in, vt_ffn_in = (jnp.transpose(a, (0, 2, 1)) for a in (w_ffn_in, m_w_ffn_in, v_w_ffn_in))
    local_w = [_cast_bf16(w_in, "cast_w_in"), _cast_bf16(w_out, "cast_w_out"),
               _cast_bf16(wt_ffn_in, "cast_w_ffn_in"), _cast_bf16(w_ffn_out, "cast_w_ffn_out")]
    wc_local = jnp.pad(jnp.transpose(w_conv, (0, 2, 1)).reshape(-1), (0, 1024 - n_layers * 3 * 64)).reshape(8, 128)
    biasm, win_next, wc_g = _bias_build(jnp.pad(rel_bias, ((0, 0), (0, 0), (0, REL_PAD - rel_bias.shape[2]))),
                                        comm=_Gather([(local_w[0], 0), (wc_local, None)]))
    weights = [None] * n_layers
    wc_full = wc_g.reshape(N_DEV, 1024)[:, :n_layers * 3 * 64].reshape(N_DEV, n_layers, 3, 64)
    wc_full = jnp.transpose(wc_full, (1, 2, 0, 3)).reshape(n_layers, 3, CONV_WIDTH)
    wc_full = jnp.pad(wc_full, ((0, 0), (0, 5), (0, 0)))

    g3 = {k: v.reshape(n_layers, 1, -1) for k, v in dict(
        conv=g_conv_out, attn=g_attn_out, pre_mix=g_pre_mix, post_mix=g_post_mix,
        pre_ffn=g_pre_ffn, post_ffn=g_post_ffn).items()}

    saved = []
    xl = x0
    h = _norm_cast(x0, g3["pre_mix"], 0, tm)
    for l in range(n_layers):
        win = win_next
        pc, ync, wout = _in_proj_conv(h, win, wc_full, g3["conv"], l, s, tq, comm=_Gather([(local_w[1], l)]))
        qkvp = _in_proj_qkv(h, win, s, tq)
        o, lse, yna, wfin = _attn_fwd(qkvp, biasm, g3["attn"], l, s, tq, comm=_Gather([(local_w[2], l)]))
        wout = wout.reshape(D_MODEL, D_MODEL)
        z, xm, h2 = _out_proj_fwd(ync, yna, wout, xl, g3["post_mix"], g3["pre_ffn"], l, s, tq)
        gu, act, wfout = _ffn_in_fwd(h2, wfin.reshape(4, FF_PAIR, D_MODEL), s, tf, comm=_Gather([(local_w[3], l)]))
        wfo = wfout.reshape(D_FF, D_MODEL)
        weights[l] = [win, wout, wfin.reshape(2 * D_FF, D_MODEL), wfo]
        sv = dict(x=xl, h=h, pc=pc, qkvp=qkvp, ync=ync, yna=yna, o=o, lse=lse, z=z, xm=xm, h2=h2, gu=gu, act=act)
        if l + 1 < n_layers:
            sv["f"], xl, h, win_next = _ffn_out_fwd(act, wfo, xm, g3["post_ffn"], g3["pre_mix"], l, l + 1, s, tm,
                                                    comm=_Gather([(local_w[0], l + 1)]))
        else:
            dx, sq, df, dg_post_ffn = _ffn_out_loss(act, wfo, xm, g3["post_ffn"], target, l, s, tm)
        saved.append(sv)

    loss = lax.psum(jnp.sum(sq) * (0.5 / D_MODEL), ("x", "y", "c"))

    lands = dict(win=[None] * n_layers, wout=[None] * n_layers, wfin=[None] * n_layers, wfout=[None] * n_layers)
    small = {k: [None] * n_layers for k in ("gco", "gao", "gpm", "gqm", "gpf", "gqf", "wc")}
    d_rel = [None] * n_layers
    started = []

    def start(name, keys, l, arrays):
        items = [(a, False) for a in arrays]
        send_sems, recv_sems, srcs, zones, token = _exchange_start(name + "_start", items)
        started.append((name, keys, l, items, send_sems, recv_sems, srcs, zones))
        return token[0:1, 0:1].reshape(1, 1, 1)

    for l in reversed(range(n_layers)):
        sv = saved[l]
        win, wout, wfin, wfo = weights[l]
        small["gqf"][l] = dg_post_ffn
        dgu, d_wfout = _ffn_out_bwd(df, wfo, sv["gu"], sv["act"], s, tm)
        d_wfout = d_wfout.reshape(N_DEV, FFO_SHARD, D_MODEL)
        d_wfin = _dw_ffn_in(sv["h2"], dgu, s).reshape(N_DEV, FF_SHARD, D_MODEL)
        g_pre_ffn = g3["pre_ffn"]
        if l == 0:
            g_pre_ffn = g_pre_ffn + start("exchange_ffn0", ("wfout", "wfin"), l, [d_wfout, d_wfin])
        dxm, dz, dg_pre_ffn, dg_post_mix = _ffn_in_bwd(
            dgu, wfin, sv["xm"], g_pre_ffn, dx, sv["z"], g3["post_mix"], l, s, tm)
        small["gpf"][l] = dg_pre_ffn
        small["gqm"][l] = dg_post_mix
        dyc, do, dg_attn, d_wout = _out_proj_bwd(dz, wout, sv["o"], g3["attn"], sv["ync"], sv["yna"], l, s, tq)
        d_wout = d_wout.reshape(N_DEV, D_MODEL // N_DEV, D_MODEL)
        small["gao"][l] = dg_attn
        dq, dk, dv, ds_sum = _attn_bwd(sv["qkvp"], biasm, sv["o"], sv["lse"], do, l, s, tq)
        d_rel[l] = _bias_bwd(ds_sum[None])
        dproj, dwc, dg_conv = _conv_bwd(sv["pc"], dyc, wc_full, g3["conv"], dq, dk, dv, l, s, tm)
        small["wc"][l] = dwc
        small["gco"][l] = dg_conv
        d_win = _dw_in(sv["h"], dproj, s)
        if l == 0:
            token = start("exchange_mix0", ("wout", "win"), l, [d_wout, d_win])
        else:
            token = start(f"exchange_layer{l}", ("wfout", "wfin", "wout", "win"), l, [d_wfout, d_wfin, d_wout, d_win])
        if l > 0:
            dx, dg_pre_mix, df, dg_post_ffn = _in_proj_bwd(
                dproj, win, sv["x"], g3["pre_mix"] + token, dxm, l, s, tm, f_prev=saved[l - 1]["f"],
                g_post3=g3["post_ffn"])
        else:
            dx, dg_pre_mix = _in_proj_bwd(dproj, win, sv["x"], g3["pre_mix"] + token, dxm, l, s, tm)
        small["gpm"][l] = dg_pre_mix
    grad_x = dx.reshape(1, s, D_MODEL)

    small_vec = _pack_small_grads(d_rel, small)
    small_items = [(small_vec, True)]
    small_sems = _exchange_start("exchange_small_start", small_items)

    owns = dict(win=[None] * n_layers, wout=[None] * n_layers, wfin=[None] * n_layers, wfout=[None] * n_layers)

    def wait(last, after):
        for name, keys, l, items, send_sems, recv_sems, srcs, zones in started:
            if (name == "exchange_mix0") == last:
                srcs, zones = _exchange_wait(name + "_wait", items, send_sems, recv_sems, srcs, zones, after)
                for key, src, zone in zip(keys, srcs, zones):
                    owns[key][l], lands[key][l] = src, zone

    me = dev.astype(jnp.int32).reshape(1)
    wait(False, small_sems[4])
    r_fin = [jnp.transpose(t, (0, 2, 1)) for t in _adamw(
        "adamw_w_ffn_in", wt_ffn_in, mt_ffn_in, vt_ffn_in, lands["wfin"], owns["wfin"], me)]
    r_fout = _adamw("adamw_w_ffn_out", w_ffn_out, m_w_ffn_out, v_w_ffn_out, lands["wfout"], owns["wfout"], me)
    wait(True, r_fout[0])
    r_out = _adamw("adamw_w_out", w_out, m_w_out, v_w_out, lands["wout"], owns["wout"], me)
    r_in = _adamw("adamw_w_in", w_in, m_w_in, v_w_in, lands["win"], owns["win"], me)
    (small_own,), (land_small,) = _exchange_wait(
        "exchange_small_wait", small_items, small_sems[0], small_sems[1], small_sems[2], small_sems[3], r_in[0])

    n_rep = 64 * n_layers
    rep = _adamw(
        "adamw_replicated",
        _pack_small(rel_bias, g_conv_out, g_attn_out, g_pre_mix, g_post_mix, g_pre_ffn, g_post_ffn)[None],
        _pack_small(m_rel_bias, m_g_conv_out, m_g_attn_out, m_g_pre_mix, m_g_post_mix, m_g_pre_ffn, m_g_post_ffn)[None],
        _pack_small(v_rel_bias, v_g_conv_out, v_g_attn_out, v_g_pre_mix, v_g_post_mix, v_g_pre_ffn, v_g_post_ffn)[None],
        [land_small[:, :n_rep]], [small_own[:n_rep]], me)
    rep = [_unpack_small(t[0], n_layers) for t in rep]

    wc_rows = n_layers * 3 * CONV_WIDTH // 128
    zeros_wc = jnp.zeros((1, wc_rows, 128), F32)
    g_wc_full = _adamw("sum_w_conv", zeros_wc, zeros_wc, zeros_wc, [land_small[:, n_rep:n_rep + wc_rows]],
                       [small_own[n_rep:n_rep + wc_rows]], me)[0]
    g_wc_full = g_wc_full.reshape(n_layers, 3, CONV_WIDTH)
    g_wc = lax.dynamic_slice_in_dim(g_wc_full, dev * (CONV_WIDTH // N_DEV), CONV_WIDTH // N_DEV, axis=2)
    g_wc = jnp.transpose(g_wc, (0, 2, 1))

    def tiny(a):
        flat = a.reshape(-1)
        return jnp.pad(flat, (0, (-flat.shape[0]) % 1024)).reshape(1, -1, 128)

    r_wc = _adamw("adamw_w_conv", tiny(w_conv), tiny(m_w_conv), tiny(v_w_conv), [tiny(g_wc)])
    r_wc = [t.reshape(-1)[:w_conv.size].reshape(w_conv.shape) for t in r_wc]

    def leaf(kind):
        return [r_in[kind], r_wc[kind], rep[kind][0], rep[kind][1], rep[kind][2], r_out[kind],
                rep[kind][3], rep[kind][4], rep[kind][5], rep[kind][6], r_fin[kind], r_fout[kind]]

    return (loss, grad_x, *leaf(0), *leaf(1), *leaf(2), *leaf(3))
```

```python
import math

import jax
import jax.numpy as jnp
from jax import lax
from jax.experimental import pallas as pl
from jax.experimental.pallas import tpu as pltpu

F32 = jnp.float32
BF16 = jnp.bfloat16

D_MODEL = 1024
N_DEV = 8
CHUNK = 64
N_LEFT_CHUNKS = 8
CONV_WIDTH = 512
ATTN_WIDTH = 512
HEAD_DIM = 64
N_HEADS = 8
REL_CLIP = 128
REL_PAD = 384
PROJ_WIDTH = 3072
PROJ_SHARD = PROJ_WIDTH // N_DEV
D_FF = 2816
FF_SHARD = 2 * D_FF // N_DEV
FFO_SHARD = D_FF // N_DEV
FF_PAIR = 2 * FF_SHARD
_COL_SUBTILES = (slice(0, 768), slice(768, FF_PAIR))
EPS = 1e-6
NEG_INF = -1e30
ATTN_PAIRS = 2
Q_STEP = 2
Q_BLOCK = 4 * CHUNK
K_BAND = Q_BLOCK + N_LEFT_CHUNKS * CHUNK
LEFT = N_LEFT_CHUNKS * CHUNK
TOEP = 1024

ADAM_LR = 0.001
ADAM_B1 = 0.9
ADAM_B2 = 0.999
ADAM_EPS = 1e-08
ADAM_WD = 0.01
ADAM_STEP = 10

VMEM_LIMIT = 52 * 1024 * 1024
SUB_ROWS = 256
MESH = pl.DeviceIdType.MESH
ANY = pl.BlockSpec(memory_space=pl.ANY)

NT = (((1,), (1,)), ((), ()))
TN = (((0,), (0,)), ((), ()))


def _dot(a, b):
    return jnp.dot(a, b, preferred_element_type=F32)


def _dot_nt(a, b):
    return lax.dot_general(a, b, NT, preferred_element_type=F32)


def _dot_tn(a, b):
    return lax.dot_general(a, b, TN, preferred_element_type=F32)


def _rstd(v):
    return lax.rsqrt(jnp.mean(v * v, axis=-1, keepdims=True) + EPS)


def _group_matrix():
    r = lax.broadcasted_iota(jnp.int32, (128, 128), 0) >> 6
    c = lax.broadcasted_iota(jnp.int32, (128, 128), 1) >> 6
    return jnp.where(r == c, 1.0, 0.0).astype(BF16)


def _group_mean(v, gmat):
    hi = v.astype(BF16)
    lo = (v - hi.astype(F32)).astype(BF16)
    return (_dot(hi, gmat) + _dot(lo, gmat)) * (1.0 / HEAD_DIM)


def _split3(v):
    hi = v.astype(BF16)
    r1 = v - hi.astype(F32)
    mid = r1.astype(BF16)
    lo = (r1 - mid.astype(F32)).astype(BF16)
    return hi, mid, lo


def _row_tile(rows, cands=(1024, 512, 704, 256, 128, 64, 32, 16)):
    for c in cands:
        if rows % c == 0:
            return c
    return rows


def _dev_index(px, py, pc):
    return 4 * px + 2 * py + pc


def _when(cond):
    if cond is True:
        return lambda fn: fn()
    return pl.when(cond)


def _phases(grid):
    def phases():
        if not grid:
            return True, True, True
        lin = pl.program_id(0)
        for a in range(1, len(grid)):
            lin = lin * grid[a] + pl.program_id(a)
        total = math.prod(grid)
        return lin == 0, lin == total - 1, lin == total - 1
    return phases


class _Gather:
    def __init__(self, items):
        self.items = items
        self.args = [a for a, _ in items]
        n = len(items)
        self.out_shape = [jax.ShapeDtypeStruct((N_DEV,) + (a.shape if lay is None else a.shape[1:]), a.dtype)
                          for a, lay in items]
        self.scratch = [pltpu.SemaphoreType.DMA((n, 7)), pltpu.SemaphoreType.DMA((n, 7)),
                        pltpu.SemaphoreType.DMA((n,))]

    def _ctx(self, ins, outs, sems):
        send_sems, recv_sems, local_sems = sems
        x, y, c = lax.axis_index("x"), lax.axis_index("y"), lax.axis_index("c")
        chips = [(1 - x, y), (x, 1 - y), (1 - x, 1 - y)]

        def src(k):
            lay = self.items[k][1]
            return ins[k] if lay is None else ins[k].at[lay]

        def copy(k, s, idx, to, from_src=False):
            return pltpu.make_async_remote_copy(
                src_ref=src(k) if from_src else outs[k].at[idx], dst_ref=outs[k].at[idx],
                send_sem=send_sems.at[k, s], recv_sem=recv_sems.at[k, s],
                device_id=to, device_id_type=MESH)

        def local(k):
            return pltpu.make_async_copy(src(k), outs[k].at[_dev_index(x, y, c)], local_sems.at[k])

        return x, y, c, chips, copy, local

    def start(self, ins, outs, sems, cond):
        n = len(self.items)

        @_when(cond)
        def _():
            x, y, c, chips, copy, local = self._ctx(ins, outs, sems)
            me = _dev_index(x, y, c)
            for k in range(n):
                local(k).start()
                copy(k, 0, me, (x, y, 1 - c), from_src=True).start()
                for j, chip in enumerate(chips):
                    copy(k, 1 + j, me, (chip[0], chip[1], c), from_src=True).start()

    def forward(self, ins, outs, sems, cond):
        n = len(self.items)

        @_when(cond)
        def _():
            x, y, c, chips, copy, local = self._ctx(ins, outs, sems)
            for j, chip in enumerate(chips):
                idx = _dev_index(chip[0], chip[1], c)
                for k in range(n):
                    copy(k, 1 + j, idx, (x, y, c)).wait_recv()
                    copy(k, 4 + j, idx, (x, y, 1 - c)).start()

    def finish(self, ins, outs, sems, cond):
        n = len(self.items)

        @_when(cond)
        def _():
            x, y, c, chips, copy, local = self._ctx(ins, outs, sems)
            me = _dev_index(x, y, c)
            for k in range(n):
                copy(k, 0, _dev_index(x, y, 1 - c), (x, y, c)).wait_recv()
            for j, chip in enumerate(chips):
                idx = _dev_index(chip[0], chip[1], 1 - c)
                for k in range(n):
                    copy(k, 4 + j, idx, (x, y, c)).wait_recv()
            for k in range(n):
                for s in range(4):
                    copy(k, s, me, (x, y, c), from_src=True).wait_send()
                for j, chip in enumerate(chips):
                    copy(k, 4 + j, _dev_index(chip[0], chip[1], c), (x, y, c)).wait_send()
                local(k).wait()


_PEER_FLIPS = [(0, 0, 1), (1, 0, 0), (0, 1, 0), (1, 1, 0), (1, 0, 1), (0, 1, 1), (1, 1, 1)]


def _call(body, *, name, grid, in_specs, out_specs, out_shape, args, scratch=(), comm=None):
    n_hi, n_ho, n_hs = len(args), len(out_shape), len(scratch)
    c_args = list(comm.args) if comm else []
    c_out = list(comm.out_shape) if comm else []
    c_scr = list(comm.scratch) if comm else []
    phases = _phases(grid)

    def kern(*refs):
        cuts = [n_hi, len(c_args), n_ho, len(c_out), n_hs, len(c_scr)]
        parts, pos = [], 0
        for n in cuts:
            parts.append(refs[pos:pos + n])
            pos += n
        hi, ci, ho, co, hs, cs = parts
        if comm:
            first, mid, last = phases()
            comm.start(ci, co, cs, first)
            comm.forward(ci, co, cs, mid)
        body(*hi, *ho, *hs)
        if comm:
            comm.finish(ci, co, cs, last)

    sem = ("arbitrary",) * len(grid) if grid else None
    return pl.pallas_call(
        kern, name=name, grid=grid,
        in_specs=list(in_specs) + [ANY] * len(c_args),
        out_specs=list(out_specs) + [ANY] * len(c_out),
        out_shape=list(out_shape) + c_out,
        scratch_shapes=list(scratch) + c_scr,
        compiler_params=pltpu.CompilerParams(dimension_semantics=sem, vmem_limit_bytes=VMEM_LIMIT),
    )(*args, *c_args)


def _comm_only(name, comm):
    return _call(lambda: None, name=name, grid=(), in_specs=[], out_specs=[], out_shape=[], args=[], comm=comm)


HBM_SPEC = pl.BlockSpec(memory_space=pltpu.HBM)
SEM_SPEC = pl.BlockSpec(memory_space=pltpu.SEMAPHORE)
SIDE_EFFECT = pltpu.SideEffectType.DATAFLOW_SIDE_EFFECTING


def _exchange_peer(x, y, c, s):
    fx, fy, fc = _PEER_FLIPS[s]
    return x ^ fx, y ^ fy, c ^ fc


def _exchange_start(name, items):
    n = len(items)
    srcs = [pltpu.with_memory_space_constraint(a, pltpu.HBM) for a, _ in items]
    land_shapes = [(N_DEV,) + (a.shape if whole else a.shape[1:]) for a, whole in items]
    lands = [pltpu.with_memory_space_constraint(lax.empty(shp, a.dtype), pltpu.HBM)
             for shp, (a, _) in zip(land_shapes, items)]

    n_sem = 7 * n

    def body(*refs):
        src_refs, land_refs = refs[:n], refs[n:2 * n]
        send_sems = refs[2 * n:2 * n + n_sem]
        recv_sems = refs[2 * n + n_sem:2 * n + 2 * n_sem]
        token = refs[-1]
        x, y, c = lax.axis_index("x"), lax.axis_index("y"), lax.axis_index("c")
        me = _dev_index(x, y, c)
        for s in range(7):
            px, py, pc = _exchange_peer(x, y, c, s)
            for k in range(n):
                src = src_refs[k] if items[k][1] else src_refs[k].at[_dev_index(px, py, pc)]
                pltpu.make_async_remote_copy(
                    src_ref=src, dst_ref=land_refs[k].at[me],
                    send_sem=send_sems[7 * k + s], recv_sem=recv_sems[7 * k + s],
                    device_id=(px, py, pc), device_id_type=MESH).start()
        token[...] = jnp.zeros(token.shape, token.dtype)

    outs = pl.pallas_call(
        body, name=name,
        out_shape=(*[pltpu.SemaphoreType.DMA(())] * (2 * n_sem),
                   *[pltpu.HBM(a.shape, a.dtype) for a in srcs],
                   *[pltpu.HBM(shp, a.dtype) for shp, a in zip(land_shapes, srcs)],
                   jax.ShapeDtypeStruct((8, 128), F32)),
        in_specs=[HBM_SPEC] * (2 * n),
        out_specs=(*[SEM_SPEC] * (2 * n_sem), *[HBM_SPEC] * (2 * n), pl.BlockSpec(memory_space=pltpu.VMEM)),
        input_output_aliases={i: 2 * n_sem + i for i in range(2 * n)},
        compiler_params=pltpu.CompilerParams(has_side_effects=SIDE_EFFECT),
    )(*srcs, *lands)
    base = 2 * n_sem
    return (list(outs[:n_sem]), list(outs[n_sem:base]), list(outs[base:base + n]),
            list(outs[base + n:base + 2 * n]), outs[-1])


def _exchange_wait(name, items, send_sems, recv_sems, srcs, lands, after):
    n = len(items)

    n_sem = 7 * n

    def body(*refs):
        src_refs, land_refs = refs[:n], refs[n:2 * n]
        send_refs = refs[2 * n:2 * n + n_sem]
        recv_refs = refs[2 * n + n_sem:2 * n + 2 * n_sem]
        x, y, c = lax.axis_index("x"), lax.axis_index("y"), lax.axis_index("c")
        for s in range(7):
            for k in range(n):
                copy = pltpu.make_async_remote_copy(
                    src_ref=src_refs[k] if items[k][1] else src_refs[k].at[0], dst_ref=land_refs[k].at[0],
                    send_sem=send_refs[7 * k + s], recv_sem=recv_refs[7 * k + s],
                    device_id=(x, y, c), device_id_type=MESH)
                copy.wait_send()
                copy.wait_recv()

    outs = pl.pallas_call(
        body, name=name,
        out_shape=(*[pltpu.HBM(a.shape, a.dtype) for a in srcs], *[pltpu.HBM(a.shape, a.dtype) for a in lands]),
        in_specs=[HBM_SPEC] * (2 * n) + [SEM_SPEC] * (2 * n_sem) + [ANY],
        out_specs=tuple([HBM_SPEC] * (2 * n)),
        input_output_aliases={i: i for i in range(2 * n)},
        compiler_params=pltpu.CompilerParams(has_side_effects=SIDE_EFFECT),
    )(*srcs, *lands, *send_sems, *recv_sems, after)
    return list(outs[:n]), list(outs[n:])


def _cast_bf16(x, name):
    shape = x.shape
    x2 = x.reshape(-1, shape[-1])
    rows, cols = x2.shape
    tr = _row_tile(rows)

    def body(x_ref, o_ref):
        o_ref[...] = x_ref[...].astype(BF16)

    blk = pl.BlockSpec((tr, cols), lambda i: (i, 0))
    out, = _call(body, name=name, grid=(rows // tr,), in_specs=[blk], out_specs=[blk],
                 out_shape=[jax.ShapeDtypeStruct((rows, cols), BF16)], args=[x2])
    return out.reshape(shape)


def _norm_cast(x, g3, l, tm):
    s = x.shape[0]

    def body(x_ref, g_ref, o_ref):
        v = x_ref[...]
        o_ref[...] = (v * _rstd(v) * g_ref[...]).astype(BF16)

    row = pl.BlockSpec((tm, D_MODEL), lambda i: (i, 0))
    out, = _call(body, name="norm_cast", grid=(s // tm,),
                 in_specs=[row, pl.BlockSpec((None, 1, D_MODEL), lambda i: (l, 0, 0))], out_specs=[row],
                 out_shape=[jax.ShapeDtypeStruct((s, D_MODEL), BF16)], args=[x, g3])
    return out


def _in_proj_qkv(h, win, s, tq):
    def body(a_ref, b_ref, o_ref):
        i = pl.program_id(0)

        @pl.when(i == 0)
        def _():
            o_ref[...] = jnp.zeros(o_ref.shape, BF16)

        @pl.when(i > 0)
        def _():
            w = jnp.concatenate([b_ref[j] for j in range(4)], axis=1)
            o_ref[...] = _dot(a_ref[...], w).astype(BF16)

    out, = _call(
        body, name="in_proj_qkv", grid=(s // tq + 1,),
        in_specs=[pl.BlockSpec((tq, D_MODEL), lambda i: (jnp.maximum(i - 1, 0), 0)),
                  pl.BlockSpec((4, D_MODEL, PROJ_SHARD), lambda i: (1, 0, 0))],
        out_specs=[pl.BlockSpec((tq, 4 * PROJ_SHARD), lambda i: (i, 0))],
        out_shape=[jax.ShapeDtypeStruct((s + tq, PROJ_WIDTH // 2), BF16)], args=[h, win])
    return out


def _in_proj_conv(h, win, wc, g3, l, s, tq, comm=None):
    def body(a_ref, b_ref, wc_ref, g_ref, pc_ref, y_ref, carry_ref, acc_ref):
        i = pl.program_id(0)
        w = jnp.concatenate([b_ref[j] for j in range(4)], axis=1)
        acc_ref[...] = _dot(a_ref[...], w)
        pc_ref[...] = acc_ref[...].astype(BF16)
        gmat = _group_matrix()
        for j in range(CONV_WIDTH // 128):
            c0, c1, c2 = 128 * j, CONV_WIDTH + 128 * j, 2 * CONV_WIDTH + 128 * j
            hc = acc_ref[:, c0:c0 + 128]
            bg = acc_ref[:, c1:c1 + 128]
            cg = acc_ref[:, c2:c2 + 128]
            u_prev = jnp.where(i > 0, carry_ref[:, c0:c0 + 128], 0.0)
            u = cg * hc
            carry_ref[:, c0:c0 + 128] = u[tq - 8:tq, :]
            full = jnp.concatenate([u_prev, u], axis=0)
            u1 = pltpu.roll(full, 1, 0)[8:]
            u2 = pltpu.roll(full, 2, 0)[8:]
            out = (u2 * wc_ref[0:1, c0:c0 + 128] + u1 * wc_ref[1:2, c0:c0 + 128]
                   + u * wc_ref[2:3, c0:c0 + 128])
            yc = bg * out
            r = lax.rsqrt(_group_mean(yc * yc, gmat) + EPS)
            y_ref[:, c0:c0 + 128] = (yc * r * g_ref[:, c0:c0 + 128]).astype(BF16)

    return _call(
        body, name="in_proj_conv", grid=(s // tq,),
        in_specs=[pl.BlockSpec((tq, D_MODEL), lambda i: (i, 0)),
                  pl.BlockSpec((4, D_MODEL, PROJ_SHARD), lambda i: (0, 0, 0)),
                  pl.BlockSpec((None, 8, CONV_WIDTH), lambda i: (l, 0, 0)),
                  pl.BlockSpec((None, 1, CONV_WIDTH), lambda i: (l, 0, 0))],
        out_specs=[pl.BlockSpec((tq, 3 * CONV_WIDTH), lambda i: (i, 0)),
                   pl.BlockSpec((tq, CONV_WIDTH), lambda i: (i, 0))],
        out_shape=[jax.ShapeDtypeStruct((s, 3 * CONV_WIDTH), BF16), jax.ShapeDtypeStruct((s, CONV_WIDTH), BF16)],
        scratch=[pltpu.VMEM((8, CONV_WIDTH), F32), pltpu.VMEM((tq, 3 * CONV_WIDTH), F32)],
        args=[h, win, wc, g3], comm=comm)


def _toeplitz_source():
    r_i = lax.broadcasted_iota(jnp.int32, (REL_PAD, TOEP), 0)
    m_i = lax.broadcasted_iota(jnp.int32, (REL_PAD, TOEP), 1)
    idx = jnp.clip((K_BAND - 1) - m_i, -REL_CLIP, REL_CLIP) + REL_CLIP
    return jnp.where(r_i == idx, 1.0, 0.0).astype(BF16)


def _bias_build(rbp, comm=None):
    n_layers = rbp.shape[0]

    def body(rb_ref, o_ref, t_ref):
        pmat = _toeplitz_source()
        hi, mid, lo = _split3(rb_ref[...])
        t_ref[...] = _dot(hi, pmat) + _dot(mid, pmat) + _dot(lo, pmat)
        shift = (CHUNK - 1) - lax.broadcasted_iota(jnp.int32, (CHUNK, TOEP), 0)
        kchunk = lax.broadcasted_iota(jnp.int32, (CHUNK, K_BAND), 1) >> 6
        for h in range(N_HEADS):
            b = jnp.broadcast_to(t_ref[pl.ds(h, 1), :], (CHUNK, TOEP))
            for bit in range(6):
                rolled = pltpu.roll(b, TOEP - (1 << bit), 1)
                b = jnp.where(((shift >> bit) & 1) == 1, rolled, b)
            for cq in range(Q_BLOCK // CHUNK):
                off = CHUNK * (Q_BLOCK // CHUNK - 1 - cq)
                band = pltpu.roll(b, TOEP - off, 1) if off else b
                dchunk = kchunk - cq
                in_band = jnp.where(dchunk >= 0, jnp.where(dchunk <= N_LEFT_CHUNKS, 1, 0), 0) == 1
                o_ref[h, CHUNK * cq:CHUNK * (cq + 1), :] = jnp.where(in_band, band[:, :K_BAND], NEG_INF)

    return _call(
        body, name="bias_build", grid=(n_layers,),
        in_specs=[pl.BlockSpec((None, N_HEADS, REL_PAD), lambda l: (l, 0, 0))],
        out_specs=[pl.BlockSpec((None, N_HEADS, Q_BLOCK, K_BAND), lambda l: (l, 0, 0, 0))],
        out_shape=[jax.ShapeDtypeStruct((n_layers, N_HEADS, Q_BLOCK, K_BAND), F32)],
        scratch=[pltpu.VMEM((N_HEADS, TOEP), F32)], args=[rbp], comm=comm)


def _bias_bwd(ds_sum):
    n_layers = ds_sum.shape[0]

    def body(ds_ref, o_ref, t_ref):
        pmat = _toeplitz_source()
        shift = (CHUNK - 1) - lax.broadcasted_iota(jnp.int32, (CHUNK, TOEP), 0)
        for h in range(N_HEADS):
            d = None
            for cq in range(Q_BLOCK // CHUNK):
                off = CHUNK * (Q_BLOCK // CHUNK - 1 - cq)
                part = jnp.concatenate([ds_ref[h, CHUNK * cq:CHUNK * (cq + 1), :],
                                        jnp.zeros((CHUNK, TOEP - K_BAND), F32)], axis=1)
                part = pltpu.roll(part, off, 1) if off else part
                d = part if d is None else d + part
            for bit in range(6):
                rolled = pltpu.roll(d, 1 << bit, 1)
                d = jnp.where(((shift >> bit) & 1) == 1, rolled, d)
            t_ref[pl.ds(h, 1), :] = jnp.sum(d, axis=0, keepdims=True)
        hi, mid, lo = _split3(t_ref[...])
        o_ref[...] = _dot_nt(hi, pmat) + _dot_nt(mid, pmat) + _dot_nt(lo, pmat)

    out, = _call(
        body, name="bias_bwd", grid=(n_layers,),
        in_specs=[pl.BlockSpec((None, N_HEADS, Q_BLOCK, K_BAND), lambda l: (l, 0, 0, 0))],
        out_specs=[pl.BlockSpec((None, N_HEADS, REL_PAD), lambda l: (l, 0, 0))],
        out_shape=[jax.ShapeDtypeStruct((n_layers, N_HEADS, REL_PAD), F32)],
        scratch=[pltpu.VMEM((N_HEADS, TOEP), F32)], args=[ds_sum])
    return out


def _attn_fwd(qkvp, biasm, g3, l, s, pad, comm=None):
    rows = Q_STEP * Q_BLOCK
    nb = s // rows
    qb0 = pad // rows
    scale = HEAD_DIM ** -0.5
    wide = 128 * ATTN_PAIRS

    def body(q_ref, k_ref, v_ref, b_ref, g_ref, o_ref, lse_ref, yn_ref):
        blk = pl.program_id(1)
        lane = lax.broadcasted_iota(jnp.int32, (1, 128), 1)
        gmat = _group_matrix()

        def step(masked):
            for sub in range(Q_STEP):
                rs = slice(Q_BLOCK * sub, Q_BLOCK * (sub + 1))
                start = blk * rows + Q_BLOCK * sub
                koff = pl.multiple_of(start + (pad - LEFT), Q_BLOCK)
                kpos = lax.broadcasted_iota(jnp.int32, (1, K_BAND), 1) + (start - LEFT)
                kmask = jnp.where(kpos >= 0, 0.0, NEG_INF)
                for pr in range(ATTN_PAIRS):
                    ls = slice(128 * pr, 128 * (pr + 1))
                    q = q_ref[rs, ls]
                    kb = k_ref[pl.ds(koff, K_BAND), ls]
                    vb = v_ref[pl.ds(koff, K_BAND), ls]
                    outs, lses = [], []
                    for hh in range(2):
                        in_head = (lane >> 6) == hh
                        qm = jnp.where(in_head, q, jnp.zeros_like(q)) * jnp.asarray(scale, BF16)
                        sc = _dot_nt(qm, kb) + b_ref[2 * pr + hh]
                        if masked:
                            sc = sc + kmask
                        m = jnp.max(sc, axis=1, keepdims=True)
                        e = jnp.exp(sc - m)
                        den = jnp.sum(e, axis=1, keepdims=True)
                        outs.append(_dot(e.astype(BF16), vb) * (1.0 / den))
                        lses.append(m + jnp.log(den))
                    first = lane < HEAD_DIM
                    o = jnp.where(first, outs[0], outs[1])
                    o_ref[rs, ls] = o
                    lse_ref[rs, ls] = jnp.where(first, lses[0], lses[1])
                    r = lax.rsqrt(_group_mean(o * o, gmat) + EPS)
                    yn_ref[rs, ls] = (o * r * g_ref[:, ls]).astype(BF16)

        pl.when(blk == 0)(lambda: step(True))
        pl.when(blk > 0)(lambda: step(False))

    blk_out = pl.BlockSpec((rows, wide), lambda p, b: (b, p))
    n_grp = ATTN_WIDTH // wide
    return _call(
        body, name="attn_fwd", grid=(n_grp, nb),
        in_specs=[pl.BlockSpec((rows, wide), lambda p, b: (qb0 + b, p)),
                  pl.BlockSpec((s + pad, wide), lambda p, b: (0, n_grp + p)),
                  pl.BlockSpec((s + pad, wide), lambda p, b: (0, 2 * n_grp + p)),
                  pl.BlockSpec((None, 2 * ATTN_PAIRS, Q_BLOCK, K_BAND), lambda p, b: (l, p, 0, 0)),
                  pl.BlockSpec((None, 1, wide), lambda p, b: (l, 0, p))],
        out_specs=[blk_out, blk_out, blk_out],
        out_shape=[jax.ShapeDtypeStruct((s, ATTN_WIDTH), F32),
                   jax.ShapeDtypeStruct((s, ATTN_WIDTH), F32),
                   jax.ShapeDtypeStruct((s, ATTN_WIDTH), BF16)],
        args=[qkvp, qkvp, qkvp, biasm, g3], comm=comm)


def _out_proj_fwd(ync, yna, wout, x, g_post3, g_next3, l, s, tm):
    half = D_MODEL // 2

    def body(a1_ref, a2_ref, w_ref, x_ref, gp_ref, gn_ref, z_ref, xm_ref, h_ref):
        for rs in _row_subtiles(tm, SUB_ROWS):
            z = _dot(a1_ref[rs, :], w_ref[0:half, :]) + _dot(a2_ref[rs, :], w_ref[half:D_MODEL, :])
            z_ref[rs, :] = z.astype(BF16)
            xm = x_ref[rs, :] + z * _rstd(z) * gp_ref[...]
            xm_ref[rs, :] = xm
            h_ref[rs, :] = (xm * _rstd(xm) * gn_ref[...]).astype(BF16)

    row = pl.BlockSpec((tm, D_MODEL), lambda i: (i, 0))
    gain = pl.BlockSpec((None, 1, D_MODEL), lambda i: (l, 0, 0))
    return _call(
        body, name="out_proj_fwd", grid=(s // tm,),
        in_specs=[pl.BlockSpec((tm, half), lambda i: (i, 0)), pl.BlockSpec((tm, half), lambda i: (i, 0)),
                  pl.BlockSpec((D_MODEL, D_MODEL), lambda i: (0, 0)), row, gain, gain],
        out_specs=[row, row, row],
        out_shape=[jax.ShapeDtypeStruct((s, D_MODEL), BF16), jax.ShapeDtypeStruct((s, D_MODEL), F32),
                   jax.ShapeDtypeStruct((s, D_MODEL), BF16)],
        args=[ync, yna, wout, x, g_post3, g_next3])


def _ffn_in_fwd(h2, wfin4, s, tm, comm=None):
    def body(h_ref, wg_ref, wu_ref, gu_ref, act_ref):
        h = h_ref[...]
        for cs in _COL_SUBTILES:
            gate = _dot_nt(h, wg_ref[cs, :])
            up = _dot_nt(h, wu_ref[cs, :])
            gu_ref[0, :, cs] = gate.astype(BF16)
            gu_ref[1, :, cs] = up.astype(BF16)
            act_ref[:, cs] = (gate * (1.0 / (1.0 + jnp.exp(-gate))) * up).astype(BF16)

    return _call(
        body, name="ffn_in_fwd", grid=(2, s // tm),
        in_specs=[pl.BlockSpec((tm, D_MODEL), lambda b, i: (i, 0)),
                  pl.BlockSpec((None, FF_PAIR, D_MODEL), lambda b, i: (b, 0, 0)),
                  pl.BlockSpec((None, FF_PAIR, D_MODEL), lambda b, i: (2 + b, 0, 0))],
        out_specs=[pl.BlockSpec((2, tm, FF_PAIR), lambda b, i: (0, i, b)),
                   pl.BlockSpec((tm, FF_PAIR), lambda b, i: (i, b))],
        out_shape=[jax.ShapeDtypeStruct((2, s, D_FF), BF16), jax.ShapeDtypeStruct((s, D_FF), BF16)],
        args=[h2, wfin4, wfin4], comm=comm)


def _ffn_out_fwd(act, wfo, xm, g_post3, g_next3, l, l_next, s, tm, comm=None):
    def body(a_ref, w_ref, x_ref, gp_ref, gn_ref, f_ref, xo_ref, h_ref):
        for rs in _row_subtiles(tm, SUB_ROWS):
            f = _dot(a_ref[rs, :], w_ref[...])
            f_ref[rs, :] = f.astype(BF16)
            xo = x_ref[rs, :] + f * _rstd(f) * gp_ref[...]
            xo_ref[rs, :] = xo
            h_ref[rs, :] = (xo * _rstd(xo) * gn_ref[...]).astype(BF16)

    row = pl.BlockSpec((tm, D_MODEL), lambda i: (i, 0))
    return _call(
        body, name="ffn_out_fwd", grid=(s // tm,),
        in_specs=[pl.BlockSpec((tm, D_FF), lambda i: (i, 0)),
                  pl.BlockSpec((D_FF, D_MODEL), lambda i: (0, 0), pipeline_mode=pl.Buffered(1)), row,
                  pl.BlockSpec((None, 1, D_MODEL), lambda i: (l, 0, 0)),
                  pl.BlockSpec((None, 1, D_MODEL), lambda i: (l_next, 0, 0))],
        out_specs=[row, row, row],
        out_shape=[jax.ShapeDtypeStruct((s, D_MODEL), BF16), jax.ShapeDtypeStruct((s, D_MODEL), F32),
                   jax.ShapeDtypeStruct((s, D_MODEL), BF16)],
        args=[act, wfo, xm, g_post3, g_next3], comm=comm)


def _ffn_out_loss(act, wfo, xm, g_post3, target, l, s, tm):
    def body(a_ref, w_ref, x_ref, gp_ref, t_ref, dx_ref, sq_ref, df_ref, dg_ref):
        _zero_first((sq_ref, dg_ref), pl.program_id(0) == 0)
        for rs in _row_subtiles(tm, SUB_ROWS):
            f = _dot(a_ref[rs, :], w_ref[...])
            gain = gp_ref[...]
            err = x_ref[rs, :] + f * _rstd(f) * gain - t_ref[rs, :]
            dx = err * (1.0 / D_MODEL)
            dx_ref[rs, :] = dx
            df, dyn = _norm_bwd_rows(f, gain, dx)
            df_ref[rs, :] = df.astype(BF16)
            _add_cols(dg_ref, dyn)
            cs = jnp.sum(err * err, axis=0, keepdims=True)
            part = cs[:, 0:128]
            for k in range(1, D_MODEL // 128):
                part = part + cs[:, 128 * k:128 * (k + 1)]
            sq_ref[0:1, :] += part

    row = pl.BlockSpec((tm, D_MODEL), lambda i: (i, 0))
    return _call(
        body, name="ffn_out_loss", grid=(s // tm,),
        in_specs=[pl.BlockSpec((tm, D_FF), lambda i: (i, 0)),
                  pl.BlockSpec((D_FF, D_MODEL), lambda i: (0, 0), pipeline_mode=pl.Buffered(1)), row,
                  pl.BlockSpec((None, 1, D_MODEL), lambda i: (l, 0, 0)), row],
        out_specs=[row, pl.BlockSpec((8, 128), lambda i: (0, 0)), row, pl.BlockSpec((8, D_MODEL), lambda i: (0, 0))],
        out_shape=[jax.ShapeDtypeStruct((s, D_MODEL), F32), jax.ShapeDtypeStruct((8, 128), F32),
                   jax.ShapeDtypeStruct((s, D_MODEL), BF16), jax.ShapeDtypeStruct((8, D_MODEL), F32)],
        args=[act, wfo, xm, g_post3, target])


def _norm_bwd_rows(v, g, dy):
    r = _rstd(v)
    vn = v * r
    gd = dy * g
    dv = r * (gd - vn * jnp.mean(vn * gd, axis=-1, keepdims=True))
    return dv, dy * vn


def _zero_first(refs, first):
    @pl.when(first)
    def _():
        for ref in refs:
            ref[...] = jnp.zeros(ref.shape, F32)


def _add_cols(ref, val):
    ref[0:1, :] += jnp.sum(val, axis=0, keepdims=True)


def _accum_cols(ref, val, first):
    _zero_first((ref,), first)
    _add_cols(ref, val)


def _row_subtiles(rows, sub):
    sub = min(sub, rows)
    return [slice(r, r + sub) for r in range(0, rows, sub)]


def _ffn_out_bwd(df, wfo, gu, act, s, tm, comm=None):
    nm = s // tm

    def body(df_ref, w_ref, gu_ref, act_ref, dgu_ref, dw_ref, acc_ref):
        i = pl.program_id(1)
        df = df_ref[...]
        _zero_first((acc_ref,), i == 0)
        acc_ref[...] += _dot_tn(act_ref[...], df)

        @pl.when(i == nm - 1)
        def _():
            dw_ref[...] = acc_ref[...].astype(BF16)

        for cs in _COL_SUBTILES:
            da = _dot_nt(df, w_ref[cs, :])
            g = gu_ref[0, :, cs].astype(F32)
            u = gu_ref[1, :, cs].astype(F32)
            sg = 1.0 / (1.0 + jnp.exp(-g))
            dgu_ref[0, :, cs] = (da * u * (sg * (1.0 + g * (1.0 - sg)))).astype(BF16)
            dgu_ref[1, :, cs] = (da * (g * sg)).astype(BF16)

    blk = pl.BlockSpec((2, tm, FF_PAIR), lambda b, i: (0, i, b))
    wblk = pl.BlockSpec((FF_PAIR, D_MODEL), lambda b, i: (b, 0))
    return _call(
        body, name="ffn_out_bwd", grid=(2, nm),
        in_specs=[pl.BlockSpec((tm, D_MODEL), lambda b, i: (i, 0)), wblk, blk,
                  pl.BlockSpec((tm, FF_PAIR), lambda b, i: (i, b))],
        out_specs=[blk, wblk],
        out_shape=[jax.ShapeDtypeStruct((2, s, D_FF), BF16), jax.ShapeDtypeStruct((D_FF, D_MODEL), BF16)],
        scratch=[pltpu.VMEM((FF_PAIR, D_MODEL), F32)],
        args=[df, wfo, gu, act], comm=comm)


def _dw_ffn_in(h2, dgu, s):
    def body(a_ref, b_ref, o_ref):
        o_ref[...] = _dot_tn(b_ref[...], a_ref[...]).astype(BF16)

    out, = _call(
        body, name="dw_ffn_in", grid=(4,),
        in_specs=[pl.BlockSpec((s, D_MODEL), lambda n: (0, 0), pipeline_mode=pl.Buffered(1)),
                  pl.BlockSpec((None, s, FF_PAIR), lambda n: (n // 2, 0, n % 2))],
        out_specs=[pl.BlockSpec((None, FF_PAIR, D_MODEL), lambda n: (n, 0, 0))],
        out_shape=[jax.ShapeDtypeStruct((4, FF_PAIR, D_MODEL), BF16)], args=[h2, dgu])
    return out


def _ffn_in_bwd(dgu, wfin, xm, g_pre3, dres, z, g_post3, l, s, tm, comm=None):
    def body(d_ref, w_ref, xm_ref, gp_ref, dres_ref, z_ref, gq_ref, dxm_ref, dz_ref, dgp_ref, dgq_ref):
        _zero_first((dgp_ref, dgq_ref), pl.program_id(0) == 0)
        for rs in _row_subtiles(tm, SUB_ROWS):
            dh = _dot(d_ref[0, rs, :], w_ref[0:D_FF, :]) + _dot(d_ref[1, rs, :], w_ref[D_FF:2 * D_FF, :])
            dx, dyn = _norm_bwd_rows(xm_ref[rs, :], gp_ref[...], dh)
            dxm = dres_ref[rs, :] + dx
            dxm_ref[rs, :] = dxm
            _add_cols(dgp_ref, dyn)
            dz, dyn2 = _norm_bwd_rows(z_ref[rs, :].astype(F32), gq_ref[...], dxm)
            dz_ref[rs, :] = dz.astype(BF16)
            _add_cols(dgq_ref, dyn2)

    row = pl.BlockSpec((tm, D_MODEL), lambda i: (i, 0))
    gain = pl.BlockSpec((None, 1, D_MODEL), lambda i: (l, 0, 0))
    dgs = pl.BlockSpec((8, D_MODEL), lambda i: (0, 0))
    return _call(
        body, name="ffn_in_bwd", grid=(s // tm,),
        in_specs=[pl.BlockSpec((2, tm, D_FF), lambda i: (0, i, 0)),
                  pl.BlockSpec((2 * D_FF, D_MODEL), lambda i: (0, 0), pipeline_mode=pl.Buffered(1)),
                  row, gain, row, row, gain],
        out_specs=[row, row, dgs, dgs],
        out_shape=[jax.ShapeDtypeStruct((s, D_MODEL), F32), jax.ShapeDtypeStruct((s, D_MODEL), BF16),
                   jax.ShapeDtypeStruct((8, D_MODEL), F32), jax.ShapeDtypeStruct((8, D_MODEL), F32)],
        args=[dgu, wfin, xm, g_pre3, dres, z, g_post3], comm=comm)


def _out_proj_bwd(dz, wout, o, g3, ync, yna, l, s, tm):
    nm = s // tm
    half = D_MODEL // 2

    def body(dz_ref, w_ref, o_ref, g_ref, a1_ref, a2_ref, dyc_ref, do_ref, dg_ref, dw_ref, acc_ref):
        i = pl.program_id(0)
        gmat = _group_matrix()
        _zero_first((dg_ref, acc_ref), i == 0)
        dzv = dz_ref[...]
        acc_ref[0:half, :] += _dot_tn(a1_ref[...], dzv)
        acc_ref[half:D_MODEL, :] += _dot_tn(a2_ref[...], dzv)

        @pl.when(i == nm - 1)
        def _():
            dw_ref[...] = acc_ref[...].astype(BF16)

        for rs in _row_subtiles(tm, SUB_ROWS):
            dy = _dot_nt(dz_ref[rs, :], w_ref[...])
            dyc_ref[rs, :] = dy[:, 0:CONV_WIDTH]
            for j in range(ATTN_WIDTH // 128):
                c0 = 128 * j
                ov = o_ref[rs, c0:c0 + 128]
                dyn = dy[:, CONV_WIDTH + c0:CONV_WIDTH + c0 + 128]
                r = lax.rsqrt(_group_mean(ov * ov, gmat) + EPS)
                on = ov * r
                gd = dyn * g_ref[:, c0:c0 + 128]
                do_ref[rs, c0:c0 + 128] = r * (gd - on * _group_mean(on * gd, gmat))
                dg_ref[0:1, c0:c0 + 128] += jnp.sum(dyn * on, axis=0, keepdims=True)

    halfrow = pl.BlockSpec((tm, ATTN_WIDTH), lambda i: (i, 0))
    return _call(
        body, name="out_proj_bwd", grid=(nm,),
        in_specs=[pl.BlockSpec((tm, D_MODEL), lambda i: (i, 0)),
                  pl.BlockSpec((D_MODEL, D_MODEL), lambda i: (0, 0)), halfrow,
                  pl.BlockSpec((None, 1, ATTN_WIDTH), lambda i: (l, 0, 0)), halfrow, halfrow],
        out_specs=[halfrow, halfrow, pl.BlockSpec((8, ATTN_WIDTH), lambda i: (0, 0)),
                   pl.BlockSpec((D_MODEL, D_MODEL), lambda i: (0, 0))],
        out_shape=[jax.ShapeDtypeStruct((s, CONV_WIDTH), F32), jax.ShapeDtypeStruct((s, ATTN_WIDTH), F32),
                   jax.ShapeDtypeStruct((8, ATTN_WIDTH), F32), jax.ShapeDtypeStruct((D_MODEL, D_MODEL), BF16)],
        scratch=[pltpu.VMEM((D_MODEL, D_MODEL), F32)],
        args=[dz, wout, o, g3, ync, yna])


def _conv_bwd(pc, dyc, wc, g3, dq, dk, dv, l, s, tr):
    hb = tr // 8
    nt = s // tr
    ext = tr + 16
    last_hb = s // 8 - 1

    def body(pc_ref, prev_ref, next_ref, dy_ref, dyn_ref, wc_ref, g_ref, dq_ref, dk_ref, dv_ref,
             dpc_ref, dw_ref, dg_ref):
        i = pl.program_id(0)
        for part, ref in enumerate((dq_ref, dk_ref, dv_ref)):
            c = 3 * CONV_WIDTH + ATTN_WIDTH * part
            dpc_ref[:, c:c + ATTN_WIDTH] = ref[...]
        gmat = _group_matrix()
        row = lax.broadcasted_iota(jnp.int32, (ext, 128), 0) + (i * tr - 8)
        inside = jnp.where(row >= 0, jnp.where(row < s, 1, 0), 0) == 1

        @pl.when(i == 0)
        def _():
            dw_ref[...] = jnp.zeros(dw_ref.shape, F32)
            dg_ref[...] = jnp.zeros(dg_ref.shape, F32)

        def extend(ref_prev, ref_mid, ref_next, c):
            if ref_prev is None:
                before = jnp.zeros((8, 128), F32)
            else:
                before = ref_prev[:, c:c + 128].astype(F32)[ref_prev.shape[0] - 8:]
            after = ref_next[:, c:c + 128].astype(F32)[0:8]
            return jnp.concatenate([before, ref_mid[:, c:c + 128].astype(F32), after], axis=0)

        for j in range(CONV_WIDTH // 128):
            c0, c1, c2 = 128 * j, CONV_WIDTH + 128 * j, 2 * CONV_WIDTH + 128 * j
            hc = extend(prev_ref, pc_ref, next_ref, c0)
            bg = extend(prev_ref, pc_ref, next_ref, c1)
            cg = extend(prev_ref, pc_ref, next_ref, c2)
            dyn = extend(None, dy_ref, dyn_ref, c0)
            w0, w1, w2 = (wc_ref[0:1, c0:c0 + 128], wc_ref[1:2, c0:c0 + 128], wc_ref[2:3, c0:c0 + 128])
            gain = g_ref[:, c0:c0 + 128]
            u = jnp.where(inside, cg * hc, 0.0)
            u1 = pltpu.roll(u, 1, 0)
            u2 = pltpu.roll(u, 2, 0)
            out = u2 * w0 + u1 * w1 + u * w2
            yc = bg * out
            r = lax.rsqrt(_group_mean(yc * yc, gmat) + EPS)
            ycn = yc * r
            gd = dyn * gain
            dyc = r * (gd - ycn * _group_mean(ycn * gd, gmat))
            dout = jnp.where(inside, dyc * bg, 0.0)
            du = dout * w2 + pltpu.roll(dout, ext - 1, 0) * w1 + pltpu.roll(dout, ext - 2, 0) * w0
            sl = slice(8, 8 + tr)
            dpc_ref[:, c0:c0 + 128] = (du[sl] * cg[sl]).astype(BF16)
            dpc_ref[:, c1:c1 + 128] = (dyc[sl] * out[sl]).astype(BF16)
            dpc_ref[:, c2:c2 + 128] = (du[sl] * hc[sl]).astype(BF16)
            dw_ref[0:1, c0:c0 + 128] += jnp.sum(dout[sl] * u2[sl], axis=0, keepdims=True)
            dw_ref[1:2, c0:c0 + 128] += jnp.sum(dout[sl] * u1[sl], axis=0, keepdims=True)
            dw_ref[2:3, c0:c0 + 128] += jnp.sum(dout[sl] * u[sl], axis=0, keepdims=True)
            dg_ref[0:1, c0:c0 + 128] += jnp.sum(dyn[sl] * ycn[sl], axis=0, keepdims=True)

    wide = 3 * CONV_WIDTH
    return _call(
        body, name="conv_bwd", grid=(nt,),
        in_specs=[pl.BlockSpec((tr, wide), lambda i: (i, 0)),
                  pl.BlockSpec((16, wide), lambda i: (jnp.maximum(i * (hb // 2) - 1, 0), 0)),
                  pl.BlockSpec((16, wide), lambda i: (jnp.minimum((i + 1) * (hb // 2), last_hb // 2), 0)),
                  pl.BlockSpec((tr, CONV_WIDTH), lambda i: (i, 0)),
                  pl.BlockSpec((8, CONV_WIDTH), lambda i: (jnp.minimum((i + 1) * hb, last_hb), 0)),
                  pl.BlockSpec((None, 8, CONV_WIDTH), lambda i: (l, 0, 0)),
                  pl.BlockSpec((None, 1, CONV_WIDTH), lambda i: (l, 0, 0)),
                  pl.BlockSpec((tr, ATTN_WIDTH), lambda i: (i, 0)),
                  pl.BlockSpec((tr, ATTN_WIDTH), lambda i: (i, 0)),
                  pl.BlockSpec((tr, ATTN_WIDTH), lambda i: (i, 0))],
        out_specs=[pl.BlockSpec((tr, PROJ_WIDTH), lambda i: (i, 0)),
                   pl.BlockSpec((8, CONV_WIDTH), lambda i: (0, 0)),
                   pl.BlockSpec((8, CONV_WIDTH), lambda i: (0, 0))],
        out_shape=[jax.ShapeDtypeStruct((s, PROJ_WIDTH), BF16), jax.ShapeDtypeStruct((8, CONV_WIDTH), F32),
                   jax.ShapeDtypeStruct((8, CONV_WIDTH), F32)],
        args=[pc, pc, pc, dyc, dyc, wc, g3, dq, dk, dv])


def _attn_bwd(qkvp, biasm, o, lse, do, l, s, pad, comm=None):
    rows = Q_STEP * Q_BLOCK
    nb = s // rows
    qb0 = pad // rows
    scale = HEAD_DIM ** -0.5
    wide = 128 * ATTN_PAIRS

    def body(q_ref, k_ref, v_ref, b_ref, o_ref, lse_ref, do_ref,
             dq_ref, dk_ref, dv_ref, ds_ref, dk_acc, dv_acc):
        blk = pl.program_id(1)

        @pl.when(blk == 0)
        def _():
            dk_acc[...] = jnp.zeros(dk_acc.shape, F32)
            dv_acc[...] = jnp.zeros(dv_acc.shape, F32)
            ds_ref[...] = jnp.zeros(ds_ref.shape, F32)

        lane = lax.broadcasted_iota(jnp.int32, (1, 128), 1)

        def step(masked):
            for sub in range(Q_STEP):
                rs = slice(Q_BLOCK * sub, Q_BLOCK * (sub + 1))
                start = blk * rows + Q_BLOCK * sub
                koff = pl.multiple_of(start + (pad - LEFT), Q_BLOCK)
                kpos = lax.broadcasted_iota(jnp.int32, (1, K_BAND), 1) + (start - LEFT)
                kmask = jnp.where(kpos >= 0, 0.0, NEG_INF)
                for pr in range(ATTN_PAIRS):
                    ls = slice(128 * pr, 128 * (pr + 1))
                    q = q_ref[rs, ls]
                    kb = k_ref[pl.ds(koff, K_BAND), ls]
                    vb = v_ref[pl.ds(koff, K_BAND), ls]
                    dov = do_ref[rs, ls]
                    lse_v = lse_ref[rs, ls]
                    prod = dov * o_ref[rs, ls]
                    dq_parts = []
                    dk_new = jnp.zeros((128, K_BAND), F32)
                    dv_new = jnp.zeros((128, K_BAND), F32)
                    for hh in range(2):
                        in_head = (lane >> 6) == hh
                        qm = jnp.where(in_head, q, jnp.zeros_like(q)) * jnp.asarray(scale, BF16)
                        dom = jnp.where(in_head, dov, 0.0).astype(BF16)
                        delta = jnp.sum(jnp.where(in_head, prod, 0.0), axis=1, keepdims=True)
                        lse_h = lse_v[:, HEAD_DIM * hh:HEAD_DIM * hh + 1]
                        sc = _dot_nt(qm, kb) + b_ref[2 * pr + hh]
                        if masked:
                            sc = sc + kmask
                        p = jnp.exp(sc - lse_h)
                        dp = _dot_nt(dom, vb)
                        ds = p * (dp - delta)
                        ds_ref[2 * pr + hh] += ds
                        dsb = ds.astype(BF16)
                        dq_parts.append(_dot(dsb, kb) * scale)
                        dk_new = dk_new + _dot_tn(qm, dsb)
                        dv_new = dv_new + _dot_tn(dom, p.astype(BF16))
                    dq_ref[rs, ls] = jnp.where(lane < HEAD_DIM, dq_parts[0], dq_parts[1]).astype(BF16)
                    dk_acc[pr, :, pl.ds(koff, K_BAND)] += dk_new
                    dv_acc[pr, :, pl.ds(koff, K_BAND)] += dv_new

        pl.when(blk == 0)(lambda: step(True))
        pl.when(blk > 0)(lambda: step(False))

        @pl.when(blk == nb - 1)
        def _():
            for pr in range(ATTN_PAIRS):
                ls = slice(128 * pr, 128 * (pr + 1))
                dk_ref[:, ls] = dk_acc[pr, :, pad:pad + s].T.astype(BF16)
                dv_ref[:, ls] = dv_acc[pr, :, pad:pad + s].T.astype(BF16)

    n_grp = ATTN_WIDTH // wide
    qblk = pl.BlockSpec((rows, wide), lambda p, b: (b, p))
    col = pl.BlockSpec((s, wide), lambda p, b: (0, p))
    shp = jax.ShapeDtypeStruct((s, ATTN_WIDTH), BF16)
    return _call(
        body, name="attn_bwd", grid=(n_grp, nb),
        in_specs=[pl.BlockSpec((rows, wide), lambda p, b: (qb0 + b, p)),
                  pl.BlockSpec((s + pad, wide), lambda p, b: (0, n_grp + p)),
                  pl.BlockSpec((s + pad, wide), lambda p, b: (0, 2 * n_grp + p)),
                  pl.BlockSpec((None, 2 * ATTN_PAIRS, Q_BLOCK, K_BAND), lambda p, b: (l, p, 0, 0)),
                  qblk, qblk, qblk],
        out_specs=[qblk, col, col, pl.BlockSpec((2 * ATTN_PAIRS, Q_BLOCK, K_BAND), lambda p, b: (p, 0, 0))],
        out_shape=[shp, shp, shp, jax.ShapeDtypeStruct((N_HEADS, Q_BLOCK, K_BAND), F32)],
        scratch=[pltpu.VMEM((ATTN_PAIRS, 128, s + pad), F32), pltpu.VMEM((ATTN_PAIRS, 128, s + pad), F32)],
        args=[qkvp, qkvp, qkvp, biasm, o, lse, do], comm=comm)


def _dw_in(h, dproj, s):
    def body(a_ref, b_ref, o_ref):
        acc = _dot_tn(a_ref[...], b_ref[...])
        o_ref[0] = acc[:, 0:PROJ_SHARD].astype(BF16)
        o_ref[1] = acc[:, PROJ_SHARD:2 * PROJ_SHARD].astype(BF16)

    out, = _call(
        body, name="dw_in", grid=(4,),
        in_specs=[pl.BlockSpec((s, D_MODEL), lambda n: (0, 0)),
                  pl.BlockSpec((s, 2 * PROJ_SHARD), lambda n: (0, n))],
        out_specs=[pl.BlockSpec((2, D_MODEL, PROJ_SHARD), lambda n: (n, 0, 0))],
        out_shape=[jax.ShapeDtypeStruct((N_DEV, D_MODEL, PROJ_SHARD), BF16)], args=[h, dproj])
    return out


def _in_proj_bwd(dproj, win, x, g3, dres, l, s, tm, f_prev=None, g_post3=None, comm=None):
    chain = f_prev is not None

    def body(d_ref, w_ref, x_ref, g_ref, dres_ref, *rest):
        if chain:
            f_ref, gq_ref, dx_ref, dg_ref, df_ref, dgq_ref = rest
            _zero_first((dg_ref, dgq_ref), pl.program_id(0) == 0)
        else:
            dx_ref, dg_ref = rest
            _zero_first((dg_ref,), pl.program_id(0) == 0)
        w = jnp.concatenate([w_ref[j] for j in range(N_DEV)], axis=1)
        for rs in _row_subtiles(tm, SUB_ROWS):
            dh = _dot_nt(d_ref[rs, :], w)
            dx, dyn = _norm_bwd_rows(x_ref[rs, :], g_ref[...], dh)
            dx = dres_ref[rs, :] + dx
            dx_ref[rs, :] = dx
            _add_cols(dg_ref, dyn)
            if chain:
                df, dyn2 = _norm_bwd_rows(f_ref[rs, :].astype(F32), gq_ref[...], dx)
                df_ref[rs, :] = df.astype(BF16)
                _add_cols(dgq_ref, dyn2)

    row = pl.BlockSpec((tm, D_MODEL), lambda i: (i, 0))
    dgs = pl.BlockSpec((8, D_MODEL), lambda i: (0, 0))
    in_specs = [pl.BlockSpec((tm, PROJ_WIDTH), lambda i: (i, 0)),
                pl.BlockSpec((N_DEV, D_MODEL, PROJ_SHARD), lambda i: (0, 0, 0), pipeline_mode=pl.Buffered(1)),
                row, pl.BlockSpec((None, 1, D_MODEL), lambda i: (l, 0, 0)), row]
    out_specs = [row, dgs]
    out_shape = [jax.ShapeDtypeStruct((s, D_MODEL), F32), jax.ShapeDtypeStruct((8, D_MODEL), F32)]
    args = [dproj, win, x, g3, dres]
    if chain:
        in_specs += [row, pl.BlockSpec((None, 1, D_MODEL), lambda i: (l - 1, 0, 0))]
        out_specs += [row, dgs]
        out_shape += [jax.ShapeDtypeStruct((s, D_MODEL), BF16), jax.ShapeDtypeStruct((8, D_MODEL), F32)]
        args += [f_prev, g_post3]
    return _call(body, name="in_proj_bwd", grid=(s // tm,), in_specs=in_specs, out_specs=out_specs,
                 out_shape=out_shape, args=args, comm=comm)


def _adamw(name, w, m, v, lands, owns=None, me=None):
    groups, rows, cols = w.shape
    assert len(lands) == groups
    n_part = lands[0].shape[0]
    tr = _row_tile(rows, tuple(c for c in (512, 352, 256, 176, 128, 64, 32, 16, 8) if c * cols <= 256 * 1024))
    c1 = 1.0 - ADAM_B1 ** ADAM_STEP
    c2 = 1.0 - ADAM_B2 ** ADAM_STEP
    n_own = groups if owns is not None else 0

    def body(*refs):
        if n_own:
            me_ref, refs = refs[0], refs[1:]
        w_ref, m_ref, v_ref = refs[:3]
        land_refs = refs[3:3 + groups]
        own_refs = refs[3 + groups:3 + groups + n_own]
        g_ref, d_ref, nm_ref, nv_ref = refs[3 + groups + n_own:]
        grp = pl.program_id(0)
        for gi in range(groups):
            @pl.when(grp == gi)
            def _():
                l_ref = land_refs[gi]
                g = None
                for p in range(n_part):
                    part = l_ref[p].astype(F32)
                    if n_own:
                        part = jnp.where(me_ref[0] == p, own_refs[gi][...].astype(F32), part)
                    g = part if g is None else g + part
                g_ref[...] = g
                m1 = ADAM_B1 * m_ref[...] + (1.0 - ADAM_B1) * g
                v1 = ADAM_B2 * v_ref[...] + (1.0 - ADAM_B2) * (g * g)
                nm_ref[...] = m1
                nv_ref[...] = v1
                d_ref[...] = -ADAM_LR * ((m1 / c1) / (jnp.sqrt(v1 / c2) + ADAM_EPS) + ADAM_WD * w_ref[...])

    blk = pl.BlockSpec((None, tr, cols), lambda g, i, *_: (g, i, 0))
    shp = jax.ShapeDtypeStruct((groups, rows, cols), F32)

    def land_spec(gi):
        return pl.BlockSpec((n_part, tr, cols), lambda g, i, *_: (0, jnp.where(g == gi, i, 0), 0))

    def own_spec(gi):
        if owns[gi].ndim == 3:
            return pl.BlockSpec((None, tr, cols), lambda g, i, me_ref: (me_ref[0], jnp.where(g == gi, i, 0), 0))
        return pl.BlockSpec((tr, cols), lambda g, i, me_ref: (jnp.where(g == gi, i, 0), 0))

    in_specs = [blk, blk, blk] + [land_spec(gi) for gi in range(groups)] + [own_spec(gi) for gi in range(n_own)]
    args = [w, m, v] + list(lands) + (list(owns) if n_own else [])
    if not n_own:
        return _call(body, name=name, grid=(groups, rows // tr), in_specs=in_specs,
                     out_specs=[blk, blk, blk, blk], out_shape=[shp, shp, shp, shp], args=args)
    return pl.pallas_call(
        body, name=name,
        grid_spec=pltpu.PrefetchScalarGridSpec(
            num_scalar_prefetch=1, grid=(groups, rows // tr), in_specs=in_specs, out_specs=[blk, blk, blk, blk]),
        out_shape=[shp, shp, shp, shp],
        compiler_params=pltpu.CompilerParams(dimension_semantics=("arbitrary", "arbitrary"),
                                             vmem_limit_bytes=VMEM_LIMIT),
    )(me, *args)


def _pack_small(rel, gco, gao, gpm, gqm, gpf, gqf):
    n_layers = rel.shape[0]
    relp = jnp.pad(rel, ((0, 0), (0, 0), (0, REL_PAD - rel.shape[2])))
    parts = [relp.reshape(n_layers * N_HEADS * REL_PAD // 128, 128)]
    parts += [a.reshape(-1, 128) for a in (gco, gao, gpm, gqm, gpf, gqf)]
    return jnp.concatenate(parts, axis=0)


def _pack_small_grads(d_rel, parts):
    n_layers = len(d_rel)
    keys = ("gco", "gao", "gpm", "gqm", "gpf", "gqf")
    arrays = list(d_rel) + [parts[k][l] for k in keys for l in range(n_layers)] + list(parts["wc"])
    rows = 0
    plan = []
    for l in range(n_layers):
        for h in range(N_HEADS):
            for t in range(REL_PAD // 128):
                plan.append((l, (0, h), t, rows))
                rows += 1
    for ki, k in enumerate(keys):
        for l in range(n_layers):
            for t in range(parts[k][l].shape[1] // 128):
                plan.append((n_layers * (1 + ki) + l, (0,), t, rows))
                rows += 1
    for l in range(n_layers):
        for tap in range(3):
            for t in range(CONV_WIDTH // 128):
                plan.append((n_layers * (1 + len(keys)) + l, (tap,), t, rows))
                rows += 1
    total = rows + (-rows) % 8

    def body(*refs):
        o_ref = refs[-1]
        if total > rows:
            o_ref[rows:total, :] = jnp.zeros((total - rows, 128), F32)
        for op, idx, t, dst in plan:
            lanes = slice(128 * t, 128 * (t + 1))
            if len(idx) == 2:
                o_ref[dst:dst + 1, :] = refs[op][idx[0], idx[1]:idx[1] + 1, lanes]
            else:
                o_ref[dst:dst + 1, :] = refs[op][idx[0]:idx[0] + 1, lanes]

    vmem = pl.BlockSpec(memory_space=pltpu.VMEM)
    out, = _call(body, name="pack_small_grads", grid=(), in_specs=[vmem] * len(arrays), out_specs=[vmem],
                 out_shape=[jax.ShapeDtypeStruct((total, 128), F32)], args=arrays)
    return out


def _unpack_small(p, n_layers):
    n_rel = n_layers * N_HEADS * REL_PAD // 128
    rel = p[:n_rel].reshape(n_layers, N_HEADS, REL_PAD)[:, :, :2 * REL_CLIP + 1]
    outs = [rel]
    r0 = n_rel
    for width in (CONV_WIDTH, ATTN_WIDTH, D_MODEL, D_MODEL, D_MODEL, D_MODEL):
        nr = n_layers * width // 128
        outs.append(p[r0:r0 + nr].reshape(n_layers, width))
        r0 += nr
    return outs


def kernel(x, w_in, w_conv, rel_bias, g_conv_out, g_attn_out, w_out, g_pre_mix, g_post_mix, g_pre_ffn, g_post_ffn, w_ffn_in, w_ffn_out, loss_target, m_w_in, m_w_conv, m_rel_bias, m_g_conv_out, m_g_attn_out, m_w_out, m_g_pre_mix, m_g_post_mix, m_g_pre_ffn, m_g_post_ffn, m_w_ffn_in, m_w_ffn_out, v_w_in, v_w_conv, v_rel_bias, v_g_conv_out, v_g_attn_out, v_w_out, v_g_pre_mix, v_g_post_mix, v_g_pre_ffn, v_g_post_ffn, v_w_ffn_in, v_w_ffn_out):
    n_layers = w_in.shape[0]
    s = x.shape[1]
    assert x.shape == (1, s, D_MODEL) and s % 1024 == 0
    assert w_in.shape == (n_layers, D_MODEL, PROJ_SHARD) and w_ffn_in.shape == (n_layers, D_MODEL, FF_SHARD)
    tm = 512
    tq = 1024 if s >= 2048 else 512
    tf = min(1024, s)
    x0 = x.reshape(s, D_MODEL)
    target = loss_target.reshape(s, D_MODEL)
    dev = _dev_index(lax.axis_index("x"), lax.axis_index("y"), lax.axis_index("c"))

    wt_ffn_in, mt_ffn_in, vt_ffn_in = (jnp.transpose(a, (0, 2, 1)) for a in (w_ffn_in, m_w_ffn_in, v_w_ffn_in))
    local_w = [_cast_bf16(w_in, "cast_w_in"), _cast_bf16(w_out, "cast_w_out"),
               _cast_bf16(wt_ffn_in, "cast_w_ffn_in"), _cast_bf16(w_ffn_out, "cast_w_ffn_out")]
    wc_local = jnp.pad(jnp.transpose(w_conv, (0, 2, 1)).reshape(-1), (0, 1024 - n_layers * 3 * 64)).reshape(8, 128)
    biasm, win_next, wc_g = _bias_build(jnp.pad(rel_bias, ((0, 0), (0, 0), (0, REL_PAD - rel_bias.shape[2]))),
                                        comm=_Gather([(local_w[0], 0), (wc_local, None)]))
    weights = [None] * n_layers
    wc_full = wc_g.reshape(N_DEV, 1024)[:, :n_layers * 3 * 64].reshape(N_DEV, n_layers, 3, 64)
    wc_full = jnp.transpose(wc_full, (1, 2, 0, 3)).reshape(n_layers, 3, CONV_WIDTH)
    wc_full = jnp.pad(wc_full, ((0, 0), (0, 5), (0, 0)))

    g3 = {k: v.reshape(n_layers, 1, -1) for k, v in dict(
        conv=g_conv_out, attn=g_attn_out, pre_mix=g_pre_mix, post_mix=g_post_mix,
        pre_ffn=g_pre_ffn, post_ffn=g_post_ffn).items()}

    saved = []
    xl = x0
    h = _norm_cast(x0, g3["pre_mix"], 0, tm)
    for l in range(n_layers):
        win = win_next
        pc, ync, wout = _in_proj_conv(h, win, wc_full, g3["conv"], l, s, tq, comm=_Gather([(local_w[1], l)]))
        qkvp = _in_proj_qkv(h, win, s, tq)
        o, lse, yna, wfin = _attn_fwd(qkvp, biasm, g3["attn"], l, s, tq, comm=_Gather([(local_w[2], l)]))
        wout = wout.reshape(D_MODEL, D_MODEL)
        z, xm, h2 = _out_proj_fwd(ync, yna, wout, xl, g3["post_mix"], g3["pre_ffn"], l, s, tq)
        gu, act, wfout = _ffn_in_fwd(h2, wfin.reshape(4, FF_PAIR, D_MODEL), s, tf, comm=_Gather([(local_w[3], l)]))
        wfo = wfout.reshape(D_FF, D_MODEL)
        weights[l] = [win, wout, wfin.reshape(2 * D_FF, D_MODEL), wfo]
        sv = dict(x=xl, h=h, pc=pc, qkvp=qkvp, ync=ync, yna=yna, o=o, lse=lse, z=z, xm=xm, h2=h2, gu=gu, act=act)
        if l + 1 < n_layers:
            sv["f"], xl, h, win_next = _ffn_out_fwd(act, wfo, xm, g3["post_ffn"], g3["pre_mix"], l, l + 1, s, tm,
                                                    comm=_Gather([(local_w[0], l + 1)]))
        else:
            dx, sq, df, dg_post_ffn = _ffn_out_loss(act, wfo, xm, g3["post_ffn"], target, l, s, tm)
        saved.append(sv)

    loss = lax.psum(jnp.sum(sq) * (0.5 / D_MODEL), ("x", "y", "c"))

    lands = dict(win=[None] * n_layers, wout=[None] * n_layers, wfin=[None] * n_layers, wfout=[None] * n_layers)
    small = {k: [None] * n_layers for k in ("gco", "gao", "gpm", "gqm", "gpf", "gqf", "wc")}
    d_rel = [None] * n_layers
    started = []

    def start(name, keys, l, arrays):
        items = [(a, False) for a in arrays]
        send_sems, recv_sems, srcs, zones, token = _exchange_start(name + "_start", items)
        started.append((name, keys, l, items, send_sems, recv_sems, srcs, zones))
        return token[0:1, 0:1].reshape(1, 1, 1)

    for l in reversed(range(n_layers)):
        sv = saved[l]
        win, wout, wfin, wfo = weights[l]
        small["gqf"][l] = dg_post_ffn
        dgu, d_wfout = _ffn_out_bwd(df, wfo, sv["gu"], sv["act"], s, tm)
        d_wfout = d_wfout.reshape(N_DEV, FFO_SHARD, D_MODEL)
        d_wfin = _dw_ffn_in(sv["h2"], dgu, s).reshape(N_DEV, FF_SHARD, D_MODEL)
        g_pre_ffn = g3["pre_ffn"]
        if l == 0:
            g_pre_ffn = g_pre_ffn + start("exchange_ffn0", ("wfout", "wfin"), l, [d_wfout, d_wfin])
        dxm, dz, dg_pre_ffn, dg_post_mix = _ffn_in_bwd(
            dgu, wfin, sv["xm"], g_pre_ffn, dx, sv["z"], g3["post_mix"], l, s, tm)
        small["gpf"][l] = dg_pre_ffn
        small["gqm"][l] = dg_post_mix
        dyc, do, dg_attn, d_wout = _out_proj_bwd(dz, wout, sv["o"], g3["attn"], sv["ync"], sv["yna"], l, s, tq)
        d_wout = d_wout.reshape(N_DEV, D_MODEL // N_DEV, D_MODEL)
        small["gao"][l] = dg_attn
        dq, dk, dv, ds_sum = _attn_bwd(sv["qkvp"], biasm, sv["o"], sv["lse"], do, l, s, tq)
        d_rel[l] = _bias_bwd(ds_sum[None])
        dproj, dwc, dg_conv = _conv_bwd(sv["pc"], dyc, wc_full, g3["conv"], dq, dk, dv, l, s, tm)
        small["wc"][l] = dwc
        small["gco"][l] = dg_conv
        d_win = _dw_in(sv["h"], dproj, s)
        if l == 0:
            token = start("exchange_mix0", ("wout", "win"), l, [d_wout, d_win])
        else:
            token = start(f"exchange_layer{l}", ("wfout", "wfin", "wout", "win"), l, [d_wfout, d_wfin, d_wout, d_win])
        if l > 0:
            dx, dg_pre_mix, df, dg_post_ffn = _in_proj_bwd(
                dproj, win, sv["x"], g3["pre_mix"] + token, dxm, l, s, tm, f_prev=saved[l - 1]["f"],
                g_post3=g3["post_ffn"])
        else:
            dx, dg_pre_mix = _in_proj_bwd(dproj, win, sv["x"], g3["pre_mix"] + token, dxm, l, s, tm)
        small["gpm"][l] = dg_pre_mix
    grad_x = dx.reshape(1, s, D_MODEL)

    small_vec = _pack_small_grads(d_rel, small)
    small_items = [(small_vec, True)]
    small_sems = _exchange_start("exchange_small_start", small_items)

    owns = dict(win=[None] * n_layers, wout=[None] * n_layers, wfin=[None] * n_layers, wfout=[None] * n_layers)

    def wait(last, after):
        for name, keys, l, items, send_sems, recv_sems, srcs, zones in started:
            if (name == "exchange_mix0") == last:
                srcs, zones = _exchange_wait(name + "_wait", items, send_sems, recv_sems, srcs, zones, after)
                for key, src, zone in zip(keys, srcs, zones):
                    owns[key][l], lands[key][l] = src, zone

    me = dev.astype(jnp.int32).reshape(1)
    wait(False, small_sems[4])
    r_fin = [jnp.transpose(t, (0, 2, 1)) for t in _adamw(
        "adamw_w_ffn_in", wt_ffn_in, mt_ffn_in, vt_ffn_in, lands["wfin"], owns["wfin"], me)]
    r_fout = _adamw("adamw_w_ffn_out", w_ffn_out, m_w_ffn_out, v_w_ffn_out, lands["wfout"], owns["wfout"], me)
    wait(True, r_fout[0])
    r_out = _adamw("adamw_w_out", w_out, m_w_out, v_w_out, lands["wout"], owns["wout"], me)
    r_in = _adamw("adamw_w_in", w_in, m_w_in, v_w_in, lands["win"], owns["win"], me)
    (small_own,), (land_small,) = _exchange_wait(
        "exchange_small_wait", small_items, small_sems[0], small_sems[1], small_sems[2], small_sems[3], r_in[0])

    n_rep = 64 * n_layers
    rep = _adamw(
        "adamw_replicated",
        _pack_small(rel_bias, g_conv_out, g_attn_out, g_pre_mix, g_post_mix, g_pre_ffn, g_post_ffn)[None],
        _pack_small(m_rel_bias, m_g_conv_out, m_g_attn_out, m_g_pre_mix, m_g_post_mix, m_g_pre_ffn, m_g_post_ffn)[None],
        _pack_small(v_rel_bias, v_g_conv_out, v_g_attn_out, v_g_pre_mix, v_g_post_mix, v_g_pre_ffn, v_g_post_ffn)[None],
        [land_small[:, :n_rep]], [small_own[:n_rep]], me)
    rep = [_unpack_small(t[0], n_layers) for t in rep]

    wc_rows = n_layers * 3 * CONV_WIDTH // 128
    zeros_wc = jnp.zeros((1, wc_rows, 128), F32)
    g_wc_full = _adamw("sum_w_conv", zeros_wc, zeros_wc, zeros_wc, [land_small[:, n_rep:n_rep + wc_rows]],
                       [small_own[n_rep:n_rep + wc_rows]], me)[0]
    g_wc_full = g_wc_full.reshape(n_layers, 3, CONV_WIDTH)
    g_wc = lax.dynamic_slice_in_dim(g_wc_full, dev * (CONV_WIDTH // N_DEV), CONV_WIDTH // N_DEV, axis=2)
    g_wc = jnp.transpose(g_wc, (0, 2, 1))

    def tiny(a):
        flat = a.reshape(-1)
        return jnp.pad(flat, (0, (-flat.shape[0]) % 1024)).reshape(1, -1, 128)

    r_wc = _adamw("adamw_w_conv", tiny(w_conv), tiny(m_w_conv), tiny(v_w_conv), [tiny(g_wc)])
    r_wc = [t.reshape(-1)[:w_conv.size].reshape(w_conv.shape) for t in r_wc]

    def leaf(kind):
        return [r_in[kind], r_wc[kind], rep[kind][0], rep[kind][1], rep[kind][2], r_out[kind],
                rep[kind][3], rep[kind][4], rep[kind][5], rep[kind][6], r_fin[kind], r_fout[kind]]

    return (loss, grad_x, *leaf(0), *leaf(1), *leaf(2), *leaf(3))
```

```python
import math

import jax
import jax.numpy as jnp
from jax import lax
from jax.experimental import pallas as pl
from jax.experimental.pallas import tpu as pltpu

F32 = jnp.float32
BF16 = jnp.bfloat16

D_MODEL = 1024
N_DEV = 8
CHUNK = 64
N_LEFT_CHUNKS = 8
CONV_WIDTH = 512
ATTN_WIDTH = 512
HEAD_DIM = 64
N_HEADS = 8
REL_CLIP = 128
REL_PAD = 384
PROJ_WIDTH = 3072
PROJ_SHARD = PROJ_WIDTH // N_DEV
D_FF = 2816
FF_SHARD = 2 * D_FF // N_DEV
FFO_SHARD = D_FF // N_DEV
FF_PAIR = 2 * FF_SHARD
_COL_SUBTILES = (slice(0, 768), slice(768, FF_PAIR))
EPS = 1e-6
NEG_INF = -1e30
ATTN_PAIRS = 2
Q_STEP = 4
Q_BLOCK = 4 * CHUNK
K_BAND = Q_BLOCK + N_LEFT_CHUNKS * CHUNK
LEFT = N_LEFT_CHUNKS * CHUNK
TOEP = 1024

ADAM_LR = 0.001
ADAM_B1 = 0.9
ADAM_B2 = 0.999
ADAM_EPS = 1e-08
ADAM_WD = 0.01
ADAM_STEP = 10

VMEM_LIMIT = 52 * 1024 * 1024
SUB_ROWS = 256
MESH = pl.DeviceIdType.MESH
ANY = pl.BlockSpec(memory_space=pl.ANY)

NT = (((1,), (1,)), ((), ()))
TN = (((0,), (0,)), ((), ()))


def _dot(a, b):
    return jnp.dot(a, b, preferred_element_type=F32)


def _dot_nt(a, b):
    return lax.dot_general(a, b, NT, preferred_element_type=F32)


def _dot_tn(a, b):
    return lax.dot_general(a, b, TN, preferred_element_type=F32)


def _rstd(v):
    return lax.rsqrt(jnp.mean(v * v, axis=-1, keepdims=True) + EPS)


def _group_matrix():
    r = lax.broadcasted_iota(jnp.int32, (128, 128), 0) >> 6
    c = lax.broadcasted_iota(jnp.int32, (128, 128), 1) >> 6
    return jnp.where(r == c, 1.0, 0.0).astype(BF16)


def _group_mean(v, gmat):
    hi = v.astype(BF16)
    lo = (v - hi.astype(F32)).astype(BF16)
    return (_dot(hi, gmat) + _dot(lo, gmat)) * (1.0 / HEAD_DIM)


def _split3(v):
    hi = v.astype(BF16)
    r1 = v - hi.astype(F32)
    mid = r1.astype(BF16)
    lo = (r1 - mid.astype(F32)).astype(BF16)
    return hi, mid, lo


def _row_tile(rows, cands=(1024, 512, 704, 256, 128, 64, 32, 16)):
    for c in cands:
        if rows % c == 0:
            return c
    return rows


def _dev_index(px, py, pc):
    return 4 * px + 2 * py + pc


def _when(cond):
    if cond is True:
        return lambda fn: fn()
    return pl.when(cond)


def _phases(grid):
    def phases():
        if not grid:
            return True, True, True
        lin = pl.program_id(0)
        for a in range(1, len(grid)):
            lin = lin * grid[a] + pl.program_id(a)
        total = math.prod(grid)
        return lin == 0, lin == total - 1, lin == total - 1
    return phases


class _Gather:
    def __init__(self, items):
        self.items = items
        self.args = [a for a, _ in items]
        n = len(items)
        self.out_shape = [jax.ShapeDtypeStruct((N_DEV,) + (a.shape if lay is None else a.shape[1:]), a.dtype)
                          for a, lay in items]
        self.scratch = [pltpu.SemaphoreType.DMA((n, 7)), pltpu.SemaphoreType.DMA((n, 7)),
                        pltpu.SemaphoreType.DMA((n,))]

    def _ctx(self, ins, outs, sems):
        send_sems, recv_sems, local_sems = sems
        x, y, c = lax.axis_index("x"), lax.axis_index("y"), lax.axis_index("c")
        chips = [(1 - x, y), (x, 1 - y), (1 - x, 1 - y)]

        def src(k):
            lay = self.items[k][1]
            return ins[k] if lay is None else ins[k].at[lay]

        def copy(k, s, idx, to, from_src=False):
            return pltpu.make_async_remote_copy(
                src_ref=src(k) if from_src else outs[k].at[idx], dst_ref=outs[k].at[idx],
                send_sem=send_sems.at[k, s], recv_sem=recv_sems.at[k, s],
                device_id=to, device_id_type=MESH)

        def local(k):
            return pltpu.make_async_copy(src(k), outs[k].at[_dev_index(x, y, c)], local_sems.at[k])

        return x, y, c, chips, copy, local

    def start(self, ins, outs, sems, cond):
        n = len(self.items)

        @_when(cond)
        def _():
            x, y, c, chips, copy, local = self._ctx(ins, outs, sems)
            me = _dev_index(x, y, c)
            for k in range(n):
                local(k).start()
                copy(k, 0, me, (x, y, 1 - c), from_src=True).start()
                for j, chip in enumerate(chips):
                    copy(k, 1 + j, me, (chip[0], chip[1], c), from_src=True).start()

    def forward(self, ins, outs, sems, cond):
        n = len(self.items)

        @_when(cond)
        def _():
            x, y, c, chips, copy, local = self._ctx(ins, outs, sems)
            for j, chip in enumerate(chips):
                idx = _dev_index(chip[0], chip[1], c)
                for k in range(n):
                    copy(k, 1 + j, idx, (x, y, c)).wait_recv()
                    copy(k, 4 + j, idx, (x, y, 1 - c)).start()

    def finish(self, ins, outs, sems, cond):
        n = len(self.items)

        @_when(cond)
        def _():
            x, y, c, chips, copy, local = self._ctx(ins, outs, sems)
            me = _dev_index(x, y, c)
            for k in range(n):
                copy(k, 0, _dev_index(x, y, 1 - c), (x, y, c)).wait_recv()
            for j, chip in enumerate(chips):
                idx = _dev_index(chip[0], chip[1], 1 - c)
                for k in range(n):
                    copy(k, 4 + j, idx, (x, y, c)).wait_recv()
            for k in range(n):
                for s in range(4):
                    copy(k, s, me, (x, y, c), from_src=True).wait_send()
                for j, chip in enumerate(chips):
                    copy(k, 4 + j, _dev_index(chip[0], chip[1], c), (x, y, c)).wait_send()
                local(k).wait()


_PEER_FLIPS = [(0, 0, 1), (1, 0, 0), (0, 1, 0), (1, 1, 0), (1, 0, 1), (0, 1, 1), (1, 1, 1)]


def _call(body, *, name, grid, in_specs, out_specs, out_shape, args, scratch=(), comm=None):
    n_hi, n_ho, n_hs = len(args), len(out_shape), len(scratch)
    c_args = list(comm.args) if comm else []
    c_out = list(comm.out_shape) if comm else []
    c_scr = list(comm.scratch) if comm else []
    phases = _phases(grid)

    def kern(*refs):
        cuts = [n_hi, len(c_args), n_ho, len(c_out), n_hs, len(c_scr)]
        parts, pos = [], 0
        for n in cuts:
            parts.append(refs[pos:pos + n])
            pos += n
        hi, ci, ho, co, hs, cs = parts
        if comm:
            first, mid, last = phases()
            comm.start(ci, co, cs, first)
            comm.forward(ci, co, cs, mid)
        body(*hi, *ho, *hs)
        if comm:
            comm.finish(ci, co, cs, last)

    sem = ("arbitrary",) * len(grid) if grid else None
    return pl.pallas_call(
        kern, name=name, grid=grid,
        in_specs=list(in_specs) + [ANY] * len(c_args),
        out_specs=list(out_specs) + [ANY] * len(c_out),
        out_shape=list(out_shape) + c_out,
        scratch_shapes=list(scratch) + c_scr,
        compiler_params=pltpu.CompilerParams(dimension_semantics=sem, vmem_limit_bytes=VMEM_LIMIT),
    )(*args, *c_args)


def _comm_only(name, comm):
    return _call(lambda: None, name=name, grid=(), in_specs=[], out_specs=[], out_shape=[], args=[], comm=comm)


HBM_SPEC = pl.BlockSpec(memory_space=pltpu.HBM)
SEM_SPEC = pl.BlockSpec(memory_space=pltpu.SEMAPHORE)
SIDE_EFFECT = pltpu.SideEffectType.DATAFLOW_SIDE_EFFECTING


def _exchange_peer(x, y, c, s):
    fx, fy, fc = _PEER_FLIPS[s]
    return x ^ fx, y ^ fy, c ^ fc


def _exchange_start(name, items):
    n = len(items)
    srcs = [pltpu.with_memory_space_constraint(a, pltpu.HBM) for a, _ in items]
    land_shapes = [(N_DEV,) + (a.shape if whole else a.shape[1:]) for a, whole in items]
    lands = [pltpu.with_memory_space_constraint(lax.empty(shp, a.dtype), pltpu.HBM)
             for shp, (a, _) in zip(land_shapes, items)]

    n_sem = 7 * n

    def body(*refs):
        src_refs, land_refs = refs[:n], refs[n:2 * n]
        send_sems = refs[2 * n:2 * n + n_sem]
        recv_sems = refs[2 * n + n_sem:2 * n + 2 * n_sem]
        token = refs[-1]
        x, y, c = lax.axis_index("x"), lax.axis_index("y"), lax.axis_index("c")
        me = _dev_index(x, y, c)
        for s in range(7):
            px, py, pc = _exchange_peer(x, y, c, s)
            for k in range(n):
                src = src_refs[k] if items[k][1] else src_refs[k].at[_dev_index(px, py, pc)]
                pltpu.make_async_remote_copy(
                    src_ref=src, dst_ref=land_refs[k].at[me],
                    send_sem=send_sems[7 * k + s], recv_sem=recv_sems[7 * k + s],
                    device_id=(px, py, pc), device_id_type=MESH).start()
        token[...] = jnp.zeros(token.shape, token.dtype)

    outs = pl.pallas_call(
        body, name=name,
        out_shape=(*[pltpu.SemaphoreType.DMA(())] * (2 * n_sem),
                   *[pltpu.HBM(a.shape, a.dtype) for a in srcs],
                   *[pltpu.HBM(shp, a.dtype) for shp, a in zip(land_shapes, srcs)],
                   jax.ShapeDtypeStruct((8, 128), F32)),
        in_specs=[HBM_SPEC] * (2 * n),
        out_specs=(*[SEM_SPEC] * (2 * n_sem), *[HBM_SPEC] * (2 * n), pl.BlockSpec(memory_space=pltpu.VMEM)),
        input_output_aliases={i: 2 * n_sem + i for i in range(2 * n)},
        compiler_params=pltpu.CompilerParams(has_side_effects=SIDE_EFFECT),
    )(*srcs, *lands)
    base = 2 * n_sem
    return (list(outs[:n_sem]), list(outs[n_sem:base]), list(outs[base:base + n]),
            list(outs[base + n:base + 2 * n]), outs[-1])


def _exchange_wait(name, items, send_sems, recv_sems, srcs, lands, after):
    n = len(items)

    n_sem = 7 * n

    def body(*refs):
        src_refs, land_refs = refs[:n], refs[n:2 * n]
        send_refs = refs[2 * n:2 * n + n_sem]
        recv_refs = refs[2 * n + n_sem:2 * n + 2 * n_sem]
        x, y, c = lax.axis_index("x"), lax.axis_index("y"), lax.axis_index("c")
        for s in range(7):
            for k in range(n):
                copy = pltpu.make_async_remote_copy(
                    src_ref=src_refs[k] if items[k][1] else src_refs[k].at[0], dst_ref=land_refs[k].at[0],
                    send_sem=send_refs[7 * k + s], recv_sem=recv_refs[7 * k + s],
                    device_id=(x, y, c), device_id_type=MESH)
                copy.wait_send()
                copy.wait_recv()

    outs = pl.pallas_call(
        body, name=name,
        out_shape=(*[pltpu.HBM(a.shape, a.dtype) for a in srcs], *[pltpu.HBM(a.shape, a.dtype) for a in lands]),
        in_specs=[HBM_SPEC] * (2 * n) + [SEM_SPEC] * (2 * n_sem) + [ANY],
        out_specs=tuple([HBM_SPEC] * (2 * n)),
        input_output_aliases={i: i for i in range(2 * n)},
        compiler_params=pltpu.CompilerParams(has_side_effects=SIDE_EFFECT),
    )(*srcs, *lands, *send_sems, *recv_sems, after)
    return list(outs[:n]), list(outs[n:])


def _cast_bf16(x, name):
    shape = x.shape
    x2 = x.reshape(-1, shape[-1])
    rows, cols = x2.shape
    tr = _row_tile(rows)

    def body(x_ref, o_ref):
        o_ref[...] = x_ref[...].astype(BF16)

    blk = pl.BlockSpec((tr, cols), lambda i: (i, 0))
    out, = _call(body, name=name, grid=(rows // tr,), in_specs=[blk], out_specs=[blk],
                 out_shape=[jax.ShapeDtypeStruct((rows, cols), BF16)], args=[x2])
    return out.reshape(shape)


def _norm_cast(x, g3, l, tm):
    s = x.shape[0]

    def body(x_ref, g_ref, o_ref):
        v = x_ref[...]
        o_ref[...] = (v * _rstd(v) * g_ref[...]).astype(BF16)

    row = pl.BlockSpec((tm, D_MODEL), lambda i: (i, 0))
    out, = _call(body, name="norm_cast", grid=(s // tm,),
                 in_specs=[row, pl.BlockSpec((None, 1, D_MODEL), lambda i: (l, 0, 0))], out_specs=[row],
                 out_shape=[jax.ShapeDtypeStruct((s, D_MODEL), BF16)], args=[x, g3])
    return out


def _in_proj_qkv(h, win, s, tq):
    def body(a_ref, b_ref, o_ref):
        i = pl.program_id(0)

        @pl.when(i == 0)
        def _():
            o_ref[...] = jnp.zeros(o_ref.shape, BF16)

        @pl.when(i > 0)
        def _():
            w = jnp.concatenate([b_ref[j] for j in range(4)], axis=1)
            o_ref[...] = _dot(a_ref[...], w).astype(BF16)

    out, = _call(
        body, name="in_proj_qkv", grid=(s // tq + 1,),
        in_specs=[pl.BlockSpec((tq, D_MODEL), lambda i: (jnp.maximum(i - 1, 0), 0)),
                  pl.BlockSpec((4, D_MODEL, PROJ_SHARD), lambda i: (1, 0, 0))],
        out_specs=[pl.BlockSpec((tq, 4 * PROJ_SHARD), lambda i: (i, 0))],
        out_shape=[jax.ShapeDtypeStruct((s + tq, PROJ_WIDTH // 2), BF16)], args=[h, win])
    return out


def _in_proj_conv(h, win, wc, g3, l, s, tq, comm=None):
    def body(a_ref, b_ref, wc_ref, g_ref, pc_ref, y_ref, carry_ref, acc_ref):
        i = pl.program_id(0)
        w = jnp.concatenate([b_ref[j] for j in range(4)], axis=1)
        acc_ref[...] = _dot(a_ref[...], w)
        pc_ref[...] = acc_ref[...].astype(BF16)
        gmat = _group_matrix()
        for j in range(CONV_WIDTH // 128):
            c0, c1, c2 = 128 * j, CONV_WIDTH + 128 * j, 2 * CONV_WIDTH + 128 * j
            hc = acc_ref[:, c0:c0 + 128]
            bg = acc_ref[:, c1:c1 + 128]
            cg = acc_ref[:, c2:c2 + 128]
            u_prev = jnp.where(i > 0, carry_ref[:, c0:c0 + 128], 0.0)
            u = cg * hc
            carry_ref[:, c0:c0 + 128] = u[tq - 8:tq, :]
            full = jnp.concatenate([u_prev, u], axis=0)
            u1 = pltpu.roll(full, 1, 0)[8:]
            u2 = pltpu.roll(full, 2, 0)[8:]
            out = (u2 * wc_ref[0:1, c0:c0 + 128] + u1 * wc_ref[1:2, c0:c0 + 128]
                   + u * wc_ref[2:3, c0:c0 + 128])
            yc = bg * out
            r = lax.rsqrt(_group_mean(yc * yc, gmat) + EPS)
            y_ref[:, c0:c0 + 128] = (yc * r * g_ref[:, c0:c0 + 128]).astype(BF16)

    return _call(
        body, name="in_proj_conv", grid=(s // tq,),
        in_specs=[pl.BlockSpec((tq, D_MODEL), lambda i: (i, 0)),
                  pl.BlockSpec((4, D_MODEL, PROJ_SHARD), lambda i: (0, 0, 0)),
                  pl.BlockSpec((None, 8, CONV_WIDTH), lambda i: (l, 0, 0)),
                  pl.BlockSpec((None, 1, CONV_WIDTH), lambda i: (l, 0, 0))],
        out_specs=[pl.BlockSpec((tq, 3 * CONV_WIDTH), lambda i: (i, 0)),
                   pl.BlockSpec((tq, CONV_WIDTH), lambda i: (i, 0))],
        out_shape=[jax.ShapeDtypeStruct((s, 3 * CONV_WIDTH), BF16), jax.ShapeDtypeStruct((s, CONV_WIDTH), BF16)],
        scratch=[pltpu.VMEM((8, CONV_WIDTH), F32), pltpu.VMEM((tq, 3 * CONV_WIDTH), F32)],
        args=[h, win, wc, g3], comm=comm)


def _toeplitz_source():
    r_i = lax.broadcasted_iota(jnp.int32, (REL_PAD, TOEP), 0)
    m_i = lax.broadcasted_iota(jnp.int32, (REL_PAD, TOEP), 1)
    idx = jnp.clip((K_BAND - 1) - m_i, -REL_CLIP, REL_CLIP) + REL_CLIP
    return jnp.where(r_i == idx, 1.0, 0.0).astype(BF16)


def _bias_build(rbp, comm=None):
    n_layers = rbp.shape[0]

    def body(rb_ref, o_ref, t_ref):
        pmat = _toeplitz_source()
        hi, mid, lo = _split3(rb_ref[...])
        t_ref[...] = _dot(hi, pmat) + _dot(mid, pmat) + _dot(lo, pmat)
        shift = (CHUNK - 1) - lax.broadcasted_iota(jnp.int32, (CHUNK, TOEP), 0)
        kchunk = lax.broadcasted_iota(jnp.int32, (CHUNK, K_BAND), 1) >> 6
        for h in range(N_HEADS):
            b = jnp.broadcast_to(t_ref[pl.ds(h, 1), :], (CHUNK, TOEP))
            for bit in range(6):
                rolled = pltpu.roll(b, TOEP - (1 << bit), 1)
                b = jnp.where(((shift >> bit) & 1) == 1, rolled, b)
            for cq in range(Q_BLOCK // CHUNK):
                off = CHUNK * (Q_BLOCK // CHUNK - 1 - cq)
                band = pltpu.roll(b, TOEP - off, 1) if off else b
                dchunk = kchunk - cq
                in_band = jnp.where(dchunk >= 0, jnp.where(dchunk <= N_LEFT_CHUNKS, 1, 0), 0) == 1
                o_ref[h, CHUNK * cq:CHUNK * (cq + 1), :] = jnp.where(in_band, band[:, :K_BAND], NEG_INF)

    return _call(
        body, name="bias_build", grid=(n_layers,),
        in_specs=[pl.BlockSpec((None, N_HEADS, REL_PAD), lambda l: (l, 0, 0))],
        out_specs=[pl.BlockSpec((None, N_HEADS, Q_BLOCK, K_BAND), lambda l: (l, 0, 0, 0))],
        out_shape=[jax.ShapeDtypeStruct((n_layers, N_HEADS, Q_BLOCK, K_BAND), F32)],
        scratch=[pltpu.VMEM((N_HEADS, TOEP), F32)], args=[rbp], comm=comm)


def _bias_bwd(ds_sum):
    n_layers = ds_sum.shape[0]

    def body(ds_ref, o_ref, t_ref):
        pmat = _toeplitz_source()
        shift = (CHUNK - 1) - lax.broadcasted_iota(jnp.int32, (CHUNK, TOEP), 0)
        for h in range(N_HEADS):
            d = None
            for cq in range(Q_BLOCK // CHUNK):
                off = CHUNK * (Q_BLOCK // CHUNK - 1 - cq)
                part = jnp.concatenate([ds_ref[h, CHUNK * cq:CHUNK * (cq + 1), :],
                                        jnp.zeros((CHUNK, TOEP - K_BAND), F32)], axis=1)
                part = pltpu.roll(part, off, 1) if off else part
                d = part if d is None else d + part
            for bit in range(6):
                rolled = pltpu.roll(d, 1 << bit, 1)
                d = jnp.where(((shift >> bit) & 1) == 1, rolled, d)
            t_ref[pl.ds(h, 1), :] = jnp.sum(d, axis=0, keepdims=True)
        hi, mid, lo = _split3(t_ref[...])
        o_ref[...] = _dot_nt(hi, pmat) + _dot_nt(mid, pmat) + _dot_nt(lo, pmat)

    out, = _call(
        body, name="bias_bwd", grid=(n_layers,),
        in_specs=[pl.BlockSpec((None, N_HEADS, Q_BLOCK, K_BAND), lambda l: (l, 0, 0, 0))],
        out_specs=[pl.BlockSpec((None, N_HEADS, REL_PAD), lambda l: (l, 0, 0))],
        out_shape=[jax.ShapeDtypeStruct((n_layers, N_HEADS, REL_PAD), F32)],
        scratch=[pltpu.VMEM((N_HEADS, TOEP), F32)], args=[ds_sum])
    return out


def _attn_fwd(qkvp, biasm, g3, l, s, pad, comm=None):
    rows = Q_STEP * Q_BLOCK
    nb = s // rows
    qb0 = pad // rows
    scale = HEAD_DIM ** -0.5
    wide = 128 * ATTN_PAIRS

    def body(q_ref, k_ref, v_ref, b_ref, g_ref, o_ref, lse_ref, yn_ref):
        blk = pl.program_id(1)
        lane = lax.broadcasted_iota(jnp.int32, (1, 128), 1)
        gmat = _group_matrix()

        def step(masked):
            for sub in range(Q_STEP):
                rs = slice(Q_BLOCK * sub, Q_BLOCK * (sub + 1))
                start = blk * rows + Q_BLOCK * sub
                koff = pl.multiple_of(start + (pad - LEFT), Q_BLOCK)
                kpos = lax.broadcasted_iota(jnp.int32, (1, K_BAND), 1) + (start - LEFT)
                kmask = jnp.where(kpos >= 0, 0.0, NEG_INF)
                for pr in range(ATTN_PAIRS):
                    ls = slice(128 * pr, 128 * (pr + 1))
                    q = q_ref[rs, ls]
                    kb = k_ref[pl.ds(koff, K_BAND), ls]
                    vb = v_ref[pl.ds(koff, K_BAND), ls]
                    outs, lses = [], []
                    for hh in range(2):
                        in_head = (lane >> 6) == hh
                        qm = jnp.where(in_head, q, jnp.zeros_like(q)) * jnp.asarray(scale, BF16)
                        sc = _dot_nt(qm, kb) + b_ref[2 * pr + hh]
                        if masked:
                            sc = sc + kmask
                        m = jnp.max(sc, axis=1, keepdims=True)
                        e = jnp.exp(sc - m)
                        den = jnp.sum(e, axis=1, keepdims=True)
                        outs.append(_dot(e.astype(BF16), vb) * (1.0 / den))
                        lses.append(m + jnp.log(den))
                    first = lane < HEAD_DIM
                    o = jnp.where(first, outs[0], outs[1])
                    o_ref[rs, ls] = o
                    lse_ref[rs, ls] = jnp.where(first, lses[0], lses[1])
                    r = lax.rsqrt(_group_mean(o * o, gmat) + EPS)
                    yn_ref[rs, ls] = (o * r * g_ref[:, ls]).astype(BF16)

        pl.when(blk == 0)(lambda: step(True))
        pl.when(blk > 0)(lambda: step(False))

    blk_out = pl.BlockSpec((rows, wide), lambda p, b: (b, p))
    n_grp = ATTN_WIDTH // wide
    return _call(
        body, name="attn_fwd", grid=(n_grp, nb),
        in_specs=[pl.BlockSpec((rows, wide), lambda p, b: (qb0 + b, p)),
                  pl.BlockSpec((s + pad, wide), lambda p, b: (0, n_grp + p)),
                  pl.BlockSpec((s + pad, wide), lambda p, b: (0, 2 * n_grp + p)),
                  pl.BlockSpec((None, 2 * ATTN_PAIRS, Q_BLOCK, K_BAND), lambda p, b: (l, p, 0, 0)),
                  pl.BlockSpec((None, 1, wide), lambda p, b: (l, 0, p))],
        out_specs=[blk_out, blk_out, blk_out],
        out_shape=[jax.ShapeDtypeStruct((s, ATTN_WIDTH), F32),
                   jax.ShapeDtypeStruct((s, ATTN_WIDTH), F32),
                   jax.ShapeDtypeStruct((s, ATTN_WIDTH), BF16)],
        args=[qkvp, qkvp, qkvp, biasm, g3], comm=comm)


def _out_proj_fwd(ync, yna, wout, x, g_post3, g_next3, l, s, tm):
    half = D_MODEL // 2

    def body(a1_ref, a2_ref, w_ref, x_ref, gp_ref, gn_ref, z_ref, xm_ref, h_ref):
        for rs in _row_subtiles(tm, SUB_ROWS):
            z = _dot(a1_ref[rs, :], w_ref[0:half, :]) + _dot(a2_ref[rs, :], w_ref[half:D_MODEL, :])
            z_ref[rs, :] = z.astype(BF16)
            xm = x_ref[rs, :] + z * _rstd(z) * gp_ref[...]
            xm_ref[rs, :] = xm
            h_ref[rs, :] = (xm * _rstd(xm) * gn_ref[...]).astype(BF16)

    row = pl.BlockSpec((tm, D_MODEL), lambda i: (i, 0))
    gain = pl.BlockSpec((None, 1, D_MODEL), lambda i: (l, 0, 0))
    return _call(
        body, name="out_proj_fwd", grid=(s // tm,),
        in_specs=[pl.BlockSpec((tm, half), lambda i: (i, 0)), pl.BlockSpec((tm, half), lambda i: (i, 0)),
                  pl.BlockSpec((D_MODEL, D_MODEL), lambda i: (0, 0)), row, gain, gain],
        out_specs=[row, row, row],
        out_shape=[jax.ShapeDtypeStruct((s, D_MODEL), BF16), jax.ShapeDtypeStruct((s, D_MODEL), F32),
                   jax.ShapeDtypeStruct((s, D_MODEL), BF16)],
        args=[ync, yna, wout, x, g_post3, g_next3])


def _ffn_in_fwd(h2, wfin4, s, tm, comm=None):
    def body(h_ref, wg_ref, wu_ref, gu_ref, act_ref):
        h = h_ref[...]
        for cs in _COL_SUBTILES:
            gate = _dot_nt(h, wg_ref[cs, :])
            up = _dot_nt(h, wu_ref[cs, :])
            gu_ref[0, :, cs] = gate.astype(BF16)
            gu_ref[1, :, cs] = up.astype(BF16)
            act_ref[:, cs] = (gate * (1.0 / (1.0 + jnp.exp(-gate))) * up).astype(BF16)

    return _call(
        body, name="ffn_in_fwd", grid=(2, s // tm),
        in_specs=[pl.BlockSpec((tm, D_MODEL), lambda b, i: (i, 0)),
                  pl.BlockSpec((None, FF_PAIR, D_MODEL), lambda b, i: (b, 0, 0)),
                  pl.BlockSpec((None, FF_PAIR, D_MODEL), lambda b, i: (2 + b, 0, 0))],
        out_specs=[pl.BlockSpec((2, tm, FF_PAIR), lambda b, i: (0, i, b)),
                   pl.BlockSpec((tm, FF_PAIR), lambda b, i: (i, b))],
        out_shape=[jax.ShapeDtypeStruct((2, s, D_FF), BF16), jax.ShapeDtypeStruct((s, D_FF), BF16)],
        args=[h2, wfin4, wfin4], comm=comm)


def _ffn_out_fwd(act, wfo, xm, g_post3, g_next3, l, l_next, s, tm, comm=None):
    def body(a_ref, w_ref, x_ref, gp_ref, gn_ref, f_ref, xo_ref, h_ref):
        for rs in _row_subtiles(tm, SUB_ROWS):
            f = _dot(a_ref[rs, :], w_ref[...])
            f_ref[rs, :] = f.astype(BF16)
            xo = x_ref[rs, :] + f * _rstd(f) * gp_ref[...]
            xo_ref[rs, :] = xo
            h_ref[rs, :] = (xo * _rstd(xo) * gn_ref[...]).astype(BF16)

    row = pl.BlockSpec((tm, D_MODEL), lambda i: (i, 0))
    return _call(
        body, name="ffn_out_fwd", grid=(s // tm,),
        in_specs=[pl.BlockSpec((tm, D_FF), lambda i: (i, 0)),
                  pl.BlockSpec((D_FF, D_MODEL), lambda i: (0, 0), pipeline_mode=pl.Buffered(1)), row,
                  pl.BlockSpec((None, 1, D_MODEL), lambda i: (l, 0, 0)),
                  pl.BlockSpec((None, 1, D_MODEL), lambda i: (l_next, 0, 0))],
        out_specs=[row, row, row],
        out_shape=[jax.ShapeDtypeStruct((s, D_MODEL), BF16), jax.ShapeDtypeStruct((s, D_MODEL), F32),
                   jax.ShapeDtypeStruct((s, D_MODEL), BF16)],
        args=[act, wfo, xm, g_post3, g_next3], comm=comm)


def _ffn_out_loss(act, wfo, xm, g_post3, target, l, s, tm):
    def body(a_ref, w_ref, x_ref, gp_ref, t_ref, dx_ref, sq_ref, df_ref, dg_ref):
        _zero_first((sq_ref, dg_ref), pl.program_id(0) == 0)
        for rs in _row_subtiles(tm, SUB_ROWS):
            f = _dot(a_ref[rs, :], w_ref[...])
            gain = gp_ref[...]
            err = x_ref[rs, :] + f * _rstd(f) * gain - t_ref[rs, :]
            dx = err * (1.0 / D_MODEL)
            dx_ref[rs, :] = dx
            df, dyn = _norm_bwd_rows(f, gain, dx)
            df_ref[rs, :] = df.astype(BF16)
            _add_cols(dg_ref, dyn)
            cs = jnp.sum(err * err, axis=0, keepdims=True)
            part = cs[:, 0:128]
            for k in range(1, D_MODEL // 128):
                part = part + cs[:, 128 * k:128 * (k + 1)]
            sq_ref[0:1, :] += part

    row = pl.BlockSpec((tm, D_MODEL), lambda i: (i, 0))
    return _call(
        body, name="ffn_out_loss", grid=(s // tm,),
        in_specs=[pl.BlockSpec((tm, D_FF), lambda i: (i, 0)),
                  pl.BlockSpec((D_FF, D_MODEL), lambda i: (0, 0), pipeline_mode=pl.Buffered(1)), row,
                  pl.BlockSpec((None, 1, D_MODEL), lambda i: (l, 0, 0)), row],
        out_specs=[row, pl.BlockSpec((8, 128), lambda i: (0, 0)), row, pl.BlockSpec((8, D_MODEL), lambda i: (0, 0))],
        out_shape=[jax.ShapeDtypeStruct((s, D_MODEL), F32), jax.ShapeDtypeStruct((8, 128), F32),
                   jax.ShapeDtypeStruct((s, D_MODEL), BF16), jax.ShapeDtypeStruct((8, D_MODEL), F32)],
        args=[act, wfo, xm, g_post3, target])


def _norm_bwd_rows(v, g, dy):
    r = _rstd(v)
    vn = v * r
    gd = dy * g
    dv = r * (gd - vn * jnp.mean(vn * gd, axis=-1, keepdims=True))
    return dv, dy * vn


def _zero_first(refs, first):
    @pl.when(first)
    def _():
        for ref in refs:
            ref[...] = jnp.zeros(ref.shape, F32)


def _add_cols(ref, val):
    ref[0:1, :] += jnp.sum(val, axis=0, keepdims=True)


def _accum_cols(ref, val, first):
    _zero_first((ref,), first)
    _add_cols(ref, val)


def _row_subtiles(rows, sub):
    sub = min(sub, rows)
    return [slice(r, r + sub) for r in range(0, rows, sub)]


def _ffn_out_bwd(df, wfo, gu, act, s, tm, comm=None):
    nm = s // tm

    def body(df_ref, w_ref, gu_ref, act_ref, dgu_ref, dw_ref, acc_ref):
        i = pl.program_id(1)
        df = df_ref[...]
        _zero_first((acc_ref,), i == 0)
        acc_ref[...] += _dot_tn(act_ref[...], df)

        @pl.when(i == nm - 1)
        def _():
            dw_ref[...] = acc_ref[...].astype(BF16)

        for cs in _COL_SUBTILES:
            da = _dot_nt(df, w_ref[cs, :])
            g = gu_ref[0, :, cs].astype(F32)
            u = gu_ref[1, :, cs].astype(F32)
            sg = 1.0 / (1.0 + jnp.exp(-g))
            dgu_ref[0, :, cs] = (da * u * (sg * (1.0 + g * (1.0 - sg)))).astype(BF16)
            dgu_ref[1, :, cs] = (da * (g * sg)).astype(BF16)

    blk = pl.BlockSpec((2, tm, FF_PAIR), lambda b, i: (0, i, b))
    wblk = pl.BlockSpec((FF_PAIR, D_MODEL), lambda b, i: (b, 0))
    return _call(
        body, name="ffn_out_bwd", grid=(2, nm),
        in_specs=[pl.BlockSpec((tm, D_MODEL), lambda b, i: (i, 0)), wblk, blk,
                  pl.BlockSpec((tm, FF_PAIR), lambda b, i: (i, b))],
        out_specs=[blk, wblk],
        out_shape=[jax.ShapeDtypeStruct((2, s, D_FF), BF16), jax.ShapeDtypeStruct((D_FF, D_MODEL), BF16)],
        scratch=[pltpu.VMEM((FF_PAIR, D_MODEL), F32)],
        args=[df, wfo, gu, act], comm=comm)


def _dw_ffn_in(h2, dgu, s):
    def body(a_ref, b_ref, o_ref):
        o_ref[...] = _dot_tn(b_ref[...], a_ref[...]).astype(BF16)

    out, = _call(
        body, name="dw_ffn_in", grid=(4,),
        in_specs=[pl.BlockSpec((s, D_MODEL), lambda n: (0, 0), pipeline_mode=pl.Buffered(1)),
                  pl.BlockSpec((None, s, FF_PAIR), lambda n: (n // 2, 0, n % 2))],
        out_specs=[pl.BlockSpec((None, FF_PAIR, D_MODEL), lambda n: (n, 0, 0))],
        out_shape=[jax.ShapeDtypeStruct((4, FF_PAIR, D_MODEL), BF16)], args=[h2, dgu])
    return out


def _ffn_in_bwd(dgu, wfin, xm, g_pre3, dres, z, g_post3, l, s, tm, comm=None):
    def body(d_ref, w_ref, xm_ref, gp_ref, dres_ref, z_ref, gq_ref, dxm_ref, dz_ref, dgp_ref, dgq_ref):
        _zero_first((dgp_ref, dgq_ref), pl.program_id(0) == 0)
        for rs in _row_subtiles(tm, SUB_ROWS):
            dh = _dot(d_ref[0, rs, :], w_ref[0:D_FF, :]) + _dot(d_ref[1, rs, :], w_ref[D_FF:2 * D_FF, :])
            dx, dyn = _norm_bwd_rows(xm_ref[rs, :], gp_ref[...], dh)
            dxm = dres_ref[rs, :] + dx
            dxm_ref[rs, :] = dxm
            _add_cols(dgp_ref, dyn)
            dz, dyn2 = _norm_bwd_rows(z_ref[rs, :].astype(F32), gq_ref[...], dxm)
            dz_ref[rs, :] = dz.astype(BF16)
            _add_cols(dgq_ref, dyn2)

    row = pl.BlockSpec((tm, D_MODEL), lambda i: (i, 0))
    gain = pl.BlockSpec((None, 1, D_MODEL), lambda i: (l, 0, 0))
    dgs = pl.BlockSpec((8, D_MODEL), lambda i: (0, 0))
    return _call(
        body, name="ffn_in_bwd", grid=(s // tm,),
        in_specs=[pl.BlockSpec((2, tm, D_FF), lambda i: (0, i, 0)),
                  pl.BlockSpec((2 * D_FF, D_MODEL), lambda i: (0, 0), pipeline_mode=pl.Buffered(1)),
                  row, gain, row, row, gain],
        out_specs=[row, row, dgs, dgs],
        out_shape=[jax.ShapeDtypeStruct((s, D_MODEL), F32), jax.ShapeDtypeStruct((s, D_MODEL), BF16),
                   jax.ShapeDtypeStruct((8, D_MODEL), F32), jax.ShapeDtypeStruct((8, D_MODEL), F32)],
        args=[dgu, wfin, xm, g_pre3, dres, z, g_post3], comm=comm)


def _out_proj_bwd(dz, wout, o, g3, ync, yna, l, s, tm):
    nm = s // tm
    half = D_MODEL // 2

    def body(dz_ref, w_ref, o_ref, g_ref, a1_ref, a2_ref, dyc_ref, do_ref, dg_ref, dw_ref, acc_ref):
        i = pl.program_id(0)
        gmat = _group_matrix()
        _zero_first((dg_ref, acc_ref), i == 0)
        dzv = dz_ref[...]
        acc_ref[0:half, :] += _dot_tn(a1_ref[...], dzv)
        acc_ref[half:D_MODEL, :] += _dot_tn(a2_ref[...], dzv)

        @pl.when(i == nm - 1)
        def _():
            dw_ref[...] = acc_ref[...].astype(BF16)

        for rs in _row_subtiles(tm, SUB_ROWS):
            dy = _dot_nt(dz_ref[rs, :], w_ref[...])
            dyc_ref[rs, :] = dy[:, 0:CONV_WIDTH]
            for j in range(ATTN_WIDTH // 128):
                c0 = 128 * j
                ov = o_ref[rs, c0:c0 + 128]
                dyn = dy[:, CONV_WIDTH + c0:CONV_WIDTH + c0 + 128]
                r = lax.rsqrt(_group_mean(ov * ov, gmat) + EPS)
                on = ov * r
                gd = dyn * g_ref[:, c0:c0 + 128]
                do_ref[rs, c0:c0 + 128] = r * (gd - on * _group_mean(on * gd, gmat))
                dg_ref[0:1, c0:c0 + 128] += jnp.sum(dyn * on, axis=0, keepdims=True)

    halfrow = pl.BlockSpec((tm, ATTN_WIDTH), lambda i: (i, 0))
    return _call(
        body, name="out_proj_bwd", grid=(nm,),
        in_specs=[pl.BlockSpec((tm, D_MODEL), lambda i: (i, 0)),
                  pl.BlockSpec((D_MODEL, D_MODEL), lambda i: (0, 0)), halfrow,
                  pl.BlockSpec((None, 1, ATTN_WIDTH), lambda i: (l, 0, 0)), halfrow, halfrow],
        out_specs=[halfrow, halfrow, pl.BlockSpec((8, ATTN_WIDTH), lambda i: (0, 0)),
                   pl.BlockSpec((D_MODEL, D_MODEL), lambda i: (0, 0))],
        out_shape=[jax.ShapeDtypeStruct((s, CONV_WIDTH), F32), jax.ShapeDtypeStruct((s, ATTN_WIDTH), F32),
                   jax.ShapeDtypeStruct((8, ATTN_WIDTH), F32), jax.ShapeDtypeStruct((D_MODEL, D_MODEL), BF16)],
        scratch=[pltpu.VMEM((D_MODEL, D_MODEL), F32)],
        args=[dz, wout, o, g3, ync, yna])


def _conv_bwd(pc, dyc, wc, g3, dq, dk, dv, l, s, tr):
    hb = tr // 8
    nt = s // tr
    ext = tr + 16
    last_hb = s // 8 - 1

    def body(pc_ref, prev_ref, next_ref, dy_ref, dyn_ref, wc_ref, g_ref, dq_ref, dk_ref, dv_ref,
             dpc_ref, dw_ref, dg_ref):
        i = pl.program_id(0)
        for part, ref in enumerate((dq_ref, dk_ref, dv_ref)):
            c = 3 * CONV_WIDTH + ATTN_WIDTH * part
            dpc_ref[:, c:c + ATTN_WIDTH] = ref[...]
        gmat = _group_matrix()
        row = lax.broadcasted_iota(jnp.int32, (ext, 128), 0) + (i * tr - 8)
        inside = jnp.where(row >= 0, jnp.where(row < s, 1, 0), 0) == 1

        @pl.when(i == 0)
        def _():
            dw_ref[...] = jnp.zeros(dw_ref.shape, F32)
            dg_ref[...] = jnp.zeros(dg_ref.shape, F32)

        def extend(ref_prev, ref_mid, ref_next, c):
            if ref_prev is None:
                before = jnp.zeros((8, 128), F32)
            else:
                before = ref_prev[:, c:c + 128].astype(F32)[ref_prev.shape[0] - 8:]
            after = ref_next[:, c:c + 128].astype(F32)[0:8]
            return jnp.concatenate([before, ref_mid[:, c:c + 128].astype(F32), after], axis=0)

        for j in range(CONV_WIDTH // 128):
            c0, c1, c2 = 128 * j, CONV_WIDTH + 128 * j, 2 * CONV_WIDTH + 128 * j
            hc = extend(prev_ref, pc_ref, next_ref, c0)
            bg = extend(prev_ref, pc_ref, next_ref, c1)
            cg = extend(prev_ref, pc_ref, next_ref, c2)
            dyn = extend(None, dy_ref, dyn_ref, c0)
            w0, w1, w2 = (wc_ref[0:1, c0:c0 + 128], wc_ref[1:2, c0:c0 + 128], wc_ref[2:3, c0:c0 + 128])
            gain = g_ref[:, c0:c0 + 128]
            u = jnp.where(inside, cg * hc, 0.0)
            u1 = pltpu.roll(u, 1, 0)
            u2 = pltpu.roll(u, 2, 0)
            out = u2 * w0 + u1 * w1 + u * w2
            yc = bg * out
            r = lax.rsqrt(_group_mean(yc * yc, gmat) + EPS)
            ycn = yc * r
            gd = dyn * gain
            dyc = r * (gd - ycn * _group_mean(ycn * gd, gmat))
            dout = jnp.where(inside, dyc * bg, 0.0)
            du = dout * w2 + pltpu.roll(dout, ext - 1, 0) * w1 + pltpu.roll(dout, ext - 2, 0) * w0
            sl = slice(8, 8 + tr)
            dpc_ref[:, c0:c0 + 128] = (du[sl] * cg[sl]).astype(BF16)
            dpc_ref[:, c1:c1 + 128] = (dyc[sl] * out[sl]).astype(BF16)
            dpc_ref[:, c2:c2 + 128] = (du[sl] * hc[sl]).astype(BF16)
            dw_ref[0:1, c0:c0 + 128] += jnp.sum(dout[sl] * u2[sl], axis=0, keepdims=True)
            dw_ref[1:2, c0:c0 + 128] += jnp.sum(dout[sl] * u1[sl], axis=0, keepdims=True)
            dw_ref[2:3, c0:c0 + 128] += jnp.sum(dout[sl] * u[sl], axis=0, keepdims=True)
            dg_ref[0:1, c0:c0 + 128] += jnp.sum(dyn[sl] * ycn[sl], axis=0, keepdims=True)

    wide = 3 * CONV_WIDTH
    return _call(
        body, name="conv_bwd", grid=(nt,),
        in_specs=[pl.BlockSpec((tr, wide), lambda i: (i, 0)),
                  pl.BlockSpec((16, wide), lambda i: (jnp.maximum(i * (hb // 2) - 1, 0), 0)),
                  pl.BlockSpec((16, wide), lambda i: (jnp.minimum((i + 1) * (hb // 2), last_hb // 2), 0)),
                  pl.BlockSpec((tr, CONV_WIDTH), lambda i: (i, 0)),
                  pl.BlockSpec((8, CONV_WIDTH), lambda i: (jnp.minimum((i + 1) * hb, last_hb), 0)),
                  pl.BlockSpec((None, 8, CONV_WIDTH), lambda i: (l, 0, 0)),
                  pl.BlockSpec((None, 1, CONV_WIDTH), lambda i: (l, 0, 0)),
                  pl.BlockSpec((tr, ATTN_WIDTH), lambda i: (i, 0)),
                  pl.BlockSpec((tr, ATTN_WIDTH), lambda i: (i, 0)),
                  pl.BlockSpec((tr, ATTN_WIDTH), lambda i: (i, 0))],
        out_specs=[pl.BlockSpec((tr, PROJ_WIDTH), lambda i: (i, 0)),
                   pl.BlockSpec((8, CONV_WIDTH), lambda i: (0, 0)),
                   pl.BlockSpec((8, CONV_WIDTH), lambda i: (0, 0))],
        out_shape=[jax.ShapeDtypeStruct((s, PROJ_WIDTH), BF16), jax.ShapeDtypeStruct((8, CONV_WIDTH), F32),
                   jax.ShapeDtypeStruct((8, CONV_WIDTH), F32)],
        args=[pc, pc, pc, dyc, dyc, wc, g3, dq, dk, dv])


def _attn_bwd(qkvp, biasm, o, lse, do, l, s, pad, comm=None):
    rows = Q_STEP * Q_BLOCK
    nb = s // rows
    qb0 = pad // rows
    scale = HEAD_DIM ** -0.5
    wide = 128 * ATTN_PAIRS

    def body(q_ref, k_ref, v_ref, b_ref, o_ref, lse_ref, do_ref,
             dq_ref, dk_ref, dv_ref, ds_ref, dk_acc, dv_acc):
        blk = pl.program_id(1)

        @pl.when(blk == 0)
        def _():
            dk_acc[...] = jnp.zeros(dk_acc.shape, F32)
            dv_acc[...] = jnp.zeros(dv_acc.shape, F32)
            ds_ref[...] = jnp.zeros(ds_ref.shape, F32)

        lane = lax.broadcasted_iota(jnp.int32, (1, 128), 1)

        def step(masked):
            for sub in range(Q_STEP):
                rs = slice(Q_BLOCK * sub, Q_BLOCK * (sub + 1))
                start = blk * rows + Q_BLOCK * sub
                koff = pl.multiple_of(start + (pad - LEFT), Q_BLOCK)
                kpos = lax.broadcasted_iota(jnp.int32, (1, K_BAND), 1) + (start - LEFT)
                kmask = jnp.where(kpos >= 0, 0.0, NEG_INF)
                for pr in range(ATTN_PAIRS):
                    ls = slice(128 * pr, 128 * (pr + 1))
                    q = q_ref[rs, ls]
                    kb = k_ref[pl.ds(koff, K_BAND), ls]
                    vb = v_ref[pl.ds(koff, K_BAND), ls]
                    dov = do_ref[rs, ls]
                    lse_v = lse_ref[rs, ls]
                    prod = dov * o_ref[rs, ls]
                    dq_parts = []
                    dk_new = jnp.zeros((128, K_BAND), F32)
                    dv_new = jnp.zeros((128, K_BAND), F32)
                    for hh in range(2):
                        in_head = (lane >> 6) == hh
                        qm = jnp.where(in_head, q, jnp.zeros_like(q)) * jnp.asarray(scale, BF16)
                        dom = jnp.where(in_head, dov, 0.0).astype(BF16)
                        delta = jnp.sum(jnp.where(in_head, prod, 0.0), axis=1, keepdims=True)
                        lse_h = lse_v[:, HEAD_DIM * hh:HEAD_DIM * hh + 1]
                        sc = _dot_nt(qm, kb) + b_ref[2 * pr + hh]
                        if masked:
                            sc = sc + kmask
                        p = jnp.exp(sc - lse_h)
                        dp = _dot_nt(dom, vb)
                        ds = p * (dp - delta)
                        ds_ref[2 * pr + hh] += ds
                        dsb = ds.astype(BF16)
                        dq_parts.append(_dot(dsb, kb) * scale)
                        dk_new = dk_new + _dot_tn(qm, dsb)
                        dv_new = dv_new + _dot_tn(dom, p.astype(BF16))
                    dq_ref[rs, ls] = jnp.where(lane < HEAD_DIM, dq_parts[0], dq_parts[1]).astype(BF16)
                    dk_acc[pr, :, pl.ds(koff, K_BAND)] += dk_new
                    dv_acc[pr, :, pl.ds(koff, K_BAND)] += dv_new

        pl.when(blk == 0)(lambda: step(True))
        pl.when(blk > 0)(lambda: step(False))

        @pl.when(blk == nb - 1)
        def _():
            for pr in range(ATTN_PAIRS):
                ls = slice(128 * pr, 128 * (pr + 1))
                dk_ref[:, ls] = dk_acc[pr, :, pad:pad + s].T.astype(BF16)
                dv_ref[:, ls] = dv_acc[pr, :, pad:pad + s].T.astype(BF16)

    n_grp = ATTN_WIDTH // wide
    qblk = pl.BlockSpec((rows, wide), lambda p, b: (b, p))
    col = pl.BlockSpec((s, wide), lambda p, b: (0, p))
    shp = jax.ShapeDtypeStruct((s, ATTN_WIDTH), BF16)
    return _call(
        body, name="attn_bwd", grid=(n_grp, nb),
        in_specs=[pl.BlockSpec((rows, wide), lambda p, b: (qb0 + b, p)),
                  pl.BlockSpec((s + pad, wide), lambda p, b: (0, n_grp + p)),
                  pl.BlockSpec((s + pad, wide), lambda p, b: (0, 2 * n_grp + p)),
                  pl.BlockSpec((None, 2 * ATTN_PAIRS, Q_BLOCK, K_BAND), lambda p, b: (l, p, 0, 0)),
                  qblk, qblk, qblk],
        out_specs=[qblk, col, col, pl.BlockSpec((2 * ATTN_PAIRS, Q_BLOCK, K_BAND), lambda p, b: (p, 0, 0))],
        out_shape=[shp, shp, shp, jax.ShapeDtypeStruct((N_HEADS, Q_BLOCK, K_BAND), F32)],
        scratch=[pltpu.VMEM((ATTN_PAIRS, 128, s + pad), F32), pltpu.VMEM((ATTN_PAIRS, 128, s + pad), F32)],
        args=[qkvp, qkvp, qkvp, biasm, o, lse, do], comm=comm)


def _dw_in(h, dproj, s):
    def body(a_ref, b_ref, o_ref):
        acc = _dot_tn(a_ref[...], b_ref[...])
        o_ref[0] = acc[:, 0:PROJ_SHARD].astype(BF16)
        o_ref[1] = acc[:, PROJ_SHARD:2 * PROJ_SHARD].astype(BF16)

    out, = _call(
        body, name="dw_in", grid=(4,),
        in_specs=[pl.BlockSpec((s, D_MODEL), lambda n: (0, 0)),
                  pl.BlockSpec((s, 2 * PROJ_SHARD), lambda n: (0, n))],
        out_specs=[pl.BlockSpec((2, D_MODEL, PROJ_SHARD), lambda n: (n, 0, 0))],
        out_shape=[jax.ShapeDtypeStruct((N_DEV, D_MODEL, PROJ_SHARD), BF16)], args=[h, dproj])
    return out


def _in_proj_bwd(dproj, win, x, g3, dres, l, s, tm, f_prev=None, g_post3=None, comm=None):
    chain = f_prev is not None

    def body(d_ref, w_ref, x_ref, g_ref, dres_ref, *rest):
        if chain:
            f_ref, gq_ref, dx_ref, dg_ref, df_ref, dgq_ref = rest
            _zero_first((dg_ref, dgq_ref), pl.program_id(0) == 0)
        else:
            dx_ref, dg_ref = rest
            _zero_first((dg_ref,), pl.program_id(0) == 0)
        w = jnp.concatenate([w_ref[j] for j in range(N_DEV)], axis=1)
        for rs in _row_subtiles(tm, SUB_ROWS):
            dh = _dot_nt(d_ref[rs, :], w)
            dx, dyn = _norm_bwd_rows(x_ref[rs, :], g_ref[...], dh)
            dx = dres_ref[rs, :] + dx
            dx_ref[rs, :] = dx
            _add_cols(dg_ref, dyn)
            if chain:
                df, dyn2 = _norm_bwd_rows(f_ref[rs, :].astype(F32), gq_ref[...], dx)
                df_ref[rs, :] = df.astype(BF16)
                _add_cols(dgq_ref, dyn2)

    row = pl.BlockSpec((tm, D_MODEL), lambda i: (i, 0))
    dgs = pl.BlockSpec((8, D_MODEL), lambda i: (0, 0))
    in_specs = [pl.BlockSpec((tm, PROJ_WIDTH), lambda i: (i, 0)),
                pl.BlockSpec((N_DEV, D_MODEL, PROJ_SHARD), lambda i: (0, 0, 0), pipeline_mode=pl.Buffered(1)),
                row, pl.BlockSpec((None, 1, D_MODEL), lambda i: (l, 0, 0)), row]
    out_specs = [row, dgs]
    out_shape = [jax.ShapeDtypeStruct((s, D_MODEL), F32), jax.ShapeDtypeStruct((8, D_MODEL), F32)]
    args = [dproj, win, x, g3, dres]
    if chain:
        in_specs += [row, pl.BlockSpec((None, 1, D_MODEL), lambda i: (l - 1, 0, 0))]
        out_specs += [row, dgs]
        out_shape += [jax.ShapeDtypeStruct((s, D_MODEL), BF16), jax.ShapeDtypeStruct((8, D_MODEL), F32)]
        args += [f_prev, g_post3]
    return _call(body, name="in_proj_bwd", grid=(s // tm,), in_specs=in_specs, out_specs=out_specs,
                 out_shape=out_shape, args=args, comm=comm)


def _adamw(name, w, m, v, lands, owns=None, me=None):
    groups, rows, cols = w.shape
    assert len(lands) == groups
    n_part = lands[0].shape[0]
    tr = _row_tile(rows, tuple(c for c in (512, 352, 256, 176, 128, 64, 32, 16, 8) if c * cols <= 256 * 1024))
    c1 = 1.0 - ADAM_B1 ** ADAM_STEP
    c2 = 1.0 - ADAM_B2 ** ADAM_STEP
    n_own = groups if owns is not None else 0

    def body(*refs):
        if n_own:
            me_ref, refs = refs[0], refs[1:]
        w_ref, m_ref, v_ref = refs[:3]
        land_refs = refs[3:3 + groups]
        own_refs = refs[3 + groups:3 + groups + n_own]
        g_ref, d_ref, nm_ref, nv_ref = refs[3 + groups + n_own:]
        grp = pl.program_id(0)
        for gi in range(groups):
            @pl.when(grp == gi)
            def _():
                l_ref = land_refs[gi]
                g = None
                for p in range(n_part):
                    part = l_ref[p].astype(F32)
                    if n_own:
                        part = jnp.where(me_ref[0] == p, own_refs[gi][...].astype(F32), part)
                    g = part if g is None else g + part
                g_ref[...] = g
                m1 = ADAM_B1 * m_ref[...] + (1.0 - ADAM_B1) * g
                v1 = ADAM_B2 * v_ref[...] + (1.0 - ADAM_B2) * (g * g)
                nm_ref[...] = m1
                nv_ref[...] = v1
                d_ref[...] = -ADAM_LR * ((m1 / c1) / (jnp.sqrt(v1 / c2) + ADAM_EPS) + ADAM_WD * w_ref[...])

    blk = pl.BlockSpec((None, tr, cols), lambda g, i, *_: (g, i, 0))
    shp = jax.ShapeDtypeStruct((groups, rows, cols), F32)

    def land_spec(gi):
        return pl.BlockSpec((n_part, tr, cols), lambda g, i, *_: (0, jnp.where(g == gi, i, 0), 0))

    def own_spec(gi):
        if owns[gi].ndim == 3:
            return pl.BlockSpec((None, tr, cols), lambda g, i, me_ref: (me_ref[0], jnp.where(g == gi, i, 0), 0))
        return pl.BlockSpec((tr, cols), lambda g, i, me_ref: (jnp.where(g == gi, i, 0), 0))

    in_specs = [blk, blk, blk] + [land_spec(gi) for gi in range(groups)] + [own_spec(gi) for gi in range(n_own)]
    args = [w, m, v] + list(lands) + (list(owns) if n_own else [])
    if not n_own:
        return _call(body, name=name, grid=(groups, rows // tr), in_specs=in_specs,
                     out_specs=[blk, blk, blk, blk], out_shape=[shp, shp, shp, shp], args=args)
    return pl.pallas_call(
        body, name=name,
        grid_spec=pltpu.PrefetchScalarGridSpec(
            num_scalar_prefetch=1, grid=(groups, rows // tr), in_specs=in_specs, out_specs=[blk, blk, blk, blk]),
        out_shape=[shp, shp, shp, shp],
        compiler_params=pltpu.CompilerParams(dimension_semantics=("arbitrary", "arbitrary"),
                                             vmem_limit_bytes=VMEM_LIMIT),
    )(me, *args)


def _pack_small(rel, gco, gao, gpm, gqm, gpf, gqf):
    n_layers = rel.shape[0]
    relp = jnp.pad(rel, ((0, 0), (0, 0), (0, REL_PAD - rel.shape[2])))
    parts = [relp.reshape(n_layers * N_HEADS * REL_PAD // 128, 128)]
    parts += [a.reshape(-1, 128) for a in (gco, gao, gpm, gqm, gpf, gqf)]
    return jnp.concatenate(parts, axis=0)


def _pack_small_grads(d_rel, parts):
    n_layers = len(d_rel)
    keys = ("gco", "gao", "gpm", "gqm", "gpf", "gqf")
    arrays = list(d_rel) + [parts[k][l] for k in keys for l in range(n_layers)] + list(parts["wc"])
    rows = 0
    plan = []
    for l in range(n_layers):
        for h in range(N_HEADS):
            for t in range(REL_PAD // 128):
                plan.append((l, (0, h), t, rows))
                rows += 1
    for ki, k in enumerate(keys):
        for l in range(n_layers):
            for t in range(parts[k][l].shape[1] // 128):
                plan.append((n_layers * (1 + ki) + l, (0,), t, rows))
                rows += 1
    for l in range(n_layers):
        for tap in range(3):
            for t in range(CONV_WIDTH // 128):
                plan.append((n_layers * (1 + len(keys)) + l, (tap,), t, rows))
                rows += 1
    total = rows + (-rows) % 8

    def body(*refs):
        o_ref = refs[-1]
        if total > rows:
            o_ref[rows:total, :] = jnp.zeros((total - rows, 128), F32)
        for op, idx, t, dst in plan:
            lanes = slice(128 * t, 128 * (t + 1))
            if len(idx) == 2:
                o_ref[dst:dst + 1, :] = refs[op][idx[0], idx[1]:idx[1] + 1, lanes]
            else:
                o_ref[dst:dst + 1, :] = refs[op][idx[0]:idx[0] + 1, lanes]

    vmem = pl.BlockSpec(memory_space=pltpu.VMEM)
    out, = _call(body, name="pack_small_grads", grid=(), in_specs=[vmem] * len(arrays), out_specs=[vmem],
                 out_shape=[jax.ShapeDtypeStruct((total, 128), F32)], args=arrays)
    return out


def _unpack_small(p, n_layers):
    n_rel = n_layers * N_HEADS * REL_PAD // 128
    rel = p[:n_rel].reshape(n_layers, N_HEADS, REL_PAD)[:, :, :2 * REL_CLIP + 1]
    outs = [rel]
    r0 = n_rel
    for width in (CONV_WIDTH, ATTN_WIDTH, D_MODEL, D_MODEL, D_MODEL, D_MODEL):
        nr = n_layers * width // 128
        outs.append(p[r0:r0 + nr].reshape(n_layers, width))
        r0 += nr
    return outs


def kernel(x, w_in, w_conv, rel_bias, g_conv_out, g_attn_out, w_out, g_pre_mix, g_post_mix, g_pre_ffn, g_post_ffn, w_ffn_in, w_ffn_out, loss_target, m_w_in, m_w_conv, m_rel_bias, m_g_conv_out, m_g_attn_out, m_w_out, m_g_pre_mix, m_g_post_mix, m_g_pre_ffn, m_g_post_ffn, m_w_ffn_in, m_w_ffn_out, v_w_in, v_w_conv, v_rel_bias, v_g_conv_out, v_g_attn_out, v_w_out, v_g_pre_mix, v_g_post_mix, v_g_pre_ffn, v_g_post_ffn, v_w_ffn_in, v_w_ffn_out):
    n_layers = w_in.shape[0]
    s = x.shape[1]
    assert x.shape == (1, s, D_MODEL) and s % 1024 == 0
    assert w_in.shape == (n_layers, D_MODEL, PROJ_SHARD) and w_ffn_in.shape == (n_layers, D_MODEL, FF_SHARD)
    tm = 512
    tq = 1024
    tf = min(1024, s)
    x0 = x.reshape(s, D_MODEL)
    target = loss_target.reshape(s, D_MODEL)
    dev = _dev_index(lax.axis_index("x"), lax.axis_index("y"), lax.axis_index("c"))

    wt_ffn_in, mt_ffn_in, vt_ffn_in = (jnp.transpose(a, (0, 2, 1)) for a in (w_ffn_in, m_w_ffn_in, v_w_ffn_in))
    local_w = [_cast_bf16(w_in, "cast_w_in"), _cast_bf16(w_out, "cast_w_out"),
               _cast_bf16(wt_ffn_in, "cast_w_ffn_in"), _cast_bf16(w_ffn_out, "cast_w_ffn_out")]
    wc_local = jnp.pad(jnp.transpose(w_conv, (0, 2, 1)).reshape(-1), (0, 1024 - n_layers * 3 * 64)).reshape(8, 128)
    biasm, win_next, wc_g = _bias_build(jnp.pad(rel_bias, ((0, 0), (0, 0), (0, REL_PAD - rel_bias.shape[2]))),
                                        comm=_Gather([(local_w[0], 0), (wc_local, None)]))
    weights = [None] * n_layers
    wc_full = wc_g.reshape(N_DEV, 1024)[:, :n_layers * 3 * 64].reshape(N_DEV, n_layers, 3, 64)
    wc_full = jnp.transpose(wc_full, (1, 2, 0, 3)).reshape(n_layers, 3, CONV_WIDTH)
    wc_full = jnp.pad(wc_full, ((0, 0), (0, 5), (0, 0)))

    g3 = {k: v.reshape(n_layers, 1, -1) for k, v in dict(
        conv=g_conv_out, attn=g_attn_out, pre_mix=g_pre_mix, post_mix=g_post_mix,
        pre_ffn=g_pre_ffn, post_ffn=g_post_ffn).items()}

    saved = []
    xl = x0
    h = _norm_cast(x0, g3["pre_mix"], 0, tm)
    for l in range(n_layers):
        win = win_next
        pc, ync, wout = _in_proj_conv(h, win, wc_full, g3["conv"], l, s, tq, comm=_Gather([(local_w[1], l)]))
        qkvp = _in_proj_qkv(h, win, s, tq)
        o, lse, yna, wfin = _attn_fwd(qkvp, biasm, g3["attn"], l, s, tq, comm=_Gather([(local_w[2], l)]))
        wout = wout.reshape(D_MODEL, D_MODEL)
        z, xm, h2 = _out_proj_fwd(ync, yna, wout, xl, g3["post_mix"], g3["pre_ffn"], l, s, tq)
        gu, act, wfout = _ffn_in_fwd(h2, wfin.reshape(4, FF_PAIR, D_MODEL), s, tf, comm=_Gather([(local_w[3], l)]))
        wfo = wfout.reshape(D_FF, D_MODEL)
        weights[l] = [win, wout, wfin.reshape(2 * D_FF, D_MODEL), wfo]
        sv = dict(x=xl, h=h, pc=pc, qkvp=qkvp, ync=ync, yna=yna, o=o, lse=lse, z=z, xm=xm, h2=h2, gu=gu, act=act)
        if l + 1 < n_layers:
            sv["f"], xl, h, win_next = _ffn_out_fwd(act, wfo, xm, g3["post_ffn"], g3["pre_mix"], l, l + 1, s, tm,
                                                    comm=_Gather([(local_w[0], l + 1)]))
        else:
            dx, sq, df, dg_post_ffn = _ffn_out_loss(act, wfo, xm, g3["post_ffn"], target, l, s, tm)
        saved.append(sv)

    loss = lax.psum(jnp.sum(sq) * (0.5 / D_MODEL), ("x", "y", "c"))

    lands = dict(win=[None] * n_layers, wout=[None] * n_layers, wfin=[None] * n_layers, wfout=[None] * n_layers)
    small = {k: [None] * n_layers for k in ("gco", "gao", "gpm", "gqm", "gpf", "gqf", "wc")}
    d_rel = [None] * n_layers
    started = []

    def start(name, keys, l, arrays):
        items = [(a, False) for a in arrays]
        send_sems, recv_sems, srcs, zones, token = _exchange_start(name + "_start", items)
        started.append((name, keys, l, items, send_sems, recv_sems, srcs, zones))
        return token[0:1, 0:1].reshape(1, 1, 1)

    for l in reversed(range(n_layers)):
        sv = saved[l]
        win, wout, wfin, wfo = weights[l]
        small["gqf"][l] = dg_post_ffn
        dgu, d_wfout = _ffn_out_bwd(df, wfo, sv["gu"], sv["act"], s, tm)
        d_wfout = d_wfout.reshape(N_DEV, FFO_SHARD, D_MODEL)
        d_wfin = _dw_ffn_in(sv["h2"], dgu, s).reshape(N_DEV, FF_SHARD, D_MODEL)
        g_pre_ffn = g3["pre_ffn"]
        if l == 0:
            g_pre_ffn = g_pre_ffn + start("exchange_ffn0", ("wfout", "wfin"), l, [d_wfout, d_wfin])
        dxm, dz, dg_pre_ffn, dg_post_mix = _ffn_in_bwd(
            dgu, wfin, sv["xm"], g_pre_ffn, dx, sv["z"], g3["post_mix"], l, s, tm)
        small["gpf"][l] = dg_pre_ffn
        small["gqm"][l] = dg_post_mix
        dyc, do, dg_attn, d_wout = _out_proj_bwd(dz, wout, sv["o"], g3["attn"], sv["ync"], sv["yna"], l, s, tq)
        d_wout = d_wout.reshape(N_DEV, D_MODEL // N_DEV, D_MODEL)
        small["gao"][l] = dg_attn
        dq, dk, dv, ds_sum = _attn_bwd(sv["qkvp"], biasm, sv["o"], sv["lse"], do, l, s, tq)
        d_rel[l] = _bias_bwd(ds_sum[None])
        dproj, dwc, dg_conv = _conv_bwd(sv["pc"], dyc, wc_full, g3["conv"], dq, dk, dv, l, s, tm)
        small["wc"][l] = dwc
        small["gco"][l] = dg_conv
        d_win = _dw_in(sv["h"], dproj, s)
        if l == 0:
            token = start("exchange_mix0", ("wout", "win"), l, [d_wout, d_win])
        else:
            token = start(f"exchange_layer{l}", ("wfout", "wfin", "wout", "win"), l, [d_wfout, d_wfin, d_wout, d_win])
        if l > 0:
            dx, dg_pre_mix, df, dg_post_ffn = _in_proj_bwd(
                dproj, win, sv["x"], g3["pre_mix"] + token, dxm, l, s, tm, f_prev=saved[l - 1]["f"],
                g_post3=g3["post_ffn"])
        else:
            dx, dg_pre_mix = _in_proj_bwd(dproj, win, sv["x"], g3["pre_mix"] + token, dxm, l, s, tm)
        small["gpm"][l] = dg_pre_mix
    grad_x = dx.reshape(1, s, D_MODEL)

    small_vec = _pack_small_grads(d_rel, small)
    small_items = [(small_vec, True)]
    small_sems = _exchange_start("exchange_small_start", small_items)

    owns = dict(win=[None] * n_layers, wout=[None] * n_layers, wfin=[None] * n_layers, wfout=[None] * n_layers)

    def wait(last, after):
        for name, keys, l, items, send_sems, recv_sems, srcs, zones in started:
            if (name == "exchange_mix0") == last:
                srcs, zones = _exchange_wait(name + "_wait", items, send_sems, recv_sems, srcs, zones, after)
                for key, src, zone in zip(keys, srcs, zones):
                    owns[key][l], lands[key][l] = src, zone

    me = dev.astype(jnp.int32).reshape(1)
    wait(False, small_sems[4])
    r_fin = [jnp.transpose(t, (0, 2, 1)) for t in _adamw(
        "adamw_w_ffn_in", wt_ffn_in, mt_ffn_in, vt_ffn_in, lands["wfin"], owns["wfin"], me)]
    r_fout = _adamw("adamw_w_ffn_out", w_ffn_out, m_w_ffn_out, v_w_ffn_out, lands["wfout"], owns["wfout"], me)
    wait(True, r_fout[0])
    r_out = _adamw("adamw_w_out", w_out, m_w_out, v_w_out, lands["wout"], owns["wout"], me)
    r_in = _adamw("adamw_w_in", w_in, m_w_in, v_w_in, lands["win"], owns["win"], me)
    (small_own,), (land_small,) = _exchange_wait(
        "exchange_small_wait", small_items, small_sems[0], small_sems[1], small_sems[2], small_sems[3], r_in[0])

    n_rep = 64 * n_layers
    rep = _adamw(
        "adamw_replicated",
        _pack_small(rel_bias, g_conv_out, g_attn_out, g_pre_mix, g_post_mix, g_pre_ffn, g_post_ffn)[None],
        _pack_small(m_rel_bias, m_g_conv_out, m_g_attn_out, m_g_pre_mix, m_g_post_mix, m_g_pre_ffn, m_g_post_ffn)[None],
        _pack_small(v_rel_bias, v_g_conv_out, v_g_attn_out, v_g_pre_mix, v_g_post_mix, v_g_pre_ffn, v_g_post_ffn)[None],
        [land_small[:, :n_rep]], [small_own[:n_rep]], me)
    rep = [_unpack_small(t[0], n_layers) for t in rep]

    wc_rows = n_layers * 3 * CONV_WIDTH // 128
    zeros_wc = jnp.zeros((1, wc_rows, 128), F32)
    g_wc_full = _adamw("sum_w_conv", zeros_wc, zeros_wc, zeros_wc, [land_small[:, n_rep:n_rep + wc_rows]],
                       [small_own[n_rep:n_rep + wc_rows]], me)[0]
    g_wc_full = g_wc_full.reshape(n_layers, 3, CONV_WIDTH)
    g_wc = lax.dynamic_slice_in_dim(g_wc_full, dev * (CONV_WIDTH // N_DEV), CONV_WIDTH // N_DEV, axis=2)
    g_wc = jnp.transpose(g_wc, (0, 2, 1))

    def tiny(a):
        flat = a.reshape(-1)
        return jnp.pad(flat, (0, (-flat.shape[0]) % 1024)).reshape(1, -1, 128)

    r_wc = _adamw("adamw_w_conv", tiny(w_conv), tiny(m_w_conv), tiny(v_w_conv), [tiny(g_wc)])
    r_wc = [t.reshape(-1)[:w_conv.size].reshape(w_conv.shape) for t in r_wc]

    def leaf(kind):
        return [r_in[kind], r_wc[kind], rep[kind][0], rep[kind][1], rep[kind][2], r_out[kind],
                rep[kind][3], rep[kind][4], rep[kind][5], rep[kind][6], r_fin[kind], r_fout[kind]]

    return (loss, grad_x, *leaf(0), *leaf(1), *leaf(2), *leaf(3))
```

```python
import math

import jax
import jax.numpy as jnp
from jax import lax
from jax.experimental import pallas as pl
from jax.experimental.pallas import tpu as pltpu

F32 = jnp.float32
BF16 = jnp.bfloat16

D_MODEL = 1024
N_DEV = 8
CHUNK = 64
N_LEFT_CHUNKS = 8
CONV_WIDTH = 512
ATTN_WIDTH = 512
HEAD_DIM = 64
N_HEADS = 8
REL_CLIP = 128
REL_PAD = 384
PROJ_WIDTH = 3072
PROJ_SHARD = PROJ_WIDTH // N_DEV
D_FF = 2816
FF_SHARD = 2 * D_FF // N_DEV
FFO_SHARD = D_FF // N_DEV
FF_PAIR = 2 * FF_SHARD
_COL_SUBTILES = (slice(0, 768), slice(768, FF_PAIR))
EPS = 1e-6
NEG_INF = -1e30
ATTN_PAIRS = 2
Q_STEP = 2
Q_BLOCK = 4 * CHUNK
K_BAND = Q_BLOCK + N_LEFT_CHUNKS * CHUNK
LEFT = N_LEFT_CHUNKS * CHUNK
TOEP = 1024

ADAM_LR = 0.001
ADAM_B1 = 0.9
ADAM_B2 = 0.999
ADAM_EPS = 1e-08
ADAM_WD = 0.01
ADAM_STEP = 10

VMEM_LIMIT = 52 * 1024 * 1024
SUB_ROWS = 256
MESH = pl.DeviceIdType.MESH
ANY = pl.BlockSpec(memory_space=pl.ANY)

NT = (((1,), (1,)), ((), ()))
TN = (((0,), (0,)), ((), ()))


def _dot(a, b):
    return jnp.dot(a, b, preferred_element_type=F32)


def _dot_nt(a, b):
    return lax.dot_general(a, b, NT, preferred_element_type=F32)


def _dot_tn(a, b):
    return lax.dot_general(a, b, TN, preferred_element_type=F32)


def _rstd(v):
    return lax.rsqrt(jnp.mean(v * v, axis=-1, keepdims=True) + EPS)


def _group_matrix():
    r = lax.broadcasted_iota(jnp.int32, (128, 128), 0) >> 6
    c = lax.broadcasted_iota(jnp.int32, (128, 128), 1) >> 6
    return jnp.where(r == c, 1.0, 0.0).astype(BF16)


def _group_mean(v, gmat):
    hi = v.astype(BF16)
    lo = (v - hi.astype(F32)).astype(BF16)
    return (_dot(hi, gmat) + _dot(lo, gmat)) * (1.0 / HEAD_DIM)


def _split3(v):
    hi = v.astype(BF16)
    r1 = v - hi.astype(F32)
    mid = r1.astype(BF16)
    lo = (r1 - mid.astype(F32)).astype(BF16)
    return hi, mid, lo


def _row_tile(rows, cands=(1024, 512, 704, 256, 128, 64, 32, 16)):
    for c in cands:
        if rows % c == 0:
            return c
    return rows


def _dev_index(px, py, pc):
    return 4 * px + 2 * py + pc


def _when(cond):
    if cond is True:
        return lambda fn: fn()
    return pl.when(cond)


def _phases(grid):
    def phases():
        if not grid:
            return True, True, True
        lin = pl.program_id(0)
        for a in range(1, len(grid)):
            lin = lin * grid[a] + pl.program_id(a)
        total = math.prod(grid)
        return lin == 0, lin == total - 1, lin == total - 1
    return phases


class _Gather:
    def __init__(self, items):
        self.items = items
        self.args = [a for a, _ in items]
        n = len(items)
        self.out_shape = [jax.ShapeDtypeStruct((N_DEV,) + (a.shape if lay is None else a.shape[1:]), a.dtype)
                          for a, lay in items]
        self.scratch = [pltpu.SemaphoreType.DMA((n, 7)), pltpu.SemaphoreType.DMA((n, 7)),
                        pltpu.SemaphoreType.DMA((n,))]

    def _ctx(self, ins, outs, sems):
        send_sems, recv_sems, local_sems = sems
        x, y, c = lax.axis_index("x"), lax.axis_index("y"), lax.axis_index("c")
        chips = [(1 - x, y), (x, 1 - y), (1 - x, 1 - y)]

        def src(k):
            lay = self.items[k][1]
            return ins[k] if lay is None else ins[k].at[lay]

        def copy(k, s, idx, to, from_src=False):
            return pltpu.make_async_remote_copy(
                src_ref=src(k) if from_src else outs[k].at[idx], dst_ref=outs[k].at[idx],
                send_sem=send_sems.at[k, s], recv_sem=recv_sems.at[k, s],
                device_id=to, device_id_type=MESH)

        def local(k):
            return pltpu.make_async_copy(src(k), outs[k].at[_dev_index(x, y, c)], local_sems.at[k])

        return x, y, c, chips, copy, local

    def start(self, ins, outs, sems, cond):
        n = len(self.items)

        @_when(cond)
        def _():
            x, y, c, chips, copy, local = self._ctx(ins, outs, sems)
            me = _dev_index(x, y, c)
            for k in range(n):
                local(k).start()
                copy(k, 0, me, (x, y, 1 - c), from_src=True).start()
                for j, chip in enumerate(chips):
                    copy(k, 1 + j, me, (chip[0], chip[1], c), from_src=True).start()

    def forward(self, ins, outs, sems, cond):
        n = len(self.items)

        @_when(cond)
        def _():
            x, y, c, chips, copy, local = self._ctx(ins, outs, sems)
            for j, chip in enumerate(chips):
                idx = _dev_index(chip[0], chip[1], c)
                for k in range(n):
                    copy(k, 1 + j, idx, (x, y, c)).wait_recv()
                    copy(k, 4 + j, idx, (x, y, 1 - c)).start()

    def finish(self, ins, outs, sems, cond):
        n = len(self.items)

        @_when(cond)
        def _():
            x, y, c, chips, copy, local = self._ctx(ins, outs, sems)
            me = _dev_index(x, y, c)
            for k in range(n):
                copy(k, 0, _dev_index(x, y, 1 - c), (x, y, c)).wait_recv()
            for j, chip in enumerate(chips):
                idx = _dev_index(chip[0], chip[1], 1 - c)
                for k in range(n):
                    copy(k, 4 + j, idx, (x, y, c)).wait_recv()
            for k in range(n):
                for s in range(4):
                    copy(k, s, me, (x, y, c), from_src=True).wait_send()
                for j, chip in enumerate(chips):
                    copy(k, 4 + j, _dev_index(chip[0], chip[1], c), (x, y, c)).wait_send()
                local(k).wait()


_PEER_FLIPS = [(0, 0, 1), (1, 0, 0), (0, 1, 0), (1, 1, 0), (1, 0, 1), (0, 1, 1), (1, 1, 1)]


def _call(body, *, name, grid, in_specs, out_specs, out_shape, args, scratch=(), comm=None):
    n_hi, n_ho, n_hs = len(args), len(out_shape), len(scratch)
    c_args = list(comm.args) if comm else []
    c_out = list(comm.out_shape) if comm else []
    c_scr = list(comm.scratch) if comm else []
    phases = _phases(grid)

    def kern(*refs):
        cuts = [n_hi, len(c_args), n_ho, len(c_out), n_hs, len(c_scr)]
        parts, pos = [], 0
        for n in cuts:
            parts.append(refs[pos:pos + n])
            pos += n
        hi, ci, ho, co, hs, cs = parts
        if comm:
            first, mid, last = phases()
            comm.start(ci, co, cs, first)
            comm.forward(ci, co, cs, mid)
        body(*hi, *ho, *hs)
        if comm:
            comm.finish(ci, co, cs, last)

    sem = ("arbitrary",) * len(grid) if grid else None
    return pl.pallas_call(
        kern, name=name, grid=grid,
        in_specs=list(in_specs) + [ANY] * len(c_args),
        out_specs=list(out_specs) + [ANY] * len(c_out),
        out_shape=list(out_shape) + c_out,
        scratch_shapes=list(scratch) + c_scr,
        compiler_params=pltpu.CompilerParams(dimension_semantics=sem, vmem_limit_bytes=VMEM_LIMIT),
    )(*args, *c_args)


def _comm_only(name, comm):
    return _call(lambda: None, name=name, grid=(), in_specs=[], out_specs=[], out_shape=[], args=[], comm=comm)


HBM_SPEC = pl.BlockSpec(memory_space=pltpu.HBM)
SEM_SPEC = pl.BlockSpec(memory_space=pltpu.SEMAPHORE)
SIDE_EFFECT = pltpu.SideEffectType.DATAFLOW_SIDE_EFFECTING


def _exchange_peer(x, y, c, s):
    fx, fy, fc = _PEER_FLIPS[s]
    return x ^ fx, y ^ fy, c ^ fc


def _exchange_start(name, items):
    n = len(items)
    srcs = [pltpu.with_memory_space_constraint(a, pltpu.HBM) for a, _ in items]
    land_shapes = [(N_DEV,) + (a.shape if whole else a.shape[1:]) for a, whole in items]
    lands = [pltpu.with_memory_space_constraint(lax.empty(shp, a.dtype), pltpu.HBM)
             for shp, (a, _) in zip(land_shapes, items)]

    n_sem = 7 * n

    def body(*refs):
        src_refs, land_refs = refs[:n], refs[n:2 * n]
        send_sems = refs[2 * n:2 * n + n_sem]
        recv_sems = refs[2 * n + n_sem:2 * n + 2 * n_sem]
        token = refs[-1]
        x, y, c = lax.axis_index("x"), lax.axis_index("y"), lax.axis_index("c")
        me = _dev_index(x, y, c)
        for s in range(7):
            px, py, pc = _exchange_peer(x, y, c, s)
            for k in range(n):
                src = src_refs[k] if items[k][1] else src_refs[k].at[_dev_index(px, py, pc)]
                pltpu.make_async_remote_copy(
                    src_ref=src, dst_ref=land_refs[k].at[me],
                    send_sem=send_sems[7 * k + s], recv_sem=recv_sems[7 * k + s],
                    device_id=(px, py, pc), device_id_type=MESH).start()
        token[...] = jnp.zeros(token.shape, token.dtype)

    outs = pl.pallas_call(
        body, name=name,
        out_shape=(*[pltpu.SemaphoreType.DMA(())] * (2 * n_sem),
                   *[pltpu.HBM(a.shape, a.dtype) for a in srcs],
                   *[pltpu.HBM(shp, a.dtype) for shp, a in zip(land_shapes, srcs)],
                   jax.ShapeDtypeStruct((8, 128), F32)),
        in_specs=[HBM_SPEC] * (2 * n),
        out_specs=(*[SEM_SPEC] * (2 * n_sem), *[HBM_SPEC] * (2 * n), pl.BlockSpec(memory_space=pltpu.VMEM)),
        input_output_aliases={i: 2 * n_sem + i for i in range(2 * n)},
        compiler_params=pltpu.CompilerParams(has_side_effects=SIDE_EFFECT),
    )(*srcs, *lands)
    base = 2 * n_sem
    return (list(outs[:n_sem]), list(outs[n_sem:base]), list(outs[base:base + n]),
            list(outs[base + n:base + 2 * n]), outs[-1])


def _exchange_wait(name, items, send_sems, recv_sems, srcs, lands, after):
    n = len(items)

    n_sem = 7 * n

    def body(*refs):
        src_refs, land_refs = refs[:n], refs[n:2 * n]
        send_refs = refs[2 * n:2 * n + n_sem]
        recv_refs = refs[2 * n + n_sem:2 * n + 2 * n_sem]
        x, y, c = lax.axis_index("x"), lax.axis_index("y"), lax.axis_index("c")
        for s in range(7):
            for k in range(n):
                copy = pltpu.make_async_remote_copy(
                    src_ref=src_refs[k] if items[k][1] else src_refs[k].at[0], dst_ref=land_refs[k].at[0],
                    send_sem=send_refs[7 * k + s], recv_sem=recv_refs[7 * k + s],
                    device_id=(x, y, c), device_id_type=MESH)
                copy.wait_send()
                copy.wait_recv()

    outs = pl.pallas_call(
        body, name=name,
        out_shape=(*[pltpu.HBM(a.shape, a.dtype) for a in srcs], *[pltpu.HBM(a.shape, a.dtype) for a in lands]),
        in_specs=[HBM_SPEC] * (2 * n) + [SEM_SPEC] * (2 * n_sem) + [ANY],
        out_specs=tuple([HBM_SPEC] * (2 * n)),
        input_output_aliases={i: i for i in range(2 * n)},
        compiler_params=pltpu.CompilerParams(has_side_effects=SIDE_EFFECT),
    )(*srcs, *lands, *send_sems, *recv_sems, after)
    return list(outs[:n]), list(outs[n:])


def _cast_bf16(x, name):
    shape = x.shape
    x2 = x.reshape(-1, shape[-1])
    rows, cols = x2.shape
    tr = _row_tile(rows)

    def body(x_ref, o_ref):
        o_ref[...] = x_ref[...].astype(BF16)

    blk = pl.BlockSpec((tr, cols), lambda i: (i, 0))
    out, = _call(body, name=name, grid=(rows // tr,), in_specs=[blk], out_specs=[blk],
                 out_shape=[jax.ShapeDtypeStruct((rows, cols), BF16)], args=[x2])
    return out.reshape(shape)


def _norm_cast(x, g3, l, tm):
    s = x.shape[0]

    def body(x_ref, g_ref, o_ref):
        v = x_ref[...]
        o_ref[...] = (v * _rstd(v) * g_ref[...]).astype(BF16)

    row = pl.BlockSpec((tm, D_MODEL), lambda i: (i, 0))
    out, = _call(body, name="norm_cast", grid=(s // tm,),
                 in_specs=[row, pl.BlockSpec((None, 1, D_MODEL), lambda i: (l, 0, 0))], out_specs=[row],
                 out_shape=[jax.ShapeDtypeStruct((s, D_MODEL), BF16)], args=[x, g3])
    return out


def _in_proj_qkv(h, win, s, tq):
    def body(a_ref, b_ref, o_ref):
        i = pl.program_id(0)

        @pl.when(i == 0)
        def _():
            o_ref[...] = jnp.zeros(o_ref.shape, BF16)

        @pl.when(i > 0)
        def _():
            w = jnp.concatenate([b_ref[j] for j in range(4)], axis=1)
            o_ref[...] = _dot(a_ref[...], w).astype(BF16)

    out, = _call(
        body, name="in_proj_qkv", grid=(s // tq + 1,),
        in_specs=[pl.BlockSpec((tq, D_MODEL), lambda i: (jnp.maximum(i - 1, 0), 0)),
                  pl.BlockSpec((4, D_MODEL, PROJ_SHARD), lambda i: (1, 0, 0))],
        out_specs=[pl.BlockSpec((tq, 4 * PROJ_SHARD), lambda i: (i, 0))],
        out_shape=[jax.ShapeDtypeStruct((s + tq, PROJ_WIDTH // 2), BF16)], args=[h, win])
    return out


def _in_proj_conv(h, win, wc, g3, l, s, tq, comm=None):
    def body(a_ref, b_ref, wc_ref, g_ref, pc_ref, y_ref, carry_ref, acc_ref):
        i = pl.program_id(0)
        w = jnp.concatenate([b_ref[j] for j in range(4)], axis=1)
        acc_ref[...] = _dot(a_ref[...], w)
        pc_ref[...] = acc_ref[...].astype(BF16)
        gmat = _group_matrix()
        for j in range(CONV_WIDTH // 128):
            c0, c1, c2 = 128 * j, CONV_WIDTH + 128 * j, 2 * CONV_WIDTH + 128 * j
            hc = acc_ref[:, c0:c0 + 128]
            bg = acc_ref[:, c1:c1 + 128]
            cg = acc_ref[:, c2:c2 + 128]
            u_prev = jnp.where(i > 0, carry_ref[:, c0:c0 + 128], 0.0)
            u = cg * hc
            carry_ref[:, c0:c0 + 128] = u[tq - 8:tq, :]
            full = jnp.concatenate([u_prev, u], axis=0)
            u1 = pltpu.roll(full, 1, 0)[8:]
            u2 = pltpu.roll(full, 2, 0)[8:]
            out = (u2 * wc_ref[0:1, c0:c0 + 128] + u1 * wc_ref[1:2, c0:c0 + 128]
                   + u * wc_ref[2:3, c0:c0 + 128])
            yc = bg * out
            r = lax.rsqrt(_group_mean(yc * yc, gmat) + EPS)
            y_ref[:, c0:c0 + 128] = (yc * r * g_ref[:, c0:c0 + 128]).astype(BF16)

    return _call(
        body, name="in_proj_conv", grid=(s // tq,),
        in_specs=[pl.BlockSpec((tq, D_MODEL), lambda i: (i, 0)),
                  pl.BlockSpec((4, D_MODEL, PROJ_SHARD), lambda i: (0, 0, 0)),
                  pl.BlockSpec((None, 8, CONV_WIDTH), lambda i: (l, 0, 0)),
                  pl.BlockSpec((None, 1, CONV_WIDTH), lambda i: (l, 0, 0))],
        out_specs=[pl.BlockSpec((tq, 3 * CONV_WIDTH), lambda i: (i, 0)),
                   pl.BlockSpec((tq, CONV_WIDTH), lambda i: (i, 0))],
        out_shape=[jax.ShapeDtypeStruct((s, 3 * CONV_WIDTH), BF16), jax.ShapeDtypeStruct((s, CONV_WIDTH), BF16)],
        scratch=[pltpu.VMEM((8, CONV_WIDTH), F32), pltpu.VMEM((tq, 3 * CONV_WIDTH), F32)],
        args=[h, win, wc, g3], comm=comm)


def _toeplitz_source():
    r_i = lax.broadcasted_iota(jnp.int32, (REL_PAD, TOEP), 0)
    m_i = lax.broadcasted_iota(jnp.int32, (REL_PAD, TOEP), 1)
    idx = jnp.clip((K_BAND - 1) - m_i, -REL_CLIP, REL_CLIP) + REL_CLIP
    return jnp.where(r_i == idx, 1.0, 0.0).astype(BF16)


def _bias_build(rbp, comm=None):
    n_layers = rbp.shape[0]

    def body(rb_ref, o_ref, t_ref):
        pmat = _toeplitz_source()
        hi, mid, lo = _split3(rb_ref[...])
        t_ref[...] = _dot(hi, pmat) + _dot(mid, pmat) + _dot(lo, pmat)
        shift = (CHUNK - 1) - lax.broadcasted_iota(jnp.int32, (CHUNK, TOEP), 0)
        kchunk = lax.broadcasted_iota(jnp.int32, (CHUNK, K_BAND), 1) >> 6
        for h in range(N_HEADS):
            b = jnp.broadcast_to(t_ref[pl.ds(h, 1), :], (CHUNK, TOEP))
            for bit in range(6):
                rolled = pltpu.roll(b, TOEP - (1 << bit), 1)
                b = jnp.where(((shift >> bit) & 1) == 1, rolled, b)
            for cq in range(Q_BLOCK // CHUNK):
                off = CHUNK * (Q_BLOCK // CHUNK - 1 - cq)
                band = pltpu.roll(b, TOEP - off, 1) if off else b
                dchunk = kchunk - cq
                in_band = jnp.where(dchunk >= 0, jnp.where(dchunk <= N_LEFT_CHUNKS, 1, 0), 0) == 1
                o_ref[h, CHUNK * cq:CHUNK * (cq + 1), :] = jnp.where(in_band, band[:, :K_BAND], NEG_INF)

    return _call(
        body, name="bias_build", grid=(n_layers,),
        in_specs=[pl.BlockSpec((None, N_HEADS, REL_PAD), lambda l: (l, 0, 0))],
        out_specs=[pl.BlockSpec((None, N_HEADS, Q_BLOCK, K_BAND), lambda l: (l, 0, 0, 0))],
        out_shape=[jax.ShapeDtypeStruct((n_layers, N_HEADS, Q_BLOCK, K_BAND), F32)],
        scratch=[pltpu.VMEM((N_HEADS, TOEP), F32)], args=[rbp], comm=comm)


def _bias_bwd(ds_sum):
    n_layers = ds_sum.shape[0]

    def body(ds_ref, o_ref, t_ref):
        pmat = _toeplitz_source()
        shift = (CHUNK - 1) - lax.broadcasted_iota(jnp.int32, (CHUNK, TOEP), 0)
        for h in range(N_HEADS):
            d = None
            for cq in range(Q_BLOCK // CHUNK):
                off = CHUNK * (Q_BLOCK // CHUNK - 1 - cq)
                part = jnp.concatenate([ds_ref[h, CHUNK * cq:CHUNK * (cq + 1), :],
                                        jnp.zeros((CHUNK, TOEP - K_BAND), F32)], axis=1)
                part = pltpu.roll(part, off, 1) if off else part
                d = part if d is None else d + part
            for bit in range(6):
                rolled = pltpu.roll(d, 1 << bit, 1)
                d = jnp.where(((shift >> bit) & 1) == 1, rolled, d)
            t_ref[pl.ds(h, 1), :] = jnp.sum(d, axis=0, keepdims=True)
        hi, mid, lo = _split3(t_ref[...])
        o_ref[...] = _dot_nt(hi, pmat) + _dot_nt(mid, pmat) + _dot_nt(lo, pmat)

    out, = _call(
        body, name="bias_bwd", grid=(n_layers,),
        in_specs=[pl.BlockSpec((None, N_HEADS, Q_BLOCK, K_BAND), lambda l: (l, 0, 0, 0))],
        out_specs=[pl.BlockSpec((None, N_HEADS, REL_PAD), lambda l: (l, 0, 0))],
        out_shape=[jax.ShapeDtypeStruct((n_layers, N_HEADS, REL_PAD), F32)],
        scratch=[pltpu.VMEM((N_HEADS, TOEP), F32)], args=[ds_sum])
    return out


def _attn_fwd(qkvp, biasm, g3, l, s, pad, comm=None):
    rows = Q_STEP * Q_BLOCK
    nb = s // rows
    qb0 = pad // rows
    scale = HEAD_DIM ** -0.5
    wide = 128 * ATTN_PAIRS

    def body(q_ref, k_ref, v_ref, b_ref, g_ref, o_ref, lse_ref, yn_ref):
        blk = pl.program_id(1)
        lane = lax.broadcasted_iota(jnp.int32, (1, 128), 1)
        gmat = _group_matrix()

        def step(masked):
            for sub in range(Q_STEP):
                rs = slice(Q_BLOCK * sub, Q_BLOCK * (sub + 1))
                start = blk * rows + Q_BLOCK * sub
                koff = pl.multiple_of(start + (pad - LEFT), Q_BLOCK)
                kpos = lax.broadcasted_iota(jnp.int32, (1, K_BAND), 1) + (start - LEFT)
                kmask = jnp.where(kpos >= 0, 0.0, NEG_INF)
                for pr in range(ATTN_PAIRS):
                    ls = slice(128 * pr, 128 * (pr + 1))
                    q = q_ref[rs, ls]
                    kb = k_ref[pl.ds(koff, K_BAND), ls]
                    vb = v_ref[pl.ds(koff, K_BAND), ls]
                    outs, lses = [], []
                    for hh in range(2):
                        in_head = (lane >> 6) == hh
                        qm = jnp.where(in_head, q, jnp.zeros_like(q)) * jnp.asarray(scale, BF16)
                        sc = _dot_nt(qm, kb) + b_ref[2 * pr + hh]
                        if masked:
                            sc = sc + kmask
                        m = jnp.max(sc, axis=1, keepdims=True)
                        e = jnp.exp(sc - m)
                        den = jnp.sum(e, axis=1, keepdims=True)
                        outs.append(_dot(e.astype(BF16), vb) * (1.0 / den))
                        lses.append(m + jnp.log(den))
                    first = lane < HEAD_DIM
                    o = jnp.where(first, outs[0], outs[1])
                    o_ref[rs, ls] = o
                    lse_ref[rs, ls] = jnp.where(first, lses[0], lses[1])
                    r = lax.rsqrt(_group_mean(o * o, gmat) + EPS)
                    yn_ref[rs, ls] = (o * r * g_ref[:, ls]).astype(BF16)

        pl.when(blk == 0)(lambda: step(True))
        pl.when(blk > 0)(lambda: step(False))

    blk_out = pl.BlockSpec((rows, wide), lambda p, b: (b, p))
    n_grp = ATTN_WIDTH // wide
    return _call(
        body, name="attn_fwd", grid=(n_grp, nb),
        in_specs=[pl.BlockSpec((rows, wide), lambda p, b: (qb0 + b, p)),
                  pl.BlockSpec((s + pad, wide), lambda p, b: (0, n_grp + p)),
                  pl.BlockSpec((s + pad, wide), lambda p, b: (0, 2 * n_grp + p)),
                  pl.BlockSpec((None, 2 * ATTN_PAIRS, Q_BLOCK, K_BAND), lambda p, b: (l, p, 0, 0)),
                  pl.BlockSpec((None, 1, wide), lambda p, b: (l, 0, p))],
        out_specs=[blk_out, blk_out, blk_out],
        out_shape=[jax.ShapeDtypeStruct((s, ATTN_WIDTH), F32),
                   jax.ShapeDtypeStruct((s, ATTN_WIDTH), F32),
                   jax.ShapeDtypeStruct((s, ATTN_WIDTH), BF16)],
        args=[qkvp, qkvp, qkvp, biasm, g3], comm=comm)


def _out_proj_fwd(ync, yna, wout, x, g_post3, g_next3, l, s, tm):
    half = D_MODEL // 2

    def body(a1_ref, a2_ref, w_ref, x_ref, gp_ref, gn_ref, z_ref, xm_ref, h_ref):
        for rs in _row_subtiles(tm, SUB_ROWS):
            z = _dot(a1_ref[rs, :], w_ref[0:half, :]) + _dot(a2_ref[rs, :], w_ref[half:D_MODEL, :])
            z_ref[rs, :] = z.astype(BF16)
            xm = x_ref[rs, :] + z * _rstd(z) * gp_ref[...]
            xm_ref[rs, :] = xm
            h_ref[rs, :] = (xm * _rstd(xm) * gn_ref[...]).astype(BF16)

    row = pl.BlockSpec((tm, D_MODEL), lambda i: (i, 0))
    gain = pl.BlockSpec((None, 1, D_MODEL), lambda i: (l, 0, 0))
    return _call(
        body, name="out_proj_fwd", grid=(s // tm,),
        in_specs=[pl.BlockSpec((tm, half), lambda i: (i, 0)), pl.BlockSpec((tm, half), lambda i: (i, 0)),
                  pl.BlockSpec((D_MODEL, D_MODEL), lambda i: (0, 0)), row, gain, gain],
        out_specs=[row, row, row],
        out_shape=[jax.ShapeDtypeStruct((s, D_MODEL), BF16), jax.ShapeDtypeStruct((s, D_MODEL), F32),
                   jax.ShapeDtypeStruct((s, D_MODEL), BF16)],
        args=[ync, yna, wout, x, g_post3, g_next3])


def _ffn_in_fwd(h2, wfin4, s, tm, comm=None):
    def body(h_ref, wg_ref, wu_ref, gu_ref, act_ref):
        h = h_ref[...]
        for cs in _COL_SUBTILES:
            gate = _dot_nt(h, wg_ref[cs, :])
            up = _dot_nt(h, wu_ref[cs, :])
            gu_ref[0, :, cs] = gate.astype(BF16)
            gu_ref[1, :, cs] = up.astype(BF16)
            act_ref[:, cs] = (gate * (1.0 / (1.0 + jnp.exp(-gate))) * up).astype(BF16)

    return _call(
        body, name="ffn_in_fwd", grid=(2, s // tm),
        in_specs=[pl.BlockSpec((tm, D_MODEL), lambda b, i: (i, 0)),
                  pl.BlockSpec((None, FF_PAIR, D_MODEL), lambda b, i: (b, 0, 0)),
                  pl.BlockSpec((None, FF_PAIR, D_MODEL), lambda b, i: (2 + b, 0, 0))],
        out_specs=[pl.BlockSpec((2, tm, FF_PAIR), lambda b, i: (0, i, b)),
                   pl.BlockSpec((tm, FF_PAIR), lambda b, i: (i, b))],
        out_shape=[jax.ShapeDtypeStruct((2, s, D_FF), BF16), jax.ShapeDtypeStruct((s, D_FF), BF16)],
        args=[h2, wfin4, wfin4], comm=comm)


def _ffn_out_fwd(act, wfo, xm, g_post3, g_next3, l, l_next, s, tm, comm=None):
    def body(a_ref, w_ref, x_ref, gp_ref, gn_ref, f_ref, xo_ref, h_ref):
        for rs in _row_subtiles(tm, SUB_ROWS):
            f = _dot(a_ref[rs, :], w_ref[...])
            f_ref[rs, :] = f.astype(BF16)
            xo = x_ref[rs, :] + f * _rstd(f) * gp_ref[...]
            xo_ref[rs, :] = xo
            h_ref[rs, :] = (xo * _rstd(xo) * gn_ref[...]).astype(BF16)

    row = pl.BlockSpec((tm, D_MODEL), lambda i: (i, 0))
    return _call(
        body, name="ffn_out_fwd", grid=(s // tm,),
        in_specs=[pl.BlockSpec((tm, D_FF), lambda i: (i, 0)),
                  pl.BlockSpec((D_FF, D_MODEL), lambda i: (0, 0), pipeline_mode=pl.Buffered(1)), row,
                  pl.BlockSpec((None, 1, D_MODEL), lambda i: (l, 0, 0)),
                  pl.BlockSpec((None, 1, D_MODEL), lambda i: (l_next, 0, 0))],
        out_specs=[row, row, row],
        out_shape=[jax.ShapeDtypeStruct((s, D_MODEL), BF16), jax.ShapeDtypeStruct((s, D_MODEL), F32),
                   jax.ShapeDtypeStruct((s, D_MODEL), BF16)],
        args=[act, wfo, xm, g_post3, g_next3], comm=comm)


def _ffn_out_loss(act, wfo, xm, g_post3, target, l, s, tm):
    def body(a_ref, w_ref, x_ref, gp_ref, t_ref, dx_ref, sq_ref, df_ref, dg_ref):
        _zero_first((sq_ref, dg_ref), pl.program_id(0) == 0)
        for rs in _row_subtiles(tm, SUB_ROWS):
            f = _dot(a_ref[rs, :], w_ref[...])
            gain = gp_ref[...]
            err = x_ref[rs, :] + f * _rstd(f) * gain - t_ref[rs, :]
            dx = err * (1.0 / D_MODEL)
            dx_ref[rs, :] = dx
            df, dyn = _norm_bwd_rows(f, gain, dx)
            df_ref[rs, :] = df.astype(BF16)
            _add_cols(dg_ref, dyn)
            cs = jnp.sum(err * err, axis=0, keepdims=True)
            part = cs[:, 0:128]
            for k in range(1, D_MODEL // 128):
                part = part + cs[:, 128 * k:128 * (k + 1)]
            sq_ref[0:1, :] += part

    row = pl.BlockSpec((tm, D_MODEL), lambda i: (i, 0))
    return _call(
        body, name="ffn_out_loss", grid=(s // tm,),
        in_specs=[pl.BlockSpec((tm, D_FF), lambda i: (i, 0)),
                  pl.BlockSpec((D_FF, D_MODEL), lambda i: (0, 0), pipeline_mode=pl.Buffered(1)), row,
                  pl.BlockSpec((None, 1, D_MODEL), lambda i: (l, 0, 0)), row],
        out_specs=[row, pl.BlockSpec((8, 128), lambda i: (0, 0)), row, pl.BlockSpec((8, D_MODEL), lambda i: (0, 0))],
        out_shape=[jax.ShapeDtypeStruct((s, D_MODEL), F32), jax.ShapeDtypeStruct((8, 128), F32),
                   jax.ShapeDtypeStruct((s, D_MODEL), BF16), jax.ShapeDtypeStruct((8, D_MODEL), F32)],
        args=[act, wfo, xm, g_post3, target])


def _norm_bwd_rows(v, g, dy):
    r = _rstd(v)
    vn = v * r
    gd = dy * g
    dv = r * (gd - vn * jnp.mean(vn * gd, axis=-1, keepdims=True))
    return dv, dy * vn


def _zero_first(refs, first):
    @pl.when(first)
    def _():
        for ref in refs:
            ref[...] = jnp.zeros(ref.shape, F32)


def _add_cols(ref, val):
    ref[0:1, :] += jnp.sum(val, axis=0, keepdims=True)


def _accum_cols(ref, val, first):
    _zero_first((ref,), first)
    _add_cols(ref, val)


def _row_subtiles(rows, sub):
    sub = min(sub, rows)
    return [slice(r, r + sub) for r in range(0, rows, sub)]


def _ffn_out_bwd(df, wfo, gu, act, s, tm, comm=None):
    nm = s // tm

    def body(df_ref, w_ref, gu_ref, act_ref, dgu_ref, dw_ref, acc_ref):
        i = pl.program_id(1)
        df = df_ref[...]
        _zero_first((acc_ref,), i == 0)
        acc_ref[...] += _dot_tn(act_ref[...], df)

        @pl.when(i == nm - 1)
        def _():
            dw_ref[...] = acc_ref[...].astype(BF16)

        for cs in _COL_SUBTILES:
            da = _dot_nt(df, w_ref[cs, :])
            g = gu_ref[0, :, cs].astype(F32)
            u = gu_ref[1, :, cs].astype(F32)
            sg = 1.0 / (1.0 + jnp.exp(-g))
            dgu_ref[0, :, cs] = (da * u * (sg * (1.0 + g * (1.0 - sg)))).astype(BF16)
            dgu_ref[1, :, cs] = (da * (g * sg)).astype(BF16)

    blk = pl.BlockSpec((2, tm, FF_PAIR), lambda b, i: (0, i, b))
    wblk = pl.BlockSpec((FF_PAIR, D_MODEL), lambda b, i: (b, 0))
    return _call(
        body, name="ffn_out_bwd", grid=(2, nm),
        in_specs=[pl.BlockSpec((tm, D_MODEL), lambda b, i: (i, 0)),
                  pl.BlockSpec((FF_PAIR, D_MODEL), lambda b, i: (b, 0), pipeline_mode=pl.Buffered(1)), blk,
                  pl.BlockSpec((tm, FF_PAIR), lambda b, i: (i, b))],
        out_specs=[blk, wblk],
        out_shape=[jax.ShapeDtypeStruct((2, s, D_FF), BF16), jax.ShapeDtypeStruct((D_FF, D_MODEL), BF16)],
        scratch=[pltpu.VMEM((FF_PAIR, D_MODEL), F32)],
        args=[df, wfo, gu, act], comm=comm)


def _dw_ffn_in(h2, dgu, s):
    def body(a_ref, b_ref, o_ref):
        o_ref[...] = _dot_tn(b_ref[...], a_ref[...]).astype(BF16)

    out, = _call(
        body, name="dw_ffn_in", grid=(4,),
        in_specs=[pl.BlockSpec((s, D_MODEL), lambda n: (0, 0), pipeline_mode=pl.Buffered(1)),
                  pl.BlockSpec((None, s, FF_PAIR), lambda n: (n // 2, 0, n % 2))],
        out_specs=[pl.BlockSpec((None, FF_PAIR, D_MODEL), lambda n: (n, 0, 0))],
        out_shape=[jax.ShapeDtypeStruct((4, FF_PAIR, D_MODEL), BF16)], args=[h2, dgu])
    return out


def _ffn_in_bwd(dgu, wfin, xm, g_pre3, dres, z, g_post3, l, s, tm, comm=None):
    def body(d_ref, w_ref, xm_ref, gp_ref, dres_ref, z_ref, gq_ref, dxm_ref, dz_ref, dgp_ref, dgq_ref):
        _zero_first((dgp_ref, dgq_ref), pl.program_id(0) == 0)
        for rs in _row_subtiles(tm, SUB_ROWS):
            dh = _dot(d_ref[0, rs, :], w_ref[0:D_FF, :]) + _dot(d_ref[1, rs, :], w_ref[D_FF:2 * D_FF, :])
            dx, dyn = _norm_bwd_rows(xm_ref[rs, :], gp_ref[...], dh)
            dxm = dres_ref[rs, :] + dx
            dxm_ref[rs, :] = dxm
            _add_cols(dgp_ref, dyn)
            dz, dyn2 = _norm_bwd_rows(z_ref[rs, :].astype(F32), gq_ref[...], dxm)
            dz_ref[rs, :] = dz.astype(BF16)
            _add_cols(dgq_ref, dyn2)

    row = pl.BlockSpec((tm, D_MODEL), lambda i: (i, 0))
    gain = pl.BlockSpec((None, 1, D_MODEL), lambda i: (l, 0, 0))
    dgs = pl.BlockSpec((8, D_MODEL), lambda i: (0, 0))
    return _call(
        body, name="ffn_in_bwd", grid=(s // tm,),
        in_specs=[pl.BlockSpec((2, tm, D_FF), lambda i: (0, i, 0)),
                  pl.BlockSpec((2 * D_FF, D_MODEL), lambda i: (0, 0), pipeline_mode=pl.Buffered(1)),
                  row, gain, row, row, gain],
        out_specs=[row, row, dgs, dgs],
        out_shape=[jax.ShapeDtypeStruct((s, D_MODEL), F32), jax.ShapeDtypeStruct((s, D_MODEL), BF16),
                   jax.ShapeDtypeStruct((8, D_MODEL), F32), jax.ShapeDtypeStruct((8, D_MODEL), F32)],
        args=[dgu, wfin, xm, g_pre3, dres, z, g_post3], comm=comm)


def _out_proj_bwd(dz, wout, o, g3, ync, yna, l, s, tm):
    nm = s // tm
    half = D_MODEL // 2

    def body(dz_ref, w_ref, o_ref, g_ref, a1_ref, a2_ref, dyc_ref, do_ref, dg_ref, dw_ref, acc_ref):
        i = pl.program_id(0)
        gmat = _group_matrix()
        _zero_first((dg_ref, acc_ref), i == 0)
        dzv = dz_ref[...]
        acc_ref[0:half, :] += _dot_tn(a1_ref[...], dzv)
        acc_ref[half:D_MODEL, :] += _dot_tn(a2_ref[...], dzv)

        @pl.when(i == nm - 1)
        def _():
            dw_ref[...] = acc_ref[...].astype(BF16)

        for rs in _row_subtiles(tm, SUB_ROWS):
            dy = _dot_nt(dz_ref[rs, :], w_ref[...])
            dyc_ref[rs, :] = dy[:, 0:CONV_WIDTH]
            for j in range(ATTN_WIDTH // 128):
                c0 = 128 * j
                ov = o_ref[rs, c0:c0 + 128]
                dyn = dy[:, CONV_WIDTH + c0:CONV_WIDTH + c0 + 128]
                r = lax.rsqrt(_group_mean(ov * ov, gmat) + EPS)
                on = ov * r
                gd = dyn * g_ref[:, c0:c0 + 128]
                do_ref[rs, c0:c0 + 128] = r * (gd - on * _group_mean(on * gd, gmat))
                dg_ref[0:1, c0:c0 + 128] += jnp.sum(dyn * on, axis=0, keepdims=True)

    halfrow = pl.BlockSpec((tm, ATTN_WIDTH), lambda i: (i, 0))
    return _call(
        body, name="out_proj_bwd", grid=(nm,),
        in_specs=[pl.BlockSpec((tm, D_MODEL), lambda i: (i, 0)),
                  pl.BlockSpec((D_MODEL, D_MODEL), lambda i: (0, 0)), halfrow,
                  pl.BlockSpec((None, 1, ATTN_WIDTH), lambda i: (l, 0, 0)), halfrow, halfrow],
        out_specs=[halfrow, halfrow, pl.BlockSpec((8, ATTN_WIDTH), lambda i: (0, 0)),
                   pl.BlockSpec((D_MODEL, D_MODEL), lambda i: (0, 0))],
        out_shape=[jax.ShapeDtypeStruct((s, CONV_WIDTH), F32), jax.ShapeDtypeStruct((s, ATTN_WIDTH), F32),
                   jax.ShapeDtypeStruct((8, ATTN_WIDTH), F32), jax.ShapeDtypeStruct((D_MODEL, D_MODEL), BF16)],
        scratch=[pltpu.VMEM((D_MODEL, D_MODEL), F32)],
        args=[dz, wout, o, g3, ync, yna])


def _conv_bwd(pc, dyc, wc, g3, dq, dk, dv, l, s, tr):
    hb = tr // 8
    nt = s // tr
    ext = tr + 16
    last_hb = s // 8 - 1

    def body(pc_ref, prev_ref, next_ref, dy_ref, dyn_ref, wc_ref, g_ref, dq_ref, dk_ref, dv_ref,
             dpc_ref, dw_ref, dg_ref):
        i = pl.program_id(0)
        for part, ref in enumerate((dq_ref, dk_ref, dv_ref)):
            c = 3 * CONV_WIDTH + ATTN_WIDTH * part
            dpc_ref[:, c:c + ATTN_WIDTH] = ref[...]
        gmat = _group_matrix()
        row = lax.broadcasted_iota(jnp.int32, (ext, 128), 0) + (i * tr - 8)
        inside = jnp.where(row >= 0, jnp.where(row < s, 1, 0), 0) == 1

        @pl.when(i == 0)
        def _():
            dw_ref[...] = jnp.zeros(dw_ref.shape, F32)
            dg_ref[...] = jnp.zeros(dg_ref.shape, F32)

        def extend(ref_prev, ref_mid, ref_next, c):
            if ref_prev is None:
                before = jnp.zeros((8, 128), F32)
            else:
                before = ref_prev[:, c:c + 128].astype(F32)[ref_prev.shape[0] - 8:]
            after = ref_next[:, c:c + 128].astype(F32)[0:8]
            return jnp.concatenate([before, ref_mid[:, c:c + 128].astype(F32), after], axis=0)

        for j in range(CONV_WIDTH // 128):
            c0, c1, c2 = 128 * j, CONV_WIDTH + 128 * j, 2 * CONV_WIDTH + 128 * j
            hc = extend(prev_ref, pc_ref, next_ref, c0)
            bg = extend(prev_ref, pc_ref, next_ref, c1)
            cg = extend(prev_ref, pc_ref, next_ref, c2)
            dyn = extend(None, dy_ref, dyn_ref, c0)
            w0, w1, w2 = (wc_ref[0:1, c0:c0 + 128], wc_ref[1:2, c0:c0 + 128], wc_ref[2:3, c0:c0 + 128])
            gain = g_ref[:, c0:c0 + 128]
            u = jnp.where(inside, cg * hc, 0.0)
            u1 = pltpu.roll(u, 1, 0)
            u2 = pltpu.roll(u, 2, 0)
            out = u2 * w0 + u1 * w1 + u * w2
            yc = bg * out
            r = lax.rsqrt(_group_mean(yc * yc, gmat) + EPS)
            ycn = yc * r
            gd = dyn * gain
            dyc = r * (gd - ycn * _group_mean(ycn * gd, gmat))
            dout = jnp.where(inside, dyc * bg, 0.0)
            du = dout * w2 + pltpu.roll(dout, ext - 1, 0) * w1 + pltpu.roll(dout, ext - 2, 0) * w0
            sl = slice(8, 8 + tr)
            dpc_ref[:, c0:c0 + 128] = (du[sl] * cg[sl]).astype(BF16)
            dpc_ref[:, c1:c1 + 128] = (dyc[sl] * out[sl]).astype(BF16)
            dpc_ref[:, c2:c2 + 128] = (du[sl] * hc[sl]).astype(BF16)
            dw_ref[0:1, c0:c0 + 128] += jnp.sum(dout[sl] * u2[sl], axis=0, keepdims=True)
            dw_ref[1:2, c0:c0 + 128] += jnp.sum(dout[sl] * u1[sl], axis=0, keepdims=True)
            dw_ref[2:3, c0:c0 + 128] += jnp.sum(dout[sl] * u[sl], axis=0, keepdims=True)
            dg_ref[0:1, c0:c0 + 128] += jnp.sum(dyn[sl] * ycn[sl], axis=0, keepdims=True)

    wide = 3 * CONV_WIDTH
    return _call(
        body, name="conv_bwd", grid=(nt,),
        in_specs=[pl.BlockSpec((tr, wide), lambda i: (i, 0)),
                  pl.BlockSpec((16, wide), lambda i: (jnp.maximum(i * (hb // 2) - 1, 0), 0)),
                  pl.BlockSpec((16, wide), lambda i: (jnp.minimum((i + 1) * (hb // 2), last_hb // 2), 0)),
                  pl.BlockSpec((tr, CONV_WIDTH), lambda i: (i, 0)),
                  pl.BlockSpec((8, CONV_WIDTH), lambda i: (jnp.minimum((i + 1) * hb, last_hb), 0)),
                  pl.BlockSpec((None, 8, CONV_WIDTH), lambda i: (l, 0, 0)),
                  pl.BlockSpec((None, 1, CONV_WIDTH), lambda i: (l, 0, 0)),
                  pl.BlockSpec((tr, ATTN_WIDTH), lambda i: (i, 0)),
                  pl.BlockSpec((tr, ATTN_WIDTH), lambda i: (i, 0)),
                  pl.BlockSpec((tr, ATTN_WIDTH), lambda i: (i, 0))],
        out_specs=[pl.BlockSpec((tr, PROJ_WIDTH), lambda i: (i, 0)),
                   pl.BlockSpec((8, CONV_WIDTH), lambda i: (0, 0)),
                   pl.BlockSpec((8, CONV_WIDTH), lambda i: (0, 0))],
        out_shape=[jax.ShapeDtypeStruct((s, PROJ_WIDTH), BF16), jax.ShapeDtypeStruct((8, CONV_WIDTH), F32),
                   jax.ShapeDtypeStruct((8, CONV_WIDTH), F32)],
        args=[pc, pc, pc, dyc, dyc, wc, g3, dq, dk, dv])


def _attn_bwd(qkvp, biasm, o, lse, do, l, s, pad, comm=None):
    rows = Q_STEP * Q_BLOCK
    nb = s // rows
    qb0 = pad // rows
    scale = HEAD_DIM ** -0.5
    wide = 128 * ATTN_PAIRS

    def body(q_ref, k_ref, v_ref, b_ref, o_ref, lse_ref, do_ref,
             dq_ref, dk_ref, dv_ref, ds_ref, dk_acc, dv_acc):
        blk = pl.program_id(1)

        @pl.when(blk == 0)
        def _():
            dk_acc[...] = jnp.zeros(dk_acc.shape, F32)
            dv_acc[...] = jnp.zeros(dv_acc.shape, F32)
            ds_ref[...] = jnp.zeros(ds_ref.shape, F32)

        lane = lax.broadcasted_iota(jnp.int32, (1, 128), 1)

        def step(masked):
            for sub in range(Q_STEP):
                rs = slice(Q_BLOCK * sub, Q_BLOCK * (sub + 1))
                start = blk * rows + Q_BLOCK * sub
                koff = pl.multiple_of(start + (pad - LEFT), Q_BLOCK)
                kpos = lax.broadcasted_iota(jnp.int32, (1, K_BAND), 1) + (start - LEFT)
                kmask = jnp.where(kpos >= 0, 0.0, NEG_INF)
                for pr in range(ATTN_PAIRS):
                    ls = slice(128 * pr, 128 * (pr + 1))
                    q = q_ref[rs, ls]
                    kb = k_ref[pl.ds(koff, K_BAND), ls]
                    vb = v_ref[pl.ds(koff, K_BAND), ls]
                    dov = do_ref[rs, ls]
                    lse_v = lse_ref[rs, ls]
                    prod = dov * o_ref[rs, ls]
                    dq_parts = []
                    dk_new = jnp.zeros((128, K_BAND), F32)
                    dv_new = jnp.zeros((128, K_BAND), F32)
                    for hh in range(2):
                        in_head = (lane >> 6) == hh
                        qm = jnp.where(in_head, q, jnp.zeros_like(q)) * jnp.asarray(scale, BF16)
                        dom = jnp.where(in_head, dov, 0.0).astype(BF16)
                        delta = jnp.sum(jnp.where(in_head, prod, 0.0), axis=1, keepdims=True)
                        lse_h = lse_v[:, HEAD_DIM * hh:HEAD_DIM * hh + 1]
                        sc = _dot_nt(qm, kb) + b_ref[2 * pr + hh]
                        if masked:
                            sc = sc + kmask
                        p = jnp.exp(sc - lse_h)
                        dp = _dot_nt(dom, vb)
                        ds = p * (dp - delta)
                        ds_ref[2 * pr + hh] += ds
                        dsb = ds.astype(BF16)
                        dq_parts.append(_dot(dsb, kb) * scale)
                        dk_new = dk_new + _dot_tn(qm, dsb)
                        dv_new = dv_new + _dot_tn(dom, p.astype(BF16))
                    dq_ref[rs, ls] = jnp.where(lane < HEAD_DIM, dq_parts[0], dq_parts[1]).astype(BF16)
                    dk_acc[pr, :, pl.ds(koff, K_BAND)] += dk_new
                    dv_acc[pr, :, pl.ds(koff, K_BAND)] += dv_new

        pl.when(blk == 0)(lambda: step(True))
        pl.when(blk > 0)(lambda: step(False))

        @pl.when(blk == nb - 1)
        def _():
            for pr in range(ATTN_PAIRS):
                ls = slice(128 * pr, 128 * (pr + 1))
                dk_ref[:, ls] = dk_acc[pr, :, pad:pad + s].T.astype(BF16)
                dv_ref[:, ls] = dv_acc[pr, :, pad:pad + s].T.astype(BF16)

    n_grp = ATTN_WIDTH // wide
    qblk = pl.BlockSpec((rows, wide), lambda p, b: (b, p))
    col = pl.BlockSpec((s, wide), lambda p, b: (0, p))
    shp = jax.ShapeDtypeStruct((s, ATTN_WIDTH), BF16)
    return _call(
        body, name="attn_bwd", grid=(n_grp, nb),
        in_specs=[pl.BlockSpec((rows, wide), lambda p, b: (qb0 + b, p)),
                  pl.BlockSpec((s + pad, wide), lambda p, b: (0, n_grp + p)),
                  pl.BlockSpec((s + pad, wide), lambda p, b: (0, 2 * n_grp + p)),
                  pl.BlockSpec((None, 2 * ATTN_PAIRS, Q_BLOCK, K_BAND), lambda p, b: (l, p, 0, 0)),
                  qblk, qblk, qblk],
        out_specs=[qblk, col, col, pl.BlockSpec((2 * ATTN_PAIRS, Q_BLOCK, K_BAND), lambda p, b: (p, 0, 0))],
        out_shape=[shp, shp, shp, jax.ShapeDtypeStruct((N_HEADS, Q_BLOCK, K_BAND), F32)],
        scratch=[pltpu.VMEM((ATTN_PAIRS, 128, s + pad), F32), pltpu.VMEM((ATTN_PAIRS, 128, s + pad), F32)],
        args=[qkvp, qkvp, qkvp, biasm, o, lse, do], comm=comm)


def _dw_in(h, dproj, s):
    def body(a_ref, b_ref, o_ref):
        acc = _dot_tn(a_ref[...], b_ref[...])
        o_ref[0] = acc[:, 0:PROJ_SHARD].astype(BF16)
        o_ref[1] = acc[:, PROJ_SHARD:2 * PROJ_SHARD].astype(BF16)

    out, = _call(
        body, name="dw_in", grid=(4,),
        in_specs=[pl.BlockSpec((s, D_MODEL), lambda n: (0, 0)),
                  pl.BlockSpec((s, 2 * PROJ_SHARD), lambda n: (0, n))],
        out_specs=[pl.BlockSpec((2, D_MODEL, PROJ_SHARD), lambda n: (n, 0, 0))],
        out_shape=[jax.ShapeDtypeStruct((N_DEV, D_MODEL, PROJ_SHARD), BF16)], args=[h, dproj])
    return out


def _in_proj_bwd(dproj, win, x, g3, dres, l, s, tm, f_prev=None, g_post3=None, comm=None):
    chain = f_prev is not None

    def body(d_ref, w_ref, x_ref, g_ref, dres_ref, *rest):
        if chain:
            f_ref, gq_ref, dx_ref, dg_ref, df_ref, dgq_ref = rest
            _zero_first((dg_ref, dgq_ref), pl.program_id(0) == 0)
        else:
            dx_ref, dg_ref = rest
            _zero_first((dg_ref,), pl.program_id(0) == 0)
        w = jnp.concatenate([w_ref[j] for j in range(N_DEV)], axis=1)
        for rs in _row_subtiles(tm, SUB_ROWS):
            dh = _dot_nt(d_ref[rs, :], w)
            dx, dyn = _norm_bwd_rows(x_ref[rs, :], g_ref[...], dh)
            dx = dres_ref[rs, :] + dx
            dx_ref[rs, :] = dx
            _add_cols(dg_ref, dyn)
            if chain:
                df, dyn2 = _norm_bwd_rows(f_ref[rs, :].astype(F32), gq_ref[...], dx)
                df_ref[rs, :] = df.astype(BF16)
                _add_cols(dgq_ref, dyn2)

    row = pl.BlockSpec((tm, D_MODEL), lambda i: (i, 0))
    dgs = pl.BlockSpec((8, D_MODEL), lambda i: (0, 0))
    in_specs = [pl.BlockSpec((tm, PROJ_WIDTH), lambda i: (i, 0)),
                pl.BlockSpec((N_DEV, D_MODEL, PROJ_SHARD), lambda i: (0, 0, 0), pipeline_mode=pl.Buffered(1)),
                row, pl.BlockSpec((None, 1, D_MODEL), lambda i: (l, 0, 0)), row]
    out_specs = [row, dgs]
    out_shape = [jax.ShapeDtypeStruct((s, D_MODEL), F32), jax.ShapeDtypeStruct((8, D_MODEL), F32)]
    args = [dproj, win, x, g3, dres]
    if chain:
        in_specs += [row, pl.BlockSpec((None, 1, D_MODEL), lambda i: (l - 1, 0, 0))]
        out_specs += [row, dgs]
        out_shape += [jax.ShapeDtypeStruct((s, D_MODEL), BF16), jax.ShapeDtypeStruct((8, D_MODEL), F32)]
        args += [f_prev, g_post3]
    return _call(body, name="in_proj_bwd", grid=(s // tm,), in_specs=in_specs, out_specs=out_specs,
                 out_shape=out_shape, args=args, comm=comm)


def _adamw(name, w, m, v, lands, owns=None, me=None):
    groups, rows, cols = w.shape
    assert len(lands) == groups
    n_part = lands[0].shape[0]
    tr = _row_tile(rows, tuple(c for c in (512, 352, 256, 176, 128, 64, 32, 16, 8) if c * cols <= 256 * 1024))
    c1 = 1.0 - ADAM_B1 ** ADAM_STEP
    c2 = 1.0 - ADAM_B2 ** ADAM_STEP
    n_own = groups if owns is not None else 0

    def body(*refs):
        if n_own:
            me_ref, refs = refs[0], refs[1:]
        w_ref, m_ref, v_ref = refs[:3]
        land_refs = refs[3:3 + groups]
        own_refs = refs[3 + groups:3 + groups + n_own]
        g_ref, d_ref, nm_ref, nv_ref = refs[3 + groups + n_own:]
        grp = pl.program_id(0)
        for gi in range(groups):
            @pl.when(grp == gi)
            def _():
                l_ref = land_refs[gi]
                g = None
                for p in range(n_part):
                    part = l_ref[p].astype(F32)
                    if n_own:
                        part = jnp.where(me_ref[0] == p, own_refs[gi][...].astype(F32), part)
                    g = part if g is None else g + part
                g_ref[...] = g
                m1 = ADAM_B1 * m_ref[...] + (1.0 - ADAM_B1) * g
                v1 = ADAM_B2 * v_ref[...] + (1.0 - ADAM_B2) * (g * g)
                nm_ref[...] = m1
                nv_ref[...] = v1
                d_ref[...] = -ADAM_LR * ((m1 / c1) / (jnp.sqrt(v1 / c2) + ADAM_EPS) + ADAM_WD * w_ref[...])

    blk = pl.BlockSpec((None, tr, cols), lambda g, i, *_: (g, i, 0))
    shp = jax.ShapeDtypeStruct((groups, rows, cols), F32)

    def land_spec(gi):
        return pl.BlockSpec((n_part, tr, cols), lambda g, i, *_: (0, jnp.where(g == gi, i, 0), 0))

    def own_spec(gi):
        if owns[gi].ndim == 3:
            return pl.BlockSpec((None, tr, cols), lambda g, i, me_ref: (me_ref[0], jnp.where(g == gi, i, 0), 0))
        return pl.BlockSpec((tr, cols), lambda g, i, me_ref: (jnp.where(g == gi, i, 0), 0))

    in_specs = [blk, blk, blk] + [land_spec(gi) for gi in range(groups)] + [own_spec(gi) for gi in range(n_own)]
    args = [w, m, v] + list(lands) + (list(owns) if n_own else [])
    if not n_own:
        return _call(body, name=name, grid=(groups, rows // tr), in_specs=in_specs,
                     out_specs=[blk, blk, blk, blk], out_shape=[shp, shp, shp, shp], args=args)
    return pl.pallas_call(
        body, name=name,
        grid_spec=pltpu.PrefetchScalarGridSpec(
            num_scalar_prefetch=1, grid=(groups, rows // tr), in_specs=in_specs, out_specs=[blk, blk, blk, blk]),
        out_shape=[shp, shp, shp, shp],
        compiler_params=pltpu.CompilerParams(dimension_semantics=("arbitrary", "arbitrary"),
                                             vmem_limit_bytes=VMEM_LIMIT),
    )(me, *args)


def _pack_small(rel, gco, gao, gpm, gqm, gpf, gqf):
    n_layers = rel.shape[0]
    relp = jnp.pad(rel, ((0, 0), (0, 0), (0, REL_PAD - rel.shape[2])))
    parts = [relp.reshape(n_layers * N_HEADS * REL_PAD // 128, 128)]
    parts += [a.reshape(-1, 128) for a in (gco, gao, gpm, gqm, gpf, gqf)]
    return jnp.concatenate(parts, axis=0)


def _pack_small_grads(d_rel, parts):
    n_layers = len(d_rel)
    keys = ("gco", "gao", "gpm", "gqm", "gpf", "gqf")
    arrays = list(d_rel) + [parts[k][l] for k in keys for l in range(n_layers)] + list(parts["wc"])
    rows = 0
    plan = []
    for l in range(n_layers):
        for h in range(N_HEADS):
            for t in range(REL_PAD // 128):
                plan.append((l, (0, h), t, rows))
                rows += 1
    for ki, k in enumerate(keys):
        for l in range(n_layers):
            for t in range(parts[k][l].shape[1] // 128):
                plan.append((n_layers * (1 + ki) + l, (0,), t, rows))
                rows += 1
    for l in range(n_layers):
        for tap in range(3):
            for t in range(CONV_WIDTH // 128):
                plan.append((n_layers * (1 + len(keys)) + l, (tap,), t, rows))
                rows += 1
    total = rows + (-rows) % 8

    def body(*refs):
        o_ref = refs[-1]
        if total > rows:
            o_ref[rows:total, :] = jnp.zeros((total - rows, 128), F32)
        for op, idx, t, dst in plan:
            lanes = slice(128 * t, 128 * (t + 1))
            if len(idx) == 2:
                o_ref[dst:dst + 1, :] = refs[op][idx[0], idx[1]:idx[1] + 1, lanes]
            else:
                o_ref[dst:dst + 1, :] = refs[op][idx[0]:idx[0] + 1, lanes]

    vmem = pl.BlockSpec(memory_space=pltpu.VMEM)
    out, = _call(body, name="pack_small_grads", grid=(), in_specs=[vmem] * len(arrays), out_specs=[vmem],
                 out_shape=[jax.ShapeDtypeStruct((total, 128), F32)], args=arrays)
    return out


def _unpack_small(p, n_layers):
    n_rel = n_layers * N_HEADS * REL_PAD // 128
    rel = p[:n_rel].reshape(n_layers, N_HEADS, REL_PAD)[:, :, :2 * REL_CLIP + 1]
    outs = [rel]
    r0 = n_rel
    for width in (CONV_WIDTH, ATTN_WIDTH, D_MODEL, D_MODEL, D_MODEL, D_MODEL):
        nr = n_layers * width // 128
        outs.append(p[r0:r0 + nr].reshape(n_layers, width))
        r0 += nr
    return outs


def kernel(x, w_in, w_conv, rel_bias, g_conv_out, g_attn_out, w_out, g_pre_mix, g_post_mix, g_pre_ffn, g_post_ffn, w_ffn_in, w_ffn_out, loss_target, m_w_in, m_w_conv, m_rel_bias, m_g_conv_out, m_g_attn_out, m_w_out, m_g_pre_mix, m_g_post_mix, m_g_pre_ffn, m_g_post_ffn, m_w_ffn_in, m_w_ffn_out, v_w_in, v_w_conv, v_rel_bias, v_g_conv_out, v_g_attn_out, v_w_out, v_g_pre_mix, v_g_post_mix, v_g_pre_ffn, v_g_post_ffn, v_w_ffn_in, v_w_ffn_out):
    n_layers = w_in.shape[0]
    s = x.shape[1]
    assert x.shape == (1, s, D_MODEL) and s % 1024 == 0
    assert w_in.shape == (n_layers, D_MODEL, PROJ_SHARD) and w_ffn_in.shape == (n_layers, D_MODEL, FF_SHARD)
    tm = 512
    tq = 1024 if s >= 2048 else 512
    tf = min(1024, s)
    x0 = x.reshape(s, D_MODEL)
    target = loss_target.reshape(s, D_MODEL)
    dev = _dev_index(lax.axis_index("x"), lax.axis_index("y"), lax.axis_index("c"))

    wt_ffn_in, mt_ffn_in, vt_ffn_in = (jnp.transpose(a, (0, 2, 1)) for a in (w_ffn_in, m_w_ffn_in, v_w_ffn_in))
    local_w = [_cast_bf16(w_in, "cast_w_in"), _cast_bf16(w_out, "cast_w_out"),
               _cast_bf16(wt_ffn_in, "cast_w_ffn_in"), _cast_bf16(w_ffn_out, "cast_w_ffn_out")]
    wc_local = jnp.pad(jnp.transpose(w_conv, (0, 2, 1)).reshape(-1), (0, 1024 - n_layers * 3 * 64)).reshape(8, 128)
    biasm, win_next, wc_g = _bias_build(jnp.pad(rel_bias, ((0, 0), (0, 0), (0, REL_PAD - rel_bias.shape[2]))),
                                        comm=_Gather([(local_w[0], 0), (wc_local, None)]))
    weights = [None] * n_layers
    wc_full = wc_g.reshape(N_DEV, 1024)[:, :n_layers * 3 * 64].reshape(N_DEV, n_layers, 3, 64)
    wc_full = jnp.transpose(wc_full, (1, 2, 0, 3)).reshape(n_layers, 3, CONV_WIDTH)
    wc_full = jnp.pad(wc_full, ((0, 0), (0, 5), (0, 0)))

    g3 = {k: v.reshape(n_layers, 1, -1) for k, v in dict(
        conv=g_conv_out, attn=g_attn_out, pre_mix=g_pre_mix, post_mix=g_post_mix,
        pre_ffn=g_pre_ffn, post_ffn=g_post_ffn).items()}

    saved = []
    xl = x0
    h = _norm_cast(x0, g3["pre_mix"], 0, tm)
    for l in range(n_layers):
        win = win_next
        pc, ync, wout = _in_proj_conv(h, win, wc_full, g3["conv"], l, s, tq, comm=_Gather([(local_w[1], l)]))
        qkvp = _in_proj_qkv(h, win, s, tq)
        o, lse, yna, wfin = _attn_fwd(qkvp, biasm, g3["attn"], l, s, tq, comm=_Gather([(local_w[2], l)]))
        wout = wout.reshape(D_MODEL, D_MODEL)
        z, xm, h2 = _out_proj_fwd(ync, yna, wout, xl, g3["post_mix"], g3["pre_ffn"], l, s, tq)
        gu, act, wfout = _ffn_in_fwd(h2, wfin.reshape(4, FF_PAIR, D_MODEL), s, tf, comm=_Gather([(local_w[3], l)]))
        wfo = wfout.reshape(D_FF, D_MODEL)
        weights[l] = [win, wout, wfin.reshape(2 * D_FF, D_MODEL), wfo]
        sv = dict(x=xl, h=h, pc=pc, qkvp=qkvp, ync=ync, yna=yna, o=o, lse=lse, z=z, xm=xm, h2=h2, gu=gu, act=act)
        if l + 1 < n_layers:
            sv["f"], xl, h, win_next = _ffn_out_fwd(act, wfo, xm, g3["post_ffn"], g3["pre_mix"], l, l + 1, s, tm,
                                                    comm=_Gather([(local_w[0], l + 1)]))
        else:
            dx, sq, df, dg_post_ffn = _ffn_out_loss(act, wfo, xm, g3["post_ffn"], target, l, s, tm)
        saved.append(sv)

    loss = lax.psum(jnp.sum(sq) * (0.5 / D_MODEL), ("x", "y", "c"))

    lands = dict(win=[None] * n_layers, wout=[None] * n_layers, wfin=[None] * n_layers, wfout=[None] * n_layers)
    small = {k: [None] * n_layers for k in ("gco", "gao", "gpm", "gqm", "gpf", "gqf", "wc")}
    d_rel = [None] * n_layers
    started = []

    def start(name, keys, l, arrays):
        items = [(a, False) for a in arrays]
        send_sems, recv_sems, srcs, zones, token = _exchange_start(name + "_start", items)
        started.append((name, keys, l, items, send_sems, recv_sems, srcs, zones))
        return token[0:1, 0:1].reshape(1, 1, 1)

    for l in reversed(range(n_layers)):
        sv = saved[l]
        win, wout, wfin, wfo = weights[l]
        small["gqf"][l] = dg_post_ffn
        dgu, d_wfout = _ffn_out_bwd(df, wfo, sv["gu"], sv["act"], s, tf)
        d_wfout = d_wfout.reshape(N_DEV, FFO_SHARD, D_MODEL)
        d_wfin = _dw_ffn_in(sv["h2"], dgu, s).reshape(N_DEV, FF_SHARD, D_MODEL)
        g_pre_ffn = g3["pre_ffn"]
        if l == 0:
            g_pre_ffn = g_pre_ffn + start("exchange_ffn0", ("wfout", "wfin"), l, [d_wfout, d_wfin])
        dxm, dz, dg_pre_ffn, dg_post_mix = _ffn_in_bwd(
            dgu, wfin, sv["xm"], g_pre_ffn, dx, sv["z"], g3["post_mix"], l, s, tm)
        small["gpf"][l] = dg_pre_ffn
        small["gqm"][l] = dg_post_mix
        dyc, do, dg_attn, d_wout = _out_proj_bwd(dz, wout, sv["o"], g3["attn"], sv["ync"], sv["yna"], l, s, tq)
        d_wout = d_wout.reshape(N_DEV, D_MODEL // N_DEV, D_MODEL)
        small["gao"][l] = dg_attn
        dq, dk, dv, ds_sum = _attn_bwd(sv["qkvp"], biasm, sv["o"], sv["lse"], do, l, s, tq)
        d_rel[l] = _bias_bwd(ds_sum[None])
        dproj, dwc, dg_conv = _conv_bwd(sv["pc"], dyc, wc_full, g3["conv"], dq, dk, dv, l, s, tf)
        small["wc"][l] = dwc
        small["gco"][l] = dg_conv
        d_win = _dw_in(sv["h"], dproj, s)
        if l == 0:
            token = start("exchange_mix0", ("wout", "win"), l, [d_wout, d_win])
        else:
            token = start(f"exchange_layer{l}", ("wfout", "wfin", "wout", "win"), l, [d_wfout, d_wfin, d_wout, d_win])
        if l > 0:
            dx, dg_pre_mix, df, dg_post_ffn = _in_proj_bwd(
                dproj, win, sv["x"], g3["pre_mix"] + token, dxm, l, s, tm, f_prev=saved[l - 1]["f"],
                g_post3=g3["post_ffn"])
        else:
            dx, dg_pre_mix = _in_proj_bwd(dproj, win, sv["x"], g3["pre_mix"] + token, dxm, l, s, tm)
        small["gpm"][l] = dg_pre_mix
    grad_x = dx.reshape(1, s, D_MODEL)

    small_vec = _pack_small_grads(d_rel, small)
    small_items = [(small_vec, True)]
    small_sems = _exchange_start("exchange_small_start", small_items)

    owns = dict(win=[None] * n_layers, wout=[None] * n_layers, wfin=[None] * n_layers, wfout=[None] * n_layers)

    def wait(last, after):
        for name, keys, l, items, send_sems, recv_sems, srcs, zones in started:
            if (name == "exchange_mix0") == last:
                srcs, zones = _exchange_wait(name + "_wait", items, send_sems, recv_sems, srcs, zones, after)
                for key, src, zone in zip(keys, srcs, zones):
                    owns[key][l], lands[key][l] = src, zone

    me = dev.astype(jnp.int32).reshape(1)
    wait(False, small_sems[4])
    r_fin = [jnp.transpose(t, (0, 2, 1)) for t in _adamw(
        "adamw_w_ffn_in", wt_ffn_in, mt_ffn_in, vt_ffn_in, lands["wfin"], owns["wfin"], me)]
    r_fout = _adamw("adamw_w_ffn_out", w_ffn_out, m_w_ffn_out, v_w_ffn_out, lands["wfout"], owns["wfout"], me)
    wait(True, r_fout[0])
    r_out = _adamw("adamw_w_out", w_out, m_w_out, v_w_out, lands["wout"], owns["wout"], me)
    r_in = _adamw("adamw_w_in", w_in, m_w_in, v_w_in, lands["win"], owns["win"], me)
    (small_own,), (land_small,) = _exchange_wait(
        "exchange_small_wait", small_items, small_sems[0], small_sems[1], small_sems[2], small_sems[3], r_in[0])

    n_rep = 64 * n_layers
    rep = _adamw(
        "adamw_replicated",
        _pack_small(rel_bias, g_conv_out, g_attn_out, g_pre_mix, g_post_mix, g_pre_ffn, g_post_ffn)[None],
        _pack_small(m_rel_bias, m_g_conv_out, m_g_attn_out, m_g_pre_mix, m_g_post_mix, m_g_pre_ffn, m_g_post_ffn)[None],
        _pack_small(v_rel_bias, v_g_conv_out, v_g_attn_out, v_g_pre_mix, v_g_post_mix, v_g_pre_ffn, v_g_post_ffn)[None],
        [land_small[:, :n_rep]], [small_own[:n_rep]], me)
    rep = [_unpack_small(t[0], n_layers) for t in rep]

    wc_rows = n_layers * 3 * CONV_WIDTH // 128
    zeros_wc = jnp.zeros((1, wc_rows, 128), F32)
    g_wc_full = _adamw("sum_w_conv", zeros_wc, zeros_wc, zeros_wc, [land_small[:, n_rep:n_rep + wc_rows]],
                       [small_own[n_rep:n_rep + wc_rows]], me)[0]
    g_wc_full = g_wc_full.reshape(n_layers, 3, CONV_WIDTH)
    g_wc = lax.dynamic_slice_in_dim(g_wc_full, dev * (CONV_WIDTH // N_DEV), CONV_WIDTH // N_DEV, axis=2)
    g_wc = jnp.transpose(g_wc, (0, 2, 1))

    def tiny(a):
        flat = a.reshape(-1)
        return jnp.pad(flat, (0, (-flat.shape[0]) % 1024)).reshape(1, -1, 128)

    r_wc = _adamw("adamw_w_conv", tiny(w_conv), tiny(m_w_conv), tiny(v_w_conv), [tiny(g_wc)])
    r_wc = [t.reshape(-1)[:w_conv.size].reshape(w_conv.shape) for t in r_wc]

    def leaf(kind):
        return [r_in[kind], r_wc[kind], rep[kind][0], rep[kind][1], rep[kind][2], r_out[kind],
                rep[kind][3], rep[kind][4], rep[kind][5], rep[kind][6], r_fin[kind], r_fout[kind]]

    return (loss, grad_x, *leaf(0), *leaf(1), *leaf(2), *leaf(3))
```

```python
import math

import jax
import jax.numpy as jnp
from jax import lax
from jax.experimental import pallas as pl
from jax.experimental.pallas import tpu as pltpu

F32 = jnp.float32
BF16 = jnp.bfloat16

D_MODEL = 1024
N_DEV = 8
CHUNK = 64
N_LEFT_CHUNKS = 8
CONV_WIDTH = 512
ATTN_WIDTH = 512
HEAD_DIM = 64
N_HEADS = 8
REL_CLIP = 128
REL_PAD = 384
PROJ_WIDTH = 3072
PROJ_SHARD = PROJ_WIDTH // N_DEV
D_FF = 2816
FF_SHARD = 2 * D_FF // N_DEV
FFO_SHARD = D_FF // N_DEV
FF_PAIR = 2 * FF_SHARD
_COL_SUBTILES = (slice(0, 768), slice(768, FF_PAIR))
EPS = 1e-6
NEG_INF = -1e30
ATTN_PAIRS = 2
Q_STEP = 2
Q_BLOCK = 4 * CHUNK
K_BAND = Q_BLOCK + N_LEFT_CHUNKS * CHUNK
LEFT = N_LEFT_CHUNKS * CHUNK
TOEP = 1024

ADAM_LR = 0.001
ADAM_B1 = 0.9
ADAM_B2 = 0.999
ADAM_EPS = 1e-08
ADAM_WD = 0.01
ADAM_STEP = 10

VMEM_LIMIT = 52 * 1024 * 1024
SUB_ROWS = 256
MESH = pl.DeviceIdType.MESH
ANY = pl.BlockSpec(memory_space=pl.ANY)

NT = (((1,), (1,)), ((), ()))
TN = (((0,), (0,)), ((), ()))


def _dot(a, b):
    return jnp.dot(a, b, preferred_element_type=F32)


def _dot_nt(a, b):
    return lax.dot_general(a, b, NT, preferred_element_type=F32)


def _dot_tn(a, b):
    return lax.dot_general(a, b, TN, preferred_element_type=F32)


def _rstd(v):
    return lax.rsqrt(jnp.mean(v * v, axis=-1, keepdims=True) + EPS)


def _group_matrix():
    r = lax.broadcasted_iota(jnp.int32, (128, 128), 0) >> 6
    c = lax.broadcasted_iota(jnp.int32, (128, 128), 1) >> 6
    return jnp.where(r == c, 1.0, 0.0).astype(BF16)


def _group_mean(v, gmat):
    hi = v.astype(BF16)
    lo = (v - hi.astype(F32)).astype(BF16)
    return (_dot(hi, gmat) + _dot(lo, gmat)) * (1.0 / HEAD_DIM)


def _split3(v):
    hi = v.astype(BF16)
    r1 = v - hi.astype(F32)
    mid = r1.astype(BF16)
    lo = (r1 - mid.astype(F32)).astype(BF16)
    return hi, mid, lo


def _row_tile(rows, cands=(1024, 512, 704, 256, 128, 64, 32, 16)):
    for c in cands:
        if rows % c == 0:
            return c
    return rows


def _dev_index(px, py, pc):
    return 4 * px + 2 * py + pc


def _when(cond):
    if cond is True:
        return lambda fn: fn()
    return pl.when(cond)


def _phases(grid):
    def phases():
        if not grid:
            return True, True, True
        lin = pl.program_id(0)
        for a in range(1, len(grid)):
            lin = lin * grid[a] + pl.program_id(a)
        total = math.prod(grid)
        return lin == 0, lin == total - 1, lin == total - 1
    return phases


class _Gather:
    def __init__(self, items):
        self.items = items
        self.args = [a for a, _ in items]
        n = len(items)
        self.out_shape = [jax.ShapeDtypeStruct((N_DEV,) + (a.shape if lay is None else a.shape[1:]), a.dtype)
                          for a, lay in items]
        self.scratch = [pltpu.SemaphoreType.DMA((n, 7)), pltpu.SemaphoreType.DMA((n, 7)),
                        pltpu.SemaphoreType.DMA((n,))]

    def _ctx(self, ins, outs, sems):
        send_sems, recv_sems, local_sems = sems
        x, y, c = lax.axis_index("x"), lax.axis_index("y"), lax.axis_index("c")
        chips = [(1 - x, y), (x, 1 - y), (1 - x, 1 - y)]

        def src(k):
            lay = self.items[k][1]
            return ins[k] if lay is None else ins[k].at[lay]

        def copy(k, s, idx, to, from_src=False):
            return pltpu.make_async_remote_copy(
                src_ref=src(k) if from_src else outs[k].at[idx], dst_ref=outs[k].at[idx],
                send_sem=send_sems.at[k, s], recv_sem=recv_sems.at[k, s],
                device_id=to, device_id_type=MESH)

        def local(k):
            return pltpu.make_async_copy(src(k), outs[k].at[_dev_index(x, y, c)], local_sems.at[k])

        return x, y, c, chips, copy, local

    def start(self, ins, outs, sems, cond):
        n = len(self.items)

        @_when(cond)
        def _():
            x, y, c, chips, copy, local = self._ctx(ins, outs, sems)
            me = _dev_index(x, y, c)
            for k in range(n):
                local(k).start()
                copy(k, 0, me, (x, y, 1 - c), from_src=True).start()
                for j, chip in enumerate(chips):
                    copy(k, 1 + j, me, (chip[0], chip[1], c), from_src=True).start()

    def forward(self, ins, outs, sems, cond):
        n = len(self.items)

        @_when(cond)
        def _():
            x, y, c, chips, copy, local = self._ctx(ins, outs, sems)
            for j, chip in enumerate(chips):
                idx = _dev_index(chip[0], chip[1], c)
                for k in range(n):
                    copy(k, 1 + j, idx, (x, y, c)).wait_recv()
                    copy(k, 4 + j, idx, (x, y, 1 - c)).start()

    def finish(self, ins, outs, sems, cond):
        n = len(self.items)

        @_when(cond)
        def _():
            x, y, c, chips, copy, local = self._ctx(ins, outs, sems)
            me = _dev_index(x, y, c)
            for k in range(n):
                copy(k, 0, _dev_index(x, y, 1 - c), (x, y, c)).wait_recv()
            for j, chip in enumerate(chips):
                idx = _dev_index(chip[0], chip[1], 1 - c)
                for k in range(n):
                    copy(k, 4 + j, idx, (x, y, c)).wait_recv()
            for k in range(n):
                for s in range(4):
                    copy(k, s, me, (x, y, c), from_src=True).wait_send()
                for j, chip in enumerate(chips):
                    copy(k, 4 + j, _dev_index(chip[0], chip[1], c), (x, y, c)).wait_send()
                local(k).wait()


_PEER_FLIPS = [(0, 0, 1), (1, 0, 0), (0, 1, 0), (1, 1, 0), (1, 0, 1), (0, 1, 1), (1, 1, 1)]


def _call(body, *, name, grid, in_specs, out_specs, out_shape, args, scratch=(), comm=None):
    n_hi, n_ho, n_hs = len(args), len(out_shape), len(scratch)
    c_args = list(comm.args) if comm else []
    c_out = list(comm.out_shape) if comm else []
    c_scr = list(comm.scratch) if comm else []
    phases = _phases(grid)

    def kern(*refs):
        cuts = [n_hi, len(c_args), n_ho, len(c_out), n_hs, len(c_scr)]
        parts, pos = [], 0
        for n in cuts:
            parts.append(refs[pos:pos + n])
            pos += n
        hi, ci, ho, co, hs, cs = parts
        if comm:
            first, mid, last = phases()
            comm.start(ci, co, cs, first)
            comm.forward(ci, co, cs, mid)
        body(*hi, *ho, *hs)
        if comm:
            comm.finish(ci, co, cs, last)

    sem = ("arbitrary",) * len(grid) if grid else None
    return pl.pallas_call(
        kern, name=name, grid=grid,
        in_specs=list(in_specs) + [ANY] * len(c_args),
        out_specs=list(out_specs) + [ANY] * len(c_out),
        out_shape=list(out_shape) + c_out,
        scratch_shapes=list(scratch) + c_scr,
        compiler_params=pltpu.CompilerParams(dimension_semantics=sem, vmem_limit_bytes=VMEM_LIMIT),
    )(*args, *c_args)


HBM_SPEC = pl.BlockSpec(memory_space=pltpu.HBM)
SEM_SPEC = pl.BlockSpec(memory_space=pltpu.SEMAPHORE)
SIDE_EFFECT = pltpu.SideEffectType.DATAFLOW_SIDE_EFFECTING


def _exchange_peer(x, y, c, s):
    fx, fy, fc = _PEER_FLIPS[s]
    return x ^ fx, y ^ fy, c ^ fc


def _exchange_start(name, items):
    n = len(items)
    srcs = [pltpu.with_memory_space_constraint(a, pltpu.HBM) for a, _ in items]
    land_shapes = [(N_DEV,) + (a.shape if whole else a.shape[1:]) for a, whole in items]
    lands = [pltpu.with_memory_space_constraint(lax.empty(shp, a.dtype), pltpu.HBM)
             for shp, (a, _) in zip(land_shapes, items)]

    n_sem = 7 * n

    def body(*refs):
        src_refs, land_refs = refs[:n], refs[n:2 * n]
        send_sems = refs[2 * n:2 * n + n_sem]
        recv_sems = refs[2 * n + n_sem:2 * n + 2 * n_sem]
        token = refs[-1]
        x, y, c = lax.axis_index("x"), lax.axis_index("y"), lax.axis_index("c")
        me = _dev_index(x, y, c)
        for s in range(7):
            px, py, pc = _exchange_peer(x, y, c, s)
            for k in range(n):
                src = src_refs[k] if items[k][1] else src_refs[k].at[_dev_index(px, py, pc)]
                pltpu.make_async_remote_copy(
                    src_ref=src, dst_ref=land_refs[k].at[me],
                    send_sem=send_sems[7 * k + s], recv_sem=recv_sems[7 * k + s],
                    device_id=(px, py, pc), device_id_type=MESH).start()
        token[...] = jnp.zeros(token.shape, token.dtype)

    outs = pl.pallas_call(
        body, name=name,
        out_shape=(*[pltpu.SemaphoreType.DMA(())] * (2 * n_sem),
                   *[pltpu.HBM(a.shape, a.dtype) for a in srcs],
                   *[pltpu.HBM(shp, a.dtype) for shp, a in zip(land_shapes, srcs)],
                   jax.ShapeDtypeStruct((8, 128), F32)),
        in_specs=[HBM_SPEC] * (2 * n),
        out_specs=(*[SEM_SPEC] * (2 * n_sem), *[HBM_SPEC] * (2 * n), pl.BlockSpec(memory_space=pltpu.VMEM)),
        input_output_aliases={i: 2 * n_sem + i for i in range(2 * n)},
        compiler_params=pltpu.CompilerParams(has_side_effects=SIDE_EFFECT),
    )(*srcs, *lands)
    base = 2 * n_sem
    return (list(outs[:n_sem]), list(outs[n_sem:base]), list(outs[base:base + n]),
            list(outs[base + n:base + 2 * n]), outs[-1])


def _exchange_wait(name, items, send_sems, recv_sems, srcs, lands, after):
    n = len(items)

    n_sem = 7 * n

    def body(*refs):
        src_refs, land_refs = refs[:n], refs[n:2 * n]
        send_refs = refs[2 * n:2 * n + n_sem]
        recv_refs = refs[2 * n + n_sem:2 * n + 2 * n_sem]
        x, y, c = lax.axis_index("x"), lax.axis_index("y"), lax.axis_index("c")
        for s in range(7):
            for k in range(n):
                copy = pltpu.make_async_remote_copy(
                    src_ref=src_refs[k] if items[k][1] else src_refs[k].at[0], dst_ref=land_refs[k].at[0],
                    send_sem=send_refs[7 * k + s], recv_sem=recv_refs[7 * k + s],
                    device_id=(x, y, c), device_id_type=MESH)
                copy.wait_send()
                copy.wait_recv()

    outs = pl.pallas_call(
        body, name=name,
        out_shape=(*[pltpu.HBM(a.shape, a.dtype) for a in srcs], *[pltpu.HBM(a.shape, a.dtype) for a in lands]),
        in_specs=[HBM_SPEC] * (2 * n) + [SEM_SPEC] * (2 * n_sem) + [ANY],
        out_specs=tuple([HBM_SPEC] * (2 * n)),
        input_output_aliases={i: i for i in range(2 * n)},
        compiler_params=pltpu.CompilerParams(has_side_effects=SIDE_EFFECT),
    )(*srcs, *lands, *send_sems, *recv_sems, after)
    return list(outs[:n]), list(outs[n:])


def _cast_bf16(x, name):
    shape = x.shape
    x2 = x.reshape(-1, shape[-1])
    rows, cols = x2.shape
    tr = _row_tile(rows)

    def body(x_ref, o_ref):
        o_ref[...] = x_ref[...].astype(BF16)

    blk = pl.BlockSpec((tr, cols), lambda i: (i, 0))
    out, = _call(body, name=name, grid=(rows // tr,), in_specs=[blk], out_specs=[blk],
                 out_shape=[jax.ShapeDtypeStruct((rows, cols), BF16)], args=[x2])
    return out.reshape(shape)


def _norm_cast(x, g3, l, tm):
    s = x.shape[0]

    def body(x_ref, g_ref, o_ref):
        v = x_ref[...]
        o_ref[...] = (v * _rstd(v) * g_ref[...]).astype(BF16)

    row = pl.BlockSpec((tm, D_MODEL), lambda i: (i, 0))
    out, = _call(body, name="norm_cast", grid=(s // tm,),
                 in_specs=[row, pl.BlockSpec((None, 1, D_MODEL), lambda i: (l, 0, 0))], out_specs=[row],
                 out_shape=[jax.ShapeDtypeStruct((s, D_MODEL), BF16)], args=[x, g3])
    return out


def _join_w_in(top_ref, bottom_ref):
    n = top_ref.shape[0]
    return jnp.concatenate([jnp.concatenate([ref[j] for j in range(n)], axis=1) for ref in (top_ref, bottom_ref)],
                           axis=0)


def _in_proj_qkv(h, win, s, tq):
    def body(a_ref, bt_ref, bb_ref, o_ref):
        i = pl.program_id(0)

        @pl.when(i == 0)
        def _():
            o_ref[...] = jnp.zeros(o_ref.shape, BF16)

        @pl.when(i > 0)
        def _():
            o_ref[...] = _dot(a_ref[...], _join_w_in(bt_ref, bb_ref)).astype(BF16)

    half = pl.BlockSpec((4, D_MODEL // 2, PROJ_SHARD), lambda i: (1, 0, 0))
    out, = _call(
        body, name="in_proj_qkv", grid=(s // tq + 1,),
        in_specs=[pl.BlockSpec((tq, D_MODEL), lambda i: (jnp.maximum(i - 1, 0), 0)), half, half],
        out_specs=[pl.BlockSpec((tq, 4 * PROJ_SHARD), lambda i: (i, 0))],
        out_shape=[jax.ShapeDtypeStruct((s + tq, PROJ_WIDTH // 2), BF16)], args=[h, *win])
    return out


def _in_proj_conv(h, win, wc, g3, l, s, tq, comm=None):
    def body(a_ref, bt_ref, bb_ref, wc_ref, g_ref, pc_ref, y_ref, carry_ref, acc_ref):
        i = pl.program_id(0)
        acc_ref[...] = _dot(a_ref[...], _join_w_in(bt_ref, bb_ref))
        pc_ref[...] = acc_ref[...].astype(BF16)
        gmat = _group_matrix()
        for j in range(CONV_WIDTH // 128):
            c0, c1, c2 = 128 * j, CONV_WIDTH + 128 * j, 2 * CONV_WIDTH + 128 * j
            hc = acc_ref[:, c0:c0 + 128]
            bg = acc_ref[:, c1:c1 + 128]
            cg = acc_ref[:, c2:c2 + 128]
            u_prev = jnp.where(i > 0, carry_ref[:, c0:c0 + 128], 0.0)
            u = cg * hc
            carry_ref[:, c0:c0 + 128] = u[tq - 8:tq, :]
            full = jnp.concatenate([u_prev, u], axis=0)
            u1 = pltpu.roll(full, 1, 0)[8:]
            u2 = pltpu.roll(full, 2, 0)[8:]
            out = (u2 * wc_ref[0:1, c0:c0 + 128] + u1 * wc_ref[1:2, c0:c0 + 128]
                   + u * wc_ref[2:3, c0:c0 + 128])
            yc = bg * out
            r = lax.rsqrt(_group_mean(yc * yc, gmat) + EPS)
            y_ref[:, c0:c0 + 128] = (yc * r * g_ref[:, c0:c0 + 128]).astype(BF16)

    half = pl.BlockSpec((4, D_MODEL // 2, PROJ_SHARD), lambda i: (0, 0, 0))
    return _call(
        body, name="in_proj_conv", grid=(s // tq,),
        in_specs=[pl.BlockSpec((tq, D_MODEL), lambda i: (i, 0)), half, half,
                  pl.BlockSpec((None, 8, CONV_WIDTH), lambda i: (l, 0, 0)),
                  pl.BlockSpec((None, 1, CONV_WIDTH), lambda i: (l, 0, 0))],
        out_specs=[pl.BlockSpec((tq, 3 * CONV_WIDTH), lambda i: (i, 0)),
                   pl.BlockSpec((tq, CONV_WIDTH), lambda i: (i, 0))],
        out_shape=[jax.ShapeDtypeStruct((s, 3 * CONV_WIDTH), BF16), jax.ShapeDtypeStruct((s, CONV_WIDTH), BF16)],
        scratch=[pltpu.VMEM((8, CONV_WIDTH), F32), pltpu.VMEM((tq, 3 * CONV_WIDTH), F32)],
        args=[h, *win, wc, g3], comm=comm)


def _toeplitz_source():
    r_i = lax.broadcasted_iota(jnp.int32, (REL_PAD, TOEP), 0)
    m_i = lax.broadcasted_iota(jnp.int32, (REL_PAD, TOEP), 1)
    idx = jnp.clip((K_BAND - 1) - m_i, -REL_CLIP, REL_CLIP) + REL_CLIP
    return jnp.where(r_i == idx, 1.0, 0.0).astype(BF16)


def _bias_build(rbp, comm=None):
    n_layers = rbp.shape[0]

    def body(rb_ref, o_ref, t_ref):
        pmat = _toeplitz_source()
        hi, mid, lo = _split3(rb_ref[...])
        t_ref[...] = _dot(hi, pmat) + _dot(mid, pmat) + _dot(lo, pmat)
        shift = (CHUNK - 1) - lax.broadcasted_iota(jnp.int32, (CHUNK, TOEP), 0)
        kchunk = lax.broadcasted_iota(jnp.int32, (CHUNK, K_BAND), 1) >> 6
        for h in range(N_HEADS):
            b = jnp.broadcast_to(t_ref[pl.ds(h, 1), :], (CHUNK, TOEP))
            for bit in range(6):
                rolled = pltpu.roll(b, TOEP - (1 << bit), 1)
                b = jnp.where(((shift >> bit) & 1) == 1, rolled, b)
            for cq in range(Q_BLOCK // CHUNK):
                off = CHUNK * (Q_BLOCK // CHUNK - 1 - cq)
                band = pltpu.roll(b, TOEP - off, 1) if off else b
                dchunk = kchunk - cq
                in_band = jnp.where(dchunk >= 0, jnp.where(dchunk <= N_LEFT_CHUNKS, 1, 0), 0) == 1
                o_ref[h, CHUNK * cq:CHUNK * (cq + 1), :] = jnp.where(in_band, band[:, :K_BAND], NEG_INF)

    return _call(
        body, name="bias_build", grid=(n_layers,),
        in_specs=[pl.BlockSpec((None, N_HEADS, REL_PAD), lambda l: (l, 0, 0))],
        out_specs=[pl.BlockSpec((None, N_HEADS, Q_BLOCK, K_BAND), lambda l: (l, 0, 0, 0))],
        out_shape=[jax.ShapeDtypeStruct((n_layers, N_HEADS, Q_BLOCK, K_BAND), F32)],
        scratch=[pltpu.VMEM((N_HEADS, TOEP), F32)], args=[rbp], comm=comm)


def _bias_bwd(ds_sum):
    n_layers = ds_sum.shape[0]

    def body(ds_ref, o_ref, t_ref):
        pmat = _toeplitz_source()
        shift = (CHUNK - 1) - lax.broadcasted_iota(jnp.int32, (CHUNK, TOEP), 0)
        for h in range(N_HEADS):
            d = None
            for cq in range(Q_BLOCK // CHUNK):
                off = CHUNK * (Q_BLOCK // CHUNK - 1 - cq)
                part = jnp.concatenate([ds_ref[h, CHUNK * cq:CHUNK * (cq + 1), :],
                                        jnp.zeros((CHUNK, TOEP - K_BAND), F32)], axis=1)
                part = pltpu.roll(part, off, 1) if off else part
                d = part if d is None else d + part
            for bit in range(6):
                rolled = pltpu.roll(d, 1 << bit, 1)
                d = jnp.where(((shift >> bit) & 1) == 1, rolled, d)
            t_ref[pl.ds(h, 1), :] = jnp.sum(d, axis=0, keepdims=True)
        hi, mid, lo = _split3(t_ref[...])
        o_ref[...] = _dot_nt(hi, pmat) + _dot_nt(mid, pmat) + _dot_nt(lo, pmat)

    out, = _call(
        body, name="bias_bwd", grid=(n_layers,),
        in_specs=[pl.BlockSpec((None, N_HEADS, Q_BLOCK, K_BAND), lambda l: (l, 0, 0, 0))],
        out_specs=[pl.BlockSpec((None, N_HEADS, REL_PAD), lambda l: (l, 0, 0))],
        out_shape=[jax.ShapeDtypeStruct((n_layers, N_HEADS, REL_PAD), F32)],
        scratch=[pltpu.VMEM((N_HEADS, TOEP), F32)], args=[ds_sum])
    return out


def _attn_fwd(qkvp, biasm, g3, l, s, pad, comm=None):
    rows = Q_STEP * Q_BLOCK
    nb = s // rows
    qb0 = pad // rows
    scale = HEAD_DIM ** -0.5
    wide = 128 * ATTN_PAIRS

    def body(q_ref, k_ref, v_ref, b_ref, g_ref, o_ref, lse_ref, yn_ref):
        blk = pl.program_id(1)
        lane = lax.broadcasted_iota(jnp.int32, (1, 128), 1)
        gmat = _group_matrix()

        def step(masked):
            for sub in range(Q_STEP):
                rs = slice(Q_BLOCK * sub, Q_BLOCK * (sub + 1))
                start = blk * rows + Q_BLOCK * sub
                koff = pl.multiple_of(start + (pad - LEFT), Q_BLOCK)
                kpos = lax.broadcasted_iota(jnp.int32, (1, K_BAND), 1) + (start - LEFT)
                kmask = jnp.where(kpos >= 0, 0.0, NEG_INF)
                for pr in range(ATTN_PAIRS):
                    ls = slice(128 * pr, 128 * (pr + 1))
                    q = q_ref[rs, ls]
                    kb = k_ref[pl.ds(koff, K_BAND), ls]
                    vb = v_ref[pl.ds(koff, K_BAND), ls]
                    outs, lses = [], []
                    for hh in range(2):
                        in_head = (lane >> 6) == hh
                        qm = jnp.where(in_head, q, jnp.zeros_like(q)) * jnp.asarray(scale, BF16)
                        sc = _dot_nt(qm, kb) + b_ref[2 * pr + hh]
                        if masked:
                            sc = sc + kmask
                        m = jnp.max(sc, axis=1, keepdims=True)
                        e = jnp.exp(sc - m)
                        den = jnp.sum(e, axis=1, keepdims=True)
                        outs.append(_dot(e.astype(BF16), vb) * (1.0 / den))
                        lses.append(m + jnp.log(den))
                    first = lane < HEAD_DIM
                    o = jnp.where(first, outs[0], outs[1])
                    o_ref[rs, ls] = o
                    lse_ref[rs, ls] = jnp.where(first, lses[0], lses[1])
                    r = lax.rsqrt(_group_mean(o * o, gmat) + EPS)
                    yn_ref[rs, ls] = (o * r * g_ref[:, ls]).astype(BF16)

        pl.when(blk == 0)(lambda: step(True))
        pl.when(blk > 0)(lambda: step(False))

    blk_out = pl.BlockSpec((rows, wide), lambda p, b: (b, p))
    n_grp = ATTN_WIDTH // wide
    return _call(
        body, name="attn_fwd", grid=(n_grp, nb),
        in_specs=[pl.BlockSpec((rows, wide), lambda p, b: (qb0 + b, p)),
                  pl.BlockSpec((s + pad, wide), lambda p, b: (0, n_grp + p)),
                  pl.BlockSpec((s + pad, wide), lambda p, b: (0, 2 * n_grp + p)),
                  pl.BlockSpec((None, 2 * ATTN_PAIRS, Q_BLOCK, K_BAND), lambda p, b: (l, p, 0, 0)),
                  pl.BlockSpec((None, 1, wide), lambda p, b: (l, 0, p))],
        out_specs=[blk_out, blk_out, blk_out],
        out_shape=[jax.ShapeDtypeStruct((s, ATTN_WIDTH), F32),
                   jax.ShapeDtypeStruct((s, ATTN_WIDTH), F32),
                   jax.ShapeDtypeStruct((s, ATTN_WIDTH), BF16)],
        args=[qkvp, qkvp, qkvp, biasm, g3], comm=comm)


def _out_proj_fwd(ync, yna, wout, x, g_post3, g_next3, l, s, tm):
    half = D_MODEL // 2

    def body(a1_ref, a2_ref, w_ref, x_ref, gp_ref, gn_ref, z_ref, xm_ref, h_ref):
        for rs in _row_subtiles(tm, SUB_ROWS):
            z = _dot(a1_ref[rs, :], w_ref[0:half, :]) + _dot(a2_ref[rs, :], w_ref[half:D_MODEL, :])
            z_ref[rs, :] = z.astype(BF16)
            xm = x_ref[rs, :] + z * _rstd(z) * gp_ref[...]
            xm_ref[rs, :] = xm
            h_ref[rs, :] = (xm * _rstd(xm) * gn_ref[...]).astype(BF16)

    row = pl.BlockSpec((tm, D_MODEL), lambda i: (i, 0))
    gain = pl.BlockSpec((None, 1, D_MODEL), lambda i: (l, 0, 0))
    return _call(
        body, name="out_proj_fwd", grid=(s // tm,),
        in_specs=[pl.BlockSpec((tm, half), lambda i: (i, 0)), pl.BlockSpec((tm, half), lambda i: (i, 0)),
                  pl.BlockSpec((D_MODEL, D_MODEL), lambda i: (0, 0)), row, gain, gain],
        out_specs=[row, row, row],
        out_shape=[jax.ShapeDtypeStruct((s, D_MODEL), BF16), jax.ShapeDtypeStruct((s, D_MODEL), F32),
                   jax.ShapeDtypeStruct((s, D_MODEL), BF16)],
        args=[ync, yna, wout, x, g_post3, g_next3])


def _ffn_in_fwd(h2, wfin4, s, tm, comm=None):
    def body(h_ref, wg_ref, wu_ref, gu_ref, act_ref):
        h = h_ref[...]
        for cs in _COL_SUBTILES:
            gate = _dot_nt(h, wg_ref[cs, :])
            up = _dot_nt(h, wu_ref[cs, :])
            gu_ref[0, :, cs] = gate.astype(BF16)
            gu_ref[1, :, cs] = up.astype(BF16)
            act_ref[:, cs] = (gate * (1.0 / (1.0 + jnp.exp(-gate))) * up).astype(BF16)

    return _call(
        body, name="ffn_in_fwd", grid=(2, s // tm),
        in_specs=[pl.BlockSpec((tm, D_MODEL), lambda b, i: (i, 0)),
                  pl.BlockSpec((None, FF_PAIR, D_MODEL), lambda b, i: (b, 0, 0)),
                  pl.BlockSpec((None, FF_PAIR, D_MODEL), lambda b, i: (2 + b, 0, 0))],
        out_specs=[pl.BlockSpec((2, tm, FF_PAIR), lambda b, i: (0, i, b)),
                   pl.BlockSpec((tm, FF_PAIR), lambda b, i: (i, b))],
        out_shape=[jax.ShapeDtypeStruct((2, s, D_FF), BF16), jax.ShapeDtypeStruct((s, D_FF), BF16)],
        args=[h2, wfin4, wfin4], comm=comm)


def _ffn_out_fwd(act, wfo, xm, g_post3, g_next3, l, l_next, s, tm, comm=None):
    def body(a_ref, w_ref, x_ref, gp_ref, gn_ref, f_ref, xo_ref, h_ref):
        for rs in _row_subtiles(tm, SUB_ROWS):
            f = _dot(a_ref[rs, :], w_ref[...])
            f_ref[rs, :] = f.astype(BF16)
            xo = x_ref[rs, :] + f * _rstd(f) * gp_ref[...]
            xo_ref[rs, :] = xo
            h_ref[rs, :] = (xo * _rstd(xo) * gn_ref[...]).astype(BF16)

    row = pl.BlockSpec((tm, D_MODEL), lambda i: (i, 0))
    return _call(
        body, name="ffn_out_fwd", grid=(s // tm,),
        in_specs=[pl.BlockSpec((tm, D_FF), lambda i: (i, 0)),
                  pl.BlockSpec((D_FF, D_MODEL), lambda i: (0, 0), pipeline_mode=pl.Buffered(1)), row,
                  pl.BlockSpec((None, 1, D_MODEL), lambda i: (l, 0, 0)),
                  pl.BlockSpec((None, 1, D_MODEL), lambda i: (l_next, 0, 0))],
        out_specs=[row, row, row],
        out_shape=[jax.ShapeDtypeStruct((s, D_MODEL), BF16), jax.ShapeDtypeStruct((s, D_MODEL), F32),
                   jax.ShapeDtypeStruct((s, D_MODEL), BF16)],
        args=[act, wfo, xm, g_post3, g_next3], comm=comm)


def _ffn_out_loss(act, wfo, xm, g_post3, target, l, s, tm):
    def body(a_ref, w_ref, x_ref, gp_ref, t_ref, dx_ref, sq_ref, df_ref, dg_ref):
        _zero_first((sq_ref, dg_ref), pl.program_id(0) == 0)
        for rs in _row_subtiles(tm, SUB_ROWS):
            f = _dot(a_ref[rs, :], w_ref[...])
            gain = gp_ref[...]
            err = x_ref[rs, :] + f * _rstd(f) * gain - t_ref[rs, :]
            dx = err * (1.0 / D_MODEL)
            dx_ref[rs, :] = dx
            df, dyn = _norm_bwd_rows(f, gain, dx)
            df_ref[rs, :] = df.astype(BF16)
            _add_cols(dg_ref, dyn)
            cs = jnp.sum(err * err, axis=0, keepdims=True)
            part = cs[:, 0:128]
            for k in range(1, D_MODEL // 128):
                part = part + cs[:, 128 * k:128 * (k + 1)]
            sq_ref[0:1, :] += part

    row = pl.BlockSpec((tm, D_MODEL), lambda i: (i, 0))
    return _call(
        body, name="ffn_out_loss", grid=(s // tm,),
        in_specs=[pl.BlockSpec((tm, D_FF), lambda i: (i, 0)),
                  pl.BlockSpec((D_FF, D_MODEL), lambda i: (0, 0), pipeline_mode=pl.Buffered(1)), row,
                  pl.BlockSpec((None, 1, D_MODEL), lambda i: (l, 0, 0)), row],
        out_specs=[row, pl.BlockSpec((8, 128), lambda i: (0, 0)), row, pl.BlockSpec((8, D_MODEL), lambda i: (0, 0))],
        out_shape=[jax.ShapeDtypeStruct((s, D_MODEL), F32), jax.ShapeDtypeStruct((8, 128), F32),
                   jax.ShapeDtypeStruct((s, D_MODEL), BF16), jax.ShapeDtypeStruct((8, D_MODEL), F32)],
        args=[act, wfo, xm, g_post3, target])


def _norm_bwd_rows(v, g, dy):
    r = _rstd(v)
    vn = v * r
    gd = dy * g
    dv = r * (gd - vn * jnp.mean(vn * gd, axis=-1, keepdims=True))
    return dv, dy * vn


def _zero_first(refs, first):
    @pl.when(first)
    def _():
        for ref in refs:
            ref[...] = jnp.zeros(ref.shape, F32)


def _add_cols(ref, val):
    ref[0:1, :] += jnp.sum(val, axis=0, keepdims=True)


def _row_subtiles(rows, sub):
    sub = min(sub, rows)
    return [slice(r, r + sub) for r in range(0, rows, sub)]


def _ffn_out_bwd(df, wfo, gu, act, s, tm, comm=None):
    nm = s // tm

    def body(df_ref, w_ref, gu_ref, act_ref, dgu_ref, dw_ref, acc_ref):
        i = pl.program_id(1)
        df = df_ref[...]
        _zero_first((acc_ref,), i == 0)
        acc_ref[...] += _dot_tn(act_ref[...], df)

        @pl.when(i == nm - 1)
        def _():
            dw_ref[...] = acc_ref[...].astype(BF16)

        for cs in _COL_SUBTILES:
            da = _dot_nt(df, w_ref[cs, :])
            g = gu_ref[0, :, cs].astype(F32)
            u = gu_ref[1, :, cs].astype(F32)
            sg = 1.0 / (1.0 + jnp.exp(-g))
            dgu_ref[0, :, cs] = (da * u * (sg * (1.0 + g * (1.0 - sg)))).astype(BF16)
            dgu_ref[1, :, cs] = (da * (g * sg)).astype(BF16)

    blk = pl.BlockSpec((2, tm, FF_PAIR), lambda b, i: (0, i, b))
    wblk = pl.BlockSpec((FF_PAIR, D_MODEL), lambda b, i: (b, 0))
    return _call(
        body, name="ffn_out_bwd", grid=(2, nm),
        in_specs=[pl.BlockSpec((tm, D_MODEL), lambda b, i: (i, 0)),
                  pl.BlockSpec((FF_PAIR, D_MODEL), lambda b, i: (b, 0), pipeline_mode=pl.Buffered(1)), blk,
                  pl.BlockSpec((tm, FF_PAIR), lambda b, i: (i, b))],
        out_specs=[blk, wblk],
        out_shape=[jax.ShapeDtypeStruct((2, s, D_FF), BF16), jax.ShapeDtypeStruct((D_FF, D_MODEL), BF16)],
        scratch=[pltpu.VMEM((FF_PAIR, D_MODEL), F32)],
        args=[df, wfo, gu, act], comm=comm)


def _dw_ffn_in(h2, dgu, s):
    def body(a_ref, b_ref, o_ref):
        o_ref[...] = _dot_tn(b_ref[...], a_ref[...]).astype(BF16)

    out, = _call(
        body, name="dw_ffn_in", grid=(4,),
        in_specs=[pl.BlockSpec((s, D_MODEL), lambda n: (0, 0), pipeline_mode=pl.Buffered(1)),
                  pl.BlockSpec((None, s, FF_PAIR), lambda n: (n // 2, 0, n % 2))],
        out_specs=[pl.BlockSpec((None, FF_PAIR, D_MODEL), lambda n: (n, 0, 0))],
        out_shape=[jax.ShapeDtypeStruct((4, FF_PAIR, D_MODEL), BF16)], args=[h2, dgu])
    return out


def _ffn_in_bwd(dgu, wfin, xm, g_pre3, dres, z, g_post3, l, s, tm, comm=None):
    def body(d_ref, w_ref, xm_ref, gp_ref, dres_ref, z_ref, gq_ref, dxm_ref, dz_ref, dgp_ref, dgq_ref):
        _zero_first((dgp_ref, dgq_ref), pl.program_id(0) == 0)
        for rs in _row_subtiles(tm, SUB_ROWS):
            dh = _dot(d_ref[0, rs, :], w_ref[0:D_FF, :]) + _dot(d_ref[1, rs, :], w_ref[D_FF:2 * D_FF, :])
            dx, dyn = _norm_bwd_rows(xm_ref[rs, :], gp_ref[...], dh)
            dxm = dres_ref[rs, :] + dx
            dxm_ref[rs, :] = dxm
            _add_cols(dgp_ref, dyn)
            dz, dyn2 = _norm_bwd_rows(z_ref[rs, :].astype(F32), gq_ref[...], dxm)
            dz_ref[rs, :] = dz.astype(BF16)
            _add_cols(dgq_ref, dyn2)

    row = pl.BlockSpec((tm, D_MODEL), lambda i: (i, 0))
    gain = pl.BlockSpec((None, 1, D_MODEL), lambda i: (l, 0, 0))
    dgs = pl.BlockSpec((8, D_MODEL), lambda i: (0, 0))
    return _call(
        body, name="ffn_in_bwd", grid=(s // tm,),
        in_specs=[pl.BlockSpec((2, tm, D_FF), lambda i: (0, i, 0)),
                  pl.BlockSpec((2 * D_FF, D_MODEL), lambda i: (0, 0), pipeline_mode=pl.Buffered(1)),
                  row, gain, row, row, gain],
        out_specs=[row, row, dgs, dgs],
        out_shape=[jax.ShapeDtypeStruct((s, D_MODEL), F32), jax.ShapeDtypeStruct((s, D_MODEL), BF16),
                   jax.ShapeDtypeStruct((8, D_MODEL), F32), jax.ShapeDtypeStruct((8, D_MODEL), F32)],
        args=[dgu, wfin, xm, g_pre3, dres, z, g_post3], comm=comm)


def _out_proj_bwd(dz, wout, o, g3, ync, yna, l, s, tm):
    nm = s // tm
    half = D_MODEL // 2

    def body(dz_ref, w_ref, o_ref, g_ref, a1_ref, a2_ref, dyc_ref, do_ref, dg_ref, dw_ref, acc_ref):
        i = pl.program_id(0)
        gmat = _group_matrix()
        _zero_first((dg_ref, acc_ref), i == 0)
        dzv = dz_ref[...]
        acc_ref[0:half, :] += _dot_tn(a1_ref[...], dzv)
        acc_ref[half:D_MODEL, :] += _dot_tn(a2_ref[...], dzv)

        @pl.when(i == nm - 1)
        def _():
            dw_ref[...] = acc_ref[...].astype(BF16)

        for rs in _row_subtiles(tm, SUB_ROWS):
            dy = _dot_nt(dz_ref[rs, :], w_ref[...])
            dyc_ref[rs, :] = dy[:, 0:CONV_WIDTH]
            for j in range(ATTN_WIDTH // 128):
                c0 = 128 * j
                ov = o_ref[rs, c0:c0 + 128]
                dyn = dy[:, CONV_WIDTH + c0:CONV_WIDTH + c0 + 128]
                r = lax.rsqrt(_group_mean(ov * ov, gmat) + EPS)
                on = ov * r
                gd = dyn * g_ref[:, c0:c0 + 128]
                do_ref[rs, c0:c0 + 128] = r * (gd - on * _group_mean(on * gd, gmat))
                dg_ref[0:1, c0:c0 + 128] += jnp.sum(dyn * on, axis=0, keepdims=True)

    halfrow = pl.BlockSpec((tm, ATTN_WIDTH), lambda i: (i, 0))
    return _call(
        body, name="out_proj_bwd", grid=(nm,),
        in_specs=[pl.BlockSpec((tm, D_MODEL), lambda i: (i, 0)),
                  pl.BlockSpec((D_MODEL, D_MODEL), lambda i: (0, 0)), halfrow,
                  pl.BlockSpec((None, 1, ATTN_WIDTH), lambda i: (l, 0, 0)), halfrow, halfrow],
        out_specs=[halfrow, halfrow, pl.BlockSpec((8, ATTN_WIDTH), lambda i: (0, 0)),
                   pl.BlockSpec((D_MODEL, D_MODEL), lambda i: (0, 0))],
        out_shape=[jax.ShapeDtypeStruct((s, CONV_WIDTH), F32), jax.ShapeDtypeStruct((s, ATTN_WIDTH), F32),
                   jax.ShapeDtypeStruct((8, ATTN_WIDTH), F32), jax.ShapeDtypeStruct((D_MODEL, D_MODEL), BF16)],
        scratch=[pltpu.VMEM((D_MODEL, D_MODEL), F32)],
        args=[dz, wout, o, g3, ync, yna])


def _conv_bwd(pc, dyc, wc, g3, dq, dk, dv, l, s, tr):
    hb = tr // 8
    nt = s // tr
    ext = tr + 16
    last_hb = s // 8 - 1

    def body(pc_ref, prev_ref, next_ref, dy_ref, dyn_ref, wc_ref, g_ref, dq_ref, dk_ref, dv_ref,
             dpc_ref, dw_ref, dg_ref):
        i = pl.program_id(0)
        for part, ref in enumerate((dq_ref, dk_ref, dv_ref)):
            c = 3 * CONV_WIDTH + ATTN_WIDTH * part
            dpc_ref[:, c:c + ATTN_WIDTH] = ref[...]
        gmat = _group_matrix()
        row = lax.broadcasted_iota(jnp.int32, (ext, 128), 0) + (i * tr - 8)
        inside = jnp.where(row >= 0, jnp.where(row < s, 1, 0), 0) == 1

        @pl.when(i == 0)
        def _():
            dw_ref[...] = jnp.zeros(dw_ref.shape, F32)
            dg_ref[...] = jnp.zeros(dg_ref.shape, F32)

        def extend(ref_prev, ref_mid, ref_next, c):
            if ref_prev is None:
                before = jnp.zeros((8, 128), F32)
            else:
                before = ref_prev[:, c:c + 128].astype(F32)[ref_prev.shape[0] - 8:]
            after = ref_next[:, c:c + 128].astype(F32)[0:8]
            return jnp.concatenate([before, ref_mid[:, c:c + 128].astype(F32), after], axis=0)

        for j in range(CONV_WIDTH // 128):
            c0, c1, c2 = 128 * j, CONV_WIDTH + 128 * j, 2 * CONV_WIDTH + 128 * j
            hc = extend(prev_ref, pc_ref, next_ref, c0)
            bg = extend(prev_ref, pc_ref, next_ref, c1)
            cg = extend(prev_ref, pc_ref, next_ref, c2)
            dyn = extend(None, dy_ref, dyn_ref, c0)
            w0, w1, w2 = (wc_ref[0:1, c0:c0 + 128], wc_ref[1:2, c0:c0 + 128], wc_ref[2:3, c0:c0 + 128])
            gain = g_ref[:, c0:c0 + 128]
            u = jnp.where(inside, cg * hc, 0.0)
            u1 = pltpu.roll(u, 1, 0)
            u2 = pltpu.roll(u, 2, 0)
            out = u2 * w0 + u1 * w1 + u * w2
            yc = bg * out
            r = lax.rsqrt(_group_mean(yc * yc, gmat) + EPS)
            ycn = yc * r
            gd = dyn * gain
            dyc = r * (gd - ycn * _group_mean(ycn * gd, gmat))
            dout = jnp.where(inside, dyc * bg, 0.0)
            du = dout * w2 + pltpu.roll(dout, ext - 1, 0) * w1 + pltpu.roll(dout, ext - 2, 0) * w0
            sl = slice(8, 8 + tr)
            dpc_ref[:, c0:c0 + 128] = (du[sl] * cg[sl]).astype(BF16)
            dpc_ref[:, c1:c1 + 128] = (dyc[sl] * out[sl]).astype(BF16)
            dpc_ref[:, c2:c2 + 128] = (du[sl] * hc[sl]).astype(BF16)
            dw_ref[0:1, c0:c0 + 128] += jnp.sum(dout[sl] * u2[sl], axis=0, keepdims=True)
            dw_ref[1:2, c0:c0 + 128] += jnp.sum(dout[sl] * u1[sl], axis=0, keepdims=True)
            dw_ref[2:3, c0:c0 + 128] += jnp.sum(dout[sl] * u[sl], axis=0, keepdims=True)
            dg_ref[0:1, c0:c0 + 128] += jnp.sum(dyn[sl] * ycn[sl], axis=0, keepdims=True)

    wide = 3 * CONV_WIDTH
    return _call(
        body, name="conv_bwd", grid=(nt,),
        in_specs=[pl.BlockSpec((tr, wide), lambda i: (i, 0)),
                  pl.BlockSpec((16, wide), lambda i: (jnp.maximum(i * (hb // 2) - 1, 0), 0)),
                  pl.BlockSpec((16, wide), lambda i: (jnp.minimum((i + 1) * (hb // 2), last_hb // 2), 0)),
                  pl.BlockSpec((tr, CONV_WIDTH), lambda i: (i, 0)),
                  pl.BlockSpec((8, CONV_WIDTH), lambda i: (jnp.minimum((i + 1) * hb, last_hb), 0)),
                  pl.BlockSpec((None, 8, CONV_WIDTH), lambda i: (l, 0, 0)),
                  pl.BlockSpec((None, 1, CONV_WIDTH), lambda i: (l, 0, 0)),
                  pl.BlockSpec((tr, ATTN_WIDTH), lambda i: (i, 0)),
                  pl.BlockSpec((tr, ATTN_WIDTH), lambda i: (i, 0)),
                  pl.BlockSpec((tr, ATTN_WIDTH), lambda i: (i, 0))],
        out_specs=[pl.BlockSpec((tr, PROJ_WIDTH), lambda i: (i, 0)),
                   pl.BlockSpec((8, CONV_WIDTH), lambda i: (0, 0)),
                   pl.BlockSpec((8, CONV_WIDTH), lambda i: (0, 0))],
        out_shape=[jax.ShapeDtypeStruct((s, PROJ_WIDTH), BF16), jax.ShapeDtypeStruct((8, CONV_WIDTH), F32),
                   jax.ShapeDtypeStruct((8, CONV_WIDTH), F32)],
        args=[pc, pc, pc, dyc, dyc, wc, g3, dq, dk, dv])


def _attn_bwd(qkvp, biasm, o, lse, do, l, s, pad, comm=None):
    rows = Q_STEP * Q_BLOCK
    nb = s // rows
    qb0 = pad // rows
    scale = HEAD_DIM ** -0.5
    wide = 128 * ATTN_PAIRS

    def body(q_ref, k_ref, v_ref, b_ref, o_ref, lse_ref, do_ref,
             dq_ref, dk_ref, dv_ref, ds_ref, dk_acc, dv_acc):
        blk = pl.program_id(1)

        @pl.when(blk == 0)
        def _():
            dk_acc[...] = jnp.zeros(dk_acc.shape, F32)
            dv_acc[...] = jnp.zeros(dv_acc.shape, F32)
            ds_ref[...] = jnp.zeros(ds_ref.shape, F32)

        lane = lax.broadcasted_iota(jnp.int32, (1, 128), 1)

        def step(masked):
            for sub in range(Q_STEP):
                rs = slice(Q_BLOCK * sub, Q_BLOCK * (sub + 1))
                start = blk * rows + Q_BLOCK * sub
                koff = pl.multiple_of(start + (pad - LEFT), Q_BLOCK)
                kpos = lax.broadcasted_iota(jnp.int32, (1, K_BAND), 1) + (start - LEFT)
                kmask = jnp.where(kpos >= 0, 0.0, NEG_INF)
                for pr in range(ATTN_PAIRS):
                    ls = slice(128 * pr, 128 * (pr + 1))
                    q = q_ref[rs, ls]
                    kb = k_ref[pl.ds(koff, K_BAND), ls]
                    vb = v_ref[pl.ds(koff, K_BAND), ls]
                    dov = do_ref[rs, ls]
                    lse_v = lse_ref[rs, ls]
                    prod = dov * o_ref[rs, ls]
                    dq_parts = []
                    dk_new = jnp.zeros((128, K_BAND), F32)
                    dv_new = jnp.zeros((128, K_BAND), F32)
                    for hh in range(2):
                        in_head = (lane >> 6) == hh
                        qm = jnp.where(in_head, q, jnp.zeros_like(q)) * jnp.asarray(scale, BF16)
                        dom = jnp.where(in_head, dov, 0.0).astype(BF16)
                        delta = jnp.sum(jnp.where(in_head, prod, 0.0), axis=1, keepdims=True)
                        lse_h = lse_v[:, HEAD_DIM * hh:HEAD_DIM * hh + 1]
                        sc = _dot_nt(qm, kb) + b_ref[2 * pr + hh]
                        if masked:
                            sc = sc + kmask
                        p = jnp.exp(sc - lse_h)
                        dp = _dot_nt(dom, vb)
                        ds = p * (dp - delta)
                        ds_ref[2 * pr + hh] += ds
                        dsb = ds.astype(BF16)
                        dq_parts.append(_dot(dsb, kb) * scale)
                        dk_new = dk_new + _dot_tn(qm, dsb)
                        dv_new = dv_new + _dot_tn(dom, p.astype(BF16))
                    dq_ref[rs, ls] = jnp.where(lane < HEAD_DIM, dq_parts[0], dq_parts[1]).astype(BF16)
                    dk_acc[pr, :, pl.ds(koff, K_BAND)] += dk_new
                    dv_acc[pr, :, pl.ds(koff, K_BAND)] += dv_new

        pl.when(blk == 0)(lambda: step(True))
        pl.when(blk > 0)(lambda: step(False))

        @pl.when(blk == nb - 1)
        def _():
            for pr in range(ATTN_PAIRS):
                ls = slice(128 * pr, 128 * (pr + 1))
                dk_ref[:, ls] = dk_acc[pr, :, pad:pad + s].T.astype(BF16)
                dv_ref[:, ls] = dv_acc[pr, :, pad:pad + s].T.astype(BF16)

    n_grp = ATTN_WIDTH // wide
    qblk = pl.BlockSpec((rows, wide), lambda p, b: (b, p))
    col = pl.BlockSpec((s, wide), lambda p, b: (0, p))
    shp = jax.ShapeDtypeStruct((s, ATTN_WIDTH), BF16)
    return _call(
        body, name="attn_bwd", grid=(n_grp, nb),
        in_specs=[pl.BlockSpec((rows, wide), lambda p, b: (qb0 + b, p)),
                  pl.BlockSpec((s + pad, wide), lambda p, b: (0, n_grp + p)),
                  pl.BlockSpec((s + pad, wide), lambda p, b: (0, 2 * n_grp + p)),
                  pl.BlockSpec((None, 2 * ATTN_PAIRS, Q_BLOCK, K_BAND), lambda p, b: (l, p, 0, 0)),
                  qblk, qblk, qblk],
        out_specs=[qblk, col, col, pl.BlockSpec((2 * ATTN_PAIRS, Q_BLOCK, K_BAND), lambda p, b: (p, 0, 0))],
        out_shape=[shp, shp, shp, jax.ShapeDtypeStruct((N_HEADS, Q_BLOCK, K_BAND), F32)],
        scratch=[pltpu.VMEM((ATTN_PAIRS, 128, s + pad), F32), pltpu.VMEM((ATTN_PAIRS, 128, s + pad), F32)],
        args=[qkvp, qkvp, qkvp, biasm, o, lse, do], comm=comm)


def _dw_in(h, dproj, s):
    def body(a_ref, b_ref, o_ref):
        acc = _dot_tn(a_ref[...], b_ref[...])
        o_ref[0] = acc[:, 0:PROJ_SHARD].astype(BF16)
        o_ref[1] = acc[:, PROJ_SHARD:2 * PROJ_SHARD].astype(BF16)

    out, = _call(
        body, name="dw_in", grid=(4,),
        in_specs=[pl.BlockSpec((s, D_MODEL), lambda n: (0, 0)),
                  pl.BlockSpec((s, 2 * PROJ_SHARD), lambda n: (0, n))],
        out_specs=[pl.BlockSpec((2, D_MODEL, PROJ_SHARD), lambda n: (n, 0, 0))],
        out_shape=[jax.ShapeDtypeStruct((N_DEV, D_MODEL, PROJ_SHARD), BF16)], args=[h, dproj])
    return out


def _in_proj_bwd(dproj, win, x, g3, dres, l, s, tm, f_prev=None, g_post3=None, comm=None):
    chain = f_prev is not None

    def body(d_ref, wt_ref, wb_ref, x_ref, g_ref, dres_ref, *rest):
        if chain:
            f_ref, gq_ref, dx_ref, dg_ref, df_ref, dgq_ref = rest
            _zero_first((dg_ref, dgq_ref), pl.program_id(0) == 0)
        else:
            dx_ref, dg_ref = rest
            _zero_first((dg_ref,), pl.program_id(0) == 0)
        w = _join_w_in(wt_ref, wb_ref)
        for rs in _row_subtiles(tm, SUB_ROWS):
            dh = _dot_nt(d_ref[rs, :], w)
            dx, dyn = _norm_bwd_rows(x_ref[rs, :], g_ref[...], dh)
            dx = dres_ref[rs, :] + dx
            dx_ref[rs, :] = dx
            _add_cols(dg_ref, dyn)
            if chain:
                df, dyn2 = _norm_bwd_rows(f_ref[rs, :].astype(F32), gq_ref[...], dx)
                df_ref[rs, :] = df.astype(BF16)
                _add_cols(dgq_ref, dyn2)

    row = pl.BlockSpec((tm, D_MODEL), lambda i: (i, 0))
    dgs = pl.BlockSpec((8, D_MODEL), lambda i: (0, 0))
    half = pl.BlockSpec((N_DEV, D_MODEL // 2, PROJ_SHARD), lambda i: (0, 0, 0), pipeline_mode=pl.Buffered(1))
    in_specs = [pl.BlockSpec((tm, PROJ_WIDTH), lambda i: (i, 0)), half, half,
                row, pl.BlockSpec((None, 1, D_MODEL), lambda i: (l, 0, 0)), row]
    out_specs = [row, dgs]
    out_shape = [jax.ShapeDtypeStruct((s, D_MODEL), F32), jax.ShapeDtypeStruct((8, D_MODEL), F32)]
    args = [dproj, *win, x, g3, dres]
    if chain:
        in_specs += [row, pl.BlockSpec((None, 1, D_MODEL), lambda i: (l - 1, 0, 0))]
        out_specs += [row, dgs]
        out_shape += [jax.ShapeDtypeStruct((s, D_MODEL), BF16), jax.ShapeDtypeStruct((8, D_MODEL), F32)]
        args += [f_prev, g_post3]
    return _call(body, name="in_proj_bwd", grid=(s // tm,), in_specs=in_specs, out_specs=out_specs,
                 out_shape=out_shape, args=args, comm=comm)


def _adamw(name, w, m, v, lands, owns=None, me=None):
    groups, rows, cols = w.shape
    assert len(lands) == groups
    n_part = lands[0].shape[0]
    tr = _row_tile(rows, tuple(c for c in (512, 352, 256, 176, 128, 64, 32, 16, 8) if c * cols <= 256 * 1024))
    c1 = 1.0 - ADAM_B1 ** ADAM_STEP
    c2 = 1.0 - ADAM_B2 ** ADAM_STEP
    n_own = groups if owns is not None else 0

    def body(*refs):
        if n_own:
            me_ref, refs = refs[0], refs[1:]
        w_ref, m_ref, v_ref = refs[:3]
        land_refs = refs[3:3 + groups]
        own_refs = refs[3 + groups:3 + groups + n_own]
        g_ref, d_ref, nm_ref, nv_ref = refs[3 + groups + n_own:]
        grp = pl.program_id(0)
        for gi in range(groups):
            @pl.when(grp == gi)
            def _():
                l_ref = land_refs[gi]
                g = None
                for p in range(n_part):
                    part = l_ref[p].astype(F32)
                    if n_own:
                        part = jnp.where(me_ref[0] == p, own_refs[gi][...].astype(F32), part)
                    g = part if g is None else g + part
                g_ref[...] = g
                m1 = ADAM_B1 * m_ref[...] + (1.0 - ADAM_B1) * g
                v1 = ADAM_B2 * v_ref[...] + (1.0 - ADAM_B2) * (g * g)
                nm_ref[...] = m1
                nv_ref[...] = v1
                d_ref[...] = -ADAM_LR * ((m1 / c1) / (jnp.sqrt(v1 / c2) + ADAM_EPS) + ADAM_WD * w_ref[...])

    blk = pl.BlockSpec((None, tr, cols), lambda g, i, *_: (g, i, 0))
    shp = jax.ShapeDtypeStruct((groups, rows, cols), F32)

    def land_spec(gi):
        return pl.BlockSpec((n_part, tr, cols), lambda g, i, *_: (0, jnp.where(g == gi, i, 0), 0))

    def own_spec(gi):
        if owns[gi].ndim == 3:
            return pl.BlockSpec((None, tr, cols), lambda g, i, me_ref: (me_ref[0], jnp.where(g == gi, i, 0), 0))
        return pl.BlockSpec((tr, cols), lambda g, i, me_ref: (jnp.where(g == gi, i, 0), 0))

    in_specs = [blk, blk, blk] + [land_spec(gi) for gi in range(groups)] + [own_spec(gi) for gi in range(n_own)]
    args = [w, m, v] + list(lands) + (list(owns) if n_own else [])
    if not n_own:
        return _call(body, name=name, grid=(groups, rows // tr), in_specs=in_specs,
                     out_specs=[blk, blk, blk, blk], out_shape=[shp, shp, shp, shp], args=args)
    return pl.pallas_call(
        body, name=name,
        grid_spec=pltpu.PrefetchScalarGridSpec(
            num_scalar_prefetch=1, grid=(groups, rows // tr), in_specs=in_specs, out_specs=[blk, blk, blk, blk]),
        out_shape=[shp, shp, shp, shp],
        compiler_params=pltpu.CompilerParams(dimension_semantics=("arbitrary", "arbitrary"),
                                             vmem_limit_bytes=VMEM_LIMIT),
    )(me, *args)


def _pack_small(rel, gco, gao, gpm, gqm, gpf, gqf):
    n_layers = rel.shape[0]
    relp = jnp.pad(rel, ((0, 0), (0, 0), (0, REL_PAD - rel.shape[2])))
    parts = [relp.reshape(n_layers * N_HEADS * REL_PAD // 128, 128)]
    parts += [a.reshape(-1, 128) for a in (gco, gao, gpm, gqm, gpf, gqf)]
    return jnp.concatenate(parts, axis=0)


def _pack_small_grads(d_rel, parts):
    n_layers = len(d_rel)
    keys = ("gco", "gao", "gpm", "gqm", "gpf", "gqf")
    arrays = list(d_rel) + [parts[k][l] for k in keys for l in range(n_layers)] + list(parts["wc"])
    rows = 0
    plan = []
    for l in range(n_layers):
        for h in range(N_HEADS):
            for t in range(REL_PAD // 128):
                plan.append((l, (0, h), t, rows))
                rows += 1
    for ki, k in enumerate(keys):
        for l in range(n_layers):
            for t in range(parts[k][l].shape[1] // 128):
                plan.append((n_layers * (1 + ki) + l, (0,), t, rows))
                rows += 1
    for l in range(n_layers):
        for tap in range(3):
            for t in range(CONV_WIDTH // 128):
                plan.append((n_layers * (1 + len(keys)) + l, (tap,), t, rows))
                rows += 1
    total = rows + (-rows) % 8

    def body(*refs):
        o_ref = refs[-1]
        if total > rows:
            o_ref[rows:total, :] = jnp.zeros((total - rows, 128), F32)
        for op, idx, t, dst in plan:
            lanes = slice(128 * t, 128 * (t + 1))
            if len(idx) == 2:
                o_ref[dst:dst + 1, :] = refs[op][idx[0], idx[1]:idx[1] + 1, lanes]
            else:
                o_ref[dst:dst + 1, :] = refs[op][idx[0]:idx[0] + 1, lanes]

    vmem = pl.BlockSpec(memory_space=pltpu.VMEM)
    out, = _call(body, name="pack_small_grads", grid=(), in_specs=[vmem] * len(arrays), out_specs=[vmem],
                 out_shape=[jax.ShapeDtypeStruct((total, 128), F32)], args=arrays)
    return out


def _unpack_small(p, n_layers):
    n_rel = n_layers * N_HEADS * REL_PAD // 128
    rel = p[:n_rel].reshape(n_layers, N_HEADS, REL_PAD)[:, :, :2 * REL_CLIP + 1]
    outs = [rel]
    r0 = n_rel
    for width in (CONV_WIDTH, ATTN_WIDTH, D_MODEL, D_MODEL, D_MODEL, D_MODEL):
        nr = n_layers * width // 128
        outs.append(p[r0:r0 + nr].reshape(n_layers, width))
        r0 += nr
    return outs


def kernel(x, w_in, w_conv, rel_bias, g_conv_out, g_attn_out, w_out, g_pre_mix, g_post_mix, g_pre_ffn, g_post_ffn, w_ffn_in, w_ffn_out, loss_target, m_w_in, m_w_conv, m_rel_bias, m_g_conv_out, m_g_attn_out, m_w_out, m_g_pre_mix, m_g_post_mix, m_g_pre_ffn, m_g_post_ffn, m_w_ffn_in, m_w_ffn_out, v_w_in, v_w_conv, v_rel_bias, v_g_conv_out, v_g_attn_out, v_w_out, v_g_pre_mix, v_g_post_mix, v_g_pre_ffn, v_g_post_ffn, v_w_ffn_in, v_w_ffn_out):
    n_layers = w_in.shape[0]
    s = x.shape[1]
    assert x.shape == (1, s, D_MODEL) and s % 1024 == 0
    assert w_in.shape == (n_layers, D_MODEL, PROJ_SHARD) and w_ffn_in.shape == (n_layers, D_MODEL, FF_SHARD)
    tm = 512
    tq = 1024 if s >= 2048 else 512
    tf = min(1024, s)
    x0 = x.reshape(s, D_MODEL)
    target = loss_target.reshape(s, D_MODEL)
    dev = _dev_index(lax.axis_index("x"), lax.axis_index("y"), lax.axis_index("c"))

    wt_ffn_in, mt_ffn_in, vt_ffn_in = (jnp.transpose(a, (0, 2, 1)) for a in (w_ffn_in, m_w_ffn_in, v_w_ffn_in))
    local_w = [_cast_bf16(w_in, "cast_w_in"), _cast_bf16(w_out, "cast_w_out"),
               _cast_bf16(wt_ffn_in, "cast_w_ffn_in"), _cast_bf16(w_ffn_out, "cast_w_ffn_out")]
    wc_local = jnp.pad(jnp.transpose(w_conv, (0, 2, 1)).reshape(-1), (0, 1024 - n_layers * 3 * 64)).reshape(8, 128)
    win_halves = local_w[0].reshape(2 * n_layers, D_MODEL // 2, PROJ_SHARD)
    biasm, win_top, win_bottom, wc_g = _bias_build(
        jnp.pad(rel_bias, ((0, 0), (0, 0), (0, REL_PAD - rel_bias.shape[2]))),
        comm=_Gather([(win_halves, 0), (win_halves, 1), (wc_local, None)]))
    weights = [None] * n_layers
    wc_full = wc_g.reshape(N_DEV, 1024)[:, :n_layers * 3 * 64].reshape(N_DEV, n_layers, 3, 64)
    wc_full = jnp.transpose(wc_full, (1, 2, 0, 3)).reshape(n_layers, 3, CONV_WIDTH)
    wc_full = jnp.pad(wc_full, ((0, 0), (0, 5), (0, 0)))

    g3 = {k: v.reshape(n_layers, 1, -1) for k, v in dict(
        conv=g_conv_out, attn=g_attn_out, pre_mix=g_pre_mix, post_mix=g_post_mix,
        pre_ffn=g_pre_ffn, post_ffn=g_post_ffn).items()}

    saved = []
    xl = x0
    h = _norm_cast(x0, g3["pre_mix"], 0, tm)
    for l in range(n_layers):
        win = (win_top, win_bottom)
        more = l + 1 < n_layers
        pc, ync, wout = _in_proj_conv(h, win, wc_full, g3["conv"], l, s, tq, comm=_Gather([(local_w[1], l)]))
        qkvp = _in_proj_qkv(h, win, s, tq)
        o, lse, yna, wfin = _attn_fwd(qkvp, biasm, g3["attn"], l, s, tq, comm=_Gather([(local_w[2], l)]))
        wout = wout.reshape(D_MODEL, D_MODEL)
        z, xm, h2 = _out_proj_fwd(ync, yna, wout, xl, g3["post_mix"], g3["pre_ffn"], l, s, tq)
        gu, act, wfout, *rest = _ffn_in_fwd(
            h2, wfin.reshape(4, FF_PAIR, D_MODEL), s, tf,
            comm=_Gather([(local_w[3], l)] + ([(win_halves, 2 * l + 2)] if more else [])))
        wfo = wfout.reshape(D_FF, D_MODEL)
        weights[l] = [win, wout, wfin.reshape(2 * D_FF, D_MODEL), wfo]
        sv = dict(x=xl, h=h, pc=pc, qkvp=qkvp, ync=ync, yna=yna, o=o, lse=lse, z=z, xm=xm, h2=h2, gu=gu, act=act)
        if more:
            win_top, = rest
            sv["f"], xl, h, win_bottom = _ffn_out_fwd(act, wfo, xm, g3["post_ffn"], g3["pre_mix"], l, l + 1, s, tm,
                                                      comm=_Gather([(win_halves, 2 * l + 3)]))
        else:
            dx, sq, df, dg_post_ffn = _ffn_out_loss(act, wfo, xm, g3["post_ffn"], target, l, s, tm)
        saved.append(sv)

    loss = lax.psum(jnp.sum(sq) * (0.5 / D_MODEL), ("x", "y", "c"))

    lands = dict(win=[None] * n_layers, wout=[None] * n_layers, wfin=[None] * n_layers, wfout=[None] * n_layers)
    small = {k: [None] * n_layers for k in ("gco", "gao", "gpm", "gqm", "gpf", "gqf", "wc")}
    d_rel = [None] * n_layers
    started = []

    def start(name, keys, l, arrays):
        items = [(a, False) for a in arrays]
        send_sems, recv_sems, srcs, zones, token = _exchange_start(name + "_start", items)
        started.append((name, keys, l, items, send_sems, recv_sems, srcs, zones))
        return token[0:1, 0:1].reshape(1, 1, 1)

    for l in reversed(range(n_layers)):
        sv = saved[l]
        win, wout, wfin, wfo = weights[l]
        small["gqf"][l] = dg_post_ffn
        dgu, d_wfout = _ffn_out_bwd(df, wfo, sv["gu"], sv["act"], s, tf)
        d_wfout = d_wfout.reshape(N_DEV, FFO_SHARD, D_MODEL)
        d_wfin = _dw_ffn_in(sv["h2"], dgu, s).reshape(N_DEV, FF_SHARD, D_MODEL)
        g_pre_ffn = g3["pre_ffn"]
        if l == 0:
            g_pre_ffn = g_pre_ffn + start("exchange_ffn0", ("wfout", "wfin"), l, [d_wfout, d_wfin])
        dxm, dz, dg_pre_ffn, dg_post_mix = _ffn_in_bwd(
            dgu, wfin, sv["xm"], g_pre_ffn, dx, sv["z"], g3["post_mix"], l, s, tm)
        small["gpf"][l] = dg_pre_ffn
        small["gqm"][l] = dg_post_mix
        dyc, do, dg_attn, d_wout = _out_proj_bwd(dz, wout, sv["o"], g3["attn"], sv["ync"], sv["yna"], l, s, tq)
        d_wout = d_wout.reshape(N_DEV, D_MODEL // N_DEV, D_MODEL)
        small["gao"][l] = dg_attn
        dq, dk, dv, ds_sum = _attn_bwd(sv["qkvp"], biasm, sv["o"], sv["lse"], do, l, s, tq)
        d_rel[l] = _bias_bwd(ds_sum[None])
        dproj, dwc, dg_conv = _conv_bwd(sv["pc"], dyc, wc_full, g3["conv"], dq, dk, dv, l, s, tf)
        small["wc"][l] = dwc
        small["gco"][l] = dg_conv
        d_win = _dw_in(sv["h"], dproj, s)
        if l == 0:
            token = start("exchange_mix0", ("wout", "win"), l, [d_wout, d_win])
        else:
            token = start(f"exchange_layer{l}", ("wfout", "wfin", "wout", "win"), l, [d_wfout, d_wfin, d_wout, d_win])
        if l > 0:
            dx, dg_pre_mix, df, dg_post_ffn = _in_proj_bwd(
                dproj, win, sv["x"], g3["pre_mix"] + token, dxm, l, s, tm, f_prev=saved[l - 1]["f"],
                g_post3=g3["post_ffn"])
        else:
            dx, dg_pre_mix = _in_proj_bwd(dproj, win, sv["x"], g3["pre_mix"] + token, dxm, l, s, tm)
        small["gpm"][l] = dg_pre_mix
    grad_x = dx.reshape(1, s, D_MODEL)

    small_vec = _pack_small_grads(d_rel, small)
    small_items = [(small_vec, True)]
    small_sems = _exchange_start("exchange_small_start", small_items)

    owns = dict(win=[None] * n_layers, wout=[None] * n_layers, wfin=[None] * n_layers, wfout=[None] * n_layers)

    def wait(last, after):
        for name, keys, l, items, send_sems, recv_sems, srcs, zones in started:
            if (name == "exchange_mix0") == last:
                srcs, zones = _exchange_wait(name + "_wait", items, send_sems, recv_sems, srcs, zones, after)
                for key, src, zone in zip(keys, srcs, zones):
                    owns[key][l], lands[key][l] = src, zone

    me = dev.astype(jnp.int32).reshape(1)
    wait(False, small_sems[4])
    r_fin = [jnp.transpose(t, (0, 2, 1)) for t in _adamw(
        "adamw_w_ffn_in", wt_ffn_in, mt_ffn_in, vt_ffn_in, lands["wfin"], owns["wfin"], me)]
    r_fout = _adamw("adamw_w_ffn_out", w_ffn_out, m_w_ffn_out, v_w_ffn_out, lands["wfout"], owns["wfout"], me)
    wait(True, r_fout[0])
    r_out = _adamw("adamw_w_out", w_out, m_w_out, v_w_out, lands["wout"], owns["wout"], me)
    r_in = _adamw("adamw_w_in", w_in, m_w_in, v_w_in, lands["win"], owns["win"], me)
    (small_own,), (land_small,) = _exchange_wait(
        "exchange_small_wait", small_items, small_sems[0], small_sems[1], small_sems[2], small_sems[3], r_in[0])

    n_rep = 64 * n_layers
    rep = _adamw(
        "adamw_replicated",
        _pack_small(rel_bias, g_conv_out, g_attn_out, g_pre_mix, g_post_mix, g_pre_ffn, g_post_ffn)[None],
        _pack_small(m_rel_bias, m_g_conv_out, m_g_attn_out, m_g_pre_mix, m_g_post_mix, m_g_pre_ffn, m_g_post_ffn)[None],
        _pack_small(v_rel_bias, v_g_conv_out, v_g_attn_out, v_g_pre_mix, v_g_post_mix, v_g_pre_ffn, v_g_post_ffn)[None],
        [land_small[:, :n_rep]], [small_own[:n_rep]], me)
    rep = [_unpack_small(t[0], n_layers) for t in rep]

    wc_rows = n_layers * 3 * CONV_WIDTH // 128
    zeros_wc = jnp.zeros((1, wc_rows, 128), F32)
    g_wc_full = _adamw("sum_w_conv", zeros_wc, zeros_wc, zeros_wc, [land_small[:, n_rep:n_rep + wc_rows]],
                       [small_own[n_rep:n_rep + wc_rows]], me)[0]
    g_wc_full = g_wc_full.reshape(n_layers, 3, CONV_WIDTH)
    g_wc = lax.dynamic_slice_in_dim(g_wc_full, dev * (CONV_WIDTH // N_DEV), CONV_WIDTH // N_DEV, axis=2)
    g_wc = jnp.transpose(g_wc, (0, 2, 1))

    def tiny(a):
        flat = a.reshape(-1)
        return jnp.pad(flat, (0, (-flat.shape[0]) % 1024)).reshape(1, -1, 128)

    r_wc = _adamw("adamw_w_conv", tiny(w_conv), tiny(m_w_conv), tiny(v_w_conv), [tiny(g_wc)])
    r_wc = [t.reshape(-1)[:w_conv.size].reshape(w_conv.shape) for t in r_wc]

    def leaf(kind):
        return [r_in[kind], r_wc[kind], rep[kind][0], rep[kind][1], rep[kind][2], r_out[kind],
                rep[kind][3], rep[kind][4], rep[kind][5], rep[kind][6], r_fin[kind], r_fout[kind]]

    return (loss, grad_x, *leaf(0), *leaf(1), *leaf(2), *leaf(3))
```

```python
import math

import jax
import jax.numpy as jnp
from jax import lax
from jax.experimental import pallas as pl
from jax.experimental.pallas import tpu as pltpu

F32 = jnp.float32
BF16 = jnp.bfloat16

D_MODEL = 1024
N_DEV = 8
CHUNK = 64
N_LEFT_CHUNKS = 8
CONV_WIDTH = 512
ATTN_WIDTH = 512
HEAD_DIM = 64
N_HEADS = 8
REL_CLIP = 128
REL_PAD = 384
PROJ_WIDTH = 3072
PROJ_SHARD = PROJ_WIDTH // N_DEV
D_FF = 2816
FF_SHARD = 2 * D_FF // N_DEV
FFO_SHARD = D_FF // N_DEV
FF_PAIR = 2 * FF_SHARD
_COL_SUBTILES = (slice(0, 768), slice(768, FF_PAIR))
EPS = 1e-6
NEG_INF = -1e30
ATTN_PAIRS = 2
Q_STEP = 2
Q_BLOCK = 4 * CHUNK
K_BAND = Q_BLOCK + N_LEFT_CHUNKS * CHUNK
LEFT = N_LEFT_CHUNKS * CHUNK
TOEP = 1024

ADAM_LR = 0.001
ADAM_B1 = 0.9
ADAM_B2 = 0.999
ADAM_EPS = 1e-08
ADAM_WD = 0.01
ADAM_STEP = 10

VMEM_LIMIT = 52 * 1024 * 1024
SUB_ROWS = 256
MESH = pl.DeviceIdType.MESH
ANY = pl.BlockSpec(memory_space=pl.ANY)

NT = (((1,), (1,)), ((), ()))
TN = (((0,), (0,)), ((), ()))


def _dot(a, b):
    return jnp.dot(a, b, preferred_element_type=F32)


def _dot_nt(a, b):
    return lax.dot_general(a, b, NT, preferred_element_type=F32)


def _dot_tn(a, b):
    return lax.dot_general(a, b, TN, preferred_element_type=F32)


def _rstd(v):
    return lax.rsqrt(jnp.mean(v * v, axis=-1, keepdims=True) + EPS)


def _group_matrix():
    r = lax.broadcasted_iota(jnp.int32, (128, 128), 0) >> 6
    c = lax.broadcasted_iota(jnp.int32, (128, 128), 1) >> 6
    return jnp.where(r == c, 1.0, 0.0).astype(BF16)


def _group_mean(v, gmat):
    hi = v.astype(BF16)
    lo = (v - hi.astype(F32)).astype(BF16)
    return (_dot(hi, gmat) + _dot(lo, gmat)) * (1.0 / HEAD_DIM)


def _split3(v):
    hi = v.astype(BF16)
    r1 = v - hi.astype(F32)
    mid = r1.astype(BF16)
    lo = (r1 - mid.astype(F32)).astype(BF16)
    return hi, mid, lo


def _row_tile(rows, cands=(1024, 512, 704, 256, 128, 64, 32, 16)):
    for c in cands:
        if rows % c == 0:
            return c
    return rows


def _dev_index(px, py, pc):
    return 4 * px + 2 * py + pc


def _when(cond):
    if cond is True:
        return lambda fn: fn()
    return pl.when(cond)


def _phases(grid):
    def phases():
        if not grid:
            return True, True, True
        lin = pl.program_id(0)
        for a in range(1, len(grid)):
            lin = lin * grid[a] + pl.program_id(a)
        total = math.prod(grid)
        return lin == 0, lin == total - 1, lin == total - 1
    return phases


class _Gather:
    def __init__(self, items):
        self.items = items
        self.args = [a for a, _ in items]
        n = len(items)
        self.out_shape = [jax.ShapeDtypeStruct((N_DEV,) + (a.shape if lay is None else a.shape[1:]), a.dtype)
                          for a, lay in items]
        self.scratch = [pltpu.SemaphoreType.DMA((n, 7)), pltpu.SemaphoreType.DMA((n, 7)),
                        pltpu.SemaphoreType.DMA((n,))]

    def _ctx(self, ins, outs, sems):
        send_sems, recv_sems, local_sems = sems
        x, y, c = lax.axis_index("x"), lax.axis_index("y"), lax.axis_index("c")
        chips = [(1 - x, y), (x, 1 - y), (1 - x, 1 - y)]

        def src(k):
            lay = self.items[k][1]
            return ins[k] if lay is None else ins[k].at[lay]

        def copy(k, s, idx, to, from_src=False):
            return pltpu.make_async_remote_copy(
                src_ref=src(k) if from_src else outs[k].at[idx], dst_ref=outs[k].at[idx],
                send_sem=send_sems.at[k, s], recv_sem=recv_sems.at[k, s],
                device_id=to, device_id_type=MESH)

        def local(k):
            return pltpu.make_async_copy(src(k), outs[k].at[_dev_index(x, y, c)], local_sems.at[k])

        return x, y, c, chips, copy, local

    def start(self, ins, outs, sems, cond):
        n = len(self.items)

        @_when(cond)
        def _():
            x, y, c, chips, copy, local = self._ctx(ins, outs, sems)
            me = _dev_index(x, y, c)
            for k in range(n):
                local(k).start()
                copy(k, 0, me, (x, y, 1 - c), from_src=True).start()
                for j, chip in enumerate(chips):
                    copy(k, 1 + j, me, (chip[0], chip[1], c), from_src=True).start()

    def forward(self, ins, outs, sems, cond):
        n = len(self.items)

        @_when(cond)
        def _():
            x, y, c, chips, copy, local = self._ctx(ins, outs, sems)
            for j, chip in enumerate(chips):
                idx = _dev_index(chip[0], chip[1], c)
                for k in range(n):
                    copy(k, 1 + j, idx, (x, y, c)).wait_recv()
                    copy(k, 4 + j, idx, (x, y, 1 - c)).start()

    def finish(self, ins, outs, sems, cond):
        n = len(self.items)

        @_when(cond)
        def _():
            x, y, c, chips, copy, local = self._ctx(ins, outs, sems)
            me = _dev_index(x, y, c)
            for k in range(n):
                copy(k, 0, _dev_index(x, y, 1 - c), (x, y, c)).wait_recv()
            for j, chip in enumerate(chips):
                idx = _dev_index(chip[0], chip[1], 1 - c)
                for k in range(n):
                    copy(k, 4 + j, idx, (x, y, c)).wait_recv()
            for k in range(n):
                for s in range(4):
                    copy(k, s, me, (x, y, c), from_src=True).wait_send()
                for j, chip in enumerate(chips):
                    copy(k, 4 + j, _dev_index(chip[0], chip[1], c), (x, y, c)).wait_send()
                local(k).wait()


_PEER_FLIPS = [(0, 0, 1), (1, 0, 0), (0, 1, 0), (1, 1, 0), (1, 0, 1), (0, 1, 1), (1, 1, 1)]


def _call(body, *, name, grid, in_specs, out_specs, out_shape, args, scratch=(), comm=None):
    n_hi, n_ho, n_hs = len(args), len(out_shape), len(scratch)
    c_args = list(comm.args) if comm else []
    c_out = list(comm.out_shape) if comm else []
    c_scr = list(comm.scratch) if comm else []
    phases = _phases(grid)

    def kern(*refs):
        cuts = [n_hi, len(c_args), n_ho, len(c_out), n_hs, len(c_scr)]
        parts, pos = [], 0
        for n in cuts:
            parts.append(refs[pos:pos + n])
            pos += n
        hi, ci, ho, co, hs, cs = parts
        if comm:
            first, mid, last = phases()
            comm.start(ci, co, cs, first)
            comm.forward(ci, co, cs, mid)
        body(*hi, *ho, *hs)
        if comm:
            comm.finish(ci, co, cs, last)

    sem = ("arbitrary",) * len(grid) if grid else None
    return pl.pallas_call(
        kern, name=name, grid=grid,
        in_specs=list(in_specs) + [ANY] * len(c_args),
        out_specs=list(out_specs) + [ANY] * len(c_out),
        out_shape=list(out_shape) + c_out,
        scratch_shapes=list(scratch) + c_scr,
        compiler_params=pltpu.CompilerParams(dimension_semantics=sem, vmem_limit_bytes=VMEM_LIMIT),
    )(*args, *c_args)


HBM_SPEC = pl.BlockSpec(memory_space=pltpu.HBM)
SEM_SPEC = pl.BlockSpec(memory_space=pltpu.SEMAPHORE)
SIDE_EFFECT = pltpu.SideEffectType.DATAFLOW_SIDE_EFFECTING


def _exchange_peer(x, y, c, s):
    fx, fy, fc = _PEER_FLIPS[s]
    return x ^ fx, y ^ fy, c ^ fc


def _exchange_start(name, items):
    n = len(items)
    srcs = [pltpu.with_memory_space_constraint(a, pltpu.HBM) for a, _ in items]
    land_shapes = [(N_DEV,) + (a.shape if whole else a.shape[1:]) for a, whole in items]
    lands = [pltpu.with_memory_space_constraint(lax.empty(shp, a.dtype), pltpu.HBM)
             for shp, (a, _) in zip(land_shapes, items)]

    n_sem = 7 * n

    def body(*refs):
        src_refs, land_refs = refs[:n], refs[n:2 * n]
        send_sems = refs[2 * n:2 * n + n_sem]
        recv_sems = refs[2 * n + n_sem:2 * n + 2 * n_sem]
        token = refs[-1]
        x, y, c = lax.axis_index("x"), lax.axis_index("y"), lax.axis_index("c")
        me = _dev_index(x, y, c)
        for s in range(7):
            px, py, pc = _exchange_peer(x, y, c, s)
            for k in range(n):
                src = src_refs[k] if items[k][1] else src_refs[k].at[_dev_index(px, py, pc)]
                pltpu.make_async_remote_copy(
                    src_ref=src, dst_ref=land_refs[k].at[me],
                    send_sem=send_sems[7 * k + s], recv_sem=recv_sems[7 * k + s],
                    device_id=(px, py, pc), device_id_type=MESH).start()
        token[...] = jnp.zeros(token.shape, token.dtype)

    outs = pl.pallas_call(
        body, name=name,
        out_shape=(*[pltpu.SemaphoreType.DMA(())] * (2 * n_sem),
                   *[pltpu.HBM(a.shape, a.dtype) for a in srcs],
                   *[pltpu.HBM(shp, a.dtype) for shp, a in zip(land_shapes, srcs)],
                   jax.ShapeDtypeStruct((8, 128), F32)),
        in_specs=[HBM_SPEC] * (2 * n),
        out_specs=(*[SEM_SPEC] * (2 * n_sem), *[HBM_SPEC] * (2 * n), pl.BlockSpec(memory_space=pltpu.VMEM)),
        input_output_aliases={i: 2 * n_sem + i for i in range(2 * n)},
        compiler_params=pltpu.CompilerParams(has_side_effects=SIDE_EFFECT),
    )(*srcs, *lands)
    base = 2 * n_sem
    return (list(outs[:n_sem]), list(outs[n_sem:base]), list(outs[base:base + n]),
            list(outs[base + n:base + 2 * n]), outs[-1])


def _exchange_wait(name, items, send_sems, recv_sems, srcs, lands, after):
    n = len(items)

    n_sem = 7 * n

    def body(*refs):
        src_refs, land_refs = refs[:n], refs[n:2 * n]
        send_refs = refs[2 * n:2 * n + n_sem]
        recv_refs = refs[2 * n + n_sem:2 * n + 2 * n_sem]
        x, y, c = lax.axis_index("x"), lax.axis_index("y"), lax.axis_index("c")
        for s in range(7):
            for k in range(n):
                copy = pltpu.make_async_remote_copy(
                    src_ref=src_refs[k] if items[k][1] else src_refs[k].at[0], dst_ref=land_refs[k].at[0],
                    send_sem=send_refs[7 * k + s], recv_sem=recv_refs[7 * k + s],
                    device_id=(x, y, c), device_id_type=MESH)
                copy.wait_send()
                copy.wait_recv()

    outs = pl.pallas_call(
        body, name=name,
        out_shape=(*[pltpu.HBM(a.shape, a.dtype) for a in srcs], *[pltpu.HBM(a.shape, a.dtype) for a in lands]),
        in_specs=[HBM_SPEC] * (2 * n) + [SEM_SPEC] * (2 * n_sem) + [ANY],
        out_specs=tuple([HBM_SPEC] * (2 * n)),
        input_output_aliases={i: i for i in range(2 * n)},
        compiler_params=pltpu.CompilerParams(has_side_effects=SIDE_EFFECT),
    )(*srcs, *lands, *send_sems, *recv_sems, after)
    return list(outs[:n]), list(outs[n:])


def _cast_bf16(x, name):
    shape = x.shape
    x2 = x.reshape(-1, shape[-1])
    rows, cols = x2.shape
    tr = _row_tile(rows)

    def body(x_ref, o_ref):
        o_ref[...] = x_ref[...].astype(BF16)

    blk = pl.BlockSpec((tr, cols), lambda i: (i, 0))
    out, = _call(body, name=name, grid=(rows // tr,), in_specs=[blk], out_specs=[blk],
                 out_shape=[jax.ShapeDtypeStruct((rows, cols), BF16)], args=[x2])
    return out.reshape(shape)


def _norm_cast(x, g3, l, tm):
    s = x.shape[0]

    def body(x_ref, g_ref, o_ref):
        v = x_ref[...]
        o_ref[...] = (v * _rstd(v) * g_ref[...]).astype(BF16)

    row = pl.BlockSpec((tm, D_MODEL), lambda i: (i, 0))
    out, = _call(body, name="norm_cast", grid=(s // tm,),
                 in_specs=[row, pl.BlockSpec((None, 1, D_MODEL), lambda i: (l, 0, 0))], out_specs=[row],
                 out_shape=[jax.ShapeDtypeStruct((s, D_MODEL), BF16)], args=[x, g3])
    return out


def _join_w_in(top_ref, bottom_ref):
    n = top_ref.shape[0]
    return jnp.concatenate([jnp.concatenate([ref[j] for j in range(n)], axis=1) for ref in (top_ref, bottom_ref)],
                           axis=0)


def _in_proj_qkv(h, win, s, tq):
    def body(a_ref, bt_ref, bb_ref, o_ref):
        i = pl.program_id(0)

        @pl.when(i == 0)
        def _():
            o_ref[...] = jnp.zeros(o_ref.shape, BF16)

        @pl.when(i > 0)
        def _():
            o_ref[...] = _dot(a_ref[...], _join_w_in(bt_ref, bb_ref)).astype(BF16)

    half = pl.BlockSpec((4, D_MODEL // 2, PROJ_SHARD), lambda i: (1, 0, 0))
    out, = _call(
        body, name="in_proj_qkv", grid=(s // tq + 1,),
        in_specs=[pl.BlockSpec((tq, D_MODEL), lambda i: (jnp.maximum(i - 1, 0), 0)), half, half],
        out_specs=[pl.BlockSpec((tq, 4 * PROJ_SHARD), lambda i: (i, 0))],
        out_shape=[jax.ShapeDtypeStruct((s + tq, PROJ_WIDTH // 2), BF16)], args=[h, *win])
    return out


def _in_proj_conv(h, win, wc, g3, l, s, tq, comm=None):
    def body(a_ref, bt_ref, bb_ref, wc_ref, g_ref, pc_ref, y_ref, carry_ref, acc_ref):
        i = pl.program_id(0)
        acc_ref[...] = _dot(a_ref[...], _join_w_in(bt_ref, bb_ref))
        pc_ref[...] = acc_ref[...].astype(BF16)
        gmat = _group_matrix()
        for j in range(CONV_WIDTH // 128):
            c0, c1, c2 = 128 * j, CONV_WIDTH + 128 * j, 2 * CONV_WIDTH + 128 * j
            hc = acc_ref[:, c0:c0 + 128]
            bg = acc_ref[:, c1:c1 + 128]
            cg = acc_ref[:, c2:c2 + 128]
            u_prev = jnp.where(i > 0, carry_ref[:, c0:c0 + 128], 0.0)
            u = cg * hc
            carry_ref[:, c0:c0 + 128] = u[tq - 8:tq, :]
            full = jnp.concatenate([u_prev, u], axis=0)
            u1 = pltpu.roll(full, 1, 0)[8:]
            u2 = pltpu.roll(full, 2, 0)[8:]
            out = (u2 * wc_ref[0:1, c0:c0 + 128] + u1 * wc_ref[1:2, c0:c0 + 128]
                   + u * wc_ref[2:3, c0:c0 + 128])
            yc = bg * out
            r = lax.rsqrt(_group_mean(yc * yc, gmat) + EPS)
            y_ref[:, c0:c0 + 128] = (yc * r * g_ref[:, c0:c0 + 128]).astype(BF16)

    half = pl.BlockSpec((4, D_MODEL // 2, PROJ_SHARD), lambda i: (0, 0, 0))
    return _call(
        body, name="in_proj_conv", grid=(s // tq,),
        in_specs=[pl.BlockSpec((tq, D_MODEL), lambda i: (i, 0)), half, half,
                  pl.BlockSpec((None, 8, CONV_WIDTH), lambda i: (l, 0, 0)),
                  pl.BlockSpec((None, 1, CONV_WIDTH), lambda i: (l, 0, 0))],
        out_specs=[pl.BlockSpec((tq, 3 * CONV_WIDTH), lambda i: (i, 0)),
                   pl.BlockSpec((tq, CONV_WIDTH), lambda i: (i, 0))],
        out_shape=[jax.ShapeDtypeStruct((s, 3 * CONV_WIDTH), BF16), jax.ShapeDtypeStruct((s, CONV_WIDTH), BF16)],
        scratch=[pltpu.VMEM((8, CONV_WIDTH), F32), pltpu.VMEM((tq, 3 * CONV_WIDTH), F32)],
        args=[h, *win, wc, g3], comm=comm)


def _toeplitz_source():
    r_i = lax.broadcasted_iota(jnp.int32, (REL_PAD, TOEP), 0)
    m_i = lax.broadcasted_iota(jnp.int32, (REL_PAD, TOEP), 1)
    idx = jnp.clip((K_BAND - 1) - m_i, -REL_CLIP, REL_CLIP) + REL_CLIP
    return jnp.where(r_i == idx, 1.0, 0.0).astype(BF16)


def _bias_build(rbp, comm=None):
    n_layers = rbp.shape[0]

    def body(rb_ref, o_ref, t_ref):
        pmat = _toeplitz_source()
        hi, mid, lo = _split3(rb_ref[...])
        t_ref[...] = _dot(hi, pmat) + _dot(mid, pmat) + _dot(lo, pmat)
        shift = (CHUNK - 1) - lax.broadcasted_iota(jnp.int32, (CHUNK, TOEP), 0)
        kchunk = lax.broadcasted_iota(jnp.int32, (CHUNK, K_BAND), 1) >> 6
        for h in range(N_HEADS):
            b = jnp.broadcast_to(t_ref[pl.ds(h, 1), :], (CHUNK, TOEP))
            for bit in range(6):
                rolled = pltpu.roll(b, TOEP - (1 << bit), 1)
                b = jnp.where(((shift >> bit) & 1) == 1, rolled, b)
            for cq in range(Q_BLOCK // CHUNK):
                off = CHUNK * (Q_BLOCK // CHUNK - 1 - cq)
                band = pltpu.roll(b, TOEP - off, 1) if off else b
                dchunk = kchunk - cq
                in_band = jnp.where(dchunk >= 0, jnp.where(dchunk <= N_LEFT_CHUNKS, 1, 0), 0) == 1
                o_ref[h, CHUNK * cq:CHUNK * (cq + 1), :] = jnp.where(in_band, band[:, :K_BAND], NEG_INF)

    return _call(
        body, name="bias_build", grid=(n_layers,),
        in_specs=[pl.BlockSpec((None, N_HEADS, REL_PAD), lambda l: (l, 0, 0))],
        out_specs=[pl.BlockSpec((None, N_HEADS, Q_BLOCK, K_BAND), lambda l: (l, 0, 0, 0))],
        out_shape=[jax.ShapeDtypeStruct((n_layers, N_HEADS, Q_BLOCK, K_BAND), F32)],
        scratch=[pltpu.VMEM((N_HEADS, TOEP), F32)], args=[rbp], comm=comm)


def _bias_bwd(ds_sum):
    n_layers = ds_sum.shape[0]

    def body(ds_ref, o_ref, t_ref):
        pmat = _toeplitz_source()
        shift = (CHUNK - 1) - lax.broadcasted_iota(jnp.int32, (CHUNK, TOEP), 0)
        for h in range(N_HEADS):
            d = None
            for cq in range(Q_BLOCK // CHUNK):
                off = CHUNK * (Q_BLOCK // CHUNK - 1 - cq)
                part = jnp.concatenate([ds_ref[h, CHUNK * cq:CHUNK * (cq + 1), :],
                                        jnp.zeros((CHUNK, TOEP - K_BAND), F32)], axis=1)
                part = pltpu.roll(part, off, 1) if off else part
                d = part if d is None else d + part
            for bit in range(6):
                rolled = pltpu.roll(d, 1 << bit, 1)
                d = jnp.where(((shift >> bit) & 1) == 1, rolled, d)
            t_ref[pl.ds(h, 1), :] = jnp.sum(d, axis=0, keepdims=True)
        hi, mid, lo = _split3(t_ref[...])
        o_ref[...] = _dot_nt(hi, pmat) + _dot_nt(mid, pmat) + _dot_nt(lo, pmat)

    out, = _call(
        body, name="bias_bwd", grid=(n_layers,),
        in_specs=[pl.BlockSpec((None, N_HEADS, Q_BLOCK, K_BAND), lambda l: (l, 0, 0, 0))],
        out_specs=[pl.BlockSpec((None, N_HEADS, REL_PAD), lambda l: (l, 0, 0))],
        out_shape=[jax.ShapeDtypeStruct((n_layers, N_HEADS, REL_PAD), F32)],
        scratch=[pltpu.VMEM((N_HEADS, TOEP), F32)], args=[ds_sum])
    return out


def _attn_fwd(qkvp, biasm, g3, l, s, pad, comm=None):
    rows = Q_STEP * Q_BLOCK
    nb = s // rows
    qb0 = pad // rows
    scale = HEAD_DIM ** -0.5
    wide = 128 * ATTN_PAIRS

    def body(q_ref, k_ref, v_ref, b_ref, g_ref, o_ref, lse_ref, yn_ref):
        blk = pl.program_id(1)
        lane = lax.broadcasted_iota(jnp.int32, (1, 128), 1)
        gmat = _group_matrix()

        def step(masked):
            for sub in range(Q_STEP):
                rs = slice(Q_BLOCK * sub, Q_BLOCK * (sub + 1))
                start = blk * rows + Q_BLOCK * sub
                koff = pl.multiple_of(start + (pad - LEFT), Q_BLOCK)
                kpos = lax.broadcasted_iota(jnp.int32, (1, K_BAND), 1) + (start - LEFT)
                kmask = jnp.where(kpos >= 0, 0.0, NEG_INF)
                for pr in range(ATTN_PAIRS):
                    ls = slice(128 * pr, 128 * (pr + 1))
                    q = q_ref[rs, ls]
                    kb = k_ref[pl.ds(koff, K_BAND), ls]
                    vb = v_ref[pl.ds(koff, K_BAND), ls]
                    outs, lses = [], []
                    for hh in range(2):
                        in_head = (lane >> 6) == hh
                        qm = jnp.where(in_head, q, jnp.zeros_like(q)) * jnp.asarray(scale, BF16)
                        sc = _dot_nt(qm, kb) + b_ref[2 * pr + hh]
                        if masked:
                            sc = sc + kmask
                        m = jnp.max(sc, axis=1, keepdims=True)
                        e = jnp.exp(sc - m)
                        den = jnp.sum(e, axis=1, keepdims=True)
                        outs.append(_dot(e.astype(BF16), vb) * (1.0 / den))
                        lses.append(m + jnp.log(den))
                    first = lane < HEAD_DIM
                    o = jnp.where(first, outs[0], outs[1])
                    o_ref[rs, ls] = o
                    lse_ref[rs, ls] = jnp.where(first, lses[0], lses[1])
                    r = lax.rsqrt(_group_mean(o * o, gmat) + EPS)
                    yn_ref[rs, ls] = (o * r * g_ref[:, ls]).astype(BF16)

        pl.when(blk == 0)(lambda: step(True))
        pl.when(blk > 0)(lambda: step(False))

    blk_out = pl.BlockSpec((rows, wide), lambda p, b: (b, p))
    n_grp = ATTN_WIDTH // wide
    return _call(
        body, name="attn_fwd", grid=(n_grp, nb),
        in_specs=[pl.BlockSpec((rows, wide), lambda p, b: (qb0 + b, p)),
                  pl.BlockSpec((s + pad, wide), lambda p, b: (0, n_grp + p)),
                  pl.BlockSpec((s + pad, wide), lambda p, b: (0, 2 * n_grp + p)),
                  pl.BlockSpec((None, 2 * ATTN_PAIRS, Q_BLOCK, K_BAND), lambda p, b: (l, p, 0, 0)),
                  pl.BlockSpec((None, 1, wide), lambda p, b: (l, 0, p))],
        out_specs=[blk_out, blk_out, blk_out],
        out_shape=[jax.ShapeDtypeStruct((s, ATTN_WIDTH), F32),
                   jax.ShapeDtypeStruct((s, ATTN_WIDTH), F32),
                   jax.ShapeDtypeStruct((s, ATTN_WIDTH), BF16)],
        args=[qkvp, qkvp, qkvp, biasm, g3], comm=comm)


def _out_proj_fwd(ync, yna, wout, x, g_post3, g_next3, l, s, tm):
    half = D_MODEL // 2

    def body(a1_ref, a2_ref, w_ref, x_ref, gp_ref, gn_ref, z_ref, xm_ref, h_ref):
        for rs in _row_subtiles(tm, SUB_ROWS):
            z = _dot(a1_ref[rs, :], w_ref[0:half, :]) + _dot(a2_ref[rs, :], w_ref[half:D_MODEL, :])
            z_ref[rs, :] = z.astype(BF16)
            xm = x_ref[rs, :] + z * _rstd(z) * gp_ref[...]
            xm_ref[rs, :] = xm
            h_ref[rs, :] = (xm * _rstd(xm) * gn_ref[...]).astype(BF16)

    row = pl.BlockSpec((tm, D_MODEL), lambda i: (i, 0))
    gain = pl.BlockSpec((None, 1, D_MODEL), lambda i: (l, 0, 0))
    return _call(
        body, name="out_proj_fwd", grid=(s // tm,),
        in_specs=[pl.BlockSpec((tm, half), lambda i: (i, 0)), pl.BlockSpec((tm, half), lambda i: (i, 0)),
                  pl.BlockSpec((D_MODEL, D_MODEL), lambda i: (0, 0)), row, gain, gain],
        out_specs=[row, row, row],
        out_shape=[jax.ShapeDtypeStruct((s, D_MODEL), BF16), jax.ShapeDtypeStruct((s, D_MODEL), F32),
                   jax.ShapeDtypeStruct((s, D_MODEL), BF16)],
        args=[ync, yna, wout, x, g_post3, g_next3])


def _ffn_in_fwd(h2, wfin4, s, tm, comm=None):
    def body(h_ref, wg_ref, wu_ref, gu_ref, act_ref):
        h = h_ref[...]
        for cs in _COL_SUBTILES:
            gate = _dot_nt(h, wg_ref[cs, :])
            up = _dot_nt(h, wu_ref[cs, :])
            gu_ref[0, :, cs] = gate.astype(BF16)
            gu_ref[1, :, cs] = up.astype(BF16)
            act_ref[:, cs] = (gate * (1.0 / (1.0 + jnp.exp(-gate))) * up).astype(BF16)

    return _call(
        body, name="ffn_in_fwd", grid=(2, s // tm),
        in_specs=[pl.BlockSpec((tm, D_MODEL), lambda b, i: (i, 0)),
                  pl.BlockSpec((None, FF_PAIR, D_MODEL), lambda b, i: (b, 0, 0)),
                  pl.BlockSpec((None, FF_PAIR, D_MODEL), lambda b, i: (2 + b, 0, 0))],
        out_specs=[pl.BlockSpec((2, tm, FF_PAIR), lambda b, i: (0, i, b)),
                   pl.BlockSpec((tm, FF_PAIR), lambda b, i: (i, b))],
        out_shape=[jax.ShapeDtypeStruct((2, s, D_FF), BF16), jax.ShapeDtypeStruct((s, D_FF), BF16)],
        args=[h2, wfin4, wfin4], comm=comm)


def _ffn_out_fwd(act, wfo, xm, g_post3, g_next3, l, l_next, s, tm, comm=None):
    def body(a_ref, w_ref, x_ref, gp_ref, gn_ref, f_ref, xo_ref, h_ref):
        for rs in _row_subtiles(tm, SUB_ROWS):
            f = _dot(a_ref[rs, :], w_ref[...])
            f_ref[rs, :] = f.astype(BF16)
            xo = x_ref[rs, :] + f * _rstd(f) * gp_ref[...]
            xo_ref[rs, :] = xo
            h_ref[rs, :] = (xo * _rstd(xo) * gn_ref[...]).astype(BF16)

    row = pl.BlockSpec((tm, D_MODEL), lambda i: (i, 0))
    return _call(
        body, name="ffn_out_fwd", grid=(s // tm,),
        in_specs=[pl.BlockSpec((tm, D_FF), lambda i: (i, 0)),
                  pl.BlockSpec((D_FF, D_MODEL), lambda i: (0, 0), pipeline_mode=pl.Buffered(1)), row,
                  pl.BlockSpec((None, 1, D_MODEL), lambda i: (l, 0, 0)),
                  pl.BlockSpec((None, 1, D_MODEL), lambda i: (l_next, 0, 0))],
        out_specs=[row, row, row],
        out_shape=[jax.ShapeDtypeStruct((s, D_MODEL), BF16), jax.ShapeDtypeStruct((s, D_MODEL), F32),
                   jax.ShapeDtypeStruct((s, D_MODEL), BF16)],
        args=[act, wfo, xm, g_post3, g_next3], comm=comm)


def _ffn_out_loss(act, wfo, xm, g_post3, target, l, s, tm):
    def body(a_ref, w_ref, x_ref, gp_ref, t_ref, dx_ref, sq_ref, df_ref, dg_ref):
        _zero_first((sq_ref, dg_ref), pl.program_id(0) == 0)
        for rs in _row_subtiles(tm, SUB_ROWS):
            f = _dot(a_ref[rs, :], w_ref[...])
            gain = gp_ref[...]
            err = x_ref[rs, :] + f * _rstd(f) * gain - t_ref[rs, :]
            dx = err * (1.0 / D_MODEL)
            dx_ref[rs, :] = dx
            df, dyn = _norm_bwd_rows(f, gain, dx)
            df_ref[rs, :] = df.astype(BF16)
            _add_cols(dg_ref, dyn)
            cs = jnp.sum(err * err, axis=0, keepdims=True)
            part = cs[:, 0:128]
            for k in range(1, D_MODEL // 128):
                part = part + cs[:, 128 * k:128 * (k + 1)]
            sq_ref[0:1, :] += part

    row = pl.BlockSpec((tm, D_MODEL), lambda i: (i, 0))
    return _call(
        body, name="ffn_out_loss", grid=(s // tm,),
        in_specs=[pl.BlockSpec((tm, D_FF), lambda i: (i, 0)),
                  pl.BlockSpec((D_FF, D_MODEL), lambda i: (0, 0), pipeline_mode=pl.Buffered(1)), row,
                  pl.BlockSpec((None, 1, D_MODEL), lambda i: (l, 0, 0)), row],
        out_specs=[row, pl.BlockSpec((8, 128), lambda i: (0, 0)), row, pl.BlockSpec((8, D_MODEL), lambda i: (0, 0))],
        out_shape=[jax.ShapeDtypeStruct((s, D_MODEL), F32), jax.ShapeDtypeStruct((8, 128), F32),
                   jax.ShapeDtypeStruct((s, D_MODEL), BF16), jax.ShapeDtypeStruct((8, D_MODEL), F32)],
        args=[act, wfo, xm, g_post3, target])


def _norm_bwd_rows(v, g, dy):
    r = _rstd(v)
    vn = v * r
    gd = dy * g
    dv = r * (gd - vn * jnp.mean(vn * gd, axis=-1, keepdims=True))
    return dv, dy * vn


def _zero_first(refs, first):
    @pl.when(first)
    def _():
        for ref in refs:
            ref[...] = jnp.zeros(ref.shape, F32)


def _add_cols(ref, val):
    ref[0:1, :] += jnp.sum(val, axis=0, keepdims=True)


def _row_subtiles(rows, sub):
    sub = min(sub, rows)
    return [slice(r, r + sub) for r in range(0, rows, sub)]


def _ffn_out_bwd(df, wfo, gu, act, s, tm, comm=None):
    nm = s // tm

    def body(df_ref, w_ref, gu_ref, act_ref, dgu_ref, dw_ref, acc_ref):
        i = pl.program_id(1)
        df = df_ref[...]
        _zero_first((acc_ref,), i == 0)
        acc_ref[...] += _dot_tn(act_ref[...], df)

        @pl.when(i == nm - 1)
        def _():
            dw_ref[...] = acc_ref[...].astype(BF16)

        for cs in _COL_SUBTILES:
            da = _dot_nt(df, w_ref[cs, :])
            g = gu_ref[0, :, cs].astype(F32)
            u = gu_ref[1, :, cs].astype(F32)
            sg = 1.0 / (1.0 + jnp.exp(-g))
            dgu_ref[0, :, cs] = (da * u * (sg * (1.0 + g * (1.0 - sg)))).astype(BF16)
            dgu_ref[1, :, cs] = (da * (g * sg)).astype(BF16)

    blk = pl.BlockSpec((2, tm, FF_PAIR), lambda b, i: (0, i, b))
    wblk = pl.BlockSpec((FF_PAIR, D_MODEL), lambda b, i: (b, 0))
    return _call(
        body, name="ffn_out_bwd", grid=(2, nm),
        in_specs=[pl.BlockSpec((tm, D_MODEL), lambda b, i: (i, 0)),
                  pl.BlockSpec((FF_PAIR, D_MODEL), lambda b, i: (b, 0), pipeline_mode=pl.Buffered(1)), blk,
                  pl.BlockSpec((tm, FF_PAIR), lambda b, i: (i, b))],
        out_specs=[blk, wblk],
        out_shape=[jax.ShapeDtypeStruct((2, s, D_FF), BF16), jax.ShapeDtypeStruct((D_FF, D_MODEL), BF16)],
        scratch=[pltpu.VMEM((FF_PAIR, D_MODEL), F32)],
        args=[df, wfo, gu, act], comm=comm)


def _dw_ffn_in(h2, dgu, s):
    def body(a_ref, b_ref, o_ref):
        o_ref[...] = _dot_tn(b_ref[...], a_ref[...]).astype(BF16)

    out, = _call(
        body, name="dw_ffn_in", grid=(4,),
        in_specs=[pl.BlockSpec((s, D_MODEL), lambda n: (0, 0), pipeline_mode=pl.Buffered(1)),
                  pl.BlockSpec((None, s, FF_PAIR), lambda n: (n // 2, 0, n % 2))],
        out_specs=[pl.BlockSpec((None, FF_PAIR, D_MODEL), lambda n: (n, 0, 0))],
        out_shape=[jax.ShapeDtypeStruct((4, FF_PAIR, D_MODEL), BF16)], args=[h2, dgu])
    return out


def _ffn_in_bwd(dgu, wfin, xm, g_pre3, dres, z, g_post3, l, s, tm, comm=None):
    def body(d_ref, w_ref, xm_ref, gp_ref, dres_ref, z_ref, gq_ref, dxm_ref, dz_ref, dgp_ref, dgq_ref):
        _zero_first((dgp_ref, dgq_ref), pl.program_id(0) == 0)
        for rs in _row_subtiles(tm, SUB_ROWS):
            dh = _dot(d_ref[0, rs, :], w_ref[0:D_FF, :]) + _dot(d_ref[1, rs, :], w_ref[D_FF:2 * D_FF, :])
            dx, dyn = _norm_bwd_rows(xm_ref[rs, :], gp_ref[...], dh)
            dxm = dres_ref[rs, :] + dx
            dxm_ref[rs, :] = dxm
            _add_cols(dgp_ref, dyn)
            dz, dyn2 = _norm_bwd_rows(z_ref[rs, :].astype(F32), gq_ref[...], dxm)
            dz_ref[rs, :] = dz.astype(BF16)
            _add_cols(dgq_ref, dyn2)

    row = pl.BlockSpec((tm, D_MODEL), lambda i: (i, 0))
    gain = pl.BlockSpec((None, 1, D_MODEL), lambda i: (l, 0, 0))
    dgs = pl.BlockSpec((8, D_MODEL), lambda i: (0, 0))
    return _call(
        body, name="ffn_in_bwd", grid=(s // tm,),
        in_specs=[pl.BlockSpec((2, tm, D_FF), lambda i: (0, i, 0)),
                  pl.BlockSpec((2 * D_FF, D_MODEL), lambda i: (0, 0), pipeline_mode=pl.Buffered(1)),
                  row, gain, row, row, gain],
        out_specs=[row, row, dgs, dgs],
        out_shape=[jax.ShapeDtypeStruct((s, D_MODEL), F32), jax.ShapeDtypeStruct((s, D_MODEL), BF16),
                   jax.ShapeDtypeStruct((8, D_MODEL), F32), jax.ShapeDtypeStruct((8, D_MODEL), F32)],
        args=[dgu, wfin, xm, g_pre3, dres, z, g_post3], comm=comm)


def _out_proj_bwd(dz, wout, o, g3, ync, yna, l, s, tm):
    nm = s // tm
    half = D_MODEL // 2

    def body(dz_ref, w_ref, o_ref, g_ref, a1_ref, a2_ref, dyc_ref, do_ref, dg_ref, dw_ref, acc_ref):
        i = pl.program_id(0)
        gmat = _group_matrix()
        _zero_first((dg_ref, acc_ref), i == 0)
        dzv = dz_ref[...]
        acc_ref[0:half, :] += _dot_tn(a1_ref[...], dzv)
        acc_ref[half:D_MODEL, :] += _dot_tn(a2_ref[...], dzv)

        @pl.when(i == nm - 1)
        def _():
            dw_ref[...] = acc_ref[...].astype(BF16)

        for rs in _row_subtiles(tm, SUB_ROWS):
            dy = _dot_nt(dz_ref[rs, :], w_ref[...])
            dyc_ref[rs, :] = dy[:, 0:CONV_WIDTH]
            for j in range(ATTN_WIDTH // 128):
                c0 = 128 * j
                ov = o_ref[rs, c0:c0 + 128]
                dyn = dy[:, CONV_WIDTH + c0:CONV_WIDTH + c0 + 128]
                r = lax.rsqrt(_group_mean(ov * ov, gmat) + EPS)
                on = ov * r
                gd = dyn * g_ref[:, c0:c0 + 128]
                do_ref[rs, c0:c0 + 128] = r * (gd - on * _group_mean(on * gd, gmat))
                dg_ref[0:1, c0:c0 + 128] += jnp.sum(dyn * on, axis=0, keepdims=True)

    halfrow = pl.BlockSpec((tm, ATTN_WIDTH), lambda i: (i, 0))
    return _call(
        body, name="out_proj_bwd", grid=(nm,),
        in_specs=[pl.BlockSpec((tm, D_MODEL), lambda i: (i, 0)),
                  pl.BlockSpec((D_MODEL, D_MODEL), lambda i: (0, 0)), halfrow,
                  pl.BlockSpec((None, 1, ATTN_WIDTH), lambda i: (l, 0, 0)), halfrow, halfrow],
        out_specs=[halfrow, halfrow, pl.BlockSpec((8, ATTN_WIDTH), lambda i: (0, 0)),
                   pl.BlockSpec((D_MODEL, D_MODEL), lambda i: (0, 0))],
        out_shape=[jax.ShapeDtypeStruct((s, CONV_WIDTH), F32), jax.ShapeDtypeStruct((s, ATTN_WIDTH), F32),
                   jax.ShapeDtypeStruct((8, ATTN_WIDTH), F32), jax.ShapeDtypeStruct((D_MODEL, D_MODEL), BF16)],
        scratch=[pltpu.VMEM((D_MODEL, D_MODEL), F32)],
        args=[dz, wout, o, g3, ync, yna])


def _conv_bwd(pc, dyc, wc, g3, dq, dk, dv, l, s, tr):
    hb = tr // 8
    nt = s // tr
    ext = tr + 16
    last_hb = s // 8 - 1

    def body(pc_ref, prev_ref, next_ref, dy_ref, dyn_ref, wc_ref, g_ref, dq_ref, dk_ref, dv_ref,
             dpc_ref, dw_ref, dg_ref):
        i = pl.program_id(0)
        for part, ref in enumerate((dq_ref, dk_ref, dv_ref)):
            c = 3 * CONV_WIDTH + ATTN_WIDTH * part
            dpc_ref[:, c:c + ATTN_WIDTH] = ref[...]
        gmat = _group_matrix()
        row = lax.broadcasted_iota(jnp.int32, (ext, 128), 0) + (i * tr - 8)
        inside = jnp.where(row >= 0, jnp.where(row < s, 1, 0), 0) == 1

        @pl.when(i == 0)
        def _():
            dw_ref[...] = jnp.zeros(dw_ref.shape, F32)
            dg_ref[...] = jnp.zeros(dg_ref.shape, F32)

        def extend(ref_prev, ref_mid, ref_next, c):
            if ref_prev is None:
                before = jnp.zeros((8, 128), F32)
            else:
                before = ref_prev[:, c:c + 128].astype(F32)[ref_prev.shape[0] - 8:]
            after = ref_next[:, c:c + 128].astype(F32)[0:8]
            return jnp.concatenate([before, ref_mid[:, c:c + 128].astype(F32), after], axis=0)

        for j in range(CONV_WIDTH // 128):
            c0, c1, c2 = 128 * j, CONV_WIDTH + 128 * j, 2 * CONV_WIDTH + 128 * j
            hc = extend(prev_ref, pc_ref, next_ref, c0)
            bg = extend(prev_ref, pc_ref, next_ref, c1)
            cg = extend(prev_ref, pc_ref, next_ref, c2)
            dyn = extend(None, dy_ref, dyn_ref, c0)
            w0, w1, w2 = (wc_ref[0:1, c0:c0 + 128], wc_ref[1:2, c0:c0 + 128], wc_ref[2:3, c0:c0 + 128])
            gain = g_ref[:, c0:c0 + 128]
            u = jnp.where(inside, cg * hc, 0.0)
            u1 = pltpu.roll(u, 1, 0)
            u2 = pltpu.roll(u, 2, 0)
            out = u2 * w0 + u1 * w1 + u * w2
            yc = bg * out
            r = lax.rsqrt(_group_mean(yc * yc, gmat) + EPS)
            ycn = yc * r
            gd = dyn * gain
            dyc = r * (gd - ycn * _group_mean(ycn * gd, gmat))
            dout = jnp.where(inside, dyc * bg, 0.0)
            du = dout * w2 + pltpu.roll(dout, ext - 1, 0) * w1 + pltpu.roll(dout, ext - 2, 0) * w0
            sl = slice(8, 8 + tr)
            dpc_ref[:, c0:c0 + 128] = (du[sl] * cg[sl]).astype(BF16)
            dpc_ref[:, c1:c1 + 128] = (dyc[sl] * out[sl]).astype(BF16)
            dpc_ref[:, c2:c2 + 128] = (du[sl] * hc[sl]).astype(BF16)
            dw_ref[0:1, c0:c0 + 128] += jnp.sum(dout[sl] * u2[sl], axis=0, keepdims=True)
            dw_ref[1:2, c0:c0 + 128] += jnp.sum(dout[sl] * u1[sl], axis=0, keepdims=True)
            dw_ref[2:3, c0:c0 + 128] += jnp.sum(dout[sl] * u[sl], axis=0, keepdims=True)
            dg_ref[0:1, c0:c0 + 128] += jnp.sum(dyn[sl] * ycn[sl], axis=0, keepdims=True)

    wide = 3 * CONV_WIDTH
    return _call(
        body, name="conv_bwd", grid=(nt,),
        in_specs=[pl.BlockSpec((tr, wide), lambda i: (i, 0)),
                  pl.BlockSpec((16, wide), lambda i: (jnp.maximum(i * (hb // 2) - 1, 0), 0)),
                  pl.BlockSpec((16, wide), lambda i: (jnp.minimum((i + 1) * (hb // 2), last_hb // 2), 0)),
                  pl.BlockSpec((tr, CONV_WIDTH), lambda i: (i, 0)),
                  pl.BlockSpec((8, CONV_WIDTH), lambda i: (jnp.minimum((i + 1) * hb, last_hb), 0)),
                  pl.BlockSpec((None, 8, CONV_WIDTH), lambda i: (l, 0, 0)),
                  pl.BlockSpec((None, 1, CONV_WIDTH), lambda i: (l, 0, 0)),
                  pl.BlockSpec((tr, ATTN_WIDTH), lambda i: (i, 0)),
                  pl.BlockSpec((tr, ATTN_WIDTH), lambda i: (i, 0)),
                  pl.BlockSpec((tr, ATTN_WIDTH), lambda i: (i, 0))],
        out_specs=[pl.BlockSpec((tr, PROJ_WIDTH), lambda i: (i, 0)),
                   pl.BlockSpec((8, CONV_WIDTH), lambda i: (0, 0)),
                   pl.BlockSpec((8, CONV_WIDTH), lambda i: (0, 0))],
        out_shape=[jax.ShapeDtypeStruct((s, PROJ_WIDTH), BF16), jax.ShapeDtypeStruct((8, CONV_WIDTH), F32),
                   jax.ShapeDtypeStruct((8, CONV_WIDTH), F32)],
        args=[pc, pc, pc, dyc, dyc, wc, g3, dq, dk, dv])


def _attn_bwd(qkvp, biasm, o, lse, do, l, s, pad, comm=None):
    rows = Q_STEP * Q_BLOCK
    nb = s // rows
    qb0 = pad // rows
    scale = HEAD_DIM ** -0.5
    wide = 128 * ATTN_PAIRS

    def body(q_ref, k_ref, v_ref, b_ref, o_ref, lse_ref, do_ref,
             dq_ref, dk_ref, dv_ref, ds_ref, dk_acc, dv_acc):
        blk = pl.program_id(1)

        @pl.when(blk == 0)
        def _():
            dk_acc[...] = jnp.zeros(dk_acc.shape, F32)
            dv_acc[...] = jnp.zeros(dv_acc.shape, F32)
            ds_ref[...] = jnp.zeros(ds_ref.shape, F32)

        lane = lax.broadcasted_iota(jnp.int32, (1, 128), 1)

        def step(masked):
            for sub in range(Q_STEP):
                rs = slice(Q_BLOCK * sub, Q_BLOCK * (sub + 1))
                start = blk * rows + Q_BLOCK * sub
                koff = pl.multiple_of(start + (pad - LEFT), Q_BLOCK)
                kpos = lax.broadcasted_iota(jnp.int32, (1, K_BAND), 1) + (start - LEFT)
                kmask = jnp.where(kpos >= 0, 0.0, NEG_INF)
                for pr in range(ATTN_PAIRS):
                    ls = slice(128 * pr, 128 * (pr + 1))
                    q = q_ref[rs, ls]
                    kb = k_ref[pl.ds(koff, K_BAND), ls]
                    vb = v_ref[pl.ds(koff, K_BAND), ls]
                    dov = do_ref[rs, ls]
                    lse_v = lse_ref[rs, ls]
                    prod = dov * o_ref[rs, ls]
                    dq_parts = []
                    dk_new = jnp.zeros((128, K_BAND), F32)
                    dv_new = jnp.zeros((128, K_BAND), F32)
                    for hh in range(2):
                        in_head = (lane >> 6) == hh
                        qm = jnp.where(in_head, q, jnp.zeros_like(q)) * jnp.asarray(scale, BF16)
                        dom = jnp.where(in_head, dov, 0.0).astype(BF16)
                        delta = jnp.sum(jnp.where(in_head, prod, 0.0), axis=1, keepdims=True)
                        lse_h = lse_v[:, HEAD_DIM * hh:HEAD_DIM * hh + 1]
                        sc = _dot_nt(qm, kb) + b_ref[2 * pr + hh]
                        if masked:
                            sc = sc + kmask
                        p = jnp.exp(sc - lse_h)
                        dp = _dot_nt(dom, vb)
                        ds = p * (dp - delta)
                        ds_ref[2 * pr + hh] += ds
                        dsb = ds.astype(BF16)
                        dq_parts.append(_dot(dsb, kb) * scale)
                        dk_new = dk_new + _dot_tn(qm, dsb)
                        dv_new = dv_new + _dot_tn(dom, p.astype(BF16))
                    dq_ref[rs, ls] = jnp.where(lane < HEAD_DIM, dq_parts[0], dq_parts[1]).astype(BF16)
                    dk_acc[pr, :, pl.ds(koff, K_BAND)] += dk_new
                    dv_acc[pr, :, pl.ds(koff, K_BAND)] += dv_new

        pl.when(blk == 0)(lambda: step(True))
        pl.when(blk > 0)(lambda: step(False))

        @pl.when(blk == nb - 1)
        def _():
            for pr in range(ATTN_PAIRS):
                ls = slice(128 * pr, 128 * (pr + 1))
                dk_ref[:, ls] = dk_acc[pr, :, pad:pad + s].T.astype(BF16)
                dv_ref[:, ls] = dv_acc[pr, :, pad:pad + s].T.astype(BF16)

    n_grp = ATTN_WIDTH // wide
    qblk = pl.BlockSpec((rows, wide), lambda p, b: (b, p))
    col = pl.BlockSpec((s, wide), lambda p, b: (0, p))
    shp = jax.ShapeDtypeStruct((s, ATTN_WIDTH), BF16)
    return _call(
        body, name="attn_bwd", grid=(n_grp, nb),
        in_specs=[pl.BlockSpec((rows, wide), lambda p, b: (qb0 + b, p)),
                  pl.BlockSpec((s + pad, wide), lambda p, b: (0, n_grp + p)),
                  pl.BlockSpec((s + pad, wide), lambda p, b: (0, 2 * n_grp + p)),
                  pl.BlockSpec((None, 2 * ATTN_PAIRS, Q_BLOCK, K_BAND), lambda p, b: (l, p, 0, 0)),
                  qblk, qblk, qblk],
        out_specs=[qblk, col, col, pl.BlockSpec((2 * ATTN_PAIRS, Q_BLOCK, K_BAND), lambda p, b: (p, 0, 0))],
        out_shape=[shp, shp, shp, jax.ShapeDtypeStruct((N_HEADS, Q_BLOCK, K_BAND), F32)],
        scratch=[pltpu.VMEM((ATTN_PAIRS, 128, s + pad), F32), pltpu.VMEM((ATTN_PAIRS, 128, s + pad), F32)],
        args=[qkvp, qkvp, qkvp, biasm, o, lse, do], comm=comm)


def _dw_in(h, dproj, s):
    def body(a_ref, b_ref, o_ref):
        acc = _dot_tn(a_ref[...], b_ref[...])
        o_ref[0] = acc[:, 0:PROJ_SHARD].astype(BF16)
        o_ref[1] = acc[:, PROJ_SHARD:2 * PROJ_SHARD].astype(BF16)

    out, = _call(
        body, name="dw_in", grid=(4,),
        in_specs=[pl.BlockSpec((s, D_MODEL), lambda n: (0, 0)),
                  pl.BlockSpec((s, 2 * PROJ_SHARD), lambda n: (0, n))],
        out_specs=[pl.BlockSpec((2, D_MODEL, PROJ_SHARD), lambda n: (n, 0, 0))],
        out_shape=[jax.ShapeDtypeStruct((N_DEV, D_MODEL, PROJ_SHARD), BF16)], args=[h, dproj])
    return out


def _in_proj_bwd(dproj, win, x, g3, dres, l, s, tm, f_prev=None, g_post3=None, comm=None):
    chain = f_prev is not None

    def body(d_ref, wt_ref, wb_ref, x_ref, g_ref, dres_ref, *rest):
        if chain:
            f_ref, gq_ref, dx_ref, dg_ref, df_ref, dgq_ref = rest
            _zero_first((dg_ref, dgq_ref), pl.program_id(0) == 0)
        else:
            dx_ref, dg_ref = rest
            _zero_first((dg_ref,), pl.program_id(0) == 0)
        w = _join_w_in(wt_ref, wb_ref)
        for rs in _row_subtiles(tm, SUB_ROWS):
            dh = _dot_nt(d_ref[rs, :], w)
            dx, dyn = _norm_bwd_rows(x_ref[rs, :], g_ref[...], dh)
            dx = dres_ref[rs, :] + dx
            dx_ref[rs, :] = dx
            _add_cols(dg_ref, dyn)
            if chain:
                df, dyn2 = _norm_bwd_rows(f_ref[rs, :].astype(F32), gq_ref[...], dx)
                df_ref[rs, :] = df.astype(BF16)
                _add_cols(dgq_ref, dyn2)

    row = pl.BlockSpec((tm, D_MODEL), lambda i: (i, 0))
    dgs = pl.BlockSpec((8, D_MODEL), lambda i: (0, 0))
    half = pl.BlockSpec((N_DEV, D_MODEL // 2, PROJ_SHARD), lambda i: (0, 0, 0), pipeline_mode=pl.Buffered(1))
    in_specs = [pl.BlockSpec((tm, PROJ_WIDTH), lambda i: (i, 0)), half, half,
                row, pl.BlockSpec((None, 1, D_MODEL), lambda i: (l, 0, 0)), row]
    out_specs = [row, dgs]
    out_shape = [jax.ShapeDtypeStruct((s, D_MODEL), F32), jax.ShapeDtypeStruct((8, D_MODEL), F32)]
    args = [dproj, *win, x, g3, dres]
    if chain:
        in_specs += [row, pl.BlockSpec((None, 1, D_MODEL), lambda i: (l - 1, 0, 0))]
        out_specs += [row, dgs]
        out_shape += [jax.ShapeDtypeStruct((s, D_MODEL), BF16), jax.ShapeDtypeStruct((8, D_MODEL), F32)]
        args += [f_prev, g_post3]
    return _call(body, name="in_proj_bwd", grid=(s // tm,), in_specs=in_specs, out_specs=out_specs,
                 out_shape=out_shape, args=args, comm=comm)


def _adamw(name, w, m, v, lands, owns=None, me=None):
    groups, rows, cols = w.shape
    assert len(lands) == groups
    n_part = lands[0].shape[0]
    tr = _row_tile(rows, tuple(c for c in (512, 352, 256, 176, 128, 64, 32, 16, 8) if c * cols <= 256 * 1024))
    c1 = 1.0 - ADAM_B1 ** ADAM_STEP
    c2 = 1.0 - ADAM_B2 ** ADAM_STEP
    n_own = groups if owns is not None else 0

    def body(*refs):
        if n_own:
            me_ref, refs = refs[0], refs[1:]
        w_ref, m_ref, v_ref = refs[:3]
        land_refs = refs[3:3 + groups]
        own_refs = refs[3 + groups:3 + groups + n_own]
        g_ref, d_ref, nm_ref, nv_ref = refs[3 + groups + n_own:]
        grp = pl.program_id(0)
        for gi in range(groups):
            @pl.when(grp == gi)
            def _():
                l_ref = land_refs[gi]
                g = None
                for p in range(n_part):
                    part = l_ref[p].astype(F32)
                    if n_own:
                        part = jnp.where(me_ref[0] == p, own_refs[gi][...].astype(F32), part)
                    g = part if g is None else g + part
                g_ref[...] = g
                m1 = ADAM_B1 * m_ref[...] + (1.0 - ADAM_B1) * g
                v1 = ADAM_B2 * v_ref[...] + (1.0 - ADAM_B2) * (g * g)
                nm_ref[...] = m1
                nv_ref[...] = v1
                d_ref[...] = -ADAM_LR * ((m1 / c1) / (jnp.sqrt(v1 / c2) + ADAM_EPS) + ADAM_WD * w_ref[...])

    blk = pl.BlockSpec((None, tr, cols), lambda g, i, *_: (g, i, 0))
    shp = jax.ShapeDtypeStruct((groups, rows, cols), F32)

    def land_spec(gi):
        return pl.BlockSpec((n_part, tr, cols), lambda g, i, *_: (0, jnp.where(g == gi, i, 0), 0))

    def own_spec(gi):
        if owns[gi].ndim == 3:
            return pl.BlockSpec((None, tr, cols), lambda g, i, me_ref: (me_ref[0], jnp.where(g == gi, i, 0), 0))
        return pl.BlockSpec((tr, cols), lambda g, i, me_ref: (jnp.where(g == gi, i, 0), 0))

    in_specs = [blk, blk, blk] + [land_spec(gi) for gi in range(groups)] + [own_spec(gi) for gi in range(n_own)]
    args = [w, m, v] + list(lands) + (list(owns) if n_own else [])
    if not n_own:
        return _call(body, name=name, grid=(groups, rows // tr), in_specs=in_specs,
                     out_specs=[blk, blk, blk, blk], out_shape=[shp, shp, shp, shp], args=args)
    return pl.pallas_call(
        body, name=name,
        grid_spec=pltpu.PrefetchScalarGridSpec(
            num_scalar_prefetch=1, grid=(groups, rows // tr), in_specs=in_specs, out_specs=[blk, blk, blk, blk]),
        out_shape=[shp, shp, shp, shp],
        compiler_params=pltpu.CompilerParams(dimension_semantics=("arbitrary", "arbitrary"),
                                             vmem_limit_bytes=VMEM_LIMIT),
    )(me, *args)


def _pack_small(rel, gco, gao, gpm, gqm, gpf, gqf):
    n_layers = rel.shape[0]
    relp = jnp.pad(rel, ((0, 0), (0, 0), (0, REL_PAD - rel.shape[2])))
    parts = [relp.reshape(n_layers * N_HEADS * REL_PAD // 128, 128)]
    parts += [a.reshape(-1, 128) for a in (gco, gao, gpm, gqm, gpf, gqf)]
    return jnp.concatenate(parts, axis=0)


def _pack_small_grads(d_rel, parts):
    n_layers = len(d_rel)
    keys = ("gco", "gao", "gpm", "gqm", "gpf", "gqf")
    arrays = list(d_rel) + [parts[k][l] for k in keys for l in range(n_layers)] + list(parts["wc"])
    rows = 0
    plan = []
    for l in range(n_layers):
        for h in range(N_HEADS):
            for t in range(REL_PAD // 128):
                plan.append((l, (0, h), t, rows))
                rows += 1
    for ki, k in enumerate(keys):
        for l in range(n_layers):
            for t in range(parts[k][l].shape[1] // 128):
                plan.append((n_layers * (1 + ki) + l, (0,), t, rows))
                rows += 1
    for l in range(n_layers):
        for tap in range(3):
            for t in range(CONV_WIDTH // 128):
                plan.append((n_layers * (1 + len(keys)) + l, (tap,), t, rows))
                rows += 1
    total = rows + (-rows) % 8

    def body(*refs):
        o_ref = refs[-1]
        if total > rows:
            o_ref[rows:total, :] = jnp.zeros((total - rows, 128), F32)
        for op, idx, t, dst in plan:
            lanes = slice(128 * t, 128 * (t + 1))
            if len(idx) == 2:
                o_ref[dst:dst + 1, :] = refs[op][idx[0], idx[1]:idx[1] + 1, lanes]
            else:
                o_ref[dst:dst + 1, :] = refs[op][idx[0]:idx[0] + 1, lanes]

    vmem = pl.BlockSpec(memory_space=pltpu.VMEM)
    out, = _call(body, name="pack_small_grads", grid=(), in_specs=[vmem] * len(arrays), out_specs=[vmem],
                 out_shape=[jax.ShapeDtypeStruct((total, 128), F32)], args=arrays)
    return out


def _unpack_small(p, n_layers):
    n_rel = n_layers * N_HEADS * REL_PAD // 128
    rel = p[:n_rel].reshape(n_layers, N_HEADS, REL_PAD)[:, :, :2 * REL_CLIP + 1]
    outs = [rel]
    r0 = n_rel
    for width in (CONV_WIDTH, ATTN_WIDTH, D_MODEL, D_MODEL, D_MODEL, D_MODEL):
        nr = n_layers * width // 128
        outs.append(p[r0:r0 + nr].reshape(n_layers, width))
        r0 += nr
    return outs


def kernel(x, w_in, w_conv, rel_bias, g_conv_out, g_attn_out, w_out, g_pre_mix, g_post_mix, g_pre_ffn, g_post_ffn, w_ffn_in, w_ffn_out, loss_target, m_w_in, m_w_conv, m_rel_bias, m_g_conv_out, m_g_attn_out, m_w_out, m_g_pre_mix, m_g_post_mix, m_g_pre_ffn, m_g_post_ffn, m_w_ffn_in, m_w_ffn_out, v_w_in, v_w_conv, v_rel_bias, v_g_conv_out, v_g_attn_out, v_w_out, v_g_pre_mix, v_g_post_mix, v_g_pre_ffn, v_g_post_ffn, v_w_ffn_in, v_w_ffn_out):
    n_layers = w_in.shape[0]
    s = x.shape[1]
    assert x.shape == (1, s, D_MODEL) and s % 1024 == 0
    assert w_in.shape == (n_layers, D_MODEL, PROJ_SHARD) and w_ffn_in.shape == (n_layers, D_MODEL, FF_SHARD)
    tm = 512
    tq = 1024 if s >= 2048 else 512
    tf = min(1024, s)
    x0 = x.reshape(s, D_MODEL)
    target = loss_target.reshape(s, D_MODEL)
    dev = _dev_index(lax.axis_index("x"), lax.axis_index("y"), lax.axis_index("c"))

    wt_ffn_in, mt_ffn_in, vt_ffn_in = (jnp.transpose(a, (0, 2, 1)) for a in (w_ffn_in, m_w_ffn_in, v_w_ffn_in))
    local_w = [_cast_bf16(w_in, "cast_w_in"), _cast_bf16(w_out, "cast_w_out"),
               _cast_bf16(wt_ffn_in, "cast_w_ffn_in"), _cast_bf16(w_ffn_out, "cast_w_ffn_out")]
    wc_local = jnp.pad(jnp.transpose(w_conv, (0, 2, 1)).reshape(-1), (0, 1024 - n_layers * 3 * 64)).reshape(8, 128)
    win_halves = local_w[0].reshape(2 * n_layers, D_MODEL // 2, PROJ_SHARD)
    biasm, win_top, win_bottom, wc_g = _bias_build(
        jnp.pad(rel_bias, ((0, 0), (0, 0), (0, REL_PAD - rel_bias.shape[2]))),
        comm=_Gather([(win_halves, 0), (win_halves, 1), (wc_local, None)]))
    weights = [None] * n_layers
    wc_full = wc_g.reshape(N_DEV, 1024)[:, :n_layers * 3 * 64].reshape(N_DEV, n_layers, 3, 64)
    wc_full = jnp.transpose(wc_full, (1, 2, 0, 3)).reshape(n_layers, 3, CONV_WIDTH)
    wc_full = jnp.pad(wc_full, ((0, 0), (0, 5), (0, 0)))

    g3 = {k: v.reshape(n_layers, 1, -1) for k, v in dict(
        conv=g_conv_out, attn=g_attn_out, pre_mix=g_pre_mix, post_mix=g_post_mix,
        pre_ffn=g_pre_ffn, post_ffn=g_post_ffn).items()}

    saved = []
    xl = x0
    h = _norm_cast(x0, g3["pre_mix"], 0, tm)
    for l in range(n_layers):
        win = (win_top, win_bottom)
        more = l + 1 < n_layers
        pc, ync, wout = _in_proj_conv(h, win, wc_full, g3["conv"], l, s, tq, comm=_Gather([(local_w[1], l)]))
        qkvp = _in_proj_qkv(h, win, s, tq)
        o, lse, yna, wfin = _attn_fwd(qkvp, biasm, g3["attn"], l, s, tq, comm=_Gather([(local_w[2], l)]))
        wout = wout.reshape(D_MODEL, D_MODEL)
        z, xm, h2 = _out_proj_fwd(ync, yna, wout, xl, g3["post_mix"], g3["pre_ffn"], l, s, tq)
        gu, act, wfout, *rest = _ffn_in_fwd(
            h2, wfin.reshape(4, FF_PAIR, D_MODEL), s, tf,
            comm=_Gather([(local_w[3], l)] + ([(win_halves, 2 * l + 2)] if more else [])))
        wfo = wfout.reshape(D_FF, D_MODEL)
        weights[l] = [win, wout, wfin.reshape(2 * D_FF, D_MODEL), wfo]
        sv = dict(x=xl, h=h, pc=pc, qkvp=qkvp, ync=ync, yna=yna, o=o, lse=lse, z=z, xm=xm, h2=h2, gu=gu, act=act)
        if more:
            win_top, = rest
            sv["f"], xl, h, win_bottom = _ffn_out_fwd(act, wfo, xm, g3["post_ffn"], g3["pre_mix"], l, l + 1, s, tm,
                                                      comm=_Gather([(win_halves, 2 * l + 3)]))
        else:
            dx, sq, df, dg_post_ffn = _ffn_out_loss(act, wfo, xm, g3["post_ffn"], target, l, s, tm)
        saved.append(sv)

    loss_part = jnp.sum(sq) * (0.5 / D_MODEL)

    lands = dict(win=[None] * n_layers, wout=[None] * n_layers, wfin=[None] * n_layers, wfout=[None] * n_layers)
    small = {k: [None] * n_layers for k in ("gco", "gao", "gpm", "gqm", "gpf", "gqf", "wc")}
    d_rel = [None] * n_layers
    started = []

    def start(name, keys, l, arrays):
        items = [(a, False) for a in arrays]
        send_sems, recv_sems, srcs, zones, token = _exchange_start(name + "_start", items)
        started.append((name, keys, l, items, send_sems, recv_sems, srcs, zones))
        return token[0:1, 0:1].reshape(1, 1, 1)

    for l in reversed(range(n_layers)):
        sv = saved[l]
        win, wout, wfin, wfo = weights[l]
        small["gqf"][l] = dg_post_ffn
        dgu, d_wfout = _ffn_out_bwd(df, wfo, sv["gu"], sv["act"], s, tf)
        d_wfout = d_wfout.reshape(N_DEV, FFO_SHARD, D_MODEL)
        d_wfin = _dw_ffn_in(sv["h2"], dgu, s).reshape(N_DEV, FF_SHARD, D_MODEL)
        g_pre_ffn = g3["pre_ffn"]
        if l == 0:
            g_pre_ffn = g_pre_ffn + start("exchange_ffn0", ("wfout", "wfin"), l, [d_wfout, d_wfin])
        dxm, dz, dg_pre_ffn, dg_post_mix = _ffn_in_bwd(
            dgu, wfin, sv["xm"], g_pre_ffn, dx, sv["z"], g3["post_mix"], l, s, tm)
        small["gpf"][l] = dg_pre_ffn
        small["gqm"][l] = dg_post_mix
        dyc, do, dg_attn, d_wout = _out_proj_bwd(dz, wout, sv["o"], g3["attn"], sv["ync"], sv["yna"], l, s, tq)
        d_wout = d_wout.reshape(N_DEV, D_MODEL // N_DEV, D_MODEL)
        small["gao"][l] = dg_attn
        dq, dk, dv, ds_sum = _attn_bwd(sv["qkvp"], biasm, sv["o"], sv["lse"], do, l, s, tq)
        d_rel[l] = _bias_bwd(ds_sum[None])
        dproj, dwc, dg_conv = _conv_bwd(sv["pc"], dyc, wc_full, g3["conv"], dq, dk, dv, l, s, tf)
        small["wc"][l] = dwc
        small["gco"][l] = dg_conv
        d_win = _dw_in(sv["h"], dproj, s)
        if l == 0:
            token = start("exchange_mix0", ("wout", "win"), l, [d_wout, d_win])
        else:
            token = start(f"exchange_layer{l}", ("wfout", "wfin", "wout", "win"), l, [d_wfout, d_wfin, d_wout, d_win])
        if l > 0:
            dx, dg_pre_mix, df, dg_post_ffn = _in_proj_bwd(
                dproj, win, sv["x"], g3["pre_mix"] + token, dxm, l, s, tm, f_prev=saved[l - 1]["f"],
                g_post3=g3["post_ffn"])
        else:
            dx, dg_pre_mix = _in_proj_bwd(dproj, win, sv["x"], g3["pre_mix"] + token, dxm, l, s, tm)
        small["gpm"][l] = dg_pre_mix
    grad_x = dx.reshape(1, s, D_MODEL)

    small_vec = _pack_small_grads(d_rel, small)
    loss_row = small_vec.shape[0]
    small_vec = jnp.concatenate([small_vec, jnp.full((8, 128), loss_part, F32)], axis=0)
    small_items = [(small_vec, True)]
    small_sems = _exchange_start("exchange_small_start", small_items)

    owns = dict(win=[None] * n_layers, wout=[None] * n_layers, wfin=[None] * n_layers, wfout=[None] * n_layers)

    def wait(last, after):
        for name, keys, l, items, send_sems, recv_sems, srcs, zones in started:
            if (name == "exchange_mix0") == last:
                srcs, zones = _exchange_wait(name + "_wait", items, send_sems, recv_sems, srcs, zones, after)
                for key, src, zone in zip(keys, srcs, zones):
                    owns[key][l], lands[key][l] = src, zone

    me = dev.astype(jnp.int32).reshape(1)
    wait(False, small_sems[4])
    r_fin = [jnp.transpose(t, (0, 2, 1)) for t in _adamw(
        "adamw_w_ffn_in", wt_ffn_in, mt_ffn_in, vt_ffn_in, lands["wfin"], owns["wfin"], me)]
    r_fout = _adamw("adamw_w_ffn_out", w_ffn_out, m_w_ffn_out, v_w_ffn_out, lands["wfout"], owns["wfout"], me)
    wait(True, r_fout[0])
    r_out = _adamw("adamw_w_out", w_out, m_w_out, v_w_out, lands["wout"], owns["wout"], me)
    r_in = _adamw("adamw_w_in", w_in, m_w_in, v_w_in, lands["win"], owns["win"], me)
    (small_own,), (land_small,) = _exchange_wait(
        "exchange_small_wait", small_items, small_sems[0], small_sems[1], small_sems[2], small_sems[3], r_in[0])

    shares = jnp.where(jnp.arange(N_DEV) == dev, small_own[loss_row, 0], land_small[:, loss_row, 0])
    loss = jnp.sum(shares)

    n_rep = 64 * n_layers
    rep = _adamw(
        "adamw_replicated",
        _pack_small(rel_bias, g_conv_out, g_attn_out, g_pre_mix, g_post_mix, g_pre_ffn, g_post_ffn)[None],
        _pack_small(m_rel_bias, m_g_conv_out, m_g_attn_out, m_g_pre_mix, m_g_post_mix, m_g_pre_ffn, m_g_post_ffn)[None],
        _pack_small(v_rel_bias, v_g_conv_out, v_g_attn_out, v_g_pre_mix, v_g_post_mix, v_g_pre_ffn, v_g_post_ffn)[None],
        [land_small[:, :n_rep]], [small_own[:n_rep]], me)
    rep = [_unpack_small(t[0], n_layers) for t in rep]

    wc_rows = n_layers * 3 * CONV_WIDTH // 128
    zeros_wc = jnp.zeros((1, wc_rows, 128), F32)
    g_wc_full = _adamw("sum_w_conv", zeros_wc, zeros_wc, zeros_wc, [land_small[:, n_rep:n_rep + wc_rows]],
                       [small_own[n_rep:n_rep + wc_rows]], me)[0]
    g_wc_full = g_wc_full.reshape(n_layers, 3, CONV_WIDTH)
    g_wc = lax.dynamic_slice_in_dim(g_wc_full, dev * (CONV_WIDTH // N_DEV), CONV_WIDTH // N_DEV, axis=2)
    g_wc = jnp.transpose(g_wc, (0, 2, 1))

    def tiny(a):
        flat = a.reshape(-1)
        return jnp.pad(flat, (0, (-flat.shape[0]) % 1024)).reshape(1, -1, 128)

    r_wc = _adamw("adamw_w_conv", tiny(w_conv), tiny(m_w_conv), tiny(v_w_conv), [tiny(g_wc)])
    r_wc = [t.reshape(-1)[:w_conv.size].reshape(w_conv.shape) for t in r_wc]

    def leaf(kind):
        return [r_in[kind], r_wc[kind], rep[kind][0], rep[kind][1], rep[kind][2], r_out[kind],
                rep[kind][3], rep[kind][4], rep[kind][5], rep[kind][6], r_fin[kind], r_fout[kind]]

    return (loss, grad_x, *leaf(0), *leaf(1), *leaf(2), *leaf(3))
```
